```python
import jax, jax.numpy as jnp
from jax import lax
import numpy as np

D_MODEL = 1024
BATCH = 8
SEQ = 8192
DEPTH = 4

RET_HEADS = 4
RET_QK_DIM = 64
RET_V_DIM = 128
RET_CHUNK = 128
MLA_HEADS = 4
MLA_Q_RANK = 384
MLA_KV_RANK = 256
MLA_NOPE_DIM = 128
MLA_ROPE_DIM = 64
MLA_V_DIM = 128
ATTN_BLOCK = 128
ROPE_DIM = 64
ROPE_BASE = 10000.0
CONV_WIDTH = D_MODEL
CONV_KERNEL = 31
EPS = 1e-6

RET_WIDTH = RET_HEADS * RET_V_DIM
MLA_WIDTH = MLA_HEADS * MLA_V_DIM
MIX_WIDTH = RET_WIDTH + MLA_WIDTH
EV_SPLITS = (RET_HEADS * RET_QK_DIM, RET_HEADS * RET_QK_DIM, RET_WIDTH, RET_WIDTH,
             MLA_Q_RANK, MLA_KV_RANK, MLA_ROPE_DIM, MLA_WIDTH)
EV_IN_WIDTH = 2 * RET_HEADS * RET_QK_DIM + 2 * RET_WIDTH + MLA_Q_RANK + MLA_KV_RANK + MLA_ROPE_DIM + MLA_WIDTH
OD_IN_WIDTH = 3 * CONV_WIDTH
N_EVEN = (DEPTH + 1) // 2
N_ODD = DEPTH // 2

kernel_name = 'hybrid_retention_mla_conformer_encoder'


def _split(z, sizes):
    outs = []
    start = 0
    for size in sizes:
        outs.append(z[..., start:start + size])
        start += size
    return outs


def _rmsnorm(x, g):
    xf = x.astype(jnp.float32)
    y = xf * lax.rsqrt(jnp.mean(xf * xf, axis=-1, keepdims=True) + EPS)
    return (y * g.astype(jnp.float32)).astype(x.dtype)


def _layernorm(x, g, b):
    xf = x.astype(jnp.float32)
    mu = jnp.mean(xf, axis=-1, keepdims=True)
    var = jnp.mean(jnp.square(xf - mu), axis=-1, keepdims=True)
    y = (xf - mu) * lax.rsqrt(var + EPS)
    return (y * g.astype(jnp.float32) + b.astype(jnp.float32)).astype(x.dtype)


def _rope_tables(positions):
    inv_freq = ROPE_BASE ** (-jnp.arange(0, ROPE_DIM, 2, dtype=jnp.float32) / ROPE_DIM)
    ang = positions.astype(jnp.float32)[..., None] * inv_freq
    return jnp.cos(ang)[:, :, None, :], jnp.sin(ang)[:, :, None, :]


def _rope(x, cos, sin):
    xf = x.astype(jnp.float32)
    x1, x2 = xf[..., :ROPE_DIM // 2], xf[..., ROPE_DIM // 2:]
    return jnp.concatenate([x1 * cos - x2 * sin, x2 * cos + x1 * sin], axis=-1).astype(x.dtype)


def _retention_dir(q, k, v, log_gamma, include_diag):
    B, S, H, dk = q.shape
    dv = v.shape[-1]
    C = RET_CHUNK
    N = S // C
    qc = q.reshape(B, N, C, H, dk)
    kc = k.reshape(B, N, C, H, dk)
    vc = v.reshape(B, N, C, H, dv)
    idx = jnp.arange(C, dtype=jnp.float32)
    diff = idx[:, None] - idx[None, :]
    mask = (diff >= 0) if include_diag else (diff > 0)
    decay = jnp.where(mask[None], jnp.exp(log_gamma[:, None, None] * jnp.maximum(diff, 0.0)[None]), 0.0)
    scores = jnp.einsum('bnihd,bnjhd->bnhij', qc, kc) * decay
    inner = jnp.einsum('bnhij,bnjhe->bnihe', scores, vc)
    k_w = jnp.exp(log_gamma[None, :] * (C - 1 - idx)[:, None])
    kv = jnp.einsum('bnjhd,jh,bnjhe->nbhde', kc, k_w, vc)
    chunk_decay = jnp.exp(log_gamma * C)[None, :, None, None]

    def step(state, kv_n):
        return state * chunk_decay + kv_n, state

    _, states = lax.scan(step, jnp.zeros((B, H, dk, dv), jnp.float32), kv)
    q_w = jnp.exp(log_gamma[None, :] * (idx + 1.0)[:, None])
    cross = jnp.einsum('bnihd,ih,nbhde->bnihe', qc, q_w, states)
    return (inner + cross).reshape(B, S, H, dv)


def _dense_attention_blocks(q, k, v):
    B, S, H, dq = q.shape
    dv = v.shape[-1]
    nb = S // ATTN_BLOCK
    qb = q.reshape(B, nb, ATTN_BLOCK, H, dq).transpose(1, 0, 2, 3, 4)
    scale = dq ** -0.5

    def one(qblk):
        s = jnp.einsum('bqhd,bkhd->bhqk', qblk, k, preferred_element_type=jnp.float32) * scale
        p = jax.nn.softmax(s, axis=-1).astype(v.dtype)
        return jnp.einsum('bhqk,bkhd->bqhd', p, v)

    o = lax.map(one, qb)
    return o.transpose(1, 0, 2, 3, 4).reshape(B, S, H, dv)


def _even_mixer(h, cos, sin, w_in, dec_f, dec_b, q_norm_g, w_uq, kv_norm_g, w_ukv, w_out):
    B, S, _ = h.shape
    z = h @ w_in
    rq, rk, rv, rg, cq, ckv, kr, mg = _split(z, EV_SPLITS)
    rq = _rope(rq.reshape(B, S, RET_HEADS, RET_QK_DIM), cos, sin).astype(jnp.float32)
    rk = _rope(rk.reshape(B, S, RET_HEADS, RET_QK_DIM), cos, sin).astype(jnp.float32) * (RET_QK_DIM ** -0.5)
    rv = rv.reshape(B, S, RET_HEADS, RET_V_DIM).astype(jnp.float32)
    lg_f = jax.nn.log_sigmoid(dec_f.astype(jnp.float32))
    lg_b = jax.nn.log_sigmoid(dec_b.astype(jnp.float32))
    o_f = _retention_dir(rq, rk, rv, lg_f, True)
    o_b = jnp.flip(_retention_dir(jnp.flip(rq, 1), jnp.flip(rk, 1), jnp.flip(rv, 1), lg_b, False), 1)
    o = o_f + o_b
    mu = jnp.mean(o, axis=-1, keepdims=True)
    var = jnp.mean(jnp.square(o - mu), axis=-1, keepdims=True)
    o = (o - mu) * lax.rsqrt(var + EPS)
    ret_out = o.reshape(B, S, RET_WIDTH).astype(h.dtype) * jax.nn.silu(rg)
    q = (_rmsnorm(cq, q_norm_g) @ w_uq).reshape(B, S, MLA_HEADS, MLA_NOPE_DIM + MLA_ROPE_DIM)
    q = jnp.concatenate([q[..., :MLA_NOPE_DIM], _rope(q[..., MLA_NOPE_DIM:], cos, sin)], axis=-1)
    kv = (_rmsnorm(ckv, kv_norm_g) @ w_ukv).reshape(B, S, MLA_HEADS, MLA_NOPE_DIM + MLA_V_DIM)
    k_nope, v = kv[..., :MLA_NOPE_DIM], kv[..., MLA_NOPE_DIM:]
    k_rope = _rope(kr[:, :, None, :], cos, sin)
    k = jnp.concatenate([k_nope, jnp.broadcast_to(k_rope, (B, S, MLA_HEADS, MLA_ROPE_DIM))], axis=-1)
    a = _dense_attention_blocks(q, k, v)
    mla_out = a.reshape(B, S, MLA_WIDTH) * jax.nn.silu(mg)
    return jnp.concatenate([ret_out, mla_out], axis=-1) @ w_out


def _odd_mixer(h, w_in, b_in, dw_w, dw_b, ln_g, ln_b, w_out):
    z = h @ w_in + b_in
    a, b, g = _split(z, (CONV_WIDTH, CONV_WIDTH, CONV_WIDTH))
    u = a * jax.nn.sigmoid(b)
    u = lax.conv_general_dilated(
        u, dw_w[:, None, :].astype(u.dtype), window_strides=(1,),
        padding=((CONV_KERNEL // 2, CONV_KERNEL // 2),),
        dimension_numbers=('NWC', 'WIO', 'NWC'),
        feature_group_count=CONV_WIDTH) + dw_b
    u = jax.nn.silu(_layernorm(u, ln_g, ln_b))
    return (u * jax.nn.silu(g)) @ w_out


def _fwd_setup_inputs(seed: int = 0) -> dict:
    key = jax.random.key(seed)
    ks = jax.random.split(key, 24)
    f32 = jnp.float32

    def nrm(k, shape, fan_in):
        return jax.random.normal(k, shape, f32) * (fan_in ** -0.5)

    def small(k, shape):
        return 0.02 * jax.random.normal(k, shape, f32)

    def gain(k, shape):
        return 1.0 + 0.02 * jax.random.normal(k, shape, f32)

    hidx = jnp.arange(RET_HEADS, dtype=f32)
    g0 = 1.0 - 2.0 ** (-5.0 - hidx)
    base_logit = jnp.log(g0) - jnp.log1p(-g0)
    return {
        'x': jax.random.normal(ks[0], (BATCH, SEQ, D_MODEL), f32),
        'c': jax.random.normal(ks[1], (BATCH, D_MODEL), f32),
        'positions': jnp.arange(SEQ, dtype=jnp.int32)[None, :]
                     + jax.random.randint(ks[2], (BATCH, 1), 0, SEQ, dtype=jnp.int32),
        'ada_w': 0.5 * nrm(ks[3], (DEPTH, D_MODEL, 3 * D_MODEL), D_MODEL),
        'ada_b': small(ks[4], (DEPTH, 3 * D_MODEL)),
        'pre_g': gain(ks[5], (DEPTH, D_MODEL)),
        'post_g': gain(ks[6], (DEPTH, D_MODEL)),
        'ev_w_in': nrm(ks[7], (N_EVEN, D_MODEL, EV_IN_WIDTH), D_MODEL),
        'ev_dec_f': base_logit[None, :] + 0.1 * jax.random.normal(ks[8], (N_EVEN, RET_HEADS), f32),
        'ev_dec_b': base_logit[None, :] + 0.1 * jax.random.normal(ks[9], (N_EVEN, RET_HEADS), f32),
        'ev_q_norm_g': gain(ks[10], (N_EVEN, MLA_Q_RANK)),
        'ev_w_uq': nrm(ks[11], (N_EVEN, MLA_Q_RANK, MLA_HEADS * (MLA_NOPE_DIM + MLA_ROPE_DIM)), MLA_Q_RANK),
        'ev_kv_norm_g': gain(ks[12], (N_EVEN, MLA_KV_RANK)),
        'ev_w_ukv': nrm(ks[13], (N_EVEN, MLA_KV_RANK, MLA_HEADS * (MLA_NOPE_DIM + MLA_V_DIM)), MLA_KV_RANK),
        'ev_w_out': nrm(ks[14], (N_EVEN, MIX_WIDTH, D_MODEL), MIX_WIDTH),
        'od_w_in': nrm(ks[15], (N_ODD, D_MODEL, OD_IN_WIDTH), D_MODEL),
        'od_b_in': small(ks[16], (N_ODD, OD_IN_WIDTH)),
        'od_dw_w': nrm(ks[17], (N_ODD, CONV_KERNEL, CONV_WIDTH), CONV_KERNEL),
        'od_dw_b': small(ks[18], (N_ODD, CONV_WIDTH)),
        'od_ln_g': gain(ks[19], (N_ODD, CONV_WIDTH)),
        'od_ln_b': small(ks[20], (N_ODD, CONV_WIDTH)),
        'od_w_out': nrm(ks[21], (N_ODD, CONV_WIDTH, D_MODEL), CONV_WIDTH),
    }


def _fwd_reference(x, c, positions, ada_w, ada_b, pre_g, post_g,
              ev_w_in, ev_dec_f, ev_dec_b, ev_q_norm_g, ev_w_uq, ev_kv_norm_g, ev_w_ukv, ev_w_out,
              od_w_in, od_b_in, od_dw_w, od_dw_b, od_ln_g, od_ln_b, od_w_out):
    cos, sin = _rope_tables(positions)
    c_act = jax.nn.silu(c)
    for layer in range(DEPTH):
        mod = c_act @ ada_w[layer] + ada_b[layer]
        shift, scale, gate = _split(mod, (D_MODEL, D_MODEL, D_MODEL))
        h = _rmsnorm(x, pre_g[layer]) * (1.0 + scale[:, None, :]) + shift[:, None, :]
        if layer % 2 == 0:
            i = layer // 2
            y = _even_mixer(h, cos, sin, ev_w_in[i], ev_dec_f[i], ev_dec_b[i], ev_q_norm_g[i],
                            ev_w_uq[i], ev_kv_norm_g[i], ev_w_ukv[i], ev_w_out[i])
        else:
            i = layer // 2
            y = _odd_mixer(h, od_w_in[i], od_b_in[i], od_dw_w[i], od_dw_b[i],
                           od_ln_g[i], od_ln_b[i], od_w_out[i])
        x = x + gate[:, None, :] * _rmsnorm(y, post_g[layer])
    return x


import jax as _jax
import jax.numpy as _jnp

TWIN_FORMAT = 'train_step'
FWD_PARAMS = ['x', 'c', 'positions', 'ada_w', 'ada_b', 'pre_g', 'post_g', 'ev_w_in', 'ev_dec_f', 'ev_dec_b', 'ev_q_norm_g', 'ev_w_uq', 'ev_kv_norm_g', 'ev_w_ukv', 'ev_w_out', 'od_w_in', 'od_b_in', 'od_dw_w', 'od_dw_b', 'od_ln_g', 'od_ln_b', 'od_w_out']
TWIN_WEIGHTS = ['ada_w', 'ada_b', 'pre_g', 'post_g', 'ev_w_in', 'ev_dec_f', 'ev_dec_b', 'ev_q_norm_g', 'ev_w_uq', 'ev_kv_norm_g', 'ev_w_ukv', 'ev_w_out', 'od_w_in', 'od_b_in', 'od_dw_w', 'od_dw_b', 'od_ln_g', 'od_ln_b', 'od_w_out']
TWIN_DIFF_INPUT = 'x'
TWIN_INPUTS = ['x', 'c', 'positions', 'ada_w', 'ada_b', 'pre_g', 'post_g', 'ev_w_in', 'ev_dec_f', 'ev_dec_b', 'ev_q_norm_g', 'ev_w_uq', 'ev_kv_norm_g', 'ev_w_ukv', 'ev_w_out', 'od_w_in', 'od_b_in', 'od_dw_w', 'od_dw_b', 'od_ln_g', 'od_ln_b', 'od_w_out', 'loss_target', 'm_ada_w', 'm_ada_b', 'm_pre_g', 'm_post_g', 'm_ev_w_in', 'm_ev_dec_f', 'm_ev_dec_b', 'm_ev_q_norm_g', 'm_ev_w_uq', 'm_ev_kv_norm_g', 'm_ev_w_ukv', 'm_ev_w_out', 'm_od_w_in', 'm_od_b_in', 'm_od_dw_w', 'm_od_dw_b', 'm_od_ln_g', 'm_od_ln_b', 'm_od_w_out', 'v_ada_w', 'v_ada_b', 'v_pre_g', 'v_post_g', 'v_ev_w_in', 'v_ev_dec_f', 'v_ev_dec_b', 'v_ev_q_norm_g', 'v_ev_w_uq', 'v_ev_kv_norm_g', 'v_ev_w_ukv', 'v_ev_w_out', 'v_od_w_in', 'v_od_b_in', 'v_od_dw_w', 'v_od_dw_b', 'v_od_ln_g', 'v_od_ln_b', 'v_od_w_out']
TWIN_OUTPUTS = ['loss', 'grad_x', 'grad_ada_w', 'grad_ada_b', 'grad_pre_g', 'grad_post_g', 'grad_ev_w_in', 'grad_ev_dec_f', 'grad_ev_dec_b', 'grad_ev_q_norm_g', 'grad_ev_w_uq', 'grad_ev_kv_norm_g', 'grad_ev_w_ukv', 'grad_ev_w_out', 'grad_od_w_in', 'grad_od_b_in', 'grad_od_dw_w', 'grad_od_dw_b', 'grad_od_ln_g', 'grad_od_ln_b', 'grad_od_w_out', 'delta_ada_w', 'delta_ada_b', 'delta_pre_g', 'delta_post_g', 'delta_ev_w_in', 'delta_ev_dec_f', 'delta_ev_dec_b', 'delta_ev_q_norm_g', 'delta_ev_w_uq', 'delta_ev_kv_norm_g', 'delta_ev_w_ukv', 'delta_ev_w_out', 'delta_od_w_in', 'delta_od_b_in', 'delta_od_dw_w', 'delta_od_dw_b', 'delta_od_ln_g', 'delta_od_ln_b', 'delta_od_w_out', 'new_m_ada_w', 'new_m_ada_b', 'new_m_pre_g', 'new_m_post_g', 'new_m_ev_w_in', 'new_m_ev_dec_f', 'new_m_ev_dec_b', 'new_m_ev_q_norm_g', 'new_m_ev_w_uq', 'new_m_ev_kv_norm_g', 'new_m_ev_w_ukv', 'new_m_ev_w_out', 'new_m_od_w_in', 'new_m_od_b_in', 'new_m_od_dw_w', 'new_m_od_dw_b', 'new_m_od_ln_g', 'new_m_od_ln_b', 'new_m_od_w_out', 'new_v_ada_w', 'new_v_ada_b', 'new_v_pre_g', 'new_v_post_g', 'new_v_ev_w_in', 'new_v_ev_dec_f', 'new_v_ev_dec_b', 'new_v_ev_q_norm_g', 'new_v_ev_w_uq', 'new_v_ev_kv_norm_g', 'new_v_ev_w_ukv', 'new_v_ev_w_out', 'new_v_od_w_in', 'new_v_od_b_in', 'new_v_od_dw_w', 'new_v_od_dw_b', 'new_v_od_ln_g', 'new_v_od_ln_b', 'new_v_od_w_out']
TWIN_LEAF_KINDS = {'loss': 'loss', 'grad_x': 'grad_x', 'grad_ada_w': 'grad_w', 'grad_ada_b': 'grad_w', 'grad_pre_g': 'grad_w', 'grad_post_g': 'grad_w', 'grad_ev_w_in': 'grad_w', 'grad_ev_dec_f': 'grad_w', 'grad_ev_dec_b': 'grad_w', 'grad_ev_q_norm_g': 'grad_w', 'grad_ev_w_uq': 'grad_w', 'grad_ev_kv_norm_g': 'grad_w', 'grad_ev_w_ukv': 'grad_w', 'grad_ev_w_out': 'grad_w', 'grad_od_w_in': 'grad_w', 'grad_od_b_in': 'grad_w', 'grad_od_dw_w': 'grad_w', 'grad_od_dw_b': 'grad_w', 'grad_od_ln_g': 'grad_w', 'grad_od_ln_b': 'grad_w', 'grad_od_w_out': 'grad_w', 'delta_ada_w': 'delta_w', 'delta_ada_b': 'delta_w', 'delta_pre_g': 'delta_w', 'delta_post_g': 'delta_w', 'delta_ev_w_in': 'delta_w', 'delta_ev_dec_f': 'delta_w', 'delta_ev_dec_b': 'delta_w', 'delta_ev_q_norm_g': 'delta_w', 'delta_ev_w_uq': 'delta_w', 'delta_ev_kv_norm_g': 'delta_w', 'delta_ev_w_ukv': 'delta_w', 'delta_ev_w_out': 'delta_w', 'delta_od_w_in': 'delta_w', 'delta_od_b_in': 'delta_w', 'delta_od_dw_w': 'delta_w', 'delta_od_dw_b': 'delta_w', 'delta_od_ln_g': 'delta_w', 'delta_od_ln_b': 'delta_w', 'delta_od_w_out': 'delta_w', 'new_m_ada_w': 'new_m', 'new_m_ada_b': 'new_m', 'new_m_pre_g': 'new_m', 'new_m_post_g': 'new_m', 'new_m_ev_w_in': 'new_m', 'new_m_ev_dec_f': 'new_m', 'new_m_ev_dec_b': 'new_m', 'new_m_ev_q_norm_g': 'new_m', 'new_m_ev_w_uq': 'new_m', 'new_m_ev_kv_norm_g': 'new_m', 'new_m_ev_w_ukv': 'new_m', 'new_m_ev_w_out': 'new_m', 'new_m_od_w_in': 'new_m', 'new_m_od_b_in': 'new_m', 'new_m_od_dw_w': 'new_m', 'new_m_od_dw_b': 'new_m', 'new_m_od_ln_g': 'new_m', 'new_m_od_ln_b': 'new_m', 'new_m_od_w_out': 'new_m', 'new_v_ada_w': 'new_v', 'new_v_ada_b': 'new_v', 'new_v_pre_g': 'new_v', 'new_v_post_g': 'new_v', 'new_v_ev_w_in': 'new_v', 'new_v_ev_dec_f': 'new_v', 'new_v_ev_dec_b': 'new_v', 'new_v_ev_q_norm_g': 'new_v', 'new_v_ev_w_uq': 'new_v', 'new_v_ev_kv_norm_g': 'new_v', 'new_v_ev_w_ukv': 'new_v', 'new_v_ev_w_out': 'new_v', 'new_v_od_w_in': 'new_v', 'new_v_od_b_in': 'new_v', 'new_v_od_dw_w': 'new_v', 'new_v_od_dw_b': 'new_v', 'new_v_od_ln_g': 'new_v', 'new_v_od_ln_b': 'new_v', 'new_v_od_w_out': 'new_v'}


def _forward(args):
    return _fwd_reference(*[args[k] for k in FWD_PARAMS])


def _output_shape():
    def fwd():
        inp = _fwd_setup_inputs(0)
        return _fwd_reference(*[inp[k] for k in FWD_PARAMS])
    out = _jax.eval_shape(fwd)
    return out.shape, out.dtype

N_MICROBATCH = 1
ADAM_LR = 0.001
ADAM_B1 = 0.9
ADAM_B2 = 0.999
ADAM_EPS = 1e-08
ADAM_WD = 0.01
ADAM_STEP = 10
PER_EXAMPLE_BATCH_AXIS = {'x': 0, 'c': 0, 'positions': 0, 'loss_target': 0}
SHARED_INPUTS = []
_WEIGHT_DTYPES = {'ada_w': _jnp.float32, 'ada_b': _jnp.float32, 'pre_g': _jnp.float32, 'post_g': _jnp.float32, 'ev_w_in': _jnp.float32, 'ev_dec_f': _jnp.float32, 'ev_dec_b': _jnp.float32, 'ev_q_norm_g': _jnp.float32, 'ev_w_uq': _jnp.float32, 'ev_kv_norm_g': _jnp.float32, 'ev_w_ukv': _jnp.float32, 'ev_w_out': _jnp.float32, 'od_w_in': _jnp.float32, 'od_b_in': _jnp.float32, 'od_dw_w': _jnp.float32, 'od_dw_b': _jnp.float32, 'od_ln_g': _jnp.float32, 'od_ln_b': _jnp.float32, 'od_w_out': _jnp.float32}
MOMENT_SCALE = {'ada_w': 2.375661e+00, 'ada_b': 5.227501e+00, 'pre_g': 2.352835e-01, 'post_g': 6.657351e+00, 'ev_w_in': 2.011466e-01, 'ev_dec_f': 1.310453e+00, 'ev_dec_b': 2.164842e+00, 'ev_q_norm_g': 2.568983e-02, 'ev_w_uq': 1.782689e-02, 'ev_kv_norm_g': 1.386133e-01, 'ev_w_ukv': 7.463231e-02, 'ev_w_out': 1.882776e-01, 'od_w_in': 1.285757e-01, 'od_b_in': 1.841668e-01, 'od_dw_w': 1.519205e-01, 'od_dw_b': 4.216328e-01, 'od_ln_g': 2.399015e-01, 'od_ln_b': 2.555697e-01, 'od_w_out': 1.807887e-01}


def _to_microbatches(a, axis):
    t = _jnp.moveaxis(a, axis, 0)
    t = t.reshape((N_MICROBATCH, t.shape[0] // N_MICROBATCH) + t.shape[1:])
    return _jnp.moveaxis(t, 1, axis + 1)


def setup_inputs(seed: int = 0) -> dict:
    inp = _fwd_setup_inputs(seed)
    key = _jax.random.fold_in(_jax.random.key(seed), 7919)
    shape, _ = _output_shape()
    out = dict(inp)
    out["loss_target"] = _jax.random.normal(_jax.random.fold_in(key, 0), shape, _jnp.float32)
    for i, name in enumerate(TWIN_WEIGHTS):
        w = inp[name].astype(_jnp.float32)
        if MOMENT_SCALE is None:
            s = _jnp.sqrt(_jnp.mean(_jnp.square(w)) + 1e-30)
        else:
            s = MOMENT_SCALE[name]
        km, kv = _jax.random.split(_jax.random.fold_in(key, i + 1))
        out[name] = w
        out["m_" + name] = s * _jax.random.normal(km, w.shape, _jnp.float32)
        out["v_" + name] = (s * s) * _jax.random.uniform(kv, w.shape, _jnp.float32, 0.5, 1.5)
    if N_MICROBATCH > 1:
        for name, axis in PER_EXAMPLE_BATCH_AXIS.items():
            out[name] = _to_microbatches(out[name], axis)
    return {'x': out['x'], 'c': out['c'], 'positions': out['positions'], 'ada_w': out['ada_w'], 'ada_b': out['ada_b'], 'pre_g': out['pre_g'], 'post_g': out['post_g'], 'ev_w_in': out['ev_w_in'], 'ev_dec_f': out['ev_dec_f'], 'ev_dec_b': out['ev_dec_b'], 'ev_q_norm_g': out['ev_q_norm_g'], 'ev_w_uq': out['ev_w_uq'], 'ev_kv_norm_g': out['ev_kv_norm_g'], 'ev_w_ukv': out['ev_w_ukv'], 'ev_w_out': out['ev_w_out'], 'od_w_in': out['od_w_in'], 'od_b_in': out['od_b_in'], 'od_dw_w': out['od_dw_w'], 'od_dw_b': out['od_dw_b'], 'od_ln_g': out['od_ln_g'], 'od_ln_b': out['od_ln_b'], 'od_w_out': out['od_w_out'], 'loss_target': out['loss_target'], 'm_ada_w': out['m_ada_w'], 'm_ada_b': out['m_ada_b'], 'm_pre_g': out['m_pre_g'], 'm_post_g': out['m_post_g'], 'm_ev_w_in': out['m_ev_w_in'], 'm_ev_dec_f': out['m_ev_dec_f'], 'm_ev_dec_b': out['m_ev_dec_b'], 'm_ev_q_norm_g': out['m_ev_q_norm_g'], 'm_ev_w_uq': out['m_ev_w_uq'], 'm_ev_kv_norm_g': out['m_ev_kv_norm_g'], 'm_ev_w_ukv': out['m_ev_w_ukv'], 'm_ev_w_out': out['m_ev_w_out'], 'm_od_w_in': out['m_od_w_in'], 'm_od_b_in': out['m_od_b_in'], 'm_od_dw_w': out['m_od_dw_w'], 'm_od_dw_b': out['m_od_dw_b'], 'm_od_ln_g': out['m_od_ln_g'], 'm_od_ln_b': out['m_od_ln_b'], 'm_od_w_out': out['m_od_w_out'], 'v_ada_w': out['v_ada_w'], 'v_ada_b': out['v_ada_b'], 'v_pre_g': out['v_pre_g'], 'v_post_g': out['v_post_g'], 'v_ev_w_in': out['v_ev_w_in'], 'v_ev_dec_f': out['v_ev_dec_f'], 'v_ev_dec_b': out['v_ev_dec_b'], 'v_ev_q_norm_g': out['v_ev_q_norm_g'], 'v_ev_w_uq': out['v_ev_w_uq'], 'v_ev_kv_norm_g': out['v_ev_kv_norm_g'], 'v_ev_w_ukv': out['v_ev_w_ukv'], 'v_ev_w_out': out['v_ev_w_out'], 'v_od_w_in': out['v_od_w_in'], 'v_od_b_in': out['v_od_b_in'], 'v_od_dw_w': out['v_od_dw_w'], 'v_od_dw_b': out['v_od_dw_b'], 'v_od_ln_g': out['v_od_ln_g'], 'v_od_ln_b': out['v_od_ln_b'], 'v_od_w_out': out['v_od_w_out']}


def _loss(weights, diff, rest, loss_target):
    with _jax.named_scope("forward"):
        args = {**rest, TWIN_DIFF_INPUT: diff, **{k: w.astype(_WEIGHT_DTYPES[k]) for k, w in weights.items()}}
        y = _forward(args)
    with _jax.named_scope("loss_head"):
        err = _jnp.square(y.astype(_jnp.float32) - loss_target)
        return 0.5 * _jnp.sum(_jnp.mean(err, axis=-1)) if err.ndim else 0.5 * err


def _adamw(w, g, m, v):
    m = ADAM_B1 * m + (1.0 - ADAM_B1) * g
    v = ADAM_B2 * v + (1.0 - ADAM_B2) * _jnp.square(g)
    m_hat = m / (1.0 - ADAM_B1 ** ADAM_STEP)
    v_hat = v / (1.0 - ADAM_B2 ** ADAM_STEP)
    delta = -ADAM_LR * (m_hat / (_jnp.sqrt(v_hat) + ADAM_EPS) + ADAM_WD * w)
    return delta, m, v


def reference(x, c, positions, ada_w, ada_b, pre_g, post_g, ev_w_in, ev_dec_f, ev_dec_b, ev_q_norm_g, ev_w_uq, ev_kv_norm_g, ev_w_ukv, ev_w_out, od_w_in, od_b_in, od_dw_w, od_dw_b, od_ln_g, od_ln_b, od_w_out, loss_target, m_ada_w, m_ada_b, m_pre_g, m_post_g, m_ev_w_in, m_ev_dec_f, m_ev_dec_b, m_ev_q_norm_g, m_ev_w_uq, m_ev_kv_norm_g, m_ev_w_ukv, m_ev_w_out, m_od_w_in, m_od_b_in, m_od_dw_w, m_od_dw_b, m_od_ln_g, m_od_ln_b, m_od_w_out, v_ada_w, v_ada_b, v_pre_g, v_post_g, v_ev_w_in, v_ev_dec_f, v_ev_dec_b, v_ev_q_norm_g, v_ev_w_uq, v_ev_kv_norm_g, v_ev_w_ukv, v_ev_w_out, v_od_w_in, v_od_b_in, v_od_dw_w, v_od_dw_b, v_od_ln_g, v_od_ln_b, v_od_w_out):
    given = dict(x=x, c=c, positions=positions, ada_w=ada_w, ada_b=ada_b, pre_g=pre_g, post_g=post_g, ev_w_in=ev_w_in, ev_dec_f=ev_dec_f, ev_dec_b=ev_dec_b, ev_q_norm_g=ev_q_norm_g, ev_w_uq=ev_w_uq, ev_kv_norm_g=ev_kv_norm_g, ev_w_ukv=ev_w_ukv, ev_w_out=ev_w_out, od_w_in=od_w_in, od_b_in=od_b_in, od_dw_w=od_dw_w, od_dw_b=od_dw_b, od_ln_g=od_ln_g, od_ln_b=od_ln_b, od_w_out=od_w_out, loss_target=loss_target, m_ada_w=m_ada_w, m_ada_b=m_ada_b, m_pre_g=m_pre_g, m_post_g=m_post_g, m_ev_w_in=m_ev_w_in, m_ev_dec_f=m_ev_dec_f, m_ev_dec_b=m_ev_dec_b, m_ev_q_norm_g=m_ev_q_norm_g, m_ev_w_uq=m_ev_w_uq, m_ev_kv_norm_g=m_ev_kv_norm_g, m_ev_w_ukv=m_ev_w_ukv, m_ev_w_out=m_ev_w_out, m_od_w_in=m_od_w_in, m_od_b_in=m_od_b_in, m_od_dw_w=m_od_dw_w, m_od_dw_b=m_od_dw_b, m_od_ln_g=m_od_ln_g, m_od_ln_b=m_od_ln_b, m_od_w_out=m_od_w_out, v_ada_w=v_ada_w, v_ada_b=v_ada_b, v_pre_g=v_pre_g, v_post_g=v_post_g, v_ev_w_in=v_ev_w_in, v_ev_dec_f=v_ev_dec_f, v_ev_dec_b=v_ev_dec_b, v_ev_q_norm_g=v_ev_q_norm_g, v_ev_w_uq=v_ev_w_uq, v_ev_kv_norm_g=v_ev_kv_norm_g, v_ev_w_ukv=v_ev_w_ukv, v_ev_w_out=v_ev_w_out, v_od_w_in=v_od_w_in, v_od_b_in=v_od_b_in, v_od_dw_w=v_od_dw_w, v_od_dw_b=v_od_dw_b, v_od_ln_g=v_od_ln_g, v_od_ln_b=v_od_ln_b, v_od_w_out=v_od_w_out)
    weights = {n: given[n] for n in TWIN_WEIGHTS}
    shared = {n: given[n] for n in SHARED_INPUTS}
    per_example = {n: given[n] for n in ['x', 'c', 'positions']}
    grad_fn = _jax.value_and_grad(_loss, argnums=(0, 1))

    def one_microbatch(ex, loss_target):
        ex = dict(ex)
        diff = ex.pop(TWIN_DIFF_INPUT)
        return grad_fn(weights, diff, {**shared, **ex}, loss_target)

    if N_MICROBATCH == 1:
        loss, (grad_w, grad_x) = one_microbatch(per_example, given["loss_target"])
    else:
        def body(carry, xs):
            loss_sum, grad_sum = carry
            l_k, (gw_k, gx_k) = one_microbatch(xs[0], xs[1])
            with _jax.named_scope("update"):
                return (loss_sum + l_k, _jax.tree.map(_jnp.add, grad_sum, gw_k)), gx_k

        init = (_jnp.zeros((), _jnp.float32), _jax.tree.map(_jnp.zeros_like, weights))
        (loss, grad_w), grad_x = _jax.lax.scan(body, init, (per_example, given["loss_target"]))
    with _jax.named_scope("update"):
        delta_w, new_m, new_v = {}, {}, {}
        for n in TWIN_WEIGHTS:
            delta_w[n], new_m[n], new_v[n] = _adamw(weights[n], grad_w[n], given["m_" + n], given["v_" + n])
    return (loss, grad_x, *[grad_w[n] for n in TWIN_WEIGHTS], *[delta_w[n] for n in TWIN_WEIGHTS],
            *[new_m[n] for n in TWIN_WEIGHTS], *[new_v[n] for n in TWIN_WEIGHTS])
```

```python
import functools

import jax
import jax.numpy as jnp
from jax import lax
from jax.experimental import pallas as pl
from jax.experimental.pallas import tpu as pltpu

F32 = jnp.float32
BF16 = jnp.bfloat16

D_MODEL = 1024
DEPTH = 4
RET_HEADS = 4
RET_QK_DIM = 64
RET_V_DIM = 128
MLA_HEADS = 4
MLA_Q_RANK = 384
MLA_KV_RANK = 256
MLA_NOPE_DIM = 128
MLA_ROPE_DIM = 64
MLA_V_DIM = 128
ROPE_DIM = 64
ROPE_BASE = 10000.0
CONV_KERNEL = 31
EPS = 1e-6
ADAM_LR, ADAM_B1, ADAM_B2, ADAM_EPS, ADAM_WD, ADAM_STEP = 0.001, 0.9, 0.999, 1e-08, 0.01, 10

LANES = 128
HP = 128
Z_RQ, Z_RK, Z_RV, Z_RG, Z_CQ, Z_CKV, Z_KR, Z_MG, Z_END = 0, 512, 1024, 1536, 2048, 2432, 2688, 2816, 3328
V7X_VMEM_BYTES = 64 * 1024 * 1024
VMEM_CAP = V7X_VMEM_BYTES - 8 * 1024 * 1024

MESH = pl.DeviceIdType.MESH


def _nbytes(shape, dtype):
    n = 1
    for s in shape:
        n *= s
    return n * jnp.dtype(dtype).itemsize


def _params(sem, block_bytes):
    limit = min(VMEM_CAP, max(32 * 1024 * 1024, 2 * block_bytes + 16 * 1024 * 1024))
    return pltpu.CompilerParams(dimension_semantics=sem, vmem_limit_bytes=limit)


def _rowcall(name, fn, rows, params, outs, accs=(), ts=256):
    S = rows[0].shape[0]
    ts = min(ts, S)
    assert S % ts == 0, (name, S, ts)
    nr, npar, no, na = len(rows), len(params), len(outs), len(accs)

    def body(*refs):
        row_refs = refs[:nr]
        pvals = [r[...] for r in refs[nr:nr + npar]]
        out_refs = refs[nr + npar:nr + npar + no]
        acc_refs = refs[nr + npar + no:]
        ovals, avals = fn(row_refs, pvals)
        for r, v in zip(out_refs, ovals, strict=True):
            r[...] = v.astype(r.dtype)
        if na:
            @pl.when(pl.program_id(0) == 0)
            def _():
                for r in acc_refs:
                    r[...] = jnp.zeros_like(r)
            for r, v in zip(acc_refs, avals, strict=True):
                r[...] += v

    in_specs = [pl.BlockSpec((ts, a.shape[1]), lambda i: (i, 0)) for a in rows]
    in_specs += [pl.BlockSpec(p.shape, lambda i: (0, 0)) for p in params]
    out_specs = [pl.BlockSpec((ts, w), lambda i: (i, 0)) for w, _ in outs]
    out_specs += [pl.BlockSpec(s, lambda i: (0, 0)) for s in accs]
    out_shape = [jax.ShapeDtypeStruct((S, w), dt) for w, dt in outs]
    out_shape += [jax.ShapeDtypeStruct(s, F32) for s in accs]
    nb = sum(_nbytes((ts, a.shape[1]), a.dtype) for a in rows) + sum(_nbytes((ts, w), dt) for w, dt in outs)
    nb += sum(_nbytes(p.shape, p.dtype) for p in params) + sum(_nbytes(s, F32) for s in accs)
    res = pl.pallas_call(
        body, name=name, grid=(S // ts,), in_specs=in_specs, out_specs=out_specs, out_shape=out_shape,
        compiler_params=_params(("arbitrary",) if na else ("parallel",), 3 * nb),
    )(*rows, *params)
    return res


def _silu(x):
    return x * jax.nn.sigmoid(x)


def _rms(x, g):
    return x * lax.rsqrt(jnp.mean(x * x, axis=-1, keepdims=True) + EPS) * g


def _rot(x):
    lane = lax.broadcasted_iota(jnp.int32, x.shape, 1)
    return jnp.where(lane < ROPE_DIM // 2, -pltpu.roll(x, LANES - ROPE_DIM // 2, 1), pltpu.roll(x, ROPE_DIM // 2, 1))


def _rope(x, cos, sin):
    return x * cos + _rot(x) * sin


def _rope_t(dy, cos, sin):
    return dy * cos - _rot(dy) * sin


def _groups(ref, start, n):
    return [ref[:, start + HP * h:start + HP * (h + 1)] for h in range(n)]


def _pre_math(x, g, m_scale, b_scale, m_shift, b_shift):
    return _rms(x, g) * (1.0 + (m_scale + b_scale)) + (m_shift + b_shift)


def _pre_fwd(x, g, mod, bias):
    D = D_MODEL

    def fn(rows, p):
        g_, mod_, b_ = p
        h = _pre_math(rows[0][...], g_, mod_[:, D:2 * D], b_[:, D:2 * D], mod_[:, :D], b_[:, :D])
        return [h], []

    return _rowcall("pre_fwd", fn, [x], [g, mod, bias], [(D, BF16)])[0]


def _pre_bwd(x, dh, dx_res, g, mod, bias):
    D = D_MODEL

    def fn(rows, p):
        g_, mod_, b_ = p
        xv = rows[0][...]
        _, vjp = jax.vjp(_pre_math, xv, g_, mod_[:, D:2 * D], b_[:, D:2 * D], mod_[:, :D], b_[:, :D])
        dx, dg, dsc, _, dsh, _ = vjp(rows[1][...].astype(F32))
        return [dx + rows[2][...]], [dg, dsc, dsh]

    return _rowcall("pre_bwd", fn, [x, dh, dx_res], [g, mod, bias], [(D, F32)], [(1, D)] * 3)


def _post_math(x, y, g, m_gate, b_gate):
    return x + (m_gate + b_gate) * _rms(y, g)


def _post_fwd(x, y, g, mod, bias):
    D = D_MODEL

    def fn(rows, p):
        g_, mod_, b_ = p
        return [_post_math(rows[0][...], rows[1][...], g_, mod_[:, 2 * D:], b_[:, 2 * D:])], []

    return _rowcall("post_fwd", fn, [x, y], [g, mod, bias], [(D, F32)])[0]


def _post_bwd(y, dxo, g, mod, bias):
    D = D_MODEL

    def fn(rows, p):
        g_, mod_, b_ = p
        yv = rows[0][...]
        _, vjp = jax.vjp(lambda y_, g2, mg: _post_math(0.0, y_, g2, mg, b_[:, 2 * D:]), yv, g_, mod_[:, 2 * D:])
        dy, dg, dgate = vjp(rows[1][...])
        return [dy], [dg, dgate]

    return _rowcall("post_bwd", fn, [y, dxo], [g, mod, bias], [(D, BF16)], [(1, D)] * 2)


def _evprep_fwd(z, cos, sin, qg, kvg):
    def fn(rows, p):
        z_, c_, s_ = rows
        qg_, kvg_ = p
        cos_, sin_ = c_[...], s_[...]
        rq = jnp.concatenate([_rope(v[...], cos_, sin_) for v in _groups(z_, Z_RQ, RET_HEADS)], axis=1)
        rk = jnp.concatenate([_rope(v[...], cos_, sin_) for v in _groups(z_, Z_RK, RET_HEADS)], axis=1)
        rk = rk * (RET_QK_DIM ** -0.5)
        rv = z_[:, Z_RV:Z_RG]
        qn = _rms(z_[:, Z_CQ:Z_CKV], qg_)
        kvn = _rms(z_[:, Z_CKV:Z_KR], kvg_)
        kr = _rope(z_[:, Z_KR:Z_MG], cos_, sin_)
        return [rq, rk, rv, qn, kvn, kr], []

    return _rowcall("evprep_fwd", fn, [z, cos, sin], [qg, kvg],
                    [(512, BF16), (512, BF16), (512, BF16), (MLA_Q_RANK, BF16), (MLA_KV_RANK, BF16), (HP, F32)])


def _evprep_bwd(z, cos, sin, qg, kvg, drq, drk, drv, drg, dqn, dkvn, dkr, dmg):
    def fn(rows, p):
        z_, c_, s_, drq_, drk_, drv_, drg_, dqn_, dkvn_, dkr_, dmg_ = rows
        qg_, kvg_ = p
        cos_, sin_ = c_[...], s_[...]
        parts = []
        for h in range(RET_HEADS):
            parts.append(_rope_t(drq_[:, HP * h:HP * (h + 1)] + drq_[:, 512 + HP * h:512 + HP * (h + 1)], cos_, sin_))
        for h in range(RET_HEADS):
            t = drk_[:, HP * h:HP * (h + 1)] + drk_[:, 512 + HP * h:512 + HP * (h + 1)]
            parts.append(_rope_t(t, cos_, sin_) * (RET_QK_DIM ** -0.5))
        parts.append(drv_[:, :512] + drv_[:, 512:])
        parts.append(drg_[...])
        _, vq = jax.vjp(_rms, z_[:, Z_CQ:Z_CKV], qg_)
        dcq, dqg = vq(dqn_[...])
        _, vkv = jax.vjp(_rms, z_[:, Z_CKV:Z_KR], kvg_)
        dckv, dkvg = vkv(dkvn_[...])
        parts += [dcq, dckv, _rope_t(dkr_[...], cos_, sin_), dmg_[...]]
        return [jnp.concatenate([v.astype(BF16) for v in parts], axis=1)], [dqg, dkvg]

    rows = [z, cos, sin, drq, drk, drv, drg, dqn, dkvn, dkr, dmg]
    return _rowcall("evprep_bwd", fn, rows, [qg, kvg], [(Z_END, BF16)], [(1, MLA_Q_RANK), (1, MLA_KV_RANK)], ts=128)


ATT_SCALE = (MLA_NOPE_DIM + MLA_ROPE_DIM) ** -0.5


def _attnprep_fwd(qf, kv, kr, cos, sin):
    def fn(rows, p):
        q_, kv_, kr_, c_, s_ = rows
        cos_, sin_ = c_[...], s_[...]
        krv = kr_[...]
        qs, ks, vs = [], [], []
        for h in range(MLA_HEADS):
            b = 2 * HP * h
            qs += [q_[:, b:b + HP] * ATT_SCALE, _rope(q_[:, b + HP:b + 2 * HP], cos_, sin_) * ATT_SCALE]
            ks += [kv_[:, b:b + HP], krv]
            vs += [kv_[:, b + HP:b + 2 * HP]]
        return [jnp.concatenate(qs, axis=1), jnp.concatenate(ks, axis=1), jnp.concatenate(vs, axis=1)], []

    return _rowcall("attnprep_fwd", fn, [qf, kv, kr, cos, sin], [], [(1024, BF16), (1024, BF16), (512, BF16)])


def _attnprep_bwd(dQ, dK, dV, cos, sin):
    def fn(rows, p):
        dq_, dk_, dv_, c_, s_ = rows
        cos_, sin_ = c_[...], s_[...]
        dqs, dkvs = [], []
        dkr = None
        for h in range(MLA_HEADS):
            b = 2 * HP * h
            dqs += [dq_[:, b:b + HP] * ATT_SCALE, _rope_t(dq_[:, b + HP:b + 2 * HP], cos_, sin_) * ATT_SCALE]
            dkvs += [dk_[:, b:b + HP], dv_[:, HP * h:HP * (h + 1)]]
            t = dk_[:, b + HP:b + 2 * HP]
            dkr = t if dkr is None else dkr + t
        return [jnp.concatenate(dqs, axis=1), jnp.concatenate(dkvs, axis=1), dkr], []

    return _rowcall("attnprep_bwd", fn, [dQ, dK, dV, cos, sin], [], [(1024, BF16), (1024, BF16), (HP, F32)])


def _mix_math(o, rg, a, mg):
    outs = []
    for h in range(RET_HEADS):
        oh = o[:, HP * h:HP * (h + 1)]
        mu = jnp.mean(oh, axis=-1, keepdims=True)
        var = jnp.mean(jnp.square(oh - mu), axis=-1, keepdims=True)
        outs.append((oh - mu) * lax.rsqrt(var + EPS))
    ret = jnp.concatenate(outs, axis=1) * _silu(rg)
    return jnp.concatenate([ret, a * _silu(mg)], axis=1)


def _mix_fwd(o2, z, a):
    def fn(rows, p):
        o_, z_, a_ = rows
        return [_mix_math(o_[:, :512] + o_[:, 512:], z_[:, Z_RG:Z_CQ], a_[...], z_[:, Z_MG:Z_END])], []

    return _rowcall("mix_fwd", fn, [o2, z, a], [], [(1024, BF16)])[0]


def _mix_bwd(o2, z, a, dmix):
    def fn(rows, p):
        o_, z_, a_, dm_ = rows
        av = a_[...]
        _, vjp = jax.vjp(_mix_math, o_[:, :512] + o_[:, 512:], z_[:, Z_RG:Z_CQ], av, z_[:, Z_MG:Z_END])
        do, drg, da, dmg = vjp(dm_[...].astype(F32))
        dl = []
        for h in range(MLA_HEADS):
            sl = slice(HP * h, HP * (h + 1))
            dl.append(jnp.broadcast_to(jnp.sum(da[:, sl] * av[:, sl], axis=-1, keepdims=True), (av.shape[0], HP)))
        return [do, drg, da, dmg, jnp.concatenate(dl, axis=1)], []

    return _rowcall("mix_bwd", fn, [o2, z, a, dmix], [],
                    [(512, BF16), (512, F32), (512, BF16), (512, F32), (512, F32)])


def _glu_math(a, b, ba, bb):
    return (a + ba) * jax.nn.sigmoid(b + bb)


def _glu_fwd(z, bin_):
    D = D_MODEL

    def fn(rows, p):
        z_, = rows
        b_, = p
        return [_glu_math(z_[:, :D], z_[:, D:2 * D], b_[:, :D], b_[:, D:2 * D])], []

    return _rowcall("glu_fwd", fn, [z], [bin_], [(D, F32)])[0]


def _odmix_math(uc, g, bg, ln_g, ln_b):
    mu = jnp.mean(uc, axis=-1, keepdims=True)
    var = jnp.mean(jnp.square(uc - mu), axis=-1, keepdims=True)
    y = (uc - mu) * lax.rsqrt(var + EPS) * ln_g + ln_b
    return _silu(y) * _silu(g + bg)


def _odmix_fwd(uc, z, bin_, ln_g, ln_b):
    D = D_MODEL

    def fn(rows, p):
        uc_, z_ = rows
        b_, g_, lb_ = p
        return [_odmix_math(uc_[...], z_[:, 2 * D:], b_[:, 2 * D:], g_, lb_)], []

    return _rowcall("odmix_fwd", fn, [uc, z], [bin_, ln_g, ln_b], [(D, BF16)])[0]


def _odmix_bwd(uc, z, bin_, ln_g, ln_b, dv):
    D = D_MODEL

    def fn(rows, p):
        uc_, z_, dv_ = rows
        b_, g_, lb_ = p
        _, vjp = jax.vjp(_odmix_math, uc_[...], z_[:, 2 * D:], b_[:, 2 * D:], g_, lb_)
        duc, dg, dbg, dlg, dlb = vjp(dv_[...])
        return [duc, dg], [dbg, dlg, dlb]

    return _rowcall("odmix_bwd", fn, [uc, z, dv], [bin_, ln_g, ln_b], [(D, F32), (D, BF16)], [(1, D)] * 3)


def _glu_bwd(z, bin_, du, dg):
    D = D_MODEL

    def fn(rows, p):
        z_, du_, dg_ = rows
        b_, = p
        _, vjp = jax.vjp(_glu_math, z_[:, :D], z_[:, D:2 * D], b_[:, :D], b_[:, D:2 * D])
        da, db, dba, dbb = vjp(du_[...])
        return [jnp.concatenate([da.astype(BF16), db.astype(BF16), dg_[...]], axis=1)], [dba, dbb]

    return _rowcall("glu_bwd", fn, [z, du, dg], [bin_], [(3 * D, BF16)], [(1, D)] * 2)


def _loss_head(x, tgt):
    D = D_MODEL

    def fn(rows, p):
        err = rows[0][...] - rows[1][...]
        part = 0.5 * jnp.sum(jnp.mean(err * err, axis=-1, keepdims=True), axis=0, keepdims=True)
        return [err * (1.0 / D)], [jnp.broadcast_to(part, (1, LANES))]

    return _rowcall("loss_head", fn, [x, tgt], [], [(D, F32)], [(1, LANES)])


def _tile(n, cap):
    if n <= cap:
        return n
    best = None
    for t in range(LANES, cap + 1, LANES):
        if n % t == 0:
            best = t
    assert best is not None, (n, cap)
    return best


def _mm_nn(name, a, b, out_dtype=F32, tm=512, tn_cap=1536):
    M, K = a.shape
    N = b.shape[1]
    tm = min(tm, M)
    tn = _tile(N, tn_cap)

    def body(a_ref, b_ref, o_ref):
        o_ref[...] = jnp.dot(a_ref[...], b_ref[...], preferred_element_type=F32).astype(o_ref.dtype)

    nb = _nbytes((tm, K), a.dtype) + _nbytes((K, tn), b.dtype) + _nbytes((tm, tn), out_dtype) + _nbytes((tm, tn), F32)
    return pl.pallas_call(
        body, name=name, grid=(N // tn, M // tm),
        in_specs=[pl.BlockSpec((tm, K), lambda j, i: (i, 0)), pl.BlockSpec((K, tn), lambda j, i: (0, j))],
        out_specs=pl.BlockSpec((tm, tn), lambda j, i: (i, j)),
        out_shape=jax.ShapeDtypeStruct((M, N), out_dtype),
        compiler_params=_params(("parallel", "parallel"), nb),
    )(a, b)


def _mm_tn(name, a, b, ts=512, tm_cap=512, tn_cap=1536):
    S, M = a.shape
    N = b.shape[1]
    ts = min(ts, S)
    tm = _tile(M, tm_cap)
    tn = _tile(N, tn_cap)

    def body(a_ref, b_ref, o_ref):
        @pl.when(pl.program_id(2) == 0)
        def _():
            o_ref[...] = jnp.zeros_like(o_ref)

        o_ref[...] += lax.dot_general(a_ref[...], b_ref[...], (((0,), (0,)), ((), ())), preferred_element_type=F32)

    nb = _nbytes((ts, tm), a.dtype) + _nbytes((ts, tn), b.dtype) + 2 * _nbytes((tm, tn), F32)
    return pl.pallas_call(
        body, name=name, grid=(M // tm, N // tn, S // ts),
        in_specs=[pl.BlockSpec((ts, tm), lambda i, j, s: (s, i)), pl.BlockSpec((ts, tn), lambda i, j, s: (s, j))],
        out_specs=pl.BlockSpec((tm, tn), lambda i, j, s: (i, j)),
        out_shape=jax.ShapeDtypeStruct((M, N), F32),
        compiler_params=_params(("parallel", "parallel", "arbitrary"), nb),
    )(a, b)


ATT_TQ = 512
ATT_TK = 512
QW = 2 * HP


def _flash_fwd(Q, K, V):
    S = Q.shape[0]
    H = MLA_HEADS
    tq, tk = min(ATT_TQ, S), min(ATT_TK, S)
    nk = S // tk

    def body(q_ref, k_ref, v_ref, o_ref, lse_ref, m_s, l_s, acc_s):
        q = q_ref[...]
        m_s[...] = jnp.full_like(m_s, -jnp.inf)
        l_s[...] = jnp.zeros_like(l_s)
        acc_s[...] = jnp.zeros_like(acc_s)

        def step(j, carry):
            r0 = pl.multiple_of(j * tk, tk)
            k = k_ref[pl.ds(r0, tk), :]
            v = v_ref[pl.ds(r0, tk), :]
            s = lax.dot_general(q, k, (((1,), (1,)), ((), ())), preferred_element_type=F32)
            m_prev = m_s[...]
            m_new = jnp.maximum(m_prev, jnp.max(s, axis=1, keepdims=True))
            alpha = jnp.exp(m_prev - m_new)
            p = jnp.exp(s - m_new)
            l_s[...] = alpha * l_s[...] + jnp.sum(p, axis=1, keepdims=True)
            acc_s[...] = alpha * acc_s[...] + jnp.dot(p.astype(BF16), v, preferred_element_type=F32)
            m_s[...] = m_new
            return carry

        lax.fori_loop(0, nk, step, 0)
        l = l_s[...]
        o_ref[...] = acc_s[...] * (1.0 / l)
        lse_ref[...] = jnp.broadcast_to(m_s[...] + jnp.log(l), lse_ref.shape)

    nb = (_nbytes((tq, QW), BF16) + _nbytes((S, QW), BF16) + _nbytes((S, HP), BF16) + 2 * _nbytes((tq, HP), F32)
          + 4 * _nbytes((tq, tk), F32))
    return pl.pallas_call(
        body, name="flash_fwd", grid=(H, S // tq),
        in_specs=[pl.BlockSpec((tq, QW), lambda h, i: (i, h)), pl.BlockSpec((S, QW), lambda h, i: (0, h)),
                  pl.BlockSpec((S, HP), lambda h, i: (0, h))],
        out_specs=[pl.BlockSpec((tq, HP), lambda h, i: (i, h)), pl.BlockSpec((tq, HP), lambda h, i: (i, h))],
        out_shape=[jax.ShapeDtypeStruct((S, H * HP), F32), jax.ShapeDtypeStruct((S, H * HP), F32)],
        scratch_shapes=[pltpu.VMEM((tq, 1), F32), pltpu.VMEM((tq, 1), F32), pltpu.VMEM((tq, HP), F32)],
        compiler_params=_params(("parallel", "parallel"), nb),
    )(Q, K, V)


def _flash_bwd(Q, K, V, dO, lse, delta):
    S = Q.shape[0]
    H = MLA_HEADS
    tq, tk = min(ATT_TQ, S), min(ATT_TK, S)
    nk = S // tk

    def body(q_ref, do_ref, lse_ref, dl_ref, k_ref, v_ref, dq_ref, dk_ref, dv_ref, dq_s):
        @pl.when(pl.program_id(1) == 0)
        def _():
            dk_ref[...] = jnp.zeros_like(dk_ref)
            dv_ref[...] = jnp.zeros_like(dv_ref)

        q = q_ref[...]
        do = do_ref[...]
        lse_c = lse_ref[:, :1]
        dl_c = dl_ref[:, :1]
        dq_s[...] = jnp.zeros_like(dq_s)

        def step(j, carry):
            r0 = pl.multiple_of(j * tk, tk)
            k = k_ref[pl.ds(r0, tk), :]
            v = v_ref[pl.ds(r0, tk), :]
            s = lax.dot_general(q, k, (((1,), (1,)), ((), ())), preferred_element_type=F32)
            p = jnp.exp(s - lse_c)
            dp = lax.dot_general(do, v, (((1,), (1,)), ((), ())), preferred_element_type=F32)
            ds = (p * (dp - dl_c)).astype(BF16)
            dv_ref[pl.ds(r0, tk), :] += lax.dot_general(p.astype(BF16), do, (((0,), (0,)), ((), ())), preferred_element_type=F32)
            dk_ref[pl.ds(r0, tk), :] += lax.dot_general(ds, q, (((0,), (0,)), ((), ())), preferred_element_type=F32)
            dq_s[...] += jnp.dot(ds, k, preferred_element_type=F32)
            return carry

        lax.fori_loop(0, nk, step, 0)
        dq_ref[...] = dq_s[...]

    nb = (_nbytes((tq, QW), BF16) + _nbytes((tq, HP), BF16) + 2 * _nbytes((tq, HP), F32) + _nbytes((S, QW), BF16)
          + _nbytes((S, HP), BF16) + 2 * _nbytes((tq, QW), F32) + _nbytes((S, QW), F32) + _nbytes((S, HP), F32))
    return pl.pallas_call(
        body, name="flash_bwd", grid=(H, S // tq),
        in_specs=[pl.BlockSpec((tq, QW), lambda h, i: (i, h)), pl.BlockSpec((tq, HP), lambda h, i: (i, h)),
                  pl.BlockSpec((tq, HP), lambda h, i: (i, h)), pl.BlockSpec((tq, HP), lambda h, i: (i, h)),
                  pl.BlockSpec((S, QW), lambda h, i: (0, h)), pl.BlockSpec((S, HP), lambda h, i: (0, h))],
        out_specs=[pl.BlockSpec((tq, QW), lambda h, i: (i, h)), pl.BlockSpec((S, QW), lambda h, i: (0, h)),
                   pl.BlockSpec((S, HP), lambda h, i: (0, h))],
        out_shape=[jax.ShapeDtypeStruct((S, H * QW), F32), jax.ShapeDtypeStruct((S, H * QW), F32),
                   jax.ShapeDtypeStruct((S, H * HP), F32)],
        scratch_shapes=[pltpu.VMEM((tq, QW), F32)],
        compiler_params=_params(("parallel", "arbitrary"), nb),
    )(Q, dO, lse, delta, K, V)


RET_CHUNK = 128


def _ret_tables(d, lg_ref, h, C):
    ii = lax.broadcasted_iota(jnp.int32, (C, C), 0).astype(F32)
    jj = lax.broadcasted_iota(jnp.int32, (C, C), 1).astype(F32)
    ci = lax.broadcasted_iota(jnp.int32, (C, 1), 0).astype(F32)
    fwd = d == 0
    dist = jnp.where(fwd, ii - jj, jj - ii)
    mask = dist >= jnp.where(fwd, 0.0, 1.0)
    dist = jnp.maximum(dist, 0.0)
    qpos = jnp.where(fwd, ci + 1.0, C - ci)
    kpos = jnp.where(fwd, C - 1.0 - ci, ci)
    lg = lg_ref[d, h]
    D = jnp.where(mask, jnp.exp(lg * dist), 0.0)
    return D, jnp.exp(lg * qpos), jnp.exp(lg * kpos), jnp.exp(lg * C), dist, qpos, kpos


def _ret_fwd(lg, q, k, v):
    S = q.shape[0]
    C = min(RET_CHUNK, S)
    N = S // C
    H = RET_HEADS

    def chunk(d, n):
        return jnp.where(d == 0, n, N - 1 - n)

    def body(lg_ref, q_ref, k_ref, v_ref, o_ref, st_ref, state):
        d, n = pl.program_id(0), pl.program_id(1)

        @pl.when(n == 0)
        def _():
            state[...] = jnp.zeros_like(state)

        for h in range(H):
            sl = slice(HP * h, HP * (h + 1))
            D, qw, kw, gc, _, _, _ = _ret_tables(d, lg_ref, h, C)
            qh, kh, vh = q_ref[:, sl], k_ref[:, sl], v_ref[:, sl]
            st = state[sl, :]
            sm = lax.dot_general(qh, kh, (((1,), (1,)), ((), ())), preferred_element_type=F32) * D
            inner = jnp.dot(sm.astype(BF16), vh, preferred_element_type=F32)
            cross = qw * jnp.dot(qh, st.astype(BF16), preferred_element_type=F32)
            o_ref[:, sl] = inner + cross
            st_ref[0, 0, sl, :] = st
            kvn = lax.dot_general((kh.astype(F32) * kw).astype(BF16), vh, (((0,), (0,)), ((), ())), preferred_element_type=F32)
            state[sl, :] = gc * st + kvn

    nb = 3 * _nbytes((C, 512), BF16) + _nbytes((C, 512), F32) + 2 * _nbytes((512, HP), F32) + 8 * _nbytes((C, C), F32)
    return pl.pallas_call(
        body, name="ret_fwd", grid=(2, N),
        in_specs=[pl.BlockSpec(memory_space=pltpu.SMEM)] + [pl.BlockSpec((C, 512), lambda d, n: (chunk(d, n), 0))] * 3,
        out_specs=[pl.BlockSpec((C, 512), lambda d, n: (chunk(d, n), d)),
                   pl.BlockSpec((1, 1, 512, HP), lambda d, n: (d, chunk(d, n), 0, 0))],
        out_shape=[jax.ShapeDtypeStruct((S, 1024), F32), jax.ShapeDtypeStruct((2, N, 512, HP), F32)],
        scratch_shapes=[pltpu.VMEM((512, HP), F32)],
        compiler_params=_params(("arbitrary", "arbitrary"), nb),
    )(lg, q, k, v)


def _ret_bwd(lg, q, k, v, do, states):
    S = q.shape[0]
    C = min(RET_CHUNK, S)
    N = S // C
    H = RET_HEADS

    def chunk(d, n):
        return jnp.where(d == 0, N - 1 - n, n)

    def body(lg_ref, q_ref, k_ref, v_ref, do_ref, st_ref, dq_ref, dk_ref, dv_ref, dlg_ref, G, accA, accB, accC):
        d, n = pl.program_id(0), pl.program_id(1)

        @pl.when(n == 0)
        def _():
            G[...] = jnp.zeros_like(G)
            accA[...] = jnp.zeros_like(accA)
            accB[...] = jnp.zeros_like(accB)
            accC[...] = jnp.zeros_like(accC)

        nt = (((1,), (1,)), ((), ()))
        tn = (((0,), (0,)), ((), ()))
        for h in range(H):
            sl = slice(HP * h, HP * (h + 1))
            D, qw, kw, gc, dist, qpos, kpos = _ret_tables(d, lg_ref, h, C)
            qh, kh, vh, doh = q_ref[:, sl], k_ref[:, sl], v_ref[:, sl], do_ref[:, sl]
            st = st_ref[0, 0, sl, :]
            g = G[sl, :]
            stb, gb = st.astype(BF16), g.astype(BF16)
            sraw = lax.dot_general(qh, kh, nt, preferred_element_type=F32)
            dS = lax.dot_general(doh, vh, nt, preferred_element_type=F32) * D
            dSb = dS.astype(BF16)
            smb = (sraw * D).astype(BF16)
            qs = jnp.dot(qh, stb, preferred_element_type=F32)
            kg = jnp.dot(kh, gb, preferred_element_type=F32)
            dq_ref[:, sl] = jnp.dot(dSb, kh, preferred_element_type=F32) + qw * lax.dot_general(doh, stb, nt, preferred_element_type=F32)
            dk_ref[:, sl] = lax.dot_general(dSb, qh, tn, preferred_element_type=F32) + kw * lax.dot_general(vh, gb, nt, preferred_element_type=F32)
            dv_ref[:, sl] = lax.dot_general(smb, doh, tn, preferred_element_type=F32) + kw * kg
            dof, vf = doh.astype(F32), vh.astype(F32)
            accA[h] += sraw * dS * dist
            accB[h] += (qw * qpos) * (qs * dof) + (kw * kpos) * (kg * vf)
            accC[h] += (C * gc) * (st * g)
            G[sl, :] = gc * g + lax.dot_general((qh.astype(F32) * qw).astype(BF16), doh, tn, preferred_element_type=F32)

        @pl.when(n == N - 1)
        def _():
            rows = [jnp.broadcast_to(jnp.sum(accA[h]) + jnp.sum(accB[h]) + jnp.sum(accC[h]), (1, HP)) for h in range(H)]
            dlg_ref[0] = jnp.concatenate(rows + [jnp.zeros((8 - H, HP), F32)], axis=0)

    nb = (4 * _nbytes((C, 512), BF16) + 3 * _nbytes((C, 512), F32) + 2 * _nbytes((512, HP), F32)
          + H * (_nbytes((C, C), F32) + _nbytes((C, HP), F32) + _nbytes((HP, HP), F32)) + 12 * _nbytes((C, C), F32))
    row = lambda d, n: (chunk(d, n), 0)
    out3 = lambda d, n: (chunk(d, n), d)
    return pl.pallas_call(
        body, name="ret_bwd", grid=(2, N),
        in_specs=[pl.BlockSpec(memory_space=pltpu.SMEM)] + [pl.BlockSpec((C, 512), row)] * 4
        + [pl.BlockSpec((1, 1, 512, HP), lambda d, n: (d, chunk(d, n), 0, 0))],
        out_specs=[pl.BlockSpec((C, 512), out3)] * 3 + [pl.BlockSpec((1, 8, HP), lambda d, n: (d, 0, 0))],
        out_shape=[jax.ShapeDtypeStruct((S, 1024), F32)] * 3 + [jax.ShapeDtypeStruct((2, 8, HP), F32)],
        scratch_shapes=[pltpu.VMEM((512, HP), F32), pltpu.VMEM((H, C, C), F32), pltpu.VMEM((H, C, HP), F32),
                        pltpu.VMEM((H, HP, HP), F32)],
        compiler_params=_params(("arbitrary", "arbitrary"), nb),
    )(lg, q, k, v, do, states)


CONV_PAD = 16
CONV_TR = 256
CONV_CB = LANES


def _fill_padded(pad, u_ref, S):
    pad[0:CONV_PAD, :] = jnp.zeros((CONV_PAD, CONV_CB), F32)
    pad[CONV_PAD + S:CONV_PAD + S + CONV_PAD, :] = jnp.zeros((CONV_PAD, CONV_CB), F32)
    pad[CONV_PAD:CONV_PAD + S, :] = u_ref[...]


def _conv_fwd(u, w32, b):
    S, D = u.shape
    tr = min(CONV_TR, S)

    def body(u_ref, w_ref, b_ref, o_ref, pad):
        _fill_padded(pad, u_ref, S)
        wv = w_ref[...]
        bv = b_ref[...]

        def step(t, carry):
            r0 = pl.multiple_of(t * tr, tr)
            win = pad[pl.ds(r0, tr + 2 * CONV_PAD), :]
            acc = jnp.broadcast_to(bv, (tr, CONV_CB))
            for k in range(CONV_KERNEL):
                acc = acc + win[k + 1:k + 1 + tr, :] * wv[k:k + 1, :]
            o_ref[pl.ds(r0, tr), :] = acc
            return carry

        lax.fori_loop(0, S // tr, step, 0)

    nb = 2 * _nbytes((S, CONV_CB), F32) + _nbytes((S + 2 * CONV_PAD, CONV_CB), F32)
    return pl.pallas_call(
        body, name="conv_fwd", grid=(D // CONV_CB,),
        in_specs=[pl.BlockSpec((S, CONV_CB), lambda j: (0, j)), pl.BlockSpec((32, CONV_CB), lambda j: (0, j)),
                  pl.BlockSpec((1, CONV_CB), lambda j: (0, j))],
        out_specs=pl.BlockSpec((S, CONV_CB), lambda j: (0, j)),
        out_shape=jax.ShapeDtypeStruct((S, D), F32),
        scratch_shapes=[pltpu.VMEM((S + 2 * CONV_PAD, CONV_CB), F32)],
        compiler_params=_params(("parallel",), nb),
    )(u, w32, b)


def _conv_bwd_w(u, dout):
    S, D = u.shape
    tr = min(CONV_TR, S)

    def body(u_ref, d_ref, o_ref, pad, acc):
        _fill_padded(pad, u_ref, S)
        acc[...] = jnp.zeros_like(acc)

        def fold(a):
            return jnp.sum(a.reshape(tr // 8, 8, CONV_CB), axis=0)

        def step(t, carry):
            r0 = pl.multiple_of(t * tr, tr)
            win = pad[pl.ds(r0, tr + 2 * CONV_PAD), :]
            dv = d_ref[pl.ds(r0, tr), :]
            for k in range(CONV_KERNEL):
                acc[8 * k:8 * k + 8, :] += fold(win[k + 1:k + 1 + tr, :] * dv)
            acc[8 * CONV_KERNEL:8 * CONV_KERNEL + 8, :] += fold(dv)
            return carry

        lax.fori_loop(0, S // tr, step, 0)
        o_ref[...] = jnp.sum(acc[...].reshape(32, 8, CONV_CB), axis=1)

    nb = 2 * _nbytes((S, CONV_CB), F32) + _nbytes((S + 2 * CONV_PAD, CONV_CB), F32)
    return pl.pallas_call(
        body, name="conv_bwd_w", grid=(D // CONV_CB,),
        in_specs=[pl.BlockSpec((S, CONV_CB), lambda j: (0, j)), pl.BlockSpec((S, CONV_CB), lambda j: (0, j))],
        out_specs=pl.BlockSpec((32, CONV_CB), lambda j: (0, j)),
        out_shape=jax.ShapeDtypeStruct((32, D), F32),
        scratch_shapes=[pltpu.VMEM((S + 2 * CONV_PAD, CONV_CB), F32), pltpu.VMEM((256, CONV_CB), F32)],
        compiler_params=_params(("parallel",), nb),
    )(u, dout)


def _adamw(w, g, m, v):
    shape = w.shape
    w2, g2, m2, v2 = [a.reshape(-1, shape[-1]) for a in (w, g, m, v)]
    W = shape[-1]

    def fn(rows, p):
        w_, g_, m_, v_ = [r[...] for r in rows]
        mn = ADAM_B1 * m_ + (1.0 - ADAM_B1) * g_
        vn = ADAM_B2 * v_ + (1.0 - ADAM_B2) * jnp.square(g_)
        m_hat = mn / (1.0 - ADAM_B1 ** ADAM_STEP)
        v_hat = vn / (1.0 - ADAM_B2 ** ADAM_STEP)
        delta = -ADAM_LR * (m_hat / (jnp.sqrt(v_hat) + ADAM_EPS) + ADAM_WD * w_)
        return [delta, mn, vn], []

    R2 = w2.shape[0]
    ts = 256 if (R2 > 256 and R2 % 256 == 0) else R2
    d, mn, vn = _rowcall("adamw", fn, [w2, g2, m2, v2], [], [(W, F32)] * 3, ts=ts)
    return d.reshape(shape), mn.reshape(shape), vn.reshape(shape)


_HBM = pl.BlockSpec(memory_space=pltpu.HBM)


def _place():
    return lax.axis_index("x"), lax.axis_index("y"), lax.axis_index("c")


def _half_rows(ref, half, h):
    idx = (slice(None),) * (len(ref.shape) - 2) + (pl.ds(half * h, h), slice(None))
    return ref.at[idx]


def _chip_exchange(name, p, bcast):
    if bcast:
        h, W = p.shape[0] // 2, p.shape[1]
    else:
        h, W = p.shape[1], p.shape[2]

    def body(p_ref, o_ref, send_sems, recv_sems, local_sem):
        x, y, c = _place()
        k_me = 2 * x + y

        def src(k):
            return _half_rows(p_ref, c, h) if bcast else p_ref.at[k]

        local = pltpu.make_async_copy(src(k_me), o_ref.at[k_me], local_sem)
        local.start()
        copies = []
        for j, (cx, cy) in enumerate([(1 - x, y), (x, 1 - y), (1 - x, 1 - y)]):
            cp = pltpu.make_async_remote_copy(
                src_ref=src(2 * cx + cy), dst_ref=o_ref.at[k_me], send_sem=send_sems.at[j], recv_sem=recv_sems.at[j],
                device_id=(cx, cy, c), device_id_type=MESH)
            cp.start()
            copies.append(cp)
        for cp in copies:
            cp.wait()
        local.wait()

    return pl.pallas_call(
        body, name=name, in_specs=[_HBM], out_specs=_HBM,
        out_shape=jax.ShapeDtypeStruct((4, h, W), p.dtype),
        scratch_shapes=[pltpu.SemaphoreType.DMA((3,)), pltpu.SemaphoreType.DMA((3,)), pltpu.SemaphoreType.DMA],
    )(p)


def _sibling_send_half(name, g):
    n, R, W = g.shape
    h = R // 2

    def body(g_ref, o_ref, send_sem, recv_sem):
        x, y, c = _place()
        cp = pltpu.make_async_remote_copy(
            src_ref=_half_rows(g_ref, 1 - c, h), dst_ref=o_ref, send_sem=send_sem, recv_sem=recv_sem,
            device_id=(x, y, 1 - c), device_id_type=MESH)
        cp.start()
        cp.wait()

    return pl.pallas_call(
        body, name=name, in_specs=[_HBM], out_specs=_HBM,
        out_shape=jax.ShapeDtypeStruct((n, h, W), g.dtype),
        scratch_shapes=[pltpu.SemaphoreType.DMA, pltpu.SemaphoreType.DMA],
    )(g)


def _halves_merge(name, mine):
    lead, (h, W) = mine.shape[:-2], mine.shape[-2:]

    def body(m_ref, o_ref, send_sem, recv_sem, local_sem):
        x, y, c = _place()
        local = pltpu.make_async_copy(m_ref, _half_rows(o_ref, c, h), local_sem)
        local.start()
        cp = pltpu.make_async_remote_copy(
            src_ref=m_ref, dst_ref=_half_rows(o_ref, c, h), send_sem=send_sem, recv_sem=recv_sem,
            device_id=(x, y, 1 - c), device_id_type=MESH)
        cp.start()
        cp.wait()
        local.wait()

    return pl.pallas_call(
        body, name=name, in_specs=[_HBM], out_specs=_HBM,
        out_shape=jax.ShapeDtypeStruct((*lead, 2 * h, W), mine.dtype),
        scratch_shapes=[pltpu.SemaphoreType.DMA, pltpu.SemaphoreType.DMA, pltpu.SemaphoreType.DMA],
    )(mine)


def _add2(name, a, b, out_dtype):
    def fn(rows, p):
        return [rows[0][...].astype(F32) + rows[1][...].astype(F32)], []

    return _rowcall(name, fn, [a, b], [], [(a.shape[1], out_dtype)], ts=PACK_TILE)[0]


def _sum4(name, b):
    _, h, W = b.shape
    ts = PACK_TILE if h % PACK_TILE == 0 else h

    def body(b_ref, o_ref):
        o_ref[...] = ((b_ref[0].astype(F32) + b_ref[1].astype(F32)) + b_ref[2].astype(F32)) + b_ref[3].astype(F32)

    nb = _nbytes((4, ts, W), b.dtype) + _nbytes((ts, W), F32)
    return pl.pallas_call(
        body, name=name, grid=(h // ts,),
        in_specs=[pl.BlockSpec((4, ts, W), lambda i: (0, i, 0))], out_specs=pl.BlockSpec((ts, W), lambda i: (i, 0)),
        out_shape=jax.ShapeDtypeStruct((h, W), F32), compiler_params=_params(("parallel",), nb),
    )(b)


PACK_W = 1024
PACK_ROWS = 16
PACK_TILE = 256
SHARDED = (("ada_w", "col", False), ("ev_w_in", "col", False), ("ev_w_uq", "col", False), ("ev_w_ukv", "col", False),
           ("ev_w_out", "row", False), ("od_w_in", "col", False), ("od_b_in", "col", True), ("od_dw_w", "col", True),
           ("od_dw_b", "col", True), ("od_ln_g", "col", True), ("od_ln_b", "col", True), ("od_w_out", "row", False))
REPLICATED = ("ada_b", "pre_g", "post_g", "ev_dec_f", "ev_dec_b", "ev_q_norm_g", "ev_kv_norm_g")
WEIGHTS = ("ada_w", "ada_b", "pre_g", "post_g", "ev_w_in", "ev_dec_f", "ev_dec_b", "ev_q_norm_g", "ev_w_uq", "ev_kv_norm_g",
           "ev_w_ukv", "ev_w_out", "od_w_in", "od_b_in", "od_dw_w", "od_dw_b", "od_ln_g", "od_ln_b", "od_w_out")


def _rows_of(n):
    unit = PACK_W * PACK_ROWS
    return (n + unit - 1) // unit * PACK_ROWS


def _to_rows(flat, lead):
    n = flat.shape[-1]
    r = _rows_of(n)
    flat = jnp.pad(flat, [(0, 0)] * len(lead) + [(0, r * PACK_W - n)])
    return flat.reshape(*lead, r, PACK_W)


def _pad_total(buf, axis):
    r = buf.shape[axis]
    r2 = (r + 2 * PACK_TILE - 1) // (2 * PACK_TILE) * (2 * PACK_TILE)
    pads = [(0, 0)] * buf.ndim
    pads[axis] = (0, r2 - r)
    return jnp.pad(buf, pads)


def _split_chips(full, how):
    if how == "col":
        n = full.shape[-1] // 4
        return jnp.moveaxis(full.reshape(*full.shape[:-1], 4, n), -2, 0)
    n = full.shape[1] // 4
    return jnp.moveaxis(full.reshape(full.shape[0], 4, n, *full.shape[2:]), 1, 0)


def _join_chips(blocks, how):
    if how == "col":
        t = jnp.moveaxis(blocks, 0, -2)
        return t.reshape(*t.shape[:-2], t.shape[-2] * t.shape[-1])
    t = jnp.moveaxis(blocks, 0, 1)
    return t.reshape(t.shape[0], t.shape[1] * t.shape[2], *t.shape[3:])


def _gather_weights(shards):
    parts = []
    for name, _, exact in SHARDED:
        a = shards[name].reshape(-1)
        a = lax.bitcast_convert_type(a, BF16).reshape(-1) if exact else a.astype(BF16)
        parts.append(_to_rows(a, ()))
    packed = _pad_total(jnp.concatenate(parts, axis=0), 0)
    gathered = _halves_merge("gather_merge", _chip_exchange("gather_chips", packed, True))
    out, r0 = {}, 0
    for (name, how, exact), part in zip(SHARDED, parts, strict=True):
        r = part.shape[0]
        shp = shards[name].shape
        n = shards[name].size * (2 if exact else 1)
        a = gathered[:, r0:r0 + r].reshape(4, -1)[:, :n]
        if exact:
            a = lax.bitcast_convert_type(a.reshape(4, -1, 2), F32)
        out[name] = _join_chips(a.reshape(4, *shp), how)
        r0 += r
    return out


def _reduce_grads(grads, shard_shapes):
    parts = []
    for name, how, _ in SHARDED:
        b = _split_chips(grads[name], how)
        parts.append(_to_rows(b.reshape(4, -1).astype(BF16), (4,)))
    for name in REPLICATED:
        a = _to_rows(grads[name].reshape(-1).astype(BF16), ())
        parts.append(jnp.broadcast_to(a[None], (4, *a.shape)))
    g = _pad_total(jnp.concatenate(parts, axis=1), 1)
    R = g.shape[1]
    h = R // 2
    c = lax.axis_index("c")
    theirs = _sibling_send_half("reduce_cores", g)
    mine = lax.dynamic_slice_in_dim(g, c * h, h, axis=1)
    chip_sum = _add2("reduce_add", mine.reshape(4 * h, PACK_W), theirs.reshape(4 * h, PACK_W), BF16).reshape(4, h, PACK_W)
    total = _halves_merge("reduce_merge", _sum4("reduce_sum", _chip_exchange("reduce_chips", chip_sum, False)))
    out, r0 = {}, 0
    names = [n for n, _, _ in SHARDED] + list(REPLICATED)
    for name, part in zip(names, parts, strict=True):
        r = part.shape[1]
        shp = shard_shapes[name]
        n = 1
        for s in shp:
            n *= s
        out[name] = total[r0:r0 + r].reshape(-1)[:n].reshape(shp)
        r0 += r
    return out


def _pad_heads(a, n, w, to):
    lead = a.shape[:-1]
    return jnp.pad(a.reshape(*lead, n, w), [(0, 0)] * (len(lead) + 1) + [(0, to - w)]).reshape(*lead, n * to)


def _unpad_heads(a, n, w, to):
    lead = a.shape[:-1]
    return a.reshape(*lead, n, to)[..., :w].reshape(*lead, n * w)


def _w_in_pad(w):
    return jnp.concatenate([_pad_heads(w[:, 0:256], 4, 64, HP), _pad_heads(w[:, 256:512], 4, 64, HP), w[:, 512:2176],
                            _pad_heads(w[:, 2176:2240], 1, 64, HP), w[:, 2240:2752]], axis=1)


def _w_in_unpad(g):
    return jnp.concatenate([_unpad_heads(g[:, Z_RQ:Z_RK], 4, 64, HP), _unpad_heads(g[:, Z_RK:Z_RV], 4, 64, HP),
                            g[:, Z_RV:Z_KR], g[:, Z_KR:Z_KR + 64], g[:, Z_MG:Z_END]], axis=1)


def _prep_layer_weights(full):
    f32 = lambda a: a.astype(F32)
    ev, od = [], []
    for i in range(DEPTH // 2 + DEPTH % 2):
        w_in_p = _w_in_pad(full["ev_w_in"][i])
        w_uq_p = _pad_heads(full["ev_w_uq"][i], MLA_HEADS, MLA_NOPE_DIM + MLA_ROPE_DIM, QW)
        w_ukv, w_out = full["ev_w_ukv"][i], full["ev_w_out"][i]
        ev.append(dict(w_in=w_in_p, w_in_t=w_in_p.T, w_uq=w_uq_p, w_uq_t=w_uq_p.T, w_ukv=w_ukv, w_ukv_t=w_ukv.T,
                       w_out=w_out, w_out_t=w_out.T, qg=f32(full["ev_q_norm_g"][i])[None], kvg=f32(full["ev_kv_norm_g"][i])[None],
                       lg=jnp.stack([jax.nn.log_sigmoid(f32(full["ev_dec_f"][i])), jax.nn.log_sigmoid(f32(full["ev_dec_b"][i]))])))
    for i in range(DEPTH // 2):
        w_in, w_out = full["od_w_in"][i], full["od_w_out"][i]
        dw = jnp.pad(f32(full["od_dw_w"][i]), ((0, 1), (0, 0)))
        od.append(dict(w_in=w_in, w_in_t=w_in.T, w_out=w_out, w_out_t=w_out.T, b_in=f32(full["od_b_in"][i])[None],
                       dw=dw, dw_flip=jnp.pad(f32(full["od_dw_w"][i])[::-1], ((0, 1), (0, 0))), dw_b=f32(full["od_dw_b"][i])[None],
                       ln_g=f32(full["od_ln_g"][i])[None], ln_b=f32(full["od_ln_b"][i])[None]))
    return ev, od


def _rope_tables(positions):
    inv_freq = ROPE_BASE ** (-jnp.arange(0, ROPE_DIM, 2, dtype=F32) / ROPE_DIM)
    ang = positions.astype(F32)[:, None] * inv_freq
    z = jnp.zeros((ang.shape[0], HP - ROPE_DIM), F32)
    return jnp.concatenate([jnp.cos(ang), jnp.cos(ang), z], axis=1), jnp.concatenate([jnp.sin(ang), jnp.sin(ang), z], axis=1)


def _even_fwd(x, mod, bias, pre_g, post_g, cos, sin, w):
    h = _pre_fwd(x, pre_g, mod, bias)
    z = _mm_nn("ev_in", h, w["w_in"])
    rq, rk, rv, qn, kvn, kr = _evprep_fwd(z, cos, sin, w["qg"], w["kvg"])
    qf = _mm_nn("ev_uq", qn, w["w_uq"])
    kv = _mm_nn("ev_ukv", kvn, w["w_ukv"])
    Q, K, V = _attnprep_fwd(qf, kv, kr, cos, sin)
    a, lse = _flash_fwd(Q, K, V)
    o2, states = _ret_fwd(w["lg"], rq, rk, rv)
    mix = _mix_fwd(o2, z, a)
    y = _mm_nn("ev_out", mix, w["w_out"])
    x_new = _post_fwd(x, y, post_g, mod, bias)
    return x_new, dict(x=x, h=h, z=z, rq=rq, rk=rk, rv=rv, qn=qn, kvn=kvn, Q=Q, K=K, V=V, a=a, lse=lse, o2=o2,
                       states=states, mix=mix, y=y)


def _even_bwd(dx, s, mod, bias, pre_g, post_g, cos, sin, w):
    dy, dpost_g, dgate = _post_bwd(s["y"], dx, post_g, mod, bias)
    dmix = _mm_nn("ev_out_dx", dy, w["w_out_t"])
    dw_out = _mm_tn("ev_out_dw", s["mix"], dy)
    do, drg, da, dmg, delta = _mix_bwd(s["o2"], s["z"], s["a"], dmix)
    drq, drk, drv, dlg = _ret_bwd(w["lg"], s["rq"], s["rk"], s["rv"], do, s["states"])
    dQ, dK, dV = _flash_bwd(s["Q"], s["K"], s["V"], da, s["lse"], delta)
    dqf, dkv, dkr = _attnprep_bwd(dQ, dK, dV, cos, sin)
    dqn = _mm_nn("ev_uq_dx", dqf, w["w_uq_t"])
    dw_uq = _mm_tn("ev_uq_dw", s["qn"], dqf)
    dkvn = _mm_nn("ev_ukv_dx", dkv, w["w_ukv_t"])
    dw_ukv = _mm_tn("ev_ukv_dw", s["kvn"], dkv)
    dz, dqg, dkvg = _evprep_bwd(s["z"], cos, sin, w["qg"], w["kvg"], drq, drk, drv, drg, dqn, dkvn, dkr, dmg)
    dh = _mm_nn("ev_in_dx", dz, w["w_in_t"])
    dw_in = _mm_tn("ev_in_dw", s["h"], dz)
    dx_new, dpre_g, dscale, dshift = _pre_bwd(s["x"], dh, dx, pre_g, mod, bias)
    g = dict(ev_w_in=_w_in_unpad(dw_in), ev_w_uq=_unpad_heads(dw_uq, MLA_HEADS, MLA_NOPE_DIM + MLA_ROPE_DIM, QW),
             ev_w_ukv=dw_ukv, ev_w_out=dw_out, ev_q_norm_g=dqg[0], ev_kv_norm_g=dkvg[0],
             dlg_f=dlg[0, :RET_HEADS, 0], dlg_b=dlg[1, :RET_HEADS, 0])
    return dx_new, g, dpre_g[0], dpost_g[0], jnp.concatenate([dshift, dscale, dgate], axis=1)


def _odd_fwd(x, mod, bias, pre_g, post_g, w):
    h = _pre_fwd(x, pre_g, mod, bias)
    z = _mm_nn("od_in", h, w["w_in"])
    u = _glu_fwd(z, w["b_in"])
    uc = _conv_fwd(u, w["dw"], w["dw_b"])
    vv = _odmix_fwd(uc, z, w["b_in"], w["ln_g"], w["ln_b"])
    y = _mm_nn("od_out", vv, w["w_out"])
    x_new = _post_fwd(x, y, post_g, mod, bias)
    return x_new, dict(x=x, h=h, z=z, u=u, uc=uc, vv=vv, y=y)


def _odd_bwd(dx, s, mod, bias, pre_g, post_g, w):
    dy, dpost_g, dgate = _post_bwd(s["y"], dx, post_g, mod, bias)
    dvv = _mm_nn("od_out_dx", dy, w["w_out_t"])
    dw_out = _mm_tn("od_out_dw", s["vv"], dy)
    duc, dg, dbg, dln_g, dln_b = _odmix_bwd(s["uc"], s["z"], w["b_in"], w["ln_g"], w["ln_b"], dvv)
    du = _conv_fwd(duc, w["dw_flip"], jnp.zeros_like(w["dw_b"]))
    dwb = _conv_bwd_w(s["u"], duc)
    dz, dba, dbb = _glu_bwd(s["z"], w["b_in"], du, dg)
    dh = _mm_nn("od_in_dx", dz, w["w_in_t"])
    dw_in = _mm_tn("od_in_dw", s["h"], dz)
    dx_new, dpre_g, dscale, dshift = _pre_bwd(s["x"], dh, dx, pre_g, mod, bias)
    g = dict(od_w_in=dw_in, od_b_in=jnp.concatenate([dba, dbb, dbg], axis=1)[0], od_dw_w=dwb[:CONV_KERNEL], od_dw_b=dwb[CONV_KERNEL],
             od_ln_g=dln_g[0], od_ln_b=dln_b[0], od_w_out=dw_out)
    return dx_new, g, dpre_g[0], dpost_g[0], jnp.concatenate([dshift, dscale, dgate], axis=1)


def _local_step(x, c, positions, full, loss_target):
    D = D_MODEL
    cos, sin = _rope_tables(positions)
    ev, od = _prep_layer_weights(full)
    c16 = jnp.zeros((16, D), F32).at[0].set(c)
    c_act = _rowcall("silu_c", lambda rows, p: ([_silu(rows[0][...])], []), [c16], [], [(D, BF16)])[0]
    f32 = lambda a: a.astype(F32)
    mods = [_mm_nn("ada", c_act, full["ada_w"][l])[0:1] for l in range(DEPTH)]
    biases = [f32(full["ada_b"][l])[None] for l in range(DEPTH)]
    pre = [f32(full["pre_g"][l])[None] for l in range(DEPTH)]
    post = [f32(full["post_g"][l])[None] for l in range(DEPTH)]

    saved = []
    for l in range(DEPTH):
        if l % 2 == 0:
            x, s = _even_fwd(x, mods[l], biases[l], pre[l], post[l], cos, sin, ev[l // 2])
        else:
            x, s = _odd_fwd(x, mods[l], biases[l], pre[l], post[l], od[l // 2])
        saved.append(s)
    dx, loss = _loss_head(x, loss_target)

    per_layer = [None] * DEPTH
    dpre, dpost, dmod = [None] * DEPTH, [None] * DEPTH, [None] * DEPTH
    for l in reversed(range(DEPTH)):
        if l % 2 == 0:
            dx, per_layer[l], dpre[l], dpost[l], dmod[l] = _even_bwd(dx, saved[l], mods[l], biases[l], pre[l], post[l], cos, sin, ev[l // 2])
        else:
            dx, per_layer[l], dpre[l], dpost[l], dmod[l] = _odd_bwd(dx, saved[l], mods[l], biases[l], pre[l], post[l], od[l // 2])

    grads = {"pre_g": jnp.stack(dpre), "post_g": jnp.stack(dpost), "ada_b": jnp.concatenate(dmod, axis=0)}
    dmod16 = [jnp.pad(m, ((0, 15), (0, 0))).astype(BF16) for m in dmod]
    grads["ada_w"] = jnp.stack([_mm_tn("ada_dw", c_act, dmod16[l]) for l in range(DEPTH)])
    evs = [per_layer[l] for l in range(0, DEPTH, 2)]
    ods = [per_layer[l] for l in range(1, DEPTH, 2)]
    for name in ("ev_w_in", "ev_w_uq", "ev_w_ukv", "ev_w_out", "ev_q_norm_g", "ev_kv_norm_g"):
        grads[name] = jnp.stack([g[name] for g in evs])
    for name in ("od_w_in", "od_b_in", "od_dw_w", "od_dw_b", "od_ln_g", "od_ln_b", "od_w_out"):
        grads[name] = jnp.stack([g[name] for g in ods])
    grads["ev_dec_f"] = jnp.stack([g["dlg_f"] for g in evs]) * jax.nn.sigmoid(-f32(full["ev_dec_f"]))
    grads["ev_dec_b"] = jnp.stack([g["dlg_b"] for g in evs]) * jax.nn.sigmoid(-f32(full["ev_dec_b"]))
    return loss[0, 0], dx, grads


def kernel(x, c, positions, ada_w, ada_b, pre_g, post_g, ev_w_in, ev_dec_f, ev_dec_b, ev_q_norm_g, ev_w_uq, ev_kv_norm_g, ev_w_ukv, ev_w_out, od_w_in, od_b_in, od_dw_w, od_dw_b, od_ln_g, od_ln_b, od_w_out, loss_target, m_ada_w, m_ada_b, m_pre_g, m_post_g, m_ev_w_in, m_ev_dec_f, m_ev_dec_b, m_ev_q_norm_g, m_ev_w_uq, m_ev_kv_norm_g, m_ev_w_ukv, m_ev_w_out, m_od_w_in, m_od_b_in, m_od_dw_w, m_od_dw_b, m_od_ln_g, m_od_ln_b, m_od_w_out, v_ada_w, v_ada_b, v_pre_g, v_post_g, v_ev_w_in, v_ev_dec_f, v_ev_dec_b, v_ev_q_norm_g, v_ev_w_uq, v_ev_kv_norm_g, v_ev_w_ukv, v_ev_w_out, v_od_w_in, v_od_b_in, v_od_dw_w, v_od_dw_b, v_od_ln_g, v_od_ln_b, v_od_w_out):
    w = dict(ada_w=ada_w, ada_b=ada_b, pre_g=pre_g, post_g=post_g, ev_w_in=ev_w_in, ev_dec_f=ev_dec_f, ev_dec_b=ev_dec_b,
             ev_q_norm_g=ev_q_norm_g, ev_w_uq=ev_w_uq, ev_kv_norm_g=ev_kv_norm_g, ev_w_ukv=ev_w_ukv, ev_w_out=ev_w_out,
             od_w_in=od_w_in, od_b_in=od_b_in, od_dw_w=od_dw_w, od_dw_b=od_dw_b, od_ln_g=od_ln_g, od_ln_b=od_ln_b, od_w_out=od_w_out)
    m = dict(ada_w=m_ada_w, ada_b=m_ada_b, pre_g=m_pre_g, post_g=m_post_g, ev_w_in=m_ev_w_in, ev_dec_f=m_ev_dec_f, ev_dec_b=m_ev_dec_b,
             ev_q_norm_g=m_ev_q_norm_g, ev_w_uq=m_ev_w_uq, ev_kv_norm_g=m_ev_kv_norm_g, ev_w_ukv=m_ev_w_ukv, ev_w_out=m_ev_w_out,
             od_w_in=m_od_w_in, od_b_in=m_od_b_in, od_dw_w=m_od_dw_w, od_dw_b=m_od_dw_b, od_ln_g=m_od_ln_g, od_ln_b=m_od_ln_b, od_w_out=m_od_w_out)
    v = dict(ada_w=v_ada_w, ada_b=v_ada_b, pre_g=v_pre_g, post_g=v_post_g, ev_w_in=v_ev_w_in, ev_dec_f=v_ev_dec_f, ev_dec_b=v_ev_dec_b,
             ev_q_norm_g=v_ev_q_norm_g, ev_w_uq=v_ev_w_uq, ev_kv_norm_g=v_ev_kv_norm_g, ev_w_ukv=v_ev_w_ukv, ev_w_out=v_ev_w_out,
             od_w_in=v_od_w_in, od_b_in=v_od_b_in, od_dw_w=v_od_dw_w, od_dw_b=v_od_dw_b, od_ln_g=v_od_ln_g, od_ln_b=v_od_ln_b, od_w_out=v_od_w_out)

    full = _gather_weights({name: w[name] for name, _, _ in SHARDED})
    for name in REPLICATED:
        full[name] = w[name]
    loss_local, grad_x, grads = _local_step(x[0], c[0], positions[0], full, loss_target[0])
    loss = lax.psum(loss_local, ("x", "y", "c"))
    g = _reduce_grads(grads, {name: w[name].shape for name in WEIGHTS})
    delta, new_m, new_v = {}, {}, {}
    for name in WEIGHTS:
        delta[name], new_m[name], new_v[name] = _adamw(w[name], g[name], m[name], v[name])
    return (loss, grad_x[None], *[g[n] for n in WEIGHTS], *[delta[n] for n in WEIGHTS],
            *[new_m[n] for n in WEIGHTS], *[new_v[n] for n in WEIGHTS])
```

```python
import functools

import jax
import jax.numpy as jnp
from jax import lax
from jax.experimental import pallas as pl
from jax.experimental.pallas import tpu as pltpu

F32 = jnp.float32
BF16 = jnp.bfloat16

D_MODEL = 1024
DEPTH = 4
RET_HEADS = 4
RET_QK_DIM = 64
RET_V_DIM = 128
MLA_HEADS = 4
MLA_Q_RANK = 384
MLA_KV_RANK = 256
MLA_NOPE_DIM = 128
MLA_ROPE_DIM = 64
MLA_V_DIM = 128
ROPE_DIM = 64
ROPE_BASE = 10000.0
CONV_KERNEL = 31
EPS = 1e-6
ADAM_LR, ADAM_B1, ADAM_B2, ADAM_EPS, ADAM_WD, ADAM_STEP = 0.001, 0.9, 0.999, 1e-08, 0.01, 10

LANES = 128
HP = 128
Z_RQ, Z_RK, Z_RV, Z_RG, Z_CQ, Z_CKV, Z_KR, Z_MG, Z_END = 0, 512, 1024, 1536, 2048, 2432, 2688, 2816, 3328
V7X_VMEM_BYTES = 64 * 1024 * 1024
VMEM_CAP = V7X_VMEM_BYTES - 8 * 1024 * 1024

MESH = pl.DeviceIdType.MESH


def _nbytes(shape, dtype):
    n = 1
    for s in shape:
        n *= s
    return n * jnp.dtype(dtype).itemsize


def _params(sem, block_bytes):
    limit = min(VMEM_CAP, max(32 * 1024 * 1024, 2 * block_bytes + 16 * 1024 * 1024))
    return pltpu.CompilerParams(dimension_semantics=sem, vmem_limit_bytes=limit)


def _rowcall(name, fn, rows, params, outs, accs=(), ts=256):
    S = rows[0].shape[0]
    ts = min(ts, S)
    assert S % ts == 0, (name, S, ts)
    nr, npar, no, na = len(rows), len(params), len(outs), len(accs)

    def body(*refs):
        row_refs = refs[:nr]
        pvals = [r[...] for r in refs[nr:nr + npar]]
        out_refs = refs[nr + npar:nr + npar + no]
        acc_refs = refs[nr + npar + no:]
        ovals, avals = fn(row_refs, pvals)
        for r, v in zip(out_refs, ovals, strict=True):
            r[...] = v.astype(r.dtype)
        if na:
            @pl.when(pl.program_id(0) == 0)
            def _():
                for r in acc_refs:
                    r[...] = jnp.zeros_like(r)
            for r, v in zip(acc_refs, avals, strict=True):
                r[...] += v

    in_specs = [pl.BlockSpec((ts, a.shape[1]), lambda i: (i, 0)) for a in rows]
    in_specs += [pl.BlockSpec(p.shape, lambda i: (0, 0)) for p in params]
    out_specs = [pl.BlockSpec((ts, w), lambda i: (i, 0)) for w, _ in outs]
    out_specs += [pl.BlockSpec(s, lambda i: (0, 0)) for s in accs]
    out_shape = [jax.ShapeDtypeStruct((S, w), dt) for w, dt in outs]
    out_shape += [jax.ShapeDtypeStruct(s, F32) for s in accs]
    nb = sum(_nbytes((ts, a.shape[1]), a.dtype) for a in rows) + sum(_nbytes((ts, w), dt) for w, dt in outs)
    nb += sum(_nbytes(p.shape, p.dtype) for p in params) + sum(_nbytes(s, F32) for s in accs)
    res = pl.pallas_call(
        body, name=name, grid=(S // ts,), in_specs=in_specs, out_specs=out_specs, out_shape=out_shape,
        compiler_params=_params(("arbitrary",) if na else ("parallel",), 3 * nb),
    )(*rows, *params)
    return res


def _silu(x):
    return x * jax.nn.sigmoid(x)


def _rms(x, g):
    return x * lax.rsqrt(jnp.mean(x * x, axis=-1, keepdims=True) + EPS) * g


def _rot(x):
    lane = lax.broadcasted_iota(jnp.int32, x.shape, 1)
    return jnp.where(lane < ROPE_DIM // 2, -pltpu.roll(x, LANES - ROPE_DIM // 2, 1), pltpu.roll(x, ROPE_DIM // 2, 1))


def _rope(x, cos, sin):
    return x * cos + _rot(x) * sin


def _rope_t(dy, cos, sin):
    return dy * cos - _rot(dy) * sin


def _groups(ref, start, n):
    return [ref[:, start + HP * h:start + HP * (h + 1)] for h in range(n)]


def _pre_math(x, g, m_scale, b_scale, m_shift, b_shift):
    return _rms(x, g) * (1.0 + (m_scale + b_scale)) + (m_shift + b_shift)


def _pre_fwd(x, g, mod, bias):
    D = D_MODEL

    def fn(rows, p):
        g_, mod_, b_ = p
        h = _pre_math(rows[0][...], g_, mod_[:, D:2 * D], b_[:, D:2 * D], mod_[:, :D], b_[:, :D])
        return [h], []

    return _rowcall("pre_fwd", fn, [x], [g, mod, bias], [(D, BF16)])[0]


def _pre_bwd(x, dh, dx_res, g, mod, bias):
    D = D_MODEL

    def fn(rows, p):
        g_, mod_, b_ = p
        xv = rows[0][...]
        _, vjp = jax.vjp(_pre_math, xv, g_, mod_[:, D:2 * D], b_[:, D:2 * D], mod_[:, :D], b_[:, :D])
        dx, dg, dsc, _, dsh, _ = vjp(rows[1][...].astype(F32))
        return [dx + rows[2][...]], [dg, dsc, dsh]

    return _rowcall("pre_bwd", fn, [x, dh, dx_res], [g, mod, bias], [(D, F32)], [(1, D)] * 3)


def _post_math(x, y, g, m_gate, b_gate):
    return x + (m_gate + b_gate) * _rms(y, g)


def _post_fwd(x, y, g, mod, bias):
    D = D_MODEL

    def fn(rows, p):
        g_, mod_, b_ = p
        return [_post_math(rows[0][...], rows[1][...], g_, mod_[:, 2 * D:], b_[:, 2 * D:])], []

    return _rowcall("post_fwd", fn, [x, y], [g, mod, bias], [(D, F32)])[0]


def _post_bwd(y, dxo, g, mod, bias):
    D = D_MODEL

    def fn(rows, p):
        g_, mod_, b_ = p
        yv = rows[0][...]
        _, vjp = jax.vjp(lambda y_, g2, mg: _post_math(0.0, y_, g2, mg, b_[:, 2 * D:]), yv, g_, mod_[:, 2 * D:])
        dy, dg, dgate = vjp(rows[1][...])
        return [dy], [dg, dgate]

    return _rowcall("post_bwd", fn, [y, dxo], [g, mod, bias], [(D, BF16)], [(1, D)] * 2)


def _evprep_fwd(z, cos, sin, qg, kvg):
    def fn(rows, p):
        z_, c_, s_ = rows
        qg_, kvg_ = p
        cos_, sin_ = c_[...], s_[...]
        rq = jnp.concatenate([_rope(v[...], cos_, sin_) for v in _groups(z_, Z_RQ, RET_HEADS)], axis=1)
        rk = jnp.concatenate([_rope(v[...], cos_, sin_) for v in _groups(z_, Z_RK, RET_HEADS)], axis=1)
        rk = rk * (RET_QK_DIM ** -0.5)
        rv = z_[:, Z_RV:Z_RG]
        qn = _rms(z_[:, Z_CQ:Z_CKV], qg_)
        kvn = _rms(z_[:, Z_CKV:Z_KR], kvg_)
        kr = _rope(z_[:, Z_KR:Z_MG], cos_, sin_)
        return [rq, rk, rv, qn, kvn, kr], []

    return _rowcall("evprep_fwd", fn, [z, cos, sin], [qg, kvg],
                    [(512, BF16), (512, BF16), (512, BF16), (MLA_Q_RANK, BF16), (MLA_KV_RANK, BF16), (HP, F32)])


def _evprep_bwd(z, cos, sin, qg, kvg, drq, drk, drv, drg, dqn, dkvn, dkr, dmg):
    def fn(rows, p):
        z_, c_, s_, drq_, drk_, drv_, drg_, dqn_, dkvn_, dkr_, dmg_ = rows
        qg_, kvg_ = p
        cos_, sin_ = c_[...], s_[...]
        parts = []
        for h in range(RET_HEADS):
            parts.append(_rope_t(drq_[:, HP * h:HP * (h + 1)] + drq_[:, 512 + HP * h:512 + HP * (h + 1)], cos_, sin_))
        for h in range(RET_HEADS):
            t = drk_[:, HP * h:HP * (h + 1)] + drk_[:, 512 + HP * h:512 + HP * (h + 1)]
            parts.append(_rope_t(t, cos_, sin_) * (RET_QK_DIM ** -0.5))
        parts.append(drv_[:, :512] + drv_[:, 512:])
        parts.append(drg_[...])
        _, vq = jax.vjp(_rms, z_[:, Z_CQ:Z_CKV], qg_)
        dcq, dqg = vq(dqn_[...])
        _, vkv = jax.vjp(_rms, z_[:, Z_CKV:Z_KR], kvg_)
        dckv, dkvg = vkv(dkvn_[...])
        parts += [dcq, dckv, _rope_t(dkr_[...], cos_, sin_), dmg_[...]]
        return [jnp.concatenate([v.astype(BF16) for v in parts], axis=1)], [dqg, dkvg]

    rows = [z, cos, sin, drq, drk, drv, drg, dqn, dkvn, dkr, dmg]
    return _rowcall("evprep_bwd", fn, rows, [qg, kvg], [(Z_END, BF16)], [(1, MLA_Q_RANK), (1, MLA_KV_RANK)], ts=128)


ATT_SCALE = (MLA_NOPE_DIM + MLA_ROPE_DIM) ** -0.5
LOG2E = 1.4426950408889634
LN2 = 0.6931471805599453


def _attnprep_fwd(qf, kv, kr, cos, sin):
    qscale = ATT_SCALE * LOG2E

    def fn(rows, p):
        q_, kv_, kr_, c_, s_ = rows
        cos_, sin_ = c_[...], s_[...]
        krv = kr_[...]
        qs, ks, vs = [], [], []
        for h in range(MLA_HEADS):
            b = 2 * HP * h
            qs += [q_[:, b:b + HP] * qscale, _rope(q_[:, b + HP:b + 2 * HP], cos_, sin_) * qscale]
            ks += [kv_[:, b:b + HP], krv]
            vs += [kv_[:, b + HP:b + 2 * HP]]
        return [jnp.concatenate(qs, axis=1), jnp.concatenate(ks, axis=1), jnp.concatenate(vs, axis=1)], []

    return _rowcall("attnprep_fwd", fn, [qf, kv, kr, cos, sin], [], [(1024, BF16), (1024, BF16), (512, BF16)])


def _attnprep_bwd(dQ, dK, dV, cos, sin):
    def fn(rows, p):
        dq_, dk_, dv_, c_, s_ = rows
        cos_, sin_ = c_[...], s_[...]
        dqs, dkvs = [], []
        dkr = None
        for h in range(MLA_HEADS):
            b = 2 * HP * h
            dqs += [dq_[:, b:b + HP] * ATT_SCALE, _rope_t(dq_[:, b + HP:b + 2 * HP], cos_, sin_) * ATT_SCALE]
            dkvs += [dk_[:, b:b + HP] * LN2, dv_[:, HP * h:HP * (h + 1)]]
            t = dk_[:, b + HP:b + 2 * HP]
            dkr = t if dkr is None else dkr + t
        return [jnp.concatenate(dqs, axis=1), jnp.concatenate(dkvs, axis=1), dkr * LN2], []

    return _rowcall("attnprep_bwd", fn, [dQ, dK, dV, cos, sin], [], [(1024, BF16), (1024, BF16), (HP, F32)])


def _mix_math(o, rg, a, mg):
    outs = []
    for h in range(RET_HEADS):
        oh = o[:, HP * h:HP * (h + 1)]
        mu = jnp.mean(oh, axis=-1, keepdims=True)
        var = jnp.mean(jnp.square(oh - mu), axis=-1, keepdims=True)
        outs.append((oh - mu) * lax.rsqrt(var + EPS))
    ret = jnp.concatenate(outs, axis=1) * _silu(rg)
    return jnp.concatenate([ret, a * _silu(mg)], axis=1)


def _mix_fwd(o2, z, a):
    def fn(rows, p):
        o_, z_, a_ = rows
        return [_mix_math(o_[:, :512] + o_[:, 512:], z_[:, Z_RG:Z_CQ], a_[...], z_[:, Z_MG:Z_END])], []

    return _rowcall("mix_fwd", fn, [o2, z, a], [], [(1024, BF16)])[0]


def _mix_bwd(o2, z, a, dmix):
    def fn(rows, p):
        o_, z_, a_, dm_ = rows
        _, vjp = jax.vjp(_mix_math, o_[:, :512] + o_[:, 512:], z_[:, Z_RG:Z_CQ], a_[...], z_[:, Z_MG:Z_END])
        return list(vjp(dm_[...].astype(F32))), []

    return _rowcall("mix_bwd", fn, [o2, z, a, dmix], [], [(512, BF16), (512, F32), (512, BF16), (512, F32)])


def _glu_math(a, b, ba, bb):
    return (a + ba) * jax.nn.sigmoid(b + bb)


def _glu_fwd(z, bin_):
    D = D_MODEL

    def fn(rows, p):
        z_, = rows
        b_, = p
        return [_glu_math(z_[:, :D], z_[:, D:2 * D], b_[:, :D], b_[:, D:2 * D])], []

    return _rowcall("glu_fwd", fn, [z], [bin_], [(D, F32)])[0]


def _odmix_math(uc, g, bg, ln_g, ln_b):
    mu = jnp.mean(uc, axis=-1, keepdims=True)
    var = jnp.mean(jnp.square(uc - mu), axis=-1, keepdims=True)
    y = (uc - mu) * lax.rsqrt(var + EPS) * ln_g + ln_b
    return _silu(y) * _silu(g + bg)


def _odmix_fwd(uc, z, bin_, ln_g, ln_b):
    D = D_MODEL

    def fn(rows, p):
        uc_, z_ = rows
        b_, g_, lb_ = p
        return [_odmix_math(uc_[...], z_[:, 2 * D:], b_[:, 2 * D:], g_, lb_)], []

    return _rowcall("odmix_fwd", fn, [uc, z], [bin_, ln_g, ln_b], [(D, BF16)])[0]


def _odmix_bwd(uc, z, bin_, ln_g, ln_b, dv):
    D = D_MODEL

    def fn(rows, p):
        uc_, z_, dv_ = rows
        b_, g_, lb_ = p
        _, vjp = jax.vjp(_odmix_math, uc_[...], z_[:, 2 * D:], b_[:, 2 * D:], g_, lb_)
        duc, dg, dbg, dlg, dlb = vjp(dv_[...])
        return [duc, dg], [dbg, dlg, dlb]

    return _rowcall("odmix_bwd", fn, [uc, z, dv], [bin_, ln_g, ln_b], [(D, F32), (D, BF16)], [(1, D)] * 3)


def _glu_bwd(z, bin_, du, dg):
    D = D_MODEL

    def fn(rows, p):
        z_, du_, dg_ = rows
        b_, = p
        _, vjp = jax.vjp(_glu_math, z_[:, :D], z_[:, D:2 * D], b_[:, :D], b_[:, D:2 * D])
        da, db, dba, dbb = vjp(du_[...])
        return [jnp.concatenate([da.astype(BF16), db.astype(BF16), dg_[...]], axis=1)], [dba, dbb]

    return _rowcall("glu_bwd", fn, [z, du, dg], [bin_], [(3 * D, BF16)], [(1, D)] * 2)


def _loss_head(x, tgt):
    D = D_MODEL

    def fn(rows, p):
        err = rows[0][...] - rows[1][...]
        part = 0.5 * jnp.sum(jnp.mean(err * err, axis=-1, keepdims=True), axis=0, keepdims=True)
        return [err * (1.0 / D)], [jnp.broadcast_to(part, (1, LANES))]

    return _rowcall("loss_head", fn, [x, tgt], [], [(D, F32)], [(1, LANES)])


def _tile(n, cap):
    if n <= cap:
        return n
    best = None
    for t in range(LANES, cap + 1, LANES):
        if n % t == 0:
            best = t
    assert best is not None, (n, cap)
    return best


MM_TN_CAP = 1792


def _mm_nn(name, a, b, out_dtype=F32, tm=512, tn_cap=MM_TN_CAP):
    M, K = a.shape
    N = b.shape[1]
    tm = min(tm, M)
    tn = _tile(N, tn_cap)

    def body(a_ref, b_ref, o_ref):
        o_ref[...] = jnp.dot(a_ref[...], b_ref[...], preferred_element_type=F32).astype(o_ref.dtype)

    nb = _nbytes((tm, K), a.dtype) + _nbytes((K, tn), b.dtype) + _nbytes((tm, tn), out_dtype) + _nbytes((tm, tn), F32)
    return pl.pallas_call(
        body, name=name, grid=(N // tn, M // tm),
        in_specs=[pl.BlockSpec((tm, K), lambda j, i: (i, 0)), pl.BlockSpec((K, tn), lambda j, i: (0, j))],
        out_specs=pl.BlockSpec((tm, tn), lambda j, i: (i, j)),
        out_shape=jax.ShapeDtypeStruct((M, N), out_dtype),
        compiler_params=_params(("parallel", "parallel"), nb),
    )(a, b)


def _mm_tn(name, a, b, ts=512, tm_cap=512, tn_cap=MM_TN_CAP):
    S, M = a.shape
    N = b.shape[1]
    ts = min(ts, S)
    tm = _tile(M, tm_cap)
    tn = _tile(N, tn_cap)

    def body(a_ref, b_ref, o_ref):
        @pl.when(pl.program_id(2) == 0)
        def _():
            o_ref[...] = jnp.zeros_like(o_ref)

        o_ref[...] += lax.dot_general(a_ref[...], b_ref[...], (((0,), (0,)), ((), ())), preferred_element_type=F32)

    nb = _nbytes((ts, tm), a.dtype) + _nbytes((ts, tn), b.dtype) + 2 * _nbytes((tm, tn), F32)
    return pl.pallas_call(
        body, name=name, grid=(M // tm, N // tn, S // ts),
        in_specs=[pl.BlockSpec((ts, tm), lambda i, j, s: (s, i)), pl.BlockSpec((ts, tn), lambda i, j, s: (s, j))],
        out_specs=pl.BlockSpec((tm, tn), lambda i, j, s: (i, j)),
        out_shape=jax.ShapeDtypeStruct((M, N), F32),
        compiler_params=_params(("parallel", "parallel", "arbitrary"), nb),
    )(a, b)


ATT_TQ = 512
ATT_TK = 512
QW = 2 * HP


NT_DIMS = (((1,), (1,)), ((), ()))
TN_DIMS = (((0,), (0,)), ((), ()))


def _kv_tiles(V, tk):
    S = V.shape[0]
    return V.reshape(S // tk, tk, MLA_HEADS, HP).transpose(2, 0, 3, 1)


def _flash_fwd(Q, K, V):
    S = Q.shape[0]
    H = MLA_HEADS
    tq, tk = min(ATT_TQ, S), min(ATT_TK, S)
    nk = S // tk
    VT = _kv_tiles(V, tk)

    def body(q_ref, k_ref, vt_ref, o_ref, lse_ref, m_s, l_s, acc_s, s_a, s_b):
        m_s[...] = jnp.full(m_s.shape, -jnp.inf, F32)
        l_s[...] = jnp.zeros(l_s.shape, F32)
        acc_s[...] = jnp.zeros(acc_s.shape, F32)

        def scores(j):
            k = k_ref[pl.ds(pl.multiple_of(j * tk, tk), tk), :]
            return lax.dot_general(k, q_ref[...], NT_DIMS, preferred_element_type=F32)

        def consume(st, j):
            m_prev = m_s[...]
            m_new = jnp.maximum(m_prev, jnp.max(st, axis=0, keepdims=True))
            alpha = jnp.exp2(m_prev - m_new)
            pt = jnp.exp2(st - m_new)
            l_s[...] = alpha * l_s[...] + jnp.sum(pt, axis=0, keepdims=True)
            acc_s[...] = alpha * acc_s[...] + jnp.dot(vt_ref[0, j], pt.astype(BF16), preferred_element_type=F32)
            m_s[...] = m_new

        if nk % 2:
            def step(j, carry):
                consume(scores(j), j)
                return carry

            lax.fori_loop(0, nk, step, 0)
        else:
            s_a[...] = scores(0)

            def pair(jj, carry):
                j0 = 2 * jj
                s_b[...] = scores(j0 + 1)
                consume(s_a[...], j0)
                s_a[...] = scores(jnp.minimum(j0 + 2, nk - 1))
                consume(s_b[...], j0 + 1)
                return carry

            lax.fori_loop(0, nk // 2, pair, 0)
        l = l_s[...]
        o_ref[...] = (acc_s[...] * (1.0 / l)).T
        lse_ref[0] = m_s[...] + jnp.log2(l)

    nb = (_nbytes((tq, QW), BF16) + _nbytes((S, QW), BF16) + _nbytes((S, HP), BF16) + 3 * _nbytes((tq, HP), F32)
          + 4 * _nbytes((tq, tk), F32))
    return pl.pallas_call(
        body, name="flash_fwd", grid=(H, S // tq),
        in_specs=[pl.BlockSpec((tq, QW), lambda h, i: (i, h)), pl.BlockSpec((S, QW), lambda h, i: (0, h)),
                  pl.BlockSpec((1, nk, HP, tk), lambda h, i: (h, 0, 0, 0))],
        out_specs=[pl.BlockSpec((tq, HP), lambda h, i: (i, h)), pl.BlockSpec((1, 1, tq), lambda h, i: (h, 0, i))],
        out_shape=[jax.ShapeDtypeStruct((S, H * HP), F32), jax.ShapeDtypeStruct((H, 1, S), F32)],
        scratch_shapes=[pltpu.VMEM((1, tq), F32), pltpu.VMEM((1, tq), F32), pltpu.VMEM((HP, tq), F32),
                        pltpu.VMEM((tk, tq), F32), pltpu.VMEM((tk, tq), F32)],
        compiler_params=_params(("parallel", "parallel"), nb),
    )(Q, K, VT)


def _flash_bwd(Q, K, V, dO, O, lse):
    S = Q.shape[0]
    H = MLA_HEADS
    tq, tk = min(ATT_TQ, S), min(ATT_TK, S)
    nk = S // tk

    def body(q_ref, do_ref, o_ref, lse_ref, k_ref, v_ref, dq_ref, dk_ref, dv_ref, dq_s):
        @pl.when(pl.program_id(1) == 0)
        def _():
            dk_ref[...] = jnp.zeros(dk_ref.shape, F32)
            dv_ref[...] = jnp.zeros(dv_ref.shape, F32)

        delta = jnp.sum((do_ref[...].astype(F32) * o_ref[...]).T, axis=0, keepdims=True)
        lse = lse_ref[0]
        dq_s[...] = jnp.zeros(dq_s.shape, F32)

        def rows(j):
            return pl.ds(pl.multiple_of(j * tk, tk), tk)

        def scores(j):
            st = lax.dot_general(k_ref[rows(j), :], q_ref[...], NT_DIMS, preferred_element_type=F32)
            dpt = lax.dot_general(v_ref[rows(j), :], do_ref[...], NT_DIMS, preferred_element_type=F32)
            return st, dpt

        def consume(st, dpt, j):
            pt = jnp.exp2(st - lse)
            dst = (pt * (dpt - delta)).astype(BF16)
            dv_ref[rows(j), :] += jnp.dot(pt.astype(BF16), do_ref[...], preferred_element_type=F32)
            dk_ref[rows(j), :] += jnp.dot(dst, q_ref[...], preferred_element_type=F32)
            dq_s[...] += lax.dot_general(dst, k_ref[rows(j), :], TN_DIMS, preferred_element_type=F32)

        def step(j, carry):
            consume(*scores(j), j)
            return carry

        lax.fori_loop(0, nk, step, 0)
        dq_ref[...] = dq_s[...]

    nb = (_nbytes((tq, QW), BF16) + _nbytes((tq, HP), BF16) + _nbytes((tq, HP), F32) + _nbytes((S, QW), BF16)
          + _nbytes((S, HP), BF16) + 2 * _nbytes((tq, QW), F32) + _nbytes((S, QW), F32) + _nbytes((S, HP), F32))
    return pl.pallas_call(
        body, name="flash_bwd", grid=(H, S // tq),
        in_specs=[pl.BlockSpec((tq, QW), lambda h, i: (i, h)), pl.BlockSpec((tq, HP), lambda h, i: (i, h)),
                  pl.BlockSpec((tq, HP), lambda h, i: (i, h)), pl.BlockSpec((1, 1, tq), lambda h, i: (h, 0, i)),
                  pl.BlockSpec((S, QW), lambda h, i: (0, h)), pl.BlockSpec((S, HP), lambda h, i: (0, h))],
        out_specs=[pl.BlockSpec((tq, QW), lambda h, i: (i, h)), pl.BlockSpec((S, QW), lambda h, i: (0, h)),
                   pl.BlockSpec((S, HP), lambda h, i: (0, h))],
        out_shape=[jax.ShapeDtypeStruct((S, H * QW), F32), jax.ShapeDtypeStruct((S, H * QW), F32),
                   jax.ShapeDtypeStruct((S, H * HP), F32)],
        scratch_shapes=[pltpu.VMEM((tq, QW), F32)],
        compiler_params=_params(("parallel", "arbitrary"), nb),
    )(Q, dO, O, lse, K, V)


RET_CHUNK = 128


def _ret_tables(d, lg_ref, h, C):
    ii = lax.broadcasted_iota(jnp.int32, (C, C), 0).astype(F32)
    jj = lax.broadcasted_iota(jnp.int32, (C, C), 1).astype(F32)
    ci = lax.broadcasted_iota(jnp.int32, (C, 1), 0).astype(F32)
    fwd = d == 0
    dist = jnp.where(fwd, ii - jj, jj - ii)
    mask = dist >= jnp.where(fwd, 0.0, 1.0)
    dist = jnp.maximum(dist, 0.0)
    qpos = jnp.where(fwd, ci + 1.0, C - ci)
    kpos = jnp.where(fwd, C - 1.0 - ci, ci)
    lg = lg_ref[d, h]
    D = jnp.where(mask, jnp.exp(lg * dist), 0.0)
    return D, jnp.exp(lg * qpos), jnp.exp(lg * kpos), jnp.exp(lg * C), dist, qpos, kpos


def _ret_fwd(lg, q, k, v):
    S = q.shape[0]
    C = min(RET_CHUNK, S)
    N = S // C
    H = RET_HEADS

    def chunk(d, n):
        return jnp.where(d == 0, n, N - 1 - n)

    def body(lg_ref, q_ref, k_ref, v_ref, o_ref, st_ref, state):
        d, n = pl.program_id(0), pl.program_id(1)

        @pl.when(n == 0)
        def _():
            state[...] = jnp.zeros_like(state)

        for h in range(H):
            sl = slice(HP * h, HP * (h + 1))
            D, qw, kw, gc, _, _, _ = _ret_tables(d, lg_ref, h, C)
            qh, kh, vh = q_ref[:, sl], k_ref[:, sl], v_ref[:, sl]
            st = state[sl, :]
            sm = lax.dot_general(qh, kh, (((1,), (1,)), ((), ())), preferred_element_type=F32) * D
            inner = jnp.dot(sm.astype(BF16), vh, preferred_element_type=F32)
            cross = qw * jnp.dot(qh, st.astype(BF16), preferred_element_type=F32)
            o_ref[:, sl] = inner + cross
            st_ref[0, 0, sl, :] = st
            kvn = lax.dot_general((kh.astype(F32) * kw).astype(BF16), vh, (((0,), (0,)), ((), ())), preferred_element_type=F32)
            state[sl, :] = gc * st + kvn

    nb = 3 * _nbytes((C, 512), BF16) + _nbytes((C, 512), F32) + 2 * _nbytes((512, HP), F32) + 8 * _nbytes((C, C), F32)
    return pl.pallas_call(
        body, name="ret_fwd", grid=(2, N),
        in_specs=[pl.BlockSpec(memory_space=pltpu.SMEM)] + [pl.BlockSpec((C, 512), lambda d, n: (chunk(d, n), 0))] * 3,
        out_specs=[pl.BlockSpec((C, 512), lambda d, n: (chunk(d, n), d)),
                   pl.BlockSpec((1, 1, 512, HP), lambda d, n: (d, chunk(d, n), 0, 0))],
        out_shape=[jax.ShapeDtypeStruct((S, 1024), F32), jax.ShapeDtypeStruct((2, N, 512, HP), F32)],
        scratch_shapes=[pltpu.VMEM((512, HP), F32)],
        compiler_params=_params(("arbitrary", "arbitrary"), nb),
    )(lg, q, k, v)


def _ret_bwd(lg, q, k, v, do, states):
    S = q.shape[0]
    C = min(RET_CHUNK, S)
    N = S // C
    H = RET_HEADS

    def chunk(d, n):
        return jnp.where(d == 0, N - 1 - n, n)

    def body(lg_ref, q_ref, k_ref, v_ref, do_ref, st_ref, dq_ref, dk_ref, dv_ref, dlg_ref, G, accA, accB, accC):
        d, n = pl.program_id(0), pl.program_id(1)

        @pl.when(n == 0)
        def _():
            G[...] = jnp.zeros_like(G)
            accA[...] = jnp.zeros_like(accA)
            accB[...] = jnp.zeros_like(accB)
            accC[...] = jnp.zeros_like(accC)

        nt = (((1,), (1,)), ((), ()))
        tn = (((0,), (0,)), ((), ()))
        for h in range(H):
            sl = slice(HP * h, HP * (h + 1))
            D, qw, kw, gc, dist, qpos, kpos = _ret_tables(d, lg_ref, h, C)
            qh, kh, vh, doh = q_ref[:, sl], k_ref[:, sl], v_ref[:, sl], do_ref[:, sl]
            st = st_ref[0, 0, sl, :]
            g = G[sl, :]
            stb, gb = st.astype(BF16), g.astype(BF16)
            sraw = lax.dot_general(qh, kh, nt, preferred_element_type=F32)
            dS = lax.dot_general(doh, vh, nt, preferred_element_type=F32) * D
            dSb = dS.astype(BF16)
            smb = (sraw * D).astype(BF16)
            qs = jnp.dot(qh, stb, preferred_element_type=F32)
            kg = jnp.dot(kh, gb, preferred_element_type=F32)
            dq_ref[:, sl] = jnp.dot(dSb, kh, preferred_element_type=F32) + qw * lax.dot_general(doh, stb, nt, preferred_element_type=F32)
            dk_ref[:, sl] = lax.dot_general(dSb, qh, tn, preferred_element_type=F32) + kw * lax.dot_general(vh, gb, nt, preferred_element_type=F32)
            dv_ref[:, sl] = lax.dot_general(smb, doh, tn, preferred_element_type=F32) + kw * kg
            dof, vf = doh.astype(F32), vh.astype(F32)
            accA[h] += sraw * dS * dist
            accB[h] += (qw * qpos) * (qs * dof) + (kw * kpos) * (kg * vf)
            accC[h] += (C * gc) * (st * g)
            G[sl, :] = gc * g + lax.dot_general((qh.astype(F32) * qw).astype(BF16), doh, tn, preferred_element_type=F32)

        @pl.when(n == N - 1)
        def _():
            rows = [jnp.broadcast_to(jnp.sum(accA[h]) + jnp.sum(accB[h]) + jnp.sum(accC[h]), (1, HP)) for h in range(H)]
            dlg_ref[0] = jnp.concatenate(rows + [jnp.zeros((8 - H, HP), F32)], axis=0)

    nb = (4 * _nbytes((C, 512), BF16) + 3 * _nbytes((C, 512), F32) + 2 * _nbytes((512, HP), F32)
          + H * (_nbytes((C, C), F32) + _nbytes((C, HP), F32) + _nbytes((HP, HP), F32)) + 12 * _nbytes((C, C), F32))
    row = lambda d, n: (chunk(d, n), 0)
    out3 = lambda d, n: (chunk(d, n), d)
    return pl.pallas_call(
        body, name="ret_bwd", grid=(2, N),
        in_specs=[pl.BlockSpec(memory_space=pltpu.SMEM)] + [pl.BlockSpec((C, 512), row)] * 4
        + [pl.BlockSpec((1, 1, 512, HP), lambda d, n: (d, chunk(d, n), 0, 0))],
        out_specs=[pl.BlockSpec((C, 512), out3)] * 3 + [pl.BlockSpec((1, 8, HP), lambda d, n: (d, 0, 0))],
        out_shape=[jax.ShapeDtypeStruct((S, 1024), F32)] * 3 + [jax.ShapeDtypeStruct((2, 8, HP), F32)],
        scratch_shapes=[pltpu.VMEM((512, HP), F32), pltpu.VMEM((H, C, C), F32), pltpu.VMEM((H, C, HP), F32),
                        pltpu.VMEM((H, HP, HP), F32)],
        compiler_params=_params(("arbitrary", "arbitrary"), nb),
    )(lg, q, k, v, do, states)


CONV_PAD = 16
CONV_TR = 256
CONV_CB = LANES


def _fill_padded(pad, u_ref, S):
    pad[0:CONV_PAD, :] = jnp.zeros((CONV_PAD, CONV_CB), F32)
    pad[CONV_PAD + S:CONV_PAD + S + CONV_PAD, :] = jnp.zeros((CONV_PAD, CONV_CB), F32)
    pad[CONV_PAD:CONV_PAD + S, :] = u_ref[...]


def _conv_fwd(u, w32, b):
    S, D = u.shape
    tr = min(CONV_TR, S)

    def body(u_ref, w_ref, b_ref, o_ref, pad):
        _fill_padded(pad, u_ref, S)
        wv = w_ref[...]
        bv = b_ref[...]

        def step(t, carry):
            r0 = pl.multiple_of(t * tr, tr)
            win = pad[pl.ds(r0, tr + 2 * CONV_PAD), :]
            acc = jnp.broadcast_to(bv, (tr, CONV_CB))
            for k in range(CONV_KERNEL):
                acc = acc + win[k + 1:k + 1 + tr, :] * wv[k:k + 1, :]
            o_ref[pl.ds(r0, tr), :] = acc
            return carry

        lax.fori_loop(0, S // tr, step, 0)

    nb = 2 * _nbytes((S, CONV_CB), F32) + _nbytes((S + 2 * CONV_PAD, CONV_CB), F32)
    return pl.pallas_call(
        body, name="conv_fwd", grid=(D // CONV_CB,),
        in_specs=[pl.BlockSpec((S, CONV_CB), lambda j: (0, j)), pl.BlockSpec((32, CONV_CB), lambda j: (0, j)),
                  pl.BlockSpec((1, CONV_CB), lambda j: (0, j))],
        out_specs=pl.BlockSpec((S, CONV_CB), lambda j: (0, j)),
        out_shape=jax.ShapeDtypeStruct((S, D), F32),
        scratch_shapes=[pltpu.VMEM((S + 2 * CONV_PAD, CONV_CB), F32)],
        compiler_params=_params(("parallel",), nb),
    )(u, w32, b)


def _conv_bwd_w(u, dout):
    S, D = u.shape
    tr = min(CONV_TR, S)

    def body(u_ref, d_ref, o_ref, pad, acc):
        _fill_padded(pad, u_ref, S)
        acc[...] = jnp.zeros_like(acc)

        def fold(a):
            return jnp.sum(a.reshape(tr // 8, 8, CONV_CB), axis=0)

        def step(t, carry):
            r0 = pl.multiple_of(t * tr, tr)
            win = pad[pl.ds(r0, tr + 2 * CONV_PAD), :]
            dv = d_ref[pl.ds(r0, tr), :]
            for k in range(CONV_KERNEL):
                acc[8 * k:8 * k + 8, :] += fold(win[k + 1:k + 1 + tr, :] * dv)
            acc[8 * CONV_KERNEL:8 * CONV_KERNEL + 8, :] += fold(dv)
            return carry

        lax.fori_loop(0, S // tr, step, 0)
        o_ref[...] = jnp.sum(acc[...].reshape(32, 8, CONV_CB), axis=1)

    nb = 2 * _nbytes((S, CONV_CB), F32) + _nbytes((S + 2 * CONV_PAD, CONV_CB), F32)
    return pl.pallas_call(
        body, name="conv_bwd_w", grid=(D // CONV_CB,),
        in_specs=[pl.BlockSpec((S, CONV_CB), lambda j: (0, j)), pl.BlockSpec((S, CONV_CB), lambda j: (0, j))],
        out_specs=pl.BlockSpec((32, CONV_CB), lambda j: (0, j)),
        out_shape=jax.ShapeDtypeStruct((32, D), F32),
        scratch_shapes=[pltpu.VMEM((S + 2 * CONV_PAD, CONV_CB), F32), pltpu.VMEM((256, CONV_CB), F32)],
        compiler_params=_params(("parallel",), nb),
    )(u, dout)


def _adamw(w, g, m, v):
    shape = w.shape
    w2, g2, m2, v2 = [a.reshape(-1, shape[-1]) for a in (w, g, m, v)]
    W = shape[-1]

    def fn(rows, p):
        w_, g_, m_, v_ = [r[...] for r in rows]
        mn = ADAM_B1 * m_ + (1.0 - ADAM_B1) * g_
        vn = ADAM_B2 * v_ + (1.0 - ADAM_B2) * jnp.square(g_)
        m_hat = mn / (1.0 - ADAM_B1 ** ADAM_STEP)
        v_hat = vn / (1.0 - ADAM_B2 ** ADAM_STEP)
        delta = -ADAM_LR * (m_hat / (jnp.sqrt(v_hat) + ADAM_EPS) + ADAM_WD * w_)
        return [delta, mn, vn], []

    R2 = w2.shape[0]
    ts = 256 if (R2 > 256 and R2 % 256 == 0) else R2
    d, mn, vn = _rowcall("adamw", fn, [w2, g2, m2, v2], [], [(W, F32)] * 3, ts=ts)
    return d.reshape(shape), mn.reshape(shape), vn.reshape(shape)


_HBM = pl.BlockSpec(memory_space=pltpu.HBM)


def _place():
    return lax.axis_index("x"), lax.axis_index("y"), lax.axis_index("c")


def _half_rows(ref, half, h):
    idx = (slice(None),) * (len(ref.shape) - 2) + (pl.ds(half * h, h), slice(None))
    return ref.at[idx]


def _chip_exchange(name, p, bcast):
    if bcast:
        h, W = p.shape[0] // 2, p.shape[1]
    else:
        h, W = p.shape[1], p.shape[2]

    def body(p_ref, o_ref, send_sems, recv_sems, local_sem):
        x, y, c = _place()
        k_me = 2 * x + y

        def src(k):
            return _half_rows(p_ref, c, h) if bcast else p_ref.at[k]

        local = pltpu.make_async_copy(src(k_me), o_ref.at[k_me], local_sem)
        local.start()
        copies = []
        for j, (cx, cy) in enumerate([(1 - x, y), (x, 1 - y), (1 - x, 1 - y)]):
            cp = pltpu.make_async_remote_copy(
                src_ref=src(2 * cx + cy), dst_ref=o_ref.at[k_me], send_sem=send_sems.at[j], recv_sem=recv_sems.at[j],
                device_id=(cx, cy, c), device_id_type=MESH)
            cp.start()
            copies.append(cp)
        for cp in copies:
            cp.wait()
        local.wait()

    return pl.pallas_call(
        body, name=name, in_specs=[_HBM], out_specs=_HBM,
        out_shape=jax.ShapeDtypeStruct((4, h, W), p.dtype),
        scratch_shapes=[pltpu.SemaphoreType.DMA((3,)), pltpu.SemaphoreType.DMA((3,)), pltpu.SemaphoreType.DMA],
    )(p)


def _sibling_send_half(name, g):
    n, R, W = g.shape
    h = R // 2

    def body(g_ref, o_ref, send_sem, recv_sem):
        x, y, c = _place()
        cp = pltpu.make_async_remote_copy(
            src_ref=_half_rows(g_ref, 1 - c, h), dst_ref=o_ref, send_sem=send_sem, recv_sem=recv_sem,
            device_id=(x, y, 1 - c), device_id_type=MESH)
        cp.start()
        cp.wait()

    return pl.pallas_call(
        body, name=name, in_specs=[_HBM], out_specs=_HBM,
        out_shape=jax.ShapeDtypeStruct((n, h, W), g.dtype),
        scratch_shapes=[pltpu.SemaphoreType.DMA, pltpu.SemaphoreType.DMA],
    )(g)


def _halves_merge(name, mine):
    def body(m_ref, o_ref, send_sem, recv_sem):
        x, y, c = _place()
        cp = pltpu.make_async_remote_copy(
            src_ref=m_ref, dst_ref=o_ref, send_sem=send_sem, recv_sem=recv_sem,
            device_id=(x, y, 1 - c), device_id_type=MESH)
        cp.start()
        cp.wait()

    theirs = pl.pallas_call(
        body, name=name, in_specs=[_HBM], out_specs=_HBM,
        out_shape=jax.ShapeDtypeStruct(mine.shape, mine.dtype),
        scratch_shapes=[pltpu.SemaphoreType.DMA, pltpu.SemaphoreType.DMA],
    )(mine)
    south = lax.axis_index("c") == 0
    axis = mine.ndim - 2
    return jnp.concatenate([jnp.where(south, mine, theirs), jnp.where(south, theirs, mine)], axis=axis)


def _add2(name, a, b, out_dtype):
    def fn(rows, p):
        return [rows[0][...].astype(F32) + rows[1][...].astype(F32)], []

    return _rowcall(name, fn, [a, b], [], [(a.shape[1], out_dtype)], ts=PACK_TILE)[0]


def _sum4(name, b):
    _, h, W = b.shape
    ts = PACK_TILE if h % PACK_TILE == 0 else h

    def body(b_ref, o_ref):
        o_ref[...] = ((b_ref[0].astype(F32) + b_ref[1].astype(F32)) + b_ref[2].astype(F32)) + b_ref[3].astype(F32)

    nb = _nbytes((4, ts, W), b.dtype) + _nbytes((ts, W), F32)
    return pl.pallas_call(
        body, name=name, grid=(h // ts,),
        in_specs=[pl.BlockSpec((4, ts, W), lambda i: (0, i, 0))], out_specs=pl.BlockSpec((ts, W), lambda i: (i, 0)),
        out_shape=jax.ShapeDtypeStruct((h, W), F32), compiler_params=_params(("parallel",), nb),
    )(b)


PACK_W = 1024
PACK_ROWS = 16
PACK_TILE = 256
SHARDED = (("ada_w", "col", False), ("ev_w_in", "col", False), ("ev_w_uq", "col", False), ("ev_w_ukv", "col", False),
           ("ev_w_out", "row", False), ("od_w_in", "col", False), ("od_b_in", "col", True), ("od_dw_w", "col", True),
           ("od_dw_b", "col", True), ("od_ln_g", "col", True), ("od_ln_b", "col", True), ("od_w_out", "row", False))
REPLICATED = ("ada_b", "pre_g", "post_g", "ev_dec_f", "ev_dec_b", "ev_q_norm_g", "ev_kv_norm_g")
WEIGHTS = ("ada_w", "ada_b", "pre_g", "post_g", "ev_w_in", "ev_dec_f", "ev_dec_b", "ev_q_norm_g", "ev_w_uq", "ev_kv_norm_g",
           "ev_w_ukv", "ev_w_out", "od_w_in", "od_b_in", "od_dw_w", "od_dw_b", "od_ln_g", "od_ln_b", "od_w_out")


def _rows_of(n):
    unit = PACK_W * PACK_ROWS
    return (n + unit - 1) // unit * PACK_ROWS


def _to_rows(flat, lead):
    n = flat.shape[-1]
    r = _rows_of(n)
    flat = jnp.pad(flat, [(0, 0)] * len(lead) + [(0, r * PACK_W - n)])
    return flat.reshape(*lead, r, PACK_W)


def _pad_total(buf, axis):
    r = buf.shape[axis]
    r2 = (r + 2 * PACK_TILE - 1) // (2 * PACK_TILE) * (2 * PACK_TILE)
    pads = [(0, 0)] * buf.ndim
    pads[axis] = (0, r2 - r)
    return jnp.pad(buf, pads)


def _split_chips(full, how):
    if how == "col":
        n = full.shape[-1] // 4
        return jnp.moveaxis(full.reshape(*full.shape[:-1], 4, n), -2, 0)
    n = full.shape[1] // 4
    return jnp.moveaxis(full.reshape(full.shape[0], 4, n, *full.shape[2:]), 1, 0)


def _join_chips(blocks, how):
    if how == "col":
        t = jnp.moveaxis(blocks, 0, -2)
        return t.reshape(*t.shape[:-2], t.shape[-2] * t.shape[-1])
    t = jnp.moveaxis(blocks, 0, 1)
    return t.reshape(t.shape[0], t.shape[1] * t.shape[2], *t.shape[3:])


def _gather_weights(shards):
    parts = []
    for name, _, exact in SHARDED:
        a = shards[name].reshape(-1)
        a = lax.bitcast_convert_type(a, BF16).reshape(-1) if exact else a.astype(BF16)
        parts.append(_to_rows(a, ()))
    packed = _pad_total(jnp.concatenate(parts, axis=0), 0)
    gathered = _halves_merge("gather_merge", _chip_exchange("gather_chips", packed, True))
    out, r0 = {}, 0
    for (name, how, exact), part in zip(SHARDED, parts, strict=True):
        r = part.shape[0]
        shp = shards[name].shape
        n = shards[name].size * (2 if exact else 1)
        a = gathered[:, r0:r0 + r].reshape(4, -1)[:, :n]
        if exact:
            a = lax.bitcast_convert_type(a.reshape(4, -1, 2), F32)
        out[name] = _join_chips(a.reshape(4, *shp), how)
        r0 += r
    return out


def _reduce_grads(grads, shard_shapes):
    parts = []
    for name, how, _ in SHARDED:
        b = _split_chips(grads[name], how)
        parts.append(_to_rows(b.reshape(4, -1).astype(BF16), (4,)))
    for name in REPLICATED:
        a = _to_rows(grads[name].reshape(-1).astype(BF16), ())
        parts.append(jnp.broadcast_to(a[None], (4, *a.shape)))
    g = _pad_total(jnp.concatenate(parts, axis=1), 1)
    R = g.shape[1]
    h = R // 2
    c = lax.axis_index("c")
    theirs = _sibling_send_half("reduce_cores", g)
    mine = lax.dynamic_slice_in_dim(g, c * h, h, axis=1)
    chip_sum = _add2("reduce_add", mine.reshape(4 * h, PACK_W), theirs.reshape(4 * h, PACK_W), BF16).reshape(4, h, PACK_W)
    total = _halves_merge("reduce_merge", _sum4("reduce_sum", _chip_exchange("reduce_chips", chip_sum, False)))
    out, r0 = {}, 0
    names = [n for n, _, _ in SHARDED] + list(REPLICATED)
    for name, part in zip(names, parts, strict=True):
        r = part.shape[1]
        shp = shard_shapes[name]
        n = 1
        for s in shp:
            n *= s
        out[name] = total[r0:r0 + r].reshape(-1)[:n].reshape(shp)
        r0 += r
    return out


def _pad_heads(a, n, w, to):
    lead = a.shape[:-1]
    return jnp.pad(a.reshape(*lead, n, w), [(0, 0)] * (len(lead) + 1) + [(0, to - w)]).reshape(*lead, n * to)


def _unpad_heads(a, n, w, to):
    lead = a.shape[:-1]
    return a.reshape(*lead, n, to)[..., :w].reshape(*lead, n * w)


def _w_in_pad(w):
    return jnp.concatenate([_pad_heads(w[:, 0:256], 4, 64, HP), _pad_heads(w[:, 256:512], 4, 64, HP), w[:, 512:2176],
                            _pad_heads(w[:, 2176:2240], 1, 64, HP), w[:, 2240:2752]], axis=1)


def _w_in_unpad(g):
    return jnp.concatenate([_unpad_heads(g[:, Z_RQ:Z_RK], 4, 64, HP), _unpad_heads(g[:, Z_RK:Z_RV], 4, 64, HP),
                            g[:, Z_RV:Z_KR], g[:, Z_KR:Z_KR + 64], g[:, Z_MG:Z_END]], axis=1)


def _prep_layer_weights(full):
    f32 = lambda a: a.astype(F32)
    ev, od = [], []
    for i in range(DEPTH // 2 + DEPTH % 2):
        w_in_p = _w_in_pad(full["ev_w_in"][i])
        w_uq_p = _pad_heads(full["ev_w_uq"][i], MLA_HEADS, MLA_NOPE_DIM + MLA_ROPE_DIM, QW)
        w_ukv, w_out = full["ev_w_ukv"][i], full["ev_w_out"][i]
        ev.append(dict(w_in=w_in_p, w_in_t=w_in_p.T, w_uq=w_uq_p, w_uq_t=w_uq_p.T, w_ukv=w_ukv, w_ukv_t=w_ukv.T,
                       w_out=w_out, w_out_t=w_out.T, qg=f32(full["ev_q_norm_g"][i])[None], kvg=f32(full["ev_kv_norm_g"][i])[None],
                       lg=jnp.stack([jax.nn.log_sigmoid(f32(full["ev_dec_f"][i])), jax.nn.log_sigmoid(f32(full["ev_dec_b"][i]))])))
    for i in range(DEPTH // 2):
        w_in, w_out = full["od_w_in"][i], full["od_w_out"][i]
        dw = jnp.pad(f32(full["od_dw_w"][i]), ((0, 1), (0, 0)))
        od.append(dict(w_in=w_in, w_in_t=w_in.T, w_out=w_out, w_out_t=w_out.T, b_in=f32(full["od_b_in"][i])[None],
                       dw=dw, dw_flip=jnp.pad(f32(full["od_dw_w"][i])[::-1], ((0, 1), (0, 0))), dw_b=f32(full["od_dw_b"][i])[None],
                       ln_g=f32(full["od_ln_g"][i])[None], ln_b=f32(full["od_ln_b"][i])[None]))
    return ev, od


def _rope_tables(positions):
    inv_freq = ROPE_BASE ** (-jnp.arange(0, ROPE_DIM, 2, dtype=F32) / ROPE_DIM)
    ang = positions.astype(F32)[:, None] * inv_freq
    z = jnp.zeros((ang.shape[0], HP - ROPE_DIM), F32)
    return jnp.concatenate([jnp.cos(ang), jnp.cos(ang), z], axis=1), jnp.concatenate([jnp.sin(ang), jnp.sin(ang), z], axis=1)


def _even_fwd(x, mod, bias, pre_g, post_g, cos, sin, w):
    h = _pre_fwd(x, pre_g, mod, bias)
    z = _mm_nn("ev_in", h, w["w_in"])
    rq, rk, rv, qn, kvn, kr = _evprep_fwd(z, cos, sin, w["qg"], w["kvg"])
    qf = _mm_nn("ev_uq", qn, w["w_uq"])
    kv = _mm_nn("ev_ukv", kvn, w["w_ukv"])
    Q, K, V = _attnprep_fwd(qf, kv, kr, cos, sin)
    a, lse = _flash_fwd(Q, K, V)
    o2, states = _ret_fwd(w["lg"], rq, rk, rv)
    mix = _mix_fwd(o2, z, a)
    y = _mm_nn("ev_out", mix, w["w_out"])
    x_new = _post_fwd(x, y, post_g, mod, bias)
    return x_new, dict(x=x, h=h, z=z, rq=rq, rk=rk, rv=rv, qn=qn, kvn=kvn, Q=Q, K=K, V=V, a=a, lse=lse, o2=o2,
                       states=states, mix=mix, y=y)


def _even_bwd(dx, s, mod, bias, pre_g, post_g, cos, sin, w):
    dy, dpost_g, dgate = _post_bwd(s["y"], dx, post_g, mod, bias)
    dmix = _mm_nn("ev_out_dx", dy, w["w_out_t"])
    dw_out = _mm_tn("ev_out_dw", s["mix"], dy)
    do, drg, da, dmg = _mix_bwd(s["o2"], s["z"], s["a"], dmix)
    drq, drk, drv, dlg = _ret_bwd(w["lg"], s["rq"], s["rk"], s["rv"], do, s["states"])
    dQ, dK, dV = _flash_bwd(s["Q"], s["K"], s["V"], da, s["a"], s["lse"])
    dqf, dkv, dkr = _attnprep_bwd(dQ, dK, dV, cos, sin)
    dqn = _mm_nn("ev_uq_dx", dqf, w["w_uq_t"])
    dw_uq = _mm_tn("ev_uq_dw", s["qn"], dqf)
    dkvn = _mm_nn("ev_ukv_dx", dkv, w["w_ukv_t"])
    dw_ukv = _mm_tn("ev_ukv_dw", s["kvn"], dkv)
    dz, dqg, dkvg = _evprep_bwd(s["z"], cos, sin, w["qg"], w["kvg"], drq, drk, drv, drg, dqn, dkvn, dkr, dmg)
    dh = _mm_nn("ev_in_dx", dz, w["w_in_t"])
    dw_in = _mm_tn("ev_in_dw", s["h"], dz)
    dx_new, dpre_g, dscale, dshift = _pre_bwd(s["x"], dh, dx, pre_g, mod, bias)
    g = dict(ev_w_in=_w_in_unpad(dw_in), ev_w_uq=_unpad_heads(dw_uq, MLA_HEADS, MLA_NOPE_DIM + MLA_ROPE_DIM, QW),
             ev_w_ukv=dw_ukv, ev_w_out=dw_out, ev_q_norm_g=dqg[0], ev_kv_norm_g=dkvg[0],
             dlg_f=dlg[0, :RET_HEADS, 0], dlg_b=dlg[1, :RET_HEADS, 0])
    return dx_new, g, dpre_g[0], dpost_g[0], jnp.concatenate([dshift, dscale, dgate], axis=1)


def _odd_fwd(x, mod, bias, pre_g, post_g, w):
    h = _pre_fwd(x, pre_g, mod, bias)
    z = _mm_nn("od_in", h, w["w_in"])
    u = _glu_fwd(z, w["b_in"])
    uc = _conv_fwd(u, w["dw"], w["dw_b"])
    vv = _odmix_fwd(uc, z, w["b_in"], w["ln_g"], w["ln_b"])
    y = _mm_nn("od_out", vv, w["w_out"])
    x_new = _post_fwd(x, y, post_g, mod, bias)
    return x_new, dict(x=x, h=h, z=z, u=u, uc=uc, vv=vv, y=y)


def _odd_bwd(dx, s, mod, bias, pre_g, post_g, w):
    dy, dpost_g, dgate = _post_bwd(s["y"], dx, post_g, mod, bias)
    dvv = _mm_nn("od_out_dx", dy, w["w_out_t"])
    dw_out = _mm_tn("od_out_dw", s["vv"], dy)
    duc, dg, dbg, dln_g, dln_b = _odmix_bwd(s["uc"], s["z"], w["b_in"], w["ln_g"], w["ln_b"], dvv)
    du = _conv_fwd(duc, w["dw_flip"], jnp.zeros_like(w["dw_b"]))
    dwb = _conv_bwd_w(s["u"], duc)
    dz, dba, dbb = _glu_bwd(s["z"], w["b_in"], du, dg)
    dh = _mm_nn("od_in_dx", dz, w["w_in_t"])
    dw_in = _mm_tn("od_in_dw", s["h"], dz)
    dx_new, dpre_g, dscale, dshift = _pre_bwd(s["x"], dh, dx, pre_g, mod, bias)
    g = dict(od_w_in=dw_in, od_b_in=jnp.concatenate([dba, dbb, dbg], axis=1)[0], od_dw_w=dwb[:CONV_KERNEL], od_dw_b=dwb[CONV_KERNEL],
             od_ln_g=dln_g[0], od_ln_b=dln_b[0], od_w_out=dw_out)
    return dx_new, g, dpre_g[0], dpost_g[0], jnp.concatenate([dshift, dscale, dgate], axis=1)


def _local_step(x, c, positions, full, loss_target):
    D = D_MODEL
    cos, sin = _rope_tables(positions)
    ev, od = _prep_layer_weights(full)
    c16 = jnp.zeros((16, D), F32).at[0].set(c)
    c_act = _rowcall("silu_c", lambda rows, p: ([_silu(rows[0][...])], []), [c16], [], [(D, BF16)])[0]
    f32 = lambda a: a.astype(F32)
    mods = [_mm_nn("ada", c_act, full["ada_w"][l])[0:1] for l in range(DEPTH)]
    biases = [f32(full["ada_b"][l])[None] for l in range(DEPTH)]
    pre = [f32(full["pre_g"][l])[None] for l in range(DEPTH)]
    post = [f32(full["post_g"][l])[None] for l in range(DEPTH)]

    saved = []
    for l in range(DEPTH):
        if l % 2 == 0:
            x, s = _even_fwd(x, mods[l], biases[l], pre[l], post[l], cos, sin, ev[l // 2])
        else:
            x, s = _odd_fwd(x, mods[l], biases[l], pre[l], post[l], od[l // 2])
        saved.append(s)
    dx, loss = _loss_head(x, loss_target)

    per_layer = [None] * DEPTH
    dpre, dpost, dmod = [None] * DEPTH, [None] * DEPTH, [None] * DEPTH
    for l in reversed(range(DEPTH)):
        if l % 2 == 0:
            dx, per_layer[l], dpre[l], dpost[l], dmod[l] = _even_bwd(dx, saved[l], mods[l], biases[l], pre[l], post[l], cos, sin, ev[l // 2])
        else:
            dx, per_layer[l], dpre[l], dpost[l], dmod[l] = _odd_bwd(dx, saved[l], mods[l], biases[l], pre[l], post[l], od[l // 2])

    grads = {"pre_g": jnp.stack(dpre), "post_g": jnp.stack(dpost), "ada_b": jnp.concatenate(dmod, axis=0)}
    dmod16 = [jnp.pad(m, ((0, 15), (0, 0))).astype(BF16) for m in dmod]
    grads["ada_w"] = jnp.stack([_mm_tn("ada_dw", c_act, dmod16[l]) for l in range(DEPTH)])
    evs = [per_layer[l] for l in range(0, DEPTH, 2)]
    ods = [per_layer[l] for l in range(1, DEPTH, 2)]
    for name in ("ev_w_in", "ev_w_uq", "ev_w_ukv", "ev_w_out", "ev_q_norm_g", "ev_kv_norm_g"):
        grads[name] = jnp.stack([g[name] for g in evs])
    for name in ("od_w_in", "od_b_in", "od_dw_w", "od_dw_b", "od_ln_g", "od_ln_b", "od_w_out"):
        grads[name] = jnp.stack([g[name] for g in ods])
    grads["ev_dec_f"] = jnp.stack([g["dlg_f"] for g in evs]) * jax.nn.sigmoid(-f32(full["ev_dec_f"]))
    grads["ev_dec_b"] = jnp.stack([g["dlg_b"] for g in evs]) * jax.nn.sigmoid(-f32(full["ev_dec_b"]))
    return loss[0, 0], dx, grads


def kernel(x, c, positions, ada_w, ada_b, pre_g, post_g, ev_w_in, ev_dec_f, ev_dec_b, ev_q_norm_g, ev_w_uq, ev_kv_norm_g, ev_w_ukv, ev_w_out, od_w_in, od_b_in, od_dw_w, od_dw_b, od_ln_g, od_ln_b, od_w_out, loss_target, m_ada_w, m_ada_b, m_pre_g, m_post_g, m_ev_w_in, m_ev_dec_f, m_ev_dec_b, m_ev_q_norm_g, m_ev_w_uq, m_ev_kv_norm_g, m_ev_w_ukv, m_ev_w_out, m_od_w_in, m_od_b_in, m_od_dw_w, m_od_dw_b, m_od_ln_g, m_od_ln_b, m_od_w_out, v_ada_w, v_ada_b, v_pre_g, v_post_g, v_ev_w_in, v_ev_dec_f, v_ev_dec_b, v_ev_q_norm_g, v_ev_w_uq, v_ev_kv_norm_g, v_ev_w_ukv, v_ev_w_out, v_od_w_in, v_od_b_in, v_od_dw_w, v_od_dw_b, v_od_ln_g, v_od_ln_b, v_od_w_out):
    w = dict(ada_w=ada_w, ada_b=ada_b, pre_g=pre_g, post_g=post_g, ev_w_in=ev_w_in, ev_dec_f=ev_dec_f, ev_dec_b=ev_dec_b,
             ev_q_norm_g=ev_q_norm_g, ev_w_uq=ev_w_uq, ev_kv_norm_g=ev_kv_norm_g, ev_w_ukv=ev_w_ukv, ev_w_out=ev_w_out,
             od_w_in=od_w_in, od_b_in=od_b_in, od_dw_w=od_dw_w, od_dw_b=od_dw_b, od_ln_g=od_ln_g, od_ln_b=od_ln_b, od_w_out=od_w_out)
    m = dict(ada_w=m_ada_w, ada_b=m_ada_b, pre_g=m_pre_g, post_g=m_post_g, ev_w_in=m_ev_w_in, ev_dec_f=m_ev_dec_f, ev_dec_b=m_ev_dec_b,
             ev_q_norm_g=m_ev_q_norm_g, ev_w_uq=m_ev_w_uq, ev_kv_norm_g=m_ev_kv_norm_g, ev_w_ukv=m_ev_w_ukv, ev_w_out=m_ev_w_out,
             od_w_in=m_od_w_in, od_b_in=m_od_b_in, od_dw_w=m_od_dw_w, od_dw_b=m_od_dw_b, od_ln_g=m_od_ln_g, od_ln_b=m_od_ln_b, od_w_out=m_od_w_out)
    v = dict(ada_w=v_ada_w, ada_b=v_ada_b, pre_g=v_pre_g, post_g=v_post_g, ev_w_in=v_ev_w_in, ev_dec_f=v_ev_dec_f, ev_dec_b=v_ev_dec_b,
             ev_q_norm_g=v_ev_q_norm_g, ev_w_uq=v_ev_w_uq, ev_kv_norm_g=v_ev_kv_norm_g, ev_w_ukv=v_ev_w_ukv, ev_w_out=v_ev_w_out,
             od_w_in=v_od_w_in, od_b_in=v_od_b_in, od_dw_w=v_od_dw_w, od_dw_b=v_od_dw_b, od_ln_g=v_od_ln_g, od_ln_b=v_od_ln_b, od_w_out=v_od_w_out)

    full = _gather_weights({name: w[name] for name, _, _ in SHARDED})
    for name in REPLICATED:
        full[name] = w[name]
    loss_local, grad_x, grads = _local_step(x[0], c[0], positions[0], full, loss_target[0])
    loss = lax.psum(loss_local, ("x", "y", "c"))
    g = _reduce_grads(grads, {name: w[name].shape for name in WEIGHTS})
    delta, new_m, new_v = {}, {}, {}
    for name in WEIGHTS:
        delta[name], new_m[name], new_v[name] = _adamw(w[name], g[name], m[name], v[name])
    return (loss, grad_x[None], *[g[n] for n in WEIGHTS], *[delta[n] for n in WEIGHTS],
            *[new_m[n] for n in WEIGHTS], *[new_v[n] for n in WEIGHTS])
```

```python
import functools

import jax
import jax.numpy as jnp
from jax import lax
from jax.experimental import pallas as pl
from jax.experimental.pallas import tpu as pltpu

F32 = jnp.float32
BF16 = jnp.bfloat16

D_MODEL = 1024
DEPTH = 4
RET_HEADS = 4
RET_QK_DIM = 64
RET_V_DIM = 128
MLA_HEADS = 4
MLA_Q_RANK = 384
MLA_KV_RANK = 256
MLA_NOPE_DIM = 128
MLA_ROPE_DIM = 64
MLA_V_DIM = 128
ROPE_DIM = 64
ROPE_BASE = 10000.0
CONV_KERNEL = 31
EPS = 1e-6
ADAM_LR, ADAM_B1, ADAM_B2, ADAM_EPS, ADAM_WD, ADAM_STEP = 0.001, 0.9, 0.999, 1e-08, 0.01, 10

LANES = 128
HP = 128
Z_RQ, Z_RK, Z_RV, Z_RG, Z_CQ, Z_CKV, Z_KR, Z_MG, Z_END = 0, 512, 1024, 1536, 2048, 2432, 2688, 2816, 3328
V7X_VMEM_BYTES = 64 * 1024 * 1024
VMEM_CAP = V7X_VMEM_BYTES - 8 * 1024 * 1024

MESH = pl.DeviceIdType.MESH


def _nbytes(shape, dtype):
    n = 1
    for s in shape:
        n *= s
    return n * jnp.dtype(dtype).itemsize


def _params(sem, block_bytes):
    limit = min(VMEM_CAP, max(32 * 1024 * 1024, 2 * block_bytes + 16 * 1024 * 1024))
    return pltpu.CompilerParams(dimension_semantics=sem, vmem_limit_bytes=limit)


def _rowcall(name, fn, rows, params, outs, accs=(), ts=256):
    S = rows[0].shape[0]
    ts = min(ts, S)
    assert S % ts == 0, (name, S, ts)
    nr, npar, no, na = len(rows), len(params), len(outs), len(accs)

    def body(*refs):
        row_refs = refs[:nr]
        pvals = [r[...] for r in refs[nr:nr + npar]]
        out_refs = refs[nr + npar:nr + npar + no]
        acc_refs = refs[nr + npar + no:]
        ovals, avals = fn(row_refs, pvals)
        for r, v in zip(out_refs, ovals, strict=True):
            r[...] = v.astype(r.dtype)
        if na:
            @pl.when(pl.program_id(0) == 0)
            def _():
                for r in acc_refs:
                    r[...] = jnp.zeros_like(r)
            for r, v in zip(acc_refs, avals, strict=True):
                r[...] += v

    in_specs = [pl.BlockSpec((ts, a.shape[1]), lambda i: (i, 0)) for a in rows]
    in_specs += [pl.BlockSpec(p.shape, lambda i: (0, 0)) for p in params]
    out_specs = [pl.BlockSpec((ts, w), lambda i: (i, 0)) for w, _ in outs]
    out_specs += [pl.BlockSpec(s, lambda i: (0, 0)) for s in accs]
    out_shape = [jax.ShapeDtypeStruct((S, w), dt) for w, dt in outs]
    out_shape += [jax.ShapeDtypeStruct(s, F32) for s in accs]
    nb = sum(_nbytes((ts, a.shape[1]), a.dtype) for a in rows) + sum(_nbytes((ts, w), dt) for w, dt in outs)
    nb += sum(_nbytes(p.shape, p.dtype) for p in params) + sum(_nbytes(s, F32) for s in accs)
    res = pl.pallas_call(
        body, name=name, grid=(S // ts,), in_specs=in_specs, out_specs=out_specs, out_shape=out_shape,
        compiler_params=_params(("arbitrary",) if na else ("parallel",), 3 * nb),
    )(*rows, *params)
    return res


def _silu(x):
    return x * jax.nn.sigmoid(x)


def _rms(x, g):
    return x * lax.rsqrt(jnp.mean(x * x, axis=-1, keepdims=True) + EPS) * g


def _rot(x):
    lane = lax.broadcasted_iota(jnp.int32, x.shape, 1)
    return jnp.where(lane < ROPE_DIM // 2, -pltpu.roll(x, LANES - ROPE_DIM // 2, 1), pltpu.roll(x, ROPE_DIM // 2, 1))


def _rope(x, cos, sin):
    return x * cos + _rot(x) * sin


def _rope_t(dy, cos, sin):
    return dy * cos - _rot(dy) * sin


def _groups(ref, start, n):
    return [ref[:, start + HP * h:start + HP * (h + 1)] for h in range(n)]


def _pre_math(x, g, m_scale, b_scale, m_shift, b_shift):
    return _rms(x, g) * (1.0 + (m_scale + b_scale)) + (m_shift + b_shift)


def _pre_fwd(x, g, mod, bias):
    D = D_MODEL

    def fn(rows, p):
        g_, mod_, b_ = p
        h = _pre_math(rows[0][...], g_, mod_[:, D:2 * D], b_[:, D:2 * D], mod_[:, :D], b_[:, :D])
        return [h], []

    return _rowcall("pre_fwd", fn, [x], [g, mod, bias], [(D, BF16)])[0]


def _pre_bwd(x, dh, dx_res, g, mod, bias):
    D = D_MODEL

    def fn(rows, p):
        g_, mod_, b_ = p
        xv = rows[0][...]
        _, vjp = jax.vjp(_pre_math, xv, g_, mod_[:, D:2 * D], b_[:, D:2 * D], mod_[:, :D], b_[:, :D])
        dx, dg, dsc, _, dsh, _ = vjp(rows[1][...].astype(F32))
        return [dx + rows[2][...]], [dg, dsc, dsh]

    return _rowcall("pre_bwd", fn, [x, dh, dx_res], [g, mod, bias], [(D, F32)], [(1, D)] * 3)


def _post_math(x, y, g, m_gate, b_gate):
    return x + (m_gate + b_gate) * _rms(y, g)


def _post_fwd(x, y, g, mod, bias):
    D = D_MODEL

    def fn(rows, p):
        g_, mod_, b_ = p
        return [_post_math(rows[0][...], rows[1][...], g_, mod_[:, 2 * D:], b_[:, 2 * D:])], []

    return _rowcall("post_fwd", fn, [x, y], [g, mod, bias], [(D, F32)])[0]


def _post_bwd(y, dxo, g, mod, bias):
    D = D_MODEL

    def fn(rows, p):
        g_, mod_, b_ = p
        yv = rows[0][...]
        _, vjp = jax.vjp(lambda y_, g2, mg: _post_math(0.0, y_, g2, mg, b_[:, 2 * D:]), yv, g_, mod_[:, 2 * D:])
        dy, dg, dgate = vjp(rows[1][...])
        return [dy], [dg, dgate]

    return _rowcall("post_bwd", fn, [y, dxo], [g, mod, bias], [(D, BF16)], [(1, D)] * 2)


def _evprep_fwd(z, cos, sin, qg, kvg):
    def fn(rows, p):
        z_, c_, s_ = rows
        qg_, kvg_ = p
        cos_, sin_ = c_[...], s_[...]
        rq = jnp.concatenate([_rope(v[...], cos_, sin_) for v in _groups(z_, Z_RQ, RET_HEADS)], axis=1)
        rk = jnp.concatenate([_rope(v[...], cos_, sin_) for v in _groups(z_, Z_RK, RET_HEADS)], axis=1)
        rk = rk * (RET_QK_DIM ** -0.5)
        rv = z_[:, Z_RV:Z_RG]
        qn = _rms(z_[:, Z_CQ:Z_CKV], qg_)
        kvn = _rms(z_[:, Z_CKV:Z_KR], kvg_)
        kr = _rope(z_[:, Z_KR:Z_MG], cos_, sin_)
        return [rq, rk, rv, qn, kvn, kr], []

    return _rowcall("evprep_fwd", fn, [z, cos, sin], [qg, kvg],
                    [(512, BF16), (512, BF16), (512, BF16), (MLA_Q_RANK, BF16), (MLA_KV_RANK, BF16), (HP, F32)])


def _evprep_bwd(z, cos, sin, qg, kvg, drq, drk, drv, drg, dqn, dkvn, dkr, dmg):
    def fn(rows, p):
        z_, c_, s_, drq_, drk_, drv_, drg_, dqn_, dkvn_, dkr_, dmg_ = rows
        qg_, kvg_ = p
        cos_, sin_ = c_[...], s_[...]
        parts = []
        for h in range(RET_HEADS):
            parts.append(_rope_t(drq_[:, HP * h:HP * (h + 1)] + drq_[:, 512 + HP * h:512 + HP * (h + 1)], cos_, sin_))
        for h in range(RET_HEADS):
            t = drk_[:, HP * h:HP * (h + 1)] + drk_[:, 512 + HP * h:512 + HP * (h + 1)]
            parts.append(_rope_t(t, cos_, sin_) * (RET_QK_DIM ** -0.5))
        parts.append(drv_[:, :512] + drv_[:, 512:])
        parts.append(drg_[...])
        _, vq = jax.vjp(_rms, z_[:, Z_CQ:Z_CKV], qg_)
        dcq, dqg = vq(dqn_[...])
        _, vkv = jax.vjp(_rms, z_[:, Z_CKV:Z_KR], kvg_)
        dckv, dkvg = vkv(dkvn_[...])
        parts += [dcq, dckv, _rope_t(dkr_[...], cos_, sin_), dmg_[...]]
        return [jnp.concatenate([v.astype(BF16) for v in parts], axis=1)], [dqg, dkvg]

    rows = [z, cos, sin, drq, drk, drv, drg, dqn, dkvn, dkr, dmg]
    return _rowcall("evprep_bwd", fn, rows, [qg, kvg], [(Z_END, BF16)], [(1, MLA_Q_RANK), (1, MLA_KV_RANK)], ts=128)


ATT_SCALE = (MLA_NOPE_DIM + MLA_ROPE_DIM) ** -0.5
LOG2E = 1.4426950408889634
LN2 = 0.6931471805599453


def _attnprep_fwd(qf, kv, kr, cos, sin):
    qscale = ATT_SCALE * LOG2E

    def fn(rows, p):
        q_, kv_, kr_, c_, s_ = rows
        cos_, sin_ = c_[...], s_[...]
        krv = kr_[...]
        qs, ks, vs = [], [], []
        for h in range(MLA_HEADS):
            b = 2 * HP * h
            qs += [q_[:, b:b + HP] * qscale, _rope(q_[:, b + HP:b + 2 * HP], cos_, sin_) * qscale]
            ks += [kv_[:, b:b + HP], krv]
            vs += [kv_[:, b + HP:b + 2 * HP]]
        return [jnp.concatenate(qs, axis=1), jnp.concatenate(ks, axis=1), jnp.concatenate(vs, axis=1)], []

    return _rowcall("attnprep_fwd", fn, [qf, kv, kr, cos, sin], [], [(1024, BF16), (1024, BF16), (512, BF16)])


def _attnprep_bwd(dQ, dK, dV, cos, sin):
    def fn(rows, p):
        dq_, dk_, dv_, c_, s_ = rows
        cos_, sin_ = c_[...], s_[...]
        dqs, dkvs = [], []
        dkr = None
        for h in range(MLA_HEADS):
            b = 2 * HP * h
            dqs += [dq_[:, b:b + HP] * ATT_SCALE, _rope_t(dq_[:, b + HP:b + 2 * HP], cos_, sin_) * ATT_SCALE]
            dkvs += [dk_[:, b:b + HP] * LN2, dv_[:, HP * h:HP * (h + 1)]]
            t = dk_[:, b + HP:b + 2 * HP]
            dkr = t if dkr is None else dkr + t
        return [jnp.concatenate(dqs, axis=1), jnp.concatenate(dkvs, axis=1), dkr * LN2], []

    return _rowcall("attnprep_bwd", fn, [dQ, dK, dV, cos, sin], [], [(1024, BF16), (1024, BF16), (HP, F32)])


def _mix_math(o, rg, a, mg):
    outs = []
    for h in range(RET_HEADS):
        oh = o[:, HP * h:HP * (h + 1)]
        mu = jnp.mean(oh, axis=-1, keepdims=True)
        var = jnp.mean(jnp.square(oh - mu), axis=-1, keepdims=True)
        outs.append((oh - mu) * lax.rsqrt(var + EPS))
    ret = jnp.concatenate(outs, axis=1) * _silu(rg)
    return jnp.concatenate([ret, a * _silu(mg)], axis=1)


def _mix_fwd(o2, z, a):
    def fn(rows, p):
        o_, z_, a_ = rows
        return [_mix_math(o_[:, :512] + o_[:, 512:], z_[:, Z_RG:Z_CQ], a_[...], z_[:, Z_MG:Z_END])], []

    return _rowcall("mix_fwd", fn, [o2, z, a], [], [(1024, BF16)])[0]


def _mix_bwd(o2, z, a, dmix):
    def fn(rows, p):
        o_, z_, a_, dm_ = rows
        _, vjp = jax.vjp(_mix_math, o_[:, :512] + o_[:, 512:], z_[:, Z_RG:Z_CQ], a_[...], z_[:, Z_MG:Z_END])
        return list(vjp(dm_[...].astype(F32))), []

    return _rowcall("mix_bwd", fn, [o2, z, a, dmix], [], [(512, BF16), (512, F32), (512, BF16), (512, F32)])


def _glu_math(a, b, ba, bb):
    return (a + ba) * jax.nn.sigmoid(b + bb)


def _glu_fwd(z, bin_):
    D = D_MODEL

    def fn(rows, p):
        z_, = rows
        b_, = p
        return [_glu_math(z_[:, :D], z_[:, D:2 * D], b_[:, :D], b_[:, D:2 * D])], []

    return _rowcall("glu_fwd", fn, [z], [bin_], [(D, F32)])[0]


def _odmix_math(uc, g, bg, ln_g, ln_b):
    mu = jnp.mean(uc, axis=-1, keepdims=True)
    var = jnp.mean(jnp.square(uc - mu), axis=-1, keepdims=True)
    y = (uc - mu) * lax.rsqrt(var + EPS) * ln_g + ln_b
    return _silu(y) * _silu(g + bg)


def _odmix_fwd(uc, z, bin_, ln_g, ln_b):
    D = D_MODEL

    def fn(rows, p):
        uc_, z_ = rows
        b_, g_, lb_ = p
        return [_odmix_math(uc_[...], z_[:, 2 * D:], b_[:, 2 * D:], g_, lb_)], []

    return _rowcall("odmix_fwd", fn, [uc, z], [bin_, ln_g, ln_b], [(D, BF16)])[0]


def _odmix_bwd(uc, z, bin_, ln_g, ln_b, dv):
    D = D_MODEL

    def fn(rows, p):
        uc_, z_, dv_ = rows
        b_, g_, lb_ = p
        _, vjp = jax.vjp(_odmix_math, uc_[...], z_[:, 2 * D:], b_[:, 2 * D:], g_, lb_)
        duc, dg, dbg, dlg, dlb = vjp(dv_[...])
        return [duc, dg], [dbg, dlg, dlb]

    return _rowcall("odmix_bwd", fn, [uc, z, dv], [bin_, ln_g, ln_b], [(D, F32), (D, BF16)], [(1, D)] * 3)


def _glu_bwd(z, bin_, du, dg):
    D = D_MODEL

    def fn(rows, p):
        z_, du_, dg_ = rows
        b_, = p
        _, vjp = jax.vjp(_glu_math, z_[:, :D], z_[:, D:2 * D], b_[:, :D], b_[:, D:2 * D])
        da, db, dba, dbb = vjp(du_[...])
        return [jnp.concatenate([da.astype(BF16), db.astype(BF16), dg_[...]], axis=1)], [dba, dbb]

    return _rowcall("glu_bwd", fn, [z, du, dg], [bin_], [(3 * D, BF16)], [(1, D)] * 2)


def _loss_head(x, tgt):
    D = D_MODEL

    def fn(rows, p):
        err = rows[0][...] - rows[1][...]
        part = 0.5 * jnp.sum(jnp.mean(err * err, axis=-1, keepdims=True), axis=0, keepdims=True)
        return [err * (1.0 / D)], [jnp.broadcast_to(part, (1, LANES))]

    return _rowcall("loss_head", fn, [x, tgt], [], [(D, F32)], [(1, LANES)])


def _tile(n, cap):
    if n <= cap:
        return n
    best = None
    for t in range(LANES, cap + 1, LANES):
        if n % t == 0:
            best = t
    assert best is not None, (n, cap)
    return best


MM_TN_CAP = 1792
NT_DIMS = (((1,), (1,)), ((), ()))
TN_DIMS = (((0,), (0,)), ((), ()))


def _mm_nn(name, a, b, out_dtype=F32, tm=512, tn_cap=MM_TN_CAP):
    M, K = a.shape
    N = b.shape[1]
    tm = min(tm, M)
    tn = _tile(N, tn_cap)

    def body(a_ref, b_ref, o_ref):
        o_ref[...] = jnp.dot(a_ref[...], b_ref[...], preferred_element_type=F32).astype(o_ref.dtype)

    nb = _nbytes((tm, K), a.dtype) + _nbytes((K, tn), b.dtype) + _nbytes((tm, tn), out_dtype) + _nbytes((tm, tn), F32)
    return pl.pallas_call(
        body, name=name, grid=(N // tn, M // tm),
        in_specs=[pl.BlockSpec((tm, K), lambda j, i: (i, 0)), pl.BlockSpec((K, tn), lambda j, i: (0, j))],
        out_specs=pl.BlockSpec((tm, tn), lambda j, i: (i, j)),
        out_shape=jax.ShapeDtypeStruct((M, N), out_dtype),
        compiler_params=_params(("parallel", "parallel"), nb),
    )(a, b)


def _mm_nt(name, a, b, out_dtype=F32, tm=512, tn_cap=1024):
    M, K = a.shape
    N = b.shape[0]
    tm = min(tm, M)
    tn = _tile(N, tn_cap)

    def body(a_ref, b_ref, o_ref):
        o_ref[...] = lax.dot_general(a_ref[...], b_ref[...], NT_DIMS, preferred_element_type=F32).astype(o_ref.dtype)

    nb = _nbytes((tm, K), a.dtype) + _nbytes((tn, K), b.dtype) + _nbytes((tm, tn), out_dtype) + _nbytes((tm, tn), F32)
    return pl.pallas_call(
        body, name=name, grid=(N // tn, M // tm),
        in_specs=[pl.BlockSpec((tm, K), lambda j, i: (i, 0)), pl.BlockSpec((tn, K), lambda j, i: (j, 0))],
        out_specs=pl.BlockSpec((tm, tn), lambda j, i: (i, j)),
        out_shape=jax.ShapeDtypeStruct((M, N), out_dtype),
        compiler_params=_params(("parallel", "parallel"), nb),
    )(a, b)


def _mm_tn(name, a, b, ts=512, tm_cap=512, tn_cap=MM_TN_CAP):
    S, M = a.shape
    N = b.shape[1]
    ts = min(ts, S)
    tm = _tile(M, tm_cap)
    tn = _tile(N, tn_cap)

    def body(a_ref, b_ref, o_ref):
        @pl.when(pl.program_id(2) == 0)
        def _():
            o_ref[...] = jnp.zeros_like(o_ref)

        o_ref[...] += lax.dot_general(a_ref[...], b_ref[...], (((0,), (0,)), ((), ())), preferred_element_type=F32)

    nb = _nbytes((ts, tm), a.dtype) + _nbytes((ts, tn), b.dtype) + 2 * _nbytes((tm, tn), F32)
    return pl.pallas_call(
        body, name=name, grid=(M // tm, N // tn, S // ts),
        in_specs=[pl.BlockSpec((ts, tm), lambda i, j, s: (s, i)), pl.BlockSpec((ts, tn), lambda i, j, s: (s, j))],
        out_specs=pl.BlockSpec((tm, tn), lambda i, j, s: (i, j)),
        out_shape=jax.ShapeDtypeStruct((M, N), F32),
        compiler_params=_params(("parallel", "parallel", "arbitrary"), nb),
    )(a, b)


ATT_TQ = 512
ATT_TK = 512
QW = 2 * HP


def _kv_tiles(V, tk):
    S = V.shape[0]
    return V.reshape(S // tk, tk, MLA_HEADS, HP).transpose(2, 0, 3, 1)


def _flash_fwd(Q, K, V):
    S = Q.shape[0]
    H = MLA_HEADS
    tq, tk = min(ATT_TQ, S), min(ATT_TK, S)
    nk = S // tk
    VT = _kv_tiles(V, tk)

    def body(q_ref, k_ref, vt_ref, o_ref, lse_ref, m_s, l_s, acc_s, s_a, s_b):
        m_s[...] = jnp.full(m_s.shape, -jnp.inf, F32)
        l_s[...] = jnp.zeros(l_s.shape, F32)
        acc_s[...] = jnp.zeros(acc_s.shape, F32)

        def scores(j):
            k = k_ref[pl.ds(pl.multiple_of(j * tk, tk), tk), :]
            return lax.dot_general(k, q_ref[...], NT_DIMS, preferred_element_type=F32)

        def consume(st, j):
            m_prev = m_s[...]
            m_new = jnp.maximum(m_prev, jnp.max(st, axis=0, keepdims=True))
            alpha = jnp.exp2(m_prev - m_new)
            pt = jnp.exp2(st - m_new)
            l_s[...] = alpha * l_s[...] + jnp.sum(pt, axis=0, keepdims=True)
            acc_s[...] = alpha * acc_s[...] + jnp.dot(vt_ref[0, j], pt.astype(BF16), preferred_element_type=F32)
            m_s[...] = m_new

        if nk % 2:
            def step(j, carry):
                consume(scores(j), j)
                return carry

            lax.fori_loop(0, nk, step, 0)
        else:
            s_a[...] = scores(0)

            def pair(jj, carry):
                j0 = 2 * jj
                s_b[...] = scores(j0 + 1)
                consume(s_a[...], j0)
                s_a[...] = scores(jnp.minimum(j0 + 2, nk - 1))
                consume(s_b[...], j0 + 1)
                return carry

            lax.fori_loop(0, nk // 2, pair, 0)
        l = l_s[...]
        o_ref[...] = (acc_s[...] * (1.0 / l)).T
        lse_ref[0] = m_s[...] + jnp.log2(l)

    nb = (_nbytes((tq, QW), BF16) + _nbytes((S, QW), BF16) + _nbytes((S, HP), BF16) + 3 * _nbytes((tq, HP), F32)
          + 4 * _nbytes((tq, tk), F32))
    return pl.pallas_call(
        body, name="flash_fwd", grid=(H, S // tq),
        in_specs=[pl.BlockSpec((tq, QW), lambda h, i: (i, h)), pl.BlockSpec((S, QW), lambda h, i: (0, h)),
                  pl.BlockSpec((1, nk, HP, tk), lambda h, i: (h, 0, 0, 0))],
        out_specs=[pl.BlockSpec((tq, HP), lambda h, i: (i, h)), pl.BlockSpec((1, 1, tq), lambda h, i: (h, 0, i))],
        out_shape=[jax.ShapeDtypeStruct((S, H * HP), F32), jax.ShapeDtypeStruct((H, 1, S), F32)],
        scratch_shapes=[pltpu.VMEM((1, tq), F32), pltpu.VMEM((1, tq), F32), pltpu.VMEM((HP, tq), F32),
                        pltpu.VMEM((tk, tq), F32), pltpu.VMEM((tk, tq), F32)],
        compiler_params=_params(("parallel", "parallel"), nb),
    )(Q, K, VT)


def _flash_bwd(Q, K, V, dO, O, lse):
    S = Q.shape[0]
    H = MLA_HEADS
    tq, tk = min(ATT_TQ, S), min(ATT_TK, S)
    nk = S // tk

    def body(q_ref, do_ref, o_ref, lse_ref, k_ref, v_ref, dq_ref, dk_ref, dv_ref, dq_s):
        @pl.when(pl.program_id(1) == 0)
        def _():
            dk_ref[...] = jnp.zeros(dk_ref.shape, F32)
            dv_ref[...] = jnp.zeros(dv_ref.shape, F32)

        delta = jnp.sum((do_ref[...].astype(F32) * o_ref[...]).T, axis=0, keepdims=True)
        lse = lse_ref[0]
        dq_s[...] = jnp.zeros(dq_s.shape, F32)

        def rows(j):
            return pl.ds(pl.multiple_of(j * tk, tk), tk)

        def scores(j):
            st = lax.dot_general(k_ref[rows(j), :], q_ref[...], NT_DIMS, preferred_element_type=F32)
            dpt = lax.dot_general(v_ref[rows(j), :], do_ref[...], NT_DIMS, preferred_element_type=F32)
            return st, dpt

        def consume(st, dpt, j):
            pt = jnp.exp2(st - lse)
            dst = (pt * (dpt - delta)).astype(BF16)
            dv_ref[rows(j), :] += jnp.dot(pt.astype(BF16), do_ref[...], preferred_element_type=F32)
            dk_ref[rows(j), :] += jnp.dot(dst, q_ref[...], preferred_element_type=F32)
            dq_s[...] += lax.dot_general(dst, k_ref[rows(j), :], TN_DIMS, preferred_element_type=F32)

        def step(j, carry):
            consume(*scores(j), j)
            return carry

        lax.fori_loop(0, nk, step, 0)
        dq_ref[...] = dq_s[...]

    nb = (_nbytes((tq, QW), BF16) + _nbytes((tq, HP), BF16) + _nbytes((tq, HP), F32) + _nbytes((S, QW), BF16)
          + _nbytes((S, HP), BF16) + 2 * _nbytes((tq, QW), F32) + _nbytes((S, QW), F32) + _nbytes((S, HP), F32))
    return pl.pallas_call(
        body, name="flash_bwd", grid=(H, S // tq),
        in_specs=[pl.BlockSpec((tq, QW), lambda h, i: (i, h)), pl.BlockSpec((tq, HP), lambda h, i: (i, h)),
                  pl.BlockSpec((tq, HP), lambda h, i: (i, h)), pl.BlockSpec((1, 1, tq), lambda h, i: (h, 0, i)),
                  pl.BlockSpec((S, QW), lambda h, i: (0, h)), pl.BlockSpec((S, HP), lambda h, i: (0, h))],
        out_specs=[pl.BlockSpec((tq, QW), lambda h, i: (i, h)), pl.BlockSpec((S, QW), lambda h, i: (0, h)),
                   pl.BlockSpec((S, HP), lambda h, i: (0, h))],
        out_shape=[jax.ShapeDtypeStruct((S, H * QW), F32), jax.ShapeDtypeStruct((S, H * QW), F32),
                   jax.ShapeDtypeStruct((S, H * HP), F32)],
        scratch_shapes=[pltpu.VMEM((tq, QW), F32)],
        compiler_params=_params(("parallel", "arbitrary"), nb),
    )(Q, dO, O, lse, K, V)


RET_CHUNK = 256


def _ret_tables(d, lg_ref, h, C):
    ii = lax.broadcasted_iota(jnp.int32, (C, C), 0).astype(F32)
    jj = lax.broadcasted_iota(jnp.int32, (C, C), 1).astype(F32)
    ci = lax.broadcasted_iota(jnp.int32, (C, 1), 0).astype(F32)
    fwd = d == 0
    dist = jnp.where(fwd, ii - jj, jj - ii)
    mask = dist >= jnp.where(fwd, 0.0, 1.0)
    dist = jnp.maximum(dist, 0.0)
    qpos = jnp.where(fwd, ci + 1.0, C - ci)
    kpos = jnp.where(fwd, C - 1.0 - ci, ci)
    lg = lg_ref[d, h]
    D = jnp.where(mask, jnp.exp(lg * dist), 0.0)
    return D, jnp.exp(lg * qpos), jnp.exp(lg * kpos), jnp.exp(lg * C), dist, qpos, kpos


def _ret_fwd(lg, q, k, v):
    S = q.shape[0]
    C = min(RET_CHUNK, S)
    N = S // C
    H = RET_HEADS

    def chunk(d, n):
        return jnp.where(d == 0, n, N - 1 - n)

    def body(lg_ref, q_ref, k_ref, v_ref, o_ref, st_ref, state):
        d, n = pl.program_id(0), pl.program_id(1)

        @pl.when(n == 0)
        def _():
            state[...] = jnp.zeros_like(state)

        for h in range(H):
            sl = slice(HP * h, HP * (h + 1))
            D, qw, kw, gc, _, _, _ = _ret_tables(d, lg_ref, h, C)
            qh, kh, vh = q_ref[:, sl], k_ref[:, sl], v_ref[:, sl]
            st = state[sl, :]
            sm = lax.dot_general(qh, kh, (((1,), (1,)), ((), ())), preferred_element_type=F32) * D
            inner = jnp.dot(sm.astype(BF16), vh, preferred_element_type=F32)
            cross = qw * jnp.dot(qh, st.astype(BF16), preferred_element_type=F32)
            o_ref[:, sl] = inner + cross
            st_ref[0, 0, sl, :] = st
            kvn = lax.dot_general((kh.astype(F32) * kw).astype(BF16), vh, (((0,), (0,)), ((), ())), preferred_element_type=F32)
            state[sl, :] = gc * st + kvn

    nb = 3 * _nbytes((C, 512), BF16) + _nbytes((C, 512), F32) + 2 * _nbytes((512, HP), F32) + 8 * _nbytes((C, C), F32)
    return pl.pallas_call(
        body, name="ret_fwd", grid=(2, N),
        in_specs=[pl.BlockSpec(memory_space=pltpu.SMEM)] + [pl.BlockSpec((C, 512), lambda d, n: (chunk(d, n), 0))] * 3,
        out_specs=[pl.BlockSpec((C, 512), lambda d, n: (chunk(d, n), d)),
                   pl.BlockSpec((1, 1, 512, HP), lambda d, n: (d, chunk(d, n), 0, 0))],
        out_shape=[jax.ShapeDtypeStruct((S, 1024), F32), jax.ShapeDtypeStruct((2, N, 512, HP), F32)],
        scratch_shapes=[pltpu.VMEM((512, HP), F32)],
        compiler_params=_params(("arbitrary", "arbitrary"), nb),
    )(lg, q, k, v)


def _ret_bwd(lg, q, k, v, do, states):
    S = q.shape[0]
    C = min(RET_CHUNK, S)
    N = S // C
    H = RET_HEADS

    def chunk(d, n):
        return jnp.where(d == 0, N - 1 - n, n)

    def body(lg_ref, q_ref, k_ref, v_ref, do_ref, st_ref, dq_ref, dk_ref, dv_ref, dlg_ref, G, accA, accB, accC):
        d, n = pl.program_id(0), pl.program_id(1)

        @pl.when(n == 0)
        def _():
            G[...] = jnp.zeros_like(G)
            accA[...] = jnp.zeros_like(accA)
            accB[...] = jnp.zeros_like(accB)
            accC[...] = jnp.zeros_like(accC)

        nt = (((1,), (1,)), ((), ()))
        tn = (((0,), (0,)), ((), ()))
        for h in range(H):
            sl = slice(HP * h, HP * (h + 1))
            D, qw, kw, gc, dist, qpos, kpos = _ret_tables(d, lg_ref, h, C)
            qh, kh, vh, doh = q_ref[:, sl], k_ref[:, sl], v_ref[:, sl], do_ref[:, sl]
            st = st_ref[0, 0, sl, :]
            g = G[sl, :]
            stb, gb = st.astype(BF16), g.astype(BF16)
            sraw = lax.dot_general(qh, kh, nt, preferred_element_type=F32)
            dS = lax.dot_general(doh, vh, nt, preferred_element_type=F32) * D
            dSb = dS.astype(BF16)
            smb = (sraw * D).astype(BF16)
            qs = jnp.dot(qh, stb, preferred_element_type=F32)
            kg = jnp.dot(kh, gb, preferred_element_type=F32)
            dq_ref[:, sl] = jnp.dot(dSb, kh, preferred_element_type=F32) + qw * lax.dot_general(doh, stb, nt, preferred_element_type=F32)
            dk_ref[:, sl] = lax.dot_general(dSb, qh, tn, preferred_element_type=F32) + kw * lax.dot_general(vh, gb, nt, preferred_element_type=F32)
            dv_ref[:, sl] = lax.dot_general(smb, doh, tn, preferred_element_type=F32) + kw * kg
            dof, vf = doh.astype(F32), vh.astype(F32)
            accA[h] += sraw * dS * dist
            accB[h] += (qw * qpos) * (qs * dof) + (kw * kpos) * (kg * vf)
            accC[h] += (C * gc) * (st * g)
            G[sl, :] = gc * g + lax.dot_general((qh.astype(F32) * qw).astype(BF16), doh, tn, preferred_element_type=F32)

        @pl.when(n == N - 1)
        def _():
            rows = [jnp.broadcast_to(jnp.sum(accA[h]) + jnp.sum(accB[h]) + jnp.sum(accC[h]), (1, HP)) for h in range(H)]
            dlg_ref[0] = jnp.concatenate(rows + [jnp.zeros((8 - H, HP), F32)], axis=0)

    nb = (4 * _nbytes((C, 512), BF16) + 3 * _nbytes((C, 512), F32) + 2 * _nbytes((512, HP), F32)
          + H * (_nbytes((C, C), F32) + _nbytes((C, HP), F32) + _nbytes((HP, HP), F32)) + 12 * _nbytes((C, C), F32))
    row = lambda d, n: (chunk(d, n), 0)
    out3 = lambda d, n: (chunk(d, n), d)
    return pl.pallas_call(
        body, name="ret_bwd", grid=(2, N),
        in_specs=[pl.BlockSpec(memory_space=pltpu.SMEM)] + [pl.BlockSpec((C, 512), row)] * 4
        + [pl.BlockSpec((1, 1, 512, HP), lambda d, n: (d, chunk(d, n), 0, 0))],
        out_specs=[pl.BlockSpec((C, 512), out3)] * 3 + [pl.BlockSpec((1, 8, HP), lambda d, n: (d, 0, 0))],
        out_shape=[jax.ShapeDtypeStruct((S, 1024), F32)] * 3 + [jax.ShapeDtypeStruct((2, 8, HP), F32)],
        scratch_shapes=[pltpu.VMEM((512, HP), F32), pltpu.VMEM((H, C, C), F32), pltpu.VMEM((H, C, HP), F32),
                        pltpu.VMEM((H, HP, HP), F32)],
        compiler_params=_params(("arbitrary", "arbitrary"), nb),
    )(lg, q, k, v, do, states)


CONV_PAD = 16
CONV_TR = 256
CONV_CB = LANES


def _fill_padded(pad, u_ref, S):
    pad[0:CONV_PAD, :] = jnp.zeros((CONV_PAD, CONV_CB), F32)
    pad[CONV_PAD + S:CONV_PAD + S + CONV_PAD, :] = jnp.zeros((CONV_PAD, CONV_CB), F32)
    pad[CONV_PAD:CONV_PAD + S, :] = u_ref[...]


def _conv_fwd(u, w32, b):
    S, D = u.shape
    tr = min(CONV_TR, S)

    def body(u_ref, w_ref, b_ref, o_ref, pad):
        _fill_padded(pad, u_ref, S)
        wv = w_ref[...]
        bv = b_ref[...]

        def step(t, carry):
            r0 = pl.multiple_of(t * tr, tr)
            acc = jnp.broadcast_to(bv, (tr, CONV_CB))
            for k in range(CONV_KERNEL):
                acc = acc + pad[pl.ds(r0 + (k + 1), tr), :] * wv[k:k + 1, :]
            o_ref[pl.ds(r0, tr), :] = acc
            return carry

        lax.fori_loop(0, S // tr, step, 0)

    nb = 2 * _nbytes((S, CONV_CB), F32) + _nbytes((S + 2 * CONV_PAD, CONV_CB), F32)
    return pl.pallas_call(
        body, name="conv_fwd", grid=(D // CONV_CB,),
        in_specs=[pl.BlockSpec((S, CONV_CB), lambda j: (0, j)), pl.BlockSpec((32, CONV_CB), lambda j: (0, j)),
                  pl.BlockSpec((1, CONV_CB), lambda j: (0, j))],
        out_specs=pl.BlockSpec((S, CONV_CB), lambda j: (0, j)),
        out_shape=jax.ShapeDtypeStruct((S, D), F32),
        scratch_shapes=[pltpu.VMEM((S + 2 * CONV_PAD, CONV_CB), F32)],
        compiler_params=_params(("parallel",), nb),
    )(u, w32, b)


def _conv_bwd_w(u, dout):
    S, D = u.shape
    tr = min(CONV_TR, S)

    def body(u_ref, d_ref, o_ref, pad, acc):
        _fill_padded(pad, u_ref, S)
        acc[...] = jnp.zeros_like(acc)

        def fold(a):
            return jnp.sum(a.reshape(tr // 8, 8, CONV_CB), axis=0)

        def step(t, carry):
            r0 = pl.multiple_of(t * tr, tr)
            dv = d_ref[pl.ds(r0, tr), :]
            for k in range(CONV_KERNEL):
                acc[8 * k:8 * k + 8, :] += fold(pad[pl.ds(r0 + (k + 1), tr), :] * dv)
            acc[8 * CONV_KERNEL:8 * CONV_KERNEL + 8, :] += fold(dv)
            return carry

        lax.fori_loop(0, S // tr, step, 0)
        o_ref[...] = jnp.sum(acc[...].reshape(32, 8, CONV_CB), axis=1)

    nb = 2 * _nbytes((S, CONV_CB), F32) + _nbytes((S + 2 * CONV_PAD, CONV_CB), F32)
    return pl.pallas_call(
        body, name="conv_bwd_w", grid=(D // CONV_CB,),
        in_specs=[pl.BlockSpec((S, CONV_CB), lambda j: (0, j)), pl.BlockSpec((S, CONV_CB), lambda j: (0, j))],
        out_specs=pl.BlockSpec((32, CONV_CB), lambda j: (0, j)),
        out_shape=jax.ShapeDtypeStruct((32, D), F32),
        scratch_shapes=[pltpu.VMEM((S + 2 * CONV_PAD, CONV_CB), F32), pltpu.VMEM((256, CONV_CB), F32)],
        compiler_params=_params(("parallel",), nb),
    )(u, dout)


def _adamw(w, g, m, v):
    shape = w.shape
    w2, g2, m2, v2 = [a.reshape(-1, shape[-1]) for a in (w, g, m, v)]
    W = shape[-1]

    def fn(rows, p):
        w_, g_, m_, v_ = [r[...] for r in rows]
        mn = ADAM_B1 * m_ + (1.0 - ADAM_B1) * g_
        vn = ADAM_B2 * v_ + (1.0 - ADAM_B2) * jnp.square(g_)
        m_hat = mn / (1.0 - ADAM_B1 ** ADAM_STEP)
        v_hat = vn / (1.0 - ADAM_B2 ** ADAM_STEP)
        delta = -ADAM_LR * (m_hat / (jnp.sqrt(v_hat) + ADAM_EPS) + ADAM_WD * w_)
        return [delta, mn, vn], []

    R2 = w2.shape[0]
    ts = 256 if (R2 > 256 and R2 % 256 == 0) else R2
    d, mn, vn = _rowcall("adamw", fn, [w2, g2, m2, v2], [], [(W, F32)] * 3, ts=ts)
    return d.reshape(shape), mn.reshape(shape), vn.reshape(shape)


_HBM = pl.BlockSpec(memory_space=pltpu.HBM)


def _place():
    return lax.axis_index("x"), lax.axis_index("y"), lax.axis_index("c")


def _half_rows(ref, half, h):
    idx = (slice(None),) * (len(ref.shape) - 2) + (pl.ds(half * h, h), slice(None))
    return ref.at[idx]


def _chip_exchange(name, p, mode):
    assert mode in ("half", "whole", "scatter")
    if mode == "half":
        h, W = p.shape[0] // 2, p.shape[1]
    elif mode == "whole":
        h, W = p.shape
    else:
        h, W = p.shape[1], p.shape[2]

    def body(p_ref, o_ref, send_sems, recv_sems, local_sem):
        x, y, c = _place()
        k_me = 2 * x + y

        def src(k):
            if mode == "half":
                return _half_rows(p_ref, c, h)
            return p_ref if mode == "whole" else p_ref.at[k]

        local = pltpu.make_async_copy(src(k_me), o_ref.at[k_me], local_sem)
        local.start()
        copies = []
        for j, (cx, cy) in enumerate([(1 - x, y), (x, 1 - y), (1 - x, 1 - y)]):
            cp = pltpu.make_async_remote_copy(
                src_ref=src(2 * cx + cy), dst_ref=o_ref.at[k_me], send_sem=send_sems.at[j], recv_sem=recv_sems.at[j],
                device_id=(cx, cy, c), device_id_type=MESH)
            cp.start()
            copies.append(cp)
        for cp in copies:
            cp.wait()
        local.wait()

    return pl.pallas_call(
        body, name=name, in_specs=[_HBM], out_specs=_HBM,
        out_shape=jax.ShapeDtypeStruct((4, h, W), p.dtype),
        scratch_shapes=[pltpu.SemaphoreType.DMA((3,)), pltpu.SemaphoreType.DMA((3,)), pltpu.SemaphoreType.DMA],
    )(p)


def _sibling_send_half(name, g):
    n, R, W = g.shape
    h = R // 2

    def body(g_ref, o_ref, send_sem, recv_sem):
        x, y, c = _place()
        cp = pltpu.make_async_remote_copy(
            src_ref=_half_rows(g_ref, 1 - c, h), dst_ref=o_ref, send_sem=send_sem, recv_sem=recv_sem,
            device_id=(x, y, 1 - c), device_id_type=MESH)
        cp.start()
        cp.wait()

    return pl.pallas_call(
        body, name=name, in_specs=[_HBM], out_specs=_HBM,
        out_shape=jax.ShapeDtypeStruct((n, h, W), g.dtype),
        scratch_shapes=[pltpu.SemaphoreType.DMA, pltpu.SemaphoreType.DMA],
    )(g)


def _halves_merge(name, mine):
    def body(m_ref, o_ref, send_sem, recv_sem):
        x, y, c = _place()
        cp = pltpu.make_async_remote_copy(
            src_ref=m_ref, dst_ref=o_ref, send_sem=send_sem, recv_sem=recv_sem,
            device_id=(x, y, 1 - c), device_id_type=MESH)
        cp.start()
        cp.wait()

    theirs = pl.pallas_call(
        body, name=name, in_specs=[_HBM], out_specs=_HBM,
        out_shape=jax.ShapeDtypeStruct(mine.shape, mine.dtype),
        scratch_shapes=[pltpu.SemaphoreType.DMA, pltpu.SemaphoreType.DMA],
    )(mine)
    south = lax.axis_index("c") == 0
    axis = mine.ndim - 2
    return jnp.concatenate([jnp.where(south, mine, theirs), jnp.where(south, theirs, mine)], axis=axis)


def _add2(name, a, b, out_dtype):
    def fn(rows, p):
        return [rows[0][...].astype(F32) + rows[1][...].astype(F32)], []

    return _rowcall(name, fn, [a, b], [], [(a.shape[1], out_dtype)], ts=PACK_TILE)[0]


def _sum4(name, b):
    _, h, W = b.shape
    ts = PACK_TILE if h % PACK_TILE == 0 else h

    def body(b_ref, o_ref):
        o_ref[...] = ((b_ref[0].astype(F32) + b_ref[1].astype(F32)) + b_ref[2].astype(F32)) + b_ref[3].astype(F32)

    nb = _nbytes((4, ts, W), b.dtype) + _nbytes((ts, W), F32)
    return pl.pallas_call(
        body, name=name, grid=(h // ts,),
        in_specs=[pl.BlockSpec((4, ts, W), lambda i: (0, i, 0))], out_specs=pl.BlockSpec((ts, W), lambda i: (i, 0)),
        out_shape=jax.ShapeDtypeStruct((h, W), F32), compiler_params=_params(("parallel",), nb),
    )(b)


PACK_W = 1024
PACK_ROWS = 16
PACK_TILE = 256
SHARDED = (("ev_w_in", "col", False), ("ev_w_uq", "col", False), ("ev_w_ukv", "col", False),
           ("ev_w_out", "row", False), ("od_w_in", "col", False), ("od_b_in", "col", True), ("od_dw_w", "col", True),
           ("od_dw_b", "col", True), ("od_ln_g", "col", True), ("od_ln_b", "col", True), ("od_w_out", "row", False))
REPLICATED = ("ada_b", "pre_g", "post_g", "ev_dec_f", "ev_dec_b", "ev_q_norm_g", "ev_kv_norm_g")
WEIGHTS = ("ada_w", "ada_b", "pre_g", "post_g", "ev_w_in", "ev_dec_f", "ev_dec_b", "ev_q_norm_g", "ev_w_uq", "ev_kv_norm_g",
           "ev_w_ukv", "ev_w_out", "od_w_in", "od_b_in", "od_dw_w", "od_dw_b", "od_ln_g", "od_ln_b", "od_w_out")


def _rows_of(n):
    unit = PACK_W * PACK_ROWS
    return (n + unit - 1) // unit * PACK_ROWS


def _to_rows(flat, lead):
    n = flat.shape[-1]
    r = _rows_of(n)
    flat = jnp.pad(flat, [(0, 0)] * len(lead) + [(0, r * PACK_W - n)])
    return flat.reshape(*lead, r, PACK_W)


def _pad_total(buf, axis):
    r = buf.shape[axis]
    r2 = (r + 2 * PACK_TILE - 1) // (2 * PACK_TILE) * (2 * PACK_TILE)
    pads = [(0, 0)] * buf.ndim
    pads[axis] = (0, r2 - r)
    return jnp.pad(buf, pads)


def _split_chips(full, how):
    if how == "col":
        n = full.shape[-1] // 4
        return jnp.moveaxis(full.reshape(*full.shape[:-1], 4, n), -2, 0)
    n = full.shape[1] // 4
    return jnp.moveaxis(full.reshape(full.shape[0], 4, n, *full.shape[2:]), 1, 0)


def _join_chips(blocks, how):
    if how == "col":
        t = jnp.moveaxis(blocks, 0, -2)
        return t.reshape(*t.shape[:-2], t.shape[-2] * t.shape[-1])
    t = jnp.moveaxis(blocks, 0, 1)
    return t.reshape(t.shape[0], t.shape[1] * t.shape[2], *t.shape[3:])


def _gather_weights(shards):
    parts = []
    for name, _, exact in SHARDED:
        a = shards[name].reshape(-1)
        a = lax.bitcast_convert_type(a, BF16).reshape(-1) if exact else a.astype(BF16)
        parts.append(_to_rows(a, ()))
    packed = _pad_total(jnp.concatenate(parts, axis=0), 0)
    gathered = _halves_merge("gather_merge", _chip_exchange("gather_chips", packed, "half"))
    out, r0 = {}, 0
    for (name, how, exact), part in zip(SHARDED, parts, strict=True):
        r = part.shape[0]
        shp = shards[name].shape
        n = shards[name].size * (2 if exact else 1)
        a = gathered[:, r0:r0 + r].reshape(4, -1)[:, :n]
        if exact:
            a = lax.bitcast_convert_type(a.reshape(4, -1, 2), F32)
        out[name] = _join_chips(a.reshape(4, *shp), how)
        r0 += r
    return out


def _reduce_grads(grads, shard_shapes):
    parts = []
    for name, how, _ in SHARDED:
        b = _split_chips(grads[name], how)
        parts.append(_to_rows(b.reshape(4, -1).astype(BF16), (4,)))
    for name in REPLICATED:
        a = _to_rows(grads[name].reshape(-1).astype(BF16), ())
        parts.append(jnp.broadcast_to(a[None], (4, *a.shape)))
    g = _pad_total(jnp.concatenate(parts, axis=1), 1)
    R = g.shape[1]
    h = R // 2
    c = lax.axis_index("c")
    theirs = _sibling_send_half("reduce_cores", g)
    mine = lax.dynamic_slice_in_dim(g, c * h, h, axis=1)
    chip_sum = _add2("reduce_add", mine.reshape(4 * h, PACK_W), theirs.reshape(4 * h, PACK_W), BF16).reshape(4, h, PACK_W)
    total = _halves_merge("reduce_merge", _sum4("reduce_sum", _chip_exchange("reduce_chips", chip_sum, "scatter")))
    out, r0 = {}, 0
    names = [n for n, _, _ in SHARDED] + list(REPLICATED)
    for name, part in zip(names, parts, strict=True):
        r = part.shape[1]
        shp = shard_shapes[name]
        n = 1
        for s in shp:
            n *= s
        out[name] = total[r0:r0 + r].reshape(-1)[:n].reshape(shp)
        r0 += r
    return out


def _pad_heads(a, n, w, to):
    lead = a.shape[:-1]
    return jnp.pad(a.reshape(*lead, n, w), [(0, 0)] * (len(lead) + 1) + [(0, to - w)]).reshape(*lead, n * to)


def _unpad_heads(a, n, w, to):
    lead = a.shape[:-1]
    return a.reshape(*lead, n, to)[..., :w].reshape(*lead, n * w)


def _w_in_pad(w):
    return jnp.concatenate([_pad_heads(w[:, 0:256], 4, 64, HP), _pad_heads(w[:, 256:512], 4, 64, HP), w[:, 512:2176],
                            _pad_heads(w[:, 2176:2240], 1, 64, HP), w[:, 2240:2752]], axis=1)


def _w_in_unpad(g):
    return jnp.concatenate([_unpad_heads(g[:, Z_RQ:Z_RK], 4, 64, HP), _unpad_heads(g[:, Z_RK:Z_RV], 4, 64, HP),
                            g[:, Z_RV:Z_KR], g[:, Z_KR:Z_KR + 64], g[:, Z_MG:Z_END]], axis=1)


def _prep_layer_weights(full):
    f32 = lambda a: a.astype(F32)
    ev, od = [], []
    for i in range(DEPTH // 2 + DEPTH % 2):
        w_in_p = _w_in_pad(full["ev_w_in"][i])
        w_uq_p = _pad_heads(full["ev_w_uq"][i], MLA_HEADS, MLA_NOPE_DIM + MLA_ROPE_DIM, QW)
        w_ukv, w_out = full["ev_w_ukv"][i], full["ev_w_out"][i]
        ev.append(dict(w_in=w_in_p, w_uq=w_uq_p, w_ukv=w_ukv, w_out=w_out,
                       qg=f32(full["ev_q_norm_g"][i])[None], kvg=f32(full["ev_kv_norm_g"][i])[None],
                       lg=jnp.stack([jax.nn.log_sigmoid(f32(full["ev_dec_f"][i])), jax.nn.log_sigmoid(f32(full["ev_dec_b"][i]))])))
    for i in range(DEPTH // 2):
        w_in, w_out = full["od_w_in"][i], full["od_w_out"][i]
        dw = jnp.pad(f32(full["od_dw_w"][i]), ((0, 1), (0, 0)))
        od.append(dict(w_in=w_in, w_out=w_out, b_in=f32(full["od_b_in"][i])[None],
                       dw=dw, dw_flip=jnp.pad(f32(full["od_dw_w"][i])[::-1], ((0, 1), (0, 0))), dw_b=f32(full["od_dw_b"][i])[None],
                       ln_g=f32(full["od_ln_g"][i])[None], ln_b=f32(full["od_ln_b"][i])[None]))
    return ev, od


def _rope_tables(positions):
    inv_freq = ROPE_BASE ** (-jnp.arange(0, ROPE_DIM, 2, dtype=F32) / ROPE_DIM)
    ang = positions.astype(F32)[:, None] * inv_freq
    z = jnp.zeros((ang.shape[0], HP - ROPE_DIM), F32)
    return jnp.concatenate([jnp.cos(ang), jnp.cos(ang), z], axis=1), jnp.concatenate([jnp.sin(ang), jnp.sin(ang), z], axis=1)


def _even_fwd(x, mod, bias, pre_g, post_g, cos, sin, w):
    h = _pre_fwd(x, pre_g, mod, bias)
    z = _mm_nn("ev_in", h, w["w_in"])
    rq, rk, rv, qn, kvn, kr = _evprep_fwd(z, cos, sin, w["qg"], w["kvg"])
    qf = _mm_nn("ev_uq", qn, w["w_uq"])
    kv = _mm_nn("ev_ukv", kvn, w["w_ukv"])
    Q, K, V = _attnprep_fwd(qf, kv, kr, cos, sin)
    a, lse = _flash_fwd(Q, K, V)
    o2, states = _ret_fwd(w["lg"], rq, rk, rv)
    mix = _mix_fwd(o2, z, a)
    y = _mm_nn("ev_out", mix, w["w_out"])
    x_new = _post_fwd(x, y, post_g, mod, bias)
    return x_new, dict(x=x, h=h, z=z, rq=rq, rk=rk, rv=rv, qn=qn, kvn=kvn, Q=Q, K=K, V=V, a=a, lse=lse, o2=o2,
                       states=states, mix=mix, y=y)


def _even_bwd(dx, s, mod, bias, pre_g, post_g, cos, sin, w):
    dy, dpost_g, dgate = _post_bwd(s["y"], dx, post_g, mod, bias)
    dmix = _mm_nt("ev_out_dx", dy, w["w_out"])
    dw_out = _mm_tn("ev_out_dw", s["mix"], dy)
    do, drg, da, dmg = _mix_bwd(s["o2"], s["z"], s["a"], dmix)
    drq, drk, drv, dlg = _ret_bwd(w["lg"], s["rq"], s["rk"], s["rv"], do, s["states"])
    dQ, dK, dV = _flash_bwd(s["Q"], s["K"], s["V"], da, s["a"], s["lse"])
    dqf, dkv, dkr = _attnprep_bwd(dQ, dK, dV, cos, sin)
    dqn = _mm_nt("ev_uq_dx", dqf, w["w_uq"])
    dw_uq = _mm_tn("ev_uq_dw", s["qn"], dqf)
    dkvn = _mm_nt("ev_ukv_dx", dkv, w["w_ukv"])
    dw_ukv = _mm_tn("ev_ukv_dw", s["kvn"], dkv)
    dz, dqg, dkvg = _evprep_bwd(s["z"], cos, sin, w["qg"], w["kvg"], drq, drk, drv, drg, dqn, dkvn, dkr, dmg)
    dh = _mm_nt("ev_in_dx", dz, w["w_in"])
    dw_in = _mm_tn("ev_in_dw", s["h"], dz)
    dx_new, dpre_g, dscale, dshift = _pre_bwd(s["x"], dh, dx, pre_g, mod, bias)
    g = dict(ev_w_in=_w_in_unpad(dw_in), ev_w_uq=_unpad_heads(dw_uq, MLA_HEADS, MLA_NOPE_DIM + MLA_ROPE_DIM, QW),
             ev_w_ukv=dw_ukv, ev_w_out=dw_out, ev_q_norm_g=dqg[0], ev_kv_norm_g=dkvg[0],
             dlg_f=dlg[0, :RET_HEADS, 0], dlg_b=dlg[1, :RET_HEADS, 0])
    return dx_new, g, dpre_g[0], dpost_g[0], jnp.concatenate([dshift, dscale, dgate], axis=1)


def _odd_fwd(x, mod, bias, pre_g, post_g, w):
    h = _pre_fwd(x, pre_g, mod, bias)
    z = _mm_nn("od_in", h, w["w_in"])
    u = _glu_fwd(z, w["b_in"])
    uc = _conv_fwd(u, w["dw"], w["dw_b"])
    vv = _odmix_fwd(uc, z, w["b_in"], w["ln_g"], w["ln_b"])
    y = _mm_nn("od_out", vv, w["w_out"])
    x_new = _post_fwd(x, y, post_g, mod, bias)
    return x_new, dict(x=x, h=h, z=z, u=u, uc=uc, vv=vv, y=y)


def _odd_bwd(dx, s, mod, bias, pre_g, post_g, w):
    dy, dpost_g, dgate = _post_bwd(s["y"], dx, post_g, mod, bias)
    dvv = _mm_nt("od_out_dx", dy, w["w_out"])
    dw_out = _mm_tn("od_out_dw", s["vv"], dy)
    duc, dg, dbg, dln_g, dln_b = _odmix_bwd(s["uc"], s["z"], w["b_in"], w["ln_g"], w["ln_b"], dvv)
    du = _conv_fwd(duc, w["dw_flip"], jnp.zeros_like(w["dw_b"]))
    dwb = _conv_bwd_w(s["u"], duc)
    dz, dba, dbb = _glu_bwd(s["z"], w["b_in"], du, dg)
    dh = _mm_nt("od_in_dx", dz, w["w_in"])
    dw_in = _mm_tn("od_in_dw", s["h"], dz)
    dx_new, dpre_g, dscale, dshift = _pre_bwd(s["x"], dh, dx, pre_g, mod, bias)
    g = dict(od_w_in=dw_in, od_b_in=jnp.concatenate([dba, dbb, dbg], axis=1)[0], od_dw_w=dwb[:CONV_KERNEL], od_dw_b=dwb[CONV_KERNEL],
             od_ln_g=dln_g[0], od_ln_b=dln_b[0], od_w_out=dw_out)
    return dx_new, g, dpre_g[0], dpost_g[0], jnp.concatenate([dshift, dscale, dgate], axis=1)


def _local_step(x, mod, positions, full, loss_target):
    cos, sin = _rope_tables(positions)
    ev, od = _prep_layer_weights(full)
    f32 = lambda a: a.astype(F32)
    mods = [mod[l:l + 1] for l in range(DEPTH)]
    biases = [f32(full["ada_b"][l])[None] for l in range(DEPTH)]
    pre = [f32(full["pre_g"][l])[None] for l in range(DEPTH)]
    post = [f32(full["post_g"][l])[None] for l in range(DEPTH)]

    saved = []
    for l in range(DEPTH):
        if l % 2 == 0:
            x, s = _even_fwd(x, mods[l], biases[l], pre[l], post[l], cos, sin, ev[l // 2])
        else:
            x, s = _odd_fwd(x, mods[l], biases[l], pre[l], post[l], od[l // 2])
        saved.append(s)
    dx, loss = _loss_head(x, loss_target)

    per_layer = [None] * DEPTH
    dpre, dpost, dmod = [None] * DEPTH, [None] * DEPTH, [None] * DEPTH
    for l in reversed(range(DEPTH)):
        if l % 2 == 0:
            dx, per_layer[l], dpre[l], dpost[l], dmod[l] = _even_bwd(dx, saved[l], mods[l], biases[l], pre[l], post[l], cos, sin, ev[l // 2])
        else:
            dx, per_layer[l], dpre[l], dpost[l], dmod[l] = _odd_bwd(dx, saved[l], mods[l], biases[l], pre[l], post[l], od[l // 2])

    grads = {"pre_g": jnp.stack(dpre), "post_g": jnp.stack(dpost), "ada_b": jnp.concatenate(dmod, axis=0)}
    evs =[per_layer[l] for l in range(0, DEPTH, 2)]
    ods = [per_layer[l] for l in range(1, DEPTH, 2)]
    for name in ("ev_w_in", "ev_w_uq", "ev_w_ukv", "ev_w_out", "ev_q_norm_g", "ev_kv_norm_g"):
        grads[name] = jnp.stack([g[name] for g in evs])
    for name in ("od_w_in", "od_b_in", "od_dw_w", "od_dw_b", "od_ln_g", "od_ln_b", "od_w_out"):
        grads[name] = jnp.stack([g[name] for g in ods])
    grads["ev_dec_f"] = jnp.stack([g["dlg_f"] for g in evs]) * jax.nn.sigmoid(-f32(full["ev_dec_f"]))
    grads["ev_dec_b"] = jnp.stack([g["dlg_b"] for g in evs]) * jax.nn.sigmoid(-f32(full["ev_dec_b"]))
    return loss[0, 0], dx, grads


ADA_ROWS = 16
ADA_COLS = 3 * D_MODEL // 4
ADA_PACK = (8, 1024)


def _ada_pack(a):
    n = ADA_PACK[0] * ADA_PACK[1]
    return jnp.pad(a, ((0, 0), (0, n - a.shape[1]))).reshape(4, *ADA_PACK)


def _ada_unpack(a):
    return a.reshape(4, -1)[:, :DEPTH * ADA_COLS]


def _ada_forward(c, ada_w):
    D = D_MODEL
    c16 = jnp.zeros((ADA_ROWS, D), F32).at[0].set(c[0])
    c_act = _rowcall("silu_c", lambda rows, p: ([_silu(rows[0][...])], []), [c16], [], [(D, BF16)])[0]
    got = _chip_exchange("ada_c", c_act, "whole")
    c4 = jnp.pad(got[:, 0, :], ((0, ADA_ROWS - 4), (0, 0)))
    per_layer = [_mm_nn("ada", c4, ada_w[l].astype(BF16))[:4] for l in range(DEPTH)]
    p = jnp.stack(per_layer, axis=1).reshape(4, DEPTH * ADA_COLS)
    got = _chip_exchange("ada_mod", _ada_pack(p), "scatter")
    mod = _ada_unpack(got).reshape(4, DEPTH, ADA_COLS).transpose(1, 0, 2).reshape(DEPTH, 3 * D)
    return mod, c4


def _ada_backward(dmod, c4):
    p = dmod.reshape(DEPTH, 4, ADA_COLS).transpose(1, 0, 2).reshape(4, DEPTH * ADA_COLS)
    got = _chip_exchange("ada_dmod", _ada_pack(p), "scatter")
    dm4 = jnp.pad(_ada_unpack(got), ((0, ADA_ROWS - 4), (0, 0)))
    both = _halves_merge("ada_merge", jnp.concatenate([dm4, c4.astype(F32)], axis=1))
    dm32, c32 = both[:, :DEPTH * ADA_COLS].astype(BF16), both[:, DEPTH * ADA_COLS:].astype(BF16)
    return jnp.stack([_mm_tn("ada_dw", c32, dm32[:, l * ADA_COLS:(l + 1) * ADA_COLS]) for l in range(DEPTH)])


def kernel(x, c, positions, ada_w, ada_b, pre_g, post_g, ev_w_in, ev_dec_f, ev_dec_b, ev_q_norm_g, ev_w_uq, ev_kv_norm_g, ev_w_ukv, ev_w_out, od_w_in, od_b_in, od_dw_w, od_dw_b, od_ln_g, od_ln_b, od_w_out, loss_target, m_ada_w, m_ada_b, m_pre_g, m_post_g, m_ev_w_in, m_ev_dec_f, m_ev_dec_b, m_ev_q_norm_g, m_ev_w_uq, m_ev_kv_norm_g, m_ev_w_ukv, m_ev_w_out, m_od_w_in, m_od_b_in, m_od_dw_w, m_od_dw_b, m_od_ln_g, m_od_ln_b, m_od_w_out, v_ada_w, v_ada_b, v_pre_g, v_post_g, v_ev_w_in, v_ev_dec_f, v_ev_dec_b, v_ev_q_norm_g, v_ev_w_uq, v_ev_kv_norm_g, v_ev_w_ukv, v_ev_w_out, v_od_w_in, v_od_b_in, v_od_dw_w, v_od_dw_b, v_od_ln_g, v_od_ln_b, v_od_w_out):
    w = dict(ada_w=ada_w, ada_b=ada_b, pre_g=pre_g, post_g=post_g, ev_w_in=ev_w_in, ev_dec_f=ev_dec_f, ev_dec_b=ev_dec_b,
             ev_q_norm_g=ev_q_norm_g, ev_w_uq=ev_w_uq, ev_kv_norm_g=ev_kv_norm_g, ev_w_ukv=ev_w_ukv, ev_w_out=ev_w_out,
             od_w_in=od_w_in, od_b_in=od_b_in, od_dw_w=od_dw_w, od_dw_b=od_dw_b, od_ln_g=od_ln_g, od_ln_b=od_ln_b, od_w_out=od_w_out)
    m = dict(ada_w=m_ada_w, ada_b=m_ada_b, pre_g=m_pre_g, post_g=m_post_g, ev_w_in=m_ev_w_in, ev_dec_f=m_ev_dec_f, ev_dec_b=m_ev_dec_b,
             ev_q_norm_g=m_ev_q_norm_g, ev_w_uq=m_ev_w_uq, ev_kv_norm_g=m_ev_kv_norm_g, ev_w_ukv=m_ev_w_ukv, ev_w_out=m_ev_w_out,
             od_w_in=m_od_w_in, od_b_in=m_od_b_in, od_dw_w=m_od_dw_w, od_dw_b=m_od_dw_b, od_ln_g=m_od_ln_g, od_ln_b=m_od_ln_b, od_w_out=m_od_w_out)
    v = dict(ada_w=v_ada_w, ada_b=v_ada_b, pre_g=v_pre_g, post_g=v_post_g, ev_w_in=v_ev_w_in, ev_dec_f=v_ev_dec_f, ev_dec_b=v_ev_dec_b,
             ev_q_norm_g=v_ev_q_norm_g, ev_w_uq=v_ev_w_uq, ev_kv_norm_g=v_ev_kv_norm_g, ev_w_ukv=v_ev_w_ukv, ev_w_out=v_ev_w_out,
             od_w_in=v_od_w_in, od_b_in=v_od_b_in, od_dw_w=v_od_dw_w, od_dw_b=v_od_dw_b, od_ln_g=v_od_ln_g, od_ln_b=v_od_ln_b, od_w_out=v_od_w_out)

    full = _gather_weights({name: w[name] for name, _, _ in SHARDED})
    for name in REPLICATED:
        full[name] = w[name]
    mod, c4 = _ada_forward(c, ada_w)
    loss_local, grad_x, grads = _local_step(x[0], mod, positions[0], full, loss_target[0])
    loss = lax.psum(loss_local, ("x", "y", "c"))
    g = _reduce_grads(grads, {name: w[name].shape for name in WEIGHTS})
    g["ada_w"] = _ada_backward(grads["ada_b"], c4)
    delta, new_m, new_v = {}, {}, {}
    for name in WEIGHTS:
        delta[name], new_m[name], new_v[name] = _adamw(w[name], g[name], m[name], v[name])
    return (loss, grad_x[None], *[g[n] for n in WEIGHTS], *[delta[n] for n in WEIGHTS],
            *[new_m[n] for n in WEIGHTS], *[new_v[n] for n in WEIGHTS])
```

```python
import functools

import jax
import jax.numpy as jnp
from jax import lax
from jax.experimental import pallas as pl
from jax.experimental.pallas import tpu as pltpu

F32 = jnp.float32
BF16 = jnp.bfloat16

D_MODEL = 1024
DEPTH = 4
RET_HEADS = 4
RET_QK_DIM = 64
RET_V_DIM = 128
MLA_HEADS = 4
MLA_Q_RANK = 384
MLA_KV_RANK = 256
MLA_NOPE_DIM = 128
MLA_ROPE_DIM = 64
MLA_V_DIM = 128
ROPE_DIM = 64
ROPE_BASE = 10000.0
CONV_KERNEL = 31
EPS = 1e-6
ADAM_LR, ADAM_B1, ADAM_B2, ADAM_EPS, ADAM_WD, ADAM_STEP = 0.001, 0.9, 0.999, 1e-08, 0.01, 10

LANES = 128
HP = 128
Z_RQ, Z_RK, Z_RV, Z_RG, Z_MG, Z_CQ, Z_CKV, Z_KR, Z_END = 0, 512, 1024, 1536, 2048, 2560, 2944, 3200, 3328
Z_LAT = Z_END - Z_CQ


def _z_ret(z):
    return (z, Z_RG, 0)


def _z_gates(z):
    return [(z, 512, Z_RG // 512), (z, 512, Z_MG // 512)]


def _z_latents(z):
    return [(z, 256, Z_CQ // 256 + i) for i in range(Z_LAT // 256)]
V7X_VMEM_BYTES = 64 * 1024 * 1024
VMEM_CAP = V7X_VMEM_BYTES - 8 * 1024 * 1024

MESH = pl.DeviceIdType.MESH


def _nbytes(shape, dtype):
    n = 1
    for s in shape:
        n *= s
    return n * jnp.dtype(dtype).itemsize


def _params(sem, block_bytes):
    limit = min(VMEM_CAP, max(32 * 1024 * 1024, 2 * block_bytes + 16 * 1024 * 1024))
    return pltpu.CompilerParams(dimension_semantics=sem, vmem_limit_bytes=limit)


def _rowcall(name, fn, rows, params, outs, accs=(), ts=256):
    rows = [r if isinstance(r, tuple) else (r, r.shape[1], 0) for r in rows]
    S = rows[0][0].shape[0]
    ts = min(ts, S)
    assert S % ts == 0, (name, S, ts)
    nr, npar, no, na = len(rows), len(params), len(outs), len(accs)

    def body(*refs):
        row_refs = refs[:nr]
        pvals = [r[...] for r in refs[nr:nr + npar]]
        out_refs = refs[nr + npar:nr + npar + no]
        acc_refs = refs[nr + npar + no:]
        ovals, avals = fn(row_refs, pvals)
        for r, v in zip(out_refs, ovals, strict=True):
            r[...] = v.astype(r.dtype)
        if na:
            @pl.when(pl.program_id(0) == 0)
            def _():
                for r in acc_refs:
                    r[...] = jnp.zeros_like(r)
            for r, v in zip(acc_refs, avals, strict=True):
                r[...] += v

    in_specs = [pl.BlockSpec((ts, w), functools.partial(lambda i, b: (i, b), b=blk)) for _, w, blk in rows]
    in_specs += [pl.BlockSpec(p.shape, lambda i: (0, 0)) for p in params]
    out_specs = [pl.BlockSpec((ts, w), lambda i: (i, 0)) for w, _ in outs]
    out_specs += [pl.BlockSpec(s, lambda i: (0, 0)) for s in accs]
    out_shape = [jax.ShapeDtypeStruct((S, w), dt) for w, dt in outs]
    out_shape += [jax.ShapeDtypeStruct(s, F32) for s in accs]
    nb = sum(_nbytes((ts, w), a.dtype) for a, w, _ in rows) + sum(_nbytes((ts, w), dt) for w, dt in outs)
    nb += sum(_nbytes(p.shape, p.dtype) for p in params) + sum(_nbytes(s, F32) for s in accs)
    res = pl.pallas_call(
        body, name=name, grid=(S // ts,), in_specs=in_specs, out_specs=out_specs, out_shape=out_shape,
        compiler_params=_params(("arbitrary",) if na else ("parallel",), 3 * nb),
    )(*[a for a, _, _ in rows], *params)
    return res


def _silu(x):
    return x * jax.nn.sigmoid(x)


def _rms(x, g):
    return x * lax.rsqrt(jnp.mean(x * x, axis=-1, keepdims=True) + EPS) * g


def _rot(x):
    lane = lax.broadcasted_iota(jnp.int32, x.shape, 1)
    return jnp.where(lane < ROPE_DIM // 2, -pltpu.roll(x, LANES - ROPE_DIM // 2, 1), pltpu.roll(x, ROPE_DIM // 2, 1))


def _rope(x, cos, sin):
    return x * cos + _rot(x) * sin


def _rope_t(dy, cos, sin):
    return dy * cos - _rot(dy) * sin


def _groups(ref, start, n):
    return [ref[:, start + HP * h:start + HP * (h + 1)] for h in range(n)]


def _pre_math(x, g, m_scale, b_scale, m_shift, b_shift):
    return _rms(x, g) * (1.0 + (m_scale + b_scale)) + (m_shift + b_shift)


def _pre_fwd(x, g, mod, bias):
    D = D_MODEL

    def fn(rows, p):
        g_, mod_, b_ = p
        h = _pre_math(rows[0][...], g_, mod_[:, D:2 * D], b_[:, D:2 * D], mod_[:, :D], b_[:, :D])
        return [h], []

    return _rowcall("pre_fwd", fn, [x], [g, mod, bias], [(D, BF16)])[0]


def _pre_bwd(x, dh, dx_res, g, mod, bias):
    D = D_MODEL

    def fn(rows, p):
        g_, mod_, b_ = p
        xv = rows[0][...]
        _, vjp = jax.vjp(_pre_math, xv, g_, mod_[:, D:2 * D], b_[:, D:2 * D], mod_[:, :D], b_[:, :D])
        dx, dg, dsc, _, dsh, _ = vjp(rows[1][...].astype(F32))
        return [dx + rows[2][...]], [dg, dsc, dsh]

    return _rowcall("pre_bwd", fn, [x, dh, dx_res], [g, mod, bias], [(D, F32)], [(1, D)] * 3)


def _post_math(x, y, g, m_gate, b_gate):
    return x + (m_gate + b_gate) * _rms(y, g)


def _post_fwd(x, y, g, mod, bias):
    D = D_MODEL

    def fn(rows, p):
        g_, mod_, b_ = p
        return [_post_math(rows[0][...], rows[1][...], g_, mod_[:, 2 * D:], b_[:, 2 * D:])], []

    return _rowcall("post_fwd", fn, [x, y], [g, mod, bias], [(D, F32)])[0]


def _post_bwd(y, dxo, g, mod, bias):
    D = D_MODEL

    def fn(rows, p):
        g_, mod_, b_ = p
        yv = rows[0][...]
        _, vjp = jax.vjp(lambda y_, g2, mg: _post_math(0.0, y_, g2, mg, b_[:, 2 * D:]), yv, g_, mod_[:, 2 * D:])
        dy, dg, dgate = vjp(rows[1][...])
        return [dy], [dg, dgate]

    return _rowcall("post_bwd", fn, [y, dxo], [g, mod, bias], [(D, BF16)], [(1, D)] * 2)


def _evprep_fwd(z, cos, sin, qg, kvg):
    def fn(rows, p):
        z_, l0, l1, l2, c_, s_ = rows
        qg_, kvg_ = p
        cos_, sin_ = c_[...], s_[...]
        lat = jnp.concatenate([l0[...], l1[...], l2[...]], axis=1)
        rq = jnp.concatenate([_rope(v[...], cos_, sin_) for v in _groups(z_, Z_RQ, RET_HEADS)], axis=1)
        rk = jnp.concatenate([_rope(v[...], cos_, sin_) for v in _groups(z_, Z_RK, RET_HEADS)], axis=1)
        rk = rk * (RET_QK_DIM ** -0.5)
        rv = z_[:, Z_RV:Z_RG]
        qn = _rms(lat[:, :Z_CKV - Z_CQ], qg_)
        kvn = _rms(lat[:, Z_CKV - Z_CQ:Z_KR - Z_CQ], kvg_)
        kr = _rope(lat[:, Z_KR - Z_CQ:], cos_, sin_)
        return [rq, rk, rv, qn, kvn, kr], []

    return _rowcall("evprep_fwd", fn, [_z_ret(z), *_z_latents(z), cos, sin], [qg, kvg],
                    [(512, BF16), (512, BF16), (512, BF16), (MLA_Q_RANK, BF16), (MLA_KV_RANK, BF16), (HP, F32)])


def _evprep_bwd(z, cos, sin, qg, kvg, drq, drk, drv, drg, dqn, dkvn, dkr, dmg):
    def fn(rows, p):
        l0, l1, l2, c_, s_, drq_, drk_, drv_, drg_, dqn_, dkvn_, dkr_, dmg_ = rows
        qg_, kvg_ = p
        cos_, sin_ = c_[...], s_[...]
        lat = jnp.concatenate([l0[...], l1[...], l2[...]], axis=1)
        parts = []
        for h in range(RET_HEADS):
            parts.append(_rope_t(drq_[:, HP * h:HP * (h + 1)] + drq_[:, 512 + HP * h:512 + HP * (h + 1)], cos_, sin_))
        for h in range(RET_HEADS):
            t = drk_[:, HP * h:HP * (h + 1)] + drk_[:, 512 + HP * h:512 + HP * (h + 1)]
            parts.append(_rope_t(t, cos_, sin_) * (RET_QK_DIM ** -0.5))
        parts.append(drv_[:, :512] + drv_[:, 512:])
        parts += [drg_[...], dmg_[...]]
        _, vq = jax.vjp(_rms, lat[:, :Z_CKV - Z_CQ], qg_)
        dcq, dqg = vq(dqn_[...])
        _, vkv = jax.vjp(_rms, lat[:, Z_CKV - Z_CQ:Z_KR - Z_CQ], kvg_)
        dckv, dkvg = vkv(dkvn_[...])
        parts += [dcq, dckv, _rope_t(dkr_[...], cos_, sin_)]
        return [jnp.concatenate([v.astype(BF16) for v in parts], axis=1)], [dqg, dkvg]

    rows = [*_z_latents(z), cos, sin, drq, drk, drv, drg, dqn, dkvn, dkr, dmg]
    return _rowcall("evprep_bwd", fn, rows, [qg, kvg], [(Z_END, BF16)], [(1, MLA_Q_RANK), (1, MLA_KV_RANK)], ts=128)


ATT_SCALE = (MLA_NOPE_DIM + MLA_ROPE_DIM) ** -0.5
LOG2E = 1.4426950408889634
LN2 = 0.6931471805599453


def _attnprep_fwd(qf, kv, kr, cos, sin):
    qscale = ATT_SCALE * LOG2E

    def fn(rows, p):
        q_, kv_, kr_, c_, s_ = rows
        cos_, sin_ = c_[...], s_[...]
        krv = kr_[...]
        qs, ks, vs = [], [], []
        for h in range(MLA_HEADS):
            b = 2 * HP * h
            qs += [q_[:, b:b + HP] * qscale, _rope(q_[:, b + HP:b + 2 * HP], cos_, sin_) * qscale]
            ks += [kv_[:, b:b + HP], krv]
            vs += [kv_[:, b + HP:b + 2 * HP]]
        return [jnp.concatenate(qs, axis=1), jnp.concatenate(ks, axis=1), jnp.concatenate(vs, axis=1)], []

    return _rowcall("attnprep_fwd", fn, [qf, kv, kr, cos, sin], [], [(1024, BF16), (1024, BF16), (512, BF16)])


def _attnprep_bwd(dQ, dK, dV, cos, sin):
    def fn(rows, p):
        dq_, dk_, dv_, c_, s_ = rows
        cos_, sin_ = c_[...], s_[...]
        dqs, dkvs = [], []
        dkr = None
        for h in range(MLA_HEADS):
            b = 2 * HP * h
            dqs += [dq_[:, b:b + HP] * ATT_SCALE, _rope_t(dq_[:, b + HP:b + 2 * HP], cos_, sin_) * ATT_SCALE]
            dkvs += [dk_[:, b:b + HP] * LN2, dv_[:, HP * h:HP * (h + 1)]]
            t = dk_[:, b + HP:b + 2 * HP]
            dkr = t if dkr is None else dkr + t
        return [jnp.concatenate(dqs, axis=1), jnp.concatenate(dkvs, axis=1), dkr * LN2], []

    return _rowcall("attnprep_bwd", fn, [dQ, dK, dV, cos, sin], [], [(1024, BF16), (1024, BF16), (HP, F32)])


def _mix_math(o, rg, a, mg):
    outs = []
    for h in range(RET_HEADS):
        oh = o[:, HP * h:HP * (h + 1)]
        mu = jnp.mean(oh, axis=-1, keepdims=True)
        var = jnp.mean(jnp.square(oh - mu), axis=-1, keepdims=True)
        outs.append((oh - mu) * lax.rsqrt(var + EPS))
    ret = jnp.concatenate(outs, axis=1) * _silu(rg)
    return jnp.concatenate([ret, a * _silu(mg)], axis=1)


def _mix_fwd(o2, z, a):
    def fn(rows, p):
        o_, rg_, mg_, a_ = rows
        return [_mix_math(o_[:, :512] + o_[:, 512:], rg_[...], a_[...], mg_[...])], []

    return _rowcall("mix_fwd", fn, [o2, *_z_gates(z), a], [], [(1024, BF16)])[0]


def _mix_bwd(o2, z, a, dmix):
    def fn(rows, p):
        o_, rg_, mg_, a_, dm_ = rows
        _, vjp = jax.vjp(_mix_math, o_[:, :512] + o_[:, 512:], rg_[...], a_[...], mg_[...])
        return list(vjp(dm_[...].astype(F32))), []

    return _rowcall("mix_bwd", fn, [o2, *_z_gates(z), a, dmix], [], [(512, BF16), (512, F32), (512, BF16), (512, F32)])


def _glu_math(a, b, ba, bb):
    return (a + ba) * jax.nn.sigmoid(b + bb)


def _glu_fwd(z, bin_):
    D = D_MODEL

    def fn(rows, p):
        za_, zb_ = rows
        b_, = p
        return [_glu_math(za_[...], zb_[...], b_[:, :D], b_[:, D:2 * D])], []

    return _rowcall("glu_fwd", fn, [(z, D, 0), (z, D, 1)], [bin_], [(D, F32)])[0]


def _odmix_math(uc, g, bg, ln_g, ln_b):
    mu = jnp.mean(uc, axis=-1, keepdims=True)
    var = jnp.mean(jnp.square(uc - mu), axis=-1, keepdims=True)
    y = (uc - mu) * lax.rsqrt(var + EPS) * ln_g + ln_b
    return _silu(y) * _silu(g + bg)


def _odmix_fwd(uc, z, bin_, ln_g, ln_b):
    D = D_MODEL

    def fn(rows, p):
        uc_, zg_ = rows
        b_, g_, lb_ = p
        return [_odmix_math(uc_[...], zg_[...], b_[:, 2 * D:], g_, lb_)], []

    return _rowcall("odmix_fwd", fn, [uc, (z, D, 2)], [bin_, ln_g, ln_b], [(D, BF16)])[0]


def _odmix_bwd(uc, z, bin_, ln_g, ln_b, dv):
    D = D_MODEL

    def fn(rows, p):
        uc_, zg_, dv_ = rows
        b_, g_, lb_ = p
        _, vjp = jax.vjp(_odmix_math, uc_[...], zg_[...], b_[:, 2 * D:], g_, lb_)
        duc, dg, dbg, dlg, dlb = vjp(dv_[...])
        return [duc, dg], [dbg, dlg, dlb]

    return _rowcall("odmix_bwd", fn, [uc, (z, D, 2), dv], [bin_, ln_g, ln_b], [(D, F32), (D, BF16)], [(1, D)] * 3)


def _glu_bwd(z, bin_, du, dg):
    D = D_MODEL

    def fn(rows, p):
        za_, zb_, du_, dg_ = rows
        b_, = p
        _, vjp = jax.vjp(_glu_math, za_[...], zb_[...], b_[:, :D], b_[:, D:2 * D])
        da, db, dba, dbb = vjp(du_[...])
        return [jnp.concatenate([da.astype(BF16), db.astype(BF16), dg_[...]], axis=1)], [dba, dbb]

    return _rowcall("glu_bwd", fn, [(z, D, 0), (z, D, 1), du, dg], [bin_], [(3 * D, BF16)], [(1, D)] * 2)


def _loss_head(x, tgt):
    D = D_MODEL

    def fn(rows, p):
        err = rows[0][...] - rows[1][...]
        part = 0.5 * jnp.sum(jnp.mean(err * err, axis=-1, keepdims=True), axis=0, keepdims=True)
        return [err * (1.0 / D)], [jnp.broadcast_to(part, (1, LANES))]

    return _rowcall("loss_head", fn, [x, tgt], [], [(D, F32)], [(1, LANES)])


def _tile(n, cap):
    if n <= cap:
        return n
    best = None
    for t in range(LANES, cap + 1, LANES):
        if n % t == 0:
            best = t
    assert best is not None, (n, cap)
    return best


MM_TN_CAP = 1792
NT_DIMS = (((1,), (1,)), ((), ()))
TN_DIMS = (((0,), (0,)), ((), ()))


def _mm_nn(name, a, b, out_dtype=F32, tm=512, tn_cap=MM_TN_CAP):
    M, K = a.shape
    N = b.shape[1]
    tm = min(tm, M)
    tn = _tile(N, tn_cap)

    def body(a_ref, b_ref, o_ref):
        o_ref[...] = jnp.dot(a_ref[...], b_ref[...], preferred_element_type=F32).astype(o_ref.dtype)

    nb = _nbytes((tm, K), a.dtype) + _nbytes((K, tn), b.dtype) + _nbytes((tm, tn), out_dtype) + _nbytes((tm, tn), F32)
    return pl.pallas_call(
        body, name=name, grid=(N // tn, M // tm),
        in_specs=[pl.BlockSpec((tm, K), lambda j, i: (i, 0)), pl.BlockSpec((K, tn), lambda j, i: (0, j))],
        out_specs=pl.BlockSpec((tm, tn), lambda j, i: (i, j)),
        out_shape=jax.ShapeDtypeStruct((M, N), out_dtype),
        compiler_params=_params(("parallel", "parallel"), nb),
    )(a, b)


def _mm_nt(name, a, b, out_dtype=F32, tm=512, tn_cap=1024):
    M, K = a.shape
    N = b.shape[0]
    tm = min(tm, M)
    tn = _tile(N, tn_cap)

    def body(a_ref, b_ref, o_ref):
        o_ref[...] = lax.dot_general(a_ref[...], b_ref[...], NT_DIMS, preferred_element_type=F32).astype(o_ref.dtype)

    nb = _nbytes((tm, K), a.dtype) + _nbytes((tn, K), b.dtype) + _nbytes((tm, tn), out_dtype) + _nbytes((tm, tn), F32)
    return pl.pallas_call(
        body, name=name, grid=(N // tn, M // tm),
        in_specs=[pl.BlockSpec((tm, K), lambda j, i: (i, 0)), pl.BlockSpec((tn, K), lambda j, i: (j, 0))],
        out_specs=pl.BlockSpec((tm, tn), lambda j, i: (i, j)),
        out_shape=jax.ShapeDtypeStruct((M, N), out_dtype),
        compiler_params=_params(("parallel", "parallel"), nb),
    )(a, b)


def _mm_tn(name, a, b, ts=512, tm_cap=512, tn_cap=MM_TN_CAP):
    S, M = a.shape
    N = b.shape[1]
    ts = min(ts, S)
    tm = _tile(M, tm_cap)
    tn = _tile(N, tn_cap)

    def body(a_ref, b_ref, o_ref):
        @pl.when(pl.program_id(2) == 0)
        def _():
            o_ref[...] = jnp.zeros_like(o_ref)

        o_ref[...] += lax.dot_general(a_ref[...], b_ref[...], (((0,), (0,)), ((), ())), preferred_element_type=F32)

    nb = _nbytes((ts, tm), a.dtype) + _nbytes((ts, tn), b.dtype) + 2 * _nbytes((tm, tn), F32)
    return pl.pallas_call(
        body, name=name, grid=(M // tm, N // tn, S // ts),
        in_specs=[pl.BlockSpec((ts, tm), lambda i, j, s: (s, i)), pl.BlockSpec((ts, tn), lambda i, j, s: (s, j))],
        out_specs=pl.BlockSpec((tm, tn), lambda i, j, s: (i, j)),
        out_shape=jax.ShapeDtypeStruct((M, N), F32),
        compiler_params=_params(("parallel", "parallel", "arbitrary"), nb),
    )(a, b)


ATT_TQ = 1024
ATT_TK = 1024
ATT_TK_FWD = 1024
QW = 2 * HP


def _kv_tiles(V, tk):
    S = V.shape[0]
    return V.reshape(S // tk, tk, MLA_HEADS, HP).transpose(2, 0, 3, 1)


def _flash_fwd(Q, K, V):
    S = Q.shape[0]
    H = MLA_HEADS
    tq, tk = min(ATT_TQ, S), min(ATT_TK_FWD, S)
    nk = S // tk
    VT = _kv_tiles(V, tk)

    def body(q_ref, k_ref, vt_ref, o_ref, lse_ref, m_s, l_s, acc_s, s_a, s_b):
        m_s[...] = jnp.full(m_s.shape, -jnp.inf, F32)
        l_s[...] = jnp.zeros(l_s.shape, F32)
        acc_s[...] = jnp.zeros(acc_s.shape, F32)

        def scores(j):
            k = k_ref[pl.ds(pl.multiple_of(j * tk, tk), tk), :]
            return lax.dot_general(k, q_ref[...], NT_DIMS, preferred_element_type=F32)

        def consume(st, j):
            m_prev = m_s[...]
            m_new = jnp.maximum(m_prev, jnp.max(st, axis=0, keepdims=True))
            alpha = jnp.exp2(m_prev - m_new)
            pt = jnp.exp2(st - m_new)
            l_s[...] = alpha * l_s[...] + jnp.sum(pt, axis=0, keepdims=True)
            acc_s[...] = alpha * acc_s[...] + jnp.dot(vt_ref[0, j], pt.astype(BF16), preferred_element_type=F32)
            m_s[...] = m_new

        if nk % 2:
            def step(j, carry):
                consume(scores(j), j)
                return carry

            lax.fori_loop(0, nk, step, 0)
        else:
            s_a[...] = scores(0)

            def pair(jj, carry):
                j0 = 2 * jj
                s_b[...] = scores(j0 + 1)
                consume(s_a[...], j0)
                s_a[...] = scores(jnp.minimum(j0 + 2, nk - 1))
                consume(s_b[...], j0 + 1)
                return carry

            lax.fori_loop(0, nk // 2, pair, 0)
        l = l_s[...]
        o_ref[...] = (acc_s[...] * (1.0 / l)).T
        lse_ref[0] = m_s[...] + jnp.log2(l)

    nb = (_nbytes((tq, QW), BF16) + _nbytes((S, QW), BF16) + _nbytes((S, HP), BF16) + 3 * _nbytes((tq, HP), F32)
          + 4 * _nbytes((tq, tk), F32))
    return pl.pallas_call(
        body, name="flash_fwd", grid=(H, S // tq),
        in_specs=[pl.BlockSpec((tq, QW), lambda h, i: (i, h)), pl.BlockSpec((S, QW), lambda h, i: (0, h)),
                  pl.BlockSpec((1, nk, HP, tk), lambda h, i: (h, 0, 0, 0))],
        out_specs=[pl.BlockSpec((tq, HP), lambda h, i: (i, h)), pl.BlockSpec((1, 1, tq), lambda h, i: (h, 0, i))],
        out_shape=[jax.ShapeDtypeStruct((S, H * HP), F32), jax.ShapeDtypeStruct((H, 1, S), F32)],
        scratch_shapes=[pltpu.VMEM((1, tq), F32), pltpu.VMEM((1, tq), F32), pltpu.VMEM((HP, tq), F32),
                        pltpu.VMEM((tk, tq), F32), pltpu.VMEM((tk, tq), F32)],
        compiler_params=_params(("parallel", "parallel"), nb),
    )(Q, K, VT)


def _flash_bwd(Q, K, V, dO, O, lse):
    S = Q.shape[0]
    H = MLA_HEADS
    tq, tk = min(ATT_TQ, S), min(ATT_TK, S)
    nk = S // tk

    def body(q_ref, do_ref, o_ref, lse_ref, k_ref, v_ref, dq_ref, dk_ref, dv_ref, dq_s):
        @pl.when(pl.program_id(1) == 0)
        def _():
            dk_ref[...] = jnp.zeros(dk_ref.shape, F32)
            dv_ref[...] = jnp.zeros(dv_ref.shape, F32)

        delta = jnp.sum((do_ref[...].astype(F32) * o_ref[...]).T, axis=0, keepdims=True)
        lse = lse_ref[0]
        dq_s[...] = jnp.zeros(dq_s.shape, F32)

        def rows(j):
            return pl.ds(pl.multiple_of(j * tk, tk), tk)

        def scores(j):
            st = lax.dot_general(k_ref[rows(j), :], q_ref[...], NT_DIMS, preferred_element_type=F32)
            dpt = lax.dot_general(v_ref[rows(j), :], do_ref[...], NT_DIMS, preferred_element_type=F32)
            return st, dpt

        def consume(st, dpt, j):
            pt = jnp.exp2(st - lse)
            dst = (pt * (dpt - delta)).astype(BF16)
            dv_ref[rows(j), :] += jnp.dot(pt.astype(BF16), do_ref[...], preferred_element_type=F32)
            dk_ref[rows(j), :] += jnp.dot(dst, q_ref[...], preferred_element_type=F32)
            dq_s[...] += lax.dot_general(dst, k_ref[rows(j), :], TN_DIMS, preferred_element_type=F32)

        def step(j, carry):
            consume(*scores(j), j)
            return carry

        lax.fori_loop(0, nk, step, 0)
        dq_ref[...] = dq_s[...]

    nb = (_nbytes((tq, QW), BF16) + _nbytes((tq, HP), BF16) + _nbytes((tq, HP), F32) + _nbytes((S, QW), BF16)
          + _nbytes((S, HP), BF16) + 2 * _nbytes((tq, QW), F32) + _nbytes((S, QW), F32) + _nbytes((S, HP), F32))
    return pl.pallas_call(
        body, name="flash_bwd", grid=(H, S // tq),
        in_specs=[pl.BlockSpec((tq, QW), lambda h, i: (i, h)), pl.BlockSpec((tq, HP), lambda h, i: (i, h)),
                  pl.BlockSpec((tq, HP), lambda h, i: (i, h)), pl.BlockSpec((1, 1, tq), lambda h, i: (h, 0, i)),
                  pl.BlockSpec((S, QW), lambda h, i: (0, h)), pl.BlockSpec((S, HP), lambda h, i: (0, h))],
        out_specs=[pl.BlockSpec((tq, QW), lambda h, i: (i, h)), pl.BlockSpec((S, QW), lambda h, i: (0, h)),
                   pl.BlockSpec((S, HP), lambda h, i: (0, h))],
        out_shape=[jax.ShapeDtypeStruct((S, H * QW), F32), jax.ShapeDtypeStruct((S, H * QW), F32),
                   jax.ShapeDtypeStruct((S, H * HP), F32)],
        scratch_shapes=[pltpu.VMEM((tq, QW), F32)],
        compiler_params=_params(("parallel", "arbitrary"), nb),
    )(Q, dO, O, lse, K, V)


RET_CHUNK = 256


def _ret_tables(d, lg_ref, h, C):
    ii = lax.broadcasted_iota(jnp.int32, (C, C), 0).astype(F32)
    jj = lax.broadcasted_iota(jnp.int32, (C, C), 1).astype(F32)
    ci = lax.broadcasted_iota(jnp.int32, (C, 1), 0).astype(F32)
    fwd = d == 0
    dist = jnp.where(fwd, ii - jj, jj - ii)
    mask = dist >= jnp.where(fwd, 0.0, 1.0)
    dist = jnp.maximum(dist, 0.0)
    qpos = jnp.where(fwd, ci + 1.0, C - ci)
    kpos = jnp.where(fwd, C - 1.0 - ci, ci)
    lg = lg_ref[d, h]
    D = jnp.where(mask, jnp.exp(lg * dist), 0.0)
    return D, jnp.exp(lg * qpos), jnp.exp(lg * kpos), jnp.exp(lg * C), dist, qpos, kpos


def _ret_fwd(lg, q, k, v):
    S = q.shape[0]
    C = min(RET_CHUNK, S)
    N = S // C
    H = RET_HEADS

    def chunk(d, n):
        return jnp.where(d == 0, n, N - 1 - n)

    def body(lg_ref, q_ref, k_ref, v_ref, o_ref, st_ref, state):
        d, n = pl.program_id(0), pl.program_id(1)

        @pl.when(n == 0)
        def _():
            state[...] = jnp.zeros_like(state)

        for h in range(H):
            sl = slice(HP * h, HP * (h + 1))
            D, qw, kw, gc, _, _, _ = _ret_tables(d, lg_ref, h, C)
            qh, kh, vh = q_ref[:, sl], k_ref[:, sl], v_ref[:, sl]
            st = state[sl, :]
            sm = lax.dot_general(qh, kh, (((1,), (1,)), ((), ())), preferred_element_type=F32) * D
            inner = jnp.dot(sm.astype(BF16), vh, preferred_element_type=F32)
            cross = qw * jnp.dot(qh, st.astype(BF16), preferred_element_type=F32)
            o_ref[:, sl] = inner + cross
            st_ref[0, 0, sl, :] = st
            kvn = lax.dot_general((kh.astype(F32) * kw).astype(BF16), vh, (((0,), (0,)), ((), ())), preferred_element_type=F32)
            state[sl, :] = gc * st + kvn

    nb = 3 * _nbytes((C, 512), BF16) + _nbytes((C, 512), F32) + 2 * _nbytes((512, HP), F32) + 8 * _nbytes((C, C), F32)
    return pl.pallas_call(
        body, name="ret_fwd", grid=(2, N),
        in_specs=[pl.BlockSpec(memory_space=pltpu.SMEM)] + [pl.BlockSpec((C, 512), lambda d, n: (chunk(d, n), 0))] * 3,
        out_specs=[pl.BlockSpec((C, 512), lambda d, n: (chunk(d, n), d)),
                   pl.BlockSpec((1, 1, 512, HP), lambda d, n: (d, chunk(d, n), 0, 0))],
        out_shape=[jax.ShapeDtypeStruct((S, 1024), F32), jax.ShapeDtypeStruct((2, N, 512, HP), F32)],
        scratch_shapes=[pltpu.VMEM((512, HP), F32)],
        compiler_params=_params(("arbitrary", "arbitrary"), nb),
    )(lg, q, k, v)


def _ret_bwd(lg, q, k, v, do, states):
    S = q.shape[0]
    C = min(RET_CHUNK, S)
    N = S // C
    H = RET_HEADS

    def chunk(d, n):
        return jnp.where(d == 0, N - 1 - n, n)

    def body(lg_ref, q_ref, k_ref, v_ref, do_ref, st_ref, dq_ref, dk_ref, dv_ref, dlg_ref, G, accA, accB, accC):
        d, n = pl.program_id(0), pl.program_id(1)

        @pl.when(n == 0)
        def _():
            G[...] = jnp.zeros_like(G)
            accA[...] = jnp.zeros_like(accA)
            accB[...] = jnp.zeros_like(accB)
            accC[...] = jnp.zeros_like(accC)

        nt = (((1,), (1,)), ((), ()))
        tn = (((0,), (0,)), ((), ()))
        for h in range(H):
            sl = slice(HP * h, HP * (h + 1))
            D, qw, kw, gc, dist, qpos, kpos = _ret_tables(d, lg_ref, h, C)
            qh, kh, vh, doh = q_ref[:, sl], k_ref[:, sl], v_ref[:, sl], do_ref[:, sl]
            st = st_ref[0, 0, sl, :]
            g = G[sl, :]
            stb, gb = st.astype(BF16), g.astype(BF16)
            sraw = lax.dot_general(qh, kh, nt, preferred_element_type=F32)
            dS = lax.dot_general(doh, vh, nt, preferred_element_type=F32) * D
            dSb = dS.astype(BF16)
            smb = (sraw * D).astype(BF16)
            qs = jnp.dot(qh, stb, preferred_element_type=F32)
            kg = jnp.dot(kh, gb, preferred_element_type=F32)
            dq_ref[:, sl] = jnp.dot(dSb, kh, preferred_element_type=F32) + qw * lax.dot_general(doh, stb, nt, preferred_element_type=F32)
            dk_ref[:, sl] = lax.dot_general(dSb, qh, tn, preferred_element_type=F32) + kw * lax.dot_general(vh, gb, nt, preferred_element_type=F32)
            dv_ref[:, sl] = lax.dot_general(smb, doh, tn, preferred_element_type=F32) + kw * kg
            dof, vf = doh.astype(F32), vh.astype(F32)
            accA[h] += sraw * dS * dist
            accB[h] += (qw * qpos) * (qs * dof) + (kw * kpos) * (kg * vf)
            accC[h] += (C * gc) * (st * g)
            G[sl, :] = gc * g + lax.dot_general((qh.astype(F32) * qw).astype(BF16), doh, tn, preferred_element_type=F32)

        @pl.when(n == N - 1)
        def _():
            rows = [jnp.broadcast_to(jnp.sum(accA[h]) + jnp.sum(accB[h]) + jnp.sum(accC[h]), (1, HP)) for h in range(H)]
            dlg_ref[0] = jnp.concatenate(rows + [jnp.zeros((8 - H, HP), F32)], axis=0)

    nb = (4 * _nbytes((C, 512), BF16) + 3 * _nbytes((C, 512), F32) + 2 * _nbytes((512, HP), F32)
          + H * (_nbytes((C, C), F32) + _nbytes((C, HP), F32) + _nbytes((HP, HP), F32)) + 12 * _nbytes((C, C), F32))
    row = lambda d, n: (chunk(d, n), 0)
    out3 = lambda d, n: (chunk(d, n), d)
    return pl.pallas_call(
        body, name="ret_bwd", grid=(2, N),
        in_specs=[pl.BlockSpec(memory_space=pltpu.SMEM)] + [pl.BlockSpec((C, 512), row)] * 4
        + [pl.BlockSpec((1, 1, 512, HP), lambda d, n: (d, chunk(d, n), 0, 0))],
        out_specs=[pl.BlockSpec((C, 512), out3)] * 3 + [pl.BlockSpec((1, 8, HP), lambda d, n: (d, 0, 0))],
        out_shape=[jax.ShapeDtypeStruct((S, 1024), F32)] * 3 + [jax.ShapeDtypeStruct((2, 8, HP), F32)],
        scratch_shapes=[pltpu.VMEM((512, HP), F32), pltpu.VMEM((H, C, C), F32), pltpu.VMEM((H, C, HP), F32),
                        pltpu.VMEM((H, HP, HP), F32)],
        compiler_params=_params(("arbitrary", "arbitrary"), nb),
    )(lg, q, k, v, do, states)


CONV_PAD = 16
CONV_TR = 256
CONV_CB = LANES


def _fill_padded(pad, u_ref, S):
    pad[0:CONV_PAD, :] = jnp.zeros((CONV_PAD, CONV_CB), F32)
    pad[CONV_PAD + S:CONV_PAD + S + CONV_PAD, :] = jnp.zeros((CONV_PAD, CONV_CB), F32)
    pad[CONV_PAD:CONV_PAD + S, :] = u_ref[...]


def _conv_fwd(u, w32, b):
    S, D = u.shape
    tr = min(CONV_TR, S)

    def body(u_ref, w_ref, b_ref, o_ref, pad):
        _fill_padded(pad, u_ref, S)
        wv = w_ref[...]
        bv = b_ref[...]

        def step(t, carry):
            r0 = pl.multiple_of(t * tr, tr)
            acc = jnp.broadcast_to(bv, (tr, CONV_CB))
            for k in range(CONV_KERNEL):
                acc = acc + pad[pl.ds(r0 + (k + 1), tr), :] * wv[k:k + 1, :]
            o_ref[pl.ds(r0, tr), :] = acc
            return carry

        lax.fori_loop(0, S // tr, step, 0)

    nb = 2 * _nbytes((S, CONV_CB), F32) + _nbytes((S + 2 * CONV_PAD, CONV_CB), F32)
    return pl.pallas_call(
        body, name="conv_fwd", grid=(D // CONV_CB,),
        in_specs=[pl.BlockSpec((S, CONV_CB), lambda j: (0, j)), pl.BlockSpec((32, CONV_CB), lambda j: (0, j)),
                  pl.BlockSpec((1, CONV_CB), lambda j: (0, j))],
        out_specs=pl.BlockSpec((S, CONV_CB), lambda j: (0, j)),
        out_shape=jax.ShapeDtypeStruct((S, D), F32),
        scratch_shapes=[pltpu.VMEM((S + 2 * CONV_PAD, CONV_CB), F32)],
        compiler_params=_params(("parallel",), nb),
    )(u, w32, b)


def _conv_bwd_w(u, dout):
    S, D = u.shape
    tr = min(CONV_TR, S)

    def body(u_ref, d_ref, o_ref, pad, acc):
        _fill_padded(pad, u_ref, S)
        acc[...] = jnp.zeros_like(acc)

        def fold(a):
            return jnp.sum(a.reshape(tr // 8, 8, CONV_CB), axis=0)

        def step(t, carry):
            r0 = pl.multiple_of(t * tr, tr)
            dv = d_ref[pl.ds(r0, tr), :]
            for k in range(CONV_KERNEL):
                acc[8 * k:8 * k + 8, :] += fold(pad[pl.ds(r0 + (k + 1), tr), :] * dv)
            acc[8 * CONV_KERNEL:8 * CONV_KERNEL + 8, :] += fold(dv)
            return carry

        lax.fori_loop(0, S // tr, step, 0)
        o_ref[...] = jnp.sum(acc[...].reshape(32, 8, CONV_CB), axis=1)

    nb = 2 * _nbytes((S, CONV_CB), F32) + _nbytes((S + 2 * CONV_PAD, CONV_CB), F32)
    return pl.pallas_call(
        body, name="conv_bwd_w", grid=(D // CONV_CB,),
        in_specs=[pl.BlockSpec((S, CONV_CB), lambda j: (0, j)), pl.BlockSpec((S, CONV_CB), lambda j: (0, j))],
        out_specs=pl.BlockSpec((32, CONV_CB), lambda j: (0, j)),
        out_shape=jax.ShapeDtypeStruct((32, D), F32),
        scratch_shapes=[pltpu.VMEM((S + 2 * CONV_PAD, CONV_CB), F32), pltpu.VMEM((256, CONV_CB), F32)],
        compiler_params=_params(("parallel",), nb),
    )(u, dout)


def _adamw(w, g, m, v):
    shape = w.shape
    w2, g2, m2, v2 = [a.reshape(-1, shape[-1]) for a in (w, g, m, v)]
    W = shape[-1]

    def fn(rows, p):
        w_, g_, m_, v_ = [r[...] for r in rows]
        mn = ADAM_B1 * m_ + (1.0 - ADAM_B1) * g_
        vn = ADAM_B2 * v_ + (1.0 - ADAM_B2) * jnp.square(g_)
        m_hat = mn / (1.0 - ADAM_B1 ** ADAM_STEP)
        v_hat = vn / (1.0 - ADAM_B2 ** ADAM_STEP)
        delta = -ADAM_LR * (m_hat / (jnp.sqrt(v_hat) + ADAM_EPS) + ADAM_WD * w_)
        return [delta, mn, vn], []

    R2 = w2.shape[0]
    ts = 256 if (R2 > 256 and R2 % 256 == 0) else R2
    d, mn, vn = _rowcall("adamw", fn, [w2, g2, m2, v2], [], [(W, F32)] * 3, ts=ts)
    return d.reshape(shape), mn.reshape(shape), vn.reshape(shape)


_HBM = pl.BlockSpec(memory_space=pltpu.HBM)


def _place():
    return lax.axis_index("x"), lax.axis_index("y"), lax.axis_index("c")


def _half_rows(ref, half, h):
    idx = (slice(None),) * (len(ref.shape) - 2) + (pl.ds(half * h, h), slice(None))
    return ref.at[idx]


def _chip_exchange(name, p, mode):
    assert mode in ("half", "whole", "scatter")
    if mode == "half":
        h, W = p.shape[0] // 2, p.shape[1]
    elif mode == "whole":
        h, W = p.shape
    else:
        h, W = p.shape[1], p.shape[2]

    def body(p_ref, o_ref, send_sems, recv_sems, local_sem):
        x, y, c = _place()
        k_me = 2 * x + y

        def src(k):
            if mode == "half":
                return _half_rows(p_ref, c, h)
            return p_ref if mode == "whole" else p_ref.at[k]

        local = pltpu.make_async_copy(src(k_me), o_ref.at[k_me], local_sem)
        local.start()
        copies = []
        for j, (cx, cy) in enumerate([(1 - x, y), (x, 1 - y), (1 - x, 1 - y)]):
            cp = pltpu.make_async_remote_copy(
                src_ref=src(2 * cx + cy), dst_ref=o_ref.at[k_me], send_sem=send_sems.at[j], recv_sem=recv_sems.at[j],
                device_id=(cx, cy, c), device_id_type=MESH)
            cp.start()
            copies.append(cp)
        for cp in copies:
            cp.wait()
        local.wait()

    return pl.pallas_call(
        body, name=name, in_specs=[_HBM], out_specs=_HBM,
        out_shape=jax.ShapeDtypeStruct((4, h, W), p.dtype),
        scratch_shapes=[pltpu.SemaphoreType.DMA((3,)), pltpu.SemaphoreType.DMA((3,)), pltpu.SemaphoreType.DMA],
    )(p)


def _sibling_send_half(name, g):
    n, R, W = g.shape
    h = R // 2

    def body(g_ref, o_ref, send_sem, recv_sem):
        x, y, c = _place()
        cp = pltpu.make_async_remote_copy(
            src_ref=_half_rows(g_ref, 1 - c, h), dst_ref=o_ref, send_sem=send_sem, recv_sem=recv_sem,
            device_id=(x, y, 1 - c), device_id_type=MESH)
        cp.start()
        cp.wait()

    return pl.pallas_call(
        body, name=name, in_specs=[_HBM], out_specs=_HBM,
        out_shape=jax.ShapeDtypeStruct((n, h, W), g.dtype),
        scratch_shapes=[pltpu.SemaphoreType.DMA, pltpu.SemaphoreType.DMA],
    )(g)


def _halves_merge(name, mine):
    def body(m_ref, o_ref, send_sem, recv_sem):
        x, y, c = _place()
        cp = pltpu.make_async_remote_copy(
            src_ref=m_ref, dst_ref=o_ref, send_sem=send_sem, recv_sem=recv_sem,
            device_id=(x, y, 1 - c), device_id_type=MESH)
        cp.start()
        cp.wait()

    theirs = pl.pallas_call(
        body, name=name, in_specs=[_HBM], out_specs=_HBM,
        out_shape=jax.ShapeDtypeStruct(mine.shape, mine.dtype),
        scratch_shapes=[pltpu.SemaphoreType.DMA, pltpu.SemaphoreType.DMA],
    )(mine)
    south = lax.axis_index("c") == 0
    axis = mine.ndim - 2
    return jnp.concatenate([jnp.where(south, mine, theirs), jnp.where(south, theirs, mine)], axis=axis)


def _add2(name, a, b, out_dtype):
    def fn(rows, p):
        return [rows[0][...].astype(F32) + rows[1][...].astype(F32)], []

    return _rowcall(name, fn, [a, b], [], [(a.shape[1], out_dtype)], ts=PACK_TILE)[0]


def _sum4(name, b):
    _, h, W = b.shape
    ts = PACK_TILE if h % PACK_TILE == 0 else h

    def body(b_ref, o_ref):
        o_ref[...] = ((b_ref[0].astype(F32) + b_ref[1].astype(F32)) + b_ref[2].astype(F32)) + b_ref[3].astype(F32)

    nb = _nbytes((4, ts, W), b.dtype) + _nbytes((ts, W), F32)
    return pl.pallas_call(
        body, name=name, grid=(h // ts,),
        in_specs=[pl.BlockSpec((4, ts, W), lambda i: (0, i, 0))], out_specs=pl.BlockSpec((ts, W), lambda i: (i, 0)),
        out_shape=jax.ShapeDtypeStruct((h, W), F32), compiler_params=_params(("parallel",), nb),
    )(b)


PACK_W = 1024
PACK_ROWS = 16
PACK_TILE = 256
SHARDED = (("ev_w_in", "col", False), ("ev_w_uq", "col", False), ("ev_w_ukv", "col", False),
           ("ev_w_out", "row", False), ("od_w_in", "col", False), ("od_b_in", "col", True), ("od_dw_w", "col", True),
           ("od_dw_b", "col", True), ("od_ln_g", "col", True), ("od_ln_b", "col", True), ("od_w_out", "row", False))
REPLICATED = ("ada_b", "pre_g", "post_g", "ev_dec_f", "ev_dec_b", "ev_q_norm_g", "ev_kv_norm_g")
WEIGHTS = ("ada_w", "ada_b", "pre_g", "post_g", "ev_w_in", "ev_dec_f", "ev_dec_b", "ev_q_norm_g", "ev_w_uq", "ev_kv_norm_g",
           "ev_w_ukv", "ev_w_out", "od_w_in", "od_b_in", "od_dw_w", "od_dw_b", "od_ln_g", "od_ln_b", "od_w_out")


def _rows_of(n):
    unit = PACK_W * PACK_ROWS
    return (n + unit - 1) // unit * PACK_ROWS


def _to_rows(flat, lead):
    n = flat.shape[-1]
    r = _rows_of(n)
    flat = jnp.pad(flat, [(0, 0)] * len(lead) + [(0, r * PACK_W - n)])
    return flat.reshape(*lead, r, PACK_W)


def _pad_total(buf, axis):
    r = buf.shape[axis]
    r2 = (r + 2 * PACK_TILE - 1) // (2 * PACK_TILE) * (2 * PACK_TILE)
    pads = [(0, 0)] * buf.ndim
    pads[axis] = (0, r2 - r)
    return jnp.pad(buf, pads)


def _split_chips(full, how):
    if how == "col":
        n = full.shape[-1] // 4
        return jnp.moveaxis(full.reshape(*full.shape[:-1], 4, n), -2, 0)
    n = full.shape[1] // 4
    return jnp.moveaxis(full.reshape(full.shape[0], 4, n, *full.shape[2:]), 1, 0)


def _join_chips(blocks, how):
    if how == "col":
        t = jnp.moveaxis(blocks, 0, -2)
        return t.reshape(*t.shape[:-2], t.shape[-2] * t.shape[-1])
    t = jnp.moveaxis(blocks, 0, 1)
    return t.reshape(t.shape[0], t.shape[1] * t.shape[2], *t.shape[3:])


def _gather_weights(shards):
    parts = []
    for name, _, exact in SHARDED:
        a = shards[name].reshape(-1)
        a = lax.bitcast_convert_type(a, BF16).reshape(-1) if exact else a.astype(BF16)
        parts.append(_to_rows(a, ()))
    packed = _pad_total(jnp.concatenate(parts, axis=0), 0)
    gathered = _halves_merge("gather_merge", _chip_exchange("gather_chips", packed, "half"))
    out, r0 = {}, 0
    for (name, how, exact), part in zip(SHARDED, parts, strict=True):
        r = part.shape[0]
        shp = shards[name].shape
        n = shards[name].size * (2 if exact else 1)
        a = gathered[:, r0:r0 + r].reshape(4, -1)[:, :n]
        if exact:
            a = lax.bitcast_convert_type(a.reshape(4, -1, 2), F32)
        out[name] = _join_chips(a.reshape(4, *shp), how)
        r0 += r
    return out


def _reduce_grads(grads, shard_shapes):
    parts = []
    for name, how, _ in SHARDED:
        b = _split_chips(grads[name], how)
        parts.append(_to_rows(b.reshape(4, -1).astype(BF16), (4,)))
    for name in REPLICATED:
        a = _to_rows(grads[name].reshape(-1).astype(BF16), ())
        parts.append(jnp.broadcast_to(a[None], (4, *a.shape)))
    g = _pad_total(jnp.concatenate(parts, axis=1), 1)
    R = g.shape[1]
    h = R // 2
    c = lax.axis_index("c")
    theirs = _sibling_send_half("reduce_cores", g)
    mine = lax.dynamic_slice_in_dim(g, c * h, h, axis=1)
    chip_sum = _add2("reduce_add", mine.reshape(4 * h, PACK_W), theirs.reshape(4 * h, PACK_W), BF16).reshape(4, h, PACK_W)
    total = _halves_merge("reduce_merge", _sum4("reduce_sum", _chip_exchange("reduce_chips", chip_sum, "scatter")))
    out, r0 = {}, 0
    names = [n for n, _, _ in SHARDED] + list(REPLICATED)
    for name, part in zip(names, parts, strict=True):
        r = part.shape[1]
        shp = shard_shapes[name]
        n = 1
        for s in shp:
            n *= s
        out[name] = total[r0:r0 + r].reshape(-1)[:n].reshape(shp)
        r0 += r
    return out


def _pad_heads(a, n, w, to):
    lead = a.shape[:-1]
    return jnp.pad(a.reshape(*lead, n, w), [(0, 0)] * (len(lead) + 1) + [(0, to - w)]).reshape(*lead, n * to)


def _unpad_heads(a, n, w, to):
    lead = a.shape[:-1]
    return a.reshape(*lead, n, to)[..., :w].reshape(*lead, n * w)


def _w_in_pad(w):
    return jnp.concatenate([_pad_heads(w[:, 0:256], 4, 64, HP), _pad_heads(w[:, 256:512], 4, 64, HP), w[:, 512:1536],
                            w[:, 2240:2752], w[:, 1536:2176], _pad_heads(w[:, 2176:2240], 1, 64, HP)], axis=1)


def _w_in_unpad(g):
    return jnp.concatenate([_unpad_heads(g[:, Z_RQ:Z_RK], 4, 64, HP), _unpad_heads(g[:, Z_RK:Z_RV], 4, 64, HP),
                            g[:, Z_RV:Z_MG], g[:, Z_CQ:Z_KR], g[:, Z_KR:Z_KR + 64], g[:, Z_MG:Z_CQ]], axis=1)


def _prep_layer_weights(full):
    f32 = lambda a: a.astype(F32)
    ev, od = [], []
    for i in range(DEPTH // 2 + DEPTH % 2):
        w_in_p = _w_in_pad(full["ev_w_in"][i])
        w_uq_p = _pad_heads(full["ev_w_uq"][i], MLA_HEADS, MLA_NOPE_DIM + MLA_ROPE_DIM, QW)
        w_ukv, w_out = full["ev_w_ukv"][i], full["ev_w_out"][i]
        ev.append(dict(w_in=w_in_p, w_uq=w_uq_p, w_ukv=w_ukv, w_out=w_out,
                       qg=f32(full["ev_q_norm_g"][i])[None], kvg=f32(full["ev_kv_norm_g"][i])[None],
                       lg=jnp.stack([jax.nn.log_sigmoid(f32(full["ev_dec_f"][i])), jax.nn.log_sigmoid(f32(full["ev_dec_b"][i]))])))
    for i in range(DEPTH // 2):
        w_in, w_out = full["od_w_in"][i], full["od_w_out"][i]
        dw = jnp.pad(f32(full["od_dw_w"][i]), ((0, 1), (0, 0)))
        od.append(dict(w_in=w_in, w_out=w_out, b_in=f32(full["od_b_in"][i])[None],
                       dw=dw, dw_flip=jnp.pad(f32(full["od_dw_w"][i])[::-1], ((0, 1), (0, 0))), dw_b=f32(full["od_dw_b"][i])[None],
                       ln_g=f32(full["od_ln_g"][i])[None], ln_b=f32(full["od_ln_b"][i])[None]))
    return ev, od


def _rope_tables(positions):
    inv_freq = ROPE_BASE ** (-jnp.arange(0, ROPE_DIM, 2, dtype=F32) / ROPE_DIM)
    ang = positions.astype(F32)[:, None] * inv_freq
    z = jnp.zeros((ang.shape[0], HP - ROPE_DIM), F32)
    return jnp.concatenate([jnp.cos(ang), jnp.cos(ang), z], axis=1), jnp.concatenate([jnp.sin(ang), jnp.sin(ang), z], axis=1)


def _even_fwd(x, mod, bias, pre_g, post_g, cos, sin, w):
    h = _pre_fwd(x, pre_g, mod, bias)
    z = _mm_nn("ev_in", h, w["w_in"])
    rq, rk, rv, qn, kvn, kr = _evprep_fwd(z, cos, sin, w["qg"], w["kvg"])
    qf = _mm_nn("ev_uq", qn, w["w_uq"])
    kv = _mm_nn("ev_ukv", kvn, w["w_ukv"])
    Q, K, V = _attnprep_fwd(qf, kv, kr, cos, sin)
    a, lse = _flash_fwd(Q, K, V)
    o2, states = _ret_fwd(w["lg"], rq, rk, rv)
    mix = _mix_fwd(o2, z, a)
    y = _mm_nn("ev_out", mix, w["w_out"])
    x_new = _post_fwd(x, y, post_g, mod, bias)
    return x_new, dict(x=x, h=h, z=z, rq=rq, rk=rk, rv=rv, qn=qn, kvn=kvn, Q=Q, K=K, V=V, a=a, lse=lse, o2=o2,
                       states=states, mix=mix, y=y)


def _even_bwd(dx, s, mod, bias, pre_g, post_g, cos, sin, w):
    dy, dpost_g, dgate = _post_bwd(s["y"], dx, post_g, mod, bias)
    dmix = _mm_nt("ev_out_dx", dy, w["w_out"])
    dw_out = _mm_tn("ev_out_dw", s["mix"], dy)
    do, drg, da, dmg = _mix_bwd(s["o2"], s["z"], s["a"], dmix)
    drq, drk, drv, dlg = _ret_bwd(w["lg"], s["rq"], s["rk"], s["rv"], do, s["states"])
    dQ, dK, dV = _flash_bwd(s["Q"], s["K"], s["V"], da, s["a"], s["lse"])
    dqf, dkv, dkr = _attnprep_bwd(dQ, dK, dV, cos, sin)
    dqn = _mm_nt("ev_uq_dx", dqf, w["w_uq"])
    dw_uq = _mm_tn("ev_uq_dw", s["qn"], dqf)
    dkvn = _mm_nt("ev_ukv_dx", dkv, w["w_ukv"])
    dw_ukv = _mm_tn("ev_ukv_dw", s["kvn"], dkv)
    dz, dqg, dkvg = _evprep_bwd(s["z"], cos, sin, w["qg"], w["kvg"], drq, drk, drv, drg, dqn, dkvn, dkr, dmg)
    dh = _mm_nt("ev_in_dx", dz, w["w_in"])
    dw_in = _mm_tn("ev_in_dw", s["h"], dz)
    dx_new, dpre_g, dscale, dshift = _pre_bwd(s["x"], dh, dx, pre_g, mod, bias)
    g = dict(ev_w_in=_w_in_unpad(dw_in), ev_w_uq=_unpad_heads(dw_uq, MLA_HEADS, MLA_NOPE_DIM + MLA_ROPE_DIM, QW),
             ev_w_ukv=dw_ukv, ev_w_out=dw_out, ev_q_norm_g=dqg[0], ev_kv_norm_g=dkvg[0],
             dlg_f=dlg[0, :RET_HEADS, 0], dlg_b=dlg[1, :RET_HEADS, 0])
    return dx_new, g, dpre_g[0], dpost_g[0], jnp.concatenate([dshift, dscale, dgate], axis=1)


def _odd_fwd(x, mod, bias, pre_g, post_g, w):
    h = _pre_fwd(x, pre_g, mod, bias)
    z = _mm_nn("od_in", h, w["w_in"])
    u = _glu_fwd(z, w["b_in"])
    uc = _conv_fwd(u, w["dw"], w["dw_b"])
    vv = _odmix_fwd(uc, z, w["b_in"], w["ln_g"], w["ln_b"])
    y = _mm_nn("od_out", vv, w["w_out"])
    x_new = _post_fwd(x, y, post_g, mod, bias)
    return x_new, dict(x=x, h=h, z=z, u=u, uc=uc, vv=vv, y=y)


def _odd_bwd(dx, s, mod, bias, pre_g, post_g, w):
    dy, dpost_g, dgate = _post_bwd(s["y"], dx, post_g, mod, bias)
    dvv = _mm_nt("od_out_dx", dy, w["w_out"])
    dw_out = _mm_tn("od_out_dw", s["vv"], dy)
    duc, dg, dbg, dln_g, dln_b = _odmix_bwd(s["uc"], s["z"], w["b_in"], w["ln_g"], w["ln_b"], dvv)
    du = _conv_fwd(duc, w["dw_flip"], jnp.zeros_like(w["dw_b"]))
    dwb = _conv_bwd_w(s["u"], duc)
    dz, dba, dbb = _glu_bwd(s["z"], w["b_in"], du, dg)
    dh = _mm_nt("od_in_dx", dz, w["w_in"])
    dw_in = _mm_tn("od_in_dw", s["h"], dz)
    dx_new, dpre_g, dscale, dshift = _pre_bwd(s["x"], dh, dx, pre_g, mod, bias)
    g = dict(od_w_in=dw_in, od_b_in=jnp.concatenate([dba, dbb, dbg], axis=1)[0], od_dw_w=dwb[:CONV_KERNEL], od_dw_b=dwb[CONV_KERNEL],
             od_ln_g=dln_g[0], od_ln_b=dln_b[0], od_w_out=dw_out)
    return dx_new, g, dpre_g[0], dpost_g[0], jnp.concatenate([dshift, dscale, dgate], axis=1)


def _local_step(x, mod, positions, full, loss_target):
    cos, sin = _rope_tables(positions)
    ev, od = _prep_layer_weights(full)
    f32 = lambda a: a.astype(F32)
    mods = [mod[l:l + 1] for l in range(DEPTH)]
    biases = [f32(full["ada_b"][l])[None] for l in range(DEPTH)]
    pre = [f32(full["pre_g"][l])[None] for l in range(DEPTH)]
    post = [f32(full["post_g"][l])[None] for l in range(DEPTH)]

    saved = []
    for l in range(DEPTH):
        if l % 2 == 0:
            x, s = _even_fwd(x, mods[l], biases[l], pre[l], post[l], cos, sin, ev[l // 2])
        else:
            x, s = _odd_fwd(x, mods[l], biases[l], pre[l], post[l], od[l // 2])
        saved.append(s)
    dx, loss = _loss_head(x, loss_target)

    per_layer = [None] * DEPTH
    dpre, dpost, dmod = [None] * DEPTH, [None] * DEPTH, [None] * DEPTH
    for l in reversed(range(DEPTH)):
        if l % 2 == 0:
            dx, per_layer[l], dpre[l], dpost[l], dmod[l] = _even_bwd(dx, saved[l], mods[l], biases[l], pre[l], post[l], cos, sin, ev[l // 2])
        else:
            dx, per_layer[l], dpre[l], dpost[l], dmod[l] = _odd_bwd(dx, saved[l], mods[l], biases[l], pre[l], post[l], od[l // 2])

    grads = {"pre_g": jnp.stack(dpre), "post_g": jnp.stack(dpost), "ada_b": jnp.concatenate(dmod, axis=0)}
    evs =[per_layer[l] for l in range(0, DEPTH, 2)]
    ods = [per_layer[l] for l in range(1, DEPTH, 2)]
    for name in ("ev_w_in", "ev_w_uq", "ev_w_ukv", "ev_w_out", "ev_q_norm_g", "ev_kv_norm_g"):
        grads[name] = jnp.stack([g[name] for g in evs])
    for name in ("od_w_in", "od_b_in", "od_dw_w", "od_dw_b", "od_ln_g", "od_ln_b", "od_w_out"):
        grads[name] = jnp.stack([g[name] for g in ods])
    grads["ev_dec_f"] = jnp.stack([g["dlg_f"] for g in evs]) * jax.nn.sigmoid(-f32(full["ev_dec_f"]))
    grads["ev_dec_b"] = jnp.stack([g["dlg_b"] for g in evs]) * jax.nn.sigmoid(-f32(full["ev_dec_b"]))
    return loss[0, 0], dx, grads


ADA_ROWS = 16
ADA_COLS = 3 * D_MODEL // 4
ADA_PACK = (8, 1024)


def _ada_pack(a):
    n = ADA_PACK[0] * ADA_PACK[1]
    return jnp.pad(a, ((0, 0), (0, n - a.shape[1]))).reshape(4, *ADA_PACK)


def _ada_unpack(a):
    return a.reshape(4, -1)[:, :DEPTH * ADA_COLS]


def _ada_forward(c, ada_w):
    D = D_MODEL
    c16 = jnp.zeros((ADA_ROWS, D), F32).at[0].set(c[0])
    c_act = _rowcall("silu_c", lambda rows, p: ([_silu(rows[0][...])], []), [c16], [], [(D, BF16)])[0]
    got = _chip_exchange("ada_c", c_act, "whole")
    c4 = jnp.pad(got[:, 0, :], ((0, ADA_ROWS - 4), (0, 0)))
    per_layer = [_mm_nn("ada", c4, ada_w[l].astype(BF16))[:4] for l in range(DEPTH)]
    p = jnp.stack(per_layer, axis=1).reshape(4, DEPTH * ADA_COLS)
    got = _chip_exchange("ada_mod", _ada_pack(p), "scatter")
    mod = _ada_unpack(got).reshape(4, DEPTH, ADA_COLS).transpose(1, 0, 2).reshape(DEPTH, 3 * D)
    return mod, c4


def _ada_backward(dmod, c4):
    p = dmod.reshape(DEPTH, 4, ADA_COLS).transpose(1, 0, 2).reshape(4, DEPTH * ADA_COLS)
    got = _chip_exchange("ada_dmod", _ada_pack(p), "scatter")
    dm4 = jnp.pad(_ada_unpack(got), ((0, ADA_ROWS - 4), (0, 0)))
    both = _halves_merge("ada_merge", jnp.concatenate([dm4, c4.astype(F32)], axis=1))
    dm32, c32 = both[:, :DEPTH * ADA_COLS].astype(BF16), both[:, DEPTH * ADA_COLS:].astype(BF16)
    return jnp.stack([_mm_tn("ada_dw", c32, dm32[:, l * ADA_COLS:(l + 1) * ADA_COLS]) for l in range(DEPTH)])


def kernel(x, c, positions, ada_w, ada_b, pre_g, post_g, ev_w_in, ev_dec_f, ev_dec_b, ev_q_norm_g, ev_w_uq, ev_kv_norm_g, ev_w_ukv, ev_w_out, od_w_in, od_b_in, od_dw_w, od_dw_b, od_ln_g, od_ln_b, od_w_out, loss_target, m_ada_w, m_ada_b, m_pre_g, m_post_g, m_ev_w_in, m_ev_dec_f, m_ev_dec_b, m_ev_q_norm_g, m_ev_w_uq, m_ev_kv_norm_g, m_ev_w_ukv, m_ev_w_out, m_od_w_in, m_od_b_in, m_od_dw_w, m_od_dw_b, m_od_ln_g, m_od_ln_b, m_od_w_out, v_ada_w, v_ada_b, v_pre_g, v_post_g, v_ev_w_in, v_ev_dec_f, v_ev_dec_b, v_ev_q_norm_g, v_ev_w_uq, v_ev_kv_norm_g, v_ev_w_ukv, v_ev_w_out, v_od_w_in, v_od_b_in, v_od_dw_w, v_od_dw_b, v_od_ln_g, v_od_ln_b, v_od_w_out):
    w = dict(ada_w=ada_w, ada_b=ada_b, pre_g=pre_g, post_g=post_g, ev_w_in=ev_w_in, ev_dec_f=ev_dec_f, ev_dec_b=ev_dec_b,
             ev_q_norm_g=ev_q_norm_g, ev_w_uq=ev_w_uq, ev_kv_norm_g=ev_kv_norm_g, ev_w_ukv=ev_w_ukv, ev_w_out=ev_w_out,
             od_w_in=od_w_in, od_b_in=od_b_in, od_dw_w=od_dw_w, od_dw_b=od_dw_b, od_ln_g=od_ln_g, od_ln_b=od_ln_b, od_w_out=od_w_out)
    m = dict(ada_w=m_ada_w, ada_b=m_ada_b, pre_g=m_pre_g, post_g=m_post_g, ev_w_in=m_ev_w_in, ev_dec_f=m_ev_dec_f, ev_dec_b=m_ev_dec_b,
             ev_q_norm_g=m_ev_q_norm_g, ev_w_uq=m_ev_w_uq, ev_kv_norm_g=m_ev_kv_norm_g, ev_w_ukv=m_ev_w_ukv, ev_w_out=m_ev_w_out,
             od_w_in=m_od_w_in, od_b_in=m_od_b_in, od_dw_w=m_od_dw_w, od_dw_b=m_od_dw_b, od_ln_g=m_od_ln_g, od_ln_b=m_od_ln_b, od_w_out=m_od_w_out)
    v = dict(ada_w=v_ada_w, ada_b=v_ada_b, pre_g=v_pre_g, post_g=v_post_g, ev_w_in=v_ev_w_in, ev_dec_f=v_ev_dec_f, ev_dec_b=v_ev_dec_b,
             ev_q_norm_g=v_ev_q_norm_g, ev_w_uq=v_ev_w_uq, ev_kv_norm_g=v_ev_kv_norm_g, ev_w_ukv=v_ev_w_ukv, ev_w_out=v_ev_w_out,
             od_w_in=v_od_w_in, od_b_in=v_od_b_in, od_dw_w=v_od_dw_w, od_dw_b=v_od_dw_b, od_ln_g=v_od_ln_g, od_ln_b=v_od_ln_b, od_w_out=v_od_w_out)

    full = _gather_weights({name: w[name] for name, _, _ in SHARDED})
    for name in REPLICATED:
        full[name] = w[name]
    mod, c4 = _ada_forward(c, ada_w)
    loss_local, grad_x, grads = _local_step(x[0], mod, positions[0], full, loss_target[0])
    loss = lax.psum(loss_local, ("x", "y", "c"))
    g = _reduce_grads(grads, {name: w[name].shape for name in WEIGHTS})
    g["ada_w"] = _ada_backward(grads["ada_b"], c4)
    delta, new_m, new_v = {}, {}, {}
    for name in WEIGHTS:
        delta[name], new_m[name], new_v[name] = _adamw(w[name], g[name], m[name], v[name])
    return (loss, grad_x[None], *[g[n] for n in WEIGHTS], *[delta[n] for n in WEIGHTS],
            *[new_m[n] for n in WEIGHTS], *[new_v[n] for n in WEIGHTS])
```

```python
import functools

import jax
import jax.numpy as jnp
from jax import lax
from jax.experimental import pallas as pl
from jax.experimental.pallas import tpu as pltpu

F32 = jnp.float32
BF16 = jnp.bfloat16

D_MODEL = 1024
DEPTH = 4
RET_HEADS = 4
RET_QK_DIM = 64
RET_V_DIM = 128
MLA_HEADS = 4
MLA_Q_RANK = 384
MLA_KV_RANK = 256
MLA_NOPE_DIM = 128
MLA_ROPE_DIM = 64
MLA_V_DIM = 128
ROPE_DIM = 64
ROPE_BASE = 10000.0
CONV_KERNEL = 31
EPS = 1e-6
ADAM_LR, ADAM_B1, ADAM_B2, ADAM_EPS, ADAM_WD, ADAM_STEP = 0.001, 0.9, 0.999, 1e-08, 0.01, 10

LANES = 128
HP = 128
Z_RQ, Z_RK, Z_RV, Z_RG, Z_MG, Z_CQ, Z_CKV, Z_KR, Z_END = 0, 512, 1024, 1536, 2048, 2560, 2944, 3200, 3328
Z_LAT = Z_END - Z_CQ


def _z_ret(z):
    return (z, Z_RG, 0)


def _z_gates(z):
    return [(z, 512, Z_RG // 512), (z, 512, Z_MG // 512)]


def _z_latents(z):
    return [(z, 256, Z_CQ // 256 + i) for i in range(Z_LAT // 256)]
V7X_VMEM_BYTES = 64 * 1024 * 1024
VMEM_CAP = V7X_VMEM_BYTES - 8 * 1024 * 1024

MESH = pl.DeviceIdType.MESH


def _nbytes(shape, dtype):
    n = 1
    for s in shape:
        n *= s
    return n * jnp.dtype(dtype).itemsize


def _params(sem, block_bytes):
    limit = min(VMEM_CAP, max(32 * 1024 * 1024, 2 * block_bytes + 16 * 1024 * 1024))
    return pltpu.CompilerParams(dimension_semantics=sem, vmem_limit_bytes=limit)


ROW_TILE = 512


def _rowcall(name, fn, rows, params, outs, accs=(), ts=ROW_TILE):
    rows = [r if isinstance(r, tuple) else (r, r.shape[1], 0) for r in rows]
    S = rows[0][0].shape[0]
    ts = min(ts, S)
    assert S % ts == 0, (name, S, ts)
    nr, npar, no, na = len(rows), len(params), len(outs), len(accs)

    def body(*refs):
        row_refs = refs[:nr]
        pvals = [r[...] for r in refs[nr:nr + npar]]
        out_refs = refs[nr + npar:nr + npar + no]
        acc_refs = refs[nr + npar + no:]
        ovals, avals = fn(row_refs, pvals)
        for r, v in zip(out_refs, ovals, strict=True):
            r[...] = v.astype(r.dtype)
        if na:
            @pl.when(pl.program_id(0) == 0)
            def _():
                for r in acc_refs:
                    r[...] = jnp.zeros_like(r)
            for r, v in zip(acc_refs, avals, strict=True):
                r[...] += v

    in_specs = [pl.BlockSpec((ts, w), functools.partial(lambda i, b: (i, b), b=blk)) for _, w, blk in rows]
    in_specs += [pl.BlockSpec(p.shape, lambda i: (0, 0)) for p in params]
    out_specs = [pl.BlockSpec((ts, w), lambda i: (i, 0)) for w, _ in outs]
    out_specs += [pl.BlockSpec(s, lambda i: (0, 0)) for s in accs]
    out_shape = [jax.ShapeDtypeStruct((S, w), dt) for w, dt in outs]
    out_shape += [jax.ShapeDtypeStruct(s, F32) for s in accs]
    nb = sum(_nbytes((ts, w), a.dtype) for a, w, _ in rows) + sum(_nbytes((ts, w), dt) for w, dt in outs)
    nb += sum(_nbytes(p.shape, p.dtype) for p in params) + sum(_nbytes(s, F32) for s in accs)
    res = pl.pallas_call(
        body, name=name, grid=(S // ts,), in_specs=in_specs, out_specs=out_specs, out_shape=out_shape,
        compiler_params=_params(("arbitrary",) if na else ("parallel",), 3 * nb),
    )(*[a for a, _, _ in rows], *params)
    return res


def _silu(x):
    return x * jax.nn.sigmoid(x)


def _rms(x, g):
    return x * lax.rsqrt(jnp.mean(x * x, axis=-1, keepdims=True) + EPS) * g


def _rot(x):
    lane = lax.broadcasted_iota(jnp.int32, x.shape, 1)
    return jnp.where(lane < ROPE_DIM // 2, -pltpu.roll(x, LANES - ROPE_DIM // 2, 1), pltpu.roll(x, ROPE_DIM // 2, 1))


def _rope(x, cos, sin):
    return x * cos + _rot(x) * sin


def _rope_t(dy, cos, sin):
    return dy * cos - _rot(dy) * sin


def _groups(ref, start, n):
    return [ref[:, start + HP * h:start + HP * (h + 1)] for h in range(n)]


def _pre_math(x, g, m_scale, b_scale, m_shift, b_shift):
    return _rms(x, g) * (1.0 + (m_scale + b_scale)) + (m_shift + b_shift)


def _pre_fwd(x, g, mod, bias):
    D = D_MODEL

    def fn(rows, p):
        g_, mod_, b_ = p
        h = _pre_math(rows[0][...], g_, mod_[:, D:2 * D], b_[:, D:2 * D], mod_[:, :D], b_[:, :D])
        return [h], []

    return _rowcall("pre_fwd", fn, [x], [g, mod, bias], [(D, BF16)])[0]


def _pre_bwd(x, dh, dx_res, g, mod, bias):
    D = D_MODEL

    def fn(rows, p):
        g_, mod_, b_ = p
        xv = rows[0][...]
        _, vjp = jax.vjp(_pre_math, xv, g_, mod_[:, D:2 * D], b_[:, D:2 * D], mod_[:, :D], b_[:, :D])
        dx, dg, dsc, _, dsh, _ = vjp(rows[1][...].astype(F32))
        return [dx + rows[2][...]], [dg, dsc, dsh]

    return _rowcall("pre_bwd", fn, [x, dh, dx_res], [g, mod, bias], [(D, F32)], [(1, D)] * 3)


def _post_math(x, y, g, m_gate, b_gate):
    return x + (m_gate + b_gate) * _rms(y, g)


def _post_fwd(x, y, g, mod, bias):
    D = D_MODEL

    def fn(rows, p):
        g_, mod_, b_ = p
        return [_post_math(rows[0][...], rows[1][...], g_, mod_[:, 2 * D:], b_[:, 2 * D:])], []

    return _rowcall("post_fwd", fn, [x, y], [g, mod, bias], [(D, F32)])[0]


def _post_bwd(y, dxo, g, mod, bias):
    D = D_MODEL

    def fn(rows, p):
        g_, mod_, b_ = p
        yv = rows[0][...]
        _, vjp = jax.vjp(lambda y_, g2, mg: _post_math(0.0, y_, g2, mg, b_[:, 2 * D:]), yv, g_, mod_[:, 2 * D:])
        dy, dg, dgate = vjp(rows[1][...])
        return [dy], [dg, dgate]

    return _rowcall("post_bwd", fn, [y, dxo], [g, mod, bias], [(D, BF16)], [(1, D)] * 2)


def _evprep_fwd(z, cos, sin, qg, kvg):
    def fn(rows, p):
        z_, l0, l1, l2, c_, s_ = rows
        qg_, kvg_ = p
        cos_, sin_ = c_[...], s_[...]
        lat = jnp.concatenate([l0[...], l1[...], l2[...]], axis=1)
        rq = jnp.concatenate([_rope(v[...], cos_, sin_) for v in _groups(z_, Z_RQ, RET_HEADS)], axis=1)
        rk = jnp.concatenate([_rope(v[...], cos_, sin_) for v in _groups(z_, Z_RK, RET_HEADS)], axis=1)
        rk = rk * (RET_QK_DIM ** -0.5)
        rv = z_[:, Z_RV:Z_RG]
        qn = _rms(lat[:, :Z_CKV - Z_CQ], qg_)
        kvn = _rms(lat[:, Z_CKV - Z_CQ:Z_KR - Z_CQ], kvg_)
        kr = _rope(lat[:, Z_KR - Z_CQ:], cos_, sin_)
        return [rq, rk, rv, qn, kvn, kr], []

    return _rowcall("evprep_fwd", fn, [_z_ret(z), *_z_latents(z), cos, sin], [qg, kvg],
                    [(512, BF16), (512, BF16), (512, BF16), (MLA_Q_RANK, BF16), (MLA_KV_RANK, BF16), (HP, F32)])


def _evprep_bwd(z, cos, sin, qg, kvg, drq, drk, drv, drg, dqn, dkvn, dkr, dmg):
    def fn(rows, p):
        l0, l1, l2, c_, s_, drq_, drk_, drv_, drg_, dqn_, dkvn_, dkr_, dmg_ = rows
        qg_, kvg_ = p
        cos_, sin_ = c_[...], s_[...]
        lat = jnp.concatenate([l0[...], l1[...], l2[...]], axis=1)
        parts = []
        for h in range(RET_HEADS):
            parts.append(_rope_t(drq_[:, HP * h:HP * (h + 1)] + drq_[:, 512 + HP * h:512 + HP * (h + 1)], cos_, sin_))
        for h in range(RET_HEADS):
            t = drk_[:, HP * h:HP * (h + 1)] + drk_[:, 512 + HP * h:512 + HP * (h + 1)]
            parts.append(_rope_t(t, cos_, sin_) * (RET_QK_DIM ** -0.5))
        parts.append(drv_[:, :512] + drv_[:, 512:])
        parts += [drg_[...], dmg_[...]]
        _, vq = jax.vjp(_rms, lat[:, :Z_CKV - Z_CQ], qg_)
        dcq, dqg = vq(dqn_[...])
        _, vkv = jax.vjp(_rms, lat[:, Z_CKV - Z_CQ:Z_KR - Z_CQ], kvg_)
        dckv, dkvg = vkv(dkvn_[...])
        parts += [dcq, dckv, _rope_t(dkr_[...], cos_, sin_)]
        return [jnp.concatenate([v.astype(BF16) for v in parts], axis=1)], [dqg, dkvg]

    rows = [*_z_latents(z), cos, sin, drq, drk, drv, drg, dqn, dkvn, dkr, dmg]
    return _rowcall("evprep_bwd", fn, rows, [qg, kvg], [(Z_END, BF16)], [(1, MLA_Q_RANK), (1, MLA_KV_RANK)], ts=256)


ATT_SCALE = (MLA_NOPE_DIM + MLA_ROPE_DIM) ** -0.5
LOG2E = 1.4426950408889634
LN2 = 0.6931471805599453


def _attnprep_fwd(qf, kv, kr, cos, sin):
    qscale = ATT_SCALE * LOG2E

    def fn(rows, p):
        q_, kv_, kr_, c_, s_ = rows
        cos_, sin_ = c_[...], s_[...]
        krv = kr_[...]
        qs, ks, vs = [], [], []
        for h in range(MLA_HEADS):
            b = 2 * HP * h
            qs += [q_[:, b:b + HP] * qscale, _rope(q_[:, b + HP:b + 2 * HP], cos_, sin_) * qscale]
            ks += [kv_[:, b:b + HP], krv]
            vs += [kv_[:, b + HP:b + 2 * HP]]
        return [jnp.concatenate(qs, axis=1), jnp.concatenate(ks, axis=1), jnp.concatenate(vs, axis=1)], []

    return _rowcall("attnprep_fwd", fn, [qf, kv, kr, cos, sin], [], [(1024, BF16), (1024, BF16), (512, BF16)])


def _attnprep_bwd(dQ, dK, dV, cos, sin):
    def fn(rows, p):
        dq_, dk_, dv_, c_, s_ = rows
        cos_, sin_ = c_[...], s_[...]
        dqs, dkvs = [], []
        dkr = None
        for h in range(MLA_HEADS):
            b = 2 * HP * h
            dqs += [dq_[:, b:b + HP] * ATT_SCALE, _rope_t(dq_[:, b + HP:b + 2 * HP], cos_, sin_) * ATT_SCALE]
            dkvs += [dk_[:, b:b + HP] * LN2, dv_[:, HP * h:HP * (h + 1)]]
            t = dk_[:, b + HP:b + 2 * HP]
            dkr = t if dkr is None else dkr + t
        return [jnp.concatenate(dqs, axis=1), jnp.concatenate(dkvs, axis=1), dkr * LN2], []

    return _rowcall("attnprep_bwd", fn, [dQ, dK, dV, cos, sin], [], [(1024, BF16), (1024, BF16), (HP, F32)])


def _mix_math(o, rg, a, mg):
    outs = []
    for h in range(RET_HEADS):
        oh = o[:, HP * h:HP * (h + 1)]
        mu = jnp.mean(oh, axis=-1, keepdims=True)
        var = jnp.mean(jnp.square(oh - mu), axis=-1, keepdims=True)
        outs.append((oh - mu) * lax.rsqrt(var + EPS))
    ret = jnp.concatenate(outs, axis=1) * _silu(rg)
    return jnp.concatenate([ret, a * _silu(mg)], axis=1)


def _mix_fwd(o2, z, a):
    def fn(rows, p):
        o_, rg_, mg_, a_ = rows
        return [_mix_math(o_[:, :512] + o_[:, 512:], rg_[...], a_[...], mg_[...])], []

    return _rowcall("mix_fwd", fn, [o2, *_z_gates(z), a], [], [(1024, BF16)])[0]


def _mix_bwd(o2, z, a, dmix):
    def fn(rows, p):
        o_, rg_, mg_, a_, dm_ = rows
        _, vjp = jax.vjp(_mix_math, o_[:, :512] + o_[:, 512:], rg_[...], a_[...], mg_[...])
        return list(vjp(dm_[...].astype(F32))), []

    return _rowcall("mix_bwd", fn, [o2, *_z_gates(z), a, dmix], [], [(512, BF16), (512, F32), (512, BF16), (512, F32)])


def _glu_math(a, b, ba, bb):
    return (a + ba) * jax.nn.sigmoid(b + bb)


def _glu_fwd(z, bin_):
    D = D_MODEL

    def fn(rows, p):
        za_, zb_ = rows
        b_, = p
        return [_glu_math(za_[...], zb_[...], b_[:, :D], b_[:, D:2 * D])], []

    return _rowcall("glu_fwd", fn, [(z, D, 0), (z, D, 1)], [bin_], [(D, F32)])[0]


def _odmix_math(uc, g, bg, ln_g, ln_b):
    mu = jnp.mean(uc, axis=-1, keepdims=True)
    var = jnp.mean(jnp.square(uc - mu), axis=-1, keepdims=True)
    y = (uc - mu) * lax.rsqrt(var + EPS) * ln_g + ln_b
    return _silu(y) * _silu(g + bg)


def _odmix_fwd(uc, z, bin_, ln_g, ln_b):
    D = D_MODEL

    def fn(rows, p):
        uc_, zg_ = rows
        b_, g_, lb_ = p
        return [_odmix_math(uc_[...], zg_[...], b_[:, 2 * D:], g_, lb_)], []

    return _rowcall("odmix_fwd", fn, [uc, (z, D, 2)], [bin_, ln_g, ln_b], [(D, BF16)])[0]


def _odmix_bwd(uc, z, bin_, ln_g, ln_b, dv):
    D = D_MODEL

    def fn(rows, p):
        uc_, zg_, dv_ = rows
        b_, g_, lb_ = p
        _, vjp = jax.vjp(_odmix_math, uc_[...], zg_[...], b_[:, 2 * D:], g_, lb_)
        duc, dg, dbg, dlg, dlb = vjp(dv_[...].astype(F32))
        return [duc, dg], [dbg, dlg, dlb]

    return _rowcall("odmix_bwd", fn, [uc, (z, D, 2), dv], [bin_, ln_g, ln_b], [(D, F32), (D, BF16)], [(1, D)] * 3)


def _glu_bwd(z, bin_, du, dg):
    D = D_MODEL

    def fn(rows, p):
        za_, zb_, du_, dg_ = rows
        b_, = p
        _, vjp = jax.vjp(_glu_math, za_[...], zb_[...], b_[:, :D], b_[:, D:2 * D])
        da, db, dba, dbb = vjp(du_[...])
        return [jnp.concatenate([da.astype(BF16), db.astype(BF16), dg_[...]], axis=1)], [dba, dbb]

    return _rowcall("glu_bwd", fn, [(z, D, 0), (z, D, 1), du, dg], [bin_], [(3 * D, BF16)], [(1, D)] * 2)


def _loss_head(x, tgt):
    D = D_MODEL

    def fn(rows, p):
        err = rows[0][...] - rows[1][...]
        part = 0.5 * jnp.sum(jnp.mean(err * err, axis=-1, keepdims=True), axis=0, keepdims=True)
        return [err * (1.0 / D)], [jnp.broadcast_to(part, (1, LANES))]

    return _rowcall("loss_head", fn, [x, tgt], [], [(D, F32)], [(1, LANES)])


def _tile(n, cap):
    if n <= cap:
        return n
    best = None
    for t in range(LANES, cap + 1, LANES):
        if n % t == 0:
            best = t
    assert best is not None, (n, cap)
    return best


MM_TN_CAP = 1792
NT_DIMS = (((1,), (1,)), ((), ()))
TN_DIMS = (((0,), (0,)), ((), ()))


def _mm_nn(name, a, b, out_dtype=F32, tm=512, tn_cap=MM_TN_CAP):
    M, K = a.shape
    N = b.shape[1]
    tm = min(tm, M)
    tn = _tile(N, tn_cap)

    def body(a_ref, b_ref, o_ref):
        o_ref[...] = jnp.dot(a_ref[...], b_ref[...], preferred_element_type=F32).astype(o_ref.dtype)

    nb = _nbytes((tm, K), a.dtype) + _nbytes((K, tn), b.dtype) + _nbytes((tm, tn), out_dtype) + _nbytes((tm, tn), F32)
    return pl.pallas_call(
        body, name=name, grid=(N // tn, M // tm),
        in_specs=[pl.BlockSpec((tm, K), lambda j, i: (i, 0)), pl.BlockSpec((K, tn), lambda j, i: (0, j))],
        out_specs=pl.BlockSpec((tm, tn), lambda j, i: (i, j)),
        out_shape=jax.ShapeDtypeStruct((M, N), out_dtype),
        compiler_params=_params(("parallel", "parallel"), nb),
    )(a, b)


def _mm_nt(name, a, b, out_dtype=F32, tm=512, tn_cap=1024):
    M, K = a.shape
    N = b.shape[0]
    tm = min(tm, M)
    tn = _tile(N, tn_cap)

    def body(a_ref, b_ref, o_ref):
        o_ref[...] = lax.dot_general(a_ref[...], b_ref[...], NT_DIMS, preferred_element_type=F32).astype(o_ref.dtype)

    nb = _nbytes((tm, K), a.dtype) + _nbytes((tn, K), b.dtype) + _nbytes((tm, tn), out_dtype) + _nbytes((tm, tn), F32)
    return pl.pallas_call(
        body, name=name, grid=(N // tn, M // tm),
        in_specs=[pl.BlockSpec((tm, K), lambda j, i: (i, 0)), pl.BlockSpec((tn, K), lambda j, i: (j, 0))],
        out_specs=pl.BlockSpec((tm, tn), lambda j, i: (i, j)),
        out_shape=jax.ShapeDtypeStruct((M, N), out_dtype),
        compiler_params=_params(("parallel", "parallel"), nb),
    )(a, b)


def _mm_tn(name, a, b, ts=1024, tm_cap=512, tn_cap=MM_TN_CAP):
    S, M = a.shape
    N = b.shape[1]
    ts = min(ts, S)
    tm = _tile(M, tm_cap)
    tn = _tile(N, tn_cap)

    def body(a_ref, b_ref, o_ref):
        @pl.when(pl.program_id(2) == 0)
        def _():
            o_ref[...] = jnp.zeros_like(o_ref)

        o_ref[...] += lax.dot_general(a_ref[...], b_ref[...], (((0,), (0,)), ((), ())), preferred_element_type=F32)

    nb = _nbytes((ts, tm), a.dtype) + _nbytes((ts, tn), b.dtype) + 2 * _nbytes((tm, tn), F32)
    return pl.pallas_call(
        body, name=name, grid=(M // tm, N // tn, S // ts),
        in_specs=[pl.BlockSpec((ts, tm), lambda i, j, s: (s, i)), pl.BlockSpec((ts, tn), lambda i, j, s: (s, j))],
        out_specs=pl.BlockSpec((tm, tn), lambda i, j, s: (i, j)),
        out_shape=jax.ShapeDtypeStruct((M, N), F32),
        compiler_params=_params(("parallel", "parallel", "arbitrary"), nb),
    )(a, b)


ATT_TQ = 1024
ATT_TK = 1024
ATT_TK_FWD = 1024
QW = 2 * HP


def _kv_tiles(V, tk):
    S = V.shape[0]
    return V.reshape(S // tk, tk, MLA_HEADS, HP).transpose(2, 0, 3, 1)


def _flash_fwd(Q, K, V):
    S = Q.shape[0]
    H = MLA_HEADS
    tq, tk = min(ATT_TQ, S), min(ATT_TK_FWD, S)
    nk = S // tk
    VT = _kv_tiles(V, tk)

    def body(q_ref, k_ref, vt_ref, o_ref, lse_ref, m_s, l_s, acc_s, s_a, s_b):
        m_s[...] = jnp.full(m_s.shape, -jnp.inf, F32)
        l_s[...] = jnp.zeros(l_s.shape, F32)
        acc_s[...] = jnp.zeros(acc_s.shape, F32)

        def scores(j):
            k = k_ref[pl.ds(pl.multiple_of(j * tk, tk), tk), :]
            return lax.dot_general(k, q_ref[...], NT_DIMS, preferred_element_type=F32)

        def consume(st, j):
            m_prev = m_s[...]
            m_new = jnp.maximum(m_prev, jnp.max(st, axis=0, keepdims=True))
            alpha = jnp.exp2(m_prev - m_new)
            pt = jnp.exp2(st - m_new)
            l_s[...] = alpha * l_s[...] + jnp.sum(pt, axis=0, keepdims=True)
            acc_s[...] = alpha * acc_s[...] + jnp.dot(vt_ref[0, j], pt.astype(BF16), preferred_element_type=F32)
            m_s[...] = m_new

        if nk % 2:
            def step(j, carry):
                consume(scores(j), j)
                return carry

            lax.fori_loop(0, nk, step, 0)
        else:
            s_a[...] = scores(0)

            def pair(jj, carry):
                j0 = 2 * jj
                s_b[...] = scores(j0 + 1)
                consume(s_a[...], j0)
                s_a[...] = scores(jnp.minimum(j0 + 2, nk - 1))
                consume(s_b[...], j0 + 1)
                return carry

            lax.fori_loop(0, nk // 2, pair, 0)
        l = l_s[...]
        o_ref[...] = (acc_s[...] * (1.0 / l)).T
        lse_ref[0] = m_s[...] + jnp.log2(l)

    nb = (_nbytes((tq, QW), BF16) + _nbytes((S, QW), BF16) + _nbytes((S, HP), BF16) + 3 * _nbytes((tq, HP), F32)
          + 4 * _nbytes((tq, tk), F32))
    return pl.pallas_call(
        body, name="flash_fwd", grid=(H, S // tq),
        in_specs=[pl.BlockSpec((tq, QW), lambda h, i: (i, h)), pl.BlockSpec((S, QW), lambda h, i: (0, h)),
                  pl.BlockSpec((1, nk, HP, tk), lambda h, i: (h, 0, 0, 0))],
        out_specs=[pl.BlockSpec((tq, HP), lambda h, i: (i, h)), pl.BlockSpec((1, 1, tq), lambda h, i: (h, 0, i))],
        out_shape=[jax.ShapeDtypeStruct((S, H * HP), F32), jax.ShapeDtypeStruct((H, 1, S), F32)],
        scratch_shapes=[pltpu.VMEM((1, tq), F32), pltpu.VMEM((1, tq), F32), pltpu.VMEM((HP, tq), F32),
                        pltpu.VMEM((tk, tq), F32), pltpu.VMEM((tk, tq), F32)],
        compiler_params=_params(("parallel", "parallel"), nb),
    )(Q, K, VT)


def _flash_bwd(Q, K, V, dO, O, lse):
    S = Q.shape[0]
    H = MLA_HEADS
    tq, tk = min(ATT_TQ, S), min(ATT_TK, S)
    nk = S // tk

    def body(q_ref, do_ref, o_ref, lse_ref, k_ref, v_ref, dq_ref, dk_ref, dv_ref, dq_s):
        @pl.when(pl.program_id(1) == 0)
        def _():
            dk_ref[...] = jnp.zeros(dk_ref.shape, F32)
            dv_ref[...] = jnp.zeros(dv_ref.shape, F32)

        delta = jnp.sum((do_ref[...].astype(F32) * o_ref[...]).T, axis=0, keepdims=True)
        lse = lse_ref[0]
        dq_s[...] = jnp.zeros(dq_s.shape, F32)

        def rows(j):
            return pl.ds(pl.multiple_of(j * tk, tk), tk)

        def scores(j):
            st = lax.dot_general(k_ref[rows(j), :], q_ref[...], NT_DIMS, preferred_element_type=F32)
            dpt = lax.dot_general(v_ref[rows(j), :], do_ref[...], NT_DIMS, preferred_element_type=F32)
            return st, dpt

        def consume(st, dpt, j):
            pt = jnp.exp2(st - lse)
            dst = (pt * (dpt - delta)).astype(BF16)
            dv_ref[rows(j), :] += jnp.dot(pt.astype(BF16), do_ref[...], preferred_element_type=F32)
            dk_ref[rows(j), :] += jnp.dot(dst, q_ref[...], preferred_element_type=F32)
            dq_s[...] += lax.dot_general(dst, k_ref[rows(j), :], TN_DIMS, preferred_element_type=F32)

        def step(j, carry):
            consume(*scores(j), j)
            return carry

        lax.fori_loop(0, nk, step, 0)
        dq_ref[...] = dq_s[...]

    nb = (_nbytes((tq, QW), BF16) + _nbytes((tq, HP), BF16) + _nbytes((tq, HP), F32) + _nbytes((S, QW), BF16)
          + _nbytes((S, HP), BF16) + 2 * _nbytes((tq, QW), F32) + _nbytes((S, QW), F32) + _nbytes((S, HP), F32))
    return pl.pallas_call(
        body, name="flash_bwd", grid=(H, S // tq),
        in_specs=[pl.BlockSpec((tq, QW), lambda h, i: (i, h)), pl.BlockSpec((tq, HP), lambda h, i: (i, h)),
                  pl.BlockSpec((tq, HP), lambda h, i: (i, h)), pl.BlockSpec((1, 1, tq), lambda h, i: (h, 0, i)),
                  pl.BlockSpec((S, QW), lambda h, i: (0, h)), pl.BlockSpec((S, HP), lambda h, i: (0, h))],
        out_specs=[pl.BlockSpec((tq, QW), lambda h, i: (i, h)), pl.BlockSpec((S, QW), lambda h, i: (0, h)),
                   pl.BlockSpec((S, HP), lambda h, i: (0, h))],
        out_shape=[jax.ShapeDtypeStruct((S, H * QW), F32), jax.ShapeDtypeStruct((S, H * QW), F32),
                   jax.ShapeDtypeStruct((S, H * HP), F32)],
        scratch_shapes=[pltpu.VMEM((tq, QW), F32)],
        compiler_params=_params(("parallel", "arbitrary"), nb),
    )(Q, dO, O, lse, K, V)


RET_CHUNK = 256


def _ret_tables(d, lg_ref, h, C):
    ii = lax.broadcasted_iota(jnp.int32, (C, C), 0).astype(F32)
    jj = lax.broadcasted_iota(jnp.int32, (C, C), 1).astype(F32)
    ci = lax.broadcasted_iota(jnp.int32, (C, 1), 0).astype(F32)
    fwd = d == 0
    dist = jnp.where(fwd, ii - jj, jj - ii)
    mask = dist >= jnp.where(fwd, 0.0, 1.0)
    dist = jnp.maximum(dist, 0.0)
    qpos = jnp.where(fwd, ci + 1.0, C - ci)
    kpos = jnp.where(fwd, C - 1.0 - ci, ci)
    lg = lg_ref[d, h]
    D = jnp.where(mask, jnp.exp(lg * dist), 0.0)
    return D, jnp.exp(lg * qpos), jnp.exp(lg * kpos), jnp.exp(lg * C), dist, qpos, kpos


def _ret_fwd(lg, q, k, v):
    S = q.shape[0]
    C = min(RET_CHUNK, S)
    N = S // C
    H = RET_HEADS

    def chunk(d, n):
        return jnp.where(d == 0, n, N - 1 - n)

    def body(lg_ref, q_ref, k_ref, v_ref, o_ref, st_ref, state, tab_d, tab_q, tab_k):
        d, n = pl.program_id(0), pl.program_id(1)

        @pl.when(n == 0)
        def _():
            state[...] = jnp.zeros_like(state)
            for h in range(H):
                D, qw, kw, _, _, _, _ = _ret_tables(d, lg_ref, h, C)
                tab_d[h] = D
                tab_q[h] = jnp.broadcast_to(qw, (C, HP))
                tab_k[h] = jnp.broadcast_to(kw, (C, HP))

        for h in range(H):
            sl = slice(HP * h, HP * (h + 1))
            D, qw, kw = tab_d[h], tab_q[h], tab_k[h]
            gc = jnp.exp(lg_ref[d, h] * C)
            qh, kh, vh = q_ref[:, sl], k_ref[:, sl], v_ref[:, sl]
            st = state[sl, :]
            sm = lax.dot_general(qh, kh, (((1,), (1,)), ((), ())), preferred_element_type=F32) * D
            inner = jnp.dot(sm.astype(BF16), vh, preferred_element_type=F32)
            cross = qw * jnp.dot(qh, st.astype(BF16), preferred_element_type=F32)
            o_ref[:, sl] = inner + cross
            st_ref[0, 0, sl, :] = st
            kvn = lax.dot_general((kh.astype(F32) * kw).astype(BF16), vh, (((0,), (0,)), ((), ())), preferred_element_type=F32)
            state[sl, :] = gc * st + kvn

    nb = 3 * _nbytes((C, 512), BF16) + _nbytes((C, 512), F32) + 2 * _nbytes((512, HP), F32) + 8 * _nbytes((C, C), F32)
    return pl.pallas_call(
        body, name="ret_fwd", grid=(2, N),
        in_specs=[pl.BlockSpec(memory_space=pltpu.SMEM)] + [pl.BlockSpec((C, 512), lambda d, n: (chunk(d, n), 0))] * 3,
        out_specs=[pl.BlockSpec((C, 512), lambda d, n: (chunk(d, n), d)),
                   pl.BlockSpec((1, 1, 512, HP), lambda d, n: (d, chunk(d, n), 0, 0))],
        out_shape=[jax.ShapeDtypeStruct((S, 1024), F32), jax.ShapeDtypeStruct((2, N, 512, HP), F32)],
        scratch_shapes=[pltpu.VMEM((512, HP), F32), pltpu.VMEM((H, C, C), F32), pltpu.VMEM((H, C, HP), F32),
                        pltpu.VMEM((H, C, HP), F32)],
        compiler_params=_params(("arbitrary", "arbitrary"), nb),
    )(lg, q, k, v)


def _ret_bwd(lg, q, k, v, do, states):
    S = q.shape[0]
    C = min(RET_CHUNK, S)
    N = S // C
    H = RET_HEADS

    def chunk(d, n):
        return jnp.where(d == 0, N - 1 - n, n)

    def body(lg_ref, q_ref, k_ref, v_ref, do_ref, st_ref, dq_ref, dk_ref, dv_ref, dlg_ref, G, accA, accB, accC,
             tab_d, tab_q, tab_k, tab_qp, tab_kp):
        d, n = pl.program_id(0), pl.program_id(1)

        @pl.when(n == 0)
        def _():
            G[...] = jnp.zeros_like(G)
            accA[...] = jnp.zeros_like(accA)
            accB[...] = jnp.zeros_like(accB)
            accC[...] = jnp.zeros_like(accC)
            for h in range(H):
                D, qw, kw, _, _, qpos, kpos = _ret_tables(d, lg_ref, h, C)
                tab_d[h] = D
                tab_q[h] = jnp.broadcast_to(qw, (C, HP))
                tab_k[h] = jnp.broadcast_to(kw, (C, HP))
                tab_qp[h] = jnp.broadcast_to(qw * qpos, (C, HP))
                tab_kp[h] = jnp.broadcast_to(kw * kpos, (C, HP))

        nt = (((1,), (1,)), ((), ()))
        tn = (((0,), (0,)), ((), ()))
        for h in range(H):
            sl = slice(HP * h, HP * (h + 1))
            D, qw, kw = tab_d[h], tab_q[h], tab_k[h]
            gc = jnp.exp(lg_ref[d, h] * C)
            qh, kh, vh, doh = q_ref[:, sl], k_ref[:, sl], v_ref[:, sl], do_ref[:, sl]
            st = st_ref[0, 0, sl, :]
            g = G[sl, :]
            stb, gb = st.astype(BF16), g.astype(BF16)
            sraw = lax.dot_general(qh, kh, nt, preferred_element_type=F32)
            dS = lax.dot_general(doh, vh, nt, preferred_element_type=F32) * D
            dSb = dS.astype(BF16)
            smb = (sraw * D).astype(BF16)
            qs = jnp.dot(qh, stb, preferred_element_type=F32)
            kg = jnp.dot(kh, gb, preferred_element_type=F32)
            dq_ref[:, sl] = jnp.dot(dSb, kh, preferred_element_type=F32) + qw * lax.dot_general(doh, stb, nt, preferred_element_type=F32)
            dk_ref[:, sl] = lax.dot_general(dSb, qh, tn, preferred_element_type=F32) + kw * lax.dot_general(vh, gb, nt, preferred_element_type=F32)
            dv_ref[:, sl] = lax.dot_general(smb, doh, tn, preferred_element_type=F32) + kw * kg
            dof, vf = doh.astype(F32), vh.astype(F32)
            accA[h] += sraw * dS
            accB[h] += tab_qp[h] * (qs * dof) + tab_kp[h] * (kg * vf)
            accC[h] += st * g
            G[sl, :] = gc * g + lax.dot_general((qh.astype(F32) * qw).astype(BF16), doh, tn, preferred_element_type=F32)

        @pl.when(n == N - 1)
        def _():
            rows = []
            for h in range(H):
                _, _, _, gc, dist, _, _ = _ret_tables(d, lg_ref, h, C)
                tot = jnp.sum(accA[h] * dist) + jnp.sum(accB[h]) + (C * gc) * jnp.sum(accC[h])
                rows.append(jnp.broadcast_to(tot, (1, HP)))
            dlg_ref[0] = jnp.concatenate(rows + [jnp.zeros((8 - H, HP), F32)], axis=0)

    nb = (4 * _nbytes((C, 512), BF16) + 3 * _nbytes((C, 512), F32) + 2 * _nbytes((512, HP), F32)
          + H * (_nbytes((C, C), F32) + _nbytes((C, HP), F32) + _nbytes((HP, HP), F32)) + 12 * _nbytes((C, C), F32))
    row = lambda d, n: (chunk(d, n), 0)
    out3 = lambda d, n: (chunk(d, n), d)
    return pl.pallas_call(
        body, name="ret_bwd", grid=(2, N),
        in_specs=[pl.BlockSpec(memory_space=pltpu.SMEM)] + [pl.BlockSpec((C, 512), row)] * 4
        + [pl.BlockSpec((1, 1, 512, HP), lambda d, n: (d, chunk(d, n), 0, 0))],
        out_specs=[pl.BlockSpec((C, 512), out3)] * 3 + [pl.BlockSpec((1, 8, HP), lambda d, n: (d, 0, 0))],
        out_shape=[jax.ShapeDtypeStruct((S, 1024), F32)] * 3 + [jax.ShapeDtypeStruct((2, 8, HP), F32)],
        scratch_shapes=[pltpu.VMEM((512, HP), F32), pltpu.VMEM((H, C, C), F32), pltpu.VMEM((H, C, HP), F32),
                        pltpu.VMEM((H, HP, HP), F32), pltpu.VMEM((H, C, C), F32)] + [pltpu.VMEM((H, C, HP), F32)] * 4,
        compiler_params=_params(("arbitrary", "arbitrary"), nb),
    )(lg, q, k, v, do, states)


CONV_PAD = 16
CONV_TR = 256
CONV_CB = LANES


def _fill_padded(pad, u_ref, S):
    pad[0:CONV_PAD, :] = jnp.zeros((CONV_PAD, CONV_CB), F32)
    pad[CONV_PAD + S:CONV_PAD + S + CONV_PAD, :] = jnp.zeros((CONV_PAD, CONV_CB), F32)
    pad[CONV_PAD:CONV_PAD + S, :] = u_ref[...]


def _conv_fwd(u, w32, b):
    S, D = u.shape
    tr = min(CONV_TR, S)

    def body(u_ref, w_ref, b_ref, o_ref, pad):
        _fill_padded(pad, u_ref, S)
        wv = w_ref[...]
        bv = b_ref[...]

        def step(t, carry):
            r0 = pl.multiple_of(t * tr, tr)
            acc = jnp.broadcast_to(bv, (tr, CONV_CB))
            for k in range(CONV_KERNEL):
                acc = acc + pad[pl.ds(r0 + (k + 1), tr), :] * wv[k:k + 1, :]
            o_ref[pl.ds(r0, tr), :] = acc
            return carry

        lax.fori_loop(0, S // tr, step, 0)

    nb = 2 * _nbytes((S, CONV_CB), F32) + _nbytes((S + 2 * CONV_PAD, CONV_CB), F32)
    return pl.pallas_call(
        body, name="conv_fwd", grid=(D // CONV_CB,),
        in_specs=[pl.BlockSpec((S, CONV_CB), lambda j: (0, j)), pl.BlockSpec((32, CONV_CB), lambda j: (0, j)),
                  pl.BlockSpec((1, CONV_CB), lambda j: (0, j))],
        out_specs=pl.BlockSpec((S, CONV_CB), lambda j: (0, j)),
        out_shape=jax.ShapeDtypeStruct((S, D), F32),
        scratch_shapes=[pltpu.VMEM((S + 2 * CONV_PAD, CONV_CB), F32)],
        compiler_params=_params(("parallel",), nb),
    )(u, w32, b)


def _conv_bwd_w(u, dout):
    S, D = u.shape
    tr = min(CONV_TR, S)

    def body(u_ref, d_ref, o_ref, pad, acc):
        _fill_padded(pad, u_ref, S)
        acc[...] = jnp.zeros_like(acc)

        def fold(a):
            return jnp.sum(a.reshape(tr // 8, 8, CONV_CB), axis=0)

        def step(t, carry):
            r0 = pl.multiple_of(t * tr, tr)
            dv = d_ref[pl.ds(r0, tr), :]
            for k in range(CONV_KERNEL):
                acc[8 * k:8 * k + 8, :] += fold(pad[pl.ds(r0 + (k + 1), tr), :] * dv)
            acc[8 * CONV_KERNEL:8 * CONV_KERNEL + 8, :] += fold(dv)
            return carry

        lax.fori_loop(0, S // tr, step, 0)
        o_ref[...] = jnp.sum(acc[...].reshape(32, 8, CONV_CB), axis=1)

    nb = 2 * _nbytes((S, CONV_CB), F32) + _nbytes((S + 2 * CONV_PAD, CONV_CB), F32)
    return pl.pallas_call(
        body, name="conv_bwd_w", grid=(D // CONV_CB,),
        in_specs=[pl.BlockSpec((S, CONV_CB), lambda j: (0, j)), pl.BlockSpec((S, CONV_CB), lambda j: (0, j))],
        out_specs=pl.BlockSpec((32, CONV_CB), lambda j: (0, j)),
        out_shape=jax.ShapeDtypeStruct((32, D), F32),
        scratch_shapes=[pltpu.VMEM((S + 2 * CONV_PAD, CONV_CB), F32), pltpu.VMEM((256, CONV_CB), F32)],
        compiler_params=_params(("parallel",), nb),
    )(u, dout)


def _adamw(w, g, m, v):
    shape = w.shape
    w2, g2, m2, v2 = [a.reshape(-1, shape[-1]) for a in (w, g, m, v)]
    W = shape[-1]

    def fn(rows, p):
        w_, g_, m_, v_ = [r[...] for r in rows]
        mn = ADAM_B1 * m_ + (1.0 - ADAM_B1) * g_
        vn = ADAM_B2 * v_ + (1.0 - ADAM_B2) * jnp.square(g_)
        m_hat = mn / (1.0 - ADAM_B1 ** ADAM_STEP)
        v_hat = vn / (1.0 - ADAM_B2 ** ADAM_STEP)
        delta = -ADAM_LR * (m_hat / (jnp.sqrt(v_hat) + ADAM_EPS) + ADAM_WD * w_)
        return [delta, mn, vn], []

    R2 = w2.shape[0]
    ts = 256 if (R2 > 256 and R2 % 256 == 0) else R2
    d, mn, vn = _rowcall("adamw", fn, [w2, g2, m2, v2], [], [(W, F32)] * 3, ts=ts)
    return d.reshape(shape), mn.reshape(shape), vn.reshape(shape)


_HBM = pl.BlockSpec(memory_space=pltpu.HBM)


def _place():
    return lax.axis_index("x"), lax.axis_index("y"), lax.axis_index("c")


def _half_rows(ref, half, h):
    idx = (slice(None),) * (len(ref.shape) - 2) + (pl.ds(half * h, h), slice(None))
    return ref.at[idx]


def _chip_exchange(name, p, mode):
    assert mode in ("half", "whole", "scatter")
    if mode == "half":
        h, W = p.shape[0] // 2, p.shape[1]
    elif mode == "whole":
        h, W = p.shape
    else:
        h, W = p.shape[1], p.shape[2]

    def body(p_ref, o_ref, send_sems, recv_sems, local_sem):
        x, y, c = _place()
        k_me = 2 * x + y

        def src(k):
            if mode == "half":
                return _half_rows(p_ref, c, h)
            return p_ref if mode == "whole" else p_ref.at[k]

        local = pltpu.make_async_copy(src(k_me), o_ref.at[k_me], local_sem)
        local.start()
        copies = []
        for j, (cx, cy) in enumerate([(1 - x, y), (x, 1 - y), (1 - x, 1 - y)]):
            cp = pltpu.make_async_remote_copy(
                src_ref=src(2 * cx + cy), dst_ref=o_ref.at[k_me], send_sem=send_sems.at[j], recv_sem=recv_sems.at[j],
                device_id=(cx, cy, c), device_id_type=MESH)
            cp.start()
            copies.append(cp)
        for cp in copies:
            cp.wait()
        local.wait()

    return pl.pallas_call(
        body, name=name, in_specs=[_HBM], out_specs=_HBM,
        out_shape=jax.ShapeDtypeStruct((4, h, W), p.dtype),
        scratch_shapes=[pltpu.SemaphoreType.DMA((3,)), pltpu.SemaphoreType.DMA((3,)), pltpu.SemaphoreType.DMA],
    )(p)


def _sibling_send_half(name, g):
    n, R, W = g.shape
    h = R // 2

    def body(g_ref, o_ref, send_sem, recv_sem):
        x, y, c = _place()
        cp = pltpu.make_async_remote_copy(
            src_ref=_half_rows(g_ref, 1 - c, h), dst_ref=o_ref, send_sem=send_sem, recv_sem=recv_sem,
            device_id=(x, y, 1 - c), device_id_type=MESH)
        cp.start()
        cp.wait()

    return pl.pallas_call(
        body, name=name, in_specs=[_HBM], out_specs=_HBM,
        out_shape=jax.ShapeDtypeStruct((n, h, W), g.dtype),
        scratch_shapes=[pltpu.SemaphoreType.DMA, pltpu.SemaphoreType.DMA],
    )(g)


def _halves_merge(name, mine):
    def body(m_ref, o_ref, send_sem, recv_sem):
        x, y, c = _place()
        cp = pltpu.make_async_remote_copy(
            src_ref=m_ref, dst_ref=o_ref, send_sem=send_sem, recv_sem=recv_sem,
            device_id=(x, y, 1 - c), device_id_type=MESH)
        cp.start()
        cp.wait()

    theirs = pl.pallas_call(
        body, name=name, in_specs=[_HBM], out_specs=_HBM,
        out_shape=jax.ShapeDtypeStruct(mine.shape, mine.dtype),
        scratch_shapes=[pltpu.SemaphoreType.DMA, pltpu.SemaphoreType.DMA],
    )(mine)
    south = lax.axis_index("c") == 0
    axis = mine.ndim - 2
    return jnp.concatenate([jnp.where(south, mine, theirs), jnp.where(south, theirs, mine)], axis=axis)


def _add2(name, a, b, out_dtype):
    def fn(rows, p):
        return [rows[0][...].astype(F32) + rows[1][...].astype(F32)], []

    return _rowcall(name, fn, [a, b], [], [(a.shape[1], out_dtype)], ts=PACK_TILE)[0]


def _sum4(name, b):
    _, h, W = b.shape
    ts = PACK_TILE if h % PACK_TILE == 0 else h

    def body(b_ref, o_ref):
        o_ref[...] = ((b_ref[0].astype(F32) + b_ref[1].astype(F32)) + b_ref[2].astype(F32)) + b_ref[3].astype(F32)

    nb = _nbytes((4, ts, W), b.dtype) + _nbytes((ts, W), F32)
    return pl.pallas_call(
        body, name=name, grid=(h // ts,),
        in_specs=[pl.BlockSpec((4, ts, W), lambda i: (0, i, 0))], out_specs=pl.BlockSpec((ts, W), lambda i: (i, 0)),
        out_shape=jax.ShapeDtypeStruct((h, W), F32), compiler_params=_params(("parallel",), nb),
    )(b)


PACK_W = 1024
PACK_ROWS = 16
PACK_TILE = 256
SHARDED = (("ev_w_in", "col", False), ("ev_w_uq", "col", False), ("ev_w_ukv", "col", False),
           ("ev_w_out", "row", False), ("od_w_in", "col", False), ("od_b_in", "col", True), ("od_dw_w", "col", True),
           ("od_dw_b", "col", True), ("od_ln_g", "col", True), ("od_ln_b", "col", True), ("od_w_out", "row", False))
REPLICATED = ("ada_b", "pre_g", "post_g", "ev_dec_f", "ev_dec_b", "ev_q_norm_g", "ev_kv_norm_g")
WEIGHTS = ("ada_w", "ada_b", "pre_g", "post_g", "ev_w_in", "ev_dec_f", "ev_dec_b", "ev_q_norm_g", "ev_w_uq", "ev_kv_norm_g",
           "ev_w_ukv", "ev_w_out", "od_w_in", "od_b_in", "od_dw_w", "od_dw_b", "od_ln_g", "od_ln_b", "od_w_out")


def _rows_of(n):
    unit = PACK_W * PACK_ROWS
    return (n + unit - 1) // unit * PACK_ROWS


def _to_rows(flat, lead):
    n = flat.shape[-1]
    r = _rows_of(n)
    flat = jnp.pad(flat, [(0, 0)] * len(lead) + [(0, r * PACK_W - n)])
    return flat.reshape(*lead, r, PACK_W)


def _pad_total(buf, axis):
    r = buf.shape[axis]
    r2 = (r + 2 * PACK_TILE - 1) // (2 * PACK_TILE) * (2 * PACK_TILE)
    pads = [(0, 0)] * buf.ndim
    pads[axis] = (0, r2 - r)
    return jnp.pad(buf, pads)


def _split_chips(full, how):
    if how == "col":
        n = full.shape[-1] // 4
        return jnp.moveaxis(full.reshape(*full.shape[:-1], 4, n), -2, 0)
    n = full.shape[1] // 4
    return jnp.moveaxis(full.reshape(full.shape[0], 4, n, *full.shape[2:]), 1, 0)


def _join_chips(blocks, how):
    if how == "col":
        t = jnp.moveaxis(blocks, 0, -2)
        return t.reshape(*t.shape[:-2], t.shape[-2] * t.shape[-1])
    t = jnp.moveaxis(blocks, 0, 1)
    return t.reshape(t.shape[0], t.shape[1] * t.shape[2], *t.shape[3:])


def _gather_weights(shards):
    parts = []
    for name, _, exact in SHARDED:
        a = shards[name].reshape(-1)
        a = lax.bitcast_convert_type(a, BF16).reshape(-1) if exact else a.astype(BF16)
        parts.append(_to_rows(a, ()))
    packed = _pad_total(jnp.concatenate(parts, axis=0), 0)
    gathered = _halves_merge("gather_merge", _chip_exchange("gather_chips", packed, "half"))
    out, r0 = {}, 0
    for (name, how, exact), part in zip(SHARDED, parts, strict=True):
        r = part.shape[0]
        shp = shards[name].shape
        n = shards[name].size * (2 if exact else 1)
        a = gathered[:, r0:r0 + r].reshape(4, -1)[:, :n]
        if exact:
            a = lax.bitcast_convert_type(a.reshape(4, -1, 2), F32)
        out[name] = _join_chips(a.reshape(4, *shp), how)
        r0 += r
    return out


def _reduce_grads(grads, shard_shapes):
    parts = []
    for name, how, _ in SHARDED:
        b = _split_chips(grads[name], how)
        parts.append(_to_rows(b.reshape(4, -1).astype(BF16), (4,)))
    for name in REPLICATED:
        a = _to_rows(grads[name].reshape(-1).astype(BF16), ())
        parts.append(jnp.broadcast_to(a[None], (4, *a.shape)))
    g = _pad_total(jnp.concatenate(parts, axis=1), 1)
    R = g.shape[1]
    h = R // 2
    c = lax.axis_index("c")
    theirs = _sibling_send_half("reduce_cores", g)
    mine = lax.dynamic_slice_in_dim(g, c * h, h, axis=1)
    chip_sum = _add2("reduce_add", mine.reshape(4 * h, PACK_W), theirs.reshape(4 * h, PACK_W), BF16).reshape(4, h, PACK_W)
    total = _halves_merge("reduce_merge", _sum4("reduce_sum", _chip_exchange("reduce_chips", chip_sum, "scatter")))
    out, r0 = {}, 0
    names = [n for n, _, _ in SHARDED] + list(REPLICATED)
    for name, part in zip(names, parts, strict=True):
        r = part.shape[1]
        shp = shard_shapes[name]
        n = 1
        for s in shp:
            n *= s
        out[name] = total[r0:r0 + r].reshape(-1)[:n].reshape(shp)
        r0 += r
    return out


def _pad_heads(a, n, w, to):
    lead = a.shape[:-1]
    return jnp.pad(a.reshape(*lead, n, w), [(0, 0)] * (len(lead) + 1) + [(0, to - w)]).reshape(*lead, n * to)


def _unpad_heads(a, n, w, to):
    lead = a.shape[:-1]
    return a.reshape(*lead, n, to)[..., :w].reshape(*lead, n * w)


def _w_in_pad(w):
    return jnp.concatenate([_pad_heads(w[:, 0:256], 4, 64, HP), _pad_heads(w[:, 256:512], 4, 64, HP), w[:, 512:1536],
                            w[:, 2240:2752], w[:, 1536:2176], _pad_heads(w[:, 2176:2240], 1, 64, HP)], axis=1)


def _w_in_unpad(g):
    return jnp.concatenate([_unpad_heads(g[:, Z_RQ:Z_RK], 4, 64, HP), _unpad_heads(g[:, Z_RK:Z_RV], 4, 64, HP),
                            g[:, Z_RV:Z_MG], g[:, Z_CQ:Z_KR], g[:, Z_KR:Z_KR + 64], g[:, Z_MG:Z_CQ]], axis=1)


def _prep_layer_weights(full):
    f32 = lambda a: a.astype(F32)
    ev, od = [], []
    for i in range(DEPTH // 2 + DEPTH % 2):
        w_in_p = _w_in_pad(full["ev_w_in"][i])
        w_uq_p = _pad_heads(full["ev_w_uq"][i], MLA_HEADS, MLA_NOPE_DIM + MLA_ROPE_DIM, QW)
        w_ukv, w_out = full["ev_w_ukv"][i], full["ev_w_out"][i]
        ev.append(dict(w_in=w_in_p, w_uq=w_uq_p, w_ukv=w_ukv, w_out=w_out,
                       qg=f32(full["ev_q_norm_g"][i])[None], kvg=f32(full["ev_kv_norm_g"][i])[None],
                       lg=jnp.stack([jax.nn.log_sigmoid(f32(full["ev_dec_f"][i])), jax.nn.log_sigmoid(f32(full["ev_dec_b"][i]))])))
    for i in range(DEPTH // 2):
        w_in, w_out = full["od_w_in"][i], full["od_w_out"][i]
        dw = jnp.pad(f32(full["od_dw_w"][i]), ((0, 1), (0, 0)))
        od.append(dict(w_in=w_in, w_out=w_out, b_in=f32(full["od_b_in"][i])[None],
                       dw=dw, dw_flip=jnp.pad(f32(full["od_dw_w"][i])[::-1], ((0, 1), (0, 0))), dw_b=f32(full["od_dw_b"][i])[None],
                       ln_g=f32(full["od_ln_g"][i])[None], ln_b=f32(full["od_ln_b"][i])[None]))
    return ev, od


def _rope_tables(positions):
    inv_freq = ROPE_BASE ** (-jnp.arange(0, ROPE_DIM, 2, dtype=F32) / ROPE_DIM)
    ang = positions.astype(F32)[:, None] * inv_freq
    z = jnp.zeros((ang.shape[0], HP - ROPE_DIM), F32)
    return jnp.concatenate([jnp.cos(ang), jnp.cos(ang), z], axis=1), jnp.concatenate([jnp.sin(ang), jnp.sin(ang), z], axis=1)


def _even_fwd(x, mod, bias, pre_g, post_g, cos, sin, w):
    h = _pre_fwd(x, pre_g, mod, bias)
    z = _mm_nn("ev_in", h, w["w_in"])
    rq, rk, rv, qn, kvn, kr = _evprep_fwd(z, cos, sin, w["qg"], w["kvg"])
    qf = _mm_nn("ev_uq", qn, w["w_uq"])
    kv = _mm_nn("ev_ukv", kvn, w["w_ukv"])
    Q, K, V = _attnprep_fwd(qf, kv, kr, cos, sin)
    a, lse = _flash_fwd(Q, K, V)
    o2, states = _ret_fwd(w["lg"], rq, rk, rv)
    mix = _mix_fwd(o2, z, a)
    y = _mm_nn("ev_out", mix, w["w_out"])
    x_new = _post_fwd(x, y, post_g, mod, bias)
    return x_new, dict(x=x, h=h, z=z, rq=rq, rk=rk, rv=rv, qn=qn, kvn=kvn, Q=Q, K=K, V=V, a=a, lse=lse, o2=o2,
                       states=states, mix=mix, y=y)


def _even_bwd(dx, s, mod, bias, pre_g, post_g, cos, sin, w):
    dy, dpost_g, dgate = _post_bwd(s["y"], dx, post_g, mod, bias)
    dmix = _mm_nt("ev_out_dx", dy, w["w_out"], out_dtype=BF16)
    dw_out = _mm_tn("ev_out_dw", s["mix"], dy)
    do, drg, da, dmg = _mix_bwd(s["o2"], s["z"], s["a"], dmix)
    drq, drk, drv, dlg = _ret_bwd(w["lg"], s["rq"], s["rk"], s["rv"], do, s["states"])
    dQ, dK, dV = _flash_bwd(s["Q"], s["K"], s["V"], da, s["a"], s["lse"])
    dqf, dkv, dkr = _attnprep_bwd(dQ, dK, dV, cos, sin)
    dqn = _mm_nt("ev_uq_dx", dqf, w["w_uq"])
    dw_uq = _mm_tn("ev_uq_dw", s["qn"], dqf)
    dkvn = _mm_nt("ev_ukv_dx", dkv, w["w_ukv"])
    dw_ukv = _mm_tn("ev_ukv_dw", s["kvn"], dkv)
    dz, dqg, dkvg = _evprep_bwd(s["z"], cos, sin, w["qg"], w["kvg"], drq, drk, drv, drg, dqn, dkvn, dkr, dmg)
    dh = _mm_nt("ev_in_dx", dz, w["w_in"], out_dtype=BF16)
    dw_in = _mm_tn("ev_in_dw", s["h"], dz)
    dx_new, dpre_g, dscale, dshift = _pre_bwd(s["x"], dh, dx, pre_g, mod, bias)
    g = dict(ev_w_in=_w_in_unpad(dw_in), ev_w_uq=_unpad_heads(dw_uq, MLA_HEADS, MLA_NOPE_DIM + MLA_ROPE_DIM, QW),
             ev_w_ukv=dw_ukv, ev_w_out=dw_out, ev_q_norm_g=dqg[0], ev_kv_norm_g=dkvg[0],
             dlg_f=dlg[0, :RET_HEADS, 0], dlg_b=dlg[1, :RET_HEADS, 0])
    return dx_new, g, dpre_g[0], dpost_g[0], jnp.concatenate([dshift, dscale, dgate], axis=1)


def _odd_fwd(x, mod, bias, pre_g, post_g, w):
    h = _pre_fwd(x, pre_g, mod, bias)
    z = _mm_nn("od_in", h, w["w_in"])
    u = _glu_fwd(z, w["b_in"])
    uc = _conv_fwd(u, w["dw"], w["dw_b"])
    vv = _odmix_fwd(uc, z, w["b_in"], w["ln_g"], w["ln_b"])
    y = _mm_nn("od_out", vv, w["w_out"])
    x_new = _post_fwd(x, y, post_g, mod, bias)
    return x_new, dict(x=x, h=h, z=z, u=u, uc=uc, vv=vv, y=y)


def _odd_bwd(dx, s, mod, bias, pre_g, post_g, w):
    dy, dpost_g, dgate = _post_bwd(s["y"], dx, post_g, mod, bias)
    dvv = _mm_nt("od_out_dx", dy, w["w_out"], out_dtype=BF16)
    dw_out = _mm_tn("od_out_dw", s["vv"], dy)
    duc, dg, dbg, dln_g, dln_b = _odmix_bwd(s["uc"], s["z"], w["b_in"], w["ln_g"], w["ln_b"], dvv)
    du = _conv_fwd(duc, w["dw_flip"], jnp.zeros_like(w["dw_b"]))
    dwb = _conv_bwd_w(s["u"], duc)
    dz, dba, dbb = _glu_bwd(s["z"], w["b_in"], du, dg)
    dh = _mm_nt("od_in_dx", dz, w["w_in"], out_dtype=BF16)
    dw_in = _mm_tn("od_in_dw", s["h"], dz)
    dx_new, dpre_g, dscale, dshift = _pre_bwd(s["x"], dh, dx, pre_g, mod, bias)
    g = dict(od_w_in=dw_in, od_b_in=jnp.concatenate([dba, dbb, dbg], axis=1)[0], od_dw_w=dwb[:CONV_KERNEL], od_dw_b=dwb[CONV_KERNEL],
             od_ln_g=dln_g[0], od_ln_b=dln_b[0], od_w_out=dw_out)
    return dx_new, g, dpre_g[0], dpost_g[0], jnp.concatenate([dshift, dscale, dgate], axis=1)


def _local_step(x, mod, positions, full, loss_target):
    cos, sin = _rope_tables(positions)
    ev, od = _prep_layer_weights(full)
    f32 = lambda a: a.astype(F32)
    mods = [mod[l:l + 1] for l in range(DEPTH)]
    biases = [f32(full["ada_b"][l])[None] for l in range(DEPTH)]
    pre = [f32(full["pre_g"][l])[None] for l in range(DEPTH)]
    post = [f32(full["post_g"][l])[None] for l in range(DEPTH)]

    saved = []
    for l in range(DEPTH):
        if l % 2 == 0:
            x, s = _even_fwd(x, mods[l], biases[l], pre[l], post[l], cos, sin, ev[l // 2])
        else:
            x, s = _odd_fwd(x, mods[l], biases[l], pre[l], post[l], od[l // 2])
        saved.append(s)
    dx, loss = _loss_head(x, loss_target)

    per_layer = [None] * DEPTH
    dpre, dpost, dmod = [None] * DEPTH, [None] * DEPTH, [None] * DEPTH
    for l in reversed(range(DEPTH)):
        if l % 2 == 0:
            dx, per_layer[l], dpre[l], dpost[l], dmod[l] = _even_bwd(dx, saved[l], mods[l], biases[l], pre[l], post[l], cos, sin, ev[l // 2])
        else:
            dx, per_layer[l], dpre[l], dpost[l], dmod[l] = _odd_bwd(dx, saved[l], mods[l], biases[l], pre[l], post[l], od[l // 2])

    grads = {"pre_g": jnp.stack(dpre), "post_g": jnp.stack(dpost), "ada_b": jnp.concatenate(dmod, axis=0)}
    evs =[per_layer[l] for l in range(0, DEPTH, 2)]
    ods = [per_layer[l] for l in range(1, DEPTH, 2)]
    for name in ("ev_w_in", "ev_w_uq", "ev_w_ukv", "ev_w_out", "ev_q_norm_g", "ev_kv_norm_g"):
        grads[name] = jnp.stack([g[name] for g in evs])
    for name in ("od_w_in", "od_b_in", "od_dw_w", "od_dw_b", "od_ln_g", "od_ln_b", "od_w_out"):
        grads[name] = jnp.stack([g[name] for g in ods])
    grads["ev_dec_f"] = jnp.stack([g["dlg_f"] for g in evs]) * jax.nn.sigmoid(-f32(full["ev_dec_f"]))
    grads["ev_dec_b"] = jnp.stack([g["dlg_b"] for g in evs]) * jax.nn.sigmoid(-f32(full["ev_dec_b"]))
    return loss[0, 0], dx, grads


ADA_ROWS = 16
ADA_COLS = 3 * D_MODEL // 4
ADA_PACK = (8, 1024)


def _ada_pack(a):
    n = ADA_PACK[0] * ADA_PACK[1]
    return jnp.pad(a, ((0, 0), (0, n - a.shape[1]))).reshape(4, *ADA_PACK)


def _ada_unpack(a):
    return a.reshape(4, -1)[:, :DEPTH * ADA_COLS]


def _ada_forward(c, ada_w):
    D = D_MODEL
    c16 = jnp.zeros((ADA_ROWS, D), F32).at[0].set(c[0])
    c_act = _rowcall("silu_c", lambda rows, p: ([_silu(rows[0][...])], []), [c16], [], [(D, BF16)])[0]
    got = _chip_exchange("ada_c", c_act, "whole")
    c4 = jnp.pad(got[:, 0, :], ((0, ADA_ROWS - 4), (0, 0)))
    per_layer = [_mm_nn("ada", c4, ada_w[l].astype(BF16))[:4] for l in range(DEPTH)]
    p = jnp.stack(per_layer, axis=1).reshape(4, DEPTH * ADA_COLS)
    got = _chip_exchange("ada_mod", _ada_pack(p), "scatter")
    mod = _ada_unpack(got).reshape(4, DEPTH, ADA_COLS).transpose(1, 0, 2).reshape(DEPTH, 3 * D)
    return mod, c4


def _ada_backward(dmod, c4):
    p = dmod.reshape(DEPTH, 4, ADA_COLS).transpose(1, 0, 2).reshape(4, DEPTH * ADA_COLS)
    got = _chip_exchange("ada_dmod", _ada_pack(p), "scatter")
    dm4 = jnp.pad(_ada_unpack(got), ((0, ADA_ROWS - 4), (0, 0)))
    both = _halves_merge("ada_merge", jnp.concatenate([dm4, c4.astype(F32)], axis=1))
    dm32, c32 = both[:, :DEPTH * ADA_COLS].astype(BF16), both[:, DEPTH * ADA_COLS:].astype(BF16)
    return jnp.stack([_mm_tn("ada_dw", c32, dm32[:, l * ADA_COLS:(l + 1) * ADA_COLS]) for l in range(DEPTH)])


def kernel(x, c, positions, ada_w, ada_b, pre_g, post_g, ev_w_in, ev_dec_f, ev_dec_b, ev_q_norm_g, ev_w_uq, ev_kv_norm_g, ev_w_ukv, ev_w_out, od_w_in, od_b_in, od_dw_w, od_dw_b, od_ln_g, od_ln_b, od_w_out, loss_target, m_ada_w, m_ada_b, m_pre_g, m_post_g, m_ev_w_in, m_ev_dec_f, m_ev_dec_b, m_ev_q_norm_g, m_ev_w_uq, m_ev_kv_norm_g, m_ev_w_ukv, m_ev_w_out, m_od_w_in, m_od_b_in, m_od_dw_w, m_od_dw_b, m_od_ln_g, m_od_ln_b, m_od_w_out, v_ada_w, v_ada_b, v_pre_g, v_post_g, v_ev_w_in, v_ev_dec_f, v_ev_dec_b, v_ev_q_norm_g, v_ev_w_uq, v_ev_kv_norm_g, v_ev_w_ukv, v_ev_w_out, v_od_w_in, v_od_b_in, v_od_dw_w, v_od_dw_b, v_od_ln_g, v_od_ln_b, v_od_w_out):
    w = dict(ada_w=ada_w, ada_b=ada_b, pre_g=pre_g, post_g=post_g, ev_w_in=ev_w_in, ev_dec_f=ev_dec_f, ev_dec_b=ev_dec_b,
             ev_q_norm_g=ev_q_norm_g, ev_w_uq=ev_w_uq, ev_kv_norm_g=ev_kv_norm_g, ev_w_ukv=ev_w_ukv, ev_w_out=ev_w_out,
             od_w_in=od_w_in, od_b_in=od_b_in, od_dw_w=od_dw_w, od_dw_b=od_dw_b, od_ln_g=od_ln_g, od_ln_b=od_ln_b, od_w_out=od_w_out)
    m = dict(ada_w=m_ada_w, ada_b=m_ada_b, pre_g=m_pre_g, post_g=m_post_g, ev_w_in=m_ev_w_in, ev_dec_f=m_ev_dec_f, ev_dec_b=m_ev_dec_b,
             ev_q_norm_g=m_ev_q_norm_g, ev_w_uq=m_ev_w_uq, ev_kv_norm_g=m_ev_kv_norm_g, ev_w_ukv=m_ev_w_ukv, ev_w_out=m_ev_w_out,
             od_w_in=m_od_w_in, od_b_in=m_od_b_in, od_dw_w=m_od_dw_w, od_dw_b=m_od_dw_b, od_ln_g=m_od_ln_g, od_ln_b=m_od_ln_b, od_w_out=m_od_w_out)
    v = dict(ada_w=v_ada_w, ada_b=v_ada_b, pre_g=v_pre_g, post_g=v_post_g, ev_w_in=v_ev_w_in, ev_dec_f=v_ev_dec_f, ev_dec_b=v_ev_dec_b,
             ev_q_norm_g=v_ev_q_norm_g, ev_w_uq=v_ev_w_uq, ev_kv_norm_g=v_ev_kv_norm_g, ev_w_ukv=v_ev_w_ukv, ev_w_out=v_ev_w_out,
             od_w_in=v_od_w_in, od_b_in=v_od_b_in, od_dw_w=v_od_dw_w, od_dw_b=v_od_dw_b, od_ln_g=v_od_ln_g, od_ln_b=v_od_ln_b, od_w_out=v_od_w_out)

    full = _gather_weights({name: w[name] for name, _, _ in SHARDED})
    for name in REPLICATED:
        full[name] = w[name]
    mod, c4 = _ada_forward(c, ada_w)
    loss_local, grad_x, grads = _local_step(x[0], mod, positions[0], full, loss_target[0])
    loss = lax.psum(loss_local, ("x", "y", "c"))
    g = _reduce_grads(grads, {name: w[name].shape for name in WEIGHTS})
    g["ada_w"] = _ada_backward(grads["ada_b"], c4)
    delta, new_m, new_v = {}, {}, {}
    for name in WEIGHTS:
        delta[name], new_m[name], new_v[name] = _adamw(w[name], g[name], m[name], v[name])
    return (loss, grad_x[None], *[g[n] for n in WEIGHTS], *[delta[n] for n in WEIGHTS],
            *[new_m[n] for n in WEIGHTS], *[new_v[n] for n in WEIGHTS])
```

```python
import functools

import jax
import jax.numpy as jnp
from jax import lax
from jax.experimental import pallas as pl
from jax.experimental.pallas import tpu as pltpu

F32 = jnp.float32
BF16 = jnp.bfloat16

D_MODEL = 1024
DEPTH = 4
RET_HEADS = 4
RET_QK_DIM = 64
RET_V_DIM = 128
MLA_HEADS = 4
MLA_Q_RANK = 384
MLA_KV_RANK = 256
MLA_NOPE_DIM = 128
MLA_ROPE_DIM = 64
MLA_V_DIM = 128
ROPE_DIM = 64
ROPE_BASE = 10000.0
CONV_KERNEL = 31
EPS = 1e-6
ADAM_LR, ADAM_B1, ADAM_B2, ADAM_EPS, ADAM_WD, ADAM_STEP = 0.001, 0.9, 0.999, 1e-08, 0.01, 10

LANES = 128
HP = 128
Z_RQ, Z_RK, Z_RV, Z_RG, Z_MG, Z_CQ, Z_CKV, Z_KR, Z_END = 0, 512, 1024, 1536, 2048, 2560, 2944, 3200, 3328
Z_LAT = Z_END - Z_CQ


def _z_ret(z):
    return (z, Z_RG, 0)


def _z_gates(z):
    return [(z, 512, Z_RG // 512), (z, 512, Z_MG // 512)]


def _z_latents(z):
    return [(z, 256, Z_CQ // 256 + i) for i in range(Z_LAT // 256)]
V7X_VMEM_BYTES = 64 * 1024 * 1024
VMEM_CAP = V7X_VMEM_BYTES - 8 * 1024 * 1024

MESH = pl.DeviceIdType.MESH


def _nbytes(shape, dtype):
    n = 1
    for s in shape:
        n *= s
    return n * jnp.dtype(dtype).itemsize


def _hbm(x):
    return pltpu.with_memory_space_constraint(x, pltpu.HBM)


def _params(sem, block_bytes):
    limit = min(VMEM_CAP, max(32 * 1024 * 1024, 2 * block_bytes + 16 * 1024 * 1024))
    return pltpu.CompilerParams(dimension_semantics=sem, vmem_limit_bytes=limit)


ROW_TILE = 512


def _rowcall(name, fn, rows, params, outs, accs=(), ts=ROW_TILE):
    rows = [r if isinstance(r, tuple) else (r, r.shape[1], 0) for r in rows]
    S = rows[0][0].shape[0]
    ts = min(ts, S)
    assert S % ts == 0, (name, S, ts)
    nr, npar, no, na = len(rows), len(params), len(outs), len(accs)

    def body(*refs):
        row_refs = refs[:nr]
        pvals = [r[...] for r in refs[nr:nr + npar]]
        out_refs = refs[nr + npar:nr + npar + no]
        acc_refs = refs[nr + npar + no:]
        ovals, avals = fn(row_refs, pvals)
        for r, v in zip(out_refs, ovals, strict=True):
            r[...] = v.astype(r.dtype)
        if na:
            @pl.when(pl.program_id(0) == 0)
            def _():
                for r in acc_refs:
                    r[...] = jnp.zeros_like(r)
            for r, v in zip(acc_refs, avals, strict=True):
                r[...] += v

    in_specs = [pl.BlockSpec((ts, w), functools.partial(lambda i, b: (i, b), b=blk)) for _, w, blk in rows]
    in_specs += [pl.BlockSpec(p.shape, lambda i: (0, 0)) for p in params]
    out_specs = [pl.BlockSpec((ts, w), lambda i: (i, 0)) for w, _ in outs]
    out_specs += [pl.BlockSpec(s, lambda i: (0, 0)) for s in accs]
    out_shape = [jax.ShapeDtypeStruct((S, w), dt) for w, dt in outs]
    out_shape += [jax.ShapeDtypeStruct(s, F32) for s in accs]
    nb = sum(_nbytes((ts, w), a.dtype) for a, w, _ in rows) + sum(_nbytes((ts, w), dt) for w, dt in outs)
    nb += sum(_nbytes(p.shape, p.dtype) for p in params) + sum(_nbytes(s, F32) for s in accs)
    res = pl.pallas_call(
        body, name=name, grid=(S // ts,), in_specs=in_specs, out_specs=out_specs, out_shape=out_shape,
        compiler_params=_params(("arbitrary",) if na else ("parallel",), 3 * nb),
    )(*[_hbm(a) for a, _, _ in rows], *params)
    return res


def _silu(x):
    return x * jax.nn.sigmoid(x)


def _rms(x, g):
    return x * lax.rsqrt(jnp.mean(x * x, axis=-1, keepdims=True) + EPS) * g


def _rot(x):
    lane = lax.broadcasted_iota(jnp.int32, x.shape, 1)
    return jnp.where(lane < ROPE_DIM // 2, -pltpu.roll(x, LANES - ROPE_DIM // 2, 1), pltpu.roll(x, ROPE_DIM // 2, 1))


def _rope(x, cos, sin):
    return x * cos + _rot(x) * sin


def _rope_t(dy, cos, sin):
    return dy * cos - _rot(dy) * sin


def _groups(ref, start, n):
    return [ref[:, start + HP * h:start + HP * (h + 1)] for h in range(n)]


def _pre_math(x, g, m_scale, b_scale, m_shift, b_shift):
    return _rms(x, g) * (1.0 + (m_scale + b_scale)) + (m_shift + b_shift)


def _pre_fwd(x, g, mod, bias):
    D = D_MODEL

    def fn(rows, p):
        g_, mod_, b_ = p
        h = _pre_math(rows[0][...], g_, mod_[:, D:2 * D], b_[:, D:2 * D], mod_[:, :D], b_[:, :D])
        return [h], []

    return _rowcall("pre_fwd", fn, [x], [g, mod, bias], [(D, BF16)])[0]


def _pre_bwd(x, dh, dx_res, g, mod, bias):
    D = D_MODEL

    def fn(rows, p):
        g_, mod_, b_ = p
        xv = rows[0][...]
        _, vjp = jax.vjp(_pre_math, xv, g_, mod_[:, D:2 * D], b_[:, D:2 * D], mod_[:, :D], b_[:, :D])
        dx, dg, dsc, _, dsh, _ = vjp(rows[1][...].astype(F32))
        return [dx + rows[2][...]], [dg, dsc, dsh]

    return _rowcall("pre_bwd", fn, [x, dh, dx_res], [g, mod, bias], [(D, F32)], [(1, D)] * 3)


def _post_math(x, y, g, m_gate, b_gate):
    return x + (m_gate + b_gate) * _rms(y, g)


def _post_fwd(x, y, g, mod, bias):
    D = D_MODEL

    def fn(rows, p):
        g_, mod_, b_ = p
        return [_post_math(rows[0][...], rows[1][...], g_, mod_[:, 2 * D:], b_[:, 2 * D:])], []

    return _rowcall("post_fwd", fn, [x, y], [g, mod, bias], [(D, F32)])[0]


def _post_bwd(y, dxo, g, mod, bias):
    D = D_MODEL

    def fn(rows, p):
        g_, mod_, b_ = p
        yv = rows[0][...]
        _, vjp = jax.vjp(lambda y_, g2, mg: _post_math(0.0, y_, g2, mg, b_[:, 2 * D:]), yv, g_, mod_[:, 2 * D:])
        dy, dg, dgate = vjp(rows[1][...])
        return [dy], [dg, dgate]

    return _rowcall("post_bwd", fn, [y, dxo], [g, mod, bias], [(D, BF16)], [(1, D)] * 2)


def _evprep_fwd(z, cos, sin, qg, kvg):
    def fn(rows, p):
        z_, l0, l1, l2, c_, s_ = rows
        qg_, kvg_ = p
        cos_, sin_ = c_[...], s_[...]
        lat = jnp.concatenate([l0[...], l1[...], l2[...]], axis=1)
        rq = jnp.concatenate([_rope(v[...], cos_, sin_) for v in _groups(z_, Z_RQ, RET_HEADS)], axis=1)
        rk = jnp.concatenate([_rope(v[...], cos_, sin_) for v in _groups(z_, Z_RK, RET_HEADS)], axis=1)
        rk = rk * (RET_QK_DIM ** -0.5)
        rv = z_[:, Z_RV:Z_RG]
        qn = _rms(lat[:, :Z_CKV - Z_CQ], qg_)
        kvn = _rms(lat[:, Z_CKV - Z_CQ:Z_KR - Z_CQ], kvg_)
        kr = _rope(lat[:, Z_KR - Z_CQ:], cos_, sin_)
        return [rq, rk, rv, qn, kvn, kr], []

    return _rowcall("evprep_fwd", fn, [_z_ret(z), *_z_latents(z), cos, sin], [qg, kvg],
                    [(512, BF16), (512, BF16), (512, BF16), (MLA_Q_RANK, BF16), (MLA_KV_RANK, BF16), (HP, F32)])


def _evprep_bwd(z, cos, sin, qg, kvg, drq, drk, drv, drg, dqn, dkvn, dkr, dmg):
    def fn(rows, p):
        l0, l1, l2, c_, s_, drq_, drk_, drv_, drg_, dqn_, dkvn_, dkr_, dmg_ = rows
        qg_, kvg_ = p
        cos_, sin_ = c_[...], s_[...]
        lat = jnp.concatenate([l0[...], l1[...], l2[...]], axis=1)
        parts = []
        for h in range(RET_HEADS):
            parts.append(_rope_t(drq_[:, HP * h:HP * (h + 1)] + drq_[:, 512 + HP * h:512 + HP * (h + 1)], cos_, sin_))
        for h in range(RET_HEADS):
            t = drk_[:, HP * h:HP * (h + 1)] + drk_[:, 512 + HP * h:512 + HP * (h + 1)]
            parts.append(_rope_t(t, cos_, sin_) * (RET_QK_DIM ** -0.5))
        parts.append(drv_[:, :512] + drv_[:, 512:])
        parts += [drg_[...], dmg_[...]]
        _, vq = jax.vjp(_rms, lat[:, :Z_CKV - Z_CQ], qg_)
        dcq, dqg = vq(dqn_[...])
        _, vkv = jax.vjp(_rms, lat[:, Z_CKV - Z_CQ:Z_KR - Z_CQ], kvg_)
        dckv, dkvg = vkv(dkvn_[...])
        parts += [dcq, dckv, _rope_t(dkr_[...], cos_, sin_)]
        return [jnp.concatenate([v.astype(BF16) for v in parts], axis=1)], [dqg, dkvg]

    rows = [*_z_latents(z), cos, sin, drq, drk, drv, drg, dqn, dkvn, dkr, dmg]
    return _rowcall("evprep_bwd", fn, rows, [qg, kvg], [(Z_END, BF16)], [(1, MLA_Q_RANK), (1, MLA_KV_RANK)], ts=256)


ATT_SCALE = (MLA_NOPE_DIM + MLA_ROPE_DIM) ** -0.5
LOG2E = 1.4426950408889634
LN2 = 0.6931471805599453


def _attnprep_fwd(qf, kv, kr, cos, sin):
    qscale = ATT_SCALE * LOG2E

    def fn(rows, p):
        q_, kv_, kr_, c_, s_ = rows
        cos_, sin_ = c_[...], s_[...]
        krv = kr_[...]
        qs, ks, vs = [], [], []
        for h in range(MLA_HEADS):
            b = 2 * HP * h
            qs += [q_[:, b:b + HP] * qscale, _rope(q_[:, b + HP:b + 2 * HP], cos_, sin_) * qscale]
            ks += [kv_[:, b:b + HP], krv]
            vs += [kv_[:, b + HP:b + 2 * HP]]
        return [jnp.concatenate(qs, axis=1), jnp.concatenate(ks, axis=1), jnp.concatenate(vs, axis=1)], []

    return _rowcall("attnprep_fwd", fn, [qf, kv, kr, cos, sin], [], [(1024, BF16), (1024, BF16), (512, BF16)])


def _attnprep_bwd(dQ, dK, dV, cos, sin):
    def fn(rows, p):
        dq_, dk_, dv_, c_, s_ = rows
        cos_, sin_ = c_[...], s_[...]
        dqs, dkvs = [], []
        dkr = None
        for h in range(MLA_HEADS):
            b = 2 * HP * h
            dqs += [dq_[:, b:b + HP] * ATT_SCALE, _rope_t(dq_[:, b + HP:b + 2 * HP], cos_, sin_) * ATT_SCALE]
            dkvs += [dk_[:, b:b + HP] * LN2, dv_[:, HP * h:HP * (h + 1)]]
            t = dk_[:, b + HP:b + 2 * HP]
            dkr = t if dkr is None else dkr + t
        return [jnp.concatenate(dqs, axis=1), jnp.concatenate(dkvs, axis=1), dkr * LN2], []

    return _rowcall("attnprep_bwd", fn, [dQ, dK, dV, cos, sin], [], [(1024, BF16), (1024, BF16), (HP, F32)])


def _mix_math(o, rg, a, mg):
    outs = []
    for h in range(RET_HEADS):
        oh = o[:, HP * h:HP * (h + 1)]
        mu = jnp.mean(oh, axis=-1, keepdims=True)
        var = jnp.mean(jnp.square(oh - mu), axis=-1, keepdims=True)
        outs.append((oh - mu) * lax.rsqrt(var + EPS))
    ret = jnp.concatenate(outs, axis=1) * _silu(rg)
    return jnp.concatenate([ret, a * _silu(mg)], axis=1)


def _mix_fwd(o2, z, a):
    def fn(rows, p):
        o_, rg_, mg_, a_ = rows
        return [_mix_math(o_[:, :512] + o_[:, 512:], rg_[...], a_[...], mg_[...])], []

    return _rowcall("mix_fwd", fn, [o2, *_z_gates(z), a], [], [(1024, BF16)])[0]


def _mix_bwd(o2, z, a, dmix):
    def fn(rows, p):
        o_, rg_, mg_, a_, dm_ = rows
        _, vjp = jax.vjp(_mix_math, o_[:, :512] + o_[:, 512:], rg_[...], a_[...], mg_[...])
        return list(vjp(dm_[...].astype(F32))), []

    return _rowcall("mix_bwd", fn, [o2, *_z_gates(z), a, dmix], [], [(512, BF16), (512, F32), (512, BF16), (512, F32)])


def _glu_math(a, b, ba, bb):
    return (a + ba) * jax.nn.sigmoid(b + bb)


def _glu_fwd(z, bin_):
    D = D_MODEL

    def fn(rows, p):
        za_, zb_ = rows
        b_, = p
        return [_glu_math(za_[...], zb_[...], b_[:, :D], b_[:, D:2 * D])], []

    return _rowcall("glu_fwd", fn, [(z, D, 0), (z, D, 1)], [bin_], [(D, F32)])[0]


def _odmix_math(uc, g, bg, ln_g, ln_b):
    mu = jnp.mean(uc, axis=-1, keepdims=True)
    var = jnp.mean(jnp.square(uc - mu), axis=-1, keepdims=True)
    y = (uc - mu) * lax.rsqrt(var + EPS) * ln_g + ln_b
    return _silu(y) * _silu(g + bg)


def _odmix_fwd(uc, z, bin_, ln_g, ln_b):
    D = D_MODEL

    def fn(rows, p):
        uc_, zg_ = rows
        b_, g_, lb_ = p
        return [_odmix_math(uc_[...], zg_[...], b_[:, 2 * D:], g_, lb_)], []

    return _rowcall("odmix_fwd", fn, [uc, (z, D, 2)], [bin_, ln_g, ln_b], [(D, BF16)])[0]


def _odmix_bwd(uc, z, bin_, ln_g, ln_b, dv):
    D = D_MODEL

    def fn(rows, p):
        uc_, zg_, dv_ = rows
        b_, g_, lb_ = p
        _, vjp = jax.vjp(_odmix_math, uc_[...], zg_[...], b_[:, 2 * D:], g_, lb_)
        duc, dg, dbg, dlg, dlb = vjp(dv_[...].astype(F32))
        return [duc, dg], [dbg, dlg, dlb]

    return _rowcall("odmix_bwd", fn, [uc, (z, D, 2), dv], [bin_, ln_g, ln_b], [(D, F32), (D, BF16)], [(1, D)] * 3)


def _glu_bwd(z, bin_, du, dg):
    D = D_MODEL

    def fn(rows, p):
        za_, zb_, du_, dg_ = rows
        b_, = p
        _, vjp = jax.vjp(_glu_math, za_[...], zb_[...], b_[:, :D], b_[:, D:2 * D])
        da, db, dba, dbb = vjp(du_[...])
        return [jnp.concatenate([da.astype(BF16), db.astype(BF16), dg_[...]], axis=1)], [dba, dbb]

    return _rowcall("glu_bwd", fn, [(z, D, 0), (z, D, 1), du, dg], [bin_], [(3 * D, BF16)], [(1, D)] * 2)


def _loss_head(x, tgt):
    D = D_MODEL

    def fn(rows, p):
        err = rows[0][...] - rows[1][...]
        part = 0.5 * jnp.sum(jnp.mean(err * err, axis=-1, keepdims=True), axis=0, keepdims=True)
        return [err * (1.0 / D)], [jnp.broadcast_to(part, (1, LANES))]

    return _rowcall("loss_head", fn, [x, tgt], [], [(D, F32)], [(1, LANES)])


def _tile(n, cap):
    if n <= cap:
        return n
    best = None
    for t in range(LANES, cap + 1, LANES):
        if n % t == 0:
            best = t
    assert best is not None, (n, cap)
    return best


MM_TN_CAP = 1792
NT_DIMS = (((1,), (1,)), ((), ()))
TN_DIMS = (((0,), (0,)), ((), ()))


def _mm_nn(name, a, b, out_dtype=F32, tm=512, tn_cap=MM_TN_CAP):
    M, K = a.shape
    N = b.shape[1]
    tm = min(tm, M)
    tn = _tile(N, tn_cap)

    def body(a_ref, b_ref, o_ref):
        o_ref[...] = jnp.dot(a_ref[...], b_ref[...], preferred_element_type=F32).astype(o_ref.dtype)

    nb = _nbytes((tm, K), a.dtype) + _nbytes((K, tn), b.dtype) + _nbytes((tm, tn), out_dtype) + _nbytes((tm, tn), F32)
    return pl.pallas_call(
        body, name=name, grid=(N // tn, M // tm),
        in_specs=[pl.BlockSpec((tm, K), lambda j, i: (i, 0)), pl.BlockSpec((K, tn), lambda j, i: (0, j))],
        out_specs=pl.BlockSpec((tm, tn), lambda j, i: (i, j)),
        out_shape=jax.ShapeDtypeStruct((M, N), out_dtype),
        compiler_params=_params(("parallel", "parallel"), nb),
    )(_hbm(a), _hbm(b))


def _mm_nt(name, a, b, out_dtype=F32, tm=512, tn_cap=1024):
    M, K = a.shape
    N = b.shape[0]
    tm = min(tm, M)
    tn = _tile(N, tn_cap)

    def body(a_ref, b_ref, o_ref):
        o_ref[...] = lax.dot_general(a_ref[...], b_ref[...], NT_DIMS, preferred_element_type=F32).astype(o_ref.dtype)

    nb = _nbytes((tm, K), a.dtype) + _nbytes((tn, K), b.dtype) + _nbytes((tm, tn), out_dtype) + _nbytes((tm, tn), F32)
    return pl.pallas_call(
        body, name=name, grid=(N // tn, M // tm),
        in_specs=[pl.BlockSpec((tm, K), lambda j, i: (i, 0)), pl.BlockSpec((tn, K), lambda j, i: (j, 0))],
        out_specs=pl.BlockSpec((tm, tn), lambda j, i: (i, j)),
        out_shape=jax.ShapeDtypeStruct((M, N), out_dtype),
        compiler_params=_params(("parallel", "parallel"), nb),
    )(_hbm(a), _hbm(b))


def _mm_tn(name, a, b, ts=1024, tm_cap=512, tn_cap=MM_TN_CAP):
    S, M = a.shape
    N = b.shape[1]
    ts = min(ts, S)
    tm = _tile(M, tm_cap)
    tn = _tile(N, tn_cap)

    def body(a_ref, b_ref, o_ref):
        @pl.when(pl.program_id(2) == 0)
        def _():
            o_ref[...] = jnp.zeros_like(o_ref)

        o_ref[...] += lax.dot_general(a_ref[...], b_ref[...], (((0,), (0,)), ((), ())), preferred_element_type=F32)

    nb = _nbytes((ts, tm), a.dtype) + _nbytes((ts, tn), b.dtype) + 2 * _nbytes((tm, tn), F32)
    return pl.pallas_call(
        body, name=name, grid=(M // tm, N // tn, S // ts),
        in_specs=[pl.BlockSpec((ts, tm), lambda i, j, s: (s, i)), pl.BlockSpec((ts, tn), lambda i, j, s: (s, j))],
        out_specs=pl.BlockSpec((tm, tn), lambda i, j, s: (i, j)),
        out_shape=jax.ShapeDtypeStruct((M, N), F32),
        compiler_params=_params(("parallel", "parallel", "arbitrary"), nb),
    )(_hbm(a), _hbm(b))


ATT_TQ = 1024
ATT_TK = 1024
ATT_TK_FWD = 1024
QW = 2 * HP


def _kv_tiles(V, tk):
    S = V.shape[0]
    return V.reshape(S // tk, tk, MLA_HEADS, HP).transpose(2, 0, 3, 1)


class _Side:
    def __init__(self, side):
        self.on = side is not None
        if self.on:
            self.p, self.mode = side
            shp = self.p.shape
            self.h, self.w = (shp[0] // 2, shp[1]) if self.mode == "half" else (shp[-2], shp[-1])
        self.in_specs = [_HBM] if self.on else []
        self.out_specs = [_HBM] if self.on else []
        self.out_shape = [jax.ShapeDtypeStruct((4, self.h, self.w), self.p.dtype)] if self.on else []
        self.scratch = list(_CHIP_SEMS) if self.on else []
        self.operands = [_hbm(self.p)] if self.on else []

    def copies(self, p_ref, x_ref, sems):
        return _chip_copies(p_ref, x_ref, *sems, self.mode, self.h)

    def start(self, p_ref, x_ref, sems, first):
        @pl.when(first)
        def _():
            for cp in self.copies(p_ref, x_ref, sems):
                cp.start()

    def finish(self, p_ref, x_ref, sems, last):
        @pl.when(last)
        def _():
            for cp in self.copies(p_ref, x_ref, sems):
                cp.wait()


def _flash_fwd(Q, K, V, side=None):
    S = Q.shape[0]
    H = MLA_HEADS
    tq, tk = min(ATT_TQ, S), min(ATT_TK_FWD, S)
    nk = S // tk
    VT = _kv_tiles(V, tk)
    sd = _Side(side)

    def body(*refs):
        if sd.on:
            q_ref, k_ref, vt_ref, p_ref, o_ref, lse_ref, x_ref, m_s, l_s, acc_s, s_a, s_b, *sems = refs
            step_id = pl.program_id(0) * (S // tq) + pl.program_id(1)
            sd.start(p_ref, x_ref, sems, step_id == 0)
        else:
            q_ref, k_ref, vt_ref, o_ref, lse_ref, m_s, l_s, acc_s, s_a, s_b = refs
        m_s[...] = jnp.full(m_s.shape, -jnp.inf, F32)
        l_s[...] = jnp.zeros(l_s.shape, F32)
        acc_s[...] = jnp.zeros(acc_s.shape, F32)

        def scores(j):
            k = k_ref[pl.ds(pl.multiple_of(j * tk, tk), tk), :]
            return lax.dot_general(k, q_ref[...], NT_DIMS, preferred_element_type=F32)

        def consume(st, j):
            m_prev = m_s[...]
            m_new = jnp.maximum(m_prev, jnp.max(st, axis=0, keepdims=True))
            alpha = jnp.exp2(m_prev - m_new)
            pt = jnp.exp2(st - m_new)
            l_s[...] = alpha * l_s[...] + jnp.sum(pt, axis=0, keepdims=True)
            acc_s[...] = alpha * acc_s[...] + jnp.dot(vt_ref[0, j], pt.astype(BF16), preferred_element_type=F32)
            m_s[...] = m_new

        if nk % 2:
            def step(j, carry):
                consume(scores(j), j)
                return carry

            lax.fori_loop(0, nk, step, 0)
        else:
            s_a[...] = scores(0)

            def pair(jj, carry):
                j0 = 2 * jj
                s_b[...] = scores(j0 + 1)
                consume(s_a[...], j0)
                s_a[...] = scores(jnp.minimum(j0 + 2, nk - 1))
                consume(s_b[...], j0 + 1)
                return carry

            lax.fori_loop(0, nk // 2, pair, 0)
        l = l_s[...]
        o_ref[...] = (acc_s[...] * (1.0 / l)).T
        lse_ref[0] = m_s[...] + jnp.log2(l)
        if sd.on:
            sd.finish(p_ref, x_ref, sems, step_id == H * (S // tq) - 1)

    nb = (_nbytes((tq, QW), BF16) + _nbytes((S, QW), BF16) + _nbytes((S, HP), BF16) + 3 * _nbytes((tq, HP), F32)
          + 4 * _nbytes((tq, tk), F32))
    return pl.pallas_call(
        body, name="flash_fwd_side" if sd.on else "flash_fwd", grid=(H, S // tq),
        in_specs=[pl.BlockSpec((tq, QW), lambda h, i: (i, h)), pl.BlockSpec((S, QW), lambda h, i: (0, h)),
                  pl.BlockSpec((1, nk, HP, tk), lambda h, i: (h, 0, 0, 0))] + sd.in_specs,
        out_specs=[pl.BlockSpec((tq, HP), lambda h, i: (i, h)), pl.BlockSpec((1, 1, tq), lambda h, i: (h, 0, i))] + sd.out_specs,
        out_shape=[jax.ShapeDtypeStruct((S, H * HP), F32), jax.ShapeDtypeStruct((H, 1, S), F32)] + sd.out_shape,
        scratch_shapes=[pltpu.VMEM((1, tq), F32), pltpu.VMEM((1, tq), F32), pltpu.VMEM((HP, tq), F32),
                        pltpu.VMEM((tk, tq), F32), pltpu.VMEM((tk, tq), F32)] + sd.scratch,
        compiler_params=_params(("arbitrary", "arbitrary") if sd.on else ("parallel", "parallel"), nb),
    )(_hbm(Q), _hbm(K), _hbm(VT), *sd.operands)


def _flash_bwd(Q, K, V, dO, O, lse, side=None):
    S = Q.shape[0]
    H = MLA_HEADS
    tq, tk = min(ATT_TQ, S), min(ATT_TK, S)
    nk = S // tk
    sd = _Side(side)

    def body(*refs):
        if sd.on:
            q_ref, do_ref, o_ref, lse_ref, k_ref, v_ref, p_ref, dq_ref, dk_ref, dv_ref, x_ref, dq_s, *sems = refs
            step_id = pl.program_id(0) * (S // tq) + pl.program_id(1)
            sd.start(p_ref, x_ref, sems, step_id == 0)
        else:
            q_ref, do_ref, o_ref, lse_ref, k_ref, v_ref, dq_ref, dk_ref, dv_ref, dq_s = refs

        @pl.when(pl.program_id(1) == 0)
        def _():
            dk_ref[...] = jnp.zeros(dk_ref.shape, F32)
            dv_ref[...] = jnp.zeros(dv_ref.shape, F32)

        delta = jnp.sum((do_ref[...].astype(F32) * o_ref[...]).T, axis=0, keepdims=True)
        lse = lse_ref[0]
        dq_s[...] = jnp.zeros(dq_s.shape, F32)

        def rows(j):
            return pl.ds(pl.multiple_of(j * tk, tk), tk)

        def scores(j):
            st = lax.dot_general(k_ref[rows(j), :], q_ref[...], NT_DIMS, preferred_element_type=F32)
            dpt = lax.dot_general(v_ref[rows(j), :], do_ref[...], NT_DIMS, preferred_element_type=F32)
            return st, dpt

        def consume(st, dpt, j):
            pt = jnp.exp2(st - lse)
            dst = (pt * (dpt - delta)).astype(BF16)
            dv_ref[rows(j), :] += jnp.dot(pt.astype(BF16), do_ref[...], preferred_element_type=F32)
            dk_ref[rows(j), :] += jnp.dot(dst, q_ref[...], preferred_element_type=F32)
            dq_s[...] += lax.dot_general(dst, k_ref[rows(j), :], TN_DIMS, preferred_element_type=F32)

        def step(j, carry):
            consume(*scores(j), j)
            return carry

        lax.fori_loop(0, nk, step, 0)
        dq_ref[...] = dq_s[...]
        if sd.on:
            sd.finish(p_ref, x_ref, sems, step_id == H * (S // tq) - 1)

    nb = (_nbytes((tq, QW), BF16) + _nbytes((tq, HP), BF16) + _nbytes((tq, HP), F32) + _nbytes((S, QW), BF16)
          + _nbytes((S, HP), BF16) + 2 * _nbytes((tq, QW), F32) + _nbytes((S, QW), F32) + _nbytes((S, HP), F32))
    return pl.pallas_call(
        body, name="flash_bwd_side" if sd.on else "flash_bwd", grid=(H, S // tq),
        in_specs=[pl.BlockSpec((tq, QW), lambda h, i: (i, h)), pl.BlockSpec((tq, HP), lambda h, i: (i, h)),
                  pl.BlockSpec((tq, HP), lambda h, i: (i, h)), pl.BlockSpec((1, 1, tq), lambda h, i: (h, 0, i)),
                  pl.BlockSpec((S, QW), lambda h, i: (0, h)), pl.BlockSpec((S, HP), lambda h, i: (0, h))] + sd.in_specs,
        out_specs=[pl.BlockSpec((tq, QW), lambda h, i: (i, h)), pl.BlockSpec((S, QW), lambda h, i: (0, h)),
                   pl.BlockSpec((S, HP), lambda h, i: (0, h))] + sd.out_specs,
        out_shape=[jax.ShapeDtypeStruct((S, H * QW), F32), jax.ShapeDtypeStruct((S, H * QW), F32),
                   jax.ShapeDtypeStruct((S, H * HP), F32)] + sd.out_shape,
        scratch_shapes=[pltpu.VMEM((tq, QW), F32)] + sd.scratch,
        compiler_params=_params(("arbitrary", "arbitrary") if sd.on else ("parallel", "arbitrary"), nb),
    )(_hbm(Q), _hbm(dO), _hbm(O), lse, _hbm(K), _hbm(V), *sd.operands)


RET_CHUNK = 256


def _ret_tables(d, lg_ref, h, C):
    ii = lax.broadcasted_iota(jnp.int32, (C, C), 0).astype(F32)
    jj = lax.broadcasted_iota(jnp.int32, (C, C), 1).astype(F32)
    ci = lax.broadcasted_iota(jnp.int32, (C, 1), 0).astype(F32)
    fwd = d == 0
    dist = jnp.where(fwd, ii - jj, jj - ii)
    mask = dist >= jnp.where(fwd, 0.0, 1.0)
    dist = jnp.maximum(dist, 0.0)
    qpos = jnp.where(fwd, ci + 1.0, C - ci)
    kpos = jnp.where(fwd, C - 1.0 - ci, ci)
    lg = lg_ref[d, h]
    D = jnp.where(mask, jnp.exp(lg * dist), 0.0)
    return D, jnp.exp(lg * qpos), jnp.exp(lg * kpos), jnp.exp(lg * C), dist, qpos, kpos


def _ret_fwd(lg, q, k, v):
    S = q.shape[0]
    C = min(RET_CHUNK, S)
    N = S // C
    H = RET_HEADS

    def chunk(d, n):
        return jnp.where(d == 0, n, N - 1 - n)

    def body(lg_ref, q_ref, k_ref, v_ref, o_ref, st_ref, state, tab_d, tab_q, tab_k):
        d, n = pl.program_id(0), pl.program_id(1)

        @pl.when(n == 0)
        def _():
            state[...] = jnp.zeros_like(state)
            for h in range(H):
                D, qw, kw, _, _, _, _ = _ret_tables(d, lg_ref, h, C)
                tab_d[h] = D
                tab_q[h] = jnp.broadcast_to(qw, (C, HP))
                tab_k[h] = jnp.broadcast_to(kw, (C, HP))

        for h in range(H):
            sl = slice(HP * h, HP * (h + 1))
            D, qw, kw = tab_d[h], tab_q[h], tab_k[h]
            gc = jnp.exp(lg_ref[d, h] * C)
            qh, kh, vh = q_ref[:, sl], k_ref[:, sl], v_ref[:, sl]
            st = state[sl, :]
            sm = lax.dot_general(qh, kh, (((1,), (1,)), ((), ())), preferred_element_type=F32) * D
            inner = jnp.dot(sm.astype(BF16), vh, preferred_element_type=F32)
            cross = qw * jnp.dot(qh, st.astype(BF16), preferred_element_type=F32)
            o_ref[:, sl] = inner + cross
            st_ref[0, 0, sl, :] = st
            kvn = lax.dot_general((kh.astype(F32) * kw).astype(BF16), vh, (((0,), (0,)), ((), ())), preferred_element_type=F32)
            state[sl, :] = gc * st + kvn

    nb = 3 * _nbytes((C, 512), BF16) + _nbytes((C, 512), F32) + 2 * _nbytes((512, HP), F32) + 8 * _nbytes((C, C), F32)
    return pl.pallas_call(
        body, name="ret_fwd", grid=(2, N),
        in_specs=[pl.BlockSpec(memory_space=pltpu.SMEM)] + [pl.BlockSpec((C, 512), lambda d, n: (chunk(d, n), 0))] * 3,
        out_specs=[pl.BlockSpec((C, 512), lambda d, n: (chunk(d, n), d)),
                   pl.BlockSpec((1, 1, 512, HP), lambda d, n: (d, chunk(d, n), 0, 0))],
        out_shape=[jax.ShapeDtypeStruct((S, 1024), F32), jax.ShapeDtypeStruct((2, N, 512, HP), F32)],
        scratch_shapes=[pltpu.VMEM((512, HP), F32), pltpu.VMEM((H, C, C), F32), pltpu.VMEM((H, C, HP), F32),
                        pltpu.VMEM((H, C, HP), F32)],
        compiler_params=_params(("arbitrary", "arbitrary"), nb),
    )(lg, _hbm(q), _hbm(k), _hbm(v))


def _ret_bwd(lg, q, k, v, do, states):
    S = q.shape[0]
    C = min(RET_CHUNK, S)
    N = S // C
    H = RET_HEADS

    def chunk(d, n):
        return jnp.where(d == 0, N - 1 - n, n)

    def body(lg_ref, q_ref, k_ref, v_ref, do_ref, st_ref, dq_ref, dk_ref, dv_ref, dlg_ref, G, accA, accB, accC,
             tab_d, tab_q, tab_k, tab_qp, tab_kp):
        d, n = pl.program_id(0), pl.program_id(1)

        @pl.when(n == 0)
        def _():
            G[...] = jnp.zeros_like(G)
            accA[...] = jnp.zeros_like(accA)
            accB[...] = jnp.zeros_like(accB)
            accC[...] = jnp.zeros_like(accC)
            for h in range(H):
                D, qw, kw, _, _, qpos, kpos = _ret_tables(d, lg_ref, h, C)
                tab_d[h] = D
                tab_q[h] = jnp.broadcast_to(qw, (C, HP))
                tab_k[h] = jnp.broadcast_to(kw, (C, HP))
                tab_qp[h] = jnp.broadcast_to(qw * qpos, (C, HP))
                tab_kp[h] = jnp.broadcast_to(kw * kpos, (C, HP))

        nt = (((1,), (1,)), ((), ()))
        tn = (((0,), (0,)), ((), ()))
        for h in range(H):
            sl = slice(HP * h, HP * (h + 1))
            D, qw, kw = tab_d[h], tab_q[h], tab_k[h]
            gc = jnp.exp(lg_ref[d, h] * C)
            qh, kh, vh, doh = q_ref[:, sl], k_ref[:, sl], v_ref[:, sl], do_ref[:, sl]
            st = st_ref[0, 0, sl, :]
            g = G[sl, :]
            stb, gb = st.astype(BF16), g.astype(BF16)
            sraw = lax.dot_general(qh, kh, nt, preferred_element_type=F32)
            dS = lax.dot_general(doh, vh, nt, preferred_element_type=F32) * D
            dSb = dS.astype(BF16)
            smb = (sraw * D).astype(BF16)
            qs = jnp.dot(qh, stb, preferred_element_type=F32)
            kg = jnp.dot(kh, gb, preferred_element_type=F32)
            dq_ref[:, sl] = jnp.dot(dSb, kh, preferred_element_type=F32) + qw * lax.dot_general(doh, stb, nt, preferred_element_type=F32)
            dk_ref[:, sl] = lax.dot_general(dSb, qh, tn, preferred_element_type=F32) + kw * lax.dot_general(vh, gb, nt, preferred_element_type=F32)
            dv_ref[:, sl] = lax.dot_general(smb, doh, tn, preferred_element_type=F32) + kw * kg
            dof, vf = doh.astype(F32), vh.astype(F32)
            accA[h] += sraw * dS
            accB[h] += tab_qp[h] * (qs * dof) + tab_kp[h] * (kg * vf)
            accC[h] += st * g
            G[sl, :] = gc * g + lax.dot_general((qh.astype(F32) * qw).astype(BF16), doh, tn, preferred_element_type=F32)

        @pl.when(n == N - 1)
        def _():
            rows = []
            for h in range(H):
                _, _, _, gc, dist, _, _ = _ret_tables(d, lg_ref, h, C)
                tot = jnp.sum(accA[h] * dist) + jnp.sum(accB[h]) + (C * gc) * jnp.sum(accC[h])
                rows.append(jnp.broadcast_to(tot, (1, HP)))
            dlg_ref[0] = jnp.concatenate(rows + [jnp.zeros((8 - H, HP), F32)], axis=0)

    nb = (4 * _nbytes((C, 512), BF16) + 3 * _nbytes((C, 512), F32) + 2 * _nbytes((512, HP), F32)
          + H * (_nbytes((C, C), F32) + _nbytes((C, HP), F32) + _nbytes((HP, HP), F32)) + 12 * _nbytes((C, C), F32))
    row = lambda d, n: (chunk(d, n), 0)
    out3 = lambda d, n: (chunk(d, n), d)
    return pl.pallas_call(
        body, name="ret_bwd", grid=(2, N),
        in_specs=[pl.BlockSpec(memory_space=pltpu.SMEM)] + [pl.BlockSpec((C, 512), row)] * 4
        + [pl.BlockSpec((1, 1, 512, HP), lambda d, n: (d, chunk(d, n), 0, 0))],
        out_specs=[pl.BlockSpec((C, 512), out3)] * 3 + [pl.BlockSpec((1, 8, HP), lambda d, n: (d, 0, 0))],
        out_shape=[jax.ShapeDtypeStruct((S, 1024), F32)] * 3 + [jax.ShapeDtypeStruct((2, 8, HP), F32)],
        scratch_shapes=[pltpu.VMEM((512, HP), F32), pltpu.VMEM((H, C, C), F32), pltpu.VMEM((H, C, HP), F32),
                        pltpu.VMEM((H, HP, HP), F32), pltpu.VMEM((H, C, C), F32)] + [pltpu.VMEM((H, C, HP), F32)] * 4,
        compiler_params=_params(("arbitrary", "arbitrary"), nb),
    )(lg, _hbm(q), _hbm(k), _hbm(v), _hbm(do), _hbm(states))


CONV_PAD = 16
CONV_TR = 256
CONV_CB = LANES


def _fill_padded(pad, u_ref, S):
    pad[0:CONV_PAD, :] = jnp.zeros((CONV_PAD, CONV_CB), F32)
    pad[CONV_PAD + S:CONV_PAD + S + CONV_PAD, :] = jnp.zeros((CONV_PAD, CONV_CB), F32)
    pad[CONV_PAD:CONV_PAD + S, :] = u_ref[...]


def _conv_fwd(u, w32, b):
    S, D = u.shape
    tr = min(CONV_TR, S)

    def body(u_ref, w_ref, b_ref, o_ref, pad):
        _fill_padded(pad, u_ref, S)
        wv = w_ref[...]
        bv = b_ref[...]

        def step(t, carry):
            r0 = pl.multiple_of(t * tr, tr)
            acc = jnp.broadcast_to(bv, (tr, CONV_CB))
            for k in range(CONV_KERNEL):
                acc = acc + pad[pl.ds(r0 + (k + 1), tr), :] * wv[k:k + 1, :]
            o_ref[pl.ds(r0, tr), :] = acc
            return carry

        lax.fori_loop(0, S // tr, step, 0)

    nb = 2 * _nbytes((S, CONV_CB), F32) + _nbytes((S + 2 * CONV_PAD, CONV_CB), F32)
    return pl.pallas_call(
        body, name="conv_fwd", grid=(D // CONV_CB,),
        in_specs=[pl.BlockSpec((S, CONV_CB), lambda j: (0, j)), pl.BlockSpec((32, CONV_CB), lambda j: (0, j)),
                  pl.BlockSpec((1, CONV_CB), lambda j: (0, j))],
        out_specs=pl.BlockSpec((S, CONV_CB), lambda j: (0, j)),
        out_shape=jax.ShapeDtypeStruct((S, D), F32),
        scratch_shapes=[pltpu.VMEM((S + 2 * CONV_PAD, CONV_CB), F32)],
        compiler_params=_params(("parallel",), nb),
    )(_hbm(u), w32, b)


def _conv_bwd_w(u, dout):
    S, D = u.shape
    tr = min(CONV_TR, S)

    def body(u_ref, d_ref, o_ref, pad, acc):
        _fill_padded(pad, u_ref, S)
        acc[...] = jnp.zeros_like(acc)

        def fold(a):
            return jnp.sum(a.reshape(tr // 8, 8, CONV_CB), axis=0)

        def step(t, carry):
            r0 = pl.multiple_of(t * tr, tr)
            dv = d_ref[pl.ds(r0, tr), :]
            for k in range(CONV_KERNEL):
                acc[8 * k:8 * k + 8, :] += fold(pad[pl.ds(r0 + (k + 1), tr), :] * dv)
            acc[8 * CONV_KERNEL:8 * CONV_KERNEL + 8, :] += fold(dv)
            return carry

        lax.fori_loop(0, S // tr, step, 0)
        o_ref[...] = jnp.sum(acc[...].reshape(32, 8, CONV_CB), axis=1)

    nb = 2 * _nbytes((S, CONV_CB), F32) + _nbytes((S + 2 * CONV_PAD, CONV_CB), F32)
    return pl.pallas_call(
        body, name="conv_bwd_w", grid=(D // CONV_CB,),
        in_specs=[pl.BlockSpec((S, CONV_CB), lambda j: (0, j)), pl.BlockSpec((S, CONV_CB), lambda j: (0, j))],
        out_specs=pl.BlockSpec((32, CONV_CB), lambda j: (0, j)),
        out_shape=jax.ShapeDtypeStruct((32, D), F32),
        scratch_shapes=[pltpu.VMEM((S + 2 * CONV_PAD, CONV_CB), F32), pltpu.VMEM((256, CONV_CB), F32)],
        compiler_params=_params(("parallel",), nb),
    )(_hbm(u), _hbm(dout))


def _adamw(w, g, m, v):
    shape = w.shape
    w2, g2, m2, v2 = [a.reshape(-1, shape[-1]) for a in (w, g, m, v)]
    W = shape[-1]

    def fn(rows, p):
        w_, g_, m_, v_ = [r[...] for r in rows]
        mn = ADAM_B1 * m_ + (1.0 - ADAM_B1) * g_
        vn = ADAM_B2 * v_ + (1.0 - ADAM_B2) * jnp.square(g_)
        m_hat = mn / (1.0 - ADAM_B1 ** ADAM_STEP)
        v_hat = vn / (1.0 - ADAM_B2 ** ADAM_STEP)
        delta = -ADAM_LR * (m_hat / (jnp.sqrt(v_hat) + ADAM_EPS) + ADAM_WD * w_)
        return [delta, mn, vn], []

    R2 = w2.shape[0]
    ts = 256 if (R2 > 256 and R2 % 256 == 0) else R2
    d, mn, vn = _rowcall("adamw", fn, [w2, g2, m2, v2], [], [(W, F32)] * 3, ts=ts)
    return d.reshape(shape), mn.reshape(shape), vn.reshape(shape)


_HBM = pl.BlockSpec(memory_space=pltpu.HBM)


def _place():
    return lax.axis_index("x"), lax.axis_index("y"), lax.axis_index("c")


def _half_rows(ref, half, h):
    idx = (slice(None),) * (len(ref.shape) - 2) + (pl.ds(half * h, h), slice(None))
    return ref.at[idx]


def _chip_exchange(name, p, mode):
    assert mode in ("half", "whole", "scatter")
    if mode == "half":
        h, W = p.shape[0] // 2, p.shape[1]
    elif mode == "whole":
        h, W = p.shape
    else:
        h, W = p.shape[1], p.shape[2]

    def body(p_ref, o_ref, send_sems, recv_sems, local_sem):
        copies = _chip_copies(p_ref, o_ref, send_sems, recv_sems, local_sem, mode, h)
        for cp in copies:
            cp.start()
        for cp in copies:
            cp.wait()

    return pl.pallas_call(
        body, name=name, in_specs=[_HBM], out_specs=_HBM,
        out_shape=jax.ShapeDtypeStruct((4, h, W), p.dtype), scratch_shapes=_CHIP_SEMS,
    )(p)


_CHIP_SEMS = [pltpu.SemaphoreType.DMA((3,)), pltpu.SemaphoreType.DMA((3,)), pltpu.SemaphoreType.DMA]


def _chip_copies(p_ref, o_ref, send_sems, recv_sems, local_sem, mode, h):
    x, y, c = _place()
    k_me = 2 * x + y

    def src(k):
        if mode == "half":
            return _half_rows(p_ref, c, h)
        return p_ref if mode == "whole" else p_ref.at[k]

    copies = [pltpu.make_async_copy(src(k_me), o_ref.at[k_me], local_sem)]
    for j, (cx, cy) in enumerate([(1 - x, y), (x, 1 - y), (1 - x, 1 - y)]):
        copies.append(pltpu.make_async_remote_copy(
            src_ref=src(2 * cx + cy), dst_ref=o_ref.at[k_me], send_sem=send_sems.at[j], recv_sem=recv_sems.at[j],
            device_id=(cx, cy, c), device_id_type=MESH))
    return copies


def _sibling_send_half(name, g):
    n, R, W = g.shape
    h = R // 2

    def body(g_ref, o_ref, send_sem, recv_sem):
        x, y, c = _place()
        cp = pltpu.make_async_remote_copy(
            src_ref=_half_rows(g_ref, 1 - c, h), dst_ref=o_ref, send_sem=send_sem, recv_sem=recv_sem,
            device_id=(x, y, 1 - c), device_id_type=MESH)
        cp.start()
        cp.wait()

    return pl.pallas_call(
        body, name=name, in_specs=[_HBM], out_specs=_HBM,
        out_shape=jax.ShapeDtypeStruct((n, h, W), g.dtype),
        scratch_shapes=[pltpu.SemaphoreType.DMA, pltpu.SemaphoreType.DMA],
    )(g)


def _halves_merge(name, mine):
    def body(m_ref, o_ref, send_sem, recv_sem):
        x, y, c = _place()
        cp = pltpu.make_async_remote_copy(
            src_ref=m_ref, dst_ref=o_ref, send_sem=send_sem, recv_sem=recv_sem,
            device_id=(x, y, 1 - c), device_id_type=MESH)
        cp.start()
        cp.wait()

    theirs = pl.pallas_call(
        body, name=name, in_specs=[_HBM], out_specs=_HBM,
        out_shape=jax.ShapeDtypeStruct(mine.shape, mine.dtype),
        scratch_shapes=[pltpu.SemaphoreType.DMA, pltpu.SemaphoreType.DMA],
    )(mine)
    south = lax.axis_index("c") == 0
    axis = mine.ndim - 2
    return jnp.concatenate([jnp.where(south, mine, theirs), jnp.where(south, theirs, mine)], axis=axis)


def _add2(name, a, b, out_dtype):
    def fn(rows, p):
        return [rows[0][...].astype(F32) + rows[1][...].astype(F32)], []

    return _rowcall(name, fn, [a, b], [], [(a.shape[1], out_dtype)], ts=PACK_TILE)[0]


def _sum4(name, b):
    _, h, W = b.shape
    ts = PACK_TILE if h % PACK_TILE == 0 else h

    def body(b_ref, o_ref):
        o_ref[...] = ((b_ref[0].astype(F32) + b_ref[1].astype(F32)) + b_ref[2].astype(F32)) + b_ref[3].astype(F32)

    nb = _nbytes((4, ts, W), b.dtype) + _nbytes((ts, W), F32)
    return pl.pallas_call(
        body, name=name, grid=(h // ts,),
        in_specs=[pl.BlockSpec((4, ts, W), lambda i: (0, i, 0))], out_specs=pl.BlockSpec((ts, W), lambda i: (i, 0)),
        out_shape=jax.ShapeDtypeStruct((h, W), F32), compiler_params=_params(("parallel",), nb),
    )(b)


PACK_W = 1024
PACK_ROWS = 16
PACK_TILE = 256
EV_SHARDED = (("ev_w_in", "col", False), ("ev_w_uq", "col", False), ("ev_w_ukv", "col", False), ("ev_w_out", "row", False))
OD_SHARDED = (("od_w_in", "col", False), ("od_b_in", "col", True), ("od_dw_w", "col", True), ("od_dw_b", "col", True),
              ("od_ln_g", "col", True), ("od_ln_b", "col", True), ("od_w_out", "row", False))
REPLICATED =("ada_b", "pre_g", "post_g", "ev_dec_f", "ev_dec_b", "ev_q_norm_g", "ev_kv_norm_g")
WEIGHTS = ("ada_w", "ada_b", "pre_g", "post_g", "ev_w_in", "ev_dec_f", "ev_dec_b", "ev_q_norm_g", "ev_w_uq", "ev_kv_norm_g",
           "ev_w_ukv", "ev_w_out", "od_w_in", "od_b_in", "od_dw_w", "od_dw_b", "od_ln_g", "od_ln_b", "od_w_out")


def _layer_entries(layers):
    out = []
    for l in layers:
        out += [(name, how, exact, l // 2) for name, how, exact in (EV_SHARDED if l % 2 == 0 else OD_SHARDED)]
    return out


def _rows_of(n):
    unit = PACK_W * PACK_ROWS
    return (n + unit - 1) // unit * PACK_ROWS


def _to_rows(flat, lead):
    n = flat.shape[-1]
    r = _rows_of(n)
    flat = jnp.pad(flat, [(0, 0)] * len(lead) + [(0, r * PACK_W - n)])
    return flat.reshape(*lead, r, PACK_W)


def _pad_total(buf, axis):
    r = buf.shape[axis]
    r2 = (r + 2 * PACK_TILE - 1) // (2 * PACK_TILE) * (2 * PACK_TILE)
    pads = [(0, 0)] * buf.ndim
    pads[axis] = (0, r2 - r)
    return jnp.pad(buf, pads)


def _split_chips(full, how):
    if how == "col":
        n = full.shape[-1] // 4
        return jnp.moveaxis(full.reshape(*full.shape[:-1], 4, n), -2, 0)
    n = full.shape[-2] // 4
    return jnp.moveaxis(full.reshape(*full.shape[:-2], 4, n, full.shape[-1]), -3, 0)


def _join_chips(blocks, how):
    if how == "col":
        t = jnp.moveaxis(blocks, 0, -2)
        return t.reshape(*t.shape[:-2], t.shape[-2] * t.shape[-1])
    t = jnp.moveaxis(blocks, 0, -3)
    return t.reshape(*t.shape[:-3], t.shape[-3] * t.shape[-2], t.shape[-1])


def _pack_weights(w, entries):
    parts = []
    for name, _, exact, idx in entries:
        a = w[name][idx].reshape(-1)
        a = lax.bitcast_convert_type(a, BF16).reshape(-1) if exact else a.astype(BF16)
        parts.append(_to_rows(a, ()))
    return _pad_total(jnp.concatenate(parts, axis=0), 0)


def _unpack_weights(gathered, w, entries):
    out, r0 = {}, 0
    for name, how, exact, idx in entries:
        shp = w[name].shape[1:]
        n = (2 if exact else 1) * w[name][idx].size
        r = _rows_of(n)
        a = gathered[:, r0:r0 + r].reshape(4, -1)[:, :n]
        if exact:
            a = lax.bitcast_convert_type(a.reshape(4, -1, 2), F32)
        out[(name, idx)] = _join_chips(a.reshape(4, *shp), how)
        r0 += r
    return out


def _pack_grads(grads, entries, replicated):
    parts = []
    for name, how, _, idx in entries:
        b = _split_chips(grads[(name, idx)], how)
        parts.append(_to_rows(b.reshape(4, -1).astype(BF16), (4,)))
    for name in replicated:
        a = _to_rows(grads[name].reshape(-1).astype(BF16), ())
        parts.append(jnp.broadcast_to(a[None], (4, *a.shape)))
    return _pad_total(jnp.concatenate(parts, axis=1), 1)


def _reduce_begin(tag, g):
    h = g.shape[1] // 2
    c = lax.axis_index("c")
    theirs = _sibling_send_half("reduce_cores_" + tag, g)
    mine = lax.dynamic_slice_in_dim(g, c * h, h, axis=1)
    return _add2("reduce_add_" + tag, mine.reshape(4 * h, PACK_W), theirs.reshape(4 * h, PACK_W), BF16).reshape(4, h, PACK_W)


def _reduce_end(tag, exchanged):
    return _halves_merge("reduce_merge_" + tag, _sum4("reduce_sum_" + tag, exchanged))


def _unpack_grads(total, w, entries, replicated):
    out, r0 = {}, 0
    for name, _, _, idx in entries:
        n = w[name][idx].size
        r = _rows_of(n)
        out[(name, idx)] = total[r0:r0 + r].reshape(-1)[:n].reshape(w[name].shape[1:])
        r0 += r
    for name in replicated:
        n = w[name].size
        r = _rows_of(n)
        out[name] = total[r0:r0 + r].reshape(-1)[:n].reshape(w[name].shape)
        r0 += r
    return out


def _pad_heads(a, n, w, to):
    lead = a.shape[:-1]
    return jnp.pad(a.reshape(*lead, n, w), [(0, 0)] * (len(lead) + 1) + [(0, to - w)]).reshape(*lead, n * to)


def _unpad_heads(a, n, w, to):
    lead = a.shape[:-1]
    return a.reshape(*lead, n, to)[..., :w].reshape(*lead, n * w)


def _w_in_pad(w):
    return jnp.concatenate([_pad_heads(w[:, 0:256], 4, 64, HP), _pad_heads(w[:, 256:512], 4, 64, HP), w[:, 512:1536],
                            w[:, 2240:2752], w[:, 1536:2176], _pad_heads(w[:, 2176:2240], 1, 64, HP)], axis=1)


def _w_in_unpad(g):
    return jnp.concatenate([_unpad_heads(g[:, Z_RQ:Z_RK], 4, 64, HP), _unpad_heads(g[:, Z_RK:Z_RV], 4, 64, HP),
                            g[:, Z_RV:Z_MG], g[:, Z_CQ:Z_KR], g[:, Z_KR:Z_KR + 64], g[:, Z_MG:Z_CQ]], axis=1)


def _layer_weights(l, full, w):
    f32 = lambda a: a.astype(F32)
    i = l // 2
    common = dict(bias=f32(w["ada_b"][l])[None], pre_g=f32(w["pre_g"][l])[None], post_g=f32(w["post_g"][l])[None])
    if l % 2 == 0:
        w_uq_p = _pad_heads(full[("ev_w_uq", i)], MLA_HEADS, MLA_NOPE_DIM + MLA_ROPE_DIM, QW)
        lg = jnp.stack([jax.nn.log_sigmoid(f32(w["ev_dec_f"][i])), jax.nn.log_sigmoid(f32(w["ev_dec_b"][i]))])
        return dict(common, w_in=_w_in_pad(full[("ev_w_in", i)]), w_uq=w_uq_p, w_ukv=full[("ev_w_ukv", i)],
                    w_out=full[("ev_w_out", i)], qg=f32(w["ev_q_norm_g"][i])[None], kvg=f32(w["ev_kv_norm_g"][i])[None], lg=lg)
    dw = f32(full[("od_dw_w", i)])
    return dict(common, w_in=full[("od_w_in", i)], w_out=full[("od_w_out", i)], b_in=f32(full[("od_b_in", i)])[None],
                dw=jnp.pad(dw, ((0, 1), (0, 0))), dw_flip=jnp.pad(dw[::-1], ((0, 1), (0, 0))),
                dw_b=f32(full[("od_dw_b", i)])[None], ln_g=f32(full[("od_ln_g", i)])[None], ln_b=f32(full[("od_ln_b", i)])[None])


def _rope_tables(positions):
    inv_freq = ROPE_BASE ** (-jnp.arange(0, ROPE_DIM, 2, dtype=F32) / ROPE_DIM)
    ang = positions.astype(F32)[:, None] * inv_freq
    z = jnp.zeros((ang.shape[0], HP - ROPE_DIM), F32)
    return jnp.concatenate([jnp.cos(ang), jnp.cos(ang), z], axis=1), jnp.concatenate([jnp.sin(ang), jnp.sin(ang), z], axis=1)


def _even_fwd(x, mod, cos, sin, w, side=None):
    bias, pre_g, post_g = w["bias"], w["pre_g"], w["post_g"]
    h = _pre_fwd(x, pre_g, mod, bias)
    z = _mm_nn("ev_in", h, w["w_in"])
    rq, rk, rv, qn, kvn, kr = _evprep_fwd(z, cos, sin, w["qg"], w["kvg"])
    qf = _mm_nn("ev_uq", qn, w["w_uq"])
    kv = _mm_nn("ev_ukv", kvn, w["w_ukv"])
    Q, K, V = _attnprep_fwd(qf, kv, kr, cos, sin)
    a, lse, *got = _flash_fwd(Q, K, V, side)
    o2, states = _ret_fwd(w["lg"], rq, rk, rv)
    mix = _mix_fwd(o2, z, a)
    y = _mm_nn("ev_out", mix, w["w_out"])
    x_new = _post_fwd(x, y, post_g, mod, bias)
    saved = dict(x=x, h=h, z=z, rq=rq, rk=rk, rv=rv, qn=qn, kvn=kvn, Q=Q, K=K, V=V, a=a, lse=lse, o2=o2,
                 states=states, mix=mix, y=y)
    return x_new, saved, (got[0] if got else None)


def _even_bwd(dx, s, mod, cos, sin, w, side=None):
    bias, pre_g, post_g = w["bias"], w["pre_g"], w["post_g"]
    dy, dpost_g, dgate = _post_bwd(s["y"], dx, post_g, mod, bias)
    dmix = _mm_nt("ev_out_dx", dy, w["w_out"], out_dtype=BF16)
    dw_out = _mm_tn("ev_out_dw", s["mix"], dy)
    do, drg, da, dmg = _mix_bwd(s["o2"], s["z"], s["a"], dmix)
    drq, drk, drv, dlg = _ret_bwd(w["lg"], s["rq"], s["rk"], s["rv"], do, s["states"])
    dQ, dK, dV, *got = _flash_bwd(s["Q"], s["K"], s["V"], da, s["a"], s["lse"], side)
    dqf, dkv, dkr = _attnprep_bwd(dQ, dK, dV, cos, sin)
    dqn = _mm_nt("ev_uq_dx", dqf, w["w_uq"])
    dw_uq = _mm_tn("ev_uq_dw", s["qn"], dqf)
    dkvn = _mm_nt("ev_ukv_dx", dkv, w["w_ukv"])
    dw_ukv = _mm_tn("ev_ukv_dw", s["kvn"], dkv)
    dz, dqg, dkvg = _evprep_bwd(s["z"], cos, sin, w["qg"], w["kvg"], drq, drk, drv, drg, dqn, dkvn, dkr, dmg)
    dh = _mm_nt("ev_in_dx", dz, w["w_in"], out_dtype=BF16)
    dw_in = _mm_tn("ev_in_dw", s["h"], dz)
    dx_new, dpre_g, dscale, dshift = _pre_bwd(s["x"], dh, dx, pre_g, mod, bias)
    g = dict(ev_w_in=_w_in_unpad(dw_in), ev_w_uq=_unpad_heads(dw_uq, MLA_HEADS, MLA_NOPE_DIM + MLA_ROPE_DIM, QW),
             ev_w_ukv=dw_ukv, ev_w_out=dw_out, ev_q_norm_g=dqg[0], ev_kv_norm_g=dkvg[0],
             dlg_f=dlg[0, :RET_HEADS, 0], dlg_b=dlg[1, :RET_HEADS, 0])
    return dx_new, g, dpre_g[0], dpost_g[0], jnp.concatenate([dshift, dscale, dgate], axis=1), (got[0] if got else None)


def _odd_fwd(x, mod, w):
    bias, pre_g, post_g = w["bias"], w["pre_g"], w["post_g"]
    h = _pre_fwd(x, pre_g, mod, bias)
    z = _mm_nn("od_in", h, w["w_in"])
    u = _glu_fwd(z, w["b_in"])
    uc = _conv_fwd(u, w["dw"], w["dw_b"])
    vv = _odmix_fwd(uc, z, w["b_in"], w["ln_g"], w["ln_b"])
    y = _mm_nn("od_out", vv, w["w_out"])
    x_new = _post_fwd(x, y, post_g, mod, bias)
    return x_new, dict(x=x, h=h, z=z, u=u, uc=uc, vv=vv, y=y)


def _odd_bwd(dx, s, mod, w):
    bias, pre_g, post_g = w["bias"], w["pre_g"], w["post_g"]
    dy, dpost_g, dgate = _post_bwd(s["y"], dx, post_g, mod, bias)
    dvv = _mm_nt("od_out_dx", dy, w["w_out"], out_dtype=BF16)
    dw_out = _mm_tn("od_out_dw", s["vv"], dy)
    duc, dg, dbg, dln_g, dln_b = _odmix_bwd(s["uc"], s["z"], w["b_in"], w["ln_g"], w["ln_b"], dvv)
    du = _conv_fwd(duc, w["dw_flip"], jnp.zeros_like(w["dw_b"]))
    dwb = _conv_bwd_w(s["u"], duc)
    dz, dba, dbb = _glu_bwd(s["z"], w["b_in"], du, dg)
    dh = _mm_nt("od_in_dx", dz, w["w_in"], out_dtype=BF16)
    dw_in = _mm_tn("od_in_dw", s["h"], dz)
    dx_new, dpre_g, dscale, dshift = _pre_bwd(s["x"], dh, dx, pre_g, mod, bias)
    g = dict(od_w_in=dw_in, od_b_in=jnp.concatenate([dba, dbb, dbg], axis=1)[0], od_dw_w=dwb[:CONV_KERNEL], od_dw_b=dwb[CONV_KERNEL],
             od_ln_g=dln_g[0], od_ln_b=dln_b[0], od_w_out=dw_out)
    return dx_new, g, dpre_g[0], dpost_g[0], jnp.concatenate([dshift, dscale, dgate], axis=1)


LATE_LAYERS = tuple(range(1, DEPTH))


def _step(x, mod, positions, loss_target, w):
    cos, sin = _rope_tables(positions)
    f32 = lambda a: a.astype(F32)
    mods = [mod[l:l + 1] for l in range(DEPTH)]
    first, late = _layer_entries([0]), _layer_entries(LATE_LAYERS)

    gathered = _halves_merge("gather_merge_first", _chip_exchange("gather_chips_first", _pack_weights(w, first), "half"))
    full = _unpack_weights(gathered, w, first)
    lw = {0: _layer_weights(0, full, w)}
    x, s0, got = _even_fwd(x, mods[0], cos, sin, lw[0], side=(_pack_weights(w, late), "half"))
    full.update(_unpack_weights(_halves_merge("gather_merge_late", got), w, late))
    saved = [s0]
    for l in LATE_LAYERS:
        lw[l] = _layer_weights(l, full, w)
        if l % 2 == 0:
            x, s, _ = _even_fwd(x, mods[l], cos, sin, lw[l])
        else:
            x, s = _odd_fwd(x, mods[l], lw[l])
        saved.append(s)
    dx, loss = _loss_head(x, loss_target)

    grads = {}
    dpre, dpost, dmod = [None] * DEPTH, [None] * DEPTH, [None] * DEPTH
    for l in reversed(LATE_LAYERS):
        if l % 2 == 0:
            dx, g, dpre[l], dpost[l], dmod[l], _ = _even_bwd(dx, saved[l], mods[l], cos, sin, lw[l])
        else:
            dx, g, dpre[l], dpost[l], dmod[l] = _odd_bwd(dx, saved[l], mods[l], lw[l])
        grads.update({(name, l // 2): val for name, val in g.items()})
    chip_sum = _reduce_begin("late", _pack_grads(grads, late, ()))
    dx, g, dpre[0], dpost[0], dmod[0], got = _even_bwd(dx, saved[0], mods[0], cos, sin, lw[0], side=(chip_sum, "scatter"))
    grads.update({(name, 0): val for name, val in g.items()})
    red = _unpack_grads(_reduce_end("late", got), w, late, ())

    n_ev = (DEPTH + 1) // 2
    grads.update(pre_g=jnp.stack(dpre), post_g=jnp.stack(dpost), ada_b=jnp.concatenate(dmod, axis=0))
    for name in ("ev_q_norm_g", "ev_kv_norm_g"):
        grads[name] = jnp.stack([grads[(name, i)] for i in range(n_ev)])
    grads["ev_dec_f"] = jnp.stack([grads[("dlg_f", i)] for i in range(n_ev)]) * jax.nn.sigmoid(-f32(w["ev_dec_f"]))
    grads["ev_dec_b"] = jnp.stack([grads[("dlg_b", i)] for i in range(n_ev)]) * jax.nn.sigmoid(-f32(w["ev_dec_b"]))
    chip_sum = _reduce_begin("first", _pack_grads(grads, first, REPLICATED))
    total = _reduce_end("first", _chip_exchange("reduce_chips_first", chip_sum, "scatter"))
    red.update(_unpack_grads(total, w, first, REPLICATED))

    out = {name: red[name] for name in REPLICATED}
    for name, _, _ in EV_SHARDED:
        out[name] = jnp.stack([red[(name, i)] for i in range(n_ev)])
    for name, _, _ in OD_SHARDED:
        out[name] = jnp.stack([red[(name, i)] for i in range(DEPTH // 2)])
    return loss[0, 0], dx, out, grads["ada_b"]


ADA_ROWS = 16
ADA_COLS = 3 * D_MODEL // 4
ADA_PACK = (8, 1024)


def _ada_pack(a):
    n = ADA_PACK[0] * ADA_PACK[1]
    return jnp.pad(a, ((0, 0), (0, n - a.shape[1]))).reshape(4, *ADA_PACK)


def _ada_unpack(a):
    return a.reshape(4, -1)[:, :DEPTH * ADA_COLS]


def _ada_forward(c, ada_w):
    D = D_MODEL
    c16 = jnp.zeros((ADA_ROWS, D), F32).at[0].set(c[0])
    c_act = _rowcall("silu_c", lambda rows, p: ([_silu(rows[0][...])], []), [c16], [], [(D, BF16)])[0]
    got = _chip_exchange("ada_c", c_act, "whole")
    c4 = jnp.pad(got[:, 0, :], ((0, ADA_ROWS - 4), (0, 0)))
    per_layer = [_mm_nn("ada", c4, ada_w[l].astype(BF16))[:4] for l in range(DEPTH)]
    p = jnp.stack(per_layer, axis=1).reshape(4, DEPTH * ADA_COLS)
    got = _chip_exchange("ada_mod", _ada_pack(p), "scatter")
    mod = _ada_unpack(got).reshape(4, DEPTH, ADA_COLS).transpose(1, 0, 2).reshape(DEPTH, 3 * D)
    return mod, c4


def _ada_backward(dmod, c4):
    p = dmod.reshape(DEPTH, 4, ADA_COLS).transpose(1, 0, 2).reshape(4, DEPTH * ADA_COLS)
    got = _chip_exchange("ada_dmod", _ada_pack(p), "scatter")
    dm4 = jnp.pad(_ada_unpack(got), ((0, ADA_ROWS - 4), (0, 0)))
    both = _halves_merge("ada_merge", jnp.concatenate([dm4, c4.astype(F32)], axis=1))
    dm32, c32 = both[:, :DEPTH * ADA_COLS].astype(BF16), both[:, DEPTH * ADA_COLS:].astype(BF16)
    return jnp.stack([_mm_tn("ada_dw", c32, dm32[:, l * ADA_COLS:(l + 1) * ADA_COLS]) for l in range(DEPTH)])


def kernel(x, c, positions, ada_w, ada_b, pre_g, post_g, ev_w_in, ev_dec_f, ev_dec_b, ev_q_norm_g, ev_w_uq, ev_kv_norm_g, ev_w_ukv, ev_w_out, od_w_in, od_b_in, od_dw_w, od_dw_b, od_ln_g, od_ln_b, od_w_out, loss_target, m_ada_w, m_ada_b, m_pre_g, m_post_g, m_ev_w_in, m_ev_dec_f, m_ev_dec_b, m_ev_q_norm_g, m_ev_w_uq, m_ev_kv_norm_g, m_ev_w_ukv, m_ev_w_out, m_od_w_in, m_od_b_in, m_od_dw_w, m_od_dw_b, m_od_ln_g, m_od_ln_b, m_od_w_out, v_ada_w, v_ada_b, v_pre_g, v_post_g, v_ev_w_in, v_ev_dec_f, v_ev_dec_b, v_ev_q_norm_g, v_ev_w_uq, v_ev_kv_norm_g, v_ev_w_ukv, v_ev_w_out, v_od_w_in, v_od_b_in, v_od_dw_w, v_od_dw_b, v_od_ln_g, v_od_ln_b, v_od_w_out):
    w = dict(ada_w=ada_w, ada_b=ada_b, pre_g=pre_g, post_g=post_g, ev_w_in=ev_w_in, ev_dec_f=ev_dec_f, ev_dec_b=ev_dec_b,
             ev_q_norm_g=ev_q_norm_g, ev_w_uq=ev_w_uq, ev_kv_norm_g=ev_kv_norm_g, ev_w_ukv=ev_w_ukv, ev_w_out=ev_w_out,
             od_w_in=od_w_in, od_b_in=od_b_in, od_dw_w=od_dw_w, od_dw_b=od_dw_b, od_ln_g=od_ln_g, od_ln_b=od_ln_b, od_w_out=od_w_out)
    m = dict(ada_w=m_ada_w, ada_b=m_ada_b, pre_g=m_pre_g, post_g=m_post_g, ev_w_in=m_ev_w_in, ev_dec_f=m_ev_dec_f, ev_dec_b=m_ev_dec_b,
             ev_q_norm_g=m_ev_q_norm_g, ev_w_uq=m_ev_w_uq, ev_kv_norm_g=m_ev_kv_norm_g, ev_w_ukv=m_ev_w_ukv, ev_w_out=m_ev_w_out,
             od_w_in=m_od_w_in, od_b_in=m_od_b_in, od_dw_w=m_od_dw_w, od_dw_b=m_od_dw_b, od_ln_g=m_od_ln_g, od_ln_b=m_od_ln_b, od_w_out=m_od_w_out)
    v = dict(ada_w=v_ada_w, ada_b=v_ada_b, pre_g=v_pre_g, post_g=v_post_g, ev_w_in=v_ev_w_in, ev_dec_f=v_ev_dec_f, ev_dec_b=v_ev_dec_b,
             ev_q_norm_g=v_ev_q_norm_g, ev_w_uq=v_ev_w_uq, ev_kv_norm_g=v_ev_kv_norm_g, ev_w_ukv=v_ev_w_ukv, ev_w_out=v_ev_w_out,
             od_w_in=v_od_w_in, od_b_in=v_od_b_in, od_dw_w=v_od_dw_w, od_dw_b=v_od_dw_b, od_ln_g=v_od_ln_g, od_ln_b=v_od_ln_b, od_w_out=v_od_w_out)

    mod, c4 = _ada_forward(c, ada_w)
    loss_local, grad_x, g, dmod = _step(x[0], mod, positions[0], loss_target[0], w)
    loss = lax.psum(loss_local, ("x", "y", "c"))
    g["ada_w"] = _ada_backward(dmod, c4)
    delta, new_m, new_v = {}, {}, {}
    for name in WEIGHTS:
        delta[name], new_m[name], new_v[name] = _adamw(w[name], g[name], m[name], v[name])
    return (loss, grad_x[None], *[g[n] for n in WEIGHTS], *[delta[n] for n in WEIGHTS],
            *[new_m[n] for n in WEIGHTS], *[new_v[n] for n in WEIGHTS])
```

```python
import functools

import jax
import jax.numpy as jnp
from jax import lax
from jax.experimental import pallas as pl
from jax.experimental.pallas import tpu as pltpu

F32 = jnp.float32
BF16 = jnp.bfloat16

D_MODEL = 1024
DEPTH = 4
RET_HEADS = 4
RET_QK_DIM = 64
RET_V_DIM = 128
MLA_HEADS = 4
MLA_Q_RANK = 384
MLA_KV_RANK = 256
MLA_NOPE_DIM = 128
MLA_ROPE_DIM = 64
MLA_V_DIM = 128
ROPE_DIM = 64
ROPE_BASE = 10000.0
CONV_KERNEL = 31
EPS = 1e-6
ADAM_LR, ADAM_B1, ADAM_B2, ADAM_EPS, ADAM_WD, ADAM_STEP = 0.001, 0.9, 0.999, 1e-08, 0.01, 10

LANES = 128
HP = 128
Z_RQ, Z_RK, Z_RV, Z_RG, Z_MG, Z_CQ, Z_CKV, Z_KR, Z_END = 0, 512, 1024, 1536, 2048, 2560, 2944, 3200, 3328
Z_LAT = Z_END - Z_CQ


def _z_ret(z):
    return (z, Z_RG, 0)


def _z_gates(z):
    return [(z, 512, Z_RG // 512), (z, 512, Z_MG // 512)]


def _z_latents(z):
    return [(z, 256, Z_CQ // 256 + i) for i in range(Z_LAT // 256)]
V7X_VMEM_BYTES = 64 * 1024 * 1024
VMEM_CAP = V7X_VMEM_BYTES - 8 * 1024 * 1024

MESH = pl.DeviceIdType.MESH


def _nbytes(shape, dtype):
    n = 1
    for s in shape:
        n *= s
    return n * jnp.dtype(dtype).itemsize


def _hbm(x):
    return pltpu.with_memory_space_constraint(x, pltpu.HBM)


def _params(sem, block_bytes):
    limit = min(VMEM_CAP, max(32 * 1024 * 1024, 2 * block_bytes + 16 * 1024 * 1024))
    return pltpu.CompilerParams(dimension_semantics=sem, vmem_limit_bytes=limit)


ROW_TILE = 512


def _rowcall(name, fn, rows, params, outs, accs=(), ts=ROW_TILE):
    rows = [r if isinstance(r, tuple) else (r, r.shape[1], 0) for r in rows]
    S = rows[0][0].shape[0]
    ts = min(ts, S)
    assert S % ts == 0, (name, S, ts)
    nr, npar, no, na = len(rows), len(params), len(outs), len(accs)

    def body(*refs):
        row_refs = refs[:nr]
        pvals = [r[...] for r in refs[nr:nr + npar]]
        out_refs = refs[nr + npar:nr + npar + no]
        acc_refs = refs[nr + npar + no:]
        ovals, avals = fn(row_refs, pvals)
        for r, v in zip(out_refs, ovals, strict=True):
            r[...] = v.astype(r.dtype)
        if na:
            @pl.when(pl.program_id(0) == 0)
            def _():
                for r in acc_refs:
                    r[...] = jnp.zeros_like(r)
            for r, v in zip(acc_refs, avals, strict=True):
                r[...] += v

    in_specs = [pl.BlockSpec((ts, w), functools.partial(lambda i, b: (i, b), b=blk)) for _, w, blk in rows]
    in_specs += [pl.BlockSpec(p.shape, lambda i: (0, 0)) for p in params]
    out_specs = [pl.BlockSpec((ts, w), lambda i: (i, 0)) for w, _ in outs]
    out_specs += [pl.BlockSpec(s, lambda i: (0, 0)) for s in accs]
    out_shape = [jax.ShapeDtypeStruct((S, w), dt) for w, dt in outs]
    out_shape += [jax.ShapeDtypeStruct(s, F32) for s in accs]
    nb = sum(_nbytes((ts, w), a.dtype) for a, w, _ in rows) + sum(_nbytes((ts, w), dt) for w, dt in outs)
    nb += sum(_nbytes(p.shape, p.dtype) for p in params) + sum(_nbytes(s, F32) for s in accs)
    res = pl.pallas_call(
        body, name=name, grid=(S // ts,), in_specs=in_specs, out_specs=out_specs, out_shape=out_shape,
        compiler_params=_params(("arbitrary",) if na else ("parallel",), 3 * nb),
    )(*[_hbm(a) for a, _, _ in rows], *params)
    return res


def _silu(x):
    return x * jax.nn.sigmoid(x)


def _rms(x, g):
    return x * lax.rsqrt(jnp.mean(x * x, axis=-1, keepdims=True) + EPS) * g


def _rot(x):
    lane = lax.broadcasted_iota(jnp.int32, x.shape, 1)
    return jnp.where(lane < ROPE_DIM // 2, -pltpu.roll(x, LANES - ROPE_DIM // 2, 1), pltpu.roll(x, ROPE_DIM // 2, 1))


def _rope(x, cos, sin):
    return x * cos + _rot(x) * sin


def _rope_t(dy, cos, sin):
    return dy * cos - _rot(dy) * sin


def _groups(ref, start, n):
    return [ref[:, start + HP * h:start + HP * (h + 1)] for h in range(n)]


def _pre_math(x, g, m_scale, b_scale, m_shift, b_shift):
    return _rms(x, g) * (1.0 + (m_scale + b_scale)) + (m_shift + b_shift)


def _post_math(x, y, g, m_gate, b_gate):
    return x + (m_gate + b_gate) * _rms(y, g)


def _evprep_fwd(z, cos, sin, qg, kvg):
    def fn(rows, p):
        z_, l0, l1, l2, c_, s_ = rows
        qg_, kvg_ = p
        cos_, sin_ = c_[...], s_[...]
        lat = jnp.concatenate([l0[...], l1[...], l2[...]], axis=1)
        rq = jnp.concatenate([_rope(v[...], cos_, sin_) for v in _groups(z_, Z_RQ, RET_HEADS)], axis=1)
        rk = jnp.concatenate([_rope(v[...], cos_, sin_) for v in _groups(z_, Z_RK, RET_HEADS)], axis=1)
        rk = rk * (RET_QK_DIM ** -0.5)
        rv = z_[:, Z_RV:Z_RG]
        qn = _rms(lat[:, :Z_CKV - Z_CQ], qg_)
        kvn = _rms(lat[:, Z_CKV - Z_CQ:Z_KR - Z_CQ], kvg_)
        kr = _rope(lat[:, Z_KR - Z_CQ:], cos_, sin_)
        return [rq, rk, rv, qn, kvn, kr], []

    return _rowcall("evprep_fwd", fn, [_z_ret(z), *_z_latents(z), cos, sin], [qg, kvg],
                    [(512, BF16), (512, BF16), (512, BF16), (MLA_Q_RANK, BF16), (MLA_KV_RANK, BF16), (HP, F32)])


def _evprep_bwd(z, cos, sin, qg, kvg, drq, drk, drv, drg, dqn, dkvn, dkr, dmg):
    def fn(rows, p):
        l0, l1, l2, c_, s_, drq_, drk_, drv_, drg_, dqn_, dkvn_, dkr_, dmg_ = rows
        qg_, kvg_ = p
        cos_, sin_ = c_[...], s_[...]
        lat = jnp.concatenate([l0[...], l1[...], l2[...]], axis=1)
        parts = []
        for h in range(RET_HEADS):
            parts.append(_rope_t(drq_[:, HP * h:HP * (h + 1)] + drq_[:, 512 + HP * h:512 + HP * (h + 1)], cos_, sin_))
        for h in range(RET_HEADS):
            t = drk_[:, HP * h:HP * (h + 1)] + drk_[:, 512 + HP * h:512 + HP * (h + 1)]
            parts.append(_rope_t(t, cos_, sin_) * (RET_QK_DIM ** -0.5))
        parts.append(drv_[:, :512] + drv_[:, 512:])
        parts += [drg_[...], dmg_[...]]
        _, vq = jax.vjp(_rms, lat[:, :Z_CKV - Z_CQ], qg_)
        dcq, dqg = vq(dqn_[...])
        _, vkv = jax.vjp(_rms, lat[:, Z_CKV - Z_CQ:Z_KR - Z_CQ], kvg_)
        dckv, dkvg = vkv(dkvn_[...])
        parts += [dcq, dckv, _rope_t(dkr_[...], cos_, sin_)]
        return [jnp.concatenate([v.astype(BF16) for v in parts], axis=1)], [dqg, dkvg]

    rows = [*_z_latents(z), cos, sin, drq, drk, drv, drg, dqn, dkvn, dkr, dmg]
    return _rowcall("evprep_bwd", fn, rows, [qg, kvg], [(Z_END, BF16)], [(1, MLA_Q_RANK), (1, MLA_KV_RANK)], ts=256)


ATT_SCALE = (MLA_NOPE_DIM + MLA_ROPE_DIM) ** -0.5
LOG2E = 1.4426950408889634
LN2 = 0.6931471805599453


def _attnprep_fwd(qf, kv, kr, cos, sin):
    qscale = ATT_SCALE * LOG2E

    def fn(rows, p):
        q_, kv_, kr_, c_, s_ = rows
        cos_, sin_ = c_[...], s_[...]
        krv = kr_[...]
        qs, ks, vs = [], [], []
        for h in range(MLA_HEADS):
            b = 2 * HP * h
            qs += [q_[:, b:b + HP] * qscale, _rope(q_[:, b + HP:b + 2 * HP], cos_, sin_) * qscale]
            ks += [kv_[:, b:b + HP], krv]
            vs += [kv_[:, b + HP:b + 2 * HP]]
        return [jnp.concatenate(qs, axis=1), jnp.concatenate(ks, axis=1), jnp.concatenate(vs, axis=1)], []

    return _rowcall("attnprep_fwd", fn, [qf, kv, kr, cos, sin], [], [(1024, BF16), (1024, BF16), (512, BF16)])


def _attnprep_bwd(dQ, dK, dV, cos, sin):
    def fn(rows, p):
        dq_, dk_, dv_, c_, s_ = rows
        cos_, sin_ = c_[...], s_[...]
        dqs, dkvs = [], []
        dkr = None
        for h in range(MLA_HEADS):
            b = 2 * HP * h
            dqs += [dq_[:, b:b + HP] * ATT_SCALE, _rope_t(dq_[:, b + HP:b + 2 * HP], cos_, sin_) * ATT_SCALE]
            dkvs += [dk_[:, b:b + HP] * LN2, dv_[:, HP * h:HP * (h + 1)]]
            t = dk_[:, b + HP:b + 2 * HP]
            dkr = t if dkr is None else dkr + t
        return [jnp.concatenate(dqs, axis=1), jnp.concatenate(dkvs, axis=1), dkr * LN2], []

    return _rowcall("attnprep_bwd", fn, [dQ, dK, dV, cos, sin], [], [(1024, BF16), (1024, BF16), (HP, F32)])


def _mix_math(o, rg, a, mg):
    outs = []
    for h in range(RET_HEADS):
        oh = o[:, HP * h:HP * (h + 1)]
        mu = jnp.mean(oh, axis=-1, keepdims=True)
        var = jnp.mean(jnp.square(oh - mu), axis=-1, keepdims=True)
        outs.append((oh - mu) * lax.rsqrt(var + EPS))
    ret = jnp.concatenate(outs, axis=1) * _silu(rg)
    return jnp.concatenate([ret, a * _silu(mg)], axis=1)


def _mix_fwd(o2, z, a):
    def fn(rows, p):
        o_, rg_, mg_, a_ = rows
        return [_mix_math(o_[:, :512] + o_[:, 512:], rg_[...], a_[...], mg_[...])], []

    return _rowcall("mix_fwd", fn, [o2, *_z_gates(z), a], [], [(1024, BF16)])[0]


def _mix_bwd(o2, z, a, dmix):
    def fn(rows, p):
        o_, rg_, mg_, a_, dm_ = rows
        _, vjp = jax.vjp(_mix_math, o_[:, :512] + o_[:, 512:], rg_[...], a_[...], mg_[...])
        return list(vjp(dm_[...].astype(F32))), []

    return _rowcall("mix_bwd", fn, [o2, *_z_gates(z), a, dmix], [], [(512, BF16), (512, F32), (512, BF16), (512, F32)])


def _glu_math(a, b, ba, bb):
    return (a + ba) * jax.nn.sigmoid(b + bb)


def _glu_fwd(z, bin_):
    D = D_MODEL

    def fn(rows, p):
        za_, zb_ = rows
        b_, = p
        return [_glu_math(za_[...], zb_[...], b_[:, :D], b_[:, D:2 * D])], []

    return _rowcall("glu_fwd", fn, [(z, D, 0), (z, D, 1)], [bin_], [(D, F32)])[0]


def _odmix_math(uc, g, bg, ln_g, ln_b):
    mu = jnp.mean(uc, axis=-1, keepdims=True)
    var = jnp.mean(jnp.square(uc - mu), axis=-1, keepdims=True)
    y = (uc - mu) * lax.rsqrt(var + EPS) * ln_g + ln_b
    return _silu(y) * _silu(g + bg)


def _odmix_fwd(uc, z, bin_, ln_g, ln_b):
    D = D_MODEL

    def fn(rows, p):
        uc_, zg_ = rows
        b_, g_, lb_ = p
        return [_odmix_math(uc_[...], zg_[...], b_[:, 2 * D:], g_, lb_)], []

    return _rowcall("odmix_fwd", fn, [uc, (z, D, 2)], [bin_, ln_g, ln_b], [(D, BF16)])[0]


def _odmix_bwd(uc, z, bin_, ln_g, ln_b, dv):
    D = D_MODEL

    def fn(rows, p):
        uc_, zg_, dv_ = rows
        b_, g_, lb_ = p
        _, vjp = jax.vjp(_odmix_math, uc_[...], zg_[...], b_[:, 2 * D:], g_, lb_)
        duc, dg, dbg, dlg, dlb = vjp(dv_[...].astype(F32))
        return [duc, dg], [dbg, dlg, dlb]

    return _rowcall("odmix_bwd", fn, [uc, (z, D, 2), dv], [bin_, ln_g, ln_b], [(D, F32), (D, BF16)], [(1, D)] * 3)


def _glu_bwd(z, bin_, du, dg):
    D = D_MODEL

    def fn(rows, p):
        za_, zb_, du_, dg_ = rows
        b_, = p
        _, vjp = jax.vjp(_glu_math, za_[...], zb_[...], b_[:, :D], b_[:, D:2 * D])
        da, db, dba, dbb = vjp(du_[...])
        return [jnp.concatenate([da.astype(BF16), db.astype(BF16), dg_[...]], axis=1)], [dba, dbb]

    return _rowcall("glu_bwd", fn, [(z, D, 0), (z, D, 1), du, dg], [bin_], [(3 * D, BF16)], [(1, D)] * 2)


def _loss_head(x, tgt):
    D = D_MODEL

    def fn(rows, p):
        err = rows[0][...] - rows[1][...]
        part = 0.5 * jnp.sum(jnp.mean(err * err, axis=-1, keepdims=True), axis=0, keepdims=True)
        return [err * (1.0 / D)], [jnp.broadcast_to(part, (1, LANES))]

    return _rowcall("loss_head", fn, [x, tgt], [], [(D, F32)], [(1, LANES)])


def _tile(n, cap):
    if n <= cap:
        return n
    best = None
    for t in range(LANES, cap + 1, LANES):
        if n % t == 0:
            best = t
    assert best is not None, (n, cap)
    return best


MM_TN_CAP = 1792
NT_DIMS = (((1,), (1,)), ((), ()))
TN_DIMS = (((0,), (0,)), ((), ()))


def _mm_nn(name, a, b, out_dtype=F32, tm=512, tn_cap=MM_TN_CAP):
    M, K = a.shape
    N = b.shape[1]
    tm = min(tm, M)
    tn = _tile(N, tn_cap)

    def body(a_ref, b_ref, o_ref):
        o_ref[...] = jnp.dot(a_ref[...], b_ref[...], preferred_element_type=F32).astype(o_ref.dtype)

    nb = _nbytes((tm, K), a.dtype) + _nbytes((K, tn), b.dtype) + _nbytes((tm, tn), out_dtype) + _nbytes((tm, tn), F32)
    return pl.pallas_call(
        body, name=name, grid=(N // tn, M // tm),
        in_specs=[pl.BlockSpec((tm, K), lambda j, i: (i, 0)), pl.BlockSpec((K, tn), lambda j, i: (0, j))],
        out_specs=pl.BlockSpec((tm, tn), lambda j, i: (i, j)),
        out_shape=jax.ShapeDtypeStruct((M, N), out_dtype),
        compiler_params=_params(("parallel", "parallel"), nb),
    )(_hbm(a), _hbm(b))


def _mm_nt(name, a, b, out_dtype=F32, tm=512, tn_cap=1024):
    M, K = a.shape
    N = b.shape[0]
    tm = min(tm, M)
    tn = _tile(N, tn_cap)

    def body(a_ref, b_ref, o_ref):
        o_ref[...] = lax.dot_general(a_ref[...], b_ref[...], NT_DIMS, preferred_element_type=F32).astype(o_ref.dtype)

    nb = _nbytes((tm, K), a.dtype) + _nbytes((tn, K), b.dtype) + _nbytes((tm, tn), out_dtype) + _nbytes((tm, tn), F32)
    return pl.pallas_call(
        body, name=name, grid=(N // tn, M // tm),
        in_specs=[pl.BlockSpec((tm, K), lambda j, i: (i, 0)), pl.BlockSpec((tn, K), lambda j, i: (j, 0))],
        out_specs=pl.BlockSpec((tm, tn), lambda j, i: (i, j)),
        out_shape=jax.ShapeDtypeStruct((M, N), out_dtype),
        compiler_params=_params(("parallel", "parallel"), nb),
    )(_hbm(a), _hbm(b))


def _pre_mm(name, x, g, mod, bias, w, tm=512):
    S, D = x.shape
    N = w.shape[1]
    tm = min(tm, S)

    def body(x_ref, g_ref, mod_ref, bias_ref, w_ref, h_ref, z_ref):
        mod_, b_ = mod_ref[...], bias_ref[...]
        h = _pre_math(x_ref[...], g_ref[...], mod_[:, D:2 * D], b_[:, D:2 * D], mod_[:, :D], b_[:, :D]).astype(BF16)
        h_ref[...] = h
        z_ref[...] = jnp.dot(h, w_ref[...], preferred_element_type=F32)

    nb = 2 * _nbytes((tm, D), F32) + _nbytes((D, N), w.dtype) + 2 * _nbytes((tm, N), F32)
    row = lambda i: (i, 0)
    whole = lambda i: (0, 0)
    return pl.pallas_call(
        body, name=name, grid=(S // tm,),
        in_specs=[pl.BlockSpec((tm, D), row), pl.BlockSpec(g.shape, whole), pl.BlockSpec(mod.shape, whole),
                  pl.BlockSpec(bias.shape, whole), pl.BlockSpec((D, N), whole)],
        out_specs=[pl.BlockSpec((tm, D), row), pl.BlockSpec((tm, N), row)],
        out_shape=[jax.ShapeDtypeStruct((S, D), BF16), jax.ShapeDtypeStruct((S, N), F32)],
        compiler_params=_params(("parallel",), nb),
    )(_hbm(x), g, mod, bias, _hbm(w))


def _mm_pre_bwd(name, dz, w, x, dx_res, g, mod, bias, tm=512):
    S, N = dz.shape
    D = w.shape[0]
    tm = min(tm, S)

    def body(dz_ref, w_ref, x_ref, res_ref, g_ref, mod_ref, bias_ref, dx_ref, dg_ref, dsc_ref, dsh_ref):
        @pl.when(pl.program_id(0) == 0)
        def _():
            for r in (dg_ref, dsc_ref, dsh_ref):
                r[...] = jnp.zeros(r.shape, F32)

        mod_, b_ = mod_ref[...], bias_ref[...]
        dh = lax.dot_general(dz_ref[...], w_ref[...], NT_DIMS, preferred_element_type=F32)
        _, vjp = jax.vjp(_pre_math, x_ref[...], g_ref[...], mod_[:, D:2 * D], b_[:, D:2 * D], mod_[:, :D], b_[:, :D])
        dx, dg, dsc, _, dsh, _ = vjp(dh)
        dx_ref[...] = dx + res_ref[...]
        dg_ref[...] += dg
        dsc_ref[...] += dsc
        dsh_ref[...] += dsh

    nb = _nbytes((tm, N), dz.dtype) + _nbytes((D, N), w.dtype) + 5 * _nbytes((tm, D), F32)
    row = lambda i: (i, 0)
    whole = lambda i: (0, 0)
    return pl.pallas_call(
        body, name=name, grid=(S // tm,),
        in_specs=[pl.BlockSpec((tm, N), row), pl.BlockSpec((D, N), whole), pl.BlockSpec((tm, D), row), pl.BlockSpec((tm, D), row),
                  pl.BlockSpec(g.shape, whole), pl.BlockSpec(mod.shape, whole), pl.BlockSpec(bias.shape, whole)],
        out_specs=[pl.BlockSpec((tm, D), row)] + [pl.BlockSpec((1, D), whole)] * 3,
        out_shape=[jax.ShapeDtypeStruct((S, D), F32)] + [jax.ShapeDtypeStruct((1, D), F32)] * 3,
        compiler_params=_params(("arbitrary",), nb),
    )(_hbm(dz), _hbm(w), _hbm(x), _hbm(dx_res), g, mod, bias)


def _mm_post(name, a, b, x, g, mod, bias, tm=512):
    S, K = a.shape
    D = b.shape[1]
    tm = min(tm, S)

    def body(a_ref, b_ref, x_ref, g_ref, mod_ref, bias_ref, y_ref, xn_ref):
        y = jnp.dot(a_ref[...], b_ref[...], preferred_element_type=F32)
        y_ref[...] = y
        xn_ref[...] = _post_math(x_ref[...], y, g_ref[...], mod_ref[:, 2 * D:], bias_ref[:, 2 * D:])

    nb = _nbytes((tm, K), a.dtype) + _nbytes((K, D), b.dtype) + 4 * _nbytes((tm, D), F32)
    row = lambda i: (i, 0)
    whole = lambda i: (0, 0)
    return pl.pallas_call(
        body, name=name, grid=(S // tm,),
        in_specs=[pl.BlockSpec((tm, K), row), pl.BlockSpec((K, D), whole), pl.BlockSpec((tm, D), row),
                  pl.BlockSpec(g.shape, whole), pl.BlockSpec(mod.shape, whole), pl.BlockSpec(bias.shape, whole)],
        out_specs=[pl.BlockSpec((tm, D), row), pl.BlockSpec((tm, D), row)],
        out_shape=[jax.ShapeDtypeStruct((S, D), F32), jax.ShapeDtypeStruct((S, D), F32)],
        compiler_params=_params(("parallel",), nb),
    )(_hbm(a), _hbm(b), _hbm(x), g, mod, bias)


def _post_bwd_mm(name, y, dxo, g, mod, bias, w, tm=512):
    S, D = y.shape
    K = w.shape[0]
    tm = min(tm, S)

    def body(y_ref, dxo_ref, g_ref, mod_ref, bias_ref, w_ref, dy_ref, dx_ref, dg_ref, dgate_ref):
        @pl.when(pl.program_id(0) == 0)
        def _():
            dg_ref[...] = jnp.zeros(dg_ref.shape, F32)
            dgate_ref[...] = jnp.zeros(dgate_ref.shape, F32)

        b_gate = bias_ref[:, 2 * D:]
        _, vjp = jax.vjp(lambda y_, g2, mg: _post_math(0.0, y_, g2, mg, b_gate), y_ref[...], g_ref[...], mod_ref[:, 2 * D:])
        dy, dg, dgate = vjp(dxo_ref[...])
        dyb = dy.astype(BF16)
        dy_ref[...] = dyb
        dx_ref[...] = lax.dot_general(dyb, w_ref[...], NT_DIMS, preferred_element_type=F32).astype(dx_ref.dtype)
        dg_ref[...] += dg
        dgate_ref[...] += dgate

    nb = 4 * _nbytes((tm, D), F32) + _nbytes((K, D), w.dtype) + _nbytes((tm, K), F32)
    row = lambda i: (i, 0)
    whole = lambda i: (0, 0)
    return pl.pallas_call(
        body, name=name, grid=(S // tm,),
        in_specs=[pl.BlockSpec((tm, D), row), pl.BlockSpec((tm, D), row), pl.BlockSpec(g.shape, whole),
                  pl.BlockSpec(mod.shape, whole), pl.BlockSpec(bias.shape, whole), pl.BlockSpec((K, D), whole)],
        out_specs=[pl.BlockSpec((tm, D), row), pl.BlockSpec((tm, K), row), pl.BlockSpec((1, D), whole), pl.BlockSpec((1, D), whole)],
        out_shape=[jax.ShapeDtypeStruct((S, D), BF16), jax.ShapeDtypeStruct((S, K), BF16),
                   jax.ShapeDtypeStruct((1, D), F32), jax.ShapeDtypeStruct((1, D), F32)],
        compiler_params=_params(("arbitrary",), nb),
    )(_hbm(y), _hbm(dxo), g, mod, bias, _hbm(w))


def _mm_tn(name, a, b, ts=1024, tm_cap=512, tn_cap=MM_TN_CAP):
    S, M = a.shape
    N = b.shape[1]
    ts = min(ts, S)
    tm = _tile(M, tm_cap)
    tn = _tile(N, tn_cap)

    def body(a_ref, b_ref, o_ref):
        @pl.when(pl.program_id(2) == 0)
        def _():
            o_ref[...] = jnp.zeros_like(o_ref)

        o_ref[...] += lax.dot_general(a_ref[...], b_ref[...], (((0,), (0,)), ((), ())), preferred_element_type=F32)

    nb = _nbytes((ts, tm), a.dtype) + _nbytes((ts, tn), b.dtype) + 2 * _nbytes((tm, tn), F32)
    return pl.pallas_call(
        body, name=name, grid=(M // tm, N // tn, S // ts),
        in_specs=[pl.BlockSpec((ts, tm), lambda i, j, s: (s, i)), pl.BlockSpec((ts, tn), lambda i, j, s: (s, j))],
        out_specs=pl.BlockSpec((tm, tn), lambda i, j, s: (i, j)),
        out_shape=jax.ShapeDtypeStruct((M, N), F32),
        compiler_params=_params(("parallel", "parallel", "arbitrary"), nb),
    )(_hbm(a), _hbm(b))


ATT_TQ = 1024
ATT_TK = 1024
ATT_TK_FWD = 1024
QW = 2 * HP


def _kv_tiles(V, tk):
    S = V.shape[0]
    return V.reshape(S // tk, tk, MLA_HEADS, HP).transpose(2, 0, 3, 1)


class _Side:
    def __init__(self, side):
        self.on = side is not None
        if self.on:
            self.p, self.mode = side
            shp = self.p.shape
            self.h, self.w = (shp[0] // 2, shp[1]) if self.mode == "half" else (shp[-2], shp[-1])
        self.in_specs = [_HBM] if self.on else []
        self.out_specs = [_HBM] if self.on else []
        self.out_shape = [jax.ShapeDtypeStruct((4, self.h, self.w), self.p.dtype)] if self.on else []
        self.scratch = list(_CHIP_SEMS) if self.on else []
        self.operands = [_hbm(self.p)] if self.on else []

    def copies(self, p_ref, x_ref, sems):
        return _chip_copies(p_ref, x_ref, *sems, self.mode, self.h)

    def start(self, p_ref, x_ref, sems, first):
        @pl.when(first)
        def _():
            for cp in self.copies(p_ref, x_ref, sems):
                cp.start()

    def finish(self, p_ref, x_ref, sems, last):
        @pl.when(last)
        def _():
            for cp in self.copies(p_ref, x_ref, sems):
                cp.wait()


def _flash_fwd(Q, K, V, side=None):
    S = Q.shape[0]
    H = MLA_HEADS
    tq, tk = min(ATT_TQ, S), min(ATT_TK_FWD, S)
    nk = S // tk
    VT = _kv_tiles(V, tk)
    sd = _Side(side)

    def body(*refs):
        if sd.on:
            q_ref, k_ref, vt_ref, p_ref, o_ref, lse_ref, x_ref, m_s, l_s, acc_s, s_a, s_b, *sems = refs
            step_id = pl.program_id(0) * (S // tq) + pl.program_id(1)
            sd.start(p_ref, x_ref, sems, step_id == 0)
        else:
            q_ref, k_ref, vt_ref, o_ref, lse_ref, m_s, l_s, acc_s, s_a, s_b = refs
        m_s[...] = jnp.full(m_s.shape, -jnp.inf, F32)
        l_s[...] = jnp.zeros(l_s.shape, F32)
        acc_s[...] = jnp.zeros(acc_s.shape, F32)

        def scores(j):
            k = k_ref[pl.ds(pl.multiple_of(j * tk, tk), tk), :]
            return lax.dot_general(k, q_ref[...], NT_DIMS, preferred_element_type=F32)

        def consume(st, j):
            m_prev = m_s[...]
            m_new = jnp.maximum(m_prev, jnp.max(st, axis=0, keepdims=True))
            alpha = jnp.exp2(m_prev - m_new)
            pt = jnp.exp2(st - m_new)
            l_s[...] = alpha * l_s[...] + jnp.sum(pt, axis=0, keepdims=True)
            acc_s[...] = alpha * acc_s[...] + jnp.dot(vt_ref[0, j], pt.astype(BF16), preferred_element_type=F32)
            m_s[...] = m_new

        if nk % 2:
            def step(j, carry):
                consume(scores(j), j)
                return carry

            lax.fori_loop(0, nk, step, 0)
        else:
            s_a[...] = scores(0)

            def pair(jj, carry):
                j0 = 2 * jj
                s_b[...] = scores(j0 + 1)
                consume(s_a[...], j0)
                s_a[...] = scores(jnp.minimum(j0 + 2, nk - 1))
                consume(s_b[...], j0 + 1)
                return carry

            lax.fori_loop(0, nk // 2, pair, 0)
        l = l_s[...]
        o_ref[...] = (acc_s[...] * (1.0 / l)).T
        lse_ref[0] = m_s[...] + jnp.log2(l)
        if sd.on:
            sd.finish(p_ref, x_ref, sems, step_id == H * (S // tq) - 1)

    nb = (_nbytes((tq, QW), BF16) + _nbytes((S, QW), BF16) + _nbytes((S, HP), BF16) + 3 * _nbytes((tq, HP), F32)
          + 4 * _nbytes((tq, tk), F32))
    return pl.pallas_call(
        body, name="flash_fwd_side" if sd.on else "flash_fwd", grid=(H, S // tq),
        in_specs=[pl.BlockSpec((tq, QW), lambda h, i: (i, h)), pl.BlockSpec((S, QW), lambda h, i: (0, h)),
                  pl.BlockSpec((1, nk, HP, tk), lambda h, i: (h, 0, 0, 0))] + sd.in_specs,
        out_specs=[pl.BlockSpec((tq, HP), lambda h, i: (i, h)), pl.BlockSpec((1, 1, tq), lambda h, i: (h, 0, i))] + sd.out_specs,
        out_shape=[jax.ShapeDtypeStruct((S, H * HP), F32), jax.ShapeDtypeStruct((H, 1, S), F32)] + sd.out_shape,
        scratch_shapes=[pltpu.VMEM((1, tq), F32), pltpu.VMEM((1, tq), F32), pltpu.VMEM((HP, tq), F32),
                        pltpu.VMEM((tk, tq), F32), pltpu.VMEM((tk, tq), F32)] + sd.scratch,
        compiler_params=_params(("arbitrary", "arbitrary") if sd.on else ("parallel", "parallel"), nb),
    )(_hbm(Q), _hbm(K), _hbm(VT), *sd.operands)


def _flash_bwd(Q, K, V, dO, O, lse, side=None):
    S = Q.shape[0]
    H = MLA_HEADS
    tq, tk = min(ATT_TQ, S), min(ATT_TK, S)
    nk = S // tk
    sd = _Side(side)

    def body(*refs):
        if sd.on:
            q_ref, do_ref, o_ref, lse_ref, k_ref, v_ref, p_ref, dq_ref, dk_ref, dv_ref, x_ref, dq_s, *sems = refs
            step_id = pl.program_id(0) * (S // tq) + pl.program_id(1)
            sd.start(p_ref, x_ref, sems, step_id == 0)
        else:
            q_ref, do_ref, o_ref, lse_ref, k_ref, v_ref, dq_ref, dk_ref, dv_ref, dq_s = refs

        @pl.when(pl.program_id(1) == 0)
        def _():
            dk_ref[...] = jnp.zeros(dk_ref.shape, F32)
            dv_ref[...] = jnp.zeros(dv_ref.shape, F32)

        delta = jnp.sum((do_ref[...].astype(F32) * o_ref[...]).T, axis=0, keepdims=True)
        lse = lse_ref[0]
        dq_s[...] = jnp.zeros(dq_s.shape, F32)

        def rows(j):
            return pl.ds(pl.multiple_of(j * tk, tk), tk)

        def scores(j):
            st = lax.dot_general(k_ref[rows(j), :], q_ref[...], NT_DIMS, preferred_element_type=F32)
            dpt = lax.dot_general(v_ref[rows(j), :], do_ref[...], NT_DIMS, preferred_element_type=F32)
            return st, dpt

        def consume(st, dpt, j):
            pt = jnp.exp2(st - lse)
            dst = (pt * (dpt - delta)).astype(BF16)
            dv_ref[rows(j), :] += jnp.dot(pt.astype(BF16), do_ref[...], preferred_element_type=F32)
            dk_ref[rows(j), :] += jnp.dot(dst, q_ref[...], preferred_element_type=F32)
            dq_s[...] += lax.dot_general(dst, k_ref[rows(j), :], TN_DIMS, preferred_element_type=F32)

        def step(j, carry):
            consume(*scores(j), j)
            return carry

        lax.fori_loop(0, nk, step, 0)
        dq_ref[...] = dq_s[...]
        if sd.on:
            sd.finish(p_ref, x_ref, sems, step_id == H * (S // tq) - 1)

    nb = (_nbytes((tq, QW), BF16) + _nbytes((tq, HP), BF16) + _nbytes((tq, HP), F32) + _nbytes((S, QW), BF16)
          + _nbytes((S, HP), BF16) + 2 * _nbytes((tq, QW), F32) + _nbytes((S, QW), F32) + _nbytes((S, HP), F32))
    return pl.pallas_call(
        body, name="flash_bwd_side" if sd.on else "flash_bwd", grid=(H, S // tq),
        in_specs=[pl.BlockSpec((tq, QW), lambda h, i: (i, h)), pl.BlockSpec((tq, HP), lambda h, i: (i, h)),
                  pl.BlockSpec((tq, HP), lambda h, i: (i, h)), pl.BlockSpec((1, 1, tq), lambda h, i: (h, 0, i)),
                  pl.BlockSpec((S, QW), lambda h, i: (0, h)), pl.BlockSpec((S, HP), lambda h, i: (0, h))] + sd.in_specs,
        out_specs=[pl.BlockSpec((tq, QW), lambda h, i: (i, h)), pl.BlockSpec((S, QW), lambda h, i: (0, h)),
                   pl.BlockSpec((S, HP), lambda h, i: (0, h))] + sd.out_specs,
        out_shape=[jax.ShapeDtypeStruct((S, H * QW), F32), jax.ShapeDtypeStruct((S, H * QW), F32),
                   jax.ShapeDtypeStruct((S, H * HP), F32)] + sd.out_shape,
        scratch_shapes=[pltpu.VMEM((tq, QW), F32)] + sd.scratch,
        compiler_params=_params(("arbitrary", "arbitrary") if sd.on else ("parallel", "arbitrary"), nb),
    )(_hbm(Q), _hbm(dO), _hbm(O), lse, _hbm(K), _hbm(V), *sd.operands)


RET_CHUNK = 256


def _ret_tables(d, lg_ref, h, C):
    ii = lax.broadcasted_iota(jnp.int32, (C, C), 0).astype(F32)
    jj = lax.broadcasted_iota(jnp.int32, (C, C), 1).astype(F32)
    ci = lax.broadcasted_iota(jnp.int32, (C, 1), 0).astype(F32)
    fwd = d == 0
    dist = jnp.where(fwd, ii - jj, jj - ii)
    mask = dist >= jnp.where(fwd, 0.0, 1.0)
    dist = jnp.maximum(dist, 0.0)
    qpos = jnp.where(fwd, ci + 1.0, C - ci)
    kpos = jnp.where(fwd, C - 1.0 - ci, ci)
    lg = lg_ref[d, h]
    D = jnp.where(mask, jnp.exp(lg * dist), 0.0)
    return D, jnp.exp(lg * qpos), jnp.exp(lg * kpos), jnp.exp(lg * C), dist, qpos, kpos


def _ret_fwd(lg, q, k, v):
    S = q.shape[0]
    C = min(RET_CHUNK, S)
    N = S // C
    H = RET_HEADS

    def chunk(d, n):
        return jnp.where(d == 0, n, N - 1 - n)

    def body(lg_ref, q_ref, k_ref, v_ref, o_ref, st_ref, state, tab_d, tab_q, tab_k):
        d, n = pl.program_id(0), pl.program_id(1)

        @pl.when(n == 0)
        def _():
            state[...] = jnp.zeros_like(state)
            for h in range(H):
                D, qw, kw, _, _, _, _ = _ret_tables(d, lg_ref, h, C)
                tab_d[h] = D
                tab_q[h] = jnp.broadcast_to(qw, (C, HP))
                tab_k[h] = jnp.broadcast_to(kw, (C, HP))

        for h in range(H):
            sl = slice(HP * h, HP * (h + 1))
            D, qw, kw = tab_d[h], tab_q[h], tab_k[h]
            gc = jnp.exp(lg_ref[d, h] * C)
            qh, kh, vh = q_ref[:, sl], k_ref[:, sl], v_ref[:, sl]
            st = state[sl, :]
            sm = lax.dot_general(qh, kh, (((1,), (1,)), ((), ())), preferred_element_type=F32) * D
            inner = jnp.dot(sm.astype(BF16), vh, preferred_element_type=F32)
            cross = qw * jnp.dot(qh, st.astype(BF16), preferred_element_type=F32)
            o_ref[:, sl] = inner + cross
            st_ref[0, 0, sl, :] = st
            kvn = lax.dot_general((kh.astype(F32) * kw).astype(BF16), vh, (((0,), (0,)), ((), ())), preferred_element_type=F32)
            state[sl, :] = gc * st + kvn

    nb = 3 * _nbytes((C, 512), BF16) + _nbytes((C, 512), F32) + 2 * _nbytes((512, HP), F32) + 8 * _nbytes((C, C), F32)
    return pl.pallas_call(
        body, name="ret_fwd", grid=(2, N),
        in_specs=[pl.BlockSpec(memory_space=pltpu.SMEM)] + [pl.BlockSpec((C, 512), lambda d, n: (chunk(d, n), 0))] * 3,
        out_specs=[pl.BlockSpec((C, 512), lambda d, n: (chunk(d, n), d)),
                   pl.BlockSpec((1, 1, 512, HP), lambda d, n: (d, chunk(d, n), 0, 0))],
        out_shape=[jax.ShapeDtypeStruct((S, 1024), F32), jax.ShapeDtypeStruct((2, N, 512, HP), F32)],
        scratch_shapes=[pltpu.VMEM((512, HP), F32), pltpu.VMEM((H, C, C), F32), pltpu.VMEM((H, C, HP), F32),
                        pltpu.VMEM((H, C, HP), F32)],
        compiler_params=_params(("arbitrary", "arbitrary"), nb),
    )(lg, _hbm(q), _hbm(k), _hbm(v))


def _ret_bwd(lg, q, k, v, do, states):
    S = q.shape[0]
    C = min(RET_CHUNK, S)
    N = S // C
    H = RET_HEADS

    def chunk(d, n):
        return jnp.where(d == 0, N - 1 - n, n)

    def body(lg_ref, q_ref, k_ref, v_ref, do_ref, st_ref, dq_ref, dk_ref, dv_ref, dlg_ref, G, accA, accB, accC,
             tab_d, tab_q, tab_k, tab_qp, tab_kp):
        d, n = pl.program_id(0), pl.program_id(1)

        @pl.when(n == 0)
        def _():
            G[...] = jnp.zeros_like(G)
            accA[...] = jnp.zeros_like(accA)
            accB[...] = jnp.zeros_like(accB)
            accC[...] = jnp.zeros_like(accC)
            for h in range(H):
                D, qw, kw, _, _, qpos, kpos = _ret_tables(d, lg_ref, h, C)
                tab_d[h] = D
                tab_q[h] = jnp.broadcast_to(qw, (C, HP))
                tab_k[h] = jnp.broadcast_to(kw, (C, HP))
                tab_qp[h] = jnp.broadcast_to(qw * qpos, (C, HP))
                tab_kp[h] = jnp.broadcast_to(kw * kpos, (C, HP))

        nt = (((1,), (1,)), ((), ()))
        tn = (((0,), (0,)), ((), ()))
        for h in range(H):
            sl = slice(HP * h, HP * (h + 1))
            D, qw, kw = tab_d[h], tab_q[h], tab_k[h]
            gc = jnp.exp(lg_ref[d, h] * C)
            qh, kh, vh, doh = q_ref[:, sl], k_ref[:, sl], v_ref[:, sl], do_ref[:, sl]
            st = st_ref[0, 0, sl, :]
            g = G[sl, :]
            stb, gb = st.astype(BF16), g.astype(BF16)
            sraw = lax.dot_general(qh, kh, nt, preferred_element_type=F32)
            dS = lax.dot_general(doh, vh, nt, preferred_element_type=F32) * D
            dSb = dS.astype(BF16)
            smb = (sraw * D).astype(BF16)
            qs = jnp.dot(qh, stb, preferred_element_type=F32)
            kg = jnp.dot(kh, gb, preferred_element_type=F32)
            dq_ref[:, sl] = jnp.dot(dSb, kh, preferred_element_type=F32) + qw * lax.dot_general(doh, stb, nt, preferred_element_type=F32)
            dk_ref[:, sl] = lax.dot_general(dSb, qh, tn, preferred_element_type=F32) + kw * lax.dot_general(vh, gb, nt, preferred_element_type=F32)
            dv_ref[:, sl] = lax.dot_general(smb, doh, tn, preferred_element_type=F32) + kw * kg
            dof, vf = doh.astype(F32), vh.astype(F32)
            accA[h] += sraw * dS
            accB[h] += tab_qp[h] * (qs * dof) + tab_kp[h] * (kg * vf)
            accC[h] += st * g
            G[sl, :] = gc * g + lax.dot_general((qh.astype(F32) * qw).astype(BF16), doh, tn, preferred_element_type=F32)

        @pl.when(n == N - 1)
        def _():
            rows = []
            for h in range(H):
                _, _, _, gc, dist, _, _ = _ret_tables(d, lg_ref, h, C)
                tot = jnp.sum(accA[h] * dist) + jnp.sum(accB[h]) + (C * gc) * jnp.sum(accC[h])
                rows.append(jnp.broadcast_to(tot, (1, HP)))
            dlg_ref[0] = jnp.concatenate(rows + [jnp.zeros((8 - H, HP), F32)], axis=0)

    nb = (4 * _nbytes((C, 512), BF16) + 3 * _nbytes((C, 512), F32) + 2 * _nbytes((512, HP), F32)
          + H * (_nbytes((C, C), F32) + _nbytes((C, HP), F32) + _nbytes((HP, HP), F32)) + 12 * _nbytes((C, C), F32))
    row = lambda d, n: (chunk(d, n), 0)
    out3 = lambda d, n: (chunk(d, n), d)
    return pl.pallas_call(
        body, name="ret_bwd", grid=(2, N),
        in_specs=[pl.BlockSpec(memory_space=pltpu.SMEM)] + [pl.BlockSpec((C, 512), row)] * 4
        + [pl.BlockSpec((1, 1, 512, HP), lambda d, n: (d, chunk(d, n), 0, 0))],
        out_specs=[pl.BlockSpec((C, 512), out3)] * 3 + [pl.BlockSpec((1, 8, HP), lambda d, n: (d, 0, 0))],
        out_shape=[jax.ShapeDtypeStruct((S, 1024), F32)] * 3 + [jax.ShapeDtypeStruct((2, 8, HP), F32)],
        scratch_shapes=[pltpu.VMEM((512, HP), F32), pltpu.VMEM((H, C, C), F32), pltpu.VMEM((H, C, HP), F32),
                        pltpu.VMEM((H, HP, HP), F32), pltpu.VMEM((H, C, C), F32)] + [pltpu.VMEM((H, C, HP), F32)] * 4,
        compiler_params=_params(("arbitrary", "arbitrary"), nb),
    )(lg, _hbm(q), _hbm(k), _hbm(v), _hbm(do), _hbm(states))


CONV_PAD = 16
CONV_TR = 256
CONV_CB = LANES


def _fill_padded(pad, u_ref, S):
    pad[0:CONV_PAD, :] = jnp.zeros((CONV_PAD, CONV_CB), F32)
    pad[CONV_PAD + S:CONV_PAD + S + CONV_PAD, :] = jnp.zeros((CONV_PAD, CONV_CB), F32)
    pad[CONV_PAD:CONV_PAD + S, :] = u_ref[...]


def _conv_fwd(u, w32, b):
    S, D = u.shape
    tr = min(CONV_TR, S)

    def body(u_ref, w_ref, b_ref, o_ref, pad):
        _fill_padded(pad, u_ref, S)
        wv = w_ref[...]
        bv = b_ref[...]

        def step(t, carry):
            r0 = pl.multiple_of(t * tr, tr)
            acc = jnp.broadcast_to(bv, (tr, CONV_CB))
            for k in range(CONV_KERNEL):
                acc = acc + pad[pl.ds(r0 + (k + 1), tr), :] * wv[k:k + 1, :]
            o_ref[pl.ds(r0, tr), :] = acc
            return carry

        lax.fori_loop(0, S // tr, step, 0)

    nb = 2 * _nbytes((S, CONV_CB), F32) + _nbytes((S + 2 * CONV_PAD, CONV_CB), F32)
    return pl.pallas_call(
        body, name="conv_fwd", grid=(D // CONV_CB,),
        in_specs=[pl.BlockSpec((S, CONV_CB), lambda j: (0, j)), pl.BlockSpec((32, CONV_CB), lambda j: (0, j)),
                  pl.BlockSpec((1, CONV_CB), lambda j: (0, j))],
        out_specs=pl.BlockSpec((S, CONV_CB), lambda j: (0, j)),
        out_shape=jax.ShapeDtypeStruct((S, D), F32),
        scratch_shapes=[pltpu.VMEM((S + 2 * CONV_PAD, CONV_CB), F32)],
        compiler_params=_params(("parallel",), nb),
    )(_hbm(u), w32, b)


def _conv_bwd_w(u, dout):
    S, D = u.shape
    tr = min(CONV_TR, S)

    def body(u_ref, d_ref, o_ref, pad, acc):
        _fill_padded(pad, u_ref, S)
        acc[...] = jnp.zeros_like(acc)

        def fold(a):
            return jnp.sum(a.reshape(tr // 8, 8, CONV_CB), axis=0)

        def step(t, carry):
            r0 = pl.multiple_of(t * tr, tr)
            dv = d_ref[pl.ds(r0, tr), :]
            for k in range(CONV_KERNEL):
                acc[8 * k:8 * k + 8, :] += fold(pad[pl.ds(r0 + (k + 1), tr), :] * dv)
            acc[8 * CONV_KERNEL:8 * CONV_KERNEL + 8, :] += fold(dv)
            return carry

        lax.fori_loop(0, S // tr, step, 0)
        o_ref[...] = jnp.sum(acc[...].reshape(32, 8, CONV_CB), axis=1)

    nb = 2 * _nbytes((S, CONV_CB), F32) + _nbytes((S + 2 * CONV_PAD, CONV_CB), F32)
    return pl.pallas_call(
        body, name="conv_bwd_w", grid=(D // CONV_CB,),
        in_specs=[pl.BlockSpec((S, CONV_CB), lambda j: (0, j)), pl.BlockSpec((S, CONV_CB), lambda j: (0, j))],
        out_specs=pl.BlockSpec((32, CONV_CB), lambda j: (0, j)),
        out_shape=jax.ShapeDtypeStruct((32, D), F32),
        scratch_shapes=[pltpu.VMEM((S + 2 * CONV_PAD, CONV_CB), F32), pltpu.VMEM((256, CONV_CB), F32)],
        compiler_params=_params(("parallel",), nb),
    )(_hbm(u), _hbm(dout))


def _adamw(w, g, m, v):
    shape = w.shape
    w2, g2, m2, v2 = [a.reshape(-1, shape[-1]) for a in (w, g, m, v)]
    W = shape[-1]

    def fn(rows, p):
        w_, g_, m_, v_ = [r[...] for r in rows]
        mn = ADAM_B1 * m_ + (1.0 - ADAM_B1) * g_
        vn = ADAM_B2 * v_ + (1.0 - ADAM_B2) * jnp.square(g_)
        m_hat = mn / (1.0 - ADAM_B1 ** ADAM_STEP)
        v_hat = vn / (1.0 - ADAM_B2 ** ADAM_STEP)
        delta = -ADAM_LR * (m_hat / (jnp.sqrt(v_hat) + ADAM_EPS) + ADAM_WD * w_)
        return [delta, mn, vn], []

    R2 = w2.shape[0]
    ts = 256 if (R2 > 256 and R2 % 256 == 0) else R2
    d, mn, vn = _rowcall("adamw", fn, [w2, g2, m2, v2], [], [(W, F32)] * 3, ts=ts)
    return d.reshape(shape), mn.reshape(shape), vn.reshape(shape)


_HBM = pl.BlockSpec(memory_space=pltpu.HBM)


def _place():
    return lax.axis_index("x"), lax.axis_index("y"), lax.axis_index("c")


def _half_rows(ref, half, h):
    idx = (slice(None),) * (len(ref.shape) - 2) + (pl.ds(half * h, h), slice(None))
    return ref.at[idx]


def _chip_exchange(name, p, mode):
    assert mode in ("half", "whole", "scatter")
    if mode == "half":
        h, W = p.shape[0] // 2, p.shape[1]
    elif mode == "whole":
        h, W = p.shape
    else:
        h, W = p.shape[1], p.shape[2]

    def body(p_ref, o_ref, send_sems, recv_sems, local_sem):
        copies = _chip_copies(p_ref, o_ref, send_sems, recv_sems, local_sem, mode, h)
        for cp in copies:
            cp.start()
        for cp in copies:
            cp.wait()

    return pl.pallas_call(
        body, name=name, in_specs=[_HBM], out_specs=_HBM,
        out_shape=jax.ShapeDtypeStruct((4, h, W), p.dtype), scratch_shapes=_CHIP_SEMS,
    )(p)


_CHIP_SEMS = [pltpu.SemaphoreType.DMA((3,)), pltpu.SemaphoreType.DMA((3,)), pltpu.SemaphoreType.DMA]


def _chip_copies(p_ref, o_ref, send_sems, recv_sems, local_sem, mode, h):
    x, y, c = _place()
    k_me = 2 * x + y

    def src(k):
        if mode == "half":
            return _half_rows(p_ref, c, h)
        return p_ref if mode == "whole" else p_ref.at[k]

    copies = [pltpu.make_async_copy(src(k_me), o_ref.at[k_me], local_sem)]
    for j, (cx, cy) in enumerate([(1 - x, y), (x, 1 - y), (1 - x, 1 - y)]):
        copies.append(pltpu.make_async_remote_copy(
            src_ref=src(2 * cx + cy), dst_ref=o_ref.at[k_me], send_sem=send_sems.at[j], recv_sem=recv_sems.at[j],
            device_id=(cx, cy, c), device_id_type=MESH))
    return copies


def _sibling_send_half(name, g):
    n, R, W = g.shape
    h = R // 2

    def body(g_ref, o_ref, send_sem, recv_sem):
        x, y, c = _place()
        cp = pltpu.make_async_remote_copy(
            src_ref=_half_rows(g_ref, 1 - c, h), dst_ref=o_ref, send_sem=send_sem, recv_sem=recv_sem,
            device_id=(x, y, 1 - c), device_id_type=MESH)
        cp.start()
        cp.wait()

    return pl.pallas_call(
        body, name=name, in_specs=[_HBM], out_specs=_HBM,
        out_shape=jax.ShapeDtypeStruct((n, h, W), g.dtype),
        scratch_shapes=[pltpu.SemaphoreType.DMA, pltpu.SemaphoreType.DMA],
    )(g)


def _halves_merge(name, mine):
    def body(m_ref, o_ref, send_sem, recv_sem):
        x, y, c = _place()
        cp = pltpu.make_async_remote_copy(
            src_ref=m_ref, dst_ref=o_ref, send_sem=send_sem, recv_sem=recv_sem,
            device_id=(x, y, 1 - c), device_id_type=MESH)
        cp.start()
        cp.wait()

    theirs = pl.pallas_call(
        body, name=name, in_specs=[_HBM], out_specs=_HBM,
        out_shape=jax.ShapeDtypeStruct(mine.shape, mine.dtype),
        scratch_shapes=[pltpu.SemaphoreType.DMA, pltpu.SemaphoreType.DMA],
    )(mine)
    south = lax.axis_index("c") == 0
    axis = mine.ndim - 2
    return jnp.concatenate([jnp.where(south, mine, theirs), jnp.where(south, theirs, mine)], axis=axis)


def _add2(name, a, b, out_dtype):
    def fn(rows, p):
        return [rows[0][...].astype(F32) + rows[1][...].astype(F32)], []

    return _rowcall(name, fn, [a, b], [], [(a.shape[1], out_dtype)], ts=PACK_TILE)[0]


def _sum4(name, b):
    _, h, W = b.shape
    ts = PACK_TILE if h % PACK_TILE == 0 else h

    def body(b_ref, o_ref):
        o_ref[...] = ((b_ref[0].astype(F32) + b_ref[1].astype(F32)) + b_ref[2].astype(F32)) + b_ref[3].astype(F32)

    nb = _nbytes((4, ts, W), b.dtype) + _nbytes((ts, W), F32)
    return pl.pallas_call(
        body, name=name, grid=(h // ts,),
        in_specs=[pl.BlockSpec((4, ts, W), lambda i: (0, i, 0))], out_specs=pl.BlockSpec((ts, W), lambda i: (i, 0)),
        out_shape=jax.ShapeDtypeStruct((h, W), F32), compiler_params=_params(("parallel",), nb),
    )(b)


PACK_W = 1024
PACK_ROWS = 16
PACK_TILE = 256
EV_SHARDED = (("ev_w_in", "col", False), ("ev_w_uq", "col", False), ("ev_w_ukv", "col", False), ("ev_w_out", "row", False))
OD_SHARDED = (("od_w_in", "col", False), ("od_b_in", "col", True), ("od_dw_w", "col", True), ("od_dw_b", "col", True),
              ("od_ln_g", "col", True), ("od_ln_b", "col", True), ("od_w_out", "row", False))
REPLICATED =("ada_b", "pre_g", "post_g", "ev_dec_f", "ev_dec_b", "ev_q_norm_g", "ev_kv_norm_g")
WEIGHTS = ("ada_w", "ada_b", "pre_g", "post_g", "ev_w_in", "ev_dec_f", "ev_dec_b", "ev_q_norm_g", "ev_w_uq", "ev_kv_norm_g",
           "ev_w_ukv", "ev_w_out", "od_w_in", "od_b_in", "od_dw_w", "od_dw_b", "od_ln_g", "od_ln_b", "od_w_out")


def _layer_entries(layers):
    out = []
    for l in layers:
        out += [(name, how, exact, l // 2) for name, how, exact in (EV_SHARDED if l % 2 == 0 else OD_SHARDED)]
    return out


def _rows_of(n):
    unit = PACK_W * PACK_ROWS
    return (n + unit - 1) // unit * PACK_ROWS


def _to_rows(flat, lead):
    n = flat.shape[-1]
    r = _rows_of(n)
    flat = jnp.pad(flat, [(0, 0)] * len(lead) + [(0, r * PACK_W - n)])
    return flat.reshape(*lead, r, PACK_W)


def _pad_total(buf, axis):
    r = buf.shape[axis]
    r2 = (r + 2 * PACK_TILE - 1) // (2 * PACK_TILE) * (2 * PACK_TILE)
    pads = [(0, 0)] * buf.ndim
    pads[axis] = (0, r2 - r)
    return jnp.pad(buf, pads)


def _split_chips(full, how):
    if how == "col":
        n = full.shape[-1] // 4
        return jnp.moveaxis(full.reshape(*full.shape[:-1], 4, n), -2, 0)
    n = full.shape[-2] // 4
    return jnp.moveaxis(full.reshape(*full.shape[:-2], 4, n, full.shape[-1]), -3, 0)


def _join_chips(blocks, how):
    if how == "col":
        t = jnp.moveaxis(blocks, 0, -2)
        return t.reshape(*t.shape[:-2], t.shape[-2] * t.shape[-1])
    t = jnp.moveaxis(blocks, 0, -3)
    return t.reshape(*t.shape[:-3], t.shape[-3] * t.shape[-2], t.shape[-1])


def _pack_weights(w, entries):
    parts = []
    for name, _, exact, idx in entries:
        a = w[name][idx].reshape(-1)
        a = lax.bitcast_convert_type(a, BF16).reshape(-1) if exact else a.astype(BF16)
        parts.append(_to_rows(a, ()))
    return _pad_total(jnp.concatenate(parts, axis=0), 0)


def _unpack_weights(gathered, w, entries):
    out, r0 = {}, 0
    for name, how, exact, idx in entries:
        shp = w[name].shape[1:]
        n = (2 if exact else 1) * w[name][idx].size
        r = _rows_of(n)
        a = gathered[:, r0:r0 + r].reshape(4, -1)[:, :n]
        if exact:
            a = lax.bitcast_convert_type(a.reshape(4, -1, 2), F32)
        out[(name, idx)] = _join_chips(a.reshape(4, *shp), how)
        r0 += r
    return out


def _pack_grads(grads, entries, replicated):
    parts = []
    for name, how, _, idx in entries:
        b = _split_chips(grads[(name, idx)], how)
        parts.append(_to_rows(b.reshape(4, -1).astype(BF16), (4,)))
    for name in replicated:
        a = _to_rows(grads[name].reshape(-1).astype(BF16), ())
        parts.append(jnp.broadcast_to(a[None], (4, *a.shape)))
    return _pad_total(jnp.concatenate(parts, axis=1), 1)


def _reduce_begin(tag, g):
    h = g.shape[1] // 2
    c = lax.axis_index("c")
    theirs = _sibling_send_half("reduce_cores_" + tag, g)
    mine = lax.dynamic_slice_in_dim(g, c * h, h, axis=1)
    return _add2("reduce_add_" + tag, mine.reshape(4 * h, PACK_W), theirs.reshape(4 * h, PACK_W), BF16).reshape(4, h, PACK_W)


def _reduce_end(tag, exchanged):
    return _halves_merge("reduce_merge_" + tag, _sum4("reduce_sum_" + tag, exchanged))


def _unpack_grads(total, w, entries, replicated):
    out, r0 = {}, 0
    for name, _, _, idx in entries:
        n = w[name][idx].size
        r = _rows_of(n)
        out[(name, idx)] = total[r0:r0 + r].reshape(-1)[:n].reshape(w[name].shape[1:])
        r0 += r
    for name in replicated:
        n = w[name].size
        r = _rows_of(n)
        out[name] = total[r0:r0 + r].reshape(-1)[:n].reshape(w[name].shape)
        r0 += r
    return out


def _pad_heads(a, n, w, to):
    lead = a.shape[:-1]
    return jnp.pad(a.reshape(*lead, n, w), [(0, 0)] * (len(lead) + 1) + [(0, to - w)]).reshape(*lead, n * to)


def _unpad_heads(a, n, w, to):
    lead = a.shape[:-1]
    return a.reshape(*lead, n, to)[..., :w].reshape(*lead, n * w)


def _w_in_pad(w):
    return jnp.concatenate([_pad_heads(w[:, 0:256], 4, 64, HP), _pad_heads(w[:, 256:512], 4, 64, HP), w[:, 512:1536],
                            w[:, 2240:2752], w[:, 1536:2176], _pad_heads(w[:, 2176:2240], 1, 64, HP)], axis=1)


def _w_in_unpad(g):
    return jnp.concatenate([_unpad_heads(g[:, Z_RQ:Z_RK], 4, 64, HP), _unpad_heads(g[:, Z_RK:Z_RV], 4, 64, HP),
                            g[:, Z_RV:Z_MG], g[:, Z_CQ:Z_KR], g[:, Z_KR:Z_KR + 64], g[:, Z_MG:Z_CQ]], axis=1)


def _layer_weights(l, full, w):
    f32 = lambda a: a.astype(F32)
    i = l // 2
    common = dict(bias=f32(w["ada_b"][l])[None], pre_g=f32(w["pre_g"][l])[None], post_g=f32(w["post_g"][l])[None])
    if l % 2 == 0:
        w_uq_p = _pad_heads(full[("ev_w_uq", i)], MLA_HEADS, MLA_NOPE_DIM + MLA_ROPE_DIM, QW)
        lg = jnp.stack([jax.nn.log_sigmoid(f32(w["ev_dec_f"][i])), jax.nn.log_sigmoid(f32(w["ev_dec_b"][i]))])
        return dict(common, w_in=_w_in_pad(full[("ev_w_in", i)]), w_uq=w_uq_p, w_ukv=full[("ev_w_ukv", i)],
                    w_out=full[("ev_w_out", i)], qg=f32(w["ev_q_norm_g"][i])[None], kvg=f32(w["ev_kv_norm_g"][i])[None], lg=lg)
    dw = f32(full[("od_dw_w", i)])
    return dict(common, w_in=full[("od_w_in", i)], w_out=full[("od_w_out", i)], b_in=f32(full[("od_b_in", i)])[None],
                dw=jnp.pad(dw, ((0, 1), (0, 0))), dw_flip=jnp.pad(dw[::-1], ((0, 1), (0, 0))),
                dw_b=f32(full[("od_dw_b", i)])[None], ln_g=f32(full[("od_ln_g", i)])[None], ln_b=f32(full[("od_ln_b", i)])[None])


def _rope_tables(positions):
    inv_freq = ROPE_BASE ** (-jnp.arange(0, ROPE_DIM, 2, dtype=F32) / ROPE_DIM)
    ang = positions.astype(F32)[:, None] * inv_freq
    z = jnp.zeros((ang.shape[0], HP - ROPE_DIM), F32)
    return jnp.concatenate([jnp.cos(ang), jnp.cos(ang), z], axis=1), jnp.concatenate([jnp.sin(ang), jnp.sin(ang), z], axis=1)


def _even_fwd(x, mod, cos, sin, w, side=None):
    bias, pre_g, post_g = w["bias"], w["pre_g"], w["post_g"]
    h, z = _pre_mm("ev_in", x, pre_g, mod, bias, w["w_in"])
    rq, rk, rv, qn, kvn, kr = _evprep_fwd(z, cos, sin, w["qg"], w["kvg"])
    qf = _mm_nn("ev_uq", qn, w["w_uq"])
    kv = _mm_nn("ev_ukv", kvn, w["w_ukv"])
    Q, K, V = _attnprep_fwd(qf, kv, kr, cos, sin)
    a, lse, *got = _flash_fwd(Q, K, V, side)
    o2, states = _ret_fwd(w["lg"], rq, rk, rv)
    mix = _mix_fwd(o2, z, a)
    y, x_new = _mm_post("ev_out", mix, w["w_out"], x, post_g, mod, bias)
    saved = dict(x=x, h=h, z=z, rq=rq, rk=rk, rv=rv, qn=qn, kvn=kvn, Q=Q, K=K, V=V, a=a, lse=lse, o2=o2,
                 states=states, mix=mix, y=y)
    return x_new, saved, (got[0] if got else None)


def _even_bwd(dx, s, mod, cos, sin, w, side=None):
    bias, pre_g, post_g = w["bias"], w["pre_g"], w["post_g"]
    dy, dmix, dpost_g, dgate = _post_bwd_mm("ev_out_dx", s["y"], dx, post_g, mod, bias, w["w_out"])
    dw_out = _mm_tn("ev_out_dw", s["mix"], dy)
    do, drg, da, dmg = _mix_bwd(s["o2"], s["z"], s["a"], dmix)
    drq, drk, drv, dlg = _ret_bwd(w["lg"], s["rq"], s["rk"], s["rv"], do, s["states"])
    dQ, dK, dV, *got = _flash_bwd(s["Q"], s["K"], s["V"], da, s["a"], s["lse"], side)
    dqf, dkv, dkr = _attnprep_bwd(dQ, dK, dV, cos, sin)
    dqn = _mm_nt("ev_uq_dx", dqf, w["w_uq"])
    dw_uq = _mm_tn("ev_uq_dw", s["qn"], dqf)
    dkvn = _mm_nt("ev_ukv_dx", dkv, w["w_ukv"])
    dw_ukv = _mm_tn("ev_ukv_dw", s["kvn"], dkv)
    dz, dqg, dkvg = _evprep_bwd(s["z"], cos, sin, w["qg"], w["kvg"], drq, drk, drv, drg, dqn, dkvn, dkr, dmg)
    dw_in = _mm_tn("ev_in_dw", s["h"], dz)
    dx_new, dpre_g, dscale, dshift = _mm_pre_bwd("ev_in_dx", dz, w["w_in"], s["x"], dx, pre_g, mod, bias)
    g = dict(ev_w_in=_w_in_unpad(dw_in), ev_w_uq=_unpad_heads(dw_uq, MLA_HEADS, MLA_NOPE_DIM + MLA_ROPE_DIM, QW),
             ev_w_ukv=dw_ukv, ev_w_out=dw_out, ev_q_norm_g=dqg[0], ev_kv_norm_g=dkvg[0],
             dlg_f=dlg[0, :RET_HEADS, 0], dlg_b=dlg[1, :RET_HEADS, 0])
    return dx_new, g, dpre_g[0], dpost_g[0], jnp.concatenate([dshift, dscale, dgate], axis=1), (got[0] if got else None)


def _odd_fwd(x, mod, w):
    bias, pre_g, post_g = w["bias"], w["pre_g"], w["post_g"]
    h, z = _pre_mm("od_in", x, pre_g, mod, bias, w["w_in"])
    u = _glu_fwd(z, w["b_in"])
    uc = _conv_fwd(u, w["dw"], w["dw_b"])
    vv = _odmix_fwd(uc, z, w["b_in"], w["ln_g"], w["ln_b"])
    y, x_new = _mm_post("od_out", vv, w["w_out"], x, post_g, mod, bias)
    return x_new, dict(x=x, h=h, z=z, u=u, uc=uc, vv=vv, y=y)


def _odd_bwd(dx, s, mod, w):
    bias, pre_g, post_g = w["bias"], w["pre_g"], w["post_g"]
    dy, dvv, dpost_g, dgate = _post_bwd_mm("od_out_dx", s["y"], dx, post_g, mod, bias, w["w_out"])
    dw_out = _mm_tn("od_out_dw", s["vv"], dy)
    duc, dg, dbg, dln_g, dln_b = _odmix_bwd(s["uc"], s["z"], w["b_in"], w["ln_g"], w["ln_b"], dvv)
    du = _conv_fwd(duc, w["dw_flip"], jnp.zeros_like(w["dw_b"]))
    dwb = _conv_bwd_w(s["u"], duc)
    dz, dba, dbb = _glu_bwd(s["z"], w["b_in"], du, dg)
    dw_in = _mm_tn("od_in_dw", s["h"], dz)
    dx_new, dpre_g, dscale, dshift = _mm_pre_bwd("od_in_dx", dz, w["w_in"], s["x"], dx, pre_g, mod, bias)
    g = dict(od_w_in=dw_in, od_b_in=jnp.concatenate([dba, dbb, dbg], axis=1)[0], od_dw_w=dwb[:CONV_KERNEL], od_dw_b=dwb[CONV_KERNEL],
             od_ln_g=dln_g[0], od_ln_b=dln_b[0], od_w_out=dw_out)
    return dx_new, g, dpre_g[0], dpost_g[0], jnp.concatenate([dshift, dscale, dgate], axis=1)


LATE_LAYERS = tuple(range(1, DEPTH))


def _step(x, mod, positions, loss_target, w):
    cos, sin = _rope_tables(positions)
    f32 = lambda a: a.astype(F32)
    mods = [mod[l:l + 1] for l in range(DEPTH)]
    first, late = _layer_entries([0]), _layer_entries(LATE_LAYERS)

    gathered = _halves_merge("gather_merge_first", _chip_exchange("gather_chips_first", _pack_weights(w, first), "half"))
    full = _unpack_weights(gathered, w, first)
    lw = {0: _layer_weights(0, full, w)}
    x, s0, got = _even_fwd(x, mods[0], cos, sin, lw[0], side=(_pack_weights(w, late), "half"))
    full.update(_unpack_weights(_halves_merge("gather_merge_late", got), w, late))
    saved = [s0]
    for l in LATE_LAYERS:
        lw[l] = _layer_weights(l, full, w)
        if l % 2 == 0:
            x, s, _ = _even_fwd(x, mods[l], cos, sin, lw[l])
        else:
            x, s = _odd_fwd(x, mods[l], lw[l])
        saved.append(s)
    dx, loss = _loss_head(x, loss_target)

    grads = {}
    dpre, dpost, dmod = [None] * DEPTH, [None] * DEPTH, [None] * DEPTH
    for l in reversed(LATE_LAYERS):
        if l % 2 == 0:
            dx, g, dpre[l], dpost[l], dmod[l], _ = _even_bwd(dx, saved[l], mods[l], cos, sin, lw[l])
        else:
            dx, g, dpre[l], dpost[l], dmod[l] = _odd_bwd(dx, saved[l], mods[l], lw[l])
        grads.update({(name, l // 2): val for name, val in g.items()})
    chip_sum = _reduce_begin("late", _pack_grads(grads, late, ()))
    dx, g, dpre[0], dpost[0], dmod[0], got = _even_bwd(dx, saved[0], mods[0], cos, sin, lw[0], side=(chip_sum, "scatter"))
    grads.update({(name, 0): val for name, val in g.items()})
    red = _unpack_grads(_reduce_end("late", got), w, late, ())

    n_ev = (DEPTH + 1) // 2
    grads.update(pre_g=jnp.stack(dpre), post_g=jnp.stack(dpost), ada_b=jnp.concatenate(dmod, axis=0))
    for name in ("ev_q_norm_g", "ev_kv_norm_g"):
        grads[name] = jnp.stack([grads[(name, i)] for i in range(n_ev)])
    grads["ev_dec_f"] = jnp.stack([grads[("dlg_f", i)] for i in range(n_ev)]) * jax.nn.sigmoid(-f32(w["ev_dec_f"]))
    grads["ev_dec_b"] = jnp.stack([grads[("dlg_b", i)] for i in range(n_ev)]) * jax.nn.sigmoid(-f32(w["ev_dec_b"]))
    chip_sum = _reduce_begin("first", _pack_grads(grads, first, REPLICATED))
    total = _reduce_end("first", _chip_exchange("reduce_chips_first", chip_sum, "scatter"))
    red.update(_unpack_grads(total, w, first, REPLICATED))

    out = {name: red[name] for name in REPLICATED}
    for name, _, _ in EV_SHARDED:
        out[name] = jnp.stack([red[(name, i)] for i in range(n_ev)])
    for name, _, _ in OD_SHARDED:
        out[name] = jnp.stack([red[(name, i)] for i in range(DEPTH // 2)])
    return loss[0, 0], dx, out, grads["ada_b"]


ADA_ROWS = 16
ADA_COLS = 3 * D_MODEL // 4
ADA_PACK = (8, 1024)


def _ada_pack(a):
    n = ADA_PACK[0] * ADA_PACK[1]
    return jnp.pad(a, ((0, 0), (0, n - a.shape[1]))).reshape(4, *ADA_PACK)


def _ada_unpack(a):
    return a.reshape(4, -1)[:, :DEPTH * ADA_COLS]


def _ada_forward(c, ada_w):
    D = D_MODEL
    c16 = jnp.zeros((ADA_ROWS, D), F32).at[0].set(c[0])
    c_act = _rowcall("silu_c", lambda rows, p: ([_silu(rows[0][...])], []), [c16], [], [(D, BF16)])[0]
    got = _chip_exchange("ada_c", c_act, "whole")
    c4 = jnp.pad(got[:, 0, :], ((0, ADA_ROWS - 4), (0, 0)))
    per_layer = [_mm_nn("ada", c4, ada_w[l].astype(BF16))[:4] for l in range(DEPTH)]
    p = jnp.stack(per_layer, axis=1).reshape(4, DEPTH * ADA_COLS)
    got = _chip_exchange("ada_mod", _ada_pack(p), "scatter")
    mod = _ada_unpack(got).reshape(4, DEPTH, ADA_COLS).transpose(1, 0, 2).reshape(DEPTH, 3 * D)
    return mod, c4


def _ada_backward(dmod, c4):
    p = dmod.reshape(DEPTH, 4, ADA_COLS).transpose(1, 0, 2).reshape(4, DEPTH * ADA_COLS)
    got = _chip_exchange("ada_dmod", _ada_pack(p), "scatter")
    dm4 = jnp.pad(_ada_unpack(got), ((0, ADA_ROWS - 4), (0, 0)))
    both = _halves_merge("ada_merge", jnp.concatenate([dm4, c4.astype(F32)], axis=1))
    dm32, c32 = both[:, :DEPTH * ADA_COLS].astype(BF16), both[:, DEPTH * ADA_COLS:].astype(BF16)
    return jnp.stack([_mm_tn("ada_dw", c32, dm32[:, l * ADA_COLS:(l + 1) * ADA_COLS]) for l in range(DEPTH)])


def kernel(x, c, positions, ada_w, ada_b, pre_g, post_g, ev_w_in, ev_dec_f, ev_dec_b, ev_q_norm_g, ev_w_uq, ev_kv_norm_g, ev_w_ukv, ev_w_out, od_w_in, od_b_in, od_dw_w, od_dw_b, od_ln_g, od_ln_b, od_w_out, loss_target, m_ada_w, m_ada_b, m_pre_g, m_post_g, m_ev_w_in, m_ev_dec_f, m_ev_dec_b, m_ev_q_norm_g, m_ev_w_uq, m_ev_kv_norm_g, m_ev_w_ukv, m_ev_w_out, m_od_w_in, m_od_b_in, m_od_dw_w, m_od_dw_b, m_od_ln_g, m_od_ln_b, m_od_w_out, v_ada_w, v_ada_b, v_pre_g, v_post_g, v_ev_w_in, v_ev_dec_f, v_ev_dec_b, v_ev_q_norm_g, v_ev_w_uq, v_ev_kv_norm_g, v_ev_w_ukv, v_ev_w_out, v_od_w_in, v_od_b_in, v_od_dw_w, v_od_dw_b, v_od_ln_g, v_od_ln_b, v_od_w_out):
    w = dict(ada_w=ada_w, ada_b=ada_b, pre_g=pre_g, post_g=post_g, ev_w_in=ev_w_in, ev_dec_f=ev_dec_f, ev_dec_b=ev_dec_b,
             ev_q_norm_g=ev_q_norm_g, ev_w_uq=ev_w_uq, ev_kv_norm_g=ev_kv_norm_g, ev_w_ukv=ev_w_ukv, ev_w_out=ev_w_out,
             od_w_in=od_w_in, od_b_in=od_b_in, od_dw_w=od_dw_w, od_dw_b=od_dw_b, od_ln_g=od_ln_g, od_ln_b=od_ln_b, od_w_out=od_w_out)
    m = dict(ada_w=m_ada_w, ada_b=m_ada_b, pre_g=m_pre_g, post_g=m_post_g, ev_w_in=m_ev_w_in, ev_dec_f=m_ev_dec_f, ev_dec_b=m_ev_dec_b,
             ev_q_norm_g=m_ev_q_norm_g, ev_w_uq=m_ev_w_uq, ev_kv_norm_g=m_ev_kv_norm_g, ev_w_ukv=m_ev_w_ukv, ev_w_out=m_ev_w_out,
             od_w_in=m_od_w_in, od_b_in=m_od_b_in, od_dw_w=m_od_dw_w, od_dw_b=m_od_dw_b, od_ln_g=m_od_ln_g, od_ln_b=m_od_ln_b, od_w_out=m_od_w_out)
    v = dict(ada_w=v_ada_w, ada_b=v_ada_b, pre_g=v_pre_g, post_g=v_post_g, ev_w_in=v_ev_w_in, ev_dec_f=v_ev_dec_f, ev_dec_b=v_ev_dec_b,
             ev_q_norm_g=v_ev_q_norm_g, ev_w_uq=v_ev_w_uq, ev_kv_norm_g=v_ev_kv_norm_g, ev_w_ukv=v_ev_w_ukv, ev_w_out=v_ev_w_out,
             od_w_in=v_od_w_in, od_b_in=v_od_b_in, od_dw_w=v_od_dw_w, od_dw_b=v_od_dw_b, od_ln_g=v_od_ln_g, od_ln_b=v_od_ln_b, od_w_out=v_od_w_out)

    mod, c4 = _ada_forward(c, ada_w)
    loss_local, grad_x, g, dmod = _step(x[0], mod, positions[0], loss_target[0], w)
    loss = lax.psum(loss_local, ("x", "y", "c"))
    g["ada_w"] = _ada_backward(dmod, c4)
    delta, new_m, new_v = {}, {}, {}
    for name in WEIGHTS:
        delta[name], new_m[name], new_v[name] = _adamw(w[name], g[name], m[name], v[name])
    return (loss, grad_x[None], *[g[n] for n in WEIGHTS], *[delta[n] for n in WEIGHTS],
            *[new_m[n] for n in WEIGHTS], *[new_v[n] for n in WEIGHTS])
```

```python
import functools

import jax
import jax.numpy as jnp
from jax import lax
from jax.experimental import pallas as pl
from jax.experimental.pallas import tpu as pltpu

F32 = jnp.float32
BF16 = jnp.bfloat16

D_MODEL = 1024
DEPTH = 4
RET_HEADS = 4
RET_QK_DIM = 64
RET_V_DIM = 128
MLA_HEADS = 4
MLA_Q_RANK = 384
MLA_KV_RANK = 256
MLA_NOPE_DIM = 128
MLA_ROPE_DIM = 64
MLA_V_DIM = 128
ROPE_DIM = 64
ROPE_BASE = 10000.0
CONV_KERNEL = 31
EPS = 1e-6
ADAM_LR, ADAM_B1, ADAM_B2, ADAM_EPS, ADAM_WD, ADAM_STEP = 0.001, 0.9, 0.999, 1e-08, 0.01, 10

LANES = 128
HP = 128
Z_RQ, Z_RK, Z_RV, Z_RG, Z_MG, Z_CQ, Z_CKV, Z_KR, Z_END = 0, 512, 1024, 1536, 2048, 2560, 2944, 3200, 3328
Z_LAT = Z_END - Z_CQ


def _z_ret(z):
    return (z, Z_RG, 0)


def _z_gates(z):
    return [(z, 512, Z_RG // 512), (z, 512, Z_MG // 512)]


def _z_latents(z):
    return [(z, 256, Z_CQ // 256 + i) for i in range(Z_LAT // 256)]
V7X_VMEM_BYTES = 64 * 1024 * 1024
VMEM_CAP = V7X_VMEM_BYTES - 8 * 1024 * 1024

MESH = pl.DeviceIdType.MESH


def _nbytes(shape, dtype):
    n = 1
    for s in shape:
        n *= s
    return n * jnp.dtype(dtype).itemsize


def _hbm(x):
    return pltpu.with_memory_space_constraint(x, pltpu.HBM)


def _params(sem, block_bytes):
    limit = min(VMEM_CAP, max(32 * 1024 * 1024, 2 * block_bytes + 16 * 1024 * 1024))
    return pltpu.CompilerParams(dimension_semantics=sem, vmem_limit_bytes=limit)


ROW_TILE = 512


def _rowcall(name, fn, rows, params, outs, accs=(), ts=ROW_TILE, mats=()):
    rows = [r if isinstance(r, tuple) else (r, r.shape[1], 0) for r in rows]
    S = rows[0][0].shape[0]
    ts = min(ts, S)
    assert S % ts == 0, (name, S, ts)
    nr, npar, nm, no, na = len(rows), len(params), len(mats), len(outs), len(accs)

    def body(*refs):
        row_refs = refs[:nr]
        pvals = [r[...] for r in refs[nr:nr + npar]]
        mat_refs = refs[nr + npar:nr + npar + nm]
        out_refs = refs[nr + npar + nm:nr + npar + nm + no]
        acc_refs = refs[nr + npar + nm + no:]
        ovals, avals = fn(row_refs, pvals, mat_refs) if nm else fn(row_refs, pvals)
        for r, v in zip(out_refs, ovals, strict=True):
            r[...] = v.astype(r.dtype)
        if na:
            @pl.when(pl.program_id(0) == 0)
            def _():
                for r in acc_refs:
                    r[...] = jnp.zeros_like(r)
            for r, v in zip(acc_refs, avals, strict=True):
                r[...] += v

    in_specs = [pl.BlockSpec((ts, w), functools.partial(lambda i, b: (i, b), b=blk)) for _, w, blk in rows]
    in_specs += [pl.BlockSpec(p.shape, lambda i: (0, 0)) for p in [*params, *mats]]
    out_specs = [pl.BlockSpec((ts, w), lambda i: (i, 0)) for w, _ in outs]
    out_specs += [pl.BlockSpec(s, lambda i: (0, 0)) for s in accs]
    out_shape = [jax.ShapeDtypeStruct((S, w), dt) for w, dt in outs]
    out_shape += [jax.ShapeDtypeStruct(s, F32) for s in accs]
    nb = sum(_nbytes((ts, w), a.dtype) for a, w, _ in rows) + sum(_nbytes((ts, w), dt) for w, dt in outs)
    nb += sum(_nbytes(p.shape, p.dtype) for p in params) + sum(_nbytes(s, F32) for s in accs)
    res = pl.pallas_call(
        body, name=name, grid=(S // ts,), in_specs=in_specs, out_specs=out_specs, out_shape=out_shape,
        compiler_params=_params(("arbitrary",) if na else ("parallel",), 3 * nb + sum(_nbytes(m.shape, m.dtype) for m in mats)),
    )(*[_hbm(a) for a, _, _ in rows], *params, *[_hbm(m) for m in mats])
    return res


def _silu(x):
    return x * jax.nn.sigmoid(x)


def _rms(x, g):
    return x * lax.rsqrt(jnp.mean(x * x, axis=-1, keepdims=True) + EPS) * g


def _rot(x):
    lane = lax.broadcasted_iota(jnp.int32, x.shape, 1)
    return jnp.where(lane < ROPE_DIM // 2, -pltpu.roll(x, LANES - ROPE_DIM // 2, 1), pltpu.roll(x, ROPE_DIM // 2, 1))


def _rope(x, cos, sin):
    return x * cos + _rot(x) * sin


def _rope_t(dy, cos, sin):
    return dy * cos - _rot(dy) * sin


def _groups(ref, start, n):
    return [ref[:, start + HP * h:start + HP * (h + 1)] for h in range(n)]


def _pre_math(x, g, m_scale, b_scale, m_shift, b_shift):
    return _rms(x, g) * (1.0 + (m_scale + b_scale)) + (m_shift + b_shift)


def _post_math(x, y, g, m_gate, b_gate):
    return x + (m_gate + b_gate) * _rms(y, g)


def _evprep_fwd(z, cos, sin, qg, kvg):
    def fn(rows, p):
        z_, l0, l1, l2, c_, s_ = rows
        qg_, kvg_ = p
        cos_, sin_ = c_[...], s_[...]
        lat = jnp.concatenate([l0[...], l1[...], l2[...]], axis=1)
        rq = jnp.concatenate([_rope(v[...], cos_, sin_) for v in _groups(z_, Z_RQ, RET_HEADS)], axis=1)
        rk = jnp.concatenate([_rope(v[...], cos_, sin_) for v in _groups(z_, Z_RK, RET_HEADS)], axis=1)
        rk = rk * (RET_QK_DIM ** -0.5)
        rv = z_[:, Z_RV:Z_RG]
        qn = _rms(lat[:, :Z_CKV - Z_CQ], qg_)
        kvn = _rms(lat[:, Z_CKV - Z_CQ:Z_KR - Z_CQ], kvg_)
        kr = _rope(lat[:, Z_KR - Z_CQ:], cos_, sin_)
        return [rq, rk, rv, qn, kvn, kr], []

    return _rowcall("evprep_fwd", fn, [_z_ret(z), *_z_latents(z), cos, sin], [qg, kvg],
                    [(512, BF16), (512, BF16), (512, BF16), (MLA_Q_RANK, BF16), (MLA_KV_RANK, BF16), (HP, F32)])


def _pre_bwd_math(dh, x, res, g, mod, bias):
    D = D_MODEL
    _, vjp = jax.vjp(_pre_math, x, g, mod[:, D:2 * D], bias[:, D:2 * D], mod[:, :D], bias[:, :D])
    dx, dg, dsc, _, dsh, _ = vjp(dh)
    return dx + res, dg, dsc, dsh


def _ev_in_bwd(z, cos, sin, qg, kvg, drq, drk, drv, drg, dqn, dkvn, dkr, dmg, x, dx_res, w_in, g, mod, bias):
    D = D_MODEL

    def fn(rows, p, mats):
        l0, l1, l2, c_, s_, drq_, drk_, drv_, drg_, dqn_, dkvn_, dkr_, dmg_, x_, res_ = rows
        qg_, kvg_, g_, mod_, b_ = p
        cos_, sin_ = c_[...], s_[...]
        lat = jnp.concatenate([l0[...], l1[...], l2[...]], axis=1)
        parts = []
        for h in range(RET_HEADS):
            parts.append(_rope_t(drq_[:, HP * h:HP * (h + 1)] + drq_[:, 512 + HP * h:512 + HP * (h + 1)], cos_, sin_))
        for h in range(RET_HEADS):
            t = drk_[:, HP * h:HP * (h + 1)] + drk_[:, 512 + HP * h:512 + HP * (h + 1)]
            parts.append(_rope_t(t, cos_, sin_) * (RET_QK_DIM ** -0.5))
        parts.append(drv_[:, :512] + drv_[:, 512:])
        parts += [drg_[...], dmg_[...]]
        _, vq = jax.vjp(_rms, lat[:, :Z_CKV - Z_CQ], qg_)
        dcq, dqg = vq(dqn_[...])
        _, vkv = jax.vjp(_rms, lat[:, Z_CKV - Z_CQ:Z_KR - Z_CQ], kvg_)
        dckv, dkvg = vkv(dkvn_[...])
        parts += [dcq, dckv, _rope_t(dkr_[...], cos_, sin_)]
        dz = jnp.concatenate([v.astype(BF16) for v in parts], axis=1)
        dh = lax.dot_general(dz, mats[0][...], NT_DIMS, preferred_element_type=F32)
        dx, dg, dsc, dsh = _pre_bwd_math(dh, x_[...], res_[...], g_, mod_, b_)
        return [dz, dx], [dqg, dkvg, dg, dsc, dsh]

    rows = [*_z_latents(z), cos, sin, drq, drk, drv, drg, dqn, dkvn, dkr, dmg, x, dx_res]
    return _rowcall("ev_in_bwd", fn, rows, [qg, kvg, g, mod, bias], [(Z_END, BF16), (D, F32)],
                    [(1, MLA_Q_RANK), (1, MLA_KV_RANK)] + [(1, D)] * 3, ts=256, mats=[w_in])


ATT_SCALE = (MLA_NOPE_DIM + MLA_ROPE_DIM) ** -0.5
LOG2E = 1.4426950408889634
LN2 = 0.6931471805599453


def _attnprep_fwd(qf, kv, kr, cos, sin):
    qscale = ATT_SCALE * LOG2E

    def fn(rows, p):
        q_, kv_, kr_, c_, s_ = rows
        cos_, sin_ = c_[...], s_[...]
        krv = kr_[...]
        qs, ks, vs = [], [], []
        for h in range(MLA_HEADS):
            b = 2 * HP * h
            qs += [q_[:, b:b + HP] * qscale, _rope(q_[:, b + HP:b + 2 * HP], cos_, sin_) * qscale]
            ks += [kv_[:, b:b + HP], krv]
            vs += [kv_[:, b + HP:b + 2 * HP]]
        return [jnp.concatenate(qs, axis=1), jnp.concatenate(ks, axis=1), jnp.concatenate(vs, axis=1)], []

    return _rowcall("attnprep_fwd", fn, [qf, kv, kr, cos, sin], [], [(1024, BF16), (1024, BF16), (512, BF16)])


def _attnprep_bwd(dQ, dK, dV, cos, sin):
    def fn(rows, p):
        dq_, dk_, dv_, c_, s_ = rows
        cos_, sin_ = c_[...], s_[...]
        dqs, dkvs = [], []
        dkr = None
        for h in range(MLA_HEADS):
            b = 2 * HP * h
            dqs += [dq_[:, b:b + HP] * ATT_SCALE, _rope_t(dq_[:, b + HP:b + 2 * HP], cos_, sin_) * ATT_SCALE]
            dkvs += [dk_[:, b:b + HP] * LN2, dv_[:, HP * h:HP * (h + 1)]]
            t = dk_[:, b + HP:b + 2 * HP]
            dkr = t if dkr is None else dkr + t
        return [jnp.concatenate(dqs, axis=1), jnp.concatenate(dkvs, axis=1), dkr * LN2], []

    return _rowcall("attnprep_bwd", fn, [dQ, dK, dV, cos, sin], [], [(1024, BF16), (1024, BF16), (HP, F32)])


def _mix_math(o, rg, a, mg):
    outs = []
    for h in range(RET_HEADS):
        oh = o[:, HP * h:HP * (h + 1)]
        mu = jnp.mean(oh, axis=-1, keepdims=True)
        var = jnp.mean(jnp.square(oh - mu), axis=-1, keepdims=True)
        outs.append((oh - mu) * lax.rsqrt(var + EPS))
    ret = jnp.concatenate(outs, axis=1) * _silu(rg)
    return jnp.concatenate([ret, a * _silu(mg)], axis=1)


def _ev_out_fwd(o2, z, a, x, w_out, g, mod, bias):
    D = D_MODEL

    def fn(rows, p, mats):
        o_, rg_, mg_, a_, x_ = rows
        g_, mod_, b_ = p
        mix = _mix_math(o_[:, :512] + o_[:, 512:], rg_[...], a_[...], mg_[...]).astype(BF16)
        y = jnp.dot(mix, mats[0][...], preferred_element_type=F32)
        return [mix, y, _post_math(x_[...], y, g_, mod_[:, 2 * D:], b_[:, 2 * D:])], []

    return _rowcall("ev_out", fn, [o2, *_z_gates(z), a, x], [g, mod, bias], [(1024, BF16), (D, F32), (D, F32)], mats=[w_out])


def _post_bwd_math(y, dxo, g, mod, bias):
    D = D_MODEL
    b_gate = bias[:, 2 * D:]
    _, vjp = jax.vjp(lambda y_, g2, mg: _post_math(0.0, y_, g2, mg, b_gate), y, g, mod[:, 2 * D:])
    return vjp(dxo)


def _ev_out_bwd(y, dxo, o2, z, a, w_out, g, mod, bias):
    D = D_MODEL

    def fn(rows, p, mats):
        y_, dxo_, o_, rg_, mg_, a_ = rows
        g_, mod_, b_ = p
        dy, dg, dgate = _post_bwd_math(y_[...], dxo_[...], g_, mod_, b_)
        dyb = dy.astype(BF16)
        dmix = lax.dot_general(dyb, mats[0][...], NT_DIMS, preferred_element_type=F32)
        _, vjp = jax.vjp(_mix_math, o_[:, :512] + o_[:, 512:], rg_[...], a_[...], mg_[...])
        return [dyb, *vjp(dmix)], [dg, dgate]

    return _rowcall("ev_out_bwd", fn, [y, dxo, o2, *_z_gates(z), a], [g, mod, bias],
                    [(D, BF16), (512, BF16), (512, F32), (512, BF16), (512, F32)], [(1, D)] * 2, mats=[w_out])


def _glu_math(a, b, ba, bb):
    return (a + ba) * jax.nn.sigmoid(b + bb)


def _glu_fwd(z, bin_):
    D = D_MODEL

    def fn(rows, p):
        za_, zb_ = rows
        b_, = p
        return [_glu_math(za_[...], zb_[...], b_[:, :D], b_[:, D:2 * D])], []

    return _rowcall("glu_fwd", fn, [(z, D, 0), (z, D, 1)], [bin_], [(D, F32)])[0]


def _odmix_math(uc, g, bg, ln_g, ln_b):
    mu = jnp.mean(uc, axis=-1, keepdims=True)
    var = jnp.mean(jnp.square(uc - mu), axis=-1, keepdims=True)
    y = (uc - mu) * lax.rsqrt(var + EPS) * ln_g + ln_b
    return _silu(y) * _silu(g + bg)


def _od_out_fwd(uc, z, x, w_out, bin_, ln_g, ln_b, g, mod, bias):
    D = D_MODEL

    def fn(rows, p, mats):
        uc_, zg_, x_ = rows
        bi_, lg_, lb_, g_, mod_, b_ = p
        vv = _odmix_math(uc_[...], zg_[...], bi_[:, 2 * D:], lg_, lb_).astype(BF16)
        y = jnp.dot(vv, mats[0][...], preferred_element_type=F32)
        return [vv, y, _post_math(x_[...], y, g_, mod_[:, 2 * D:], b_[:, 2 * D:])], []

    return _rowcall("od_out", fn, [uc, (z, D, 2), x], [bin_, ln_g, ln_b, g, mod, bias],
                    [(D, BF16), (D, F32), (D, F32)], mats=[w_out])


def _od_out_bwd(y, dxo, uc, z, w_out, bin_, ln_g, ln_b, g, mod, bias):
    D = D_MODEL

    def fn(rows, p, mats):
        y_, dxo_, uc_, zg_ = rows
        bi_, lg_, lb_, g_, mod_, b_ = p
        dy, dpg, dgate = _post_bwd_math(y_[...], dxo_[...], g_, mod_, b_)
        dyb = dy.astype(BF16)
        dvv = lax.dot_general(dyb, mats[0][...], NT_DIMS, preferred_element_type=F32)
        _, vjp = jax.vjp(_odmix_math, uc_[...], zg_[...], bi_[:, 2 * D:], lg_, lb_)
        duc, dg, dbg, dlg, dlb = vjp(dvv)
        return [dyb, duc, dg], [dpg, dgate, dbg, dlg, dlb]

    return _rowcall("od_out_bwd", fn, [y, dxo, uc, (z, D, 2)], [bin_, ln_g, ln_b, g, mod, bias],
                    [(D, BF16), (D, F32), (D, BF16)], [(1, D)] * 5, mats=[w_out])


def _od_in_bwd(z, bin_, du, dg, x, dx_res, w_in, g, mod, bias):
    D = D_MODEL

    def fn(rows, p, mats):
        za_, zb_, du_, dg_, x_, res_ = rows
        bi_, g_, mod_, b_ = p
        _, vjp = jax.vjp(_glu_math, za_[...], zb_[...], bi_[:, :D], bi_[:, D:2 * D])
        da, db, dba, dbb = vjp(du_[...])
        dz = jnp.concatenate([da.astype(BF16), db.astype(BF16), dg_[...]], axis=1)
        dh = lax.dot_general(dz, mats[0][...], NT_DIMS, preferred_element_type=F32)
        dx, dpg, dsc, dsh = _pre_bwd_math(dh, x_[...], res_[...], g_, mod_, b_)
        return [dz, dx], [dba, dbb, dpg, dsc, dsh]

    return _rowcall("od_in_bwd", fn, [(z, D, 0), (z, D, 1), du, dg, x, dx_res], [bin_, g, mod, bias],
                    [(3 * D, BF16), (D, F32)], [(1, D)] * 5, ts=256, mats=[w_in])


def _loss_head(x, tgt):
    D = D_MODEL

    def fn(rows, p):
        err = rows[0][...] - rows[1][...]
        part = 0.5 * jnp.sum(jnp.mean(err * err, axis=-1, keepdims=True), axis=0, keepdims=True)
        return [err * (1.0 / D)], [jnp.broadcast_to(part, (1, LANES))]

    return _rowcall("loss_head", fn, [x, tgt], [], [(D, F32)], [(1, LANES)])


def _tile(n, cap):
    if n <= cap:
        return n
    best = None
    for t in range(LANES, cap + 1, LANES):
        if n % t == 0:
            best = t
    assert best is not None, (n, cap)
    return best


MM_TN_CAP = 1792
NT_DIMS = (((1,), (1,)), ((), ()))
TN_DIMS = (((0,), (0,)), ((), ()))


def _mm_nn(name, a, b, out_dtype=F32, tm=512, tn_cap=MM_TN_CAP):
    M, K = a.shape
    N = b.shape[1]
    tm = min(tm, M)
    tn = _tile(N, tn_cap)

    def body(a_ref, b_ref, o_ref):
        o_ref[...] = jnp.dot(a_ref[...], b_ref[...], preferred_element_type=F32).astype(o_ref.dtype)

    nb = _nbytes((tm, K), a.dtype) + _nbytes((K, tn), b.dtype) + _nbytes((tm, tn), out_dtype) + _nbytes((tm, tn), F32)
    return pl.pallas_call(
        body, name=name, grid=(N // tn, M // tm),
        in_specs=[pl.BlockSpec((tm, K), lambda j, i: (i, 0)), pl.BlockSpec((K, tn), lambda j, i: (0, j))],
        out_specs=pl.BlockSpec((tm, tn), lambda j, i: (i, j)),
        out_shape=jax.ShapeDtypeStruct((M, N), out_dtype),
        compiler_params=_params(("parallel", "parallel"), nb),
    )(_hbm(a), _hbm(b))


def _mm_nt(name, a, b, out_dtype=F32, tm=512, tn_cap=1024):
    M, K = a.shape
    N = b.shape[0]
    tm = min(tm, M)
    tn = _tile(N, tn_cap)

    def body(a_ref, b_ref, o_ref):
        o_ref[...] = lax.dot_general(a_ref[...], b_ref[...], NT_DIMS, preferred_element_type=F32).astype(o_ref.dtype)

    nb = _nbytes((tm, K), a.dtype) + _nbytes((tn, K), b.dtype) + _nbytes((tm, tn), out_dtype) + _nbytes((tm, tn), F32)
    return pl.pallas_call(
        body, name=name, grid=(N // tn, M // tm),
        in_specs=[pl.BlockSpec((tm, K), lambda j, i: (i, 0)), pl.BlockSpec((tn, K), lambda j, i: (j, 0))],
        out_specs=pl.BlockSpec((tm, tn), lambda j, i: (i, j)),
        out_shape=jax.ShapeDtypeStruct((M, N), out_dtype),
        compiler_params=_params(("parallel", "parallel"), nb),
    )(_hbm(a), _hbm(b))


def _pre_mm(name, x, g, mod, bias, w, tm=512):
    S, D = x.shape
    N = w.shape[1]
    tm = min(tm, S)

    def body(x_ref, g_ref, mod_ref, bias_ref, w_ref, h_ref, z_ref):
        mod_, b_ = mod_ref[...], bias_ref[...]
        h = _pre_math(x_ref[...], g_ref[...], mod_[:, D:2 * D], b_[:, D:2 * D], mod_[:, :D], b_[:, :D]).astype(BF16)
        h_ref[...] = h
        z_ref[...] = jnp.dot(h, w_ref[...], preferred_element_type=F32)

    nb = 2 * _nbytes((tm, D), F32) + _nbytes((D, N), w.dtype) + 2 * _nbytes((tm, N), F32)
    row = lambda i: (i, 0)
    whole = lambda i: (0, 0)
    return pl.pallas_call(
        body, name=name, grid=(S // tm,),
        in_specs=[pl.BlockSpec((tm, D), row), pl.BlockSpec(g.shape, whole), pl.BlockSpec(mod.shape, whole),
                  pl.BlockSpec(bias.shape, whole), pl.BlockSpec((D, N), whole)],
        out_specs=[pl.BlockSpec((tm, D), row), pl.BlockSpec((tm, N), row)],
        out_shape=[jax.ShapeDtypeStruct((S, D), BF16), jax.ShapeDtypeStruct((S, N), F32)],
        compiler_params=_params(("parallel",), nb),
    )(_hbm(x), g, mod, bias, _hbm(w))


def _mm_tn(name, a, b, ts=1024, tm_cap=512, tn_cap=MM_TN_CAP):
    S, M = a.shape
    N = b.shape[1]
    ts = min(ts, S)
    tm = _tile(M, tm_cap)
    tn = _tile(N, tn_cap)

    def body(a_ref, b_ref, o_ref):
        @pl.when(pl.program_id(2) == 0)
        def _():
            o_ref[...] = jnp.zeros_like(o_ref)

        o_ref[...] += lax.dot_general(a_ref[...], b_ref[...], (((0,), (0,)), ((), ())), preferred_element_type=F32)

    nb = _nbytes((ts, tm), a.dtype) + _nbytes((ts, tn), b.dtype) + 2 * _nbytes((tm, tn), F32)
    return pl.pallas_call(
        body, name=name, grid=(M // tm, N // tn, S // ts),
        in_specs=[pl.BlockSpec((ts, tm), lambda i, j, s: (s, i)), pl.BlockSpec((ts, tn), lambda i, j, s: (s, j))],
        out_specs=pl.BlockSpec((tm, tn), lambda i, j, s: (i, j)),
        out_shape=jax.ShapeDtypeStruct((M, N), F32),
        compiler_params=_params(("parallel", "parallel", "arbitrary"), nb),
    )(_hbm(a), _hbm(b))


ATT_TQ = 1024
ATT_TK = 1024
ATT_TK_FWD = 1024
QW = 2 * HP


def _kv_tiles(V, tk):
    S = V.shape[0]
    return V.reshape(S // tk, tk, MLA_HEADS, HP).transpose(2, 0, 3, 1)


class _Side:
    def __init__(self, side):
        self.on = side is not None
        if self.on:
            self.p, self.mode = side
            shp = self.p.shape
            self.h, self.w = (shp[0] // 2, shp[1]) if self.mode == "half" else (shp[-2], shp[-1])
        self.in_specs = [_HBM] if self.on else []
        self.out_specs = [_HBM] if self.on else []
        self.out_shape = [jax.ShapeDtypeStruct((4, self.h, self.w), self.p.dtype)] if self.on else []
        self.scratch = list(_CHIP_SEMS) if self.on else []
        self.operands = [_hbm(self.p)] if self.on else []

    def copies(self, p_ref, x_ref, sems):
        return _chip_copies(p_ref, x_ref, *sems, self.mode, self.h)

    def start(self, p_ref, x_ref, sems, first):
        @pl.when(first)
        def _():
            for cp in self.copies(p_ref, x_ref, sems):
                cp.start()

    def finish(self, p_ref, x_ref, sems, last):
        @pl.when(last)
        def _():
            for cp in self.copies(p_ref, x_ref, sems):
                cp.wait()


def _flash_fwd(Q, K, V, side=None):
    S = Q.shape[0]
    H = MLA_HEADS
    tq, tk = min(ATT_TQ, S), min(ATT_TK_FWD, S)
    nk = S // tk
    VT = _kv_tiles(V, tk)
    sd = _Side(side)

    def body(*refs):
        if sd.on:
            q_ref, k_ref, vt_ref, p_ref, o_ref, lse_ref, x_ref, m_s, l_s, acc_s, s_a, s_b, *sems = refs
            step_id = pl.program_id(0) * (S // tq) + pl.program_id(1)
            sd.start(p_ref, x_ref, sems, step_id == 0)
        else:
            q_ref, k_ref, vt_ref, o_ref, lse_ref, m_s, l_s, acc_s, s_a, s_b = refs
        m_s[...] = jnp.full(m_s.shape, -jnp.inf, F32)
        l_s[...] = jnp.zeros(l_s.shape, F32)
        acc_s[...] = jnp.zeros(acc_s.shape, F32)

        def scores(j):
            k = k_ref[pl.ds(pl.multiple_of(j * tk, tk), tk), :]
            return lax.dot_general(k, q_ref[...], NT_DIMS, preferred_element_type=F32)

        def consume(st, j):
            m_prev = m_s[...]
            m_new = jnp.maximum(m_prev, jnp.max(st, axis=0, keepdims=True))
            alpha = jnp.exp2(m_prev - m_new)
            pt = jnp.exp2(st - m_new)
            l_s[...] = alpha * l_s[...] + jnp.sum(pt, axis=0, keepdims=True)
            acc_s[...] = alpha * acc_s[...] + jnp.dot(vt_ref[0, j], pt.astype(BF16), preferred_element_type=F32)
            m_s[...] = m_new

        if nk % 2:
            def step(j, carry):
                consume(scores(j), j)
                return carry

            lax.fori_loop(0, nk, step, 0)
        else:
            s_a[...] = scores(0)

            def pair(jj, carry):
                j0 = 2 * jj
                s_b[...] = scores(j0 + 1)
                consume(s_a[...], j0)
                s_a[...] = scores(jnp.minimum(j0 + 2, nk - 1))
                consume(s_b[...], j0 + 1)
                return carry

            lax.fori_loop(0, nk // 2, pair, 0)
        l = l_s[...]
        o_ref[...] = (acc_s[...] * (1.0 / l)).T
        lse_ref[0] = m_s[...] + jnp.log2(l)
        if sd.on:
            sd.finish(p_ref, x_ref, sems, step_id == H * (S // tq) - 1)

    nb = (_nbytes((tq, QW), BF16) + _nbytes((S, QW), BF16) + _nbytes((S, HP), BF16) + 3 * _nbytes((tq, HP), F32)
          + 4 * _nbytes((tq, tk), F32))
    return pl.pallas_call(
        body, name="flash_fwd_side" if sd.on else "flash_fwd", grid=(H, S // tq),
        in_specs=[pl.BlockSpec((tq, QW), lambda h, i: (i, h)), pl.BlockSpec((S, QW), lambda h, i: (0, h)),
                  pl.BlockSpec((1, nk, HP, tk), lambda h, i: (h, 0, 0, 0))] + sd.in_specs,
        out_specs=[pl.BlockSpec((tq, HP), lambda h, i: (i, h)), pl.BlockSpec((1, 1, tq), lambda h, i: (h, 0, i))] + sd.out_specs,
        out_shape=[jax.ShapeDtypeStruct((S, H * HP), F32), jax.ShapeDtypeStruct((H, 1, S), F32)] + sd.out_shape,
        scratch_shapes=[pltpu.VMEM((1, tq), F32), pltpu.VMEM((1, tq), F32), pltpu.VMEM((HP, tq), F32),
                        pltpu.VMEM((tk, tq), F32), pltpu.VMEM((tk, tq), F32)] + sd.scratch,
        compiler_params=_params(("arbitrary", "arbitrary") if sd.on else ("parallel", "parallel"), nb),
    )(_hbm(Q), _hbm(K), _hbm(VT), *sd.operands)


def _flash_bwd(Q, K, V, dO, O, lse, side=None):
    S = Q.shape[0]
    H = MLA_HEADS
    tq, tk = min(ATT_TQ, S), min(ATT_TK, S)
    nk = S // tk
    sd = _Side(side)

    def body(*refs):
        if sd.on:
            q_ref, do_ref, o_ref, lse_ref, k_ref, v_ref, p_ref, dq_ref, dk_ref, dv_ref, x_ref, dq_s, *sems = refs
            step_id = pl.program_id(0) * (S // tq) + pl.program_id(1)
            sd.start(p_ref, x_ref, sems, step_id == 0)
        else:
            q_ref, do_ref, o_ref, lse_ref, k_ref, v_ref, dq_ref, dk_ref, dv_ref, dq_s = refs

        @pl.when(pl.program_id(1) == 0)
        def _():
            dk_ref[...] = jnp.zeros(dk_ref.shape, F32)
            dv_ref[...] = jnp.zeros(dv_ref.shape, F32)

        delta = jnp.sum((do_ref[...].astype(F32) * o_ref[...]).T, axis=0, keepdims=True)
        lse = lse_ref[0]
        dq_s[...] = jnp.zeros(dq_s.shape, F32)

        def rows(j):
            return pl.ds(pl.multiple_of(j * tk, tk), tk)

        def scores(j):
            st = lax.dot_general(k_ref[rows(j), :], q_ref[...], NT_DIMS, preferred_element_type=F32)
            dpt = lax.dot_general(v_ref[rows(j), :], do_ref[...], NT_DIMS, preferred_element_type=F32)
            return st, dpt

        def consume(st, dpt, j):
            pt = jnp.exp2(st - lse)
            dst = (pt * (dpt - delta)).astype(BF16)
            dv_ref[rows(j), :] += jnp.dot(pt.astype(BF16), do_ref[...], preferred_element_type=F32)
            dk_ref[rows(j), :] += jnp.dot(dst, q_ref[...], preferred_element_type=F32)
            dq_s[...] += lax.dot_general(dst, k_ref[rows(j), :], TN_DIMS, preferred_element_type=F32)

        def step(j, carry):
            consume(*scores(j), j)
            return carry

        lax.fori_loop(0, nk, step, 0)
        dq_ref[...] = dq_s[...]
        if sd.on:
            sd.finish(p_ref, x_ref, sems, step_id == H * (S // tq) - 1)

    nb = (_nbytes((tq, QW), BF16) + _nbytes((tq, HP), BF16) + _nbytes((tq, HP), F32) + _nbytes((S, QW), BF16)
          + _nbytes((S, HP), BF16) + 2 * _nbytes((tq, QW), F32) + _nbytes((S, QW), F32) + _nbytes((S, HP), F32))
    return pl.pallas_call(
        body, name="flash_bwd_side" if sd.on else "flash_bwd", grid=(H, S // tq),
        in_specs=[pl.BlockSpec((tq, QW), lambda h, i: (i, h)), pl.BlockSpec((tq, HP), lambda h, i: (i, h)),
                  pl.BlockSpec((tq, HP), lambda h, i: (i, h)), pl.BlockSpec((1, 1, tq), lambda h, i: (h, 0, i)),
                  pl.BlockSpec((S, QW), lambda h, i: (0, h)), pl.BlockSpec((S, HP), lambda h, i: (0, h))] + sd.in_specs,
        out_specs=[pl.BlockSpec((tq, QW), lambda h, i: (i, h)), pl.BlockSpec((S, QW), lambda h, i: (0, h)),
                   pl.BlockSpec((S, HP), lambda h, i: (0, h))] + sd.out_specs,
        out_shape=[jax.ShapeDtypeStruct((S, H * QW), F32), jax.ShapeDtypeStruct((S, H * QW), F32),
                   jax.ShapeDtypeStruct((S, H * HP), F32)] + sd.out_shape,
        scratch_shapes=[pltpu.VMEM((tq, QW), F32)] + sd.scratch,
        compiler_params=_params(("arbitrary", "arbitrary") if sd.on else ("parallel", "arbitrary"), nb),
    )(_hbm(Q), _hbm(dO), _hbm(O), lse, _hbm(K), _hbm(V), *sd.operands)


RET_CHUNK = 256


def _ret_tables(d, lg_ref, h, C):
    ii = lax.broadcasted_iota(jnp.int32, (C, C), 0).astype(F32)
    jj = lax.broadcasted_iota(jnp.int32, (C, C), 1).astype(F32)
    ci = lax.broadcasted_iota(jnp.int32, (C, 1), 0).astype(F32)
    fwd = d == 0
    dist = jnp.where(fwd, ii - jj, jj - ii)
    mask = dist >= jnp.where(fwd, 0.0, 1.0)
    dist = jnp.maximum(dist, 0.0)
    qpos = jnp.where(fwd, ci + 1.0, C - ci)
    kpos = jnp.where(fwd, C - 1.0 - ci, ci)
    lg = lg_ref[d, h]
    D = jnp.where(mask, jnp.exp(lg * dist), 0.0)
    return D, jnp.exp(lg * qpos), jnp.exp(lg * kpos), jnp.exp(lg * C), dist, qpos, kpos


def _ret_fwd(lg, q, k, v):
    S = q.shape[0]
    C = min(RET_CHUNK, S)
    N = S // C
    H = RET_HEADS

    def chunk(d, n):
        return jnp.where(d == 0, n, N - 1 - n)

    def body(lg_ref, q_ref, k_ref, v_ref, o_ref, st_ref, state, tab_d, tab_q, tab_k):
        d, n = pl.program_id(0), pl.program_id(1)

        @pl.when(n == 0)
        def _():
            state[...] = jnp.zeros_like(state)
            for h in range(H):
                D, qw, kw, _, _, _, _ = _ret_tables(d, lg_ref, h, C)
                tab_d[h] = D
                tab_q[h] = jnp.broadcast_to(qw, (C, HP))
                tab_k[h] = jnp.broadcast_to(kw, (C, HP))

        for h in range(H):
            sl = slice(HP * h, HP * (h + 1))
            D, qw, kw = tab_d[h], tab_q[h], tab_k[h]
            gc = jnp.exp(lg_ref[d, h] * C)
            qh, kh, vh = q_ref[:, sl], k_ref[:, sl], v_ref[:, sl]
            st = state[sl, :]
            sm = lax.dot_general(qh, kh, (((1,), (1,)), ((), ())), preferred_element_type=F32) * D
            inner = jnp.dot(sm.astype(BF16), vh, preferred_element_type=F32)
            cross = qw * jnp.dot(qh, st.astype(BF16), preferred_element_type=F32)
            o_ref[:, sl] = inner + cross
            st_ref[0, 0, sl, :] = st
            kvn = lax.dot_general((kh.astype(F32) * kw).astype(BF16), vh, (((0,), (0,)), ((), ())), preferred_element_type=F32)
            state[sl, :] = gc * st + kvn

    nb = 3 * _nbytes((C, 512), BF16) + _nbytes((C, 512), F32) + 2 * _nbytes((512, HP), F32) + 8 * _nbytes((C, C), F32)
    return pl.pallas_call(
        body, name="ret_fwd", grid=(2, N),
        in_specs=[pl.BlockSpec(memory_space=pltpu.SMEM)] + [pl.BlockSpec((C, 512), lambda d, n: (chunk(d, n), 0))] * 3,
        out_specs=[pl.BlockSpec((C, 512), lambda d, n: (chunk(d, n), d)),
                   pl.BlockSpec((1, 1, 512, HP), lambda d, n: (d, chunk(d, n), 0, 0))],
        out_shape=[jax.ShapeDtypeStruct((S, 1024), F32), jax.ShapeDtypeStruct((2, N, 512, HP), F32)],
        scratch_shapes=[pltpu.VMEM((512, HP), F32), pltpu.VMEM((H, C, C), F32), pltpu.VMEM((H, C, HP), F32),
                        pltpu.VMEM((H, C, HP), F32)],
        compiler_params=_params(("arbitrary", "arbitrary"), nb),
    )(lg, _hbm(q), _hbm(k), _hbm(v))


def _ret_bwd(lg, q, k, v, do, states):
    S = q.shape[0]
    C = min(RET_CHUNK, S)
    N = S // C
    H = RET_HEADS

    def chunk(d, n):
        return jnp.where(d == 0, N - 1 - n, n)

    def body(lg_ref, q_ref, k_ref, v_ref, do_ref, st_ref, dq_ref, dk_ref, dv_ref, dlg_ref, G, accA, accB, accC,
             tab_d, tab_q, tab_k, tab_qp, tab_kp):
        d, n = pl.program_id(0), pl.program_id(1)

        @pl.when(n == 0)
        def _():
            G[...] = jnp.zeros_like(G)
            accA[...] = jnp.zeros_like(accA)
            accB[...] = jnp.zeros_like(accB)
            accC[...] = jnp.zeros_like(accC)
            for h in range(H):
                D, qw, kw, _, _, qpos, kpos = _ret_tables(d, lg_ref, h, C)
                tab_d[h] = D
                tab_q[h] = jnp.broadcast_to(qw, (C, HP))
                tab_k[h] = jnp.broadcast_to(kw, (C, HP))
                tab_qp[h] = jnp.broadcast_to(qw * qpos, (C, HP))
                tab_kp[h] = jnp.broadcast_to(kw * kpos, (C, HP))

        nt = (((1,), (1,)), ((), ()))
        tn = (((0,), (0,)), ((), ()))
        for h in range(H):
            sl = slice(HP * h, HP * (h + 1))
            D, qw, kw = tab_d[h], tab_q[h], tab_k[h]
            gc = jnp.exp(lg_ref[d, h] * C)
            qh, kh, vh, doh = q_ref[:, sl], k_ref[:, sl], v_ref[:, sl], do_ref[:, sl]
            st = st_ref[0, 0, sl, :]
            g = G[sl, :]
            stb, gb = st.astype(BF16), g.astype(BF16)
            sraw = lax.dot_general(qh, kh, nt, preferred_element_type=F32)
            dS = lax.dot_general(doh, vh, nt, preferred_element_type=F32) * D
            dSb = dS.astype(BF16)
            smb = (sraw * D).astype(BF16)
            qs = jnp.dot(qh, stb, preferred_element_type=F32)
            kg = jnp.dot(kh, gb, preferred_element_type=F32)
            dq_ref[:, sl] = jnp.dot(dSb, kh, preferred_element_type=F32) + qw * lax.dot_general(doh, stb, nt, preferred_element_type=F32)
            dk_ref[:, sl] = lax.dot_general(dSb, qh, tn, preferred_element_type=F32) + kw * lax.dot_general(vh, gb, nt, preferred_element_type=F32)
            dv_ref[:, sl] = lax.dot_general(smb, doh, tn, preferred_element_type=F32) + kw * kg
            dof, vf = doh.astype(F32), vh.astype(F32)
            accA[h] += sraw * dS
            accB[h] += tab_qp[h] * (qs * dof) + tab_kp[h] * (kg * vf)
            accC[h] += st * g
            G[sl, :] = gc * g + lax.dot_general((qh.astype(F32) * qw).astype(BF16), doh, tn, preferred_element_type=F32)

        @pl.when(n == N - 1)
        def _():
            rows = []
            for h in range(H):
                _, _, _, gc, dist, _, _ = _ret_tables(d, lg_ref, h, C)
                tot = jnp.sum(accA[h] * dist) + jnp.sum(accB[h]) + (C * gc) * jnp.sum(accC[h])
                rows.append(jnp.broadcast_to(tot, (1, HP)))
            dlg_ref[0] = jnp.concatenate(rows + [jnp.zeros((8 - H, HP), F32)], axis=0)

    nb = (4 * _nbytes((C, 512), BF16) + 3 * _nbytes((C, 512), F32) + 2 * _nbytes((512, HP), F32)
          + H * (_nbytes((C, C), F32) + _nbytes((C, HP), F32) + _nbytes((HP, HP), F32)) + 12 * _nbytes((C, C), F32))
    row = lambda d, n: (chunk(d, n), 0)
    out3 = lambda d, n: (chunk(d, n), d)
    return pl.pallas_call(
        body, name="ret_bwd", grid=(2, N),
        in_specs=[pl.BlockSpec(memory_space=pltpu.SMEM)] + [pl.BlockSpec((C, 512), row)] * 4
        + [pl.BlockSpec((1, 1, 512, HP), lambda d, n: (d, chunk(d, n), 0, 0))],
        out_specs=[pl.BlockSpec((C, 512), out3)] * 3 + [pl.BlockSpec((1, 8, HP), lambda d, n: (d, 0, 0))],
        out_shape=[jax.ShapeDtypeStruct((S, 1024), F32)] * 3 + [jax.ShapeDtypeStruct((2, 8, HP), F32)],
        scratch_shapes=[pltpu.VMEM((512, HP), F32), pltpu.VMEM((H, C, C), F32), pltpu.VMEM((H, C, HP), F32),
                        pltpu.VMEM((H, HP, HP), F32), pltpu.VMEM((H, C, C), F32)] + [pltpu.VMEM((H, C, HP), F32)] * 4,
        compiler_params=_params(("arbitrary", "arbitrary"), nb),
    )(lg, _hbm(q), _hbm(k), _hbm(v), _hbm(do), _hbm(states))


CONV_PAD = 16
CONV_TR = 256
CONV_CB = LANES


def _fill_padded(pad, u_ref, S):
    pad[0:CONV_PAD, :] = jnp.zeros((CONV_PAD, CONV_CB), F32)
    pad[CONV_PAD + S:CONV_PAD + S + CONV_PAD, :] = jnp.zeros((CONV_PAD, CONV_CB), F32)
    pad[CONV_PAD:CONV_PAD + S, :] = u_ref[...]


def _conv_fwd(u, w32, b):
    S, D = u.shape
    tr = min(CONV_TR, S)

    def body(u_ref, w_ref, b_ref, o_ref, pad):
        _fill_padded(pad, u_ref, S)
        wv = w_ref[...]
        bv = b_ref[...]

        def step(t, carry):
            r0 = pl.multiple_of(t * tr, tr)
            acc = jnp.broadcast_to(bv, (tr, CONV_CB))
            for k in range(CONV_KERNEL):
                acc = acc + pad[pl.ds(r0 + (k + 1), tr), :] * wv[k:k + 1, :]
            o_ref[pl.ds(r0, tr), :] = acc
            return carry

        lax.fori_loop(0, S // tr, step, 0)

    nb = 2 * _nbytes((S, CONV_CB), F32) + _nbytes((S + 2 * CONV_PAD, CONV_CB), F32)
    return pl.pallas_call(
        body, name="conv_fwd", grid=(D // CONV_CB,),
        in_specs=[pl.BlockSpec((S, CONV_CB), lambda j: (0, j)), pl.BlockSpec((32, CONV_CB), lambda j: (0, j)),
                  pl.BlockSpec((1, CONV_CB), lambda j: (0, j))],
        out_specs=pl.BlockSpec((S, CONV_CB), lambda j: (0, j)),
        out_shape=jax.ShapeDtypeStruct((S, D), F32),
        scratch_shapes=[pltpu.VMEM((S + 2 * CONV_PAD, CONV_CB), F32)],
        compiler_params=_params(("parallel",), nb),
    )(_hbm(u), w32, b)


def _conv_bwd_w(u, dout):
    S, D = u.shape
    tr = min(CONV_TR, S)

    def body(u_ref, d_ref, o_ref, pad, acc):
        _fill_padded(pad, u_ref, S)
        acc[...] = jnp.zeros_like(acc)

        def fold(a):
            return jnp.sum(a.reshape(tr // 8, 8, CONV_CB), axis=0)

        def step(t, carry):
            r0 = pl.multiple_of(t * tr, tr)
            dv = d_ref[pl.ds(r0, tr), :]
            for k in range(CONV_KERNEL):
                acc[8 * k:8 * k + 8, :] += fold(pad[pl.ds(r0 + (k + 1), tr), :] * dv)
            acc[8 * CONV_KERNEL:8 * CONV_KERNEL + 8, :] += fold(dv)
            return carry

        lax.fori_loop(0, S // tr, step, 0)
        o_ref[...] = jnp.sum(acc[...].reshape(32, 8, CONV_CB), axis=1)

    nb = 2 * _nbytes((S, CONV_CB), F32) + _nbytes((S + 2 * CONV_PAD, CONV_CB), F32)
    return pl.pallas_call(
        body, name="conv_bwd_w", grid=(D // CONV_CB,),
        in_specs=[pl.BlockSpec((S, CONV_CB), lambda j: (0, j)), pl.BlockSpec((S, CONV_CB), lambda j: (0, j))],
        out_specs=pl.BlockSpec((32, CONV_CB), lambda j: (0, j)),
        out_shape=jax.ShapeDtypeStruct((32, D), F32),
        scratch_shapes=[pltpu.VMEM((S + 2 * CONV_PAD, CONV_CB), F32), pltpu.VMEM((256, CONV_CB), F32)],
        compiler_params=_params(("parallel",), nb),
    )(_hbm(u), _hbm(dout))


def _adamw(w, g, m, v):
    shape = w.shape
    w2, g2, m2, v2 = [a.reshape(-1, shape[-1]) for a in (w, g, m, v)]
    W = shape[-1]

    def fn(rows, p):
        w_, g_, m_, v_ = [r[...] for r in rows]
        mn = ADAM_B1 * m_ + (1.0 - ADAM_B1) * g_
        vn = ADAM_B2 * v_ + (1.0 - ADAM_B2) * jnp.square(g_)
        m_hat = mn / (1.0 - ADAM_B1 ** ADAM_STEP)
        v_hat = vn / (1.0 - ADAM_B2 ** ADAM_STEP)
        delta = -ADAM_LR * (m_hat / (jnp.sqrt(v_hat) + ADAM_EPS) + ADAM_WD * w_)
        return [delta, mn, vn], []

    R2 = w2.shape[0]
    ts = 256 if (R2 > 256 and R2 % 256 == 0) else R2
    d, mn, vn = _rowcall("adamw", fn, [w2, g2, m2, v2], [], [(W, F32)] * 3, ts=ts)
    return d.reshape(shape), mn.reshape(shape), vn.reshape(shape)


_HBM = pl.BlockSpec(memory_space=pltpu.HBM)


def _place():
    return lax.axis_index("x"), lax.axis_index("y"), lax.axis_index("c")


def _half_rows(ref, half, h):
    idx = (slice(None),) * (len(ref.shape) - 2) + (pl.ds(half * h, h), slice(None))
    return ref.at[idx]


def _chip_exchange(name, p, mode):
    assert mode in ("half", "whole", "scatter")
    if mode == "half":
        h, W = p.shape[0] // 2, p.shape[1]
    elif mode == "whole":
        h, W = p.shape
    else:
        h, W = p.shape[1], p.shape[2]

    def body(p_ref, o_ref, send_sems, recv_sems, local_sem):
        copies = _chip_copies(p_ref, o_ref, send_sems, recv_sems, local_sem, mode, h)
        for cp in copies:
            cp.start()
        for cp in copies:
            cp.wait()

    return pl.pallas_call(
        body, name=name, in_specs=[_HBM], out_specs=_HBM,
        out_shape=jax.ShapeDtypeStruct((4, h, W), p.dtype), scratch_shapes=_CHIP_SEMS,
    )(p)


_CHIP_SEMS = [pltpu.SemaphoreType.DMA((3,)), pltpu.SemaphoreType.DMA((3,)), pltpu.SemaphoreType.DMA]


def _chip_copies(p_ref, o_ref, send_sems, recv_sems, local_sem, mode, h):
    x, y, c = _place()
    k_me = 2 * x + y

    def src(k):
        if mode == "half":
            return _half_rows(p_ref, c, h)
        return p_ref if mode == "whole" else p_ref.at[k]

    copies = [pltpu.make_async_copy(src(k_me), o_ref.at[k_me], local_sem)]
    for j, (cx, cy) in enumerate([(1 - x, y), (x, 1 - y), (1 - x, 1 - y)]):
        copies.append(pltpu.make_async_remote_copy(
            src_ref=src(2 * cx + cy), dst_ref=o_ref.at[k_me], send_sem=send_sems.at[j], recv_sem=recv_sems.at[j],
            device_id=(cx, cy, c), device_id_type=MESH))
    return copies


def _sibling_send_half(name, g):
    n, R, W = g.shape
    h = R // 2

    def body(g_ref, o_ref, send_sem, recv_sem):
        x, y, c = _place()
        cp = pltpu.make_async_remote_copy(
            src_ref=_half_rows(g_ref, 1 - c, h), dst_ref=o_ref, send_sem=send_sem, recv_sem=recv_sem,
            device_id=(x, y, 1 - c), device_id_type=MESH)
        cp.start()
        cp.wait()

    return pl.pallas_call(
        body, name=name, in_specs=[_HBM], out_specs=_HBM,
        out_shape=jax.ShapeDtypeStruct((n, h, W), g.dtype),
        scratch_shapes=[pltpu.SemaphoreType.DMA, pltpu.SemaphoreType.DMA],
    )(g)


def _halves_merge(name, mine):
    def body(m_ref, o_ref, send_sem, recv_sem):
        x, y, c = _place()
        cp = pltpu.make_async_remote_copy(
            src_ref=m_ref, dst_ref=o_ref, send_sem=send_sem, recv_sem=recv_sem,
            device_id=(x, y, 1 - c), device_id_type=MESH)
        cp.start()
        cp.wait()

    theirs = pl.pallas_call(
        body, name=name, in_specs=[_HBM], out_specs=_HBM,
        out_shape=jax.ShapeDtypeStruct(mine.shape, mine.dtype),
        scratch_shapes=[pltpu.SemaphoreType.DMA, pltpu.SemaphoreType.DMA],
    )(mine)
    south = lax.axis_index("c") == 0
    axis = mine.ndim - 2
    return jnp.concatenate([jnp.where(south, mine, theirs), jnp.where(south, theirs, mine)], axis=axis)


def _add2(name, a, b, out_dtype):
    def fn(rows, p):
        return [rows[0][...].astype(F32) + rows[1][...].astype(F32)], []

    return _rowcall(name, fn, [a, b], [], [(a.shape[1], out_dtype)], ts=PACK_TILE)[0]


def _sum4(name, b):
    _, h, W = b.shape
    ts = PACK_TILE if h % PACK_TILE == 0 else h

    def body(b_ref, o_ref):
        o_ref[...] = ((b_ref[0].astype(F32) + b_ref[1].astype(F32)) + b_ref[2].astype(F32)) + b_ref[3].astype(F32)

    nb = _nbytes((4, ts, W), b.dtype) + _nbytes((ts, W), F32)
    return pl.pallas_call(
        body, name=name, grid=(h // ts,),
        in_specs=[pl.BlockSpec((4, ts, W), lambda i: (0, i, 0))], out_specs=pl.BlockSpec((ts, W), lambda i: (i, 0)),
        out_shape=jax.ShapeDtypeStruct((h, W), F32), compiler_params=_params(("parallel",), nb),
    )(b)


PACK_W = 1024
PACK_ROWS = 16
PACK_TILE = 256
EV_SHARDED = (("ev_w_in", "col", False), ("ev_w_uq", "col", False), ("ev_w_ukv", "col", False), ("ev_w_out", "row", False))
OD_SHARDED = (("od_w_in", "col", False), ("od_b_in", "col", True), ("od_dw_w", "col", True), ("od_dw_b", "col", True),
              ("od_ln_g", "col", True), ("od_ln_b", "col", True), ("od_w_out", "row", False))
REPLICATED =("ada_b", "pre_g", "post_g", "ev_dec_f", "ev_dec_b", "ev_q_norm_g", "ev_kv_norm_g")
WEIGHTS = ("ada_w", "ada_b", "pre_g", "post_g", "ev_w_in", "ev_dec_f", "ev_dec_b", "ev_q_norm_g", "ev_w_uq", "ev_kv_norm_g",
           "ev_w_ukv", "ev_w_out", "od_w_in", "od_b_in", "od_dw_w", "od_dw_b", "od_ln_g", "od_ln_b", "od_w_out")


def _layer_entries(layers):
    out = []
    for l in layers:
        out += [(name, how, exact, l // 2) for name, how, exact in (EV_SHARDED if l % 2 == 0 else OD_SHARDED)]
    return out


def _rows_of(n):
    unit = PACK_W * PACK_ROWS
    return (n + unit - 1) // unit * PACK_ROWS


def _to_rows(flat, lead):
    n = flat.shape[-1]
    r = _rows_of(n)
    flat = jnp.pad(flat, [(0, 0)] * len(lead) + [(0, r * PACK_W - n)])
    return flat.reshape(*lead, r, PACK_W)


def _pad_total(buf, axis):
    r = buf.shape[axis]
    r2 = (r + 2 * PACK_TILE - 1) // (2 * PACK_TILE) * (2 * PACK_TILE)
    pads = [(0, 0)] * buf.ndim
    pads[axis] = (0, r2 - r)
    return jnp.pad(buf, pads)


def _split_chips(full, how):
    if how == "col":
        n = full.shape[-1] // 4
        return jnp.moveaxis(full.reshape(*full.shape[:-1], 4, n), -2, 0)
    n = full.shape[-2] // 4
    return jnp.moveaxis(full.reshape(*full.shape[:-2], 4, n, full.shape[-1]), -3, 0)


def _join_chips(blocks, how):
    if how == "col":
        t = jnp.moveaxis(blocks, 0, -2)
        return t.reshape(*t.shape[:-2], t.shape[-2] * t.shape[-1])
    t = jnp.moveaxis(blocks, 0, -3)
    return t.reshape(*t.shape[:-3], t.shape[-3] * t.shape[-2], t.shape[-1])


def _pack_weights(w, entries):
    parts = []
    for name, _, exact, idx in entries:
        a = w[name][idx].reshape(-1)
        a = lax.bitcast_convert_type(a, BF16).reshape(-1) if exact else a.astype(BF16)
        parts.append(_to_rows(a, ()))
    return _pad_total(jnp.concatenate(parts, axis=0), 0)


def _unpack_weights(gathered, w, entries):
    out, r0 = {}, 0
    for name, how, exact, idx in entries:
        shp = w[name].shape[1:]
        n = (2 if exact else 1) * w[name][idx].size
        r = _rows_of(n)
        a = gathered[:, r0:r0 + r].reshape(4, -1)[:, :n]
        if exact:
            a = lax.bitcast_convert_type(a.reshape(4, -1, 2), F32)
        out[(name, idx)] = _join_chips(a.reshape(4, *shp), how)
        r0 += r
    return out


def _pack_grads(grads, entries, replicated):
    parts = []
    for name, how, _, idx in entries:
        b = _split_chips(grads[(name, idx)], how)
        parts.append(_to_rows(b.reshape(4, -1).astype(BF16), (4,)))
    for name in replicated:
        a = _to_rows(grads[name].reshape(-1).astype(BF16), ())
        parts.append(jnp.broadcast_to(a[None], (4, *a.shape)))
    return _pad_total(jnp.concatenate(parts, axis=1), 1)


def _reduce_begin(tag, g):
    h = g.shape[1] // 2
    c = lax.axis_index("c")
    theirs = _sibling_send_half("reduce_cores_" + tag, g)
    mine = lax.dynamic_slice_in_dim(g, c * h, h, axis=1)
    return _add2("reduce_add_" + tag, mine.reshape(4 * h, PACK_W), theirs.reshape(4 * h, PACK_W), BF16).reshape(4, h, PACK_W)


def _reduce_end(tag, exchanged):
    return _halves_merge("reduce_merge_" + tag, _sum4("reduce_sum_" + tag, exchanged))


def _unpack_grads(total, w, entries, replicated):
    out, r0 = {}, 0
    for name, _, _, idx in entries:
        n = w[name][idx].size
        r = _rows_of(n)
        out[(name, idx)] = total[r0:r0 + r].reshape(-1)[:n].reshape(w[name].shape[1:])
        r0 += r
    for name in replicated:
        n = w[name].size
        r = _rows_of(n)
        out[name] = total[r0:r0 + r].reshape(-1)[:n].reshape(w[name].shape)
        r0 += r
    return out


def _pad_heads(a, n, w, to):
    lead = a.shape[:-1]
    return jnp.pad(a.reshape(*lead, n, w), [(0, 0)] * (len(lead) + 1) + [(0, to - w)]).reshape(*lead, n * to)


def _unpad_heads(a, n, w, to):
    lead = a.shape[:-1]
    return a.reshape(*lead, n, to)[..., :w].reshape(*lead, n * w)


def _w_in_pad(w):
    return jnp.concatenate([_pad_heads(w[:, 0:256], 4, 64, HP), _pad_heads(w[:, 256:512], 4, 64, HP), w[:, 512:1536],
                            w[:, 2240:2752], w[:, 1536:2176], _pad_heads(w[:, 2176:2240], 1, 64, HP)], axis=1)


def _w_in_unpad(g):
    return jnp.concatenate([_unpad_heads(g[:, Z_RQ:Z_RK], 4, 64, HP), _unpad_heads(g[:, Z_RK:Z_RV], 4, 64, HP),
                            g[:, Z_RV:Z_MG], g[:, Z_CQ:Z_KR], g[:, Z_KR:Z_KR + 64], g[:, Z_MG:Z_CQ]], axis=1)


def _layer_weights(l, full, w):
    f32 = lambda a: a.astype(F32)
    i = l // 2
    common = dict(bias=f32(w["ada_b"][l])[None], pre_g=f32(w["pre_g"][l])[None], post_g=f32(w["post_g"][l])[None])
    if l % 2 == 0:
        w_uq_p = _pad_heads(full[("ev_w_uq", i)], MLA_HEADS, MLA_NOPE_DIM + MLA_ROPE_DIM, QW)
        lg = jnp.stack([jax.nn.log_sigmoid(f32(w["ev_dec_f"][i])), jax.nn.log_sigmoid(f32(w["ev_dec_b"][i]))])
        return dict(common, w_in=_w_in_pad(full[("ev_w_in", i)]), w_uq=w_uq_p, w_ukv=full[("ev_w_ukv", i)],
                    w_out=full[("ev_w_out", i)], qg=f32(w["ev_q_norm_g"][i])[None], kvg=f32(w["ev_kv_norm_g"][i])[None], lg=lg)
    dw = f32(full[("od_dw_w", i)])
    return dict(common, w_in=full[("od_w_in", i)], w_out=full[("od_w_out", i)], b_in=f32(full[("od_b_in", i)])[None],
                dw=jnp.pad(dw, ((0, 1), (0, 0))), dw_flip=jnp.pad(dw[::-1], ((0, 1), (0, 0))),
                dw_b=f32(full[("od_dw_b", i)])[None], ln_g=f32(full[("od_ln_g", i)])[None], ln_b=f32(full[("od_ln_b", i)])[None])


def _rope_tables(positions):
    inv_freq = ROPE_BASE ** (-jnp.arange(0, ROPE_DIM, 2, dtype=F32) / ROPE_DIM)
    ang = positions.astype(F32)[:, None] * inv_freq
    z = jnp.zeros((ang.shape[0], HP - ROPE_DIM), F32)
    return jnp.concatenate([jnp.cos(ang), jnp.cos(ang), z], axis=1), jnp.concatenate([jnp.sin(ang), jnp.sin(ang), z], axis=1)


def _even_fwd(x, mod, cos, sin, w, side=None):
    bias, pre_g, post_g = w["bias"], w["pre_g"], w["post_g"]
    h, z = _pre_mm("ev_in", x, pre_g, mod, bias, w["w_in"])
    rq, rk, rv, qn, kvn, kr = _evprep_fwd(z, cos, sin, w["qg"], w["kvg"])
    qf = _mm_nn("ev_uq", qn, w["w_uq"])
    kv = _mm_nn("ev_ukv", kvn, w["w_ukv"])
    Q, K, V = _attnprep_fwd(qf, kv, kr, cos, sin)
    a, lse, *got = _flash_fwd(Q, K, V, side)
    o2, states = _ret_fwd(w["lg"], rq, rk, rv)
    mix, y, x_new = _ev_out_fwd(o2, z, a, x, w["w_out"], post_g, mod, bias)
    saved = dict(x=x, h=h, z=z, rq=rq, rk=rk, rv=rv, qn=qn, kvn=kvn, Q=Q, K=K, V=V, a=a, lse=lse, o2=o2,
                 states=states, mix=mix, y=y)
    return x_new, saved, (got[0] if got else None)


def _even_bwd(dx, s, mod, cos, sin, w, side=None):
    bias, pre_g, post_g = w["bias"], w["pre_g"], w["post_g"]
    dy, do, drg, da, dmg, dpost_g, dgate = _ev_out_bwd(s["y"], dx, s["o2"], s["z"], s["a"], w["w_out"], post_g, mod, bias)
    dw_out = _mm_tn("ev_out_dw", s["mix"], dy)
    drq, drk, drv, dlg = _ret_bwd(w["lg"], s["rq"], s["rk"], s["rv"], do, s["states"])
    dQ, dK, dV, *got = _flash_bwd(s["Q"], s["K"], s["V"], da, s["a"], s["lse"], side)
    dqf, dkv, dkr = _attnprep_bwd(dQ, dK, dV, cos, sin)
    dqn = _mm_nt("ev_uq_dx", dqf, w["w_uq"])
    dw_uq = _mm_tn("ev_uq_dw", s["qn"], dqf)
    dkvn = _mm_nt("ev_ukv_dx", dkv, w["w_ukv"])
    dw_ukv = _mm_tn("ev_ukv_dw", s["kvn"], dkv)
    dz, dx_new, dqg, dkvg, dpre_g, dscale, dshift = _ev_in_bwd(
        s["z"], cos, sin, w["qg"], w["kvg"], drq, drk, drv, drg, dqn, dkvn, dkr, dmg, s["x"], dx, w["w_in"], pre_g, mod, bias)
    dw_in = _mm_tn("ev_in_dw", s["h"], dz)
    g = dict(ev_w_in=_w_in_unpad(dw_in), ev_w_uq=_unpad_heads(dw_uq, MLA_HEADS, MLA_NOPE_DIM + MLA_ROPE_DIM, QW),
             ev_w_ukv=dw_ukv, ev_w_out=dw_out, ev_q_norm_g=dqg[0], ev_kv_norm_g=dkvg[0],
             dlg_f=dlg[0, :RET_HEADS, 0], dlg_b=dlg[1, :RET_HEADS, 0])
    return dx_new, g, dpre_g[0], dpost_g[0], jnp.concatenate([dshift, dscale, dgate], axis=1), (got[0] if got else None)


def _odd_fwd(x, mod, w):
    bias, pre_g, post_g = w["bias"], w["pre_g"], w["post_g"]
    h, z = _pre_mm("od_in", x, pre_g, mod, bias, w["w_in"])
    u = _glu_fwd(z, w["b_in"])
    uc = _conv_fwd(u, w["dw"], w["dw_b"])
    vv, y, x_new = _od_out_fwd(uc, z, x, w["w_out"], w["b_in"], w["ln_g"], w["ln_b"], post_g, mod, bias)
    return x_new, dict(x=x, h=h, z=z, u=u, uc=uc, vv=vv, y=y)


def _odd_bwd(dx, s, mod, w):
    bias, pre_g, post_g = w["bias"], w["pre_g"], w["post_g"]
    dy, duc, dg, dpost_g, dgate, dbg, dln_g, dln_b = _od_out_bwd(
        s["y"], dx, s["uc"], s["z"], w["w_out"], w["b_in"], w["ln_g"], w["ln_b"], post_g, mod, bias)
    dw_out = _mm_tn("od_out_dw", s["vv"], dy)
    du = _conv_fwd(duc, w["dw_flip"], jnp.zeros_like(w["dw_b"]))
    dwb = _conv_bwd_w(s["u"], duc)
    dz, dx_new, dba, dbb, dpre_g, dscale, dshift = _od_in_bwd(s["z"], w["b_in"], du, dg, s["x"], dx, w["w_in"], pre_g, mod, bias)
    dw_in = _mm_tn("od_in_dw", s["h"], dz)
    g = dict(od_w_in=dw_in, od_b_in=jnp.concatenate([dba, dbb, dbg], axis=1)[0], od_dw_w=dwb[:CONV_KERNEL], od_dw_b=dwb[CONV_KERNEL],
             od_ln_g=dln_g[0], od_ln_b=dln_b[0], od_w_out=dw_out)
    return dx_new, g, dpre_g[0], dpost_g[0], jnp.concatenate([dshift, dscale, dgate], axis=1)


LATE_LAYERS = tuple(range(1, DEPTH))


def _step(x, mod, positions, loss_target, w):
    cos, sin = _rope_tables(positions)
    f32 = lambda a: a.astype(F32)
    mods = [mod[l:l + 1] for l in range(DEPTH)]
    first, late = _layer_entries([0]), _layer_entries(LATE_LAYERS)

    gathered = _halves_merge("gather_merge_first", _chip_exchange("gather_chips_first", _pack_weights(w, first), "half"))
    full = _unpack_weights(gathered, w, first)
    lw = {0: _layer_weights(0, full, w)}
    x, s0, got = _even_fwd(x, mods[0], cos, sin, lw[0], side=(_pack_weights(w, late), "half"))
    full.update(_unpack_weights(_halves_merge("gather_merge_late", got), w, late))
    saved = [s0]
    for l in LATE_LAYERS:
        lw[l] = _layer_weights(l, full, w)
        if l % 2 == 0:
            x, s, _ = _even_fwd(x, mods[l], cos, sin, lw[l])
        else:
            x, s = _odd_fwd(x, mods[l], lw[l])
        saved.append(s)
    dx, loss = _loss_head(x, loss_target)

    grads = {}
    dpre, dpost, dmod = [None] * DEPTH, [None] * DEPTH, [None] * DEPTH
    for l in reversed(LATE_LAYERS):
        if l % 2 == 0:
            dx, g, dpre[l], dpost[l], dmod[l], _ = _even_bwd(dx, saved[l], mods[l], cos, sin, lw[l])
        else:
            dx, g, dpre[l], dpost[l], dmod[l] = _odd_bwd(dx, saved[l], mods[l], lw[l])
        grads.update({(name, l // 2): val for name, val in g.items()})
    chip_sum = _reduce_begin("late", _pack_grads(grads, late, ()))
    dx, g, dpre[0], dpost[0], dmod[0], got = _even_bwd(dx, saved[0], mods[0], cos, sin, lw[0], side=(chip_sum, "scatter"))
    grads.update({(name, 0): val for name, val in g.items()})
    red = _unpack_grads(_reduce_end("late", got), w, late, ())

    n_ev = (DEPTH + 1) // 2
    grads.update(pre_g=jnp.stack(dpre), post_g=jnp.stack(dpost), ada_b=jnp.concatenate(dmod, axis=0))
    for name in ("ev_q_norm_g", "ev_kv_norm_g"):
        grads[name] = jnp.stack([grads[(name, i)] for i in range(n_ev)])
    grads["ev_dec_f"] = jnp.stack([grads[("dlg_f", i)] for i in range(n_ev)]) * jax.nn.sigmoid(-f32(w["ev_dec_f"]))
    grads["ev_dec_b"] = jnp.stack([grads[("dlg_b", i)] for i in range(n_ev)]) * jax.nn.sigmoid(-f32(w["ev_dec_b"]))
    chip_sum = _reduce_begin("first", _pack_grads(grads, first, REPLICATED))
    total = _reduce_end("first", _chip_exchange("reduce_chips_first", chip_sum, "scatter"))
    red.update(_unpack_grads(total, w, first, REPLICATED))

    out = {name: red[name] for name in REPLICATED}
    for name, _, _ in EV_SHARDED:
        out[name] = jnp.stack([red[(name, i)] for i in range(n_ev)])
    for name, _, _ in OD_SHARDED:
        out[name] = jnp.stack([red[(name, i)] for i in range(DEPTH // 2)])
    return loss[0, 0], dx, out, grads["ada_b"]


ADA_ROWS = 16
ADA_COLS = 3 * D_MODEL // 4
ADA_PACK = (8, 1024)


def _ada_pack(a):
    n = ADA_PACK[0] * ADA_PACK[1]
    return jnp.pad(a, ((0, 0), (0, n - a.shape[1]))).reshape(4, *ADA_PACK)


def _ada_unpack(a):
    return a.reshape(4, -1)[:, :DEPTH * ADA_COLS]


def _ada_forward(c, ada_w):
    D = D_MODEL
    c16 = jnp.zeros((ADA_ROWS, D), F32).at[0].set(c[0])
    c_act = _rowcall("silu_c", lambda rows, p: ([_silu(rows[0][...])], []), [c16], [], [(D, BF16)])[0]
    got = _chip_exchange("ada_c", c_act, "whole")
    c4 = jnp.pad(got[:, 0, :], ((0, ADA_ROWS - 4), (0, 0)))
    per_layer = [_mm_nn("ada", c4, ada_w[l].astype(BF16))[:4] for l in range(DEPTH)]
    p = jnp.stack(per_layer, axis=1).reshape(4, DEPTH * ADA_COLS)
    got = _chip_exchange("ada_mod", _ada_pack(p), "scatter")
    mod = _ada_unpack(got).reshape(4, DEPTH, ADA_COLS).transpose(1, 0, 2).reshape(DEPTH, 3 * D)
    return mod, c4


def _ada_backward(dmod, c4):
    p = dmod.reshape(DEPTH, 4, ADA_COLS).transpose(1, 0, 2).reshape(4, DEPTH * ADA_COLS)
    got = _chip_exchange("ada_dmod", _ada_pack(p), "scatter")
    dm4 = jnp.pad(_ada_unpack(got), ((0, ADA_ROWS - 4), (0, 0)))
    both = _halves_merge("ada_merge", jnp.concatenate([dm4, c4.astype(F32)], axis=1))
    dm32, c32 = both[:, :DEPTH * ADA_COLS].astype(BF16), both[:, DEPTH * ADA_COLS:].astype(BF16)
    return jnp.stack([_mm_tn("ada_dw", c32, dm32[:, l * ADA_COLS:(l + 1) * ADA_COLS]) for l in range(DEPTH)])


def kernel(x, c, positions, ada_w, ada_b, pre_g, post_g, ev_w_in, ev_dec_f, ev_dec_b, ev_q_norm_g, ev_w_uq, ev_kv_norm_g, ev_w_ukv, ev_w_out, od_w_in, od_b_in, od_dw_w, od_dw_b, od_ln_g, od_ln_b, od_w_out, loss_target, m_ada_w, m_ada_b, m_pre_g, m_post_g, m_ev_w_in, m_ev_dec_f, m_ev_dec_b, m_ev_q_norm_g, m_ev_w_uq, m_ev_kv_norm_g, m_ev_w_ukv, m_ev_w_out, m_od_w_in, m_od_b_in, m_od_dw_w, m_od_dw_b, m_od_ln_g, m_od_ln_b, m_od_w_out, v_ada_w, v_ada_b, v_pre_g, v_post_g, v_ev_w_in, v_ev_dec_f, v_ev_dec_b, v_ev_q_norm_g, v_ev_w_uq, v_ev_kv_norm_g, v_ev_w_ukv, v_ev_w_out, v_od_w_in, v_od_b_in, v_od_dw_w, v_od_dw_b, v_od_ln_g, v_od_ln_b, v_od_w_out):
    w = dict(ada_w=ada_w, ada_b=ada_b, pre_g=pre_g, post_g=post_g, ev_w_in=ev_w_in, ev_dec_f=ev_dec_f, ev_dec_b=ev_dec_b,
             ev_q_norm_g=ev_q_norm_g, ev_w_uq=ev_w_uq, ev_kv_norm_g=ev_kv_norm_g, ev_w_ukv=ev_w_ukv, ev_w_out=ev_w_out,
             od_w_in=od_w_in, od_b_in=od_b_in, od_dw_w=od_dw_w, od_dw_b=od_dw_b, od_ln_g=od_ln_g, od_ln_b=od_ln_b, od_w_out=od_w_out)
    m = dict(ada_w=m_ada_w, ada_b=m_ada_b, pre_g=m_pre_g, post_g=m_post_g, ev_w_in=m_ev_w_in, ev_dec_f=m_ev_dec_f, ev_dec_b=m_ev_dec_b,
             ev_q_norm_g=m_ev_q_norm_g, ev_w_uq=m_ev_w_uq, ev_kv_norm_g=m_ev_kv_norm_g, ev_w_ukv=m_ev_w_ukv, ev_w_out=m_ev_w_out,
             od_w_in=m_od_w_in, od_b_in=m_od_b_in, od_dw_w=m_od_dw_w, od_dw_b=m_od_dw_b, od_ln_g=m_od_ln_g, od_ln_b=m_od_ln_b, od_w_out=m_od_w_out)
    v = dict(ada_w=v_ada_w, ada_b=v_ada_b, pre_g=v_pre_g, post_g=v_post_g, ev_w_in=v_ev_w_in, ev_dec_f=v_ev_dec_f, ev_dec_b=v_ev_dec_b,
             ev_q_norm_g=v_ev_q_norm_g, ev_w_uq=v_ev_w_uq, ev_kv_norm_g=v_ev_kv_norm_g, ev_w_ukv=v_ev_w_ukv, ev_w_out=v_ev_w_out,
             od_w_in=v_od_w_in, od_b_in=v_od_b_in, od_dw_w=v_od_dw_w, od_dw_b=v_od_dw_b, od_ln_g=v_od_ln_g, od_ln_b=v_od_ln_b, od_w_out=v_od_w_out)

    mod, c4 = _ada_forward(c, ada_w)
    loss_local, grad_x, g, dmod = _step(x[0], mod, positions[0], loss_target[0], w)
    loss = lax.psum(loss_local, ("x", "y", "c"))
    g["ada_w"] = _ada_backward(dmod, c4)
    delta, new_m, new_v = {}, {}, {}
    for name in WEIGHTS:
        delta[name], new_m[name], new_v[name] = _adamw(w[name], g[name], m[name], v[name])
    return (loss, grad_x[None], *[g[n] for n in WEIGHTS], *[delta[n] for n in WEIGHTS],
            *[new_m[n] for n in WEIGHTS], *[new_v[n] for n in WEIGHTS])
```

```python
import functools

import jax
import jax.numpy as jnp
from jax import lax
from jax.experimental import pallas as pl
from jax.experimental.pallas import tpu as pltpu

F32 = jnp.float32
BF16 = jnp.bfloat16

D_MODEL = 1024
DEPTH = 4
RET_HEADS = 4
RET_QK_DIM = 64
RET_V_DIM = 128
MLA_HEADS = 4
MLA_Q_RANK = 384
MLA_KV_RANK = 256
MLA_NOPE_DIM = 128
MLA_ROPE_DIM = 64
MLA_V_DIM = 128
ROPE_DIM = 64
ROPE_BASE = 10000.0
CONV_KERNEL = 31
EPS = 1e-6
ADAM_LR, ADAM_B1, ADAM_B2, ADAM_EPS, ADAM_WD, ADAM_STEP = 0.001, 0.9, 0.999, 1e-08, 0.01, 10

LANES = 128
HP = 128
Z_RQ, Z_RK, Z_RV, Z_RG, Z_MG, Z_CQ, Z_CKV, Z_KR, Z_END = 0, 512, 1024, 1536, 2048, 2560, 2944, 3200, 3328
Z_LAT = Z_END - Z_CQ


def _z_ret(z):
    return (z, Z_RG, 0)


def _z_gates(z):
    return [(z, 512, Z_RG // 512), (z, 512, Z_MG // 512)]


def _z_latents(z):
    return [(z, 256, Z_CQ // 256 + i) for i in range(Z_LAT // 256)]
V7X_VMEM_BYTES = 64 * 1024 * 1024
VMEM_CAP = V7X_VMEM_BYTES - 8 * 1024 * 1024

MESH = pl.DeviceIdType.MESH


def _nbytes(shape, dtype):
    n = 1
    for s in shape:
        n *= s
    return n * jnp.dtype(dtype).itemsize


def _hbm(x):
    return pltpu.with_memory_space_constraint(x, pltpu.HBM)


def _params(sem, block_bytes):
    limit = min(VMEM_CAP, max(32 * 1024 * 1024, 2 * block_bytes + 16 * 1024 * 1024))
    return pltpu.CompilerParams(dimension_semantics=sem, vmem_limit_bytes=limit)


ROW_TILE = 512


def _rowcall(name, fn, rows, params, outs, accs=(), ts=ROW_TILE, mats=()):
    rows = [r if isinstance(r, tuple) else (r, r.shape[1], 0) for r in rows]
    S = rows[0][0].shape[0]
    ts = min(ts, S)
    assert S % ts == 0, (name, S, ts)
    nr, npar, nm, no, na = len(rows), len(params), len(mats), len(outs), len(accs)

    def body(*refs):
        row_refs = refs[:nr]
        pvals = [r[...] for r in refs[nr:nr + npar]]
        mat_refs = refs[nr + npar:nr + npar + nm]
        out_refs = refs[nr + npar + nm:nr + npar + nm + no]
        acc_refs = refs[nr + npar + nm + no:]
        ovals, avals = fn(row_refs, pvals, mat_refs) if nm else fn(row_refs, pvals)
        for r, v in zip(out_refs, ovals, strict=True):
            r[...] = v.astype(r.dtype)
        if na:
            @pl.when(pl.program_id(0) == 0)
            def _():
                for r in acc_refs:
                    r[...] = jnp.zeros_like(r)
            for r, v in zip(acc_refs, avals, strict=True):
                r[...] += v

    in_specs = [pl.BlockSpec((ts, w), functools.partial(lambda i, b: (i, b), b=blk)) for _, w, blk in rows]
    in_specs += [pl.BlockSpec(p.shape, lambda i: (0, 0)) for p in [*params, *mats]]
    out_specs = [pl.BlockSpec((ts, w), lambda i: (i, 0)) for w, _ in outs]
    out_specs += [pl.BlockSpec(s, lambda i: (0, 0)) for s in accs]
    out_shape = [jax.ShapeDtypeStruct((S, w), dt) for w, dt in outs]
    out_shape += [jax.ShapeDtypeStruct(s, F32) for s in accs]
    nb = sum(_nbytes((ts, w), a.dtype) for a, w, _ in rows) + sum(_nbytes((ts, w), dt) for w, dt in outs)
    nb += sum(_nbytes(p.shape, p.dtype) for p in params) + sum(_nbytes(s, F32) for s in accs)
    res = pl.pallas_call(
        body, name=name, grid=(S // ts,), in_specs=in_specs, out_specs=out_specs, out_shape=out_shape,
        compiler_params=_params(("arbitrary",) if na else ("parallel",), 3 * nb + sum(_nbytes(m.shape, m.dtype) for m in mats)),
    )(*[_hbm(a) for a, _, _ in rows], *params, *[_hbm(m) for m in mats])
    return res


def _silu(x):
    return x * jax.nn.sigmoid(x)


def _rms(x, g):
    return x * lax.rsqrt(jnp.mean(x * x, axis=-1, keepdims=True) + EPS) * g


def _rot(x):
    lane = lax.broadcasted_iota(jnp.int32, x.shape, 1)
    return jnp.where(lane < ROPE_DIM // 2, -pltpu.roll(x, LANES - ROPE_DIM // 2, 1), pltpu.roll(x, ROPE_DIM // 2, 1))


def _rope(x, cos, sin):
    return x * cos + _rot(x) * sin


def _rope_t(dy, cos, sin):
    return dy * cos - _rot(dy) * sin


def _groups(ref, start, n):
    return [ref[:, start + HP * h:start + HP * (h + 1)] for h in range(n)]


def _pre_math(x, g, m_scale, b_scale, m_shift, b_shift):
    return _rms(x, g) * (1.0 + (m_scale + b_scale)) + (m_shift + b_shift)


def _post_math(x, y, g, m_gate, b_gate):
    return x + (m_gate + b_gate) * _rms(y, g)


def _ev_mid_fwd(z, cos, sin, qg, kvg, w_uq, w_ukv):
    qscale = ATT_SCALE * LOG2E

    def fn(rows, p, mats):
        z_, l0, l1, l2, c_, s_ = rows
        qg_, kvg_ = p
        cos_, sin_ = c_[...], s_[...]
        lat = jnp.concatenate([l0[...], l1[...], l2[...]], axis=1)
        rq = jnp.concatenate([_rope(v[...], cos_, sin_) for v in _groups(z_, Z_RQ, RET_HEADS)], axis=1)
        rk = jnp.concatenate([_rope(v[...], cos_, sin_) for v in _groups(z_, Z_RK, RET_HEADS)], axis=1)
        rk = rk * (RET_QK_DIM ** -0.5)
        rv = z_[:, Z_RV:Z_RG]
        qn = _rms(lat[:, :Z_CKV - Z_CQ], qg_).astype(BF16)
        kvn = _rms(lat[:, Z_CKV - Z_CQ:Z_KR - Z_CQ], kvg_).astype(BF16)
        kr = _rope(lat[:, Z_KR - Z_CQ:], cos_, sin_)
        qf = jnp.dot(qn, mats[0][...], preferred_element_type=F32)
        kv = jnp.dot(kvn, mats[1][...], preferred_element_type=F32)
        qs, ks, vs = [], [], []
        for h in range(MLA_HEADS):
            b = 2 * HP * h
            qs += [qf[:, b:b + HP] * qscale, _rope(qf[:, b + HP:b + 2 * HP], cos_, sin_) * qscale]
            ks += [kv[:, b:b + HP], kr]
            vs += [kv[:, b + HP:b + 2 * HP]]
        return [rq, rk, rv, qn, kvn, jnp.concatenate(qs, axis=1), jnp.concatenate(ks, axis=1), jnp.concatenate(vs, axis=1)], []

    return _rowcall("ev_mid_fwd", fn, [_z_ret(z), *_z_latents(z), cos, sin], [qg, kvg],
                    [(512, BF16), (512, BF16), (512, BF16), (MLA_Q_RANK, BF16), (MLA_KV_RANK, BF16),
                     (1024, BF16), (1024, BF16), (512, BF16)], mats=[w_uq, w_ukv])


def _pre_bwd_math(dh, x, res, g, mod, bias):
    D = D_MODEL
    _, vjp = jax.vjp(_pre_math, x, g, mod[:, D:2 * D], bias[:, D:2 * D], mod[:, :D], bias[:, :D])
    dx, dg, dsc, _, dsh, _ = vjp(dh)
    return dx + res, dg, dsc, dsh


def _ev_in_bwd(z, cos, sin, qg, kvg, drq, drk, drv, drg, dqn, dkvn, dkr, dmg, x, dx_res, w_in, g, mod, bias):
    D = D_MODEL

    def fn(rows, p, mats):
        l0, l1, l2, c_, s_, drq_, drk_, drv_, drg_, dqn_, dkvn_, dkr_, dmg_, x_, res_ = rows
        qg_, kvg_, g_, mod_, b_ = p
        cos_, sin_ = c_[...], s_[...]
        lat = jnp.concatenate([l0[...], l1[...], l2[...]], axis=1)
        parts = []
        for h in range(RET_HEADS):
            parts.append(_rope_t(drq_[:, HP * h:HP * (h + 1)] + drq_[:, 512 + HP * h:512 + HP * (h + 1)], cos_, sin_))
        for h in range(RET_HEADS):
            t = drk_[:, HP * h:HP * (h + 1)] + drk_[:, 512 + HP * h:512 + HP * (h + 1)]
            parts.append(_rope_t(t, cos_, sin_) * (RET_QK_DIM ** -0.5))
        parts.append(drv_[:, :512] + drv_[:, 512:])
        parts += [drg_[...], dmg_[...]]
        _, vq = jax.vjp(_rms, lat[:, :Z_CKV - Z_CQ], qg_)
        dcq, dqg = vq(dqn_[...])
        _, vkv = jax.vjp(_rms, lat[:, Z_CKV - Z_CQ:Z_KR - Z_CQ], kvg_)
        dckv, dkvg = vkv(dkvn_[...])
        parts += [dcq, dckv, _rope_t(dkr_[...], cos_, sin_)]
        dz = jnp.concatenate([v.astype(BF16) for v in parts], axis=1)
        dh = lax.dot_general(dz, mats[0][...], NT_DIMS, preferred_element_type=F32)
        dx, dg, dsc, dsh = _pre_bwd_math(dh, x_[...], res_[...], g_, mod_, b_)
        return [dz, dx], [dqg, dkvg, dg, dsc, dsh]

    rows = [*_z_latents(z), cos, sin, drq, drk, drv, drg, dqn, dkvn, dkr, dmg, x, dx_res]
    return _rowcall("ev_in_bwd", fn, rows, [qg, kvg, g, mod, bias], [(Z_END, BF16), (D, F32)],
                    [(1, MLA_Q_RANK), (1, MLA_KV_RANK)] + [(1, D)] * 3, ts=256, mats=[w_in])


ATT_SCALE = (MLA_NOPE_DIM + MLA_ROPE_DIM) ** -0.5
LOG2E = 1.4426950408889634
LN2 = 0.6931471805599453


def _ev_mid_bwd(dQ, dK, dV, cos, sin, w_uq, w_ukv):
    def fn(rows, p, mats):
        dq_, dk_, dv_, c_, s_ = rows
        cos_, sin_ = c_[...], s_[...]
        dqs, dkvs = [], []
        dkr = None
        for h in range(MLA_HEADS):
            b = 2 * HP * h
            dqs += [dq_[:, b:b + HP] * ATT_SCALE, _rope_t(dq_[:, b + HP:b + 2 * HP], cos_, sin_) * ATT_SCALE]
            dkvs += [dk_[:, b:b + HP] * LN2, dv_[:, HP * h:HP * (h + 1)]]
            t = dk_[:, b + HP:b + 2 * HP]
            dkr = t if dkr is None else dkr + t
        dqf = jnp.concatenate(dqs, axis=1).astype(BF16)
        dkv = jnp.concatenate(dkvs, axis=1).astype(BF16)
        dqn = lax.dot_general(dqf, mats[0][...], NT_DIMS, preferred_element_type=F32)
        dkvn = lax.dot_general(dkv, mats[1][...], NT_DIMS, preferred_element_type=F32)
        return [dqf, dkv, dqn, dkvn, dkr * LN2], []

    return _rowcall("ev_mid_bwd", fn, [dQ, dK, dV, cos, sin], [],
                    [(1024, BF16), (1024, BF16), (MLA_Q_RANK, F32), (MLA_KV_RANK, F32), (HP, F32)], mats=[w_uq, w_ukv])


def _mix_math(o, rg, a, mg):
    outs = []
    for h in range(RET_HEADS):
        oh = o[:, HP * h:HP * (h + 1)]
        mu = jnp.mean(oh, axis=-1, keepdims=True)
        var = jnp.mean(jnp.square(oh - mu), axis=-1, keepdims=True)
        outs.append((oh - mu) * lax.rsqrt(var + EPS))
    ret = jnp.concatenate(outs, axis=1) * _silu(rg)
    return jnp.concatenate([ret, a * _silu(mg)], axis=1)


def _ev_out_fwd(o2, z, a, x, w_out, g, mod, bias):
    D = D_MODEL

    def fn(rows, p, mats):
        o_, rg_, mg_, a_, x_ = rows
        g_, mod_, b_ = p
        mix = _mix_math(o_[:, :512] + o_[:, 512:], rg_[...], a_[...], mg_[...]).astype(BF16)
        y = jnp.dot(mix, mats[0][...], preferred_element_type=F32)
        return [mix, y, _post_math(x_[...], y, g_, mod_[:, 2 * D:], b_[:, 2 * D:])], []

    return _rowcall("ev_out", fn, [o2, *_z_gates(z), a, x], [g, mod, bias], [(1024, BF16), (D, F32), (D, F32)], mats=[w_out])


def _post_bwd_math(y, dxo, g, mod, bias):
    D = D_MODEL
    b_gate = bias[:, 2 * D:]
    _, vjp = jax.vjp(lambda y_, g2, mg: _post_math(0.0, y_, g2, mg, b_gate), y, g, mod[:, 2 * D:])
    return vjp(dxo)


def _ev_out_bwd(y, dxo, o2, z, a, w_out, g, mod, bias):
    D = D_MODEL

    def fn(rows, p, mats):
        y_, dxo_, o_, rg_, mg_, a_ = rows
        g_, mod_, b_ = p
        dy, dg, dgate = _post_bwd_math(y_[...], dxo_[...], g_, mod_, b_)
        dyb = dy.astype(BF16)
        dmix = lax.dot_general(dyb, mats[0][...], NT_DIMS, preferred_element_type=F32)
        _, vjp = jax.vjp(_mix_math, o_[:, :512] + o_[:, 512:], rg_[...], a_[...], mg_[...])
        return [dyb, *vjp(dmix)], [dg, dgate]

    return _rowcall("ev_out_bwd", fn, [y, dxo, o2, *_z_gates(z), a], [g, mod, bias],
                    [(D, BF16), (512, BF16), (512, F32), (512, BF16), (512, F32)], [(1, D)] * 2, mats=[w_out])


def _glu_math(a, b, ba, bb):
    return (a + ba) * jax.nn.sigmoid(b + bb)


def _od_in_fwd(x, g, mod, bias, w_in, bin_):
    D = D_MODEL

    def fn(rows, p, mats):
        g_, mod_, b_, bi_ = p
        h = _pre_math(rows[0][...], g_, mod_[:, D:2 * D], b_[:, D:2 * D], mod_[:, :D], b_[:, :D]).astype(BF16)
        z = jnp.dot(h, mats[0][...], preferred_element_type=F32)
        return [h, z, _glu_math(z[:, :D], z[:, D:2 * D], bi_[:, :D], bi_[:, D:2 * D])], []

    return _rowcall("od_in", fn, [x], [g, mod, bias, bin_], [(D, BF16), (3 * D, F32), (D, F32)], mats=[w_in])


def _odmix_math(uc, g, bg, ln_g, ln_b):
    mu = jnp.mean(uc, axis=-1, keepdims=True)
    var = jnp.mean(jnp.square(uc - mu), axis=-1, keepdims=True)
    y = (uc - mu) * lax.rsqrt(var + EPS) * ln_g + ln_b
    return _silu(y) * _silu(g + bg)


def _od_out_fwd(uc, z, x, w_out, bin_, ln_g, ln_b, g, mod, bias, tgt=None):
    D = D_MODEL

    def fn(rows, p, mats):
        uc_, zg_, x_ = rows[:3]
        bi_, lg_, lb_, g_, mod_, b_ = p
        vv = _odmix_math(uc_[...], zg_[...], bi_[:, 2 * D:], lg_, lb_).astype(BF16)
        y = jnp.dot(vv, mats[0][...], preferred_element_type=F32)
        xn = _post_math(x_[...], y, g_, mod_[:, 2 * D:], b_[:, 2 * D:])
        if tgt is None:
            return [vv, y, xn], []
        err = xn - rows[3][...]
        part = 0.5 * jnp.sum(jnp.mean(err * err, axis=-1, keepdims=True), axis=0, keepdims=True)
        return [vv, y, err * (1.0 / D)], [jnp.broadcast_to(part, (1, LANES))]

    rows = [uc, (z, D, 2), x] + ([] if tgt is None else [tgt])
    return _rowcall("od_out" if tgt is None else "od_out_loss", fn, rows, [bin_, ln_g, ln_b, g, mod, bias],
                    [(D, BF16), (D, F32), (D, F32)], [] if tgt is None else [(1, LANES)], mats=[w_out])


def _od_out_bwd(y, dxo, uc, z, w_out, bin_, ln_g, ln_b, g, mod, bias):
    D = D_MODEL

    def fn(rows, p, mats):
        y_, dxo_, uc_, zg_ = rows
        bi_, lg_, lb_, g_, mod_, b_ = p
        dy, dpg, dgate = _post_bwd_math(y_[...], dxo_[...], g_, mod_, b_)
        dyb = dy.astype(BF16)
        dvv = lax.dot_general(dyb, mats[0][...], NT_DIMS, preferred_element_type=F32)
        _, vjp = jax.vjp(_odmix_math, uc_[...], zg_[...], bi_[:, 2 * D:], lg_, lb_)
        duc, dg, dbg, dlg, dlb = vjp(dvv)
        return [dyb, duc, dg], [dpg, dgate, dbg, dlg, dlb]

    return _rowcall("od_out_bwd", fn, [y, dxo, uc, (z, D, 2)], [bin_, ln_g, ln_b, g, mod, bias],
                    [(D, BF16), (D, F32), (D, BF16)], [(1, D)] * 5, mats=[w_out])


def _od_in_bwd(z, bin_, du, dg, x, dx_res, w_in, g, mod, bias):
    D = D_MODEL

    def fn(rows, p, mats):
        za_, zb_, du_, dg_, x_, res_ = rows
        bi_, g_, mod_, b_ = p
        _, vjp = jax.vjp(_glu_math, za_[...], zb_[...], bi_[:, :D], bi_[:, D:2 * D])
        da, db, dba, dbb = vjp(du_[...])
        dz = jnp.concatenate([da.astype(BF16), db.astype(BF16), dg_[...]], axis=1)
        dh = lax.dot_general(dz, mats[0][...], NT_DIMS, preferred_element_type=F32)
        dx, dpg, dsc, dsh = _pre_bwd_math(dh, x_[...], res_[...], g_, mod_, b_)
        return [dz, dx], [dba, dbb, dpg, dsc, dsh]

    return _rowcall("od_in_bwd", fn, [(z, D, 0), (z, D, 1), du, dg, x, dx_res], [bin_, g, mod, bias],
                    [(3 * D, BF16), (D, F32)], [(1, D)] * 5, ts=256, mats=[w_in])


def _tile(n, cap):
    if n <= cap:
        return n
    best = None
    for t in range(LANES, cap + 1, LANES):
        if n % t == 0:
            best = t
    assert best is not None, (n, cap)
    return best


MM_TN_CAP = 1792
NT_DIMS = (((1,), (1,)), ((), ()))
TN_DIMS = (((0,), (0,)), ((), ()))


def _mm_nn(name, a, b, out_dtype=F32, tm=512, tn_cap=MM_TN_CAP):
    M, K = a.shape
    N = b.shape[1]
    tm = min(tm, M)
    tn = _tile(N, tn_cap)

    def body(a_ref, b_ref, o_ref):
        o_ref[...] = jnp.dot(a_ref[...], b_ref[...], preferred_element_type=F32).astype(o_ref.dtype)

    nb = _nbytes((tm, K), a.dtype) + _nbytes((K, tn), b.dtype) + _nbytes((tm, tn), out_dtype) + _nbytes((tm, tn), F32)
    return pl.pallas_call(
        body, name=name, grid=(N // tn, M // tm),
        in_specs=[pl.BlockSpec((tm, K), lambda j, i: (i, 0)), pl.BlockSpec((K, tn), lambda j, i: (0, j))],
        out_specs=pl.BlockSpec((tm, tn), lambda j, i: (i, j)),
        out_shape=jax.ShapeDtypeStruct((M, N), out_dtype),
        compiler_params=_params(("parallel", "parallel"), nb),
    )(_hbm(a), _hbm(b))


def _pre_mm(name, x, g, mod, bias, w, tm=512):
    S, D = x.shape
    N = w.shape[1]
    tm = min(tm, S)

    def body(x_ref, g_ref, mod_ref, bias_ref, w_ref, h_ref, z_ref):
        mod_, b_ = mod_ref[...], bias_ref[...]
        h = _pre_math(x_ref[...], g_ref[...], mod_[:, D:2 * D], b_[:, D:2 * D], mod_[:, :D], b_[:, :D]).astype(BF16)
        h_ref[...] = h
        z_ref[...] = jnp.dot(h, w_ref[...], preferred_element_type=F32)

    nb = 2 * _nbytes((tm, D), F32) + _nbytes((D, N), w.dtype) + 2 * _nbytes((tm, N), F32)
    row = lambda i: (i, 0)
    whole = lambda i: (0, 0)
    return pl.pallas_call(
        body, name=name, grid=(S // tm,),
        in_specs=[pl.BlockSpec((tm, D), row), pl.BlockSpec(g.shape, whole), pl.BlockSpec(mod.shape, whole),
                  pl.BlockSpec(bias.shape, whole), pl.BlockSpec((D, N), whole)],
        out_specs=[pl.BlockSpec((tm, D), row), pl.BlockSpec((tm, N), row)],
        out_shape=[jax.ShapeDtypeStruct((S, D), BF16), jax.ShapeDtypeStruct((S, N), F32)],
        compiler_params=_params(("parallel",), nb),
    )(_hbm(x), g, mod, bias, _hbm(w))


def _mm_tn(name, a, b, ts=1024, tm_cap=512, tn_cap=MM_TN_CAP):
    S, M = a.shape
    N = b.shape[1]
    ts = min(ts, S)
    tm = _tile(M, tm_cap)
    tn = _tile(N, tn_cap)

    def body(a_ref, b_ref, o_ref):
        @pl.when(pl.program_id(2) == 0)
        def _():
            o_ref[...] = jnp.zeros_like(o_ref)

        o_ref[...] += lax.dot_general(a_ref[...], b_ref[...], (((0,), (0,)), ((), ())), preferred_element_type=F32)

    nb = _nbytes((ts, tm), a.dtype) + _nbytes((ts, tn), b.dtype) + 2 * _nbytes((tm, tn), F32)
    return pl.pallas_call(
        body, name=name, grid=(M // tm, N // tn, S // ts),
        in_specs=[pl.BlockSpec((ts, tm), lambda i, j, s: (s, i)), pl.BlockSpec((ts, tn), lambda i, j, s: (s, j))],
        out_specs=pl.BlockSpec((tm, tn), lambda i, j, s: (i, j)),
        out_shape=jax.ShapeDtypeStruct((M, N), F32),
        compiler_params=_params(("parallel", "parallel", "arbitrary"), nb),
    )(_hbm(a), _hbm(b))


ATT_TQ = 1024
ATT_TK = 1024
ATT_TK_FWD = 1024
QW = 2 * HP


def _kv_tiles(V, tk):
    S = V.shape[0]
    return V.reshape(S // tk, tk, MLA_HEADS, HP).transpose(2, 0, 3, 1)


class _Side:
    def __init__(self, side):
        self.on = side is not None
        if self.on:
            self.p, self.mode = side
            shp = self.p.shape
            self.h, self.w = (shp[0] // 2, shp[1]) if self.mode == "half" else (shp[-2], shp[-1])
        self.in_specs = [_HBM] if self.on else []
        self.out_specs = [_HBM] if self.on else []
        self.out_shape = [jax.ShapeDtypeStruct((4, self.h, self.w), self.p.dtype)] if self.on else []
        self.scratch = list(_CHIP_SEMS) if self.on else []
        self.operands = [_hbm(self.p)] if self.on else []

    def copies(self, p_ref, x_ref, sems):
        return _chip_copies(p_ref, x_ref, *sems, self.mode, self.h)

    def start(self, p_ref, x_ref, sems, first):
        @pl.when(first)
        def _():
            for cp in self.copies(p_ref, x_ref, sems):
                cp.start()

    def finish(self, p_ref, x_ref, sems, last):
        @pl.when(last)
        def _():
            for cp in self.copies(p_ref, x_ref, sems):
                cp.wait()


def _flash_fwd(Q, K, V, side=None):
    S = Q.shape[0]
    H = MLA_HEADS
    tq, tk = min(ATT_TQ, S), min(ATT_TK_FWD, S)
    nk = S // tk
    VT = _kv_tiles(V, tk)
    sd = _Side(side)

    def body(*refs):
        if sd.on:
            q_ref, k_ref, vt_ref, p_ref, o_ref, lse_ref, x_ref, m_s, l_s, acc_s, s_a, s_b, *sems = refs
            step_id = pl.program_id(0) * (S // tq) + pl.program_id(1)
            sd.start(p_ref, x_ref, sems, step_id == 0)
        else:
            q_ref, k_ref, vt_ref, o_ref, lse_ref, m_s, l_s, acc_s, s_a, s_b = refs
        m_s[...] = jnp.full(m_s.shape, -jnp.inf, F32)
        l_s[...] = jnp.zeros(l_s.shape, F32)
        acc_s[...] = jnp.zeros(acc_s.shape, F32)

        def scores(j):
            k = k_ref[pl.ds(pl.multiple_of(j * tk, tk), tk), :]
            return lax.dot_general(k, q_ref[...], NT_DIMS, preferred_element_type=F32)

        def consume(st, j):
            m_prev = m_s[...]
            m_new = jnp.maximum(m_prev, jnp.max(st, axis=0, keepdims=True))
            alpha = jnp.exp2(m_prev - m_new)
            pt = jnp.exp2(st - m_new)
            l_s[...] = alpha * l_s[...] + jnp.sum(pt, axis=0, keepdims=True)
            acc_s[...] = alpha * acc_s[...] + jnp.dot(vt_ref[0, j], pt.astype(BF16), preferred_element_type=F32)
            m_s[...] = m_new

        if nk % 2:
            def step(j, carry):
                consume(scores(j), j)
                return carry

            lax.fori_loop(0, nk, step, 0)
        else:
            s_a[...] = scores(0)

            def pair(jj, carry):
                j0 = 2 * jj
                s_b[...] = scores(j0 + 1)
                consume(s_a[...], j0)
                s_a[...] = scores(jnp.minimum(j0 + 2, nk - 1))
                consume(s_b[...], j0 + 1)
                return carry

            lax.fori_loop(0, nk // 2, pair, 0)
        l = l_s[...]
        o_ref[...] = (acc_s[...] * (1.0 / l)).T
        lse_ref[0] = m_s[...] + jnp.log2(l)
        if sd.on:
            sd.finish(p_ref, x_ref, sems, step_id == H * (S // tq) - 1)

    nb = (_nbytes((tq, QW), BF16) + _nbytes((S, QW), BF16) + _nbytes((S, HP), BF16) + 3 * _nbytes((tq, HP), F32)
          + 4 * _nbytes((tq, tk), F32))
    return pl.pallas_call(
        body, name="flash_fwd_side" if sd.on else "flash_fwd", grid=(H, S // tq),
        in_specs=[pl.BlockSpec((tq, QW), lambda h, i: (i, h)), pl.BlockSpec((S, QW), lambda h, i: (0, h)),
                  pl.BlockSpec((1, nk, HP, tk), lambda h, i: (h, 0, 0, 0))] + sd.in_specs,
        out_specs=[pl.BlockSpec((tq, HP), lambda h, i: (i, h)), pl.BlockSpec((1, 1, tq), lambda h, i: (h, 0, i))] + sd.out_specs,
        out_shape=[jax.ShapeDtypeStruct((S, H * HP), F32), jax.ShapeDtypeStruct((H, 1, S), F32)] + sd.out_shape,
        scratch_shapes=[pltpu.VMEM((1, tq), F32), pltpu.VMEM((1, tq), F32), pltpu.VMEM((HP, tq), F32),
                        pltpu.VMEM((tk, tq), F32), pltpu.VMEM((tk, tq), F32)] + sd.scratch,
        compiler_params=_params(("arbitrary", "arbitrary") if sd.on else ("parallel", "parallel"), nb),
    )(_hbm(Q), _hbm(K), _hbm(VT), *sd.operands)


def _flash_bwd(Q, K, V, dO, O, lse, side=None):
    S = Q.shape[0]
    H = MLA_HEADS
    tq, tk = min(ATT_TQ, S), min(ATT_TK, S)
    nk = S // tk
    sd = _Side(side)

    def body(*refs):
        if sd.on:
            q_ref, do_ref, o_ref, lse_ref, k_ref, v_ref, p_ref, dq_ref, dk_ref, dv_ref, x_ref, dq_s, *sems = refs
            step_id = pl.program_id(0) * (S // tq) + pl.program_id(1)
            sd.start(p_ref, x_ref, sems, step_id == 0)
        else:
            q_ref, do_ref, o_ref, lse_ref, k_ref, v_ref, dq_ref, dk_ref, dv_ref, dq_s = refs

        @pl.when(pl.program_id(1) == 0)
        def _():
            dk_ref[...] = jnp.zeros(dk_ref.shape, F32)
            dv_ref[...] = jnp.zeros(dv_ref.shape, F32)

        delta = jnp.sum((do_ref[...].astype(F32) * o_ref[...]).T, axis=0, keepdims=True)
        lse = lse_ref[0]
        dq_s[...] = jnp.zeros(dq_s.shape, F32)

        def rows(j):
            return pl.ds(pl.multiple_of(j * tk, tk), tk)

        def scores(j):
            st = lax.dot_general(k_ref[rows(j), :], q_ref[...], NT_DIMS, preferred_element_type=F32)
            dpt = lax.dot_general(v_ref[rows(j), :], do_ref[...], NT_DIMS, preferred_element_type=F32)
            return st, dpt

        def consume(st, dpt, j):
            pt = jnp.exp2(st - lse)
            dst = (pt * (dpt - delta)).astype(BF16)
            dv_ref[rows(j), :] += jnp.dot(pt.astype(BF16), do_ref[...], preferred_element_type=F32)
            dk_ref[rows(j), :] += jnp.dot(dst, q_ref[...], preferred_element_type=F32)
            dq_s[...] += lax.dot_general(dst, k_ref[rows(j), :], TN_DIMS, preferred_element_type=F32)

        def step(j, carry):
            consume(*scores(j), j)
            return carry

        lax.fori_loop(0, nk, step, 0)
        dq_ref[...] = dq_s[...]
        if sd.on:
            sd.finish(p_ref, x_ref, sems, step_id == H * (S // tq) - 1)

    nb = (_nbytes((tq, QW), BF16) + _nbytes((tq, HP), BF16) + _nbytes((tq, HP), F32) + _nbytes((S, QW), BF16)
          + _nbytes((S, HP), BF16) + 2 * _nbytes((tq, QW), F32) + _nbytes((S, QW), F32) + _nbytes((S, HP), F32))
    return pl.pallas_call(
        body, name="flash_bwd_side" if sd.on else "flash_bwd", grid=(H, S // tq),
        in_specs=[pl.BlockSpec((tq, QW), lambda h, i: (i, h)), pl.BlockSpec((tq, HP), lambda h, i: (i, h)),
                  pl.BlockSpec((tq, HP), lambda h, i: (i, h)), pl.BlockSpec((1, 1, tq), lambda h, i: (h, 0, i)),
                  pl.BlockSpec((S, QW), lambda h, i: (0, h)), pl.BlockSpec((S, HP), lambda h, i: (0, h))] + sd.in_specs,
        out_specs=[pl.BlockSpec((tq, QW), lambda h, i: (i, h)), pl.BlockSpec((S, QW), lambda h, i: (0, h)),
                   pl.BlockSpec((S, HP), lambda h, i: (0, h))] + sd.out_specs,
        out_shape=[jax.ShapeDtypeStruct((S, H * QW), F32), jax.ShapeDtypeStruct((S, H * QW), F32),
                   jax.ShapeDtypeStruct((S, H * HP), F32)] + sd.out_shape,
        scratch_shapes=[pltpu.VMEM((tq, QW), F32)] + sd.scratch,
        compiler_params=_params(("arbitrary", "arbitrary") if sd.on else ("parallel", "arbitrary"), nb),
    )(_hbm(Q), _hbm(dO), _hbm(O), lse, _hbm(K), _hbm(V), *sd.operands)


RET_CHUNK = 256


def _ret_tables(d, lg_ref, h, C):
    ii = lax.broadcasted_iota(jnp.int32, (C, C), 0).astype(F32)
    jj = lax.broadcasted_iota(jnp.int32, (C, C), 1).astype(F32)
    ci = lax.broadcasted_iota(jnp.int32, (C, 1), 0).astype(F32)
    fwd = d == 0
    dist = jnp.where(fwd, ii - jj, jj - ii)
    mask = dist >= jnp.where(fwd, 0.0, 1.0)
    dist = jnp.maximum(dist, 0.0)
    qpos = jnp.where(fwd, ci + 1.0, C - ci)
    kpos = jnp.where(fwd, C - 1.0 - ci, ci)
    lg = lg_ref[d, h]
    D = jnp.where(mask, jnp.exp(lg * dist), 0.0)
    return D, jnp.exp(lg * qpos), jnp.exp(lg * kpos), jnp.exp(lg * C), dist, qpos, kpos


def _ret_fwd(lg, q, k, v):
    S = q.shape[0]
    C = min(RET_CHUNK, S)
    N = S // C
    H = RET_HEADS

    def chunk(d, n):
        return jnp.where(d == 0, n, N - 1 - n)

    def body(lg_ref, q_ref, k_ref, v_ref, o_ref, st_ref, state, tab_d, tab_q, tab_k):
        d, n = pl.program_id(0), pl.program_id(1)

        @pl.when(n == 0)
        def _():
            state[...] = jnp.zeros_like(state)
            for h in range(H):
                D, qw, kw, _, _, _, _ = _ret_tables(d, lg_ref, h, C)
                tab_d[h] = D
                tab_q[h] = jnp.broadcast_to(qw, (C, HP))
                tab_k[h] = jnp.broadcast_to(kw, (C, HP))

        for h in range(H):
            sl = slice(HP * h, HP * (h + 1))
            D, qw, kw = tab_d[h], tab_q[h], tab_k[h]
            gc = jnp.exp(lg_ref[d, h] * C)
            qh, kh, vh = q_ref[:, sl], k_ref[:, sl], v_ref[:, sl]
            st = state[sl, :]
            sm = lax.dot_general(qh, kh, (((1,), (1,)), ((), ())), preferred_element_type=F32) * D
            inner = jnp.dot(sm.astype(BF16), vh, preferred_element_type=F32)
            cross = qw * jnp.dot(qh, st.astype(BF16), preferred_element_type=F32)
            o_ref[:, sl] = inner + cross
            st_ref[0, 0, sl, :] = st
            kvn = lax.dot_general((kh.astype(F32) * kw).astype(BF16), vh, (((0,), (0,)), ((), ())), preferred_element_type=F32)
            state[sl, :] = gc * st + kvn

    nb = 3 * _nbytes((C, 512), BF16) + _nbytes((C, 512), F32) + 2 * _nbytes((512, HP), F32) + 8 * _nbytes((C, C), F32)
    return pl.pallas_call(
        body, name="ret_fwd", grid=(2, N),
        in_specs=[pl.BlockSpec(memory_space=pltpu.SMEM)] + [pl.BlockSpec((C, 512), lambda d, n: (chunk(d, n), 0))] * 3,
        out_specs=[pl.BlockSpec((C, 512), lambda d, n: (chunk(d, n), d)),
                   pl.BlockSpec((1, 1, 512, HP), lambda d, n: (d, chunk(d, n), 0, 0))],
        out_shape=[jax.ShapeDtypeStruct((S, 1024), F32), jax.ShapeDtypeStruct((2, N, 512, HP), F32)],
        scratch_shapes=[pltpu.VMEM((512, HP), F32), pltpu.VMEM((H, C, C), F32), pltpu.VMEM((H, C, HP), F32),
                        pltpu.VMEM((H, C, HP), F32)],
        compiler_params=_params(("arbitrary", "arbitrary"), nb),
    )(lg, _hbm(q), _hbm(k), _hbm(v))


def _ret_bwd(lg, q, k, v, do, states):
    S = q.shape[0]
    C = min(RET_CHUNK, S)
    N = S // C
    H = RET_HEADS

    def chunk(d, n):
        return jnp.where(d == 0, N - 1 - n, n)

    def body(lg_ref, q_ref, k_ref, v_ref, do_ref, st_ref, dq_ref, dk_ref, dv_ref, dlg_ref, G, accA, accB, accC,
             tab_d, tab_q, tab_k, tab_qp, tab_kp):
        d, n = pl.program_id(0), pl.program_id(1)

        @pl.when(n == 0)
        def _():
            G[...] = jnp.zeros_like(G)
            accA[...] = jnp.zeros_like(accA)
            accB[...] = jnp.zeros_like(accB)
            accC[...] = jnp.zeros_like(accC)
            for h in range(H):
                D, qw, kw, _, _, qpos, kpos = _ret_tables(d, lg_ref, h, C)
                tab_d[h] = D
                tab_q[h] = jnp.broadcast_to(qw, (C, HP))
                tab_k[h] = jnp.broadcast_to(kw, (C, HP))
                tab_qp[h] = jnp.broadcast_to(qw * qpos, (C, HP))
                tab_kp[h] = jnp.broadcast_to(kw * kpos, (C, HP))

        nt = (((1,), (1,)), ((), ()))
        tn = (((0,), (0,)), ((), ()))
        for h in range(H):
            sl = slice(HP * h, HP * (h + 1))
            D, qw, kw = tab_d[h], tab_q[h], tab_k[h]
            gc = jnp.exp(lg_ref[d, h] * C)
            qh, kh, vh, doh = q_ref[:, sl], k_ref[:, sl], v_ref[:, sl], do_ref[:, sl]
            st = st_ref[0, 0, sl, :]
            g = G[sl, :]
            stb, gb = st.astype(BF16), g.astype(BF16)
            sraw = lax.dot_general(qh, kh, nt, preferred_element_type=F32)
            dS = lax.dot_general(doh, vh, nt, preferred_element_type=F32) * D
            dSb = dS.astype(BF16)
            smb = (sraw * D).astype(BF16)
            qs = jnp.dot(qh, stb, preferred_element_type=F32)
            kg = jnp.dot(kh, gb, preferred_element_type=F32)
            dq_ref[:, sl] = jnp.dot(dSb, kh, preferred_element_type=F32) + qw * lax.dot_general(doh, stb, nt, preferred_element_type=F32)
            dk_ref[:, sl] = lax.dot_general(dSb, qh, tn, preferred_element_type=F32) + kw * lax.dot_general(vh, gb, nt, preferred_element_type=F32)
            dv_ref[:, sl] = lax.dot_general(smb, doh, tn, preferred_element_type=F32) + kw * kg
            dof, vf = doh.astype(F32), vh.astype(F32)
            accA[h] += sraw * dS
            accB[h] += tab_qp[h] * (qs * dof) + tab_kp[h] * (kg * vf)
            accC[h] += st * g
            G[sl, :] = gc * g + lax.dot_general((qh.astype(F32) * qw).astype(BF16), doh, tn, preferred_element_type=F32)

        @pl.when(n == N - 1)
        def _():
            rows = []
            for h in range(H):
                _, _, _, gc, dist, _, _ = _ret_tables(d, lg_ref, h, C)
                tot = jnp.sum(accA[h] * dist) + jnp.sum(accB[h]) + (C * gc) * jnp.sum(accC[h])
                rows.append(jnp.broadcast_to(tot, (1, HP)))
            dlg_ref[0] = jnp.concatenate(rows + [jnp.zeros((8 - H, HP), F32)], axis=0)

    nb = (4 * _nbytes((C, 512), BF16) + 3 * _nbytes((C, 512), F32) + 2 * _nbytes((512, HP), F32)
          + H * (_nbytes((C, C), F32) + _nbytes((C, HP), F32) + _nbytes((HP, HP), F32)) + 12 * _nbytes((C, C), F32))
    row = lambda d, n: (chunk(d, n), 0)
    out3 = lambda d, n: (chunk(d, n), d)
    return pl.pallas_call(
        body, name="ret_bwd", grid=(2, N),
        in_specs=[pl.BlockSpec(memory_space=pltpu.SMEM)] + [pl.BlockSpec((C, 512), row)] * 4
        + [pl.BlockSpec((1, 1, 512, HP), lambda d, n: (d, chunk(d, n), 0, 0))],
        out_specs=[pl.BlockSpec((C, 512), out3)] * 3 + [pl.BlockSpec((1, 8, HP), lambda d, n: (d, 0, 0))],
        out_shape=[jax.ShapeDtypeStruct((S, 1024), F32)] * 3 + [jax.ShapeDtypeStruct((2, 8, HP), F32)],
        scratch_shapes=[pltpu.VMEM((512, HP), F32), pltpu.VMEM((H, C, C), F32), pltpu.VMEM((H, C, HP), F32),
                        pltpu.VMEM((H, HP, HP), F32), pltpu.VMEM((H, C, C), F32)] + [pltpu.VMEM((H, C, HP), F32)] * 4,
        compiler_params=_params(("arbitrary", "arbitrary"), nb),
    )(lg, _hbm(q), _hbm(k), _hbm(v), _hbm(do), _hbm(states))


CONV_PAD = 16
CONV_TR = 256
CONV_CB = LANES


def _fill_padded(pad, u_ref, S):
    pad[0:CONV_PAD, :] = jnp.zeros((CONV_PAD, CONV_CB), F32)
    pad[CONV_PAD + S:CONV_PAD + S + CONV_PAD, :] = jnp.zeros((CONV_PAD, CONV_CB), F32)
    pad[CONV_PAD:CONV_PAD + S, :] = u_ref[...]


def _conv_fwd(u, w32, b):
    S, D = u.shape
    tr = min(CONV_TR, S)

    def body(u_ref, w_ref, b_ref, o_ref, pad):
        _fill_padded(pad, u_ref, S)
        wv = w_ref[...]
        bv = b_ref[...]

        def step(t, carry):
            r0 = pl.multiple_of(t * tr, tr)
            acc = jnp.broadcast_to(bv, (tr, CONV_CB))
            for k in range(CONV_KERNEL):
                acc = acc + pad[pl.ds(r0 + (k + 1), tr), :] * wv[k:k + 1, :]
            o_ref[pl.ds(r0, tr), :] = acc
            return carry

        lax.fori_loop(0, S // tr, step, 0)

    nb = 2 * _nbytes((S, CONV_CB), F32) + _nbytes((S + 2 * CONV_PAD, CONV_CB), F32)
    return pl.pallas_call(
        body, name="conv_fwd", grid=(D // CONV_CB,),
        in_specs=[pl.BlockSpec((S, CONV_CB), lambda j: (0, j)), pl.BlockSpec((32, CONV_CB), lambda j: (0, j)),
                  pl.BlockSpec((1, CONV_CB), lambda j: (0, j))],
        out_specs=pl.BlockSpec((S, CONV_CB), lambda j: (0, j)),
        out_shape=jax.ShapeDtypeStruct((S, D), F32),
        scratch_shapes=[pltpu.VMEM((S + 2 * CONV_PAD, CONV_CB), F32)],
        compiler_params=_params(("parallel",), nb),
    )(_hbm(u), w32, b)


def _conv_bwd_w(u, dout):
    S, D = u.shape
    tr = min(CONV_TR, S)

    def body(u_ref, d_ref, o_ref, pad, acc):
        _fill_padded(pad, u_ref, S)
        acc[...] = jnp.zeros_like(acc)

        def fold(a):
            return jnp.sum(a.reshape(tr // 8, 8, CONV_CB), axis=0)

        def step(t, carry):
            r0 = pl.multiple_of(t * tr, tr)
            dv = d_ref[pl.ds(r0, tr), :]
            for k in range(CONV_KERNEL):
                acc[8 * k:8 * k + 8, :] += fold(pad[pl.ds(r0 + (k + 1), tr), :] * dv)
            acc[8 * CONV_KERNEL:8 * CONV_KERNEL + 8, :] += fold(dv)
            return carry

        lax.fori_loop(0, S // tr, step, 0)
        o_ref[...] = jnp.sum(acc[...].reshape(32, 8, CONV_CB), axis=1)

    nb = 2 * _nbytes((S, CONV_CB), F32) + _nbytes((S + 2 * CONV_PAD, CONV_CB), F32)
    return pl.pallas_call(
        body, name="conv_bwd_w", grid=(D // CONV_CB,),
        in_specs=[pl.BlockSpec((S, CONV_CB), lambda j: (0, j)), pl.BlockSpec((S, CONV_CB), lambda j: (0, j))],
        out_specs=pl.BlockSpec((32, CONV_CB), lambda j: (0, j)),
        out_shape=jax.ShapeDtypeStruct((32, D), F32),
        scratch_shapes=[pltpu.VMEM((S + 2 * CONV_PAD, CONV_CB), F32), pltpu.VMEM((256, CONV_CB), F32)],
        compiler_params=_params(("parallel",), nb),
    )(_hbm(u), _hbm(dout))


def _adamw(w, g, m, v):
    shape = w.shape
    w2, g2, m2, v2 = [a.reshape(-1, shape[-1]) for a in (w, g, m, v)]
    W = shape[-1]

    def fn(rows, p):
        w_, g_, m_, v_ = [r[...] for r in rows]
        mn = ADAM_B1 * m_ + (1.0 - ADAM_B1) * g_
        vn = ADAM_B2 * v_ + (1.0 - ADAM_B2) * jnp.square(g_)
        m_hat = mn / (1.0 - ADAM_B1 ** ADAM_STEP)
        v_hat = vn / (1.0 - ADAM_B2 ** ADAM_STEP)
        delta = -ADAM_LR * (m_hat / (jnp.sqrt(v_hat) + ADAM_EPS) + ADAM_WD * w_)
        return [delta, mn, vn], []

    R2 = w2.shape[0]
    ts = 256 if (R2 > 256 and R2 % 256 == 0) else R2
    d, mn, vn = _rowcall("adamw", fn, [w2, g2, m2, v2], [], [(W, F32)] * 3, ts=ts)
    return d.reshape(shape), mn.reshape(shape), vn.reshape(shape)


_HBM = pl.BlockSpec(memory_space=pltpu.HBM)


def _place():
    return lax.axis_index("x"), lax.axis_index("y"), lax.axis_index("c")


def _half_rows(ref, half, h):
    idx = (slice(None),) * (len(ref.shape) - 2) + (pl.ds(half * h, h), slice(None))
    return ref.at[idx]


def _chip_exchange(name, p, mode):
    assert mode in ("half", "whole", "scatter")
    if mode == "half":
        h, W = p.shape[0] // 2, p.shape[1]
    elif mode == "whole":
        h, W = p.shape
    else:
        h, W = p.shape[1], p.shape[2]

    def body(p_ref, o_ref, send_sems, recv_sems, local_sem):
        copies = _chip_copies(p_ref, o_ref, send_sems, recv_sems, local_sem, mode, h)
        for cp in copies:
            cp.start()
        for cp in copies:
            cp.wait()

    return pl.pallas_call(
        body, name=name, in_specs=[_HBM], out_specs=_HBM,
        out_shape=jax.ShapeDtypeStruct((4, h, W), p.dtype), scratch_shapes=_CHIP_SEMS,
    )(p)


_CHIP_SEMS = [pltpu.SemaphoreType.DMA((3,)), pltpu.SemaphoreType.DMA((3,)), pltpu.SemaphoreType.DMA]


def _chip_copies(p_ref, o_ref, send_sems, recv_sems, local_sem, mode, h):
    x, y, c = _place()
    k_me = 2 * x + y

    def src(k):
        if mode == "half":
            return _half_rows(p_ref, c, h)
        return p_ref if mode == "whole" else p_ref.at[k]

    copies = [pltpu.make_async_copy(src(k_me), o_ref.at[k_me], local_sem)]
    for j, (cx, cy) in enumerate([(1 - x, y), (x, 1 - y), (1 - x, 1 - y)]):
        copies.append(pltpu.make_async_remote_copy(
            src_ref=src(2 * cx + cy), dst_ref=o_ref.at[k_me], send_sem=send_sems.at[j], recv_sem=recv_sems.at[j],
            device_id=(cx, cy, c), device_id_type=MESH))
    return copies


def _sibling_send_half(name, g):
    n, R, W = g.shape
    h = R // 2

    def body(g_ref, o_ref, send_sem, recv_sem):
        x, y, c = _place()
        cp = pltpu.make_async_remote_copy(
            src_ref=_half_rows(g_ref, 1 - c, h), dst_ref=o_ref, send_sem=send_sem, recv_sem=recv_sem,
            device_id=(x, y, 1 - c), device_id_type=MESH)
        cp.start()
        cp.wait()

    return pl.pallas_call(
        body, name=name, in_specs=[_HBM], out_specs=_HBM,
        out_shape=jax.ShapeDtypeStruct((n, h, W), g.dtype),
        scratch_shapes=[pltpu.SemaphoreType.DMA, pltpu.SemaphoreType.DMA],
    )(g)


def _halves_merge(name, mine):
    def body(m_ref, o_ref, send_sem, recv_sem):
        x, y, c = _place()
        cp = pltpu.make_async_remote_copy(
            src_ref=m_ref, dst_ref=o_ref, send_sem=send_sem, recv_sem=recv_sem,
            device_id=(x, y, 1 - c), device_id_type=MESH)
        cp.start()
        cp.wait()

    theirs = pl.pallas_call(
        body, name=name, in_specs=[_HBM], out_specs=_HBM,
        out_shape=jax.ShapeDtypeStruct(mine.shape, mine.dtype),
        scratch_shapes=[pltpu.SemaphoreType.DMA, pltpu.SemaphoreType.DMA],
    )(mine)
    south = lax.axis_index("c") == 0
    axis = mine.ndim - 2
    return jnp.concatenate([jnp.where(south, mine, theirs), jnp.where(south, theirs, mine)], axis=axis)


def _add2(name, a, b, out_dtype):
    def fn(rows, p):
        return [rows[0][...].astype(F32) + rows[1][...].astype(F32)], []

    return _rowcall(name, fn, [a, b], [], [(a.shape[1], out_dtype)], ts=PACK_TILE)[0]


def _sum4(name, b):
    _, h, W = b.shape
    ts = PACK_TILE if h % PACK_TILE == 0 else h

    def body(b_ref, o_ref):
        o_ref[...] = ((b_ref[0].astype(F32) + b_ref[1].astype(F32)) + b_ref[2].astype(F32)) + b_ref[3].astype(F32)

    nb = _nbytes((4, ts, W), b.dtype) + _nbytes((ts, W), F32)
    return pl.pallas_call(
        body, name=name, grid=(h // ts,),
        in_specs=[pl.BlockSpec((4, ts, W), lambda i: (0, i, 0))], out_specs=pl.BlockSpec((ts, W), lambda i: (i, 0)),
        out_shape=jax.ShapeDtypeStruct((h, W), F32), compiler_params=_params(("parallel",), nb),
    )(b)


PACK_W = 1024
PACK_ROWS = 16
PACK_TILE = 256
EV_SHARDED = (("ev_w_in", "col", False), ("ev_w_uq", "col", False), ("ev_w_ukv", "col", False), ("ev_w_out", "row", False))
OD_SHARDED = (("od_w_in", "col", False), ("od_b_in", "col", True), ("od_dw_w", "col", True), ("od_dw_b", "col", True),
              ("od_ln_g", "col", True), ("od_ln_b", "col", True), ("od_w_out", "row", False))
REPLICATED =("ada_b", "pre_g", "post_g", "ev_dec_f", "ev_dec_b", "ev_q_norm_g", "ev_kv_norm_g")
WEIGHTS = ("ada_w", "ada_b", "pre_g", "post_g", "ev_w_in", "ev_dec_f", "ev_dec_b", "ev_q_norm_g", "ev_w_uq", "ev_kv_norm_g",
           "ev_w_ukv", "ev_w_out", "od_w_in", "od_b_in", "od_dw_w", "od_dw_b", "od_ln_g", "od_ln_b", "od_w_out")


def _layer_entries(layers):
    out = []
    for l in layers:
        out += [(name, how, exact, l // 2) for name, how, exact in (EV_SHARDED if l % 2 == 0 else OD_SHARDED)]
    return out


def _rows_of(n):
    unit = PACK_W * PACK_ROWS
    return (n + unit - 1) // unit * PACK_ROWS


def _to_rows(flat, lead):
    n = flat.shape[-1]
    r = _rows_of(n)
    flat = jnp.pad(flat, [(0, 0)] * len(lead) + [(0, r * PACK_W - n)])
    return flat.reshape(*lead, r, PACK_W)


def _pad_total(buf, axis):
    r = buf.shape[axis]
    r2 = (r + 2 * PACK_TILE - 1) // (2 * PACK_TILE) * (2 * PACK_TILE)
    pads = [(0, 0)] * buf.ndim
    pads[axis] = (0, r2 - r)
    return jnp.pad(buf, pads)


def _split_chips(full, how):
    if how == "col":
        n = full.shape[-1] // 4
        return jnp.moveaxis(full.reshape(*full.shape[:-1], 4, n), -2, 0)
    n = full.shape[-2] // 4
    return jnp.moveaxis(full.reshape(*full.shape[:-2], 4, n, full.shape[-1]), -3, 0)


def _join_chips(blocks, how):
    if how == "col":
        t = jnp.moveaxis(blocks, 0, -2)
        return t.reshape(*t.shape[:-2], t.shape[-2] * t.shape[-1])
    t = jnp.moveaxis(blocks, 0, -3)
    return t.reshape(*t.shape[:-3], t.shape[-3] * t.shape[-2], t.shape[-1])


def _pack_weights(w, entries):
    parts = []
    for name, _, exact, idx in entries:
        a = w[name][idx].reshape(-1)
        a = lax.bitcast_convert_type(a, BF16).reshape(-1) if exact else a.astype(BF16)
        parts.append(_to_rows(a, ()))
    return _pad_total(jnp.concatenate(parts, axis=0), 0)


def _unpack_weights(gathered, w, entries):
    out, r0 = {}, 0
    for name, how, exact, idx in entries:
        shp = w[name].shape[1:]
        n = (2 if exact else 1) * w[name][idx].size
        r = _rows_of(n)
        a = gathered[:, r0:r0 + r].reshape(4, -1)[:, :n]
        if exact:
            a = lax.bitcast_convert_type(a.reshape(4, -1, 2), F32)
        out[(name, idx)] = _join_chips(a.reshape(4, *shp), how)
        r0 += r
    return out


def _pack_grads(grads, entries, replicated):
    parts = []
    for name, how, _, idx in entries:
        b = _split_chips(grads[(name, idx)], how)
        parts.append(_to_rows(b.reshape(4, -1).astype(BF16), (4,)))
    for name in replicated:
        a = _to_rows(grads[name].reshape(-1).astype(BF16), ())
        parts.append(jnp.broadcast_to(a[None], (4, *a.shape)))
    return _pad_total(jnp.concatenate(parts, axis=1), 1)


def _reduce_begin(tag, g):
    h = g.shape[1] // 2
    c = lax.axis_index("c")
    theirs = _sibling_send_half("reduce_cores_" + tag, g)
    mine = lax.dynamic_slice_in_dim(g, c * h, h, axis=1)
    return _add2("reduce_add_" + tag, mine.reshape(4 * h, PACK_W), theirs.reshape(4 * h, PACK_W), BF16).reshape(4, h, PACK_W)


def _reduce_end(tag, exchanged):
    return _halves_merge("reduce_merge_" + tag, _sum4("reduce_sum_" + tag, exchanged))


def _unpack_grads(total, w, entries, replicated):
    out, r0 = {}, 0
    for name, _, _, idx in entries:
        n = w[name][idx].size
        r = _rows_of(n)
        out[(name, idx)] = total[r0:r0 + r].reshape(-1)[:n].reshape(w[name].shape[1:])
        r0 += r
    for name in replicated:
        n = w[name].size
        r = _rows_of(n)
        out[name] = total[r0:r0 + r].reshape(-1)[:n].reshape(w[name].shape)
        r0 += r
    return out


def _pad_heads(a, n, w, to):
    lead = a.shape[:-1]
    return jnp.pad(a.reshape(*lead, n, w), [(0, 0)] * (len(lead) + 1) + [(0, to - w)]).reshape(*lead, n * to)


def _unpad_heads(a, n, w, to):
    lead = a.shape[:-1]
    return a.reshape(*lead, n, to)[..., :w].reshape(*lead, n * w)


def _w_in_pad(w):
    return jnp.concatenate([_pad_heads(w[:, 0:256], 4, 64, HP), _pad_heads(w[:, 256:512], 4, 64, HP), w[:, 512:1536],
                            w[:, 2240:2752], w[:, 1536:2176], _pad_heads(w[:, 2176:2240], 1, 64, HP)], axis=1)


def _w_in_unpad(g):
    return jnp.concatenate([_unpad_heads(g[:, Z_RQ:Z_RK], 4, 64, HP), _unpad_heads(g[:, Z_RK:Z_RV], 4, 64, HP),
                            g[:, Z_RV:Z_MG], g[:, Z_CQ:Z_KR], g[:, Z_KR:Z_KR + 64], g[:, Z_MG:Z_CQ]], axis=1)


def _layer_weights(l, full, w):
    f32 = lambda a: a.astype(F32)
    i = l // 2
    common = dict(bias=f32(w["ada_b"][l])[None], pre_g=f32(w["pre_g"][l])[None], post_g=f32(w["post_g"][l])[None])
    if l % 2 == 0:
        w_uq_p = _pad_heads(full[("ev_w_uq", i)], MLA_HEADS, MLA_NOPE_DIM + MLA_ROPE_DIM, QW)
        lg = jnp.stack([jax.nn.log_sigmoid(f32(w["ev_dec_f"][i])), jax.nn.log_sigmoid(f32(w["ev_dec_b"][i]))])
        return dict(common, w_in=_w_in_pad(full[("ev_w_in", i)]), w_uq=w_uq_p, w_ukv=full[("ev_w_ukv", i)],
                    w_out=full[("ev_w_out", i)], qg=f32(w["ev_q_norm_g"][i])[None], kvg=f32(w["ev_kv_norm_g"][i])[None], lg=lg)
    dw = f32(full[("od_dw_w", i)])
    return dict(common, w_in=full[("od_w_in", i)], w_out=full[("od_w_out", i)], b_in=f32(full[("od_b_in", i)])[None],
                dw=jnp.pad(dw, ((0, 1), (0, 0))), dw_flip=jnp.pad(dw[::-1], ((0, 1), (0, 0))),
                dw_b=f32(full[("od_dw_b", i)])[None], ln_g=f32(full[("od_ln_g", i)])[None], ln_b=f32(full[("od_ln_b", i)])[None])


def _rope_tables(positions):
    inv_freq = ROPE_BASE ** (-jnp.arange(0, ROPE_DIM, 2, dtype=F32) / ROPE_DIM)
    ang = positions.astype(F32)[:, None] * inv_freq
    z = jnp.zeros((ang.shape[0], HP - ROPE_DIM), F32)
    return jnp.concatenate([jnp.cos(ang), jnp.cos(ang), z], axis=1), jnp.concatenate([jnp.sin(ang), jnp.sin(ang), z], axis=1)


def _even_fwd(x, mod, cos, sin, w, side=None):
    bias, pre_g, post_g = w["bias"], w["pre_g"], w["post_g"]
    h, z = _pre_mm("ev_in", x, pre_g, mod, bias, w["w_in"])
    rq, rk, rv, qn, kvn, Q, K, V = _ev_mid_fwd(z, cos, sin, w["qg"], w["kvg"], w["w_uq"], w["w_ukv"])
    a, lse, *got = _flash_fwd(Q, K, V, side)
    o2, states = _ret_fwd(w["lg"], rq, rk, rv)
    mix, y, x_new = _ev_out_fwd(o2, z, a, x, w["w_out"], post_g, mod, bias)
    saved = dict(x=x, h=h, z=z, rq=rq, rk=rk, rv=rv, qn=qn, kvn=kvn, Q=Q, K=K, V=V, a=a, lse=lse, o2=o2,
                 states=states, mix=mix, y=y)
    return x_new, saved, (got[0] if got else None)


def _even_bwd(dx, s, mod, cos, sin, w, side=None):
    bias, pre_g, post_g = w["bias"], w["pre_g"], w["post_g"]
    dy, do, drg, da, dmg, dpost_g, dgate = _ev_out_bwd(s["y"], dx, s["o2"], s["z"], s["a"], w["w_out"], post_g, mod, bias)
    dw_out = _mm_tn("ev_out_dw", s["mix"], dy)
    drq, drk, drv, dlg = _ret_bwd(w["lg"], s["rq"], s["rk"], s["rv"], do, s["states"])
    dQ, dK, dV, *got = _flash_bwd(s["Q"], s["K"], s["V"], da, s["a"], s["lse"], side)
    dqf, dkv, dqn, dkvn, dkr = _ev_mid_bwd(dQ, dK, dV, cos, sin, w["w_uq"], w["w_ukv"])
    dw_uq = _mm_tn("ev_uq_dw", s["qn"], dqf)
    dw_ukv = _mm_tn("ev_ukv_dw", s["kvn"], dkv)
    dz, dx_new, dqg, dkvg, dpre_g, dscale, dshift = _ev_in_bwd(
        s["z"], cos, sin, w["qg"], w["kvg"], drq, drk, drv, drg, dqn, dkvn, dkr, dmg, s["x"], dx, w["w_in"], pre_g, mod, bias)
    dw_in = _mm_tn("ev_in_dw", s["h"], dz)
    g = dict(ev_w_in=_w_in_unpad(dw_in), ev_w_uq=_unpad_heads(dw_uq, MLA_HEADS, MLA_NOPE_DIM + MLA_ROPE_DIM, QW),
             ev_w_ukv=dw_ukv, ev_w_out=dw_out, ev_q_norm_g=dqg[0], ev_kv_norm_g=dkvg[0],
             dlg_f=dlg[0, :RET_HEADS, 0], dlg_b=dlg[1, :RET_HEADS, 0])
    return dx_new, g, dpre_g[0], dpost_g[0], jnp.concatenate([dshift, dscale, dgate], axis=1), (got[0] if got else None)


def _odd_fwd(x, mod, w, tgt=None):
    bias, pre_g, post_g = w["bias"], w["pre_g"], w["post_g"]
    h, z, u = _od_in_fwd(x, pre_g, mod, bias, w["w_in"], w["b_in"])
    uc = _conv_fwd(u, w["dw"], w["dw_b"])
    vv, y, *rest = _od_out_fwd(uc, z, x, w["w_out"], w["b_in"], w["ln_g"], w["ln_b"], post_g, mod, bias, tgt)
    return (rest[0] if tgt is None else tuple(rest)), dict(x=x, h=h, z=z, u=u, uc=uc, vv=vv, y=y)


def _odd_bwd(dx, s, mod, w):
    bias, pre_g, post_g = w["bias"], w["pre_g"], w["post_g"]
    dy, duc, dg, dpost_g, dgate, dbg, dln_g, dln_b = _od_out_bwd(
        s["y"], dx, s["uc"], s["z"], w["w_out"], w["b_in"], w["ln_g"], w["ln_b"], post_g, mod, bias)
    dw_out = _mm_tn("od_out_dw", s["vv"], dy)
    du = _conv_fwd(duc, w["dw_flip"], jnp.zeros_like(w["dw_b"]))
    dwb = _conv_bwd_w(s["u"], duc)
    dz, dx_new, dba, dbb, dpre_g, dscale, dshift = _od_in_bwd(s["z"], w["b_in"], du, dg, s["x"], dx, w["w_in"], pre_g, mod, bias)
    dw_in = _mm_tn("od_in_dw", s["h"], dz)
    g = dict(od_w_in=dw_in, od_b_in=jnp.concatenate([dba, dbb, dbg], axis=1)[0], od_dw_w=dwb[:CONV_KERNEL], od_dw_b=dwb[CONV_KERNEL],
             od_ln_g=dln_g[0], od_ln_b=dln_b[0], od_w_out=dw_out)
    return dx_new, g, dpre_g[0], dpost_g[0], jnp.concatenate([dshift, dscale, dgate], axis=1)


LATE_LAYERS = tuple(range(1, DEPTH))


def _step(x, mod, positions, loss_target, w):
    cos, sin = _rope_tables(positions)
    f32 = lambda a: a.astype(F32)
    mods = [mod[l:l + 1] for l in range(DEPTH)]
    first, late = _layer_entries([0]), _layer_entries(LATE_LAYERS)

    gathered = _halves_merge("gather_merge_first", _chip_exchange("gather_chips_first", _pack_weights(w, first), "half"))
    full = _unpack_weights(gathered, w, first)
    lw = {0: _layer_weights(0, full, w)}
    x, s0, got = _even_fwd(x, mods[0], cos, sin, lw[0], side=(_pack_weights(w, late), "half"))
    full.update(_unpack_weights(_halves_merge("gather_merge_late", got), w, late))
    saved = [s0]
    assert DEPTH % 2 == 0
    for l in LATE_LAYERS:
        lw[l] = _layer_weights(l, full, w)
        if l % 2 == 0:
            x, s, _ = _even_fwd(x, mods[l], cos, sin, lw[l])
        else:
            x, s = _odd_fwd(x, mods[l], lw[l], loss_target if l == DEPTH - 1 else None)
        saved.append(s)
    dx, loss = x

    grads = {}
    dpre, dpost, dmod = [None] * DEPTH, [None] * DEPTH, [None] * DEPTH
    for l in reversed(LATE_LAYERS):
        if l % 2 == 0:
            dx, g, dpre[l], dpost[l], dmod[l], _ = _even_bwd(dx, saved[l], mods[l], cos, sin, lw[l])
        else:
            dx, g, dpre[l], dpost[l], dmod[l] = _odd_bwd(dx, saved[l], mods[l], lw[l])
        grads.update({(name, l // 2): val for name, val in g.items()})
    chip_sum = _reduce_begin("late", _pack_grads(grads, late, ()))
    dx, g, dpre[0], dpost[0], dmod[0], got = _even_bwd(dx, saved[0], mods[0], cos, sin, lw[0], side=(chip_sum, "scatter"))
    grads.update({(name, 0): val for name, val in g.items()})
    red = _unpack_grads(_reduce_end("late", got), w, late, ())

    n_ev = (DEPTH + 1) // 2
    grads.update(pre_g=jnp.stack(dpre), post_g=jnp.stack(dpost), ada_b=jnp.concatenate(dmod, axis=0))
    for name in ("ev_q_norm_g", "ev_kv_norm_g"):
        grads[name] = jnp.stack([grads[(name, i)] for i in range(n_ev)])
    grads["ev_dec_f"] = jnp.stack([grads[("dlg_f", i)] for i in range(n_ev)]) * jax.nn.sigmoid(-f32(w["ev_dec_f"]))
    grads["ev_dec_b"] = jnp.stack([grads[("dlg_b", i)] for i in range(n_ev)]) * jax.nn.sigmoid(-f32(w["ev_dec_b"]))
    chip_sum = _reduce_begin("first", _pack_grads(grads, first, REPLICATED))
    total = _reduce_end("first", _chip_exchange("reduce_chips_first", chip_sum, "scatter"))
    red.update(_unpack_grads(total, w, first, REPLICATED))

    out = {name: red[name] for name in REPLICATED}
    for name, _, _ in EV_SHARDED:
        out[name] = jnp.stack([red[(name, i)] for i in range(n_ev)])
    for name, _, _ in OD_SHARDED:
        out[name] = jnp.stack([red[(name, i)] for i in range(DEPTH // 2)])
    return loss[0, 0], dx, out, grads["ada_b"]


ADA_ROWS = 16
ADA_COLS = 3 * D_MODEL // 4
ADA_PACK = (8, 1024)


def _ada_pack(a):
    n = ADA_PACK[0] * ADA_PACK[1]
    return jnp.pad(a, ((0, 0), (0, n - a.shape[1]))).reshape(4, *ADA_PACK)


def _ada_unpack(a):
    return a.reshape(4, -1)[:, :DEPTH * ADA_COLS]


def _ada_forward(c, ada_w):
    D = D_MODEL
    c16 = jnp.zeros((ADA_ROWS, D), F32).at[0].set(c[0])
    c_act = _rowcall("silu_c", lambda rows, p: ([_silu(rows[0][...])], []), [c16], [], [(D, BF16)])[0]
    got = _chip_exchange("ada_c", c_act, "whole")
    c4 = jnp.pad(got[:, 0, :], ((0, ADA_ROWS - 4), (0, 0)))
    per_layer = [_mm_nn("ada", c4, ada_w[l].astype(BF16))[:4] for l in range(DEPTH)]
    p = jnp.stack(per_layer, axis=1).reshape(4, DEPTH * ADA_COLS)
    got = _chip_exchange("ada_mod", _ada_pack(p), "scatter")
    mod = _ada_unpack(got).reshape(4, DEPTH, ADA_COLS).transpose(1, 0, 2).reshape(DEPTH, 3 * D)
    return mod, c4


def _ada_backward(dmod, c4):
    p = dmod.reshape(DEPTH, 4, ADA_COLS).transpose(1, 0, 2).reshape(4, DEPTH * ADA_COLS)
    got = _chip_exchange("ada_dmod", _ada_pack(p), "scatter")
    dm4 = jnp.pad(_ada_unpack(got), ((0, ADA_ROWS - 4), (0, 0)))
    both = _halves_merge("ada_merge", jnp.concatenate([dm4, c4.astype(F32)], axis=1))
    dm32, c32 = both[:, :DEPTH * ADA_COLS].astype(BF16), both[:, DEPTH * ADA_COLS:].astype(BF16)
    return jnp.stack([_mm_tn("ada_dw", c32, dm32[:, l * ADA_COLS:(l + 1) * ADA_COLS]) for l in range(DEPTH)])


def kernel(x, c, positions, ada_w, ada_b, pre_g, post_g, ev_w_in, ev_dec_f, ev_dec_b, ev_q_norm_g, ev_w_uq, ev_kv_norm_g, ev_w_ukv, ev_w_out, od_w_in, od_b_in, od_dw_w, od_dw_b, od_ln_g, od_ln_b, od_w_out, loss_target, m_ada_w, m_ada_b, m_pre_g, m_post_g, m_ev_w_in, m_ev_dec_f, m_ev_dec_b, m_ev_q_norm_g, m_ev_w_uq, m_ev_kv_norm_g, m_ev_w_ukv, m_ev_w_out, m_od_w_in, m_od_b_in, m_od_dw_w, m_od_dw_b, m_od_ln_g, m_od_ln_b, m_od_w_out, v_ada_w, v_ada_b, v_pre_g, v_post_g, v_ev_w_in, v_ev_dec_f, v_ev_dec_b, v_ev_q_norm_g, v_ev_w_uq, v_ev_kv_norm_g, v_ev_w_ukv, v_ev_w_out, v_od_w_in, v_od_b_in, v_od_dw_w, v_od_dw_b, v_od_ln_g, v_od_ln_b, v_od_w_out):
    w = dict(ada_w=ada_w, ada_b=ada_b, pre_g=pre_g, post_g=post_g, ev_w_in=ev_w_in, ev_dec_f=ev_dec_f, ev_dec_b=ev_dec_b,
             ev_q_norm_g=ev_q_norm_g, ev_w_uq=ev_w_uq, ev_kv_norm_g=ev_kv_norm_g, ev_w_ukv=ev_w_ukv, ev_w_out=ev_w_out,
             od_w_in=od_w_in, od_b_in=od_b_in, od_dw_w=od_dw_w, od_dw_b=od_dw_b, od_ln_g=od_ln_g, od_ln_b=od_ln_b, od_w_out=od_w_out)
    m = dict(ada_w=m_ada_w, ada_b=m_ada_b, pre_g=m_pre_g, post_g=m_post_g, ev_w_in=m_ev_w_in, ev_dec_f=m_ev_dec_f, ev_dec_b=m_ev_dec_b,
             ev_q_norm_g=m_ev_q_norm_g, ev_w_uq=m_ev_w_uq, ev_kv_norm_g=m_ev_kv_norm_g, ev_w_ukv=m_ev_w_ukv, ev_w_out=m_ev_w_out,
             od_w_in=m_od_w_in, od_b_in=m_od_b_in, od_dw_w=m_od_dw_w, od_dw_b=m_od_dw_b, od_ln_g=m_od_ln_g, od_ln_b=m_od_ln_b, od_w_out=m_od_w_out)
    v = dict(ada_w=v_ada_w, ada_b=v_ada_b, pre_g=v_pre_g, post_g=v_post_g, ev_w_in=v_ev_w_in, ev_dec_f=v_ev_dec_f, ev_dec_b=v_ev_dec_b,
             ev_q_norm_g=v_ev_q_norm_g, ev_w_uq=v_ev_w_uq, ev_kv_norm_g=v_ev_kv_norm_g, ev_w_ukv=v_ev_w_ukv, ev_w_out=v_ev_w_out,
             od_w_in=v_od_w_in, od_b_in=v_od_b_in, od_dw_w=v_od_dw_w, od_dw_b=v_od_dw_b, od_ln_g=v_od_ln_g, od_ln_b=v_od_ln_b, od_w_out=v_od_w_out)

    mod, c4 = _ada_forward(c, ada_w)
    loss_local, grad_x, g, dmod = _step(x[0], mod, positions[0], loss_target[0], w)
    loss = lax.psum(loss_local, ("x", "y", "c"))
    g["ada_w"] = _ada_backward(dmod, c4)
    delta, new_m, new_v = {}, {}, {}
    for name in WEIGHTS:
        delta[name], new_m[name], new_v[name] = _adamw(w[name], g[name], m[name], v[name])
    return (loss, grad_x[None], *[g[n] for n in WEIGHTS], *[delta[n] for n in WEIGHTS],
            *[new_m[n] for n in WEIGHTS], *[new_v[n] for n in WEIGHTS])
```

```python
import functools

import jax
import jax.numpy as jnp
from jax import lax
from jax.experimental import pallas as pl
from jax.experimental.pallas import tpu as pltpu

F32 = jnp.float32
BF16 = jnp.bfloat16

D_MODEL = 1024
DEPTH = 4
RET_HEADS = 4
RET_QK_DIM = 64
RET_V_DIM = 128
MLA_HEADS = 4
MLA_Q_RANK = 384
MLA_KV_RANK = 256
MLA_NOPE_DIM = 128
MLA_ROPE_DIM = 64
MLA_V_DIM = 128
ROPE_DIM = 64
ROPE_BASE = 10000.0
CONV_KERNEL = 31
EPS = 1e-6
ADAM_LR, ADAM_B1, ADAM_B2, ADAM_EPS, ADAM_WD, ADAM_STEP = 0.001, 0.9, 0.999, 1e-08, 0.01, 10

LANES = 128
HP = 128
Z_RQ, Z_RK, Z_RV, Z_RG, Z_MG, Z_CQ, Z_CKV, Z_KR, Z_END = 0, 512, 1024, 1536, 2048, 2560, 2944, 3200, 3328
Z_LAT = Z_END - Z_CQ


def _z_ret(z):
    return (z, Z_RG, 0)


def _z_gates(z):
    return [(z, 512, Z_RG // 512), (z, 512, Z_MG // 512)]


def _z_latents(z):
    return [(z, 256, Z_CQ // 256 + i) for i in range(Z_LAT // 256)]
V7X_VMEM_BYTES = 64 * 1024 * 1024
VMEM_CAP = V7X_VMEM_BYTES - 8 * 1024 * 1024

MESH = pl.DeviceIdType.MESH


def _nbytes(shape, dtype):
    n = 1
    for s in shape:
        n *= s
    return n * jnp.dtype(dtype).itemsize


def _hbm(x):
    return pltpu.with_memory_space_constraint(x, pltpu.HBM)


def _params(sem, block_bytes):
    limit = min(VMEM_CAP, max(32 * 1024 * 1024, 2 * block_bytes + 16 * 1024 * 1024))
    return pltpu.CompilerParams(dimension_semantics=sem, vmem_limit_bytes=limit)


ROW_TILE = 512


def _rowcall(name, fn, rows, params, outs, accs=(), ts=ROW_TILE, mats=()):
    rows = [r if isinstance(r, tuple) else (r, r.shape[1], 0) for r in rows]
    S = rows[0][0].shape[0]
    ts = min(ts, S)
    assert S % ts == 0, (name, S, ts)
    nr, npar, nm, no, na = len(rows), len(params), len(mats), len(outs), len(accs)

    def body(*refs):
        row_refs = refs[:nr]
        pvals = [r[...] for r in refs[nr:nr + npar]]
        mat_refs = refs[nr + npar:nr + npar + nm]
        out_refs = refs[nr + npar + nm:nr + npar + nm + no]
        acc_refs = refs[nr + npar + nm + no:]
        ovals, avals = fn(row_refs, pvals, mat_refs) if nm else fn(row_refs, pvals)
        for r, v in zip(out_refs, ovals, strict=True):
            r[...] = v.astype(r.dtype)
        if na:
            @pl.when(pl.program_id(0) == 0)
            def _():
                for r in acc_refs:
                    r[...] = jnp.zeros_like(r)
            for r, v in zip(acc_refs, avals, strict=True):
                r[...] += v

    in_specs = [pl.BlockSpec((ts, w), functools.partial(lambda i, b: (i, b), b=blk)) for _, w, blk in rows]
    in_specs += [pl.BlockSpec(p.shape, lambda i: (0, 0)) for p in [*params, *mats]]
    out_specs = [pl.BlockSpec((ts, w), lambda i: (i, 0)) for w, _ in outs]
    out_specs += [pl.BlockSpec(s, lambda i: (0, 0)) for s in accs]
    out_shape = [jax.ShapeDtypeStruct((S, w), dt) for w, dt in outs]
    out_shape += [jax.ShapeDtypeStruct(s, F32) for s in accs]
    nb = sum(_nbytes((ts, w), a.dtype) for a, w, _ in rows) + sum(_nbytes((ts, w), dt) for w, dt in outs)
    nb += sum(_nbytes(p.shape, p.dtype) for p in params) + sum(_nbytes(s, F32) for s in accs)
    res = pl.pallas_call(
        body, name=name, grid=(S // ts,), in_specs=in_specs, out_specs=out_specs, out_shape=out_shape,
        compiler_params=_params(("arbitrary",) if na else ("parallel",), 3 * nb + sum(_nbytes(m.shape, m.dtype) for m in mats)),
    )(*[_hbm(a) for a, _, _ in rows], *params, *[_hbm(m) for m in mats])
    return res


def _silu(x):
    return x * jax.nn.sigmoid(x)


def _rms(x, g):
    return x * lax.rsqrt(jnp.mean(x * x, axis=-1, keepdims=True) + EPS) * g


def _rot(x):
    lane = lax.broadcasted_iota(jnp.int32, x.shape, 1)
    return jnp.where(lane < ROPE_DIM // 2, -pltpu.roll(x, LANES - ROPE_DIM // 2, 1), pltpu.roll(x, ROPE_DIM // 2, 1))


def _rope(x, cos, sin):
    return x * cos + _rot(x) * sin


def _rope_t(dy, cos, sin):
    return dy * cos - _rot(dy) * sin


def _groups(ref, start, n):
    return [ref[:, start + HP * h:start + HP * (h + 1)] for h in range(n)]


def _pre_math(x, g, m_scale, b_scale, m_shift, b_shift):
    return _rms(x, g) * (1.0 + (m_scale + b_scale)) + (m_shift + b_shift)


def _post_math(x, y, g, m_gate, b_gate):
    return x + (m_gate + b_gate) * _rms(y, g)


def _ev_mid_fwd(z, cos, sin, qg, kvg, w_uq, w_ukv):
    qscale = ATT_SCALE * LOG2E

    def fn(rows, p, mats):
        z_, l0, l1, l2, c_, s_ = rows
        qg_, kvg_ = p
        cos_, sin_ = c_[...], s_[...]
        lat = jnp.concatenate([l0[...], l1[...], l2[...]], axis=1)
        rq = jnp.concatenate([_rope(v[...], cos_, sin_) for v in _groups(z_, Z_RQ, RET_HEADS)], axis=1)
        rk = jnp.concatenate([_rope(v[...], cos_, sin_) for v in _groups(z_, Z_RK, RET_HEADS)], axis=1)
        rk = rk * (RET_QK_DIM ** -0.5)
        rv = z_[:, Z_RV:Z_RG]
        qn = _rms(lat[:, :Z_CKV - Z_CQ], qg_).astype(BF16)
        kvn = _rms(lat[:, Z_CKV - Z_CQ:Z_KR - Z_CQ], kvg_).astype(BF16)
        kr = _rope(lat[:, Z_KR - Z_CQ:], cos_, sin_)
        qf = jnp.dot(qn, mats[0][...], preferred_element_type=F32)
        kv = jnp.dot(kvn, mats[1][...], preferred_element_type=F32)
        qs, ks, vs = [], [], []
        for h in range(MLA_HEADS):
            b = 2 * HP * h
            qs += [qf[:, b:b + HP] * qscale, _rope(qf[:, b + HP:b + 2 * HP], cos_, sin_) * qscale]
            ks += [kv[:, b:b + HP], kr]
            vs += [kv[:, b + HP:b + 2 * HP]]
        return [rq, rk, rv, qn, kvn, jnp.concatenate(qs, axis=1), jnp.concatenate(ks, axis=1), jnp.concatenate(vs, axis=1)], []

    return _rowcall("ev_mid_fwd", fn, [_z_ret(z), *_z_latents(z), cos, sin], [qg, kvg],
                    [(512, BF16), (512, BF16), (512, BF16), (MLA_Q_RANK, BF16), (MLA_KV_RANK, BF16),
                     (1024, BF16), (1024, BF16), (512, BF16)], mats=[w_uq, w_ukv])


def _pre_bwd_math(dh, x, res, g, mod, bias):
    D = D_MODEL
    _, vjp = jax.vjp(_pre_math, x, g, mod[:, D:2 * D], bias[:, D:2 * D], mod[:, :D], bias[:, :D])
    dx, dg, dsc, _, dsh, _ = vjp(dh)
    return dx + res, dg, dsc, dsh


def _ev_in_bwd(z, cos, sin, qg, kvg, drq, drk, drv, drg, dqn, dkvn, dkr, dmg, x, dx_res, w_in, g, mod, bias):
    D = D_MODEL

    def fn(rows, p, mats):
        l0, l1, l2, c_, s_, drq_, drk_, drv_, drg_, dqn_, dkvn_, dkr_, dmg_, x_, res_ = rows
        qg_, kvg_, g_, mod_, b_ = p
        cos_, sin_ = c_[...], s_[...]
        lat = jnp.concatenate([l0[...], l1[...], l2[...]], axis=1)
        parts = []
        for h in range(RET_HEADS):
            parts.append(_rope_t(drq_[:, HP * h:HP * (h + 1)] + drq_[:, 512 + HP * h:512 + HP * (h + 1)], cos_, sin_))
        for h in range(RET_HEADS):
            t = drk_[:, HP * h:HP * (h + 1)] + drk_[:, 512 + HP * h:512 + HP * (h + 1)]
            parts.append(_rope_t(t, cos_, sin_) * (RET_QK_DIM ** -0.5))
        parts.append(drv_[:, :512] + drv_[:, 512:])
        parts += [drg_[...], dmg_[...]]
        _, vq = jax.vjp(_rms, lat[:, :Z_CKV - Z_CQ], qg_)
        dcq, dqg = vq(dqn_[...])
        _, vkv = jax.vjp(_rms, lat[:, Z_CKV - Z_CQ:Z_KR - Z_CQ], kvg_)
        dckv, dkvg = vkv(dkvn_[...])
        parts += [dcq, dckv, _rope_t(dkr_[...], cos_, sin_)]
        dz = jnp.concatenate([v.astype(BF16) for v in parts], axis=1)
        dh = lax.dot_general(dz, mats[0][...], NT_DIMS, preferred_element_type=F32)
        dx, dg, dsc, dsh = _pre_bwd_math(dh, x_[...], res_[...], g_, mod_, b_)
        return [dz, dx], [dqg, dkvg, dg, dsc, dsh]

    rows = [*_z_latents(z), cos, sin, drq, drk, drv, drg, dqn, dkvn, dkr, dmg, x, dx_res]
    return _rowcall("ev_in_bwd", fn, rows, [qg, kvg, g, mod, bias], [(Z_END, BF16), (D, F32)],
                    [(1, MLA_Q_RANK), (1, MLA_KV_RANK)] + [(1, D)] * 3, ts=256, mats=[w_in])


ATT_SCALE = (MLA_NOPE_DIM + MLA_ROPE_DIM) ** -0.5
LOG2E = 1.4426950408889634
LN2 = 0.6931471805599453


def _ev_mid_bwd(dQ, dK, dV, cos, sin, w_uq, w_ukv):
    def fn(rows, p, mats):
        dq_, dk_, dv_, c_, s_ = rows
        cos_, sin_ = c_[...], s_[...]
        dqs, dkvs = [], []
        dkr = None
        for h in range(MLA_HEADS):
            b = 2 * HP * h
            dqs += [dq_[:, b:b + HP] * ATT_SCALE, _rope_t(dq_[:, b + HP:b + 2 * HP], cos_, sin_) * ATT_SCALE]
            dkvs += [dk_[:, b:b + HP] * LN2, dv_[:, HP * h:HP * (h + 1)]]
            t = dk_[:, b + HP:b + 2 * HP]
            dkr = t if dkr is None else dkr + t
        dqf = jnp.concatenate(dqs, axis=1).astype(BF16)
        dkv = jnp.concatenate(dkvs, axis=1).astype(BF16)
        dqn = lax.dot_general(dqf, mats[0][...], NT_DIMS, preferred_element_type=F32)
        dkvn = lax.dot_general(dkv, mats[1][...], NT_DIMS, preferred_element_type=F32)
        return [dqf, dkv, dqn, dkvn, dkr * LN2], []

    return _rowcall("ev_mid_bwd", fn, [dQ, dK, dV, cos, sin], [],
                    [(1024, BF16), (1024, BF16), (MLA_Q_RANK, F32), (MLA_KV_RANK, F32), (HP, F32)], mats=[w_uq, w_ukv])


def _mix_math(o, rg, a, mg):
    outs = []
    for h in range(RET_HEADS):
        oh = o[:, HP * h:HP * (h + 1)]
        mu = jnp.mean(oh, axis=-1, keepdims=True)
        var = jnp.mean(jnp.square(oh - mu), axis=-1, keepdims=True)
        outs.append((oh - mu) * lax.rsqrt(var + EPS))
    ret = jnp.concatenate(outs, axis=1) * _silu(rg)
    return jnp.concatenate([ret, a * _silu(mg)], axis=1)


def _ev_out_fwd(o2, z, a, x, w_out, g, mod, bias):
    D = D_MODEL

    def fn(rows, p, mats):
        o_, rg_, mg_, a_, x_ = rows
        g_, mod_, b_ = p
        mix = _mix_math(o_[:, :512] + o_[:, 512:], rg_[...], a_[...], mg_[...]).astype(BF16)
        y = jnp.dot(mix, mats[0][...], preferred_element_type=F32)
        return [mix, y, _post_math(x_[...], y, g_, mod_[:, 2 * D:], b_[:, 2 * D:])], []

    return _rowcall("ev_out", fn, [o2, *_z_gates(z), a, x], [g, mod, bias], [(1024, BF16), (D, F32), (D, F32)], mats=[w_out])


def _post_bwd_math(y, dxo, g, mod, bias):
    D = D_MODEL
    b_gate = bias[:, 2 * D:]
    _, vjp = jax.vjp(lambda y_, g2, mg: _post_math(0.0, y_, g2, mg, b_gate), y, g, mod[:, 2 * D:])
    return vjp(dxo)


def _ev_out_bwd(y, dxo, o2, z, a, w_out, g, mod, bias):
    D = D_MODEL

    def fn(rows, p, mats):
        y_, dxo_, o_, rg_, mg_, a_ = rows
        g_, mod_, b_ = p
        dy, dg, dgate = _post_bwd_math(y_[...], dxo_[...], g_, mod_, b_)
        dyb = dy.astype(BF16)
        dmix = lax.dot_general(dyb, mats[0][...], NT_DIMS, preferred_element_type=F32)
        _, vjp = jax.vjp(_mix_math, o_[:, :512] + o_[:, 512:], rg_[...], a_[...], mg_[...])
        return [dyb, *vjp(dmix)], [dg, dgate]

    return _rowcall("ev_out_bwd", fn, [y, dxo, o2, *_z_gates(z), a], [g, mod, bias],
                    [(D, BF16), (512, BF16), (512, F32), (512, BF16), (512, F32)], [(1, D)] * 2, mats=[w_out])


def _glu_math(a, b, ba, bb):
    return (a + ba) * jax.nn.sigmoid(b + bb)


def _od_in_fwd(x, g, mod, bias, w_in, bin_):
    D = D_MODEL

    def fn(rows, p, mats):
        g_, mod_, b_, bi_ = p
        h = _pre_math(rows[0][...], g_, mod_[:, D:2 * D], b_[:, D:2 * D], mod_[:, :D], b_[:, :D]).astype(BF16)
        z = jnp.dot(h, mats[0][...], preferred_element_type=F32)
        return [h, z, _glu_math(z[:, :D], z[:, D:2 * D], bi_[:, :D], bi_[:, D:2 * D])], []

    return _rowcall("od_in", fn, [x], [g, mod, bias, bin_], [(D, BF16), (3 * D, F32), (D, F32)], mats=[w_in])


def _odmix_math(uc, g, bg, ln_g, ln_b):
    mu = jnp.mean(uc, axis=-1, keepdims=True)
    var = jnp.mean(jnp.square(uc - mu), axis=-1, keepdims=True)
    y = (uc - mu) * lax.rsqrt(var + EPS) * ln_g + ln_b
    return _silu(y) * _silu(g + bg)


def _od_out_fwd(uc, z, x, w_out, bin_, ln_g, ln_b, g, mod, bias, tgt=None):
    D = D_MODEL

    def fn(rows, p, mats):
        uc_, zg_, x_ = rows[:3]
        bi_, lg_, lb_, g_, mod_, b_ = p
        vv = _odmix_math(uc_[...], zg_[...], bi_[:, 2 * D:], lg_, lb_).astype(BF16)
        y = jnp.dot(vv, mats[0][...], preferred_element_type=F32)
        xn = _post_math(x_[...], y, g_, mod_[:, 2 * D:], b_[:, 2 * D:])
        if tgt is None:
            return [vv, y, xn], []
        err = xn - rows[3][...]
        part = 0.5 * jnp.sum(jnp.mean(err * err, axis=-1, keepdims=True), axis=0, keepdims=True)
        return [vv, y, err * (1.0 / D)], [jnp.broadcast_to(part, (1, LANES))]

    rows = [uc, (z, D, 2), x] + ([] if tgt is None else [tgt])
    return _rowcall("od_out" if tgt is None else "od_out_loss", fn, rows, [bin_, ln_g, ln_b, g, mod, bias],
                    [(D, BF16), (D, F32), (D, F32)], [] if tgt is None else [(1, LANES)], mats=[w_out])


def _od_out_bwd(y, dxo, uc, z, w_out, bin_, ln_g, ln_b, g, mod, bias):
    D = D_MODEL

    def fn(rows, p, mats):
        y_, dxo_, uc_, zg_ = rows
        bi_, lg_, lb_, g_, mod_, b_ = p
        dy, dpg, dgate = _post_bwd_math(y_[...], dxo_[...], g_, mod_, b_)
        dyb = dy.astype(BF16)
        dvv = lax.dot_general(dyb, mats[0][...], NT_DIMS, preferred_element_type=F32)
        _, vjp = jax.vjp(_odmix_math, uc_[...], zg_[...], bi_[:, 2 * D:], lg_, lb_)
        duc, dg, dbg, dlg, dlb = vjp(dvv)
        return [dyb, duc, dg], [dpg, dgate, dbg, dlg, dlb]

    return _rowcall("od_out_bwd", fn, [y, dxo, uc, (z, D, 2)], [bin_, ln_g, ln_b, g, mod, bias],
                    [(D, BF16), (D, F32), (D, BF16)], [(1, D)] * 5, mats=[w_out])


def _od_in_bwd(z, bin_, du, dg, x, dx_res, w_in, g, mod, bias):
    D = D_MODEL

    def fn(rows, p, mats):
        za_, zb_, du_, dg_, x_, res_ = rows
        bi_, g_, mod_, b_ = p
        _, vjp = jax.vjp(_glu_math, za_[...], zb_[...], bi_[:, :D], bi_[:, D:2 * D])
        da, db, dba, dbb = vjp(du_[...])
        dz = jnp.concatenate([da.astype(BF16), db.astype(BF16), dg_[...]], axis=1)
        dh = lax.dot_general(dz, mats[0][...], NT_DIMS, preferred_element_type=F32)
        dx, dpg, dsc, dsh = _pre_bwd_math(dh, x_[...], res_[...], g_, mod_, b_)
        return [dz, dx], [dba, dbb, dpg, dsc, dsh]

    return _rowcall("od_in_bwd", fn, [(z, D, 0), (z, D, 1), du, dg, x, dx_res], [bin_, g, mod, bias],
                    [(3 * D, BF16), (D, F32)], [(1, D)] * 5, ts=256, mats=[w_in])


def _tile(n, cap):
    if n <= cap:
        return n
    best = None
    for t in range(LANES, cap + 1, LANES):
        if n % t == 0:
            best = t
    assert best is not None, (n, cap)
    return best


MM_TN_CAP = 1792
NT_DIMS = (((1,), (1,)), ((), ()))
TN_DIMS = (((0,), (0,)), ((), ()))


def _mm_nn(name, a, b, out_dtype=F32, tm=512, tn_cap=MM_TN_CAP):
    M, K = a.shape
    N = b.shape[1]
    tm = min(tm, M)
    tn = _tile(N, tn_cap)

    def body(a_ref, b_ref, o_ref):
        o_ref[...] = jnp.dot(a_ref[...], b_ref[...], preferred_element_type=F32).astype(o_ref.dtype)

    nb = _nbytes((tm, K), a.dtype) + _nbytes((K, tn), b.dtype) + _nbytes((tm, tn), out_dtype) + _nbytes((tm, tn), F32)
    return pl.pallas_call(
        body, name=name, grid=(N // tn, M // tm),
        in_specs=[pl.BlockSpec((tm, K), lambda j, i: (i, 0)), pl.BlockSpec((K, tn), lambda j, i: (0, j))],
        out_specs=pl.BlockSpec((tm, tn), lambda j, i: (i, j)),
        out_shape=jax.ShapeDtypeStruct((M, N), out_dtype),
        compiler_params=_params(("parallel", "parallel"), nb),
    )(_hbm(a), _hbm(b))


def _pre_mm(name, x, g, mod, bias, w, tm=512):
    S, D = x.shape
    N = w.shape[1]
    tm = min(tm, S)

    def body(x_ref, g_ref, mod_ref, bias_ref, w_ref, h_ref, z_ref):
        mod_, b_ = mod_ref[...], bias_ref[...]
        h = _pre_math(x_ref[...], g_ref[...], mod_[:, D:2 * D], b_[:, D:2 * D], mod_[:, :D], b_[:, :D]).astype(BF16)
        h_ref[...] = h
        z_ref[...] = jnp.dot(h, w_ref[...], preferred_element_type=F32)

    nb = 2 * _nbytes((tm, D), F32) + _nbytes((D, N), w.dtype) + 2 * _nbytes((tm, N), F32)
    row = lambda i: (i, 0)
    whole = lambda i: (0, 0)
    return pl.pallas_call(
        body, name=name, grid=(S // tm,),
        in_specs=[pl.BlockSpec((tm, D), row), pl.BlockSpec(g.shape, whole), pl.BlockSpec(mod.shape, whole),
                  pl.BlockSpec(bias.shape, whole), pl.BlockSpec((D, N), whole)],
        out_specs=[pl.BlockSpec((tm, D), row), pl.BlockSpec((tm, N), row)],
        out_shape=[jax.ShapeDtypeStruct((S, D), BF16), jax.ShapeDtypeStruct((S, N), F32)],
        compiler_params=_params(("parallel",), nb),
    )(_hbm(x), g, mod, bias, _hbm(w))


def _mm_tn(name, a, b, out_dtype=F32, ts=1024, tm_cap=512, tn_cap=MM_TN_CAP):
    S, M = a.shape
    N = b.shape[1]
    ts = min(ts, S)
    tm = _tile(M, tm_cap)
    tn = _tile(N, tn_cap)
    ns = S // ts

    def body(a_ref, b_ref, o_ref, acc):
        s = pl.program_id(2)

        @pl.when(s == 0)
        def _():
            acc[...] = jnp.zeros(acc.shape, F32)

        acc[...] += lax.dot_general(a_ref[...], b_ref[...], TN_DIMS, preferred_element_type=F32)

        @pl.when(s == ns - 1)
        def _():
            o_ref[...] = acc[...].astype(o_ref.dtype)

    nb = _nbytes((ts, tm), a.dtype) + _nbytes((ts, tn), b.dtype) + 3 * _nbytes((tm, tn), F32)
    return pl.pallas_call(
        body, name=name, grid=(M // tm, N // tn, ns),
        in_specs=[pl.BlockSpec((ts, tm), lambda i, j, s: (s, i)), pl.BlockSpec((ts, tn), lambda i, j, s: (s, j))],
        out_specs=pl.BlockSpec((tm, tn), lambda i, j, s: (i, j)),
        out_shape=jax.ShapeDtypeStruct((M, N), out_dtype),
        scratch_shapes=[pltpu.VMEM((tm, tn), F32)],
        compiler_params=_params(("parallel", "parallel", "arbitrary"), nb),
    )(_hbm(a), _hbm(b))


ATT_TQ = 1024
ATT_TK = 1024
ATT_TK_FWD = 1024
QW = 2 * HP


def _kv_tiles(V, tk):
    S = V.shape[0]
    return V.reshape(S // tk, tk, MLA_HEADS, HP).transpose(2, 0, 3, 1)


class _Side:
    def __init__(self, side):
        self.on = side is not None
        if self.on:
            self.p, self.mode = side
            shp = self.p.shape
            self.h, self.w = (shp[0] // 2, shp[1]) if self.mode == "half" else (shp[-2], shp[-1])
        self.in_specs = [_HBM] if self.on else []
        self.out_specs = [_HBM] if self.on else []
        self.out_shape = [jax.ShapeDtypeStruct((4, self.h, self.w), self.p.dtype)] if self.on else []
        self.scratch = list(_CHIP_SEMS) if self.on else []
        self.operands = [_hbm(self.p)] if self.on else []

    def copies(self, p_ref, x_ref, sems):
        return _chip_copies(p_ref, x_ref, *sems, self.mode, self.h)

    def start(self, p_ref, x_ref, sems, first):
        @pl.when(first)
        def _():
            for cp in self.copies(p_ref, x_ref, sems):
                cp.start()

    def finish(self, p_ref, x_ref, sems, last):
        @pl.when(last)
        def _():
            for cp in self.copies(p_ref, x_ref, sems):
                cp.wait()


def _flash_fwd(Q, K, V, side=None):
    S = Q.shape[0]
    H = MLA_HEADS
    tq, tk = min(ATT_TQ, S), min(ATT_TK_FWD, S)
    nk = S // tk
    VT = _kv_tiles(V, tk)
    sd = _Side(side)

    def body(*refs):
        if sd.on:
            q_ref, k_ref, vt_ref, p_ref, o_ref, lse_ref, x_ref, m_s, l_s, acc_s, s_a, s_b, *sems = refs
            step_id = pl.program_id(0) * (S // tq) + pl.program_id(1)
            sd.start(p_ref, x_ref, sems, step_id == 0)
        else:
            q_ref, k_ref, vt_ref, o_ref, lse_ref, m_s, l_s, acc_s, s_a, s_b = refs
        m_s[...] = jnp.full(m_s.shape, -jnp.inf, F32)
        l_s[...] = jnp.zeros(l_s.shape, F32)
        acc_s[...] = jnp.zeros(acc_s.shape, F32)

        def scores(j):
            k = k_ref[pl.ds(pl.multiple_of(j * tk, tk), tk), :]
            return lax.dot_general(k, q_ref[...], NT_DIMS, preferred_element_type=F32)

        def consume(st, j):
            m_prev = m_s[...]
            m_new = jnp.maximum(m_prev, jnp.max(st, axis=0, keepdims=True))
            alpha = jnp.exp2(m_prev - m_new)
            pt = jnp.exp2(st - m_new)
            l_s[...] = alpha * l_s[...] + jnp.sum(pt, axis=0, keepdims=True)
            acc_s[...] = alpha * acc_s[...] + jnp.dot(vt_ref[0, j], pt.astype(BF16), preferred_element_type=F32)
            m_s[...] = m_new

        if nk % 2:
            def step(j, carry):
                consume(scores(j), j)
                return carry

            lax.fori_loop(0, nk, step, 0)
        else:
            s_a[...] = scores(0)

            def pair(jj, carry):
                j0 = 2 * jj
                s_b[...] = scores(j0 + 1)
                consume(s_a[...], j0)
                s_a[...] = scores(jnp.minimum(j0 + 2, nk - 1))
                consume(s_b[...], j0 + 1)
                return carry

            lax.fori_loop(0, nk // 2, pair, 0)
        l = l_s[...]
        o_ref[...] = (acc_s[...] * (1.0 / l)).T
        lse_ref[0] = m_s[...] + jnp.log2(l)
        if sd.on:
            sd.finish(p_ref, x_ref, sems, step_id == H * (S // tq) - 1)

    nb = (_nbytes((tq, QW), BF16) + _nbytes((S, QW), BF16) + _nbytes((S, HP), BF16) + 3 * _nbytes((tq, HP), F32)
          + 4 * _nbytes((tq, tk), F32))
    return pl.pallas_call(
        body, name="flash_fwd_side" if sd.on else "flash_fwd", grid=(H, S // tq),
        in_specs=[pl.BlockSpec((tq, QW), lambda h, i: (i, h)), pl.BlockSpec((S, QW), lambda h, i: (0, h)),
                  pl.BlockSpec((1, nk, HP, tk), lambda h, i: (h, 0, 0, 0))] + sd.in_specs,
        out_specs=[pl.BlockSpec((tq, HP), lambda h, i: (i, h)), pl.BlockSpec((1, 1, tq), lambda h, i: (h, 0, i))] + sd.out_specs,
        out_shape=[jax.ShapeDtypeStruct((S, H * HP), F32), jax.ShapeDtypeStruct((H, 1, S), F32)] + sd.out_shape,
        scratch_shapes=[pltpu.VMEM((1, tq), F32), pltpu.VMEM((1, tq), F32), pltpu.VMEM((HP, tq), F32),
                        pltpu.VMEM((tk, tq), F32), pltpu.VMEM((tk, tq), F32)] + sd.scratch,
        compiler_params=_params(("arbitrary", "arbitrary") if sd.on else ("parallel", "parallel"), nb),
    )(_hbm(Q), _hbm(K), _hbm(VT), *sd.operands)


def _flash_bwd(Q, K, V, dO, O, lse, side=None):
    S = Q.shape[0]
    H = MLA_HEADS
    tq, tk = min(ATT_TQ, S), min(ATT_TK, S)
    nk = S // tk
    sd = _Side(side)

    def body(*refs):
        if sd.on:
            q_ref, do_ref, o_ref, lse_ref, k_ref, v_ref, p_ref, dq_ref, dk_ref, dv_ref, x_ref, dq_s, *sems = refs
            step_id = pl.program_id(0) * (S // tq) + pl.program_id(1)
            sd.start(p_ref, x_ref, sems, step_id == 0)
        else:
            q_ref, do_ref, o_ref, lse_ref, k_ref, v_ref, dq_ref, dk_ref, dv_ref, dq_s = refs

        @pl.when(pl.program_id(1) == 0)
        def _():
            dk_ref[...] = jnp.zeros(dk_ref.shape, F32)
            dv_ref[...] = jnp.zeros(dv_ref.shape, F32)

        delta = jnp.sum((do_ref[...].astype(F32) * o_ref[...]).T, axis=0, keepdims=True)
        lse = lse_ref[0]
        dq_s[...] = jnp.zeros(dq_s.shape, F32)

        def rows(j):
            return pl.ds(pl.multiple_of(j * tk, tk), tk)

        def scores(j):
            st = lax.dot_general(k_ref[rows(j), :], q_ref[...], NT_DIMS, preferred_element_type=F32)
            dpt = lax.dot_general(v_ref[rows(j), :], do_ref[...], NT_DIMS, preferred_element_type=F32)
            return st, dpt

        def consume(st, dpt, j):
            pt = jnp.exp2(st - lse)
            dst = (pt * (dpt - delta)).astype(BF16)
            dv_ref[rows(j), :] += jnp.dot(pt.astype(BF16), do_ref[...], preferred_element_type=F32)
            dk_ref[rows(j), :] += jnp.dot(dst, q_ref[...], preferred_element_type=F32)
            dq_s[...] += lax.dot_general(dst, k_ref[rows(j), :], TN_DIMS, preferred_element_type=F32)

        def step(j, carry):
            consume(*scores(j), j)
            return carry

        lax.fori_loop(0, nk, step, 0)
        dq_ref[...] = dq_s[...]
        if sd.on:
            sd.finish(p_ref, x_ref, sems, step_id == H * (S // tq) - 1)

    nb = (_nbytes((tq, QW), BF16) + _nbytes((tq, HP), BF16) + _nbytes((tq, HP), F32) + _nbytes((S, QW), BF16)
          + _nbytes((S, HP), BF16) + 2 * _nbytes((tq, QW), F32) + _nbytes((S, QW), F32) + _nbytes((S, HP), F32))
    return pl.pallas_call(
        body, name="flash_bwd_side" if sd.on else "flash_bwd", grid=(H, S // tq),
        in_specs=[pl.BlockSpec((tq, QW), lambda h, i: (i, h)), pl.BlockSpec((tq, HP), lambda h, i: (i, h)),
                  pl.BlockSpec((tq, HP), lambda h, i: (i, h)), pl.BlockSpec((1, 1, tq), lambda h, i: (h, 0, i)),
                  pl.BlockSpec((S, QW), lambda h, i: (0, h)), pl.BlockSpec((S, HP), lambda h, i: (0, h))] + sd.in_specs,
        out_specs=[pl.BlockSpec((tq, QW), lambda h, i: (i, h)), pl.BlockSpec((S, QW), lambda h, i: (0, h)),
                   pl.BlockSpec((S, HP), lambda h, i: (0, h))] + sd.out_specs,
        out_shape=[jax.ShapeDtypeStruct((S, H * QW), F32), jax.ShapeDtypeStruct((S, H * QW), F32),
                   jax.ShapeDtypeStruct((S, H * HP), F32)] + sd.out_shape,
        scratch_shapes=[pltpu.VMEM((tq, QW), F32)] + sd.scratch,
        compiler_params=_params(("arbitrary", "arbitrary") if sd.on else ("parallel", "arbitrary"), nb),
    )(_hbm(Q), _hbm(dO), _hbm(O), lse, _hbm(K), _hbm(V), *sd.operands)


RET_CHUNK = 256


def _ret_tables(d, lg_ref, h, C):
    ii = lax.broadcasted_iota(jnp.int32, (C, C), 0).astype(F32)
    jj = lax.broadcasted_iota(jnp.int32, (C, C), 1).astype(F32)
    ci = lax.broadcasted_iota(jnp.int32, (C, 1), 0).astype(F32)
    fwd = d == 0
    dist = jnp.where(fwd, ii - jj, jj - ii)
    mask = dist >= jnp.where(fwd, 0.0, 1.0)
    dist = jnp.maximum(dist, 0.0)
    qpos = jnp.where(fwd, ci + 1.0, C - ci)
    kpos = jnp.where(fwd, C - 1.0 - ci, ci)
    lg = lg_ref[d, h]
    D = jnp.where(mask, jnp.exp(lg * dist), 0.0)
    return D, jnp.exp(lg * qpos), jnp.exp(lg * kpos), jnp.exp(lg * C), dist, qpos, kpos


def _ret_fwd(lg, q, k, v):
    S = q.shape[0]
    C = min(RET_CHUNK, S)
    N = S // C
    H = RET_HEADS

    def chunk(d, n):
        return jnp.where(d == 0, n, N - 1 - n)

    def body(lg_ref, q_ref, k_ref, v_ref, o_ref, st_ref, state, tab_d, tab_q, tab_k):
        d, n = pl.program_id(0), pl.program_id(1)

        @pl.when(n == 0)
        def _():
            state[...] = jnp.zeros_like(state)
            for h in range(H):
                D, qw, kw, _, _, _, _ = _ret_tables(d, lg_ref, h, C)
                tab_d[h] = D
                tab_q[h] = jnp.broadcast_to(qw, (C, HP))
                tab_k[h] = jnp.broadcast_to(kw, (C, HP))

        for h in range(H):
            sl = slice(HP * h, HP * (h + 1))
            D, qw, kw = tab_d[h], tab_q[h], tab_k[h]
            gc = jnp.exp(lg_ref[d, h] * C)
            qh, kh, vh = q_ref[:, sl], k_ref[:, sl], v_ref[:, sl]
            st = state[sl, :]
            sm = lax.dot_general(qh, kh, (((1,), (1,)), ((), ())), preferred_element_type=F32) * D
            inner = jnp.dot(sm.astype(BF16), vh, preferred_element_type=F32)
            cross = qw * jnp.dot(qh, st.astype(BF16), preferred_element_type=F32)
            o_ref[:, sl] = inner + cross
            st_ref[0, 0, sl, :] = st
            kvn = lax.dot_general((kh.astype(F32) * kw).astype(BF16), vh, (((0,), (0,)), ((), ())), preferred_element_type=F32)
            state[sl, :] = gc * st + kvn

    nb = 3 * _nbytes((C, 512), BF16) + _nbytes((C, 512), F32) + 2 * _nbytes((512, HP), F32) + 8 * _nbytes((C, C), F32)
    return pl.pallas_call(
        body, name="ret_fwd", grid=(2, N),
        in_specs=[pl.BlockSpec(memory_space=pltpu.SMEM)] + [pl.BlockSpec((C, 512), lambda d, n: (chunk(d, n), 0))] * 3,
        out_specs=[pl.BlockSpec((C, 512), lambda d, n: (chunk(d, n), d)),
                   pl.BlockSpec((1, 1, 512, HP), lambda d, n: (d, chunk(d, n), 0, 0))],
        out_shape=[jax.ShapeDtypeStruct((S, 1024), F32), jax.ShapeDtypeStruct((2, N, 512, HP), F32)],
        scratch_shapes=[pltpu.VMEM((512, HP), F32), pltpu.VMEM((H, C, C), F32), pltpu.VMEM((H, C, HP), F32),
                        pltpu.VMEM((H, C, HP), F32)],
        compiler_params=_params(("arbitrary", "arbitrary"), nb),
    )(lg, _hbm(q), _hbm(k), _hbm(v))


def _ret_bwd(lg, q, k, v, do, states):
    S = q.shape[0]
    C = min(RET_CHUNK, S)
    N = S // C
    H = RET_HEADS

    def chunk(d, n):
        return jnp.where(d == 0, N - 1 - n, n)

    def body(lg_ref, q_ref, k_ref, v_ref, do_ref, st_ref, dq_ref, dk_ref, dv_ref, dlg_ref, G, accA, accB, accC,
             tab_d, tab_q, tab_k, tab_qp, tab_kp):
        d, n = pl.program_id(0), pl.program_id(1)

        @pl.when(n == 0)
        def _():
            G[...] = jnp.zeros_like(G)
            accA[...] = jnp.zeros_like(accA)
            accB[...] = jnp.zeros_like(accB)
            accC[...] = jnp.zeros_like(accC)
            for h in range(H):
                D, qw, kw, _, _, qpos, kpos = _ret_tables(d, lg_ref, h, C)
                tab_d[h] = D
                tab_q[h] = jnp.broadcast_to(qw, (C, HP))
                tab_k[h] = jnp.broadcast_to(kw, (C, HP))
                tab_qp[h] = jnp.broadcast_to(qw * qpos, (C, HP))
                tab_kp[h] = jnp.broadcast_to(kw * kpos, (C, HP))

        nt = (((1,), (1,)), ((), ()))
        tn = (((0,), (0,)), ((), ()))
        for h in range(H):
            sl = slice(HP * h, HP * (h + 1))
            D, qw, kw = tab_d[h], tab_q[h], tab_k[h]
            gc = jnp.exp(lg_ref[d, h] * C)
            qh, kh, vh, doh = q_ref[:, sl], k_ref[:, sl], v_ref[:, sl], do_ref[:, sl]
            st = st_ref[0, 0, sl, :]
            g = G[sl, :]
            stb, gb = st.astype(BF16), g.astype(BF16)
            sraw = lax.dot_general(qh, kh, nt, preferred_element_type=F32)
            dS = lax.dot_general(doh, vh, nt, preferred_element_type=F32) * D
            dSb = dS.astype(BF16)
            smb = (sraw * D).astype(BF16)
            qs = jnp.dot(qh, stb, preferred_element_type=F32)
            kg = jnp.dot(kh, gb, preferred_element_type=F32)
            dq_ref[:, sl] = jnp.dot(dSb, kh, preferred_element_type=F32) + qw * lax.dot_general(doh, stb, nt, preferred_element_type=F32)
            dk_ref[:, sl] = lax.dot_general(dSb, qh, tn, preferred_element_type=F32) + kw * lax.dot_general(vh, gb, nt, preferred_element_type=F32)
            dv_ref[:, sl] = lax.dot_general(smb, doh, tn, preferred_element_type=F32) + kw * kg
            dof, vf = doh.astype(F32), vh.astype(F32)
            accA[h] += sraw * dS
            accB[h] += tab_qp[h] * (qs * dof) + tab_kp[h] * (kg * vf)
            accC[h] += st * g
            G[sl, :] = gc * g + lax.dot_general((qh.astype(F32) * qw).astype(BF16), doh, tn, preferred_element_type=F32)

        @pl.when(n == N - 1)
        def _():
            rows = []
            for h in range(H):
                _, _, _, gc, dist, _, _ = _ret_tables(d, lg_ref, h, C)
                tot = jnp.sum(accA[h] * dist) + jnp.sum(accB[h]) + (C * gc) * jnp.sum(accC[h])
                rows.append(jnp.broadcast_to(tot, (1, HP)))
            dlg_ref[0] = jnp.concatenate(rows + [jnp.zeros((8 - H, HP), F32)], axis=0)

    nb = (4 * _nbytes((C, 512), BF16) + 3 * _nbytes((C, 512), F32) + 2 * _nbytes((512, HP), F32)
          + H * (_nbytes((C, C), F32) + _nbytes((C, HP), F32) + _nbytes((HP, HP), F32)) + 12 * _nbytes((C, C), F32))
    row = lambda d, n: (chunk(d, n), 0)
    out3 = lambda d, n: (chunk(d, n), d)
    return pl.pallas_call(
        body, name="ret_bwd", grid=(2, N),
        in_specs=[pl.BlockSpec(memory_space=pltpu.SMEM)] + [pl.BlockSpec((C, 512), row)] * 4
        + [pl.BlockSpec((1, 1, 512, HP), lambda d, n: (d, chunk(d, n), 0, 0))],
        out_specs=[pl.BlockSpec((C, 512), out3)] * 3 + [pl.BlockSpec((1, 8, HP), lambda d, n: (d, 0, 0))],
        out_shape=[jax.ShapeDtypeStruct((S, 1024), F32)] * 3 + [jax.ShapeDtypeStruct((2, 8, HP), F32)],
        scratch_shapes=[pltpu.VMEM((512, HP), F32), pltpu.VMEM((H, C, C), F32), pltpu.VMEM((H, C, HP), F32),
                        pltpu.VMEM((H, HP, HP), F32), pltpu.VMEM((H, C, C), F32)] + [pltpu.VMEM((H, C, HP), F32)] * 4,
        compiler_params=_params(("arbitrary", "arbitrary"), nb),
    )(lg, _hbm(q), _hbm(k), _hbm(v), _hbm(do), _hbm(states))


CONV_PAD = 16
CONV_TR = 256
CONV_CB = LANES


def _fill_padded(pad, u_ref, S):
    pad[0:CONV_PAD, :] = jnp.zeros((CONV_PAD, CONV_CB), F32)
    pad[CONV_PAD + S:CONV_PAD + S + CONV_PAD, :] = jnp.zeros((CONV_PAD, CONV_CB), F32)
    pad[CONV_PAD:CONV_PAD + S, :] = u_ref[...]


def _conv_fwd(u, w32, b):
    S, D = u.shape
    tr = min(CONV_TR, S)

    def body(u_ref, w_ref, b_ref, o_ref, pad):
        _fill_padded(pad, u_ref, S)
        wv = w_ref[...]
        bv = b_ref[...]

        def step(t, carry):
            r0 = pl.multiple_of(t * tr, tr)
            acc = jnp.broadcast_to(bv, (tr, CONV_CB))
            for k in range(CONV_KERNEL):
                acc = acc + pad[pl.ds(r0 + (k + 1), tr), :] * wv[k:k + 1, :]
            o_ref[pl.ds(r0, tr), :] = acc
            return carry

        lax.fori_loop(0, S // tr, step, 0)

    nb = 2 * _nbytes((S, CONV_CB), F32) + _nbytes((S + 2 * CONV_PAD, CONV_CB), F32)
    return pl.pallas_call(
        body, name="conv_fwd", grid=(D // CONV_CB,),
        in_specs=[pl.BlockSpec((S, CONV_CB), lambda j: (0, j)), pl.BlockSpec((32, CONV_CB), lambda j: (0, j)),
                  pl.BlockSpec((1, CONV_CB), lambda j: (0, j))],
        out_specs=pl.BlockSpec((S, CONV_CB), lambda j: (0, j)),
        out_shape=jax.ShapeDtypeStruct((S, D), F32),
        scratch_shapes=[pltpu.VMEM((S + 2 * CONV_PAD, CONV_CB), F32)],
        compiler_params=_params(("parallel",), nb),
    )(_hbm(u), w32, b)


def _conv_bwd_w(u, dout):
    S, D = u.shape
    tr = min(CONV_TR, S)

    def body(u_ref, d_ref, o_ref, pad, acc):
        _fill_padded(pad, u_ref, S)
        acc[...] = jnp.zeros_like(acc)

        def fold(a):
            return jnp.sum(a.reshape(tr // 8, 8, CONV_CB), axis=0)

        def step(t, carry):
            r0 = pl.multiple_of(t * tr, tr)
            dv = d_ref[pl.ds(r0, tr), :]
            for k in range(CONV_KERNEL):
                acc[8 * k:8 * k + 8, :] += fold(pad[pl.ds(r0 + (k + 1), tr), :] * dv)
            acc[8 * CONV_KERNEL:8 * CONV_KERNEL + 8, :] += fold(dv)
            return carry

        lax.fori_loop(0, S // tr, step, 0)
        o_ref[...] = jnp.sum(acc[...].reshape(32, 8, CONV_CB), axis=1)

    nb = 2 * _nbytes((S, CONV_CB), F32) + _nbytes((S + 2 * CONV_PAD, CONV_CB), F32)
    return pl.pallas_call(
        body, name="conv_bwd_w", grid=(D // CONV_CB,),
        in_specs=[pl.BlockSpec((S, CONV_CB), lambda j: (0, j)), pl.BlockSpec((S, CONV_CB), lambda j: (0, j))],
        out_specs=pl.BlockSpec((32, CONV_CB), lambda j: (0, j)),
        out_shape=jax.ShapeDtypeStruct((32, D), F32),
        scratch_shapes=[pltpu.VMEM((S + 2 * CONV_PAD, CONV_CB), F32), pltpu.VMEM((256, CONV_CB), F32)],
        compiler_params=_params(("parallel",), nb),
    )(_hbm(u), _hbm(dout))


def _adamw(w, g, m, v):
    shape = w.shape
    w2, g2, m2, v2 = [a.reshape(-1, shape[-1]) for a in (w, g, m, v)]
    W = shape[-1]

    def fn(rows, p):
        w_, g_, m_, v_ = [r[...] for r in rows]
        mn = ADAM_B1 * m_ + (1.0 - ADAM_B1) * g_
        vn = ADAM_B2 * v_ + (1.0 - ADAM_B2) * jnp.square(g_)
        m_hat = mn / (1.0 - ADAM_B1 ** ADAM_STEP)
        v_hat = vn / (1.0 - ADAM_B2 ** ADAM_STEP)
        delta = -ADAM_LR * (m_hat / (jnp.sqrt(v_hat) + ADAM_EPS) + ADAM_WD * w_)
        return [delta, mn, vn], []

    R2 = w2.shape[0]
    ts = 256 if (R2 > 256 and R2 % 256 == 0) else R2
    d, mn, vn = _rowcall("adamw", fn, [w2, g2, m2, v2], [], [(W, F32)] * 3, ts=ts)
    return d.reshape(shape), mn.reshape(shape), vn.reshape(shape)


_HBM = pl.BlockSpec(memory_space=pltpu.HBM)


def _place():
    return lax.axis_index("x"), lax.axis_index("y"), lax.axis_index("c")


def _half_rows(ref, half, h):
    idx = (slice(None),) * (len(ref.shape) - 2) + (pl.ds(half * h, h), slice(None))
    return ref.at[idx]


def _chip_exchange(name, p, mode):
    assert mode in ("half", "whole", "scatter")
    if mode == "half":
        h, W = p.shape[0] // 2, p.shape[1]
    elif mode == "whole":
        h, W = p.shape
    else:
        h, W = p.shape[1], p.shape[2]

    def body(p_ref, o_ref, send_sems, recv_sems, local_sem):
        copies = _chip_copies(p_ref, o_ref, send_sems, recv_sems, local_sem, mode, h)
        for cp in copies:
            cp.start()
        for cp in copies:
            cp.wait()

    return pl.pallas_call(
        body, name=name, in_specs=[_HBM], out_specs=_HBM,
        out_shape=jax.ShapeDtypeStruct((4, h, W), p.dtype), scratch_shapes=_CHIP_SEMS,
    )(p)


_CHIP_SEMS = [pltpu.SemaphoreType.DMA((3,)), pltpu.SemaphoreType.DMA((3,)), pltpu.SemaphoreType.DMA]


def _chip_copies(p_ref, o_ref, send_sems, recv_sems, local_sem, mode, h):
    x, y, c = _place()
    k_me = 2 * x + y

    def src(k):
        if mode == "half":
            return _half_rows(p_ref, c, h)
        return p_ref if mode == "whole" else p_ref.at[k]

    copies = [pltpu.make_async_copy(src(k_me), o_ref.at[k_me], local_sem)]
    for j, (cx, cy) in enumerate([(1 - x, y), (x, 1 - y), (1 - x, 1 - y)]):
        copies.append(pltpu.make_async_remote_copy(
            src_ref=src(2 * cx + cy), dst_ref=o_ref.at[k_me], send_sem=send_sems.at[j], recv_sem=recv_sems.at[j],
            device_id=(cx, cy, c), device_id_type=MESH))
    return copies


def _sibling_send_half(name, g):
    n, R, W = g.shape
    h = R // 2

    def body(g_ref, o_ref, send_sem, recv_sem):
        x, y, c = _place()
        cp = pltpu.make_async_remote_copy(
            src_ref=_half_rows(g_ref, 1 - c, h), dst_ref=o_ref, send_sem=send_sem, recv_sem=recv_sem,
            device_id=(x, y, 1 - c), device_id_type=MESH)
        cp.start()
        cp.wait()

    return pl.pallas_call(
        body, name=name, in_specs=[_HBM], out_specs=_HBM,
        out_shape=jax.ShapeDtypeStruct((n, h, W), g.dtype),
        scratch_shapes=[pltpu.SemaphoreType.DMA, pltpu.SemaphoreType.DMA],
    )(g)


def _halves_merge(name, mine):
    def body(m_ref, o_ref, send_sem, recv_sem):
        x, y, c = _place()
        cp = pltpu.make_async_remote_copy(
            src_ref=m_ref, dst_ref=o_ref, send_sem=send_sem, recv_sem=recv_sem,
            device_id=(x, y, 1 - c), device_id_type=MESH)
        cp.start()
        cp.wait()

    theirs = pl.pallas_call(
        body, name=name, in_specs=[_HBM], out_specs=_HBM,
        out_shape=jax.ShapeDtypeStruct(mine.shape, mine.dtype),
        scratch_shapes=[pltpu.SemaphoreType.DMA, pltpu.SemaphoreType.DMA],
    )(mine)
    south = lax.axis_index("c") == 0
    axis = mine.ndim - 2
    return jnp.concatenate([jnp.where(south, mine, theirs), jnp.where(south, theirs, mine)], axis=axis)


def _add2(name, a, b, out_dtype):
    def fn(rows, p):
        return [rows[0][...].astype(F32) + rows[1][...].astype(F32)], []

    return _rowcall(name, fn, [a, b], [], [(a.shape[1], out_dtype)], ts=4 * PACK_TILE)[0]


def _sum4(name, b):
    _, h, W = b.shape
    ts = 2 * PACK_TILE if h % (2 * PACK_TILE) == 0 else (PACK_TILE if h % PACK_TILE == 0 else h)

    def body(b_ref, o_ref):
        o_ref[...] = ((b_ref[0].astype(F32) + b_ref[1].astype(F32)) + b_ref[2].astype(F32)) + b_ref[3].astype(F32)

    nb = _nbytes((4, ts, W), b.dtype) + _nbytes((ts, W), F32)
    return pl.pallas_call(
        body, name=name, grid=(h // ts,),
        in_specs=[pl.BlockSpec((4, ts, W), lambda i: (0, i, 0))], out_specs=pl.BlockSpec((ts, W), lambda i: (i, 0)),
        out_shape=jax.ShapeDtypeStruct((h, W), F32), compiler_params=_params(("parallel",), nb),
    )(b)


PACK_W = 1024
PACK_ROWS = 16
PACK_TILE = 128
EV_SHARDED = (("ev_w_in", "col", False), ("ev_w_uq", "col", False), ("ev_w_ukv", "col", False), ("ev_w_out", "row", False))
OD_SHARDED = (("od_w_in", "col", False), ("od_b_in", "col", True), ("od_dw_w", "col", True), ("od_dw_b", "col", True),
              ("od_ln_g", "col", True), ("od_ln_b", "col", True), ("od_w_out", "row", False))
REPLICATED =("ada_b", "pre_g", "post_g", "ev_dec_f", "ev_dec_b", "ev_q_norm_g", "ev_kv_norm_g")
WEIGHTS = ("ada_w", "ada_b", "pre_g", "post_g", "ev_w_in", "ev_dec_f", "ev_dec_b", "ev_q_norm_g", "ev_w_uq", "ev_kv_norm_g",
           "ev_w_ukv", "ev_w_out", "od_w_in", "od_b_in", "od_dw_w", "od_dw_b", "od_ln_g", "od_ln_b", "od_w_out")


def _layer_entries(layers):
    out = []
    for l in layers:
        out += [(name, how, exact, l // 2) for name, how, exact in (EV_SHARDED if l % 2 == 0 else OD_SHARDED)]
    return out


def _rows_of(n):
    unit = PACK_W * PACK_ROWS
    return (n + unit - 1) // unit * PACK_ROWS


def _to_rows(flat, lead):
    n = flat.shape[-1]
    r = _rows_of(n)
    flat = jnp.pad(flat, [(0, 0)] * len(lead) + [(0, r * PACK_W - n)])
    return flat.reshape(*lead, r, PACK_W)


def _pad_total(buf, axis):
    r = buf.shape[axis]
    r2 = (r + 2 * PACK_TILE - 1) // (2 * PACK_TILE) * (2 * PACK_TILE)
    pads = [(0, 0)] * buf.ndim
    pads[axis] = (0, r2 - r)
    return jnp.pad(buf, pads)


def _split_chips(full, how):
    if how == "col":
        n = full.shape[-1] // 4
        return jnp.moveaxis(full.reshape(*full.shape[:-1], 4, n), -2, 0)
    n = full.shape[-2] // 4
    return jnp.moveaxis(full.reshape(*full.shape[:-2], 4, n, full.shape[-1]), -3, 0)


def _join_chips(blocks, how):
    if how == "col":
        t = jnp.moveaxis(blocks, 0, -2)
        return t.reshape(*t.shape[:-2], t.shape[-2] * t.shape[-1])
    t = jnp.moveaxis(blocks, 0, -3)
    return t.reshape(*t.shape[:-3], t.shape[-3] * t.shape[-2], t.shape[-1])


def _pack_weights(w, entries):
    parts = []
    for name, _, exact, idx in entries:
        a = w[name][idx].reshape(-1)
        a = lax.bitcast_convert_type(a, BF16).reshape(-1) if exact else a.astype(BF16)
        parts.append(_to_rows(a, ()))
    return _pad_total(jnp.concatenate(parts, axis=0), 0)


def _unpack_weights(gathered, w, entries):
    out, r0 = {}, 0
    for name, how, exact, idx in entries:
        shp = w[name].shape[1:]
        n = (2 if exact else 1) * w[name][idx].size
        r = _rows_of(n)
        a = gathered[:, r0:r0 + r].reshape(4, -1)[:, :n]
        if exact:
            a = lax.bitcast_convert_type(a.reshape(4, -1, 2), F32)
        out[(name, idx)] = _join_chips(a.reshape(4, *shp), how)
        r0 += r
    return out


def _pack_grads(grads, entries, replicated):
    parts = []
    for name, how, _, idx in entries:
        b = _split_chips(grads[(name, idx)], how)
        parts.append(_to_rows(b.reshape(4, -1).astype(BF16), (4,)))
    for name in replicated:
        a = _to_rows(grads[name].reshape(-1).astype(BF16), ())
        parts.append(jnp.broadcast_to(a[None], (4, *a.shape)))
    return _pad_total(jnp.concatenate(parts, axis=1), 1)


def _reduce_begin(tag, g):
    h = g.shape[1] // 2
    c = lax.axis_index("c")
    theirs = _sibling_send_half("reduce_cores_" + tag, g)
    mine = lax.dynamic_slice_in_dim(g, c * h, h, axis=1)
    return _add2("reduce_add_" + tag, mine.reshape(4 * h, PACK_W), theirs.reshape(4 * h, PACK_W), BF16).reshape(4, h, PACK_W)


def _reduce_end(tag, exchanged):
    return _halves_merge("reduce_merge_" + tag, _sum4("reduce_sum_" + tag, exchanged))


def _unpack_grads(total, w, entries, replicated):
    out, r0 = {}, 0
    for name, _, _, idx in entries:
        n = w[name][idx].size
        r = _rows_of(n)
        out[(name, idx)] = total[r0:r0 + r].reshape(-1)[:n].reshape(w[name].shape[1:])
        r0 += r
    for name in replicated:
        n = w[name].size
        r = _rows_of(n)
        out[name] = total[r0:r0 + r].reshape(-1)[:n].reshape(w[name].shape)
        r0 += r
    return out


def _pad_heads(a, n, w, to):
    lead = a.shape[:-1]
    return jnp.pad(a.reshape(*lead, n, w), [(0, 0)] * (len(lead) + 1) + [(0, to - w)]).reshape(*lead, n * to)


def _unpad_heads(a, n, w, to):
    lead = a.shape[:-1]
    return a.reshape(*lead, n, to)[..., :w].reshape(*lead, n * w)


def _w_in_pad(w):
    return jnp.concatenate([_pad_heads(w[:, 0:256], 4, 64, HP), _pad_heads(w[:, 256:512], 4, 64, HP), w[:, 512:1536],
                            w[:, 2240:2752], w[:, 1536:2176], _pad_heads(w[:, 2176:2240], 1, 64, HP)], axis=1)


def _w_in_unpad(g):
    return jnp.concatenate([_unpad_heads(g[:, Z_RQ:Z_RK], 4, 64, HP), _unpad_heads(g[:, Z_RK:Z_RV], 4, 64, HP),
                            g[:, Z_RV:Z_MG], g[:, Z_CQ:Z_KR], g[:, Z_KR:Z_KR + 64], g[:, Z_MG:Z_CQ]], axis=1)


def _layer_weights(l, full, w):
    f32 = lambda a: a.astype(F32)
    i = l // 2
    common = dict(bias=f32(w["ada_b"][l])[None], pre_g=f32(w["pre_g"][l])[None], post_g=f32(w["post_g"][l])[None])
    if l % 2 == 0:
        w_uq_p = _pad_heads(full[("ev_w_uq", i)], MLA_HEADS, MLA_NOPE_DIM + MLA_ROPE_DIM, QW)
        lg = jnp.stack([jax.nn.log_sigmoid(f32(w["ev_dec_f"][i])), jax.nn.log_sigmoid(f32(w["ev_dec_b"][i]))])
        return dict(common, w_in=_w_in_pad(full[("ev_w_in", i)]), w_uq=w_uq_p, w_ukv=full[("ev_w_ukv", i)],
                    w_out=full[("ev_w_out", i)], qg=f32(w["ev_q_norm_g"][i])[None], kvg=f32(w["ev_kv_norm_g"][i])[None], lg=lg)
    dw = f32(full[("od_dw_w", i)])
    return dict(common, w_in=full[("od_w_in", i)], w_out=full[("od_w_out", i)], b_in=f32(full[("od_b_in", i)])[None],
                dw=jnp.pad(dw, ((0, 1), (0, 0))), dw_flip=jnp.pad(dw[::-1], ((0, 1), (0, 0))),
                dw_b=f32(full[("od_dw_b", i)])[None], ln_g=f32(full[("od_ln_g", i)])[None], ln_b=f32(full[("od_ln_b", i)])[None])


def _rope_tables(positions):
    inv_freq = ROPE_BASE ** (-jnp.arange(0, ROPE_DIM, 2, dtype=F32) / ROPE_DIM)
    ang = positions.astype(F32)[:, None] * inv_freq
    z = jnp.zeros((ang.shape[0], HP - ROPE_DIM), F32)
    return jnp.concatenate([jnp.cos(ang), jnp.cos(ang), z], axis=1), jnp.concatenate([jnp.sin(ang), jnp.sin(ang), z], axis=1)


def _even_fwd(x, mod, cos, sin, w, side=None):
    bias, pre_g, post_g = w["bias"], w["pre_g"], w["post_g"]
    h, z = _pre_mm("ev_in", x, pre_g, mod, bias, w["w_in"])
    rq, rk, rv, qn, kvn, Q, K, V = _ev_mid_fwd(z, cos, sin, w["qg"], w["kvg"], w["w_uq"], w["w_ukv"])
    a, lse, *got = _flash_fwd(Q, K, V, side)
    o2, states = _ret_fwd(w["lg"], rq, rk, rv)
    mix, y, x_new = _ev_out_fwd(o2, z, a, x, w["w_out"], post_g, mod, bias)
    saved = dict(x=x, h=h, z=z, rq=rq, rk=rk, rv=rv, qn=qn, kvn=kvn, Q=Q, K=K, V=V, a=a, lse=lse, o2=o2,
                 states=states, mix=mix, y=y)
    return x_new, saved, (got[0] if got else None)


def _even_bwd(dx, s, mod, cos, sin, w, side=None):
    bias, pre_g, post_g = w["bias"], w["pre_g"], w["post_g"]
    dy, do, drg, da, dmg, dpost_g, dgate = _ev_out_bwd(s["y"], dx, s["o2"], s["z"], s["a"], w["w_out"], post_g, mod, bias)
    dw_out = _mm_tn("ev_out_dw", s["mix"], dy, BF16)
    drq, drk, drv, dlg = _ret_bwd(w["lg"], s["rq"], s["rk"], s["rv"], do, s["states"])
    dQ, dK, dV, *got = _flash_bwd(s["Q"], s["K"], s["V"], da, s["a"], s["lse"], side)
    dqf, dkv, dqn, dkvn, dkr = _ev_mid_bwd(dQ, dK, dV, cos, sin, w["w_uq"], w["w_ukv"])
    dw_uq = _mm_tn("ev_uq_dw", s["qn"], dqf, BF16)
    dw_ukv = _mm_tn("ev_ukv_dw", s["kvn"], dkv, BF16)
    dz, dx_new, dqg, dkvg, dpre_g, dscale, dshift = _ev_in_bwd(
        s["z"], cos, sin, w["qg"], w["kvg"], drq, drk, drv, drg, dqn, dkvn, dkr, dmg, s["x"], dx, w["w_in"], pre_g, mod, bias)
    dw_in = _mm_tn("ev_in_dw", s["h"], dz, BF16)
    g = dict(ev_w_in=_w_in_unpad(dw_in), ev_w_uq=_unpad_heads(dw_uq, MLA_HEADS, MLA_NOPE_DIM + MLA_ROPE_DIM, QW),
             ev_w_ukv=dw_ukv, ev_w_out=dw_out, ev_q_norm_g=dqg[0], ev_kv_norm_g=dkvg[0],
             dlg_f=dlg[0, :RET_HEADS, 0], dlg_b=dlg[1, :RET_HEADS, 0])
    return dx_new, g, dpre_g[0], dpost_g[0], jnp.concatenate([dshift, dscale, dgate], axis=1), (got[0] if got else None)


def _odd_fwd(x, mod, w, tgt=None):
    bias, pre_g, post_g = w["bias"], w["pre_g"], w["post_g"]
    h, z, u = _od_in_fwd(x, pre_g, mod, bias, w["w_in"], w["b_in"])
    uc = _conv_fwd(u, w["dw"], w["dw_b"])
    vv, y, *rest = _od_out_fwd(uc, z, x, w["w_out"], w["b_in"], w["ln_g"], w["ln_b"], post_g, mod, bias, tgt)
    return (rest[0] if tgt is None else tuple(rest)), dict(x=x, h=h, z=z, u=u, uc=uc, vv=vv, y=y)


def _odd_bwd(dx, s, mod, w):
    bias, pre_g, post_g = w["bias"], w["pre_g"], w["post_g"]
    dy, duc, dg, dpost_g, dgate, dbg, dln_g, dln_b = _od_out_bwd(
        s["y"], dx, s["uc"], s["z"], w["w_out"], w["b_in"], w["ln_g"], w["ln_b"], post_g, mod, bias)
    dw_out = _mm_tn("od_out_dw", s["vv"], dy, BF16)
    du = _conv_fwd(duc, w["dw_flip"], jnp.zeros_like(w["dw_b"]))
    dwb = _conv_bwd_w(s["u"], duc)
    dz, dx_new, dba, dbb, dpre_g, dscale, dshift = _od_in_bwd(s["z"], w["b_in"], du, dg, s["x"], dx, w["w_in"], pre_g, mod, bias)
    dw_in = _mm_tn("od_in_dw", s["h"], dz, BF16)
    g = dict(od_w_in=dw_in, od_b_in=jnp.concatenate([dba, dbb, dbg], axis=1)[0], od_dw_w=dwb[:CONV_KERNEL], od_dw_b=dwb[CONV_KERNEL],
             od_ln_g=dln_g[0], od_ln_b=dln_b[0], od_w_out=dw_out)
    return dx_new, g, dpre_g[0], dpost_g[0], jnp.concatenate([dshift, dscale, dgate], axis=1)


LATE_LAYERS = tuple(range(1, DEPTH))


def _step(x, mod, positions, loss_target, w):
    cos, sin = _rope_tables(positions)
    f32 = lambda a: a.astype(F32)
    mods = [mod[l:l + 1] for l in range(DEPTH)]
    first, late = _layer_entries([0]), _layer_entries(LATE_LAYERS)

    gathered = _halves_merge("gather_merge_first", _chip_exchange("gather_chips_first", _pack_weights(w, first), "half"))
    full = _unpack_weights(gathered, w, first)
    lw = {0: _layer_weights(0, full, w)}
    x, s0, got = _even_fwd(x, mods[0], cos, sin, lw[0], side=(_pack_weights(w, late), "half"))
    full.update(_unpack_weights(_halves_merge("gather_merge_late", got), w, late))
    saved = [s0]
    assert DEPTH % 2 == 0
    for l in LATE_LAYERS:
        lw[l] = _layer_weights(l, full, w)
        if l % 2 == 0:
            x, s, _ = _even_fwd(x, mods[l], cos, sin, lw[l])
        else:
            x, s = _odd_fwd(x, mods[l], lw[l], loss_target if l == DEPTH - 1 else None)
        saved.append(s)
    dx, loss = x

    grads = {}
    dpre, dpost, dmod = [None] * DEPTH, [None] * DEPTH, [None] * DEPTH
    for l in reversed(LATE_LAYERS):
        if l % 2 == 0:
            dx, g, dpre[l], dpost[l], dmod[l], _ = _even_bwd(dx, saved[l], mods[l], cos, sin, lw[l])
        else:
            dx, g, dpre[l], dpost[l], dmod[l] = _odd_bwd(dx, saved[l], mods[l], lw[l])
        grads.update({(name, l // 2): val for name, val in g.items()})
    chip_sum = _reduce_begin("late", _pack_grads(grads, late, ()))
    dx, g, dpre[0], dpost[0], dmod[0], got = _even_bwd(dx, saved[0], mods[0], cos, sin, lw[0], side=(chip_sum, "scatter"))
    grads.update({(name, 0): val for name, val in g.items()})
    red = _unpack_grads(_reduce_end("late", got), w, late, ())

    n_ev = (DEPTH + 1) // 2
    grads.update(pre_g=jnp.stack(dpre), post_g=jnp.stack(dpost), ada_b=jnp.concatenate(dmod, axis=0))
    for name in ("ev_q_norm_g", "ev_kv_norm_g"):
        grads[name] = jnp.stack([grads[(name, i)] for i in range(n_ev)])
    grads["ev_dec_f"] = jnp.stack([grads[("dlg_f", i)] for i in range(n_ev)]) * jax.nn.sigmoid(-f32(w["ev_dec_f"]))
    grads["ev_dec_b"] = jnp.stack([grads[("dlg_b", i)] for i in range(n_ev)]) * jax.nn.sigmoid(-f32(w["ev_dec_b"]))
    chip_sum = _reduce_begin("first", _pack_grads(grads, first, REPLICATED))
    total = _reduce_end("first", _chip_exchange("reduce_chips_first", chip_sum, "scatter"))
    red.update(_unpack_grads(total, w, first, REPLICATED))

    out = {name: red[name] for name in REPLICATED}
    for name, _, _ in EV_SHARDED:
        out[name] = jnp.stack([red[(name, i)] for i in range(n_ev)])
    for name, _, _ in OD_SHARDED:
        out[name] = jnp.stack([red[(name, i)] for i in range(DEPTH // 2)])
    return loss[0, 0], dx, out, grads["ada_b"]


ADA_ROWS = 16
ADA_COLS = 3 * D_MODEL // 4
ADA_PACK = (8, 1024)


def _ada_pack(a):
    n = ADA_PACK[0] * ADA_PACK[1]
    return jnp.pad(a, ((0, 0), (0, n - a.shape[1]))).reshape(4, *ADA_PACK)


def _ada_unpack(a):
    return a.reshape(4, -1)[:, :DEPTH * ADA_COLS]


def _ada_forward(c, ada_w):
    D = D_MODEL
    c16 = jnp.zeros((ADA_ROWS, D), F32).at[0].set(c[0])
    c_act = _rowcall("silu_c", lambda rows, p: ([_silu(rows[0][...])], []), [c16], [], [(D, BF16)])[0]
    got = _chip_exchange("ada_c", c_act, "whole")
    c4 = jnp.pad(got[:, 0, :], ((0, ADA_ROWS - 4), (0, 0)))
    per_layer = [_mm_nn("ada", c4, ada_w[l].astype(BF16))[:4] for l in range(DEPTH)]
    p = jnp.stack(per_layer, axis=1).reshape(4, DEPTH * ADA_COLS)
    got = _chip_exchange("ada_mod", _ada_pack(p), "scatter")
    mod = _ada_unpack(got).reshape(4, DEPTH, ADA_COLS).transpose(1, 0, 2).reshape(DEPTH, 3 * D)
    return mod, c4


def _ada_backward(dmod, c4):
    p = dmod.reshape(DEPTH, 4, ADA_COLS).transpose(1, 0, 2).reshape(4, DEPTH * ADA_COLS)
    got = _chip_exchange("ada_dmod", _ada_pack(p), "scatter")
    dm4 = jnp.pad(_ada_unpack(got), ((0, ADA_ROWS - 4), (0, 0)))
    both = _halves_merge("ada_merge", jnp.concatenate([dm4, c4.astype(F32)], axis=1))
    dm32, c32 = both[:, :DEPTH * ADA_COLS].astype(BF16), both[:, DEPTH * ADA_COLS:].astype(BF16)
    return jnp.stack([_mm_tn("ada_dw", c32, dm32[:, l * ADA_COLS:(l + 1) * ADA_COLS]) for l in range(DEPTH)])


def kernel(x, c, positions, ada_w, ada_b, pre_g, post_g, ev_w_in, ev_dec_f, ev_dec_b, ev_q_norm_g, ev_w_uq, ev_kv_norm_g, ev_w_ukv, ev_w_out, od_w_in, od_b_in, od_dw_w, od_dw_b, od_ln_g, od_ln_b, od_w_out, loss_target, m_ada_w, m_ada_b, m_pre_g, m_post_g, m_ev_w_in, m_ev_dec_f, m_ev_dec_b, m_ev_q_norm_g, m_ev_w_uq, m_ev_kv_norm_g, m_ev_w_ukv, m_ev_w_out, m_od_w_in, m_od_b_in, m_od_dw_w, m_od_dw_b, m_od_ln_g, m_od_ln_b, m_od_w_out, v_ada_w, v_ada_b, v_pre_g, v_post_g, v_ev_w_in, v_ev_dec_f, v_ev_dec_b, v_ev_q_norm_g, v_ev_w_uq, v_ev_kv_norm_g, v_ev_w_ukv, v_ev_w_out, v_od_w_in, v_od_b_in, v_od_dw_w, v_od_dw_b, v_od_ln_g, v_od_ln_b, v_od_w_out):
    w = dict(ada_w=ada_w, ada_b=ada_b, pre_g=pre_g, post_g=post_g, ev_w_in=ev_w_in, ev_dec_f=ev_dec_f, ev_dec_b=ev_dec_b,
             ev_q_norm_g=ev_q_norm_g, ev_w_uq=ev_w_uq, ev_kv_norm_g=ev_kv_norm_g, ev_w_ukv=ev_w_ukv, ev_w_out=ev_w_out,
             od_w_in=od_w_in, od_b_in=od_b_in, od_dw_w=od_dw_w, od_dw_b=od_dw_b, od_ln_g=od_ln_g, od_ln_b=od_ln_b, od_w_out=od_w_out)
    m = dict(ada_w=m_ada_w, ada_b=m_ada_b, pre_g=m_pre_g, post_g=m_post_g, ev_w_in=m_ev_w_in, ev_dec_f=m_ev_dec_f, ev_dec_b=m_ev_dec_b,
             ev_q_norm_g=m_ev_q_norm_g, ev_w_uq=m_ev_w_uq, ev_kv_norm_g=m_ev_kv_norm_g, ev_w_ukv=m_ev_w_ukv, ev_w_out=m_ev_w_out,
             od_w_in=m_od_w_in, od_b_in=m_od_b_in, od_dw_w=m_od_dw_w, od_dw_b=m_od_dw_b, od_ln_g=m_od_ln_g, od_ln_b=m_od_ln_b, od_w_out=m_od_w_out)
    v = dict(ada_w=v_ada_w, ada_b=v_ada_b, pre_g=v_pre_g, post_g=v_post_g, ev_w_in=v_ev_w_in, ev_dec_f=v_ev_dec_f, ev_dec_b=v_ev_dec_b,
             ev_q_norm_g=v_ev_q_norm_g, ev_w_uq=v_ev_w_uq, ev_kv_norm_g=v_ev_kv_norm_g, ev_w_ukv=v_ev_w_ukv, ev_w_out=v_ev_w_out,
             od_w_in=v_od_w_in, od_b_in=v_od_b_in, od_dw_w=v_od_dw_w, od_dw_b=v_od_dw_b, od_ln_g=v_od_ln_g, od_ln_b=v_od_ln_b, od_w_out=v_od_w_out)

    mod, c4 = _ada_forward(c, ada_w)
    loss_local, grad_x, g, dmod = _step(x[0], mod, positions[0], loss_target[0], w)
    loss = lax.psum(loss_local, ("x", "y", "c"))
    g["ada_w"] = _ada_backward(dmod, c4)
    delta, new_m, new_v = {}, {}, {}
    for name in WEIGHTS:
        delta[name], new_m[name], new_v[name] = _adamw(w[name], g[name], m[name], v[name])
    return (loss, grad_x[None], *[g[n] for n in WEIGHTS], *[delta[n] for n in WEIGHTS],
            *[new_m[n] for n in WEIGHTS], *[new_v[n] for n in WEIGHTS])
```

```python
import functools

import jax
import jax.numpy as jnp
from jax import lax
from jax.experimental import pallas as pl
from jax.experimental.pallas import tpu as pltpu

F32 = jnp.float32
BF16 = jnp.bfloat16

D_MODEL = 1024
DEPTH = 4
RET_HEADS = 4
RET_QK_DIM = 64
RET_V_DIM = 128
MLA_HEADS = 4
MLA_Q_RANK = 384
MLA_KV_RANK = 256
MLA_NOPE_DIM = 128
MLA_ROPE_DIM = 64
MLA_V_DIM = 128
ROPE_DIM = 64
ROPE_BASE = 10000.0
CONV_KERNEL = 31
EPS = 1e-6
ADAM_LR, ADAM_B1, ADAM_B2, ADAM_EPS, ADAM_WD, ADAM_STEP = 0.001, 0.9, 0.999, 1e-08, 0.01, 10

LANES = 128
HP = 128
Z_RQ, Z_RK, Z_RV, Z_RG, Z_MG, Z_CQ, Z_CKV, Z_KR, Z_END = 0, 512, 1024, 1536, 2048, 2560, 2944, 3200, 3328
Z_LAT = Z_END - Z_CQ


def _z_ret(z):
    return (z, Z_RG, 0)


def _z_gates(z):
    return [(z, 512, Z_RG // 512), (z, 512, Z_MG // 512)]


def _z_latents(z):
    return [(z, 256, Z_CQ // 256 + i) for i in range(Z_LAT // 256)]
V7X_VMEM_BYTES = 64 * 1024 * 1024
VMEM_CAP = V7X_VMEM_BYTES - 8 * 1024 * 1024

MESH = pl.DeviceIdType.MESH


def _nbytes(shape, dtype):
    n = 1
    for s in shape:
        n *= s
    return n * jnp.dtype(dtype).itemsize


def _hbm(x):
    return pltpu.with_memory_space_constraint(x, pltpu.HBM)


def _params(sem, block_bytes):
    limit = min(VMEM_CAP, max(32 * 1024 * 1024, 2 * block_bytes + 16 * 1024 * 1024))
    return pltpu.CompilerParams(dimension_semantics=sem, vmem_limit_bytes=limit)


ROW_TILE = 512


def _rowcall(name, fn, rows, params, outs, accs=(), ts=ROW_TILE, mats=()):
    rows = [r if isinstance(r, tuple) else (r, r.shape[1], 0) for r in rows]
    S = rows[0][0].shape[0]
    ts = min(ts, S)
    assert S % ts == 0, (name, S, ts)
    nr, npar, nm, no, na = len(rows), len(params), len(mats), len(outs), len(accs)

    def body(*refs):
        row_refs = refs[:nr]
        pvals = [r[...] for r in refs[nr:nr + npar]]
        mat_refs = refs[nr + npar:nr + npar + nm]
        out_refs = refs[nr + npar + nm:nr + npar + nm + no]
        acc_refs = refs[nr + npar + nm + no:]
        ovals, avals = fn(row_refs, pvals, mat_refs) if nm else fn(row_refs, pvals)
        for r, v in zip(out_refs, ovals, strict=True):
            r[...] = v.astype(r.dtype)
        if na:
            @pl.when(pl.program_id(0) == 0)
            def _():
                for r in acc_refs:
                    r[...] = jnp.zeros_like(r)
            for r, v in zip(acc_refs, avals, strict=True):
                r[...] += v

    in_specs = [pl.BlockSpec((ts, w), functools.partial(lambda i, b: (i, b), b=blk)) for _, w, blk in rows]
    in_specs += [pl.BlockSpec(p.shape, lambda i: (0, 0)) for p in [*params, *mats]]
    out_specs = [pl.BlockSpec((ts, w), lambda i: (i, 0)) for w, _ in outs]
    out_specs += [pl.BlockSpec(s, lambda i: (0, 0)) for s in accs]
    out_shape = [jax.ShapeDtypeStruct((S, w), dt) for w, dt in outs]
    out_shape += [jax.ShapeDtypeStruct(s, F32) for s in accs]
    nb = sum(_nbytes((ts, w), a.dtype) for a, w, _ in rows) + sum(_nbytes((ts, w), dt) for w, dt in outs)
    nb += sum(_nbytes(p.shape, p.dtype) for p in params) + sum(_nbytes(s, F32) for s in accs)
    res = pl.pallas_call(
        body, name=name, grid=(S // ts,), in_specs=in_specs, out_specs=out_specs, out_shape=out_shape,
        compiler_params=_params(("arbitrary",) if na else ("parallel",), 3 * nb + sum(_nbytes(m.shape, m.dtype) for m in mats)),
    )(*[_hbm(a) for a, _, _ in rows], *params, *[_hbm(m) for m in mats])
    return res


def _silu(x):
    return x * jax.nn.sigmoid(x)


def _rms(x, g):
    return x * lax.rsqrt(jnp.mean(x * x, axis=-1, keepdims=True) + EPS) * g


def _rot(x):
    lane = lax.broadcasted_iota(jnp.int32, x.shape, 1)
    return jnp.where(lane < ROPE_DIM // 2, -pltpu.roll(x, LANES - ROPE_DIM // 2, 1), pltpu.roll(x, ROPE_DIM // 2, 1))


def _rope(x, cos, sin):
    return x * cos + _rot(x) * sin


def _rope_t(dy, cos, sin):
    return dy * cos - _rot(dy) * sin


def _groups(ref, start, n):
    return [ref[:, start + HP * h:start + HP * (h + 1)] for h in range(n)]


def _pre_math(x, g, m_scale, b_scale, m_shift, b_shift):
    return _rms(x, g) * (1.0 + (m_scale + b_scale)) + (m_shift + b_shift)


def _post_math(x, y, g, m_gate, b_gate):
    return x + (m_gate + b_gate) * _rms(y, g)


def _ev_mid_fwd(z, cos, sin, qg, kvg, w_uq, w_ukv):
    qscale = ATT_SCALE * LOG2E

    def fn(rows, p, mats):
        z_, l0, l1, l2, c_, s_ = rows
        qg_, kvg_ = p
        cos_, sin_ = c_[...], s_[...]
        lat = jnp.concatenate([l0[...], l1[...], l2[...]], axis=1)
        rq = jnp.concatenate([_rope(v[...], cos_, sin_) for v in _groups(z_, Z_RQ, RET_HEADS)], axis=1)
        rk = jnp.concatenate([_rope(v[...], cos_, sin_) for v in _groups(z_, Z_RK, RET_HEADS)], axis=1)
        rk = rk * (RET_QK_DIM ** -0.5)
        rv = z_[:, Z_RV:Z_RG]
        qn = _rms(lat[:, :Z_CKV - Z_CQ], qg_).astype(BF16)
        kvn = _rms(lat[:, Z_CKV - Z_CQ:Z_KR - Z_CQ], kvg_).astype(BF16)
        kr = _rope(lat[:, Z_KR - Z_CQ:], cos_, sin_)
        qf = jnp.dot(qn, mats[0][...], preferred_element_type=F32)
        kv = jnp.dot(kvn, mats[1][...], preferred_element_type=F32)
        qs, ks, vs = [], [], []
        for h in range(MLA_HEADS):
            b = 2 * HP * h
            qs += [qf[:, b:b + HP] * qscale, _rope(qf[:, b + HP:b + 2 * HP], cos_, sin_) * qscale]
            ks += [kv[:, b:b + HP], kr]
            vs += [kv[:, b + HP:b + 2 * HP]]
        return [rq, rk, rv, qn, kvn, jnp.concatenate(qs, axis=1), jnp.concatenate(ks, axis=1), jnp.concatenate(vs, axis=1)], []

    return _rowcall("ev_mid_fwd", fn, [_z_ret(z), *_z_latents(z), cos, sin], [qg, kvg],
                    [(512, BF16), (512, BF16), (512, BF16), (MLA_Q_RANK, BF16), (MLA_KV_RANK, BF16),
                     (1024, BF16), (1024, BF16), (512, BF16)], mats=[w_uq, w_ukv])


def _pre_bwd_math(dh, x, res, g, mod, bias):
    D = D_MODEL
    _, vjp = jax.vjp(_pre_math, x, g, mod[:, D:2 * D], bias[:, D:2 * D], mod[:, :D], bias[:, :D])
    dx, dg, dsc, _, dsh, _ = vjp(dh)
    return dx + res, dg, dsc, dsh


def _ev_in_bwd(z, cos, sin, qg, kvg, drq, drk, drv, drg, dqn, dkvn, dkr, dmg, x, dx_res, w_in, g, mod, bias):
    D = D_MODEL

    def fn(rows, p, mats):
        l0, l1, l2, c_, s_, drq_, drk_, drv_, drg_, dqn_, dkvn_, dkr_, dmg_, x_, res_ = rows
        qg_, kvg_, g_, mod_, b_ = p
        cos_, sin_ = c_[...], s_[...]
        lat = jnp.concatenate([l0[...], l1[...], l2[...]], axis=1)
        parts = []
        for h in range(RET_HEADS):
            t = drq_[:, HP * h:HP * (h + 1)].astype(F32) + drq_[:, 512 + HP * h:512 + HP * (h + 1)].astype(F32)
            parts.append(_rope_t(t, cos_, sin_))
        for h in range(RET_HEADS):
            t = drk_[:, HP * h:HP * (h + 1)].astype(F32) + drk_[:, 512 + HP * h:512 + HP * (h + 1)].astype(F32)
            parts.append(_rope_t(t, cos_, sin_) * (RET_QK_DIM ** -0.5))
        parts.append(drv_[:, :512].astype(F32) + drv_[:, 512:].astype(F32))
        parts += [drg_[...], dmg_[...]]
        _, vq = jax.vjp(_rms, lat[:, :Z_CKV - Z_CQ], qg_)
        dcq, dqg = vq(dqn_[...])
        _, vkv = jax.vjp(_rms, lat[:, Z_CKV - Z_CQ:Z_KR - Z_CQ], kvg_)
        dckv, dkvg = vkv(dkvn_[...])
        parts += [dcq, dckv, _rope_t(dkr_[...], cos_, sin_)]
        dz = jnp.concatenate([v.astype(BF16) for v in parts], axis=1)
        dh = lax.dot_general(dz, mats[0][...], NT_DIMS, preferred_element_type=F32)
        dx, dg, dsc, dsh = _pre_bwd_math(dh, x_[...], res_[...], g_, mod_, b_)
        return [dz, dx], [dqg, dkvg, dg, dsc, dsh]

    rows = [*_z_latents(z), cos, sin, drq, drk, drv, drg, dqn, dkvn, dkr, dmg, x, dx_res]
    return _rowcall("ev_in_bwd", fn, rows, [qg, kvg, g, mod, bias], [(Z_END, BF16), (D, F32)],
                    [(1, MLA_Q_RANK), (1, MLA_KV_RANK)] + [(1, D)] * 3, ts=256, mats=[w_in])


ATT_SCALE = (MLA_NOPE_DIM + MLA_ROPE_DIM) ** -0.5
LOG2E = 1.4426950408889634
LN2 = 0.6931471805599453


def _ev_mid_bwd(dQ, dK, dV, cos, sin, w_uq, w_ukv):
    def fn(rows, p, mats):
        dq_, dk_, dv_, c_, s_ = rows
        cos_, sin_ = c_[...], s_[...]
        dqs, dkvs = [], []
        dkr = None
        for h in range(MLA_HEADS):
            b = 2 * HP * h
            dqs += [dq_[:, b:b + HP] * ATT_SCALE, _rope_t(dq_[:, b + HP:b + 2 * HP], cos_, sin_) * ATT_SCALE]
            dkvs += [dk_[:, b:b + HP] * LN2, dv_[:, HP * h:HP * (h + 1)]]
            t = dk_[:, b + HP:b + 2 * HP]
            dkr = t if dkr is None else dkr + t
        dqf = jnp.concatenate(dqs, axis=1).astype(BF16)
        dkv = jnp.concatenate(dkvs, axis=1).astype(BF16)
        dqn = lax.dot_general(dqf, mats[0][...], NT_DIMS, preferred_element_type=F32)
        dkvn = lax.dot_general(dkv, mats[1][...], NT_DIMS, preferred_element_type=F32)
        return [dqf, dkv, dqn, dkvn, dkr * LN2], []

    return _rowcall("ev_mid_bwd", fn, [dQ, dK, dV, cos, sin], [],
                    [(1024, BF16), (1024, BF16), (MLA_Q_RANK, F32), (MLA_KV_RANK, F32), (HP, F32)], mats=[w_uq, w_ukv])


def _mix_math(o, rg, a, mg):
    outs = []
    for h in range(RET_HEADS):
        oh = o[:, HP * h:HP * (h + 1)]
        mu = jnp.mean(oh, axis=-1, keepdims=True)
        var = jnp.mean(jnp.square(oh - mu), axis=-1, keepdims=True)
        outs.append((oh - mu) * lax.rsqrt(var + EPS))
    ret = jnp.concatenate(outs, axis=1) * _silu(rg)
    return jnp.concatenate([ret, a * _silu(mg)], axis=1)


def _ev_out_fwd(o2, z, a, x, w_out, g, mod, bias):
    D = D_MODEL

    def fn(rows, p, mats):
        o_, rg_, mg_, a_, x_ = rows
        g_, mod_, b_ = p
        mix = _mix_math(o_[:, :512] + o_[:, 512:], rg_[...], a_[...], mg_[...]).astype(BF16)
        y = jnp.dot(mix, mats[0][...], preferred_element_type=F32)
        return [mix, y, _post_math(x_[...], y, g_, mod_[:, 2 * D:], b_[:, 2 * D:])], []

    return _rowcall("ev_out", fn, [o2, *_z_gates(z), a, x], [g, mod, bias], [(1024, BF16), (D, F32), (D, F32)], mats=[w_out])


def _post_bwd_math(y, dxo, g, mod, bias):
    D = D_MODEL
    b_gate = bias[:, 2 * D:]
    _, vjp = jax.vjp(lambda y_, g2, mg: _post_math(0.0, y_, g2, mg, b_gate), y, g, mod[:, 2 * D:])
    return vjp(dxo)


def _ev_out_bwd(y, dxo, o2, z, a, w_out, g, mod, bias):
    D = D_MODEL

    def fn(rows, p, mats):
        y_, dxo_, o_, rg_, mg_, a_ = rows
        g_, mod_, b_ = p
        dy, dg, dgate = _post_bwd_math(y_[...], dxo_[...], g_, mod_, b_)
        dyb = dy.astype(BF16)
        dmix = lax.dot_general(dyb, mats[0][...], NT_DIMS, preferred_element_type=F32)
        _, vjp = jax.vjp(_mix_math, o_[:, :512] + o_[:, 512:], rg_[...], a_[...], mg_[...])
        return [dyb, *vjp(dmix)], [dg, dgate]

    return _rowcall("ev_out_bwd", fn, [y, dxo, o2, *_z_gates(z), a], [g, mod, bias],
                    [(D, BF16), (512, BF16), (512, F32), (512, BF16), (512, F32)], [(1, D)] * 2, mats=[w_out])


def _glu_math(a, b, ba, bb):
    return (a + ba) * jax.nn.sigmoid(b + bb)


def _od_in_fwd(x, g, mod, bias, w_in, bin_):
    D = D_MODEL

    def fn(rows, p, mats):
        g_, mod_, b_, bi_ = p
        h = _pre_math(rows[0][...], g_, mod_[:, D:2 * D], b_[:, D:2 * D], mod_[:, :D], b_[:, :D]).astype(BF16)
        z = jnp.dot(h, mats[0][...], preferred_element_type=F32)
        return [h, z, _glu_math(z[:, :D], z[:, D:2 * D], bi_[:, :D], bi_[:, D:2 * D])], []

    return _rowcall("od_in", fn, [x], [g, mod, bias, bin_], [(D, BF16), (3 * D, F32), (D, F32)], mats=[w_in])


def _odmix_math(uc, g, bg, ln_g, ln_b):
    mu = jnp.mean(uc, axis=-1, keepdims=True)
    var = jnp.mean(jnp.square(uc - mu), axis=-1, keepdims=True)
    y = (uc - mu) * lax.rsqrt(var + EPS) * ln_g + ln_b
    return _silu(y) * _silu(g + bg)


def _od_out_fwd(uc, z, x, w_out, bin_, ln_g, ln_b, g, mod, bias, tgt=None):
    D = D_MODEL

    def fn(rows, p, mats):
        uc_, zg_, x_ = rows[:3]
        bi_, lg_, lb_, g_, mod_, b_ = p
        vv = _odmix_math(uc_[...], zg_[...], bi_[:, 2 * D:], lg_, lb_).astype(BF16)
        y = jnp.dot(vv, mats[0][...], preferred_element_type=F32)
        xn = _post_math(x_[...], y, g_, mod_[:, 2 * D:], b_[:, 2 * D:])
        if tgt is None:
            return [vv, y, xn], []
        err = xn - rows[3][...]
        part = 0.5 * jnp.sum(jnp.mean(err * err, axis=-1, keepdims=True), axis=0, keepdims=True)
        return [vv, y, err * (1.0 / D)], [jnp.broadcast_to(part, (1, LANES))]

    rows = [uc, (z, D, 2), x] + ([] if tgt is None else [tgt])
    return _rowcall("od_out" if tgt is None else "od_out_loss", fn, rows, [bin_, ln_g, ln_b, g, mod, bias],
                    [(D, BF16), (D, F32), (D, F32)], [] if tgt is None else [(1, LANES)], mats=[w_out])


def _od_out_bwd(y, dxo, uc, z, w_out, bin_, ln_g, ln_b, g, mod, bias):
    D = D_MODEL

    def fn(rows, p, mats):
        y_, dxo_, uc_, zg_ = rows
        bi_, lg_, lb_, g_, mod_, b_ = p
        dy, dpg, dgate = _post_bwd_math(y_[...], dxo_[...], g_, mod_, b_)
        dyb = dy.astype(BF16)
        dvv = lax.dot_general(dyb, mats[0][...], NT_DIMS, preferred_element_type=F32)
        _, vjp = jax.vjp(_odmix_math, uc_[...], zg_[...], bi_[:, 2 * D:], lg_, lb_)
        duc, dg, dbg, dlg, dlb = vjp(dvv)
        return [dyb, duc, dg], [dpg, dgate, dbg, dlg, dlb]

    return _rowcall("od_out_bwd", fn, [y, dxo, uc, (z, D, 2)], [bin_, ln_g, ln_b, g, mod, bias],
                    [(D, BF16), (D, F32), (D, BF16)], [(1, D)] * 5, mats=[w_out])


def _od_in_bwd(z, bin_, du, dg, x, dx_res, w_in, g, mod, bias):
    D = D_MODEL

    def fn(rows, p, mats):
        za_, zb_, du_, dg_, x_, res_ = rows
        bi_, g_, mod_, b_ = p
        _, vjp = jax.vjp(_glu_math, za_[...], zb_[...], bi_[:, :D], bi_[:, D:2 * D])
        da, db, dba, dbb = vjp(du_[...])
        dz = jnp.concatenate([da.astype(BF16), db.astype(BF16), dg_[...]], axis=1)
        dh = lax.dot_general(dz, mats[0][...], NT_DIMS, preferred_element_type=F32)
        dx, dpg, dsc, dsh = _pre_bwd_math(dh, x_[...], res_[...], g_, mod_, b_)
        return [dz, dx], [dba, dbb, dpg, dsc, dsh]

    return _rowcall("od_in_bwd", fn, [(z, D, 0), (z, D, 1), du, dg, x, dx_res], [bin_, g, mod, bias],
                    [(3 * D, BF16), (D, F32)], [(1, D)] * 5, ts=256, mats=[w_in])


def _tile(n, cap):
    if n <= cap:
        return n
    best = None
    for t in range(LANES, cap + 1, LANES):
        if n % t == 0:
            best = t
    assert best is not None, (n, cap)
    return best


MM_TN_CAP = 1792
NT_DIMS = (((1,), (1,)), ((), ()))
TN_DIMS = (((0,), (0,)), ((), ()))


def _mm_nn(name, a, b, out_dtype=F32, tm=512, tn_cap=MM_TN_CAP):
    M, K = a.shape
    N = b.shape[1]
    tm = min(tm, M)
    tn = _tile(N, tn_cap)

    def body(a_ref, b_ref, o_ref):
        o_ref[...] = jnp.dot(a_ref[...], b_ref[...], preferred_element_type=F32).astype(o_ref.dtype)

    nb = _nbytes((tm, K), a.dtype) + _nbytes((K, tn), b.dtype) + _nbytes((tm, tn), out_dtype) + _nbytes((tm, tn), F32)
    return pl.pallas_call(
        body, name=name, grid=(N // tn, M // tm),
        in_specs=[pl.BlockSpec((tm, K), lambda j, i: (i, 0)), pl.BlockSpec((K, tn), lambda j, i: (0, j))],
        out_specs=pl.BlockSpec((tm, tn), lambda j, i: (i, j)),
        out_shape=jax.ShapeDtypeStruct((M, N), out_dtype),
        compiler_params=_params(("parallel", "parallel"), nb),
    )(_hbm(a), _hbm(b))


def _pre_mm(name, x, g, mod, bias, w, tm=512):
    S, D = x.shape
    N = w.shape[1]
    tm = min(tm, S)

    def body(x_ref, g_ref, mod_ref, bias_ref, w_ref, h_ref, z_ref):
        mod_, b_ = mod_ref[...], bias_ref[...]
        h = _pre_math(x_ref[...], g_ref[...], mod_[:, D:2 * D], b_[:, D:2 * D], mod_[:, :D], b_[:, :D]).astype(BF16)
        h_ref[...] = h
        z_ref[...] = jnp.dot(h, w_ref[...], preferred_element_type=F32)

    nb = 2 * _nbytes((tm, D), F32) + _nbytes((D, N), w.dtype) + 2 * _nbytes((tm, N), F32)
    row = lambda i: (i, 0)
    whole = lambda i: (0, 0)
    return pl.pallas_call(
        body, name=name, grid=(S // tm,),
        in_specs=[pl.BlockSpec((tm, D), row), pl.BlockSpec(g.shape, whole), pl.BlockSpec(mod.shape, whole),
                  pl.BlockSpec(bias.shape, whole), pl.BlockSpec((D, N), whole)],
        out_specs=[pl.BlockSpec((tm, D), row), pl.BlockSpec((tm, N), row)],
        out_shape=[jax.ShapeDtypeStruct((S, D), BF16), jax.ShapeDtypeStruct((S, N), F32)],
        compiler_params=_params(("parallel",), nb),
    )(_hbm(x), g, mod, bias, _hbm(w))


def _mm_tn(name, a, b, out_dtype=F32, ts=1024, tm_cap=512, tn_cap=MM_TN_CAP):
    S, M = a.shape
    N = b.shape[1]
    ts = min(ts, S)
    tm = _tile(M, tm_cap)
    tn = _tile(N, tn_cap)
    ns = S // ts

    def body(a_ref, b_ref, o_ref, acc):
        s = pl.program_id(2)

        @pl.when(s == 0)
        def _():
            acc[...] = jnp.zeros(acc.shape, F32)

        acc[...] += lax.dot_general(a_ref[...], b_ref[...], TN_DIMS, preferred_element_type=F32)

        @pl.when(s == ns - 1)
        def _():
            o_ref[...] = acc[...].astype(o_ref.dtype)

    nb = _nbytes((ts, tm), a.dtype) + _nbytes((ts, tn), b.dtype) + 3 * _nbytes((tm, tn), F32)
    return pl.pallas_call(
        body, name=name, grid=(M // tm, N // tn, ns),
        in_specs=[pl.BlockSpec((ts, tm), lambda i, j, s: (s, i)), pl.BlockSpec((ts, tn), lambda i, j, s: (s, j))],
        out_specs=pl.BlockSpec((tm, tn), lambda i, j, s: (i, j)),
        out_shape=jax.ShapeDtypeStruct((M, N), out_dtype),
        scratch_shapes=[pltpu.VMEM((tm, tn), F32)],
        compiler_params=_params(("parallel", "parallel", "arbitrary"), nb),
    )(_hbm(a), _hbm(b))


ATT_TQ = 1024
ATT_TK = 1024
ATT_TK_FWD = 1024
QW = 2 * HP


class _Side:
    def __init__(self, side):
        self.on = side is not None
        if self.on:
            self.p, self.mode = side
            shp = self.p.shape
            self.h, self.w = (shp[0] // 2, shp[1]) if self.mode == "half" else (shp[-2], shp[-1])
        self.in_specs = [_HBM] if self.on else []
        self.out_specs = [_HBM] if self.on else []
        self.out_shape = [jax.ShapeDtypeStruct((4, self.h, self.w), self.p.dtype)] if self.on else []
        self.scratch = list(_CHIP_SEMS) if self.on else []
        self.operands = [_hbm(self.p)] if self.on else []

    def copies(self, p_ref, x_ref, sems):
        return _chip_copies(p_ref, x_ref, *sems, self.mode, self.h)

    def start(self, p_ref, x_ref, sems, first):
        @pl.when(first)
        def _():
            for cp in self.copies(p_ref, x_ref, sems):
                cp.start()

    def finish(self, p_ref, x_ref, sems, last):
        @pl.when(last)
        def _():
            for cp in self.copies(p_ref, x_ref, sems):
                cp.wait()


def _flash_fwd(Q, K, V, side=None):
    S = Q.shape[0]
    H = MLA_HEADS
    tq, tk = min(ATT_TQ, S), min(ATT_TK_FWD, S)
    nk = S // tk
    sd = _Side(side)

    def body(*refs):
        if sd.on:
            q_ref, k_ref, vt_ref, p_ref, o_ref, lse_ref, x_ref, m_s, l_s, acc_s, s_a, s_b, *sems = refs
            step_id = pl.program_id(0) * (S // tq) + pl.program_id(1)
            sd.start(p_ref, x_ref, sems, step_id == 0)
        else:
            q_ref, k_ref, vt_ref, o_ref, lse_ref, m_s, l_s, acc_s, s_a, s_b = refs
        m_s[...] = jnp.full(m_s.shape, -jnp.inf, F32)
        l_s[...] = jnp.zeros(l_s.shape, F32)
        acc_s[...] = jnp.zeros(acc_s.shape, F32)

        def scores(j):
            k = k_ref[pl.ds(pl.multiple_of(j * tk, tk), tk), :]
            return lax.dot_general(k, q_ref[...], NT_DIMS, preferred_element_type=F32)

        def consume(st, j):
            m_prev = m_s[...]
            m_new = jnp.maximum(m_prev, jnp.max(st, axis=0, keepdims=True))
            alpha = jnp.exp2(m_prev - m_new)
            pt = jnp.exp2(st - m_new)
            l_s[...] = alpha * l_s[...] + jnp.sum(pt, axis=0, keepdims=True)
            v = vt_ref[pl.ds(pl.multiple_of(j * tk, tk), tk), :]
            acc_s[...] = alpha * acc_s[...] + lax.dot_general(v, pt.astype(BF16), TN_DIMS, preferred_element_type=F32)
            m_s[...] = m_new

        if nk % 2:
            def step(j, carry):
                consume(scores(j), j)
                return carry

            lax.fori_loop(0, nk, step, 0)
        else:
            s_a[...] = scores(0)

            def pair(jj, carry):
                j0 = 2 * jj
                s_b[...] = scores(j0 + 1)
                consume(s_a[...], j0)
                s_a[...] = scores(jnp.minimum(j0 + 2, nk - 1))
                consume(s_b[...], j0 + 1)
                return carry

            lax.fori_loop(0, nk // 2, pair, 0)
        l = l_s[...]
        o_ref[...] = (acc_s[...] * (1.0 / l)).T
        lse_ref[0] = m_s[...] + jnp.log2(l)
        if sd.on:
            sd.finish(p_ref, x_ref, sems, step_id == H * (S // tq) - 1)

    nb = (_nbytes((tq, QW), BF16) + _nbytes((S, QW), BF16) + _nbytes((S, HP), BF16) + 3 * _nbytes((tq, HP), F32)
          + 4 * _nbytes((tq, tk), F32))
    return pl.pallas_call(
        body, name="flash_fwd_side" if sd.on else "flash_fwd", grid=(H, S // tq),
        in_specs=[pl.BlockSpec((tq, QW), lambda h, i: (i, h)), pl.BlockSpec((S, QW), lambda h, i: (0, h)),
                  pl.BlockSpec((S, HP), lambda h, i: (0, h))] + sd.in_specs,
        out_specs=[pl.BlockSpec((tq, HP), lambda h, i: (i, h)), pl.BlockSpec((1, 1, tq), lambda h, i: (h, 0, i))] + sd.out_specs,
        out_shape=[jax.ShapeDtypeStruct((S, H * HP), F32), jax.ShapeDtypeStruct((H, 1, S), F32)] + sd.out_shape,
        scratch_shapes=[pltpu.VMEM((1, tq), F32), pltpu.VMEM((1, tq), F32), pltpu.VMEM((HP, tq), F32),
                        pltpu.VMEM((tk, tq), F32), pltpu.VMEM((tk, tq), F32)] + sd.scratch,
        compiler_params=_params(("arbitrary", "arbitrary") if sd.on else ("parallel", "parallel"), nb),
    )(_hbm(Q), _hbm(K), _hbm(V), *sd.operands)


def _flash_bwd(Q, K, V, dO, O, lse, side=None):
    S = Q.shape[0]
    H = MLA_HEADS
    tq, tk = min(ATT_TQ, S), min(ATT_TK, S)
    nk = S // tk
    sd = _Side(side)

    def body(*refs):
        if sd.on:
            q_ref, do_ref, o_ref, lse_ref, k_ref, v_ref, p_ref, dq_ref, dk_ref, dv_ref, x_ref, dq_s, *sems = refs
            step_id = pl.program_id(0) * (S // tq) + pl.program_id(1)
            sd.start(p_ref, x_ref, sems, step_id == 0)
        else:
            q_ref, do_ref, o_ref, lse_ref, k_ref, v_ref, dq_ref, dk_ref, dv_ref, dq_s = refs

        @pl.when(pl.program_id(1) == 0)
        def _():
            dk_ref[...] = jnp.zeros(dk_ref.shape, F32)
            dv_ref[...] = jnp.zeros(dv_ref.shape, F32)

        delta = jnp.sum((do_ref[...].astype(F32) * o_ref[...]).T, axis=0, keepdims=True)
        lse = lse_ref[0]
        dq_s[...] = jnp.zeros(dq_s.shape, F32)

        def rows(j):
            return pl.ds(pl.multiple_of(j * tk, tk), tk)

        def scores(j):
            st = lax.dot_general(k_ref[rows(j), :], q_ref[...], NT_DIMS, preferred_element_type=F32)
            dpt = lax.dot_general(v_ref[rows(j), :], do_ref[...], NT_DIMS, preferred_element_type=F32)
            return st, dpt

        def consume(st, dpt, j):
            pt = jnp.exp2(st - lse)
            dst = (pt * (dpt - delta)).astype(BF16)
            dv_ref[rows(j), :] += jnp.dot(pt.astype(BF16), do_ref[...], preferred_element_type=F32)
            dk_ref[rows(j), :] += jnp.dot(dst, q_ref[...], preferred_element_type=F32)
            dq_s[...] += lax.dot_general(dst, k_ref[rows(j), :], TN_DIMS, preferred_element_type=F32)

        def step(j, carry):
            consume(*scores(j), j)
            return carry

        lax.fori_loop(0, nk, step, 0)
        dq_ref[...] = dq_s[...]
        if sd.on:
            sd.finish(p_ref, x_ref, sems, step_id == H * (S // tq) - 1)

    nb = (_nbytes((tq, QW), BF16) + _nbytes((tq, HP), BF16) + _nbytes((tq, HP), F32) + _nbytes((S, QW), BF16)
          + _nbytes((S, HP), BF16) + 2 * _nbytes((tq, QW), F32) + _nbytes((S, QW), F32) + _nbytes((S, HP), F32))
    return pl.pallas_call(
        body, name="flash_bwd_side" if sd.on else "flash_bwd", grid=(H, S // tq),
        in_specs=[pl.BlockSpec((tq, QW), lambda h, i: (i, h)), pl.BlockSpec((tq, HP), lambda h, i: (i, h)),
                  pl.BlockSpec((tq, HP), lambda h, i: (i, h)), pl.BlockSpec((1, 1, tq), lambda h, i: (h, 0, i)),
                  pl.BlockSpec((S, QW), lambda h, i: (0, h)), pl.BlockSpec((S, HP), lambda h, i: (0, h))] + sd.in_specs,
        out_specs=[pl.BlockSpec((tq, QW), lambda h, i: (i, h)), pl.BlockSpec((S, QW), lambda h, i: (0, h)),
                   pl.BlockSpec((S, HP), lambda h, i: (0, h))] + sd.out_specs,
        out_shape=[jax.ShapeDtypeStruct((S, H * QW), F32), jax.ShapeDtypeStruct((S, H * QW), F32),
                   jax.ShapeDtypeStruct((S, H * HP), F32)] + sd.out_shape,
        scratch_shapes=[pltpu.VMEM((tq, QW), F32)] + sd.scratch,
        compiler_params=_params(("arbitrary", "arbitrary") if sd.on else ("parallel", "arbitrary"), nb),
    )(_hbm(Q), _hbm(dO), _hbm(O), lse, _hbm(K), _hbm(V), *sd.operands)


RET_CHUNK = 256


def _ret_tables(d, lg_ref, h, C):
    ii = lax.broadcasted_iota(jnp.int32, (C, C), 0).astype(F32)
    jj = lax.broadcasted_iota(jnp.int32, (C, C), 1).astype(F32)
    ci = lax.broadcasted_iota(jnp.int32, (C, 1), 0).astype(F32)
    fwd = d == 0
    dist = jnp.where(fwd, ii - jj, jj - ii)
    mask = dist >= jnp.where(fwd, 0.0, 1.0)
    dist = jnp.maximum(dist, 0.0)
    qpos = jnp.where(fwd, ci + 1.0, C - ci)
    kpos = jnp.where(fwd, C - 1.0 - ci, ci)
    lg = lg_ref[d, h]
    D = jnp.where(mask, jnp.exp(lg * dist), 0.0)
    return D, jnp.exp(lg * qpos), jnp.exp(lg * kpos), jnp.exp(lg * C), dist, qpos, kpos


def _ret_fwd(lg, q, k, v):
    S = q.shape[0]
    C = min(RET_CHUNK, S)
    N = S // C
    H = RET_HEADS

    def chunk(d, n):
        return jnp.where(d == 0, n, N - 1 - n)

    def body(lg_ref, q_ref, k_ref, v_ref, o_ref, st_ref, state, tab_d, tab_q, tab_k):
        d, n = pl.program_id(0), pl.program_id(1)

        @pl.when(n == 0)
        def _():
            state[...] = jnp.zeros_like(state)
            for h in range(H):
                D, qw, kw, _, _, _, _ = _ret_tables(d, lg_ref, h, C)
                tab_d[h] = D
                tab_q[h] = jnp.broadcast_to(qw, (C, HP))
                tab_k[h] = jnp.broadcast_to(kw, (C, HP))

        for h in range(H):
            sl = slice(HP * h, HP * (h + 1))
            D, qw, kw = tab_d[h], tab_q[h], tab_k[h]
            gc = jnp.exp(lg_ref[d, h] * C)
            qh, kh, vh = q_ref[:, sl], k_ref[:, sl], v_ref[:, sl]
            st = state[sl, :]
            sm = lax.dot_general(qh, kh, (((1,), (1,)), ((), ())), preferred_element_type=F32) * D
            inner = jnp.dot(sm.astype(BF16), vh, preferred_element_type=F32)
            cross = qw * jnp.dot(qh, st.astype(BF16), preferred_element_type=F32)
            o_ref[:, sl] = inner + cross
            st_ref[0, 0, sl, :] = st
            kvn = lax.dot_general((kh.astype(F32) * kw).astype(BF16), vh, (((0,), (0,)), ((), ())), preferred_element_type=F32)
            state[sl, :] = gc * st + kvn

    nb = 3 * _nbytes((C, 512), BF16) + _nbytes((C, 512), F32) + 2 * _nbytes((512, HP), F32) + 8 * _nbytes((C, C), F32)
    return pl.pallas_call(
        body, name="ret_fwd", grid=(2, N),
        in_specs=[pl.BlockSpec(memory_space=pltpu.SMEM)] + [pl.BlockSpec((C, 512), lambda d, n: (chunk(d, n), 0))] * 3,
        out_specs=[pl.BlockSpec((C, 512), lambda d, n: (chunk(d, n), d)),
                   pl.BlockSpec((1, 1, 512, HP), lambda d, n: (d, chunk(d, n), 0, 0))],
        out_shape=[jax.ShapeDtypeStruct((S, 1024), F32), jax.ShapeDtypeStruct((2, N, 512, HP), F32)],
        scratch_shapes=[pltpu.VMEM((512, HP), F32), pltpu.VMEM((H, C, C), F32), pltpu.VMEM((H, C, HP), F32),
                        pltpu.VMEM((H, C, HP), F32)],
        compiler_params=_params(("arbitrary", "arbitrary"), nb),
    )(lg, _hbm(q), _hbm(k), _hbm(v))


def _ret_bwd(lg, q, k, v, do, states):
    S = q.shape[0]
    C = min(RET_CHUNK, S)
    N = S // C
    H = RET_HEADS

    def chunk(d, n):
        return jnp.where(d == 0, N - 1 - n, n)

    def body(lg_ref, q_ref, k_ref, v_ref, do_ref, st_ref, dq_ref, dk_ref, dv_ref, dlg_ref, G, accA, accB, accC,
             tab_d, tab_q, tab_k, tab_qp, tab_kp):
        d, n = pl.program_id(0), pl.program_id(1)

        @pl.when(n == 0)
        def _():
            G[...] = jnp.zeros_like(G)
            accA[...] = jnp.zeros_like(accA)
            accB[...] = jnp.zeros_like(accB)
            accC[...] = jnp.zeros_like(accC)
            for h in range(H):
                D, qw, kw, _, _, qpos, kpos = _ret_tables(d, lg_ref, h, C)
                tab_d[h] = D
                tab_q[h] = jnp.broadcast_to(qw, (C, HP))
                tab_k[h] = jnp.broadcast_to(kw, (C, HP))
                tab_qp[h] = jnp.broadcast_to(qw * qpos, (C, HP))
                tab_kp[h] = jnp.broadcast_to(kw * kpos, (C, HP))

        nt = (((1,), (1,)), ((), ()))
        tn = (((0,), (0,)), ((), ()))
        for h in range(H):
            sl = slice(HP * h, HP * (h + 1))
            D, qw, kw = tab_d[h], tab_q[h], tab_k[h]
            gc = jnp.exp(lg_ref[d, h] * C)
            qh, kh, vh, doh = q_ref[:, sl], k_ref[:, sl], v_ref[:, sl], do_ref[:, sl]
            st = st_ref[0, 0, sl, :]
            g = G[sl, :]
            stb, gb = st.astype(BF16), g.astype(BF16)
            sraw = lax.dot_general(qh, kh, nt, preferred_element_type=F32)
            dS = lax.dot_general(doh, vh, nt, preferred_element_type=F32) * D
            dSb = dS.astype(BF16)
            smb = (sraw * D).astype(BF16)
            qs = jnp.dot(qh, stb, preferred_element_type=F32)
            kg = jnp.dot(kh, gb, preferred_element_type=F32)
            dqh = jnp.dot(dSb, kh, preferred_element_type=F32) + qw * lax.dot_general(doh, stb, nt, preferred_element_type=F32)
            dkh = lax.dot_general(dSb, qh, tn, preferred_element_type=F32) + kw * lax.dot_general(vh, gb, nt, preferred_element_type=F32)
            dvh = lax.dot_general(smb, doh, tn, preferred_element_type=F32) + kw * kg
            dq_ref[:, sl] = dqh.astype(dq_ref.dtype)
            dk_ref[:, sl] = dkh.astype(dk_ref.dtype)
            dv_ref[:, sl] = dvh.astype(dv_ref.dtype)
            dof, vf = doh.astype(F32), vh.astype(F32)
            accA[h] += sraw * dS
            accB[h] += tab_qp[h] * (qs * dof) + tab_kp[h] * (kg * vf)
            accC[h] += st * g
            G[sl, :] = gc * g + lax.dot_general((qh.astype(F32) * qw).astype(BF16), doh, tn, preferred_element_type=F32)

        @pl.when(n == N - 1)
        def _():
            rows = []
            for h in range(H):
                _, _, _, gc, dist, _, _ = _ret_tables(d, lg_ref, h, C)
                tot = jnp.sum(accA[h] * dist) + jnp.sum(accB[h]) + (C * gc) * jnp.sum(accC[h])
                rows.append(jnp.broadcast_to(tot, (1, HP)))
            dlg_ref[0] = jnp.concatenate(rows + [jnp.zeros((8 - H, HP), F32)], axis=0)

    nb = (4 * _nbytes((C, 512), BF16) + 3 * _nbytes((C, 512), F32) + 2 * _nbytes((512, HP), F32)
          + H * (_nbytes((C, C), F32) + _nbytes((C, HP), F32) + _nbytes((HP, HP), F32)) + 12 * _nbytes((C, C), F32))
    row = lambda d, n: (chunk(d, n), 0)
    out3 = lambda d, n: (chunk(d, n), d)
    return pl.pallas_call(
        body, name="ret_bwd", grid=(2, N),
        in_specs=[pl.BlockSpec(memory_space=pltpu.SMEM)] + [pl.BlockSpec((C, 512), row)] * 4
        + [pl.BlockSpec((1, 1, 512, HP), lambda d, n: (d, chunk(d, n), 0, 0))],
        out_specs=[pl.BlockSpec((C, 512), out3)] * 3 + [pl.BlockSpec((1, 8, HP), lambda d, n: (d, 0, 0))],
        out_shape=[jax.ShapeDtypeStruct((S, 1024), BF16)] * 3 + [jax.ShapeDtypeStruct((2, 8, HP), F32)],
        scratch_shapes=[pltpu.VMEM((512, HP), F32), pltpu.VMEM((H, C, C), F32), pltpu.VMEM((H, C, HP), F32),
                        pltpu.VMEM((H, HP, HP), F32), pltpu.VMEM((H, C, C), F32)] + [pltpu.VMEM((H, C, HP), F32)] * 4,
        compiler_params=_params(("arbitrary", "arbitrary"), nb),
    )(lg, _hbm(q), _hbm(k), _hbm(v), _hbm(do), _hbm(states))


CONV_PAD = 16
CONV_TR = 256
CONV_CB = LANES


def _fill_padded(pad, u_ref, S):
    pad[0:CONV_PAD, :] = jnp.zeros((CONV_PAD, CONV_CB), F32)
    pad[CONV_PAD + S:CONV_PAD + S + CONV_PAD, :] = jnp.zeros((CONV_PAD, CONV_CB), F32)
    pad[CONV_PAD:CONV_PAD + S, :] = u_ref[...]


def _conv_fwd(u, w32, b):
    S, D = u.shape
    tr = min(CONV_TR, S)

    def body(u_ref, w_ref, b_ref, o_ref, pad):
        _fill_padded(pad, u_ref, S)
        wv = w_ref[...]
        bv = b_ref[...]

        def step(t, carry):
            r0 = pl.multiple_of(t * tr, tr)
            acc = jnp.broadcast_to(bv, (tr, CONV_CB))
            for k in range(CONV_KERNEL):
                acc = acc + pad[pl.ds(r0 + (k + 1), tr), :] * wv[k:k + 1, :]
            o_ref[pl.ds(r0, tr), :] = acc
            return carry

        lax.fori_loop(0, S // tr, step, 0)

    nb = 2 * _nbytes((S, CONV_CB), F32) + _nbytes((S + 2 * CONV_PAD, CONV_CB), F32)
    return pl.pallas_call(
        body, name="conv_fwd", grid=(D // CONV_CB,),
        in_specs=[pl.BlockSpec((S, CONV_CB), lambda j: (0, j)), pl.BlockSpec((32, CONV_CB), lambda j: (0, j)),
                  pl.BlockSpec((1, CONV_CB), lambda j: (0, j))],
        out_specs=pl.BlockSpec((S, CONV_CB), lambda j: (0, j)),
        out_shape=jax.ShapeDtypeStruct((S, D), F32),
        scratch_shapes=[pltpu.VMEM((S + 2 * CONV_PAD, CONV_CB), F32)],
        compiler_params=_params(("parallel",), nb),
    )(_hbm(u), w32, b)


def _conv_bwd_w(u, dout):
    S, D = u.shape
    tr = min(CONV_TR, S)

    def body(u_ref, d_ref, o_ref, pad, acc):
        _fill_padded(pad, u_ref, S)
        acc[...] = jnp.zeros_like(acc)

        def fold(a):
            return jnp.sum(a.reshape(tr // 8, 8, CONV_CB), axis=0)

        def step(t, carry):
            r0 = pl.multiple_of(t * tr, tr)
            dv = d_ref[pl.ds(r0, tr), :]
            for k in range(CONV_KERNEL):
                acc[8 * k:8 * k + 8, :] += fold(pad[pl.ds(r0 + (k + 1), tr), :] * dv)
            acc[8 * CONV_KERNEL:8 * CONV_KERNEL + 8, :] += fold(dv)
            return carry

        lax.fori_loop(0, S // tr, step, 0)
        o_ref[...] = jnp.sum(acc[...].reshape(32, 8, CONV_CB), axis=1)

    nb = 2 * _nbytes((S, CONV_CB), F32) + _nbytes((S + 2 * CONV_PAD, CONV_CB), F32)
    return pl.pallas_call(
        body, name="conv_bwd_w", grid=(D // CONV_CB,),
        in_specs=[pl.BlockSpec((S, CONV_CB), lambda j: (0, j)), pl.BlockSpec((S, CONV_CB), lambda j: (0, j))],
        out_specs=pl.BlockSpec((32, CONV_CB), lambda j: (0, j)),
        out_shape=jax.ShapeDtypeStruct((32, D), F32),
        scratch_shapes=[pltpu.VMEM((S + 2 * CONV_PAD, CONV_CB), F32), pltpu.VMEM((256, CONV_CB), F32)],
        compiler_params=_params(("parallel",), nb),
    )(_hbm(u), _hbm(dout))


def _adamw(w, g, m, v):
    shape = w.shape
    w2, g2, m2, v2 = [a.reshape(-1, shape[-1]) for a in (w, g, m, v)]
    W = shape[-1]

    def fn(rows, p):
        w_, g_, m_, v_ = [r[...] for r in rows]
        mn = ADAM_B1 * m_ + (1.0 - ADAM_B1) * g_
        vn = ADAM_B2 * v_ + (1.0 - ADAM_B2) * jnp.square(g_)
        m_hat = mn / (1.0 - ADAM_B1 ** ADAM_STEP)
        v_hat = vn / (1.0 - ADAM_B2 ** ADAM_STEP)
        delta = -ADAM_LR * (m_hat / (jnp.sqrt(v_hat) + ADAM_EPS) + ADAM_WD * w_)
        return [delta, mn, vn], []

    R2 = w2.shape[0]
    ts = 256 if (R2 > 256 and R2 % 256 == 0) else R2
    d, mn, vn = _rowcall("adamw", fn, [w2, g2, m2, v2], [], [(W, F32)] * 3, ts=ts)
    return d.reshape(shape), mn.reshape(shape), vn.reshape(shape)


_HBM = pl.BlockSpec(memory_space=pltpu.HBM)


def _place():
    return lax.axis_index("x"), lax.axis_index("y"), lax.axis_index("c")


def _half_rows(ref, half, h):
    idx = (slice(None),) * (len(ref.shape) - 2) + (pl.ds(half * h, h), slice(None))
    return ref.at[idx]


def _chip_exchange(name, p, mode):
    assert mode in ("half", "whole", "scatter")
    if mode == "half":
        h, W = p.shape[0] // 2, p.shape[1]
    elif mode == "whole":
        h, W = p.shape
    else:
        h, W = p.shape[1], p.shape[2]

    def body(p_ref, o_ref, send_sems, recv_sems, local_sem):
        copies = _chip_copies(p_ref, o_ref, send_sems, recv_sems, local_sem, mode, h)
        for cp in copies:
            cp.start()
        for cp in copies:
            cp.wait()

    return pl.pallas_call(
        body, name=name, in_specs=[_HBM], out_specs=_HBM,
        out_shape=jax.ShapeDtypeStruct((4, h, W), p.dtype), scratch_shapes=_CHIP_SEMS,
    )(p)


_CHIP_SEMS = [pltpu.SemaphoreType.DMA((3,)), pltpu.SemaphoreType.DMA((3,)), pltpu.SemaphoreType.DMA]


def _chip_copies(p_ref, o_ref, send_sems, recv_sems, local_sem, mode, h):
    x, y, c = _place()
    k_me = 2 * x + y

    def src(k):
        if mode == "half":
            return _half_rows(p_ref, c, h)
        return p_ref if mode == "whole" else p_ref.at[k]

    copies = [pltpu.make_async_copy(src(k_me), o_ref.at[k_me], local_sem)]
    for j, (cx, cy) in enumerate([(1 - x, y), (x, 1 - y), (1 - x, 1 - y)]):
        copies.append(pltpu.make_async_remote_copy(
            src_ref=src(2 * cx + cy), dst_ref=o_ref.at[k_me], send_sem=send_sems.at[j], recv_sem=recv_sems.at[j],
            device_id=(cx, cy, c), device_id_type=MESH))
    return copies


def _sibling_send_half(name, g):
    n, R, W = g.shape
    h = R // 2

    def body(g_ref, o_ref, send_sem, recv_sem):
        x, y, c = _place()
        cp = pltpu.make_async_remote_copy(
            src_ref=_half_rows(g_ref, 1 - c, h), dst_ref=o_ref, send_sem=send_sem, recv_sem=recv_sem,
            device_id=(x, y, 1 - c), device_id_type=MESH)
        cp.start()
        cp.wait()

    return pl.pallas_call(
        body, name=name, in_specs=[_HBM], out_specs=_HBM,
        out_shape=jax.ShapeDtypeStruct((n, h, W), g.dtype),
        scratch_shapes=[pltpu.SemaphoreType.DMA, pltpu.SemaphoreType.DMA],
    )(g)


def _halves_merge(name, mine):
    def body(m_ref, o_ref, send_sem, recv_sem):
        x, y, c = _place()
        cp = pltpu.make_async_remote_copy(
            src_ref=m_ref, dst_ref=o_ref, send_sem=send_sem, recv_sem=recv_sem,
            device_id=(x, y, 1 - c), device_id_type=MESH)
        cp.start()
        cp.wait()

    theirs = pl.pallas_call(
        body, name=name, in_specs=[_HBM], out_specs=_HBM,
        out_shape=jax.ShapeDtypeStruct(mine.shape, mine.dtype),
        scratch_shapes=[pltpu.SemaphoreType.DMA, pltpu.SemaphoreType.DMA],
    )(mine)
    south = lax.axis_index("c") == 0
    axis = mine.ndim - 2
    return jnp.concatenate([jnp.where(south, mine, theirs), jnp.where(south, theirs, mine)], axis=axis)


def _add2(name, a, b, out_dtype):
    def fn(rows, p):
        return [rows[0][...].astype(F32) + rows[1][...].astype(F32)], []

    return _rowcall(name, fn, [a, b], [], [(a.shape[1], out_dtype)], ts=4 * PACK_TILE)[0]


def _sum4(name, b):
    _, h, W = b.shape
    ts = 2 * PACK_TILE if h % (2 * PACK_TILE) == 0 else (PACK_TILE if h % PACK_TILE == 0 else h)

    def body(b_ref, o_ref):
        o_ref[...] = ((b_ref[0].astype(F32) + b_ref[1].astype(F32)) + b_ref[2].astype(F32)) + b_ref[3].astype(F32)

    nb = _nbytes((4, ts, W), b.dtype) + _nbytes((ts, W), F32)
    return pl.pallas_call(
        body, name=name, grid=(h // ts,),
        in_specs=[pl.BlockSpec((4, ts, W), lambda i: (0, i, 0))], out_specs=pl.BlockSpec((ts, W), lambda i: (i, 0)),
        out_shape=jax.ShapeDtypeStruct((h, W), F32), compiler_params=_params(("parallel",), nb),
    )(b)


PACK_W = 1024
PACK_ROWS = 16
PACK_TILE = 128
EV_SHARDED = (("ev_w_in", "col", False), ("ev_w_uq", "col", False), ("ev_w_ukv", "col", False), ("ev_w_out", "row", False))
OD_SHARDED = (("od_w_in", "col", False), ("od_b_in", "col", True), ("od_dw_w", "col", True), ("od_dw_b", "col", True),
              ("od_ln_g", "col", True), ("od_ln_b", "col", True), ("od_w_out", "row", False))
REPLICATED =("ada_b", "pre_g", "post_g", "ev_dec_f", "ev_dec_b", "ev_q_norm_g", "ev_kv_norm_g")
WEIGHTS = ("ada_w", "ada_b", "pre_g", "post_g", "ev_w_in", "ev_dec_f", "ev_dec_b", "ev_q_norm_g", "ev_w_uq", "ev_kv_norm_g",
           "ev_w_ukv", "ev_w_out", "od_w_in", "od_b_in", "od_dw_w", "od_dw_b", "od_ln_g", "od_ln_b", "od_w_out")


def _layer_entries(layers):
    out = []
    for l in layers:
        out += [(name, how, exact, l // 2) for name, how, exact in (EV_SHARDED if l % 2 == 0 else OD_SHARDED)]
    return out


def _rows_of(n):
    unit = PACK_W * PACK_ROWS
    return (n + unit - 1) // unit * PACK_ROWS


def _to_rows(flat, lead):
    n = flat.shape[-1]
    r = _rows_of(n)
    flat = jnp.pad(flat, [(0, 0)] * len(lead) + [(0, r * PACK_W - n)])
    return flat.reshape(*lead, r, PACK_W)


def _pad_total(buf, axis):
    r = buf.shape[axis]
    r2 = (r + 2 * PACK_TILE - 1) // (2 * PACK_TILE) * (2 * PACK_TILE)
    pads = [(0, 0)] * buf.ndim
    pads[axis] = (0, r2 - r)
    return jnp.pad(buf, pads)


def _split_chips(full, how):
    if how == "col":
        n = full.shape[-1] // 4
        return jnp.moveaxis(full.reshape(*full.shape[:-1], 4, n), -2, 0)
    n = full.shape[-2] // 4
    return jnp.moveaxis(full.reshape(*full.shape[:-2], 4, n, full.shape[-1]), -3, 0)


def _join_chips(blocks, how):
    if how == "col":
        t = jnp.moveaxis(blocks, 0, -2)
        return t.reshape(*t.shape[:-2], t.shape[-2] * t.shape[-1])
    t = jnp.moveaxis(blocks, 0, -3)
    return t.reshape(*t.shape[:-3], t.shape[-3] * t.shape[-2], t.shape[-1])


def _pack_weights(w, entries):
    parts = []
    for name, _, exact, idx in entries:
        a = w[name][idx].reshape(-1)
        a = lax.bitcast_convert_type(a, BF16).reshape(-1) if exact else a.astype(BF16)
        parts.append(_to_rows(a, ()))
    return _pad_total(jnp.concatenate(parts, axis=0), 0)


def _unpack_weights(gathered, w, entries):
    out, r0 = {}, 0
    for name, how, exact, idx in entries:
        shp = w[name].shape[1:]
        n = (2 if exact else 1) * w[name][idx].size
        r = _rows_of(n)
        a = gathered[:, r0:r0 + r].reshape(4, -1)[:, :n]
        if exact:
            a = lax.bitcast_convert_type(a.reshape(4, -1, 2), F32)
        out[(name, idx)] = _join_chips(a.reshape(4, *shp), how)
        r0 += r
    return out


def _pack_grads(grads, entries, replicated):
    parts = []
    for name, how, _, idx in entries:
        b = _split_chips(grads[(name, idx)], how)
        parts.append(_to_rows(b.reshape(4, -1).astype(BF16), (4,)))
    for name in replicated:
        a = _to_rows(grads[name].reshape(-1).astype(BF16), ())
        parts.append(jnp.broadcast_to(a[None], (4, *a.shape)))
    return _pad_total(jnp.concatenate(parts, axis=1), 1)


def _reduce_begin(tag, g):
    h = g.shape[1] // 2
    c = lax.axis_index("c")
    theirs = _sibling_send_half("reduce_cores_" + tag, g)
    mine = lax.dynamic_slice_in_dim(g, c * h, h, axis=1)
    return _add2("reduce_add_" + tag, mine.reshape(4 * h, PACK_W), theirs.reshape(4 * h, PACK_W), BF16).reshape(4, h, PACK_W)


def _reduce_end(tag, exchanged):
    return _halves_merge("reduce_merge_" + tag, _sum4("reduce_sum_" + tag, exchanged))


def _unpack_grads(total, w, entries, replicated):
    out, r0 = {}, 0
    for name, _, _, idx in entries:
        n = w[name][idx].size
        r = _rows_of(n)
        out[(name, idx)] = total[r0:r0 + r].reshape(-1)[:n].reshape(w[name].shape[1:])
        r0 += r
    for name in replicated:
        n = w[name].size
        r = _rows_of(n)
        out[name] = total[r0:r0 + r].reshape(-1)[:n].reshape(w[name].shape)
        r0 += r
    return out


def _pad_heads(a, n, w, to):
    lead = a.shape[:-1]
    return jnp.pad(a.reshape(*lead, n, w), [(0, 0)] * (len(lead) + 1) + [(0, to - w)]).reshape(*lead, n * to)


def _unpad_heads(a, n, w, to):
    lead = a.shape[:-1]
    return a.reshape(*lead, n, to)[..., :w].reshape(*lead, n * w)


def _w_in_pad(w):
    return jnp.concatenate([_pad_heads(w[:, 0:256], 4, 64, HP), _pad_heads(w[:, 256:512], 4, 64, HP), w[:, 512:1536],
                            w[:, 2240:2752], w[:, 1536:2176], _pad_heads(w[:, 2176:2240], 1, 64, HP)], axis=1)


def _w_in_unpad(g):
    return jnp.concatenate([_unpad_heads(g[:, Z_RQ:Z_RK], 4, 64, HP), _unpad_heads(g[:, Z_RK:Z_RV], 4, 64, HP),
                            g[:, Z_RV:Z_MG], g[:, Z_CQ:Z_KR], g[:, Z_KR:Z_KR + 64], g[:, Z_MG:Z_CQ]], axis=1)


def _layer_weights(l, full, w):
    f32 = lambda a: a.astype(F32)
    i = l // 2
    common = dict(bias=f32(w["ada_b"][l])[None], pre_g=f32(w["pre_g"][l])[None], post_g=f32(w["post_g"][l])[None])
    if l % 2 == 0:
        w_uq_p = _pad_heads(full[("ev_w_uq", i)], MLA_HEADS, MLA_NOPE_DIM + MLA_ROPE_DIM, QW)
        lg = jnp.stack([jax.nn.log_sigmoid(f32(w["ev_dec_f"][i])), jax.nn.log_sigmoid(f32(w["ev_dec_b"][i]))])
        return dict(common, w_in=_w_in_pad(full[("ev_w_in", i)]), w_uq=w_uq_p, w_ukv=full[("ev_w_ukv", i)],
                    w_out=full[("ev_w_out", i)], qg=f32(w["ev_q_norm_g"][i])[None], kvg=f32(w["ev_kv_norm_g"][i])[None], lg=lg)
    dw = f32(full[("od_dw_w", i)])
    return dict(common, w_in=full[("od_w_in", i)], w_out=full[("od_w_out", i)], b_in=f32(full[("od_b_in", i)])[None],
                dw=jnp.pad(dw, ((0, 1), (0, 0))), dw_flip=jnp.pad(dw[::-1], ((0, 1), (0, 0))),
                dw_b=f32(full[("od_dw_b", i)])[None], ln_g=f32(full[("od_ln_g", i)])[None], ln_b=f32(full[("od_ln_b", i)])[None])


def _rope_tables(positions):
    inv_freq = ROPE_BASE ** (-jnp.arange(0, ROPE_DIM, 2, dtype=F32) / ROPE_DIM)
    ang = positions.astype(F32)[:, None] * inv_freq
    z = jnp.zeros((ang.shape[0], HP - ROPE_DIM), F32)
    return jnp.concatenate([jnp.cos(ang), jnp.cos(ang), z], axis=1), jnp.concatenate([jnp.sin(ang), jnp.sin(ang), z], axis=1)


def _even_fwd(x, mod, cos, sin, w, side=None):
    bias, pre_g, post_g = w["bias"], w["pre_g"], w["post_g"]
    h, z = _pre_mm("ev_in", x, pre_g, mod, bias, w["w_in"])
    rq, rk, rv, qn, kvn, Q, K, V = _ev_mid_fwd(z, cos, sin, w["qg"], w["kvg"], w["w_uq"], w["w_ukv"])
    a, lse, *got = _flash_fwd(Q, K, V, side)
    o2, states = _ret_fwd(w["lg"], rq, rk, rv)
    mix, y, x_new = _ev_out_fwd(o2, z, a, x, w["w_out"], post_g, mod, bias)
    saved = dict(x=x, h=h, z=z, rq=rq, rk=rk, rv=rv, qn=qn, kvn=kvn, Q=Q, K=K, V=V, a=a, lse=lse, o2=o2,
                 states=states, mix=mix, y=y)
    return x_new, saved, (got[0] if got else None)


def _even_bwd(dx, s, mod, cos, sin, w, side=None):
    bias, pre_g, post_g = w["bias"], w["pre_g"], w["post_g"]
    dy, do, drg, da, dmg, dpost_g, dgate = _ev_out_bwd(s["y"], dx, s["o2"], s["z"], s["a"], w["w_out"], post_g, mod, bias)
    dw_out = _mm_tn("ev_out_dw", s["mix"], dy, BF16)
    drq, drk, drv, dlg = _ret_bwd(w["lg"], s["rq"], s["rk"], s["rv"], do, s["states"])
    dQ, dK, dV, *got = _flash_bwd(s["Q"], s["K"], s["V"], da, s["a"], s["lse"], side)
    dqf, dkv, dqn, dkvn, dkr = _ev_mid_bwd(dQ, dK, dV, cos, sin, w["w_uq"], w["w_ukv"])
    dw_uq = _mm_tn("ev_uq_dw", s["qn"], dqf, BF16)
    dw_ukv = _mm_tn("ev_ukv_dw", s["kvn"], dkv, BF16)
    dz, dx_new, dqg, dkvg, dpre_g, dscale, dshift = _ev_in_bwd(
        s["z"], cos, sin, w["qg"], w["kvg"], drq, drk, drv, drg, dqn, dkvn, dkr, dmg, s["x"], dx, w["w_in"], pre_g, mod, bias)
    dw_in = _mm_tn("ev_in_dw", s["h"], dz, BF16)
    g = dict(ev_w_in=_w_in_unpad(dw_in), ev_w_uq=_unpad_heads(dw_uq, MLA_HEADS, MLA_NOPE_DIM + MLA_ROPE_DIM, QW),
             ev_w_ukv=dw_ukv, ev_w_out=dw_out, ev_q_norm_g=dqg[0], ev_kv_norm_g=dkvg[0],
             dlg_f=dlg[0, :RET_HEADS, 0], dlg_b=dlg[1, :RET_HEADS, 0])
    return dx_new, g, dpre_g[0], dpost_g[0], jnp.concatenate([dshift, dscale, dgate], axis=1), (got[0] if got else None)


def _odd_fwd(x, mod, w, tgt=None):
    bias, pre_g, post_g = w["bias"], w["pre_g"], w["post_g"]
    h, z, u = _od_in_fwd(x, pre_g, mod, bias, w["w_in"], w["b_in"])
    uc = _conv_fwd(u, w["dw"], w["dw_b"])
    vv, y, *rest = _od_out_fwd(uc, z, x, w["w_out"], w["b_in"], w["ln_g"], w["ln_b"], post_g, mod, bias, tgt)
    return (rest[0] if tgt is None else tuple(rest)), dict(x=x, h=h, z=z, u=u, uc=uc, vv=vv, y=y)


def _odd_bwd(dx, s, mod, w):
    bias, pre_g, post_g = w["bias"], w["pre_g"], w["post_g"]
    dy, duc, dg, dpost_g, dgate, dbg, dln_g, dln_b = _od_out_bwd(
        s["y"], dx, s["uc"], s["z"], w["w_out"], w["b_in"], w["ln_g"], w["ln_b"], post_g, mod, bias)
    dw_out = _mm_tn("od_out_dw", s["vv"], dy, BF16)
    du = _conv_fwd(duc, w["dw_flip"], jnp.zeros_like(w["dw_b"]))
    dwb = _conv_bwd_w(s["u"], duc)
    dz, dx_new, dba, dbb, dpre_g, dscale, dshift = _od_in_bwd(s["z"], w["b_in"], du, dg, s["x"], dx, w["w_in"], pre_g, mod, bias)
    dw_in = _mm_tn("od_in_dw", s["h"], dz, BF16)
    g = dict(od_w_in=dw_in, od_b_in=jnp.concatenate([dba, dbb, dbg], axis=1)[0], od_dw_w=dwb[:CONV_KERNEL], od_dw_b=dwb[CONV_KERNEL],
             od_ln_g=dln_g[0], od_ln_b=dln_b[0], od_w_out=dw_out)
    return dx_new, g, dpre_g[0], dpost_g[0], jnp.concatenate([dshift, dscale, dgate], axis=1)


LATE_LAYERS = tuple(range(1, DEPTH))


def _step(x, mod, positions, loss_target, w):
    cos, sin = _rope_tables(positions)
    f32 = lambda a: a.astype(F32)
    mods = [mod[l:l + 1] for l in range(DEPTH)]
    first, late = _layer_entries([0]), _layer_entries(LATE_LAYERS)

    gathered = _halves_merge("gather_merge_first", _chip_exchange("gather_chips_first", _pack_weights(w, first), "half"))
    full = _unpack_weights(gathered, w, first)
    lw = {0: _layer_weights(0, full, w)}
    x, s0, got = _even_fwd(x, mods[0], cos, sin, lw[0], side=(_pack_weights(w, late), "half"))
    full.update(_unpack_weights(_halves_merge("gather_merge_late", got), w, late))
    saved = [s0]
    assert DEPTH % 2 == 0
    for l in LATE_LAYERS:
        lw[l] = _layer_weights(l, full, w)
        if l % 2 == 0:
            x, s, _ = _even_fwd(x, mods[l], cos, sin, lw[l])
        else:
            x, s = _odd_fwd(x, mods[l], lw[l], loss_target if l == DEPTH - 1 else None)
        saved.append(s)
    dx, loss = x

    grads = {}
    dpre, dpost, dmod = [None] * DEPTH, [None] * DEPTH, [None] * DEPTH
    for l in reversed(LATE_LAYERS):
        if l % 2 == 0:
            dx, g, dpre[l], dpost[l], dmod[l], _ = _even_bwd(dx, saved[l], mods[l], cos, sin, lw[l])
        else:
            dx, g, dpre[l], dpost[l], dmod[l] = _odd_bwd(dx, saved[l], mods[l], lw[l])
        grads.update({(name, l // 2): val for name, val in g.items()})
    chip_sum = _reduce_begin("late", _pack_grads(grads, late, ()))
    dx, g, dpre[0], dpost[0], dmod[0], got = _even_bwd(dx, saved[0], mods[0], cos, sin, lw[0], side=(chip_sum, "scatter"))
    grads.update({(name, 0): val for name, val in g.items()})
    red = _unpack_grads(_reduce_end("late", got), w, late, ())

    n_ev = (DEPTH + 1) // 2
    grads.update(pre_g=jnp.stack(dpre), post_g=jnp.stack(dpost), ada_b=jnp.concatenate(dmod, axis=0))
    for name in ("ev_q_norm_g", "ev_kv_norm_g"):
        grads[name] = jnp.stack([grads[(name, i)] for i in range(n_ev)])
    grads["ev_dec_f"] = jnp.stack([grads[("dlg_f", i)] for i in range(n_ev)]) * jax.nn.sigmoid(-f32(w["ev_dec_f"]))
    grads["ev_dec_b"] = jnp.stack([grads[("dlg_b", i)] for i in range(n_ev)]) * jax.nn.sigmoid(-f32(w["ev_dec_b"]))
    chip_sum = _reduce_begin("first", _pack_grads(grads, first, REPLICATED))
    total = _reduce_end("first", _chip_exchange("reduce_chips_first", chip_sum, "scatter"))
    red.update(_unpack_grads(total, w, first, REPLICATED))

    out = {name: red[name] for name in REPLICATED}
    for name, _, _ in EV_SHARDED:
        out[name] = jnp.stack([red[(name, i)] for i in range(n_ev)])
    for name, _, _ in OD_SHARDED:
        out[name] = jnp.stack([red[(name, i)] for i in range(DEPTH // 2)])
    return loss[0, 0], dx, out, grads["ada_b"]


ADA_ROWS = 16
ADA_COLS = 3 * D_MODEL // 4
ADA_PACK = (8, 1024)


def _ada_pack(a):
    n = ADA_PACK[0] * ADA_PACK[1]
    return jnp.pad(a, ((0, 0), (0, n - a.shape[1]))).reshape(4, *ADA_PACK)


def _ada_unpack(a):
    return a.reshape(4, -1)[:, :DEPTH * ADA_COLS]


def _ada_forward(c, ada_w):
    D = D_MODEL
    c16 = jnp.zeros((ADA_ROWS, D), F32).at[0].set(c[0])
    c_act = _rowcall("silu_c", lambda rows, p: ([_silu(rows[0][...])], []), [c16], [], [(D, BF16)])[0]
    got = _chip_exchange("ada_c", c_act, "whole")
    c4 = jnp.pad(got[:, 0, :], ((0, ADA_ROWS - 4), (0, 0)))
    per_layer = [_mm_nn("ada", c4, ada_w[l].astype(BF16))[:4] for l in range(DEPTH)]
    p = jnp.stack(per_layer, axis=1).reshape(4, DEPTH * ADA_COLS)
    got = _chip_exchange("ada_mod", _ada_pack(p), "scatter")
    mod = _ada_unpack(got).reshape(4, DEPTH, ADA_COLS).transpose(1, 0, 2).reshape(DEPTH, 3 * D)
    return mod, c4


def _ada_backward(dmod, c4):
    p = dmod.reshape(DEPTH, 4, ADA_COLS).transpose(1, 0, 2).reshape(4, DEPTH * ADA_COLS)
    got = _chip_exchange("ada_dmod", _ada_pack(p), "scatter")
    dm4 = jnp.pad(_ada_unpack(got), ((0, ADA_ROWS - 4), (0, 0)))
    both = _halves_merge("ada_merge", jnp.concatenate([dm4, c4.astype(F32)], axis=1))
    dm32, c32 = both[:, :DEPTH * ADA_COLS].astype(BF16), both[:, DEPTH * ADA_COLS:].astype(BF16)
    return jnp.stack([_mm_tn("ada_dw", c32, dm32[:, l * ADA_COLS:(l + 1) * ADA_COLS]) for l in range(DEPTH)])


def kernel(x, c, positions, ada_w, ada_b, pre_g, post_g, ev_w_in, ev_dec_f, ev_dec_b, ev_q_norm_g, ev_w_uq, ev_kv_norm_g, ev_w_ukv, ev_w_out, od_w_in, od_b_in, od_dw_w, od_dw_b, od_ln_g, od_ln_b, od_w_out, loss_target, m_ada_w, m_ada_b, m_pre_g, m_post_g, m_ev_w_in, m_ev_dec_f, m_ev_dec_b, m_ev_q_norm_g, m_ev_w_uq, m_ev_kv_norm_g, m_ev_w_ukv, m_ev_w_out, m_od_w_in, m_od_b_in, m_od_dw_w, m_od_dw_b, m_od_ln_g, m_od_ln_b, m_od_w_out, v_ada_w, v_ada_b, v_pre_g, v_post_g, v_ev_w_in, v_ev_dec_f, v_ev_dec_b, v_ev_q_norm_g, v_ev_w_uq, v_ev_kv_norm_g, v_ev_w_ukv, v_ev_w_out, v_od_w_in, v_od_b_in, v_od_dw_w, v_od_dw_b, v_od_ln_g, v_od_ln_b, v_od_w_out):
    w = dict(ada_w=ada_w, ada_b=ada_b, pre_g=pre_g, post_g=post_g, ev_w_in=ev_w_in, ev_dec_f=ev_dec_f, ev_dec_b=ev_dec_b,
             ev_q_norm_g=ev_q_norm_g, ev_w_uq=ev_w_uq, ev_kv_norm_g=ev_kv_norm_g, ev_w_ukv=ev_w_ukv, ev_w_out=ev_w_out,
             od_w_in=od_w_in, od_b_in=od_b_in, od_dw_w=od_dw_w, od_dw_b=od_dw_b, od_ln_g=od_ln_g, od_ln_b=od_ln_b, od_w_out=od_w_out)
    m = dict(ada_w=m_ada_w, ada_b=m_ada_b, pre_g=m_pre_g, post_g=m_post_g, ev_w_in=m_ev_w_in, ev_dec_f=m_ev_dec_f, ev_dec_b=m_ev_dec_b,
             ev_q_norm_g=m_ev_q_norm_g, ev_w_uq=m_ev_w_uq, ev_kv_norm_g=m_ev_kv_norm_g, ev_w_ukv=m_ev_w_ukv, ev_w_out=m_ev_w_out,
             od_w_in=m_od_w_in, od_b_in=m_od_b_in, od_dw_w=m_od_dw_w, od_dw_b=m_od_dw_b, od_ln_g=m_od_ln_g, od_ln_b=m_od_ln_b, od_w_out=m_od_w_out)
    v = dict(ada_w=v_ada_w, ada_b=v_ada_b, pre_g=v_pre_g, post_g=v_post_g, ev_w_in=v_ev_w_in, ev_dec_f=v_ev_dec_f, ev_dec_b=v_ev_dec_b,
             ev_q_norm_g=v_ev_q_norm_g, ev_w_uq=v_ev_w_uq, ev_kv_norm_g=v_ev_kv_norm_g, ev_w_ukv=v_ev_w_ukv, ev_w_out=v_ev_w_out,
             od_w_in=v_od_w_in, od_b_in=v_od_b_in, od_dw_w=v_od_dw_w, od_dw_b=v_od_dw_b, od_ln_g=v_od_ln_g, od_ln_b=v_od_ln_b, od_w_out=v_od_w_out)

    mod, c4 = _ada_forward(c, ada_w)
    loss_local, grad_x, g, dmod = _step(x[0], mod, positions[0], loss_target[0], w)
    loss = lax.psum(loss_local, ("x", "y", "c"))
    g["ada_w"] = _ada_backward(dmod, c4)
    delta, new_m, new_v = {}, {}, {}
    for name in WEIGHTS:
        delta[name], new_m[name], new_v[name] = _adamw(w[name], g[name], m[name], v[name])
    return (loss, grad_x[None], *[g[n] for n in WEIGHTS], *[delta[n] for n in WEIGHTS],
            *[new_m[n] for n in WEIGHTS], *[new_v[n] for n in WEIGHTS])
```

```python
import functools

import jax
import jax.numpy as jnp
from jax import lax
from jax.experimental import pallas as pl
from jax.experimental.pallas import tpu as pltpu

F32 = jnp.float32
BF16 = jnp.bfloat16

D_MODEL = 1024
DEPTH = 4
RET_HEADS = 4
RET_QK_DIM = 64
RET_V_DIM = 128
MLA_HEADS = 4
MLA_Q_RANK = 384
MLA_KV_RANK = 256
MLA_NOPE_DIM = 128
MLA_ROPE_DIM = 64
MLA_V_DIM = 128
ROPE_DIM = 64
ROPE_BASE = 10000.0
CONV_KERNEL = 31
EPS = 1e-6
ADAM_LR, ADAM_B1, ADAM_B2, ADAM_EPS, ADAM_WD, ADAM_STEP = 0.001, 0.9, 0.999, 1e-08, 0.01, 10

LANES = 128
HP = 128
Z_RQ, Z_RK, Z_RV, Z_RG, Z_MG, Z_CQ, Z_CKV, Z_KR, Z_END = 0, 512, 1024, 1536, 2048, 2560, 2944, 3200, 3328
Z_LAT = Z_END - Z_CQ


def _z_ret(z):
    return (z, Z_RG, 0)


def _z_gates(z):
    return [(z, 512, Z_RG // 512), (z, 512, Z_MG // 512)]


def _z_latents(z):
    return [(z, 256, Z_CQ // 256 + i) for i in range(Z_LAT // 256)]
V7X_VMEM_BYTES = 64 * 1024 * 1024
VMEM_CAP = V7X_VMEM_BYTES - 8 * 1024 * 1024

MESH = pl.DeviceIdType.MESH


def _nbytes(shape, dtype):
    n = 1
    for s in shape:
        n *= s
    return n * jnp.dtype(dtype).itemsize


def _hbm(x):
    return pltpu.with_memory_space_constraint(x, pltpu.HBM)


def _params(sem, block_bytes):
    limit = min(VMEM_CAP, max(32 * 1024 * 1024, 2 * block_bytes + 16 * 1024 * 1024))
    return pltpu.CompilerParams(dimension_semantics=sem, vmem_limit_bytes=limit)


ROW_TILE = 512


def _rowcall(name, fn, rows, params, outs, accs=(), ts=ROW_TILE, mats=()):
    rows = [r if isinstance(r, tuple) else (r, r.shape[1], 0) for r in rows]
    S = rows[0][0].shape[0]
    ts = min(ts, S)
    assert S % ts == 0, (name, S, ts)
    nr, npar, nm, no, na = len(rows), len(params), len(mats), len(outs), len(accs)

    def body(*refs):
        row_refs = refs[:nr]
        pvals = [r[...] for r in refs[nr:nr + npar]]
        mat_refs = refs[nr + npar:nr + npar + nm]
        out_refs = refs[nr + npar + nm:nr + npar + nm + no]
        acc_refs = refs[nr + npar + nm + no:]
        ovals, avals = fn(row_refs, pvals, mat_refs) if nm else fn(row_refs, pvals)
        for r, v in zip(out_refs, ovals, strict=True):
            r[...] = v.astype(r.dtype)
        if na:
            @pl.when(pl.program_id(0) == 0)
            def _():
                for r in acc_refs:
                    r[...] = jnp.zeros_like(r)
            for r, v in zip(acc_refs, avals, strict=True):
                r[...] += v

    in_specs = [pl.BlockSpec((ts, w), functools.partial(lambda i, b: (i, b), b=blk)) for _, w, blk in rows]
    in_specs += [pl.BlockSpec(p.shape, lambda i: (0, 0)) for p in [*params, *mats]]
    out_specs = [pl.BlockSpec((ts, w), lambda i: (i, 0)) for w, _ in outs]
    out_specs += [pl.BlockSpec(s, lambda i: (0, 0)) for s in accs]
    out_shape = [jax.ShapeDtypeStruct((S, w), dt) for w, dt in outs]
    out_shape += [jax.ShapeDtypeStruct(s, F32) for s in accs]
    nb = sum(_nbytes((ts, w), a.dtype) for a, w, _ in rows) + sum(_nbytes((ts, w), dt) for w, dt in outs)
    nb += sum(_nbytes(p.shape, p.dtype) for p in params) + sum(_nbytes(s, F32) for s in accs)
    res = pl.pallas_call(
        body, name=name, grid=(S // ts,), in_specs=in_specs, out_specs=out_specs, out_shape=out_shape,
        compiler_params=_params(("arbitrary",) if na else ("parallel",), 3 * nb + sum(_nbytes(m.shape, m.dtype) for m in mats)),
    )(*[_hbm(a) for a, _, _ in rows], *params, *[_hbm(m) for m in mats])
    return res


def _silu(x):
    return x * jax.nn.sigmoid(x)


def _rms(x, g):
    return x * lax.rsqrt(jnp.mean(x * x, axis=-1, keepdims=True) + EPS) * g


def _rot(x):
    lane = lax.broadcasted_iota(jnp.int32, x.shape, 1)
    return jnp.where(lane < ROPE_DIM // 2, -pltpu.roll(x, LANES - ROPE_DIM // 2, 1), pltpu.roll(x, ROPE_DIM // 2, 1))


def _rope(x, cos, sin):
    return x * cos + _rot(x) * sin


def _rope_t(dy, cos, sin):
    return dy * cos - _rot(dy) * sin


def _groups(ref, start, n):
    return [ref[:, start + HP * h:start + HP * (h + 1)] for h in range(n)]


def _pre_math(x, g, m_scale, b_scale, m_shift, b_shift):
    return _rms(x, g) * (1.0 + (m_scale + b_scale)) + (m_shift + b_shift)


def _post_math(x, y, g, m_gate, b_gate):
    return x + (m_gate + b_gate) * _rms(y, g)


def _ev_mid_fwd(z, cos, sin, qg, kvg, w_uq, w_ukv):
    qscale = ATT_SCALE * LOG2E

    def fn(rows, p, mats):
        z_, l0, l1, l2, c_, s_ = rows
        qg_, kvg_ = p
        cos_, sin_ = c_[...], s_[...]
        lat = jnp.concatenate([l0[...], l1[...], l2[...]], axis=1)
        rq = jnp.concatenate([_rope(v[...], cos_, sin_) for v in _groups(z_, Z_RQ, RET_HEADS)], axis=1)
        rk = jnp.concatenate([_rope(v[...], cos_, sin_) for v in _groups(z_, Z_RK, RET_HEADS)], axis=1)
        rk = rk * (RET_QK_DIM ** -0.5)
        rv = z_[:, Z_RV:Z_RG]
        qn = _rms(lat[:, :Z_CKV - Z_CQ], qg_).astype(BF16)
        kvn = _rms(lat[:, Z_CKV - Z_CQ:Z_KR - Z_CQ], kvg_).astype(BF16)
        kr = _rope(lat[:, Z_KR - Z_CQ:], cos_, sin_)
        qf = jnp.dot(qn, mats[0][...], preferred_element_type=F32)
        kv = jnp.dot(kvn, mats[1][...], preferred_element_type=F32)
        qs, ks, vs = [], [], []
        for h in range(MLA_HEADS):
            b = 2 * HP * h
            qs += [qf[:, b:b + HP] * qscale, _rope(qf[:, b + HP:b + 2 * HP], cos_, sin_) * qscale]
            ks += [kv[:, b:b + HP], kr]
            vs += [kv[:, b + HP:b + 2 * HP]]
        return [rq, rk, rv, qn, kvn, jnp.concatenate(qs, axis=1), jnp.concatenate(ks, axis=1), jnp.concatenate(vs, axis=1)], []

    return _rowcall("ev_mid_fwd", fn, [_z_ret(z), *_z_latents(z), cos, sin], [qg, kvg],
                    [(512, BF16), (512, BF16), (512, BF16), (MLA_Q_RANK, BF16), (MLA_KV_RANK, BF16),
                     (1024, BF16), (1024, BF16), (512, BF16)], mats=[w_uq, w_ukv])


def _pre_bwd_math(dh, x, res, g, mod, bias):
    D = D_MODEL
    _, vjp = jax.vjp(_pre_math, x, g, mod[:, D:2 * D], bias[:, D:2 * D], mod[:, :D], bias[:, :D])
    dx, dg, dsc, _, dsh, _ = vjp(dh)
    return dx + res, dg, dsc, dsh


def _ev_in_bwd(z, cos, sin, qg, kvg, drq, drk, drv, drg, dqn, dkvn, dkr, dmg, x, dx_res, w_in, g, mod, bias):
    D = D_MODEL

    def fn(rows, p, mats):
        l0, l1, l2, c_, s_, drq_, drk_, drv_, drg_, dqn_, dkvn_, dkr_, dmg_, x_, res_ = rows
        qg_, kvg_, g_, mod_, b_ = p
        cos_, sin_ = c_[...], s_[...]
        lat = jnp.concatenate([l0[...], l1[...], l2[...]], axis=1)
        parts = []
        for h in range(RET_HEADS):
            t = drq_[:, HP * h:HP * (h + 1)].astype(F32) + drq_[:, 512 + HP * h:512 + HP * (h + 1)].astype(F32)
            parts.append(_rope_t(t, cos_, sin_))
        for h in range(RET_HEADS):
            t = drk_[:, HP * h:HP * (h + 1)].astype(F32) + drk_[:, 512 + HP * h:512 + HP * (h + 1)].astype(F32)
            parts.append(_rope_t(t, cos_, sin_) * (RET_QK_DIM ** -0.5))
        parts.append(drv_[:, :512].astype(F32) + drv_[:, 512:].astype(F32))
        parts += [drg_[...], dmg_[...]]
        _, vq = jax.vjp(_rms, lat[:, :Z_CKV - Z_CQ], qg_)
        dcq, dqg = vq(dqn_[...])
        _, vkv = jax.vjp(_rms, lat[:, Z_CKV - Z_CQ:Z_KR - Z_CQ], kvg_)
        dckv, dkvg = vkv(dkvn_[...])
        parts += [dcq, dckv, _rope_t(dkr_[...], cos_, sin_)]
        dz = jnp.concatenate([v.astype(BF16) for v in parts], axis=1)
        dh = lax.dot_general(dz, mats[0][...], NT_DIMS, preferred_element_type=F32)
        dx, dg, dsc, dsh = _pre_bwd_math(dh, x_[...], res_[...], g_, mod_, b_)
        return [dz, dx], [dqg, dkvg, dg, dsc, dsh]

    rows = [*_z_latents(z), cos, sin, drq, drk, drv, drg, dqn, dkvn, dkr, dmg, x, dx_res]
    return _rowcall("ev_in_bwd", fn, rows, [qg, kvg, g, mod, bias], [(Z_END, BF16), (D, F32)],
                    [(1, MLA_Q_RANK), (1, MLA_KV_RANK)] + [(1, D)] * 3, ts=256, mats=[w_in])


ATT_SCALE = (MLA_NOPE_DIM + MLA_ROPE_DIM) ** -0.5
LOG2E = 1.4426950408889634
LN2 = 0.6931471805599453


def _ev_mid_bwd(dQ, dK, dV, cos, sin, w_uq, w_ukv):
    def fn(rows, p, mats):
        dq_, dk_, dv_, c_, s_ = rows
        cos_, sin_ = c_[...], s_[...]
        dqs, dkvs = [], []
        dkr = None
        for h in range(MLA_HEADS):
            b = 2 * HP * h
            dqs += [dq_[:, b:b + HP] * ATT_SCALE, _rope_t(dq_[:, b + HP:b + 2 * HP], cos_, sin_) * ATT_SCALE]
            dkvs += [dk_[:, b:b + HP] * LN2, dv_[:, HP * h:HP * (h + 1)]]
            t = dk_[:, b + HP:b + 2 * HP]
            dkr = t if dkr is None else dkr + t
        dqf = jnp.concatenate(dqs, axis=1).astype(BF16)
        dkv = jnp.concatenate(dkvs, axis=1).astype(BF16)
        dqn = lax.dot_general(dqf, mats[0][...], NT_DIMS, preferred_element_type=F32)
        dkvn = lax.dot_general(dkv, mats[1][...], NT_DIMS, preferred_element_type=F32)
        return [dqf, dkv, dqn, dkvn, dkr * LN2], []

    return _rowcall("ev_mid_bwd", fn, [dQ, dK, dV, cos, sin], [],
                    [(1024, BF16), (1024, BF16), (MLA_Q_RANK, F32), (MLA_KV_RANK, F32), (HP, F32)], mats=[w_uq, w_ukv])


def _mix_math(o, rg, a, mg):
    outs = []
    for h in range(RET_HEADS):
        oh = o[:, HP * h:HP * (h + 1)]
        mu = jnp.mean(oh, axis=-1, keepdims=True)
        var = jnp.mean(jnp.square(oh - mu), axis=-1, keepdims=True)
        outs.append((oh - mu) * lax.rsqrt(var + EPS))
    ret = jnp.concatenate(outs, axis=1) * _silu(rg)
    return jnp.concatenate([ret, a * _silu(mg)], axis=1)


def _ev_out_fwd(o2, z, a, x, w_out, g, mod, bias):
    D = D_MODEL

    def fn(rows, p, mats):
        o_, rg_, mg_, a_, x_ = rows
        g_, mod_, b_ = p
        mix = _mix_math(o_[:, :512] + o_[:, 512:], rg_[...], a_[...], mg_[...]).astype(BF16)
        y = jnp.dot(mix, mats[0][...], preferred_element_type=F32)
        return [mix, y, _post_math(x_[...], y, g_, mod_[:, 2 * D:], b_[:, 2 * D:])], []

    return _rowcall("ev_out", fn, [o2, *_z_gates(z), a, x], [g, mod, bias], [(1024, BF16), (D, F32), (D, F32)], mats=[w_out])


def _post_bwd_math(y, dxo, g, mod, bias):
    D = D_MODEL
    b_gate = bias[:, 2 * D:]
    _, vjp = jax.vjp(lambda y_, g2, mg: _post_math(0.0, y_, g2, mg, b_gate), y, g, mod[:, 2 * D:])
    return vjp(dxo)


def _ev_out_bwd(y, dxo, o2, z, a, w_out, g, mod, bias):
    D = D_MODEL

    def fn(rows, p, mats):
        y_, dxo_, o_, rg_, mg_, a_ = rows
        g_, mod_, b_ = p
        dy, dg, dgate = _post_bwd_math(y_[...], dxo_[...], g_, mod_, b_)
        dyb = dy.astype(BF16)
        dmix = lax.dot_general(dyb, mats[0][...], NT_DIMS, preferred_element_type=F32)
        _, vjp = jax.vjp(_mix_math, o_[:, :512] + o_[:, 512:], rg_[...], a_[...], mg_[...])
        return [dyb, *vjp(dmix)], [dg, dgate]

    return _rowcall("ev_out_bwd", fn, [y, dxo, o2, *_z_gates(z), a], [g, mod, bias],
                    [(D, BF16), (512, BF16), (512, F32), (512, BF16), (512, F32)], [(1, D)] * 2, mats=[w_out])


def _glu_math(a, b, ba, bb):
    return (a + ba) * jax.nn.sigmoid(b + bb)


def _od_in_fwd(x, g, mod, bias, w_in, bin_):
    D = D_MODEL

    def fn(rows, p, mats):
        g_, mod_, b_, bi_ = p
        h = _pre_math(rows[0][...], g_, mod_[:, D:2 * D], b_[:, D:2 * D], mod_[:, :D], b_[:, :D]).astype(BF16)
        z = jnp.dot(h, mats[0][...], preferred_element_type=F32)
        return [h, z, _glu_math(z[:, :D], z[:, D:2 * D], bi_[:, :D], bi_[:, D:2 * D])], []

    return _rowcall("od_in", fn, [x], [g, mod, bias, bin_], [(D, BF16), (3 * D, F32), (D, F32)], mats=[w_in])


def _odmix_math(uc, g, bg, ln_g, ln_b):
    mu = jnp.mean(uc, axis=-1, keepdims=True)
    var = jnp.mean(jnp.square(uc - mu), axis=-1, keepdims=True)
    y = (uc - mu) * lax.rsqrt(var + EPS) * ln_g + ln_b
    return _silu(y) * _silu(g + bg)


def _od_out_fwd(uc, z, x, w_out, bin_, ln_g, ln_b, g, mod, bias, tgt=None):
    D = D_MODEL

    def fn(rows, p, mats):
        uc_, zg_, x_ = rows[:3]
        bi_, lg_, lb_, g_, mod_, b_ = p
        vv = _odmix_math(uc_[...], zg_[...], bi_[:, 2 * D:], lg_, lb_).astype(BF16)
        y = jnp.dot(vv, mats[0][...], preferred_element_type=F32)
        xn = _post_math(x_[...], y, g_, mod_[:, 2 * D:], b_[:, 2 * D:])
        if tgt is None:
            return [vv, y, xn], []
        err = xn - rows[3][...]
        part = 0.5 * jnp.sum(jnp.mean(err * err, axis=-1, keepdims=True), axis=0, keepdims=True)
        return [vv, y, err * (1.0 / D)], [jnp.broadcast_to(part, (1, LANES))]

    rows = [uc, (z, D, 2), x] + ([] if tgt is None else [tgt])
    return _rowcall("od_out" if tgt is None else "od_out_loss", fn, rows, [bin_, ln_g, ln_b, g, mod, bias],
                    [(D, BF16), (D, F32), (D, F32)], [] if tgt is None else [(1, LANES)], mats=[w_out])


def _od_out_bwd(y, dxo, uc, z, w_out, bin_, ln_g, ln_b, g, mod, bias):
    D = D_MODEL

    def fn(rows, p, mats):
        y_, dxo_, uc_, zg_ = rows
        bi_, lg_, lb_, g_, mod_, b_ = p
        dy, dpg, dgate = _post_bwd_math(y_[...], dxo_[...], g_, mod_, b_)
        dyb = dy.astype(BF16)
        dvv = lax.dot_general(dyb, mats[0][...], NT_DIMS, preferred_element_type=F32)
        _, vjp = jax.vjp(_odmix_math, uc_[...], zg_[...], bi_[:, 2 * D:], lg_, lb_)
        duc, dg, dbg, dlg, dlb = vjp(dvv)
        return [dyb, duc, dg], [dpg, dgate, dbg, dlg, dlb]

    return _rowcall("od_out_bwd", fn, [y, dxo, uc, (z, D, 2)], [bin_, ln_g, ln_b, g, mod, bias],
                    [(D, BF16), (D, F32), (D, BF16)], [(1, D)] * 5, mats=[w_out])


def _od_in_bwd(z, bin_, du, dg, x, dx_res, w_in, g, mod, bias):
    D = D_MODEL

    def fn(rows, p, mats):
        za_, zb_, du_, dg_, x_, res_ = rows
        bi_, g_, mod_, b_ = p
        _, vjp = jax.vjp(_glu_math, za_[...], zb_[...], bi_[:, :D], bi_[:, D:2 * D])
        da, db, dba, dbb = vjp(du_[...])
        dz = jnp.concatenate([da.astype(BF16), db.astype(BF16), dg_[...]], axis=1)
        dh = lax.dot_general(dz, mats[0][...], NT_DIMS, preferred_element_type=F32)
        dx, dpg, dsc, dsh = _pre_bwd_math(dh, x_[...], res_[...], g_, mod_, b_)
        return [dz, dx], [dba, dbb, dpg, dsc, dsh]

    return _rowcall("od_in_bwd", fn, [(z, D, 0), (z, D, 1), du, dg, x, dx_res], [bin_, g, mod, bias],
                    [(3 * D, BF16), (D, F32)], [(1, D)] * 5, ts=256, mats=[w_in])


def _tile(n, cap):
    if n <= cap:
        return n
    best = None
    for t in range(LANES, cap + 1, LANES):
        if n % t == 0:
            best = t
    assert best is not None, (n, cap)
    return best


MM_TN_CAP = 1792
NT_DIMS = (((1,), (1,)), ((), ()))
TN_DIMS = (((0,), (0,)), ((), ()))


def _mm_nn(name, a, b, out_dtype=F32, tm=512, tn_cap=MM_TN_CAP):
    M, K = a.shape
    N = b.shape[1]
    tm = min(tm, M)
    tn = _tile(N, tn_cap)

    def body(a_ref, b_ref, o_ref):
        o_ref[...] = jnp.dot(a_ref[...], b_ref[...], preferred_element_type=F32).astype(o_ref.dtype)

    nb = _nbytes((tm, K), a.dtype) + _nbytes((K, tn), b.dtype) + _nbytes((tm, tn), out_dtype) + _nbytes((tm, tn), F32)
    return pl.pallas_call(
        body, name=name, grid=(N // tn, M // tm),
        in_specs=[pl.BlockSpec((tm, K), lambda j, i: (i, 0)), pl.BlockSpec((K, tn), lambda j, i: (0, j))],
        out_specs=pl.BlockSpec((tm, tn), lambda j, i: (i, j)),
        out_shape=jax.ShapeDtypeStruct((M, N), out_dtype),
        compiler_params=_params(("parallel", "parallel"), nb),
    )(_hbm(a), _hbm(b))


def _pre_mm(name, x, g, mod, bias, w, tm=512):
    S, D = x.shape
    N = w.shape[1]
    tm = min(tm, S)

    def body(x_ref, g_ref, mod_ref, bias_ref, w_ref, h_ref, z_ref):
        mod_, b_ = mod_ref[...], bias_ref[...]
        h = _pre_math(x_ref[...], g_ref[...], mod_[:, D:2 * D], b_[:, D:2 * D], mod_[:, :D], b_[:, :D]).astype(BF16)
        h_ref[...] = h
        z_ref[...] = jnp.dot(h, w_ref[...], preferred_element_type=F32)

    nb = 2 * _nbytes((tm, D), F32) + _nbytes((D, N), w.dtype) + 2 * _nbytes((tm, N), F32)
    row = lambda i: (i, 0)
    whole = lambda i: (0, 0)
    return pl.pallas_call(
        body, name=name, grid=(S // tm,),
        in_specs=[pl.BlockSpec((tm, D), row), pl.BlockSpec(g.shape, whole), pl.BlockSpec(mod.shape, whole),
                  pl.BlockSpec(bias.shape, whole), pl.BlockSpec((D, N), whole)],
        out_specs=[pl.BlockSpec((tm, D), row), pl.BlockSpec((tm, N), row)],
        out_shape=[jax.ShapeDtypeStruct((S, D), BF16), jax.ShapeDtypeStruct((S, N), F32)],
        compiler_params=_params(("parallel",), nb),
    )(_hbm(x), g, mod, bias, _hbm(w))


def _mm_tn(name, a, b, out_dtype=F32, ts=1024, tm_cap=1024, tn_cap=MM_TN_CAP):
    S, M = a.shape
    N = b.shape[1]
    ts = min(ts, S)
    tm = _tile(M, tm_cap)
    tn = _tile(N, tn_cap)
    ns = S // ts

    def body(a_ref, b_ref, o_ref, acc):
        s = pl.program_id(2)

        @pl.when(s == 0)
        def _():
            acc[...] = jnp.zeros(acc.shape, F32)

        acc[...] += lax.dot_general(a_ref[...], b_ref[...], TN_DIMS, preferred_element_type=F32)

        @pl.when(s == ns - 1)
        def _():
            o_ref[...] = acc[...].astype(o_ref.dtype)

    nb = _nbytes((ts, tm), a.dtype) + _nbytes((ts, tn), b.dtype) + 3 * _nbytes((tm, tn), F32)
    return pl.pallas_call(
        body, name=name, grid=(M // tm, N // tn, ns),
        in_specs=[pl.BlockSpec((ts, tm), lambda i, j, s: (s, i)), pl.BlockSpec((ts, tn), lambda i, j, s: (s, j))],
        out_specs=pl.BlockSpec((tm, tn), lambda i, j, s: (i, j)),
        out_shape=jax.ShapeDtypeStruct((M, N), out_dtype),
        scratch_shapes=[pltpu.VMEM((tm, tn), F32)],
        compiler_params=_params(("parallel", "parallel", "arbitrary"), nb),
    )(_hbm(a), _hbm(b))


ATT_TQ = 1024
ATT_TK = 1024
ATT_TK_FWD = 1024
QW = 2 * HP


class _Side:
    def __init__(self, side):
        self.on = side is not None
        if self.on:
            self.p, self.mode = side
            shp = self.p.shape
            self.h, self.w = (shp[0] // 2, shp[1]) if self.mode == "half" else (shp[-2], shp[-1])
        self.in_specs = [_HBM] if self.on else []
        self.out_specs = [_HBM] if self.on else []
        self.out_shape = [jax.ShapeDtypeStruct((4, self.h, self.w), self.p.dtype)] if self.on else []
        self.scratch = list(_CHIP_SEMS) if self.on else []
        self.operands = [_hbm(self.p)] if self.on else []

    def copies(self, p_ref, x_ref, sems):
        return _chip_copies(p_ref, x_ref, *sems, self.mode, self.h)

    def start(self, p_ref, x_ref, sems, first):
        @pl.when(first)
        def _():
            for cp in self.copies(p_ref, x_ref, sems):
                cp.start()

    def finish(self, p_ref, x_ref, sems, last):
        @pl.when(last)
        def _():
            for cp in self.copies(p_ref, x_ref, sems):
                cp.wait()


def _flash_fwd(Q, K, V, side=None):
    S = Q.shape[0]
    H = MLA_HEADS
    tq, tk = min(ATT_TQ, S), min(ATT_TK_FWD, S)
    nk = S // tk
    sd = _Side(side)

    def body(*refs):
        if sd.on:
            q_ref, k_ref, vt_ref, p_ref, o_ref, lse_ref, x_ref, m_s, l_s, acc_s, s_a, s_b, *sems = refs
            step_id = pl.program_id(0) * (S // tq) + pl.program_id(1)
            sd.start(p_ref, x_ref, sems, step_id == 0)
        else:
            q_ref, k_ref, vt_ref, o_ref, lse_ref, m_s, l_s, acc_s, s_a, s_b = refs
        m_s[...] = jnp.full(m_s.shape, -jnp.inf, F32)
        l_s[...] = jnp.zeros(l_s.shape, F32)
        acc_s[...] = jnp.zeros(acc_s.shape, F32)

        def scores(j):
            k = k_ref[pl.ds(pl.multiple_of(j * tk, tk), tk), :]
            return lax.dot_general(k, q_ref[...], NT_DIMS, preferred_element_type=F32)

        def consume(st, j):
            m_prev = m_s[...]
            m_new = jnp.maximum(m_prev, jnp.max(st, axis=0, keepdims=True))
            alpha = jnp.exp2(m_prev - m_new)
            pt = jnp.exp2(st - m_new)
            l_s[...] = alpha * l_s[...] + jnp.sum(pt, axis=0, keepdims=True)
            v = vt_ref[pl.ds(pl.multiple_of(j * tk, tk), tk), :]
            acc_s[...] = alpha * acc_s[...] + lax.dot_general(v, pt.astype(BF16), TN_DIMS, preferred_element_type=F32)
            m_s[...] = m_new

        if nk % 2:
            def step(j, carry):
                consume(scores(j), j)
                return carry

            lax.fori_loop(0, nk, step, 0)
        else:
            s_a[...] = scores(0)

            def pair(jj, carry):
                j0 = 2 * jj
                s_b[...] = scores(j0 + 1)
                consume(s_a[...], j0)
                s_a[...] = scores(jnp.minimum(j0 + 2, nk - 1))
                consume(s_b[...], j0 + 1)
                return carry

            lax.fori_loop(0, nk // 2, pair, 0)
        l = l_s[...]
        o_ref[...] = (acc_s[...] * (1.0 / l)).T
        lse_ref[0] = m_s[...] + jnp.log2(l)
        if sd.on:
            sd.finish(p_ref, x_ref, sems, step_id == H * (S // tq) - 1)

    nb = (_nbytes((tq, QW), BF16) + _nbytes((S, QW), BF16) + _nbytes((S, HP), BF16) + 3 * _nbytes((tq, HP), F32)
          + 4 * _nbytes((tq, tk), F32))
    return pl.pallas_call(
        body, name="flash_fwd_side" if sd.on else "flash_fwd", grid=(H, S // tq),
        in_specs=[pl.BlockSpec((tq, QW), lambda h, i: (i, h)), pl.BlockSpec((S, QW), lambda h, i: (0, h)),
                  pl.BlockSpec((S, HP), lambda h, i: (0, h))] + sd.in_specs,
        out_specs=[pl.BlockSpec((tq, HP), lambda h, i: (i, h)), pl.BlockSpec((1, 1, tq), lambda h, i: (h, 0, i))] + sd.out_specs,
        out_shape=[jax.ShapeDtypeStruct((S, H * HP), F32), jax.ShapeDtypeStruct((H, 1, S), F32)] + sd.out_shape,
        scratch_shapes=[pltpu.VMEM((1, tq), F32), pltpu.VMEM((1, tq), F32), pltpu.VMEM((HP, tq), F32),
                        pltpu.VMEM((tk, tq), F32), pltpu.VMEM((tk, tq), F32)] + sd.scratch,
        compiler_params=_params(("arbitrary", "arbitrary") if sd.on else ("parallel", "parallel"), nb),
    )(_hbm(Q), _hbm(K), _hbm(V), *sd.operands)


def _flash_bwd(Q, K, V, dO, O, lse, side=None):
    S = Q.shape[0]
    H = MLA_HEADS
    tq, tk = min(ATT_TQ, S), min(ATT_TK, S)
    nk = S // tk
    sd = _Side(side)

    def body(*refs):
        if sd.on:
            q_ref, do_ref, o_ref, lse_ref, k_ref, v_ref, p_ref, dq_ref, dk_ref, dv_ref, x_ref, dq_s, *sems = refs
            step_id = pl.program_id(0) * (S // tq) + pl.program_id(1)
            sd.start(p_ref, x_ref, sems, step_id == 0)
        else:
            q_ref, do_ref, o_ref, lse_ref, k_ref, v_ref, dq_ref, dk_ref, dv_ref, dq_s = refs

        @pl.when(pl.program_id(1) == 0)
        def _():
            dk_ref[...] = jnp.zeros(dk_ref.shape, F32)
            dv_ref[...] = jnp.zeros(dv_ref.shape, F32)

        delta = jnp.sum((do_ref[...].astype(F32) * o_ref[...]).T, axis=0, keepdims=True)
        lse = lse_ref[0]
        dq_s[...] = jnp.zeros(dq_s.shape, F32)

        def rows(j):
            return pl.ds(pl.multiple_of(j * tk, tk), tk)

        def scores(j):
            st = lax.dot_general(k_ref[rows(j), :], q_ref[...], NT_DIMS, preferred_element_type=F32)
            dpt = lax.dot_general(v_ref[rows(j), :], do_ref[...], NT_DIMS, preferred_element_type=F32)
            return st, dpt

        def consume(st, dpt, j):
            pt = jnp.exp2(st - lse)
            dst = (pt * (dpt - delta)).astype(BF16)
            dv_ref[rows(j), :] += jnp.dot(pt.astype(BF16), do_ref[...], preferred_element_type=F32)
            dk_ref[rows(j), :] += jnp.dot(dst, q_ref[...], preferred_element_type=F32)
            dq_s[...] += lax.dot_general(dst, k_ref[rows(j), :], TN_DIMS, preferred_element_type=F32)

        def step(j, carry):
            consume(*scores(j), j)
            return carry

        lax.fori_loop(0, nk, step, 0)
        dq_ref[...] = dq_s[...]
        if sd.on:
            sd.finish(p_ref, x_ref, sems, step_id == H * (S // tq) - 1)

    nb = (_nbytes((tq, QW), BF16) + _nbytes((tq, HP), BF16) + _nbytes((tq, HP), F32) + _nbytes((S, QW), BF16)
          + _nbytes((S, HP), BF16) + 2 * _nbytes((tq, QW), F32) + _nbytes((S, QW), F32) + _nbytes((S, HP), F32))
    return pl.pallas_call(
        body, name="flash_bwd_side" if sd.on else "flash_bwd", grid=(H, S // tq),
        in_specs=[pl.BlockSpec((tq, QW), lambda h, i: (i, h)), pl.BlockSpec((tq, HP), lambda h, i: (i, h)),
                  pl.BlockSpec((tq, HP), lambda h, i: (i, h)), pl.BlockSpec((1, 1, tq), lambda h, i: (h, 0, i)),
                  pl.BlockSpec((S, QW), lambda h, i: (0, h)), pl.BlockSpec((S, HP), lambda h, i: (0, h))] + sd.in_specs,
        out_specs=[pl.BlockSpec((tq, QW), lambda h, i: (i, h)), pl.BlockSpec((S, QW), lambda h, i: (0, h)),
                   pl.BlockSpec((S, HP), lambda h, i: (0, h))] + sd.out_specs,
        out_shape=[jax.ShapeDtypeStruct((S, H * QW), F32), jax.ShapeDtypeStruct((S, H * QW), F32),
                   jax.ShapeDtypeStruct((S, H * HP), F32)] + sd.out_shape,
        scratch_shapes=[pltpu.VMEM((tq, QW), F32)] + sd.scratch,
        compiler_params=_params(("arbitrary", "arbitrary") if sd.on else ("parallel", "arbitrary"), nb),
    )(_hbm(Q), _hbm(dO), _hbm(O), lse, _hbm(K), _hbm(V), *sd.operands)


RET_CHUNK = 256


def _ret_tables(d, lg_ref, h, C):
    ii = lax.broadcasted_iota(jnp.int32, (C, C), 0).astype(F32)
    jj = lax.broadcasted_iota(jnp.int32, (C, C), 1).astype(F32)
    ci = lax.broadcasted_iota(jnp.int32, (C, 1), 0).astype(F32)
    fwd = d == 0
    dist = jnp.where(fwd, ii - jj, jj - ii)
    mask = dist >= jnp.where(fwd, 0.0, 1.0)
    dist = jnp.maximum(dist, 0.0)
    qpos = jnp.where(fwd, ci + 1.0, C - ci)
    kpos = jnp.where(fwd, C - 1.0 - ci, ci)
    lg = lg_ref[d, h]
    D = jnp.where(mask, jnp.exp(lg * dist), 0.0)
    return D, jnp.exp(lg * qpos), jnp.exp(lg * kpos), jnp.exp(lg * C), dist, qpos, kpos


def _ret_fwd(lg, q, k, v):
    S = q.shape[0]
    C = min(RET_CHUNK, S)
    N = S // C
    H = RET_HEADS

    def chunk(d, n):
        return jnp.where(d == 0, n, N - 1 - n)

    def body(lg_ref, q_ref, k_ref, v_ref, o_ref, st_ref, state, tab_d, tab_q, tab_k):
        d, n = pl.program_id(0), pl.program_id(1)

        @pl.when(n == 0)
        def _():
            state[...] = jnp.zeros_like(state)
            for h in range(H):
                D, qw, kw, _, _, _, _ = _ret_tables(d, lg_ref, h, C)
                tab_d[h] = D
                tab_q[h] = jnp.broadcast_to(qw, (C, HP))
                tab_k[h] = jnp.broadcast_to(kw, (C, HP))

        for h in range(H):
            sl = slice(HP * h, HP * (h + 1))
            D, qw, kw = tab_d[h], tab_q[h], tab_k[h]
            gc = jnp.exp(lg_ref[d, h] * C)
            qh, kh, vh = q_ref[:, sl], k_ref[:, sl], v_ref[:, sl]
            st = state[sl, :]
            sm = lax.dot_general(qh, kh, (((1,), (1,)), ((), ())), preferred_element_type=F32) * D
            inner = jnp.dot(sm.astype(BF16), vh, preferred_element_type=F32)
            cross = qw * jnp.dot(qh, st.astype(BF16), preferred_element_type=F32)
            o_ref[:, sl] = inner + cross
            st_ref[0, 0, sl, :] = st
            kvn = lax.dot_general((kh.astype(F32) * kw).astype(BF16), vh, (((0,), (0,)), ((), ())), preferred_element_type=F32)
            state[sl, :] = gc * st + kvn

    nb = 3 * _nbytes((C, 512), BF16) + _nbytes((C, 512), F32) + 2 * _nbytes((512, HP), F32) + 8 * _nbytes((C, C), F32)
    return pl.pallas_call(
        body, name="ret_fwd", grid=(2, N),
        in_specs=[pl.BlockSpec(memory_space=pltpu.SMEM)] + [pl.BlockSpec((C, 512), lambda d, n: (chunk(d, n), 0))] * 3,
        out_specs=[pl.BlockSpec((C, 512), lambda d, n: (chunk(d, n), d)),
                   pl.BlockSpec((1, 1, 512, HP), lambda d, n: (d, chunk(d, n), 0, 0))],
        out_shape=[jax.ShapeDtypeStruct((S, 1024), F32), jax.ShapeDtypeStruct((2, N, 512, HP), F32)],
        scratch_shapes=[pltpu.VMEM((512, HP), F32), pltpu.VMEM((H, C, C), F32), pltpu.VMEM((H, C, HP), F32),
                        pltpu.VMEM((H, C, HP), F32)],
        compiler_params=_params(("arbitrary", "arbitrary"), nb),
    )(lg, _hbm(q), _hbm(k), _hbm(v))


def _ret_bwd(lg, q, k, v, do, states):
    S = q.shape[0]
    C = min(RET_CHUNK, S)
    N = S // C
    H = RET_HEADS

    def chunk(d, n):
        return jnp.where(d == 0, N - 1 - n, n)

    def body(lg_ref, q_ref, k_ref, v_ref, do_ref, st_ref, dq_ref, dk_ref, dv_ref, dlg_ref, G, accA, accB, accC,
             tab_d, tab_q, tab_k, tab_qp, tab_kp):
        d, n = pl.program_id(0), pl.program_id(1)

        @pl.when(n == 0)
        def _():
            G[...] = jnp.zeros_like(G)
            accA[...] = jnp.zeros_like(accA)
            accB[...] = jnp.zeros_like(accB)
            accC[...] = jnp.zeros_like(accC)
            for h in range(H):
                D, qw, kw, _, _, qpos, kpos = _ret_tables(d, lg_ref, h, C)
                tab_d[h] = D
                tab_q[h] = jnp.broadcast_to(qw, (C, HP))
                tab_k[h] = jnp.broadcast_to(kw, (C, HP))
                tab_qp[h] = jnp.broadcast_to(qw * qpos, (C, HP))
                tab_kp[h] = jnp.broadcast_to(kw * kpos, (C, HP))

        nt = (((1,), (1,)), ((), ()))
        tn = (((0,), (0,)), ((), ()))
        for h in range(H):
            sl = slice(HP * h, HP * (h + 1))
            D, qw, kw = tab_d[h], tab_q[h], tab_k[h]
            gc = jnp.exp(lg_ref[d, h] * C)
            qh, kh, vh, doh = q_ref[:, sl], k_ref[:, sl], v_ref[:, sl], do_ref[:, sl]
            st = st_ref[0, 0, sl, :]
            g = G[sl, :]
            stb, gb = st.astype(BF16), g.astype(BF16)
            sraw = lax.dot_general(qh, kh, nt, preferred_element_type=F32)
            dS = lax.dot_general(doh, vh, nt, preferred_element_type=F32) * D
            dSb = dS.astype(BF16)
            smb = (sraw * D).astype(BF16)
            qs = jnp.dot(qh, stb, preferred_element_type=F32)
            kg = jnp.dot(kh, gb, preferred_element_type=F32)
            dqh = jnp.dot(dSb, kh, preferred_element_type=F32) + qw * lax.dot_general(doh, stb, nt, preferred_element_type=F32)
            dkh = lax.dot_general(dSb, qh, tn, preferred_element_type=F32) + kw * lax.dot_general(vh, gb, nt, preferred_element_type=F32)
            dvh = lax.dot_general(smb, doh, tn, preferred_element_type=F32) + kw * kg
            dq_ref[:, sl] = dqh.astype(dq_ref.dtype)
            dk_ref[:, sl] = dkh.astype(dk_ref.dtype)
            dv_ref[:, sl] = dvh.astype(dv_ref.dtype)
            dof, vf = doh.astype(F32), vh.astype(F32)
            accA[h] += sraw * dS
            accB[h] += tab_qp[h] * (qs * dof) + tab_kp[h] * (kg * vf)
            accC[h] += st * g
            G[sl, :] = gc * g + lax.dot_general((qh.astype(F32) * qw).astype(BF16), doh, tn, preferred_element_type=F32)

        @pl.when(n == N - 1)
        def _():
            rows = []
            for h in range(H):
                _, _, _, gc, dist, _, _ = _ret_tables(d, lg_ref, h, C)
                tot = jnp.sum(accA[h] * dist) + jnp.sum(accB[h]) + (C * gc) * jnp.sum(accC[h])
                rows.append(jnp.broadcast_to(tot, (1, HP)))
            dlg_ref[0] = jnp.concatenate(rows + [jnp.zeros((8 - H, HP), F32)], axis=0)

    nb = (4 * _nbytes((C, 512), BF16) + 3 * _nbytes((C, 512), F32) + 2 * _nbytes((512, HP), F32)
          + H * (_nbytes((C, C), F32) + _nbytes((C, HP), F32) + _nbytes((HP, HP), F32)) + 12 * _nbytes((C, C), F32))
    row = lambda d, n: (chunk(d, n), 0)
    out3 = lambda d, n: (chunk(d, n), d)
    return pl.pallas_call(
        body, name="ret_bwd", grid=(2, N),
        in_specs=[pl.BlockSpec(memory_space=pltpu.SMEM)] + [pl.BlockSpec((C, 512), row)] * 4
        + [pl.BlockSpec((1, 1, 512, HP), lambda d, n: (d, chunk(d, n), 0, 0))],
        out_specs=[pl.BlockSpec((C, 512), out3)] * 3 + [pl.BlockSpec((1, 8, HP), lambda d, n: (d, 0, 0))],
        out_shape=[jax.ShapeDtypeStruct((S, 1024), BF16)] * 3 + [jax.ShapeDtypeStruct((2, 8, HP), F32)],
        scratch_shapes=[pltpu.VMEM((512, HP), F32), pltpu.VMEM((H, C, C), F32), pltpu.VMEM((H, C, HP), F32),
                        pltpu.VMEM((H, HP, HP), F32), pltpu.VMEM((H, C, C), F32)] + [pltpu.VMEM((H, C, HP), F32)] * 4,
        compiler_params=_params(("arbitrary", "arbitrary"), nb),
    )(lg, _hbm(q), _hbm(k), _hbm(v), _hbm(do), _hbm(states))


CONV_PAD = 16
CONV_TR = 256
CONV_CB = LANES


def _fill_padded(pad, u_ref, S):
    pad[0:CONV_PAD, :] = jnp.zeros((CONV_PAD, CONV_CB), F32)
    pad[CONV_PAD + S:CONV_PAD + S + CONV_PAD, :] = jnp.zeros((CONV_PAD, CONV_CB), F32)
    pad[CONV_PAD:CONV_PAD + S, :] = u_ref[...]


def _conv_fwd(u, w32, b):
    S, D = u.shape
    tr = min(CONV_TR, S)

    def body(u_ref, w_ref, b_ref, o_ref, pad):
        _fill_padded(pad, u_ref, S)
        wv = w_ref[...]
        bv = b_ref[...]

        def step(t, carry):
            r0 = pl.multiple_of(t * tr, tr)
            acc = jnp.broadcast_to(bv, (tr, CONV_CB))
            for k in range(CONV_KERNEL):
                acc = acc + pad[pl.ds(r0 + (k + 1), tr), :] * wv[k:k + 1, :]
            o_ref[pl.ds(r0, tr), :] = acc
            return carry

        lax.fori_loop(0, S // tr, step, 0)

    nb = 2 * _nbytes((S, CONV_CB), F32) + _nbytes((S + 2 * CONV_PAD, CONV_CB), F32)
    return pl.pallas_call(
        body, name="conv_fwd", grid=(D // CONV_CB,),
        in_specs=[pl.BlockSpec((S, CONV_CB), lambda j: (0, j)), pl.BlockSpec((32, CONV_CB), lambda j: (0, j)),
                  pl.BlockSpec((1, CONV_CB), lambda j: (0, j))],
        out_specs=pl.BlockSpec((S, CONV_CB), lambda j: (0, j)),
        out_shape=jax.ShapeDtypeStruct((S, D), F32),
        scratch_shapes=[pltpu.VMEM((S + 2 * CONV_PAD, CONV_CB), F32)],
        compiler_params=_params(("parallel",), nb),
    )(_hbm(u), w32, b)


def _conv_bwd_w(u, dout):
    S, D = u.shape
    tr = min(CONV_TR, S)

    def body(u_ref, d_ref, o_ref, pad, acc):
        _fill_padded(pad, u_ref, S)
        acc[...] = jnp.zeros_like(acc)

        def fold(a):
            return jnp.sum(a.reshape(tr // 8, 8, CONV_CB), axis=0)

        def step(t, carry):
            r0 = pl.multiple_of(t * tr, tr)
            dv = d_ref[pl.ds(r0, tr), :]
            for k in range(CONV_KERNEL):
                acc[8 * k:8 * k + 8, :] += fold(pad[pl.ds(r0 + (k + 1), tr), :] * dv)
            acc[8 * CONV_KERNEL:8 * CONV_KERNEL + 8, :] += fold(dv)
            return carry

        lax.fori_loop(0, S // tr, step, 0)
        o_ref[...] = jnp.sum(acc[...].reshape(32, 8, CONV_CB), axis=1)

    nb = 2 * _nbytes((S, CONV_CB), F32) + _nbytes((S + 2 * CONV_PAD, CONV_CB), F32)
    return pl.pallas_call(
        body, name="conv_bwd_w", grid=(D // CONV_CB,),
        in_specs=[pl.BlockSpec((S, CONV_CB), lambda j: (0, j)), pl.BlockSpec((S, CONV_CB), lambda j: (0, j))],
        out_specs=pl.BlockSpec((32, CONV_CB), lambda j: (0, j)),
        out_shape=jax.ShapeDtypeStruct((32, D), F32),
        scratch_shapes=[pltpu.VMEM((S + 2 * CONV_PAD, CONV_CB), F32), pltpu.VMEM((256, CONV_CB), F32)],
        compiler_params=_params(("parallel",), nb),
    )(_hbm(u), _hbm(dout))


def _adamw(w, g, m, v):
    shape = w.shape
    w2, g2, m2, v2 = [a.reshape(-1, shape[-1]) for a in (w, g, m, v)]
    W = shape[-1]

    def fn(rows, p):
        w_, g_, m_, v_ = [r[...] for r in rows]
        mn = ADAM_B1 * m_ + (1.0 - ADAM_B1) * g_
        vn = ADAM_B2 * v_ + (1.0 - ADAM_B2) * jnp.square(g_)
        m_hat = mn / (1.0 - ADAM_B1 ** ADAM_STEP)
        v_hat = vn / (1.0 - ADAM_B2 ** ADAM_STEP)
        delta = -ADAM_LR * (m_hat / (jnp.sqrt(v_hat) + ADAM_EPS) + ADAM_WD * w_)
        return [delta, mn, vn], []

    R2 = w2.shape[0]
    ts = ROW_TILE if (R2 > ROW_TILE and R2 % ROW_TILE == 0) else R2
    d, mn, vn = _rowcall("adamw", fn, [w2, g2, m2, v2], [], [(W, F32)] * 3, ts=ts)
    return d.reshape(shape), mn.reshape(shape), vn.reshape(shape)


_HBM = pl.BlockSpec(memory_space=pltpu.HBM)


def _place():
    return lax.axis_index("x"), lax.axis_index("y"), lax.axis_index("c")


def _half_rows(ref, half, h):
    idx = (slice(None),) * (len(ref.shape) - 2) + (pl.ds(half * h, h), slice(None))
    return ref.at[idx]


def _chip_exchange(name, p, mode):
    assert mode in ("half", "whole", "scatter")
    if mode == "half":
        h, W = p.shape[0] // 2, p.shape[1]
    elif mode == "whole":
        h, W = p.shape
    else:
        h, W = p.shape[1], p.shape[2]

    def body(p_ref, o_ref, send_sems, recv_sems, local_sem):
        copies = _chip_copies(p_ref, o_ref, send_sems, recv_sems, local_sem, mode, h)
        for cp in copies:
            cp.start()
        for cp in copies:
            cp.wait()

    return pl.pallas_call(
        body, name=name, in_specs=[_HBM], out_specs=_HBM,
        out_shape=jax.ShapeDtypeStruct((4, h, W), p.dtype), scratch_shapes=_CHIP_SEMS,
    )(p)


_CHIP_SEMS = [pltpu.SemaphoreType.DMA((3,)), pltpu.SemaphoreType.DMA((3,)), pltpu.SemaphoreType.DMA]


def _chip_copies(p_ref, o_ref, send_sems, recv_sems, local_sem, mode, h):
    x, y, c = _place()
    k_me = 2 * x + y

    def src(k):
        if mode == "half":
            return _half_rows(p_ref, c, h)
        return p_ref if mode == "whole" else p_ref.at[k]

    copies = [pltpu.make_async_copy(src(k_me), o_ref.at[k_me], local_sem)]
    for j, (cx, cy) in enumerate([(1 - x, y), (x, 1 - y), (1 - x, 1 - y)]):
        copies.append(pltpu.make_async_remote_copy(
            src_ref=src(2 * cx + cy), dst_ref=o_ref.at[k_me], send_sem=send_sems.at[j], recv_sem=recv_sems.at[j],
            device_id=(cx, cy, c), device_id_type=MESH))
    return copies


def _sibling_send_half(name, g):
    n, R, W = g.shape
    h = R // 2

    def body(g_ref, o_ref, send_sem, recv_sem):
        x, y, c = _place()
        cp = pltpu.make_async_remote_copy(
            src_ref=_half_rows(g_ref, 1 - c, h), dst_ref=o_ref, send_sem=send_sem, recv_sem=recv_sem,
            device_id=(x, y, 1 - c), device_id_type=MESH)
        cp.start()
        cp.wait()

    return pl.pallas_call(
        body, name=name, in_specs=[_HBM], out_specs=_HBM,
        out_shape=jax.ShapeDtypeStruct((n, h, W), g.dtype),
        scratch_shapes=[pltpu.SemaphoreType.DMA, pltpu.SemaphoreType.DMA],
    )(g)


def _halves_merge(name, mine):
    def body(m_ref, o_ref, send_sem, recv_sem):
        x, y, c = _place()
        cp = pltpu.make_async_remote_copy(
            src_ref=m_ref, dst_ref=o_ref, send_sem=send_sem, recv_sem=recv_sem,
            device_id=(x, y, 1 - c), device_id_type=MESH)
        cp.start()
        cp.wait()

    theirs = pl.pallas_call(
        body, name=name, in_specs=[_HBM], out_specs=_HBM,
        out_shape=jax.ShapeDtypeStruct(mine.shape, mine.dtype),
        scratch_shapes=[pltpu.SemaphoreType.DMA, pltpu.SemaphoreType.DMA],
    )(mine)
    south = lax.axis_index("c") == 0
    axis = mine.ndim - 2
    return jnp.concatenate([jnp.where(south, mine, theirs), jnp.where(south, theirs, mine)], axis=axis)


def _add2(name, a, b, out_dtype):
    def fn(rows, p):
        return [rows[0][...].astype(F32) + rows[1][...].astype(F32)], []

    return _rowcall(name, fn, [a, b], [], [(a.shape[1], out_dtype)], ts=4 * PACK_TILE)[0]


def _sum4(name, b):
    _, h, W = b.shape
    ts = 2 * PACK_TILE if h % (2 * PACK_TILE) == 0 else (PACK_TILE if h % PACK_TILE == 0 else h)

    def body(b_ref, o_ref):
        o_ref[...] = ((b_ref[0].astype(F32) + b_ref[1].astype(F32)) + b_ref[2].astype(F32)) + b_ref[3].astype(F32)

    nb = _nbytes((4, ts, W), b.dtype) + _nbytes((ts, W), F32)
    return pl.pallas_call(
        body, name=name, grid=(h // ts,),
        in_specs=[pl.BlockSpec((4, ts, W), lambda i: (0, i, 0))], out_specs=pl.BlockSpec((ts, W), lambda i: (i, 0)),
        out_shape=jax.ShapeDtypeStruct((h, W), F32), compiler_params=_params(("parallel",), nb),
    )(b)


PACK_W = 1024
PACK_ROWS = 16
PACK_TILE = 128
EV_SHARDED = (("ev_w_in", "col", False), ("ev_w_uq", "col", False), ("ev_w_ukv", "col", False), ("ev_w_out", "row", False))
OD_SHARDED = (("od_w_in", "col", False), ("od_b_in", "col", True), ("od_dw_w", "col", True), ("od_dw_b", "col", True),
              ("od_ln_g", "col", True), ("od_ln_b", "col", True), ("od_w_out", "row", False))
REPLICATED =("ada_b", "pre_g", "post_g", "ev_dec_f", "ev_dec_b", "ev_q_norm_g", "ev_kv_norm_g")
WEIGHTS = ("ada_w", "ada_b", "pre_g", "post_g", "ev_w_in", "ev_dec_f", "ev_dec_b", "ev_q_norm_g", "ev_w_uq", "ev_kv_norm_g",
           "ev_w_ukv", "ev_w_out", "od_w_in", "od_b_in", "od_dw_w", "od_dw_b", "od_ln_g", "od_ln_b", "od_w_out")


def _layer_entries(layers):
    out = []
    for l in layers:
        out += [(name, how, exact, l // 2) for name, how, exact in (EV_SHARDED if l % 2 == 0 else OD_SHARDED)]
    return out


def _rows_of(n):
    unit = PACK_W * PACK_ROWS
    return (n + unit - 1) // unit * PACK_ROWS


def _to_rows(flat, lead):
    n = flat.shape[-1]
    r = _rows_of(n)
    flat = jnp.pad(flat, [(0, 0)] * len(lead) + [(0, r * PACK_W - n)])
    return flat.reshape(*lead, r, PACK_W)


def _pad_total(buf, axis):
    r = buf.shape[axis]
    r2 = (r + 2 * PACK_TILE - 1) // (2 * PACK_TILE) * (2 * PACK_TILE)
    pads = [(0, 0)] * buf.ndim
    pads[axis] = (0, r2 - r)
    return jnp.pad(buf, pads)


def _split_chips(full, how):
    if how == "col":
        n = full.shape[-1] // 4
        return jnp.moveaxis(full.reshape(*full.shape[:-1], 4, n), -2, 0)
    n = full.shape[-2] // 4
    return jnp.moveaxis(full.reshape(*full.shape[:-2], 4, n, full.shape[-1]), -3, 0)


def _join_chips(blocks, how):
    if how == "col":
        t = jnp.moveaxis(blocks, 0, -2)
        return t.reshape(*t.shape[:-2], t.shape[-2] * t.shape[-1])
    t = jnp.moveaxis(blocks, 0, -3)
    return t.reshape(*t.shape[:-3], t.shape[-3] * t.shape[-2], t.shape[-1])


def _pack_weights(w, entries):
    parts = []
    for name, _, exact, idx in entries:
        a = w[name][idx].reshape(-1)
        a = lax.bitcast_convert_type(a, BF16).reshape(-1) if exact else a.astype(BF16)
        parts.append(_to_rows(a, ()))
    return _pad_total(jnp.concatenate(parts, axis=0), 0)


def _unpack_weights(gathered, w, entries):
    out, r0 = {}, 0
    for name, how, exact, idx in entries:
        shp = w[name].shape[1:]
        n = (2 if exact else 1) * w[name][idx].size
        r = _rows_of(n)
        a = gathered[:, r0:r0 + r].reshape(4, -1)[:, :n]
        if exact:
            a = lax.bitcast_convert_type(a.reshape(4, -1, 2), F32)
        out[(name, idx)] = _join_chips(a.reshape(4, *shp), how)
        r0 += r
    return out


def _pack_grads(grads, entries, replicated):
    parts = []
    for name, how, _, idx in entries:
        b = _split_chips(grads[(name, idx)], how)
        parts.append(_to_rows(b.reshape(4, -1).astype(BF16), (4,)))
    for name in replicated:
        a = _to_rows(grads[name].reshape(-1).astype(BF16), ())
        parts.append(jnp.broadcast_to(a[None], (4, *a.shape)))
    return _pad_total(jnp.concatenate(parts, axis=1), 1)


def _reduce_begin(tag, g):
    h = g.shape[1] // 2
    c = lax.axis_index("c")
    theirs = _sibling_send_half("reduce_cores_" + tag, g)
    mine = lax.dynamic_slice_in_dim(g, c * h, h, axis=1)
    return _add2("reduce_add_" + tag, mine.reshape(4 * h, PACK_W), theirs.reshape(4 * h, PACK_W), BF16).reshape(4, h, PACK_W)


def _reduce_end(tag, exchanged):
    return _halves_merge("reduce_merge_" + tag, _sum4("reduce_sum_" + tag, exchanged))


def _unpack_grads(total, w, entries, replicated):
    out, r0 = {}, 0
    for name, _, _, idx in entries:
        n = w[name][idx].size
        r = _rows_of(n)
        out[(name, idx)] = total[r0:r0 + r].reshape(-1)[:n].reshape(w[name].shape[1:])
        r0 += r
    for name in replicated:
        n = w[name].size
        r = _rows_of(n)
        out[name] = total[r0:r0 + r].reshape(-1)[:n].reshape(w[name].shape)
        r0 += r
    return out


def _pad_heads(a, n, w, to):
    lead = a.shape[:-1]
    return jnp.pad(a.reshape(*lead, n, w), [(0, 0)] * (len(lead) + 1) + [(0, to - w)]).reshape(*lead, n * to)


def _unpad_heads(a, n, w, to):
    lead = a.shape[:-1]
    return a.reshape(*lead, n, to)[..., :w].reshape(*lead, n * w)


def _w_in_pad(w):
    return jnp.concatenate([_pad_heads(w[:, 0:256], 4, 64, HP), _pad_heads(w[:, 256:512], 4, 64, HP), w[:, 512:1536],
                            w[:, 2240:2752], w[:, 1536:2176], _pad_heads(w[:, 2176:2240], 1, 64, HP)], axis=1)


def _w_in_unpad(g):
    return jnp.concatenate([_unpad_heads(g[:, Z_RQ:Z_RK], 4, 64, HP), _unpad_heads(g[:, Z_RK:Z_RV], 4, 64, HP),
                            g[:, Z_RV:Z_MG], g[:, Z_CQ:Z_KR], g[:, Z_KR:Z_KR + 64], g[:, Z_MG:Z_CQ]], axis=1)


def _layer_weights(l, full, w):
    f32 = lambda a: a.astype(F32)
    i = l // 2
    common = dict(bias=f32(w["ada_b"][l])[None], pre_g=f32(w["pre_g"][l])[None], post_g=f32(w["post_g"][l])[None])
    if l % 2 == 0:
        w_uq_p = _pad_heads(full[("ev_w_uq", i)], MLA_HEADS, MLA_NOPE_DIM + MLA_ROPE_DIM, QW)
        lg = jnp.stack([jax.nn.log_sigmoid(f32(w["ev_dec_f"][i])), jax.nn.log_sigmoid(f32(w["ev_dec_b"][i]))])
        return dict(common, w_in=_w_in_pad(full[("ev_w_in", i)]), w_uq=w_uq_p, w_ukv=full[("ev_w_ukv", i)],
                    w_out=full[("ev_w_out", i)], qg=f32(w["ev_q_norm_g"][i])[None], kvg=f32(w["ev_kv_norm_g"][i])[None], lg=lg)
    dw = f32(full[("od_dw_w", i)])
    return dict(common, w_in=full[("od_w_in", i)], w_out=full[("od_w_out", i)], b_in=f32(full[("od_b_in", i)])[None],
                dw=jnp.pad(dw, ((0, 1), (0, 0))), dw_flip=jnp.pad(dw[::-1], ((0, 1), (0, 0))),
                dw_b=f32(full[("od_dw_b", i)])[None], ln_g=f32(full[("od_ln_g", i)])[None], ln_b=f32(full[("od_ln_b", i)])[None])


def _rope_tables(positions):
    inv_freq = ROPE_BASE ** (-jnp.arange(0, ROPE_DIM, 2, dtype=F32) / ROPE_DIM)
    ang = positions.astype(F32)[:, None] * inv_freq
    z = jnp.zeros((ang.shape[0], HP - ROPE_DIM), F32)
    return jnp.concatenate([jnp.cos(ang), jnp.cos(ang), z], axis=1), jnp.concatenate([jnp.sin(ang), jnp.sin(ang), z], axis=1)


def _even_fwd(x, mod, cos, sin, w, side=None):
    bias, pre_g, post_g = w["bias"], w["pre_g"], w["post_g"]
    h, z = _pre_mm("ev_in", x, pre_g, mod, bias, w["w_in"])
    rq, rk, rv, qn, kvn, Q, K, V = _ev_mid_fwd(z, cos, sin, w["qg"], w["kvg"], w["w_uq"], w["w_ukv"])
    a, lse, *got = _flash_fwd(Q, K, V, side)
    o2, states = _ret_fwd(w["lg"], rq, rk, rv)
    mix, y, x_new = _ev_out_fwd(o2, z, a, x, w["w_out"], post_g, mod, bias)
    saved = dict(x=x, h=h, z=z, rq=rq, rk=rk, rv=rv, qn=qn, kvn=kvn, Q=Q, K=K, V=V, a=a, lse=lse, o2=o2,
                 states=states, mix=mix, y=y)
    return x_new, saved, (got[0] if got else None)


def _even_bwd(dx, s, mod, cos, sin, w, side=None):
    bias, pre_g, post_g = w["bias"], w["pre_g"], w["post_g"]
    dy, do, drg, da, dmg, dpost_g, dgate = _ev_out_bwd(s["y"], dx, s["o2"], s["z"], s["a"], w["w_out"], post_g, mod, bias)
    dw_out = _mm_tn("ev_out_dw", s["mix"], dy, BF16)
    drq, drk, drv, dlg = _ret_bwd(w["lg"], s["rq"], s["rk"], s["rv"], do, s["states"])
    dQ, dK, dV, *got = _flash_bwd(s["Q"], s["K"], s["V"], da, s["a"], s["lse"], side)
    dqf, dkv, dqn, dkvn, dkr = _ev_mid_bwd(dQ, dK, dV, cos, sin, w["w_uq"], w["w_ukv"])
    dw_uq = _mm_tn("ev_uq_dw", s["qn"], dqf, BF16)
    dw_ukv = _mm_tn("ev_ukv_dw", s["kvn"], dkv, BF16)
    dz, dx_new, dqg, dkvg, dpre_g, dscale, dshift = _ev_in_bwd(
        s["z"], cos, sin, w["qg"], w["kvg"], drq, drk, drv, drg, dqn, dkvn, dkr, dmg, s["x"], dx, w["w_in"], pre_g, mod, bias)
    dw_in = _mm_tn("ev_in_dw", s["h"], dz, BF16)
    g = dict(ev_w_in=_w_in_unpad(dw_in), ev_w_uq=_unpad_heads(dw_uq, MLA_HEADS, MLA_NOPE_DIM + MLA_ROPE_DIM, QW),
             ev_w_ukv=dw_ukv, ev_w_out=dw_out, ev_q_norm_g=dqg[0], ev_kv_norm_g=dkvg[0],
             dlg_f=dlg[0, :RET_HEADS, 0], dlg_b=dlg[1, :RET_HEADS, 0])
    return dx_new, g, dpre_g[0], dpost_g[0], jnp.concatenate([dshift, dscale, dgate], axis=1), (got[0] if got else None)


def _odd_fwd(x, mod, w, tgt=None):
    bias, pre_g, post_g = w["bias"], w["pre_g"], w["post_g"]
    h, z, u = _od_in_fwd(x, pre_g, mod, bias, w["w_in"], w["b_in"])
    uc = _conv_fwd(u, w["dw"], w["dw_b"])
    vv, y, *rest = _od_out_fwd(uc, z, x, w["w_out"], w["b_in"], w["ln_g"], w["ln_b"], post_g, mod, bias, tgt)
    return (rest[0] if tgt is None else tuple(rest)), dict(x=x, h=h, z=z, u=u, uc=uc, vv=vv, y=y)


def _odd_bwd(dx, s, mod, w):
    bias, pre_g, post_g = w["bias"], w["pre_g"], w["post_g"]
    dy, duc, dg, dpost_g, dgate, dbg, dln_g, dln_b = _od_out_bwd(
        s["y"], dx, s["uc"], s["z"], w["w_out"], w["b_in"], w["ln_g"], w["ln_b"], post_g, mod, bias)
    dw_out = _mm_tn("od_out_dw", s["vv"], dy, BF16)
    du = _conv_fwd(duc, w["dw_flip"], jnp.zeros_like(w["dw_b"]))
    dwb = _conv_bwd_w(s["u"], duc)
    dz, dx_new, dba, dbb, dpre_g, dscale, dshift = _od_in_bwd(s["z"], w["b_in"], du, dg, s["x"], dx, w["w_in"], pre_g, mod, bias)
    dw_in = _mm_tn("od_in_dw", s["h"], dz, BF16)
    g = dict(od_w_in=dw_in, od_b_in=jnp.concatenate([dba, dbb, dbg], axis=1)[0], od_dw_w=dwb[:CONV_KERNEL], od_dw_b=dwb[CONV_KERNEL],
             od_ln_g=dln_g[0], od_ln_b=dln_b[0], od_w_out=dw_out)
    return dx_new, g, dpre_g[0], dpost_g[0], jnp.concatenate([dshift, dscale, dgate], axis=1)


LATE_LAYERS = tuple(range(1, DEPTH))


def _step(x, mod, positions, loss_target, w):
    cos, sin = _rope_tables(positions)
    f32 = lambda a: a.astype(F32)
    mods = [mod[l:l + 1] for l in range(DEPTH)]
    first, late = _layer_entries([0]), _layer_entries(LATE_LAYERS)

    gathered = _halves_merge("gather_merge_first", _chip_exchange("gather_chips_first", _pack_weights(w, first), "half"))
    full = _unpack_weights(gathered, w, first)
    lw = {0: _layer_weights(0, full, w)}
    x, s0, got = _even_fwd(x, mods[0], cos, sin, lw[0], side=(_pack_weights(w, late), "half"))
    full.update(_unpack_weights(_halves_merge("gather_merge_late", got), w, late))
    saved = [s0]
    assert DEPTH % 2 == 0
    for l in LATE_LAYERS:
        lw[l] = _layer_weights(l, full, w)
        if l % 2 == 0:
            x, s, _ = _even_fwd(x, mods[l], cos, sin, lw[l])
        else:
            x, s = _odd_fwd(x, mods[l], lw[l], loss_target if l == DEPTH - 1 else None)
        saved.append(s)
    dx, loss = x

    grads = {}
    dpre, dpost, dmod = [None] * DEPTH, [None] * DEPTH, [None] * DEPTH
    for l in reversed(LATE_LAYERS):
        if l % 2 == 0:
            dx, g, dpre[l], dpost[l], dmod[l], _ = _even_bwd(dx, saved[l], mods[l], cos, sin, lw[l])
        else:
            dx, g, dpre[l], dpost[l], dmod[l] = _odd_bwd(dx, saved[l], mods[l], lw[l])
        grads.update({(name, l // 2): val for name, val in g.items()})
    chip_sum = _reduce_begin("late", _pack_grads(grads, late, ()))
    dx, g, dpre[0], dpost[0], dmod[0], got = _even_bwd(dx, saved[0], mods[0], cos, sin, lw[0], side=(chip_sum, "scatter"))
    grads.update({(name, 0): val for name, val in g.items()})
    red = _unpack_grads(_reduce_end("late", got), w, late, ())

    n_ev = (DEPTH + 1) // 2
    grads.update(pre_g=jnp.stack(dpre), post_g=jnp.stack(dpost), ada_b=jnp.concatenate(dmod, axis=0))
    for name in ("ev_q_norm_g", "ev_kv_norm_g"):
        grads[name] = jnp.stack([grads[(name, i)] for i in range(n_ev)])
    grads["ev_dec_f"] = jnp.stack([grads[("dlg_f", i)] for i in range(n_ev)]) * jax.nn.sigmoid(-f32(w["ev_dec_f"]))
    grads["ev_dec_b"] = jnp.stack([grads[("dlg_b", i)] for i in range(n_ev)]) * jax.nn.sigmoid(-f32(w["ev_dec_b"]))
    chip_sum = _reduce_begin("first", _pack_grads(grads, first, REPLICATED))
    total = _reduce_end("first", _chip_exchange("reduce_chips_first", chip_sum, "scatter"))
    red.update(_unpack_grads(total, w, first, REPLICATED))

    out = {name: red[name] for name in REPLICATED}
    for name, _, _ in EV_SHARDED:
        out[name] = jnp.stack([red[(name, i)] for i in range(n_ev)])
    for name, _, _ in OD_SHARDED:
        out[name] = jnp.stack([red[(name, i)] for i in range(DEPTH // 2)])
    return loss[0, 0], dx, out, grads["ada_b"]


ADA_ROWS = 16
ADA_COLS = 3 * D_MODEL // 4
ADA_PACK = (8, 1024)


def _ada_pack(a):
    n = ADA_PACK[0] * ADA_PACK[1]
    return jnp.pad(a, ((0, 0), (0, n - a.shape[1]))).reshape(4, *ADA_PACK)


def _ada_unpack(a):
    return a.reshape(4, -1)[:, :DEPTH * ADA_COLS]


def _ada_forward(c, ada_w):
    D = D_MODEL
    c16 = jnp.zeros((ADA_ROWS, D), F32).at[0].set(c[0])
    c_act = _rowcall("silu_c", lambda rows, p: ([_silu(rows[0][...])], []), [c16], [], [(D, BF16)])[0]
    got = _chip_exchange("ada_c", c_act, "whole")
    c4 = jnp.pad(got[:, 0, :], ((0, ADA_ROWS - 4), (0, 0)))
    per_layer = [_mm_nn("ada", c4, ada_w[l].astype(BF16))[:4] for l in range(DEPTH)]
    p = jnp.stack(per_layer, axis=1).reshape(4, DEPTH * ADA_COLS)
    got = _chip_exchange("ada_mod", _ada_pack(p), "scatter")
    mod = _ada_unpack(got).reshape(4, DEPTH, ADA_COLS).transpose(1, 0, 2).reshape(DEPTH, 3 * D)
    return mod, c4


def _ada_backward(dmod, c4):
    p = dmod.reshape(DEPTH, 4, ADA_COLS).transpose(1, 0, 2).reshape(4, DEPTH * ADA_COLS)
    got = _chip_exchange("ada_dmod", _ada_pack(p), "scatter")
    dm4 = jnp.pad(_ada_unpack(got), ((0, ADA_ROWS - 4), (0, 0)))
    both = _halves_merge("ada_merge", jnp.concatenate([dm4, c4.astype(F32)], axis=1))
    dm32, c32 = both[:, :DEPTH * ADA_COLS].astype(BF16), both[:, DEPTH * ADA_COLS:].astype(BF16)
    return jnp.stack([_mm_tn("ada_dw", c32, dm32[:, l * ADA_COLS:(l + 1) * ADA_COLS]) for l in range(DEPTH)])


def kernel(x, c, positions, ada_w, ada_b, pre_g, post_g, ev_w_in, ev_dec_f, ev_dec_b, ev_q_norm_g, ev_w_uq, ev_kv_norm_g, ev_w_ukv, ev_w_out, od_w_in, od_b_in, od_dw_w, od_dw_b, od_ln_g, od_ln_b, od_w_out, loss_target, m_ada_w, m_ada_b, m_pre_g, m_post_g, m_ev_w_in, m_ev_dec_f, m_ev_dec_b, m_ev_q_norm_g, m_ev_w_uq, m_ev_kv_norm_g, m_ev_w_ukv, m_ev_w_out, m_od_w_in, m_od_b_in, m_od_dw_w, m_od_dw_b, m_od_ln_g, m_od_ln_b, m_od_w_out, v_ada_w, v_ada_b, v_pre_g, v_post_g, v_ev_w_in, v_ev_dec_f, v_ev_dec_b, v_ev_q_norm_g, v_ev_w_uq, v_ev_kv_norm_g, v_ev_w_ukv, v_ev_w_out, v_od_w_in, v_od_b_in, v_od_dw_w, v_od_dw_b, v_od_ln_g, v_od_ln_b, v_od_w_out):
    w = dict(ada_w=ada_w, ada_b=ada_b, pre_g=pre_g, post_g=post_g, ev_w_in=ev_w_in, ev_dec_f=ev_dec_f, ev_dec_b=ev_dec_b,
             ev_q_norm_g=ev_q_norm_g, ev_w_uq=ev_w_uq, ev_kv_norm_g=ev_kv_norm_g, ev_w_ukv=ev_w_ukv, ev_w_out=ev_w_out,
             od_w_in=od_w_in, od_b_in=od_b_in, od_dw_w=od_dw_w, od_dw_b=od_dw_b, od_ln_g=od_ln_g, od_ln_b=od_ln_b, od_w_out=od_w_out)
    m = dict(ada_w=m_ada_w, ada_b=m_ada_b, pre_g=m_pre_g, post_g=m_post_g, ev_w_in=m_ev_w_in, ev_dec_f=m_ev_dec_f, ev_dec_b=m_ev_dec_b,
             ev_q_norm_g=m_ev_q_norm_g, ev_w_uq=m_ev_w_uq, ev_kv_norm_g=m_ev_kv_norm_g, ev_w_ukv=m_ev_w_ukv, ev_w_out=m_ev_w_out,
             od_w_in=m_od_w_in, od_b_in=m_od_b_in, od_dw_w=m_od_dw_w, od_dw_b=m_od_dw_b, od_ln_g=m_od_ln_g, od_ln_b=m_od_ln_b, od_w_out=m_od_w_out)
    v = dict(ada_w=v_ada_w, ada_b=v_ada_b, pre_g=v_pre_g, post_g=v_post_g, ev_w_in=v_ev_w_in, ev_dec_f=v_ev_dec_f, ev_dec_b=v_ev_dec_b,
             ev_q_norm_g=v_ev_q_norm_g, ev_w_uq=v_ev_w_uq, ev_kv_norm_g=v_ev_kv_norm_g, ev_w_ukv=v_ev_w_ukv, ev_w_out=v_ev_w_out,
             od_w_in=v_od_w_in, od_b_in=v_od_b_in, od_dw_w=v_od_dw_w, od_dw_b=v_od_dw_b, od_ln_g=v_od_ln_g, od_ln_b=v_od_ln_b, od_w_out=v_od_w_out)

    mod, c4 = _ada_forward(c, ada_w)
    loss_local, grad_x, g, dmod = _step(x[0], mod, positions[0], loss_target[0], w)
    loss = lax.psum(loss_local, ("x", "y", "c"))
    g["ada_w"] = _ada_backward(dmod, c4)
    delta, new_m, new_v = {}, {}, {}
    for name in WEIGHTS:
        delta[name], new_m[name], new_v[name] = _adamw(w[name], g[name], m[name], v[name])
    return (loss, grad_x[None], *[g[n] for n in WEIGHTS], *[delta[n] for n in WEIGHTS],
            *[new_m[n] for n in WEIGHTS], *[new_v[n] for n in WEIGHTS])
```

```python
import functools

import jax
import jax.numpy as jnp
from jax import lax
from jax.experimental import pallas as pl
from jax.experimental.pallas import tpu as pltpu

F32 = jnp.float32
BF16 = jnp.bfloat16

D_MODEL = 1024
DEPTH = 4
RET_HEADS = 4
RET_QK_DIM = 64
RET_V_DIM = 128
MLA_HEADS = 4
MLA_Q_RANK = 384
MLA_KV_RANK = 256
MLA_NOPE_DIM = 128
MLA_ROPE_DIM = 64
MLA_V_DIM = 128
ROPE_DIM = 64
ROPE_BASE = 10000.0
CONV_KERNEL = 31
EPS = 1e-6
ADAM_LR, ADAM_B1, ADAM_B2, ADAM_EPS, ADAM_WD, ADAM_STEP = 0.001, 0.9, 0.999, 1e-08, 0.01, 10

LANES = 128
HP = 128
Z_RQ, Z_RK, Z_RV, Z_RG, Z_MG, Z_CQ, Z_CKV, Z_KR, Z_END = 0, 512, 1024, 1536, 2048, 2560, 2944, 3200, 3328
Z_LAT = Z_END - Z_CQ


def _z_ret(z):
    return (z, Z_RG, 0)


def _z_gates(z):
    return [(z, 512, Z_RG // 512), (z, 512, Z_MG // 512)]


def _z_latents(z):
    return [(z, 256, Z_CQ // 256 + i) for i in range(Z_LAT // 256)]
V7X_VMEM_BYTES = 64 * 1024 * 1024
VMEM_CAP = V7X_VMEM_BYTES - 8 * 1024 * 1024

MESH = pl.DeviceIdType.MESH


def _nbytes(shape, dtype):
    n = 1
    for s in shape:
        n *= s
    return n * jnp.dtype(dtype).itemsize


def _hbm(x):
    return pltpu.with_memory_space_constraint(x, pltpu.HBM)


def _params(sem, block_bytes):
    limit = min(VMEM_CAP, max(32 * 1024 * 1024, 2 * block_bytes + 16 * 1024 * 1024))
    return pltpu.CompilerParams(dimension_semantics=sem, vmem_limit_bytes=limit)


ROW_TILE = 512


def _rowcall(name, fn, rows, params, outs, accs=(), ts=ROW_TILE, mats=()):
    rows = [r if isinstance(r, tuple) else (r, r.shape[1], 0) for r in rows]
    S = rows[0][0].shape[0]
    ts = min(ts, S)
    assert S % ts == 0, (name, S, ts)
    nr, npar, nm, no, na = len(rows), len(params), len(mats), len(outs), len(accs)

    def body(*refs):
        row_refs = refs[:nr]
        pvals = [r[...] for r in refs[nr:nr + npar]]
        mat_refs = refs[nr + npar:nr + npar + nm]
        out_refs = refs[nr + npar + nm:nr + npar + nm + no]
        acc_refs = refs[nr + npar + nm + no:]
        ovals, avals = fn(row_refs, pvals, mat_refs) if nm else fn(row_refs, pvals)
        for r, v in zip(out_refs, ovals, strict=True):
            r[...] = v.astype(r.dtype)
        if na:
            @pl.when(pl.program_id(0) == 0)
            def _():
                for r in acc_refs:
                    r[...] = jnp.zeros_like(r)
            for r, v in zip(acc_refs, avals, strict=True):
                r[...] += v

    in_specs = [pl.BlockSpec((ts, w), functools.partial(lambda i, b: (i, b), b=blk)) for _, w, blk in rows]
    in_specs += [pl.BlockSpec(p.shape, lambda i: (0, 0)) for p in [*params, *mats]]
    out_specs = [pl.BlockSpec((ts, w), lambda i: (i, 0)) for w, _ in outs]
    out_specs += [pl.BlockSpec(s, lambda i: (0, 0)) for s in accs]
    out_shape = [jax.ShapeDtypeStruct((S, w), dt) for w, dt in outs]
    out_shape += [jax.ShapeDtypeStruct(s, F32) for s in accs]
    nb = sum(_nbytes((ts, w), a.dtype) for a, w, _ in rows) + sum(_nbytes((ts, w), dt) for w, dt in outs)
    nb += sum(_nbytes(p.shape, p.dtype) for p in params) + sum(_nbytes(s, F32) for s in accs)
    res = pl.pallas_call(
        body, name=name, grid=(S // ts,), in_specs=in_specs, out_specs=out_specs, out_shape=out_shape,
        compiler_params=_params(("arbitrary",) if na else ("parallel",), 3 * nb + sum(_nbytes(m.shape, m.dtype) for m in mats)),
    )(*[_hbm(a) for a, _, _ in rows], *params, *[_hbm(m) for m in mats])
    return res


def _silu(x):
    return x * jax.nn.sigmoid(x)


def _rms(x, g):
    return x * lax.rsqrt(jnp.mean(x * x, axis=-1, keepdims=True) + EPS) * g


def _rot(x):
    lane = lax.broadcasted_iota(jnp.int32, x.shape, 1)
    return jnp.where(lane < ROPE_DIM // 2, -pltpu.roll(x, LANES - ROPE_DIM // 2, 1), pltpu.roll(x, ROPE_DIM // 2, 1))


def _rope(x, cos, sin):
    return x * cos + _rot(x) * sin


def _rope_t(dy, cos, sin):
    return dy * cos - _rot(dy) * sin


def _groups(ref, start, n):
    return [ref[:, start + HP * h:start + HP * (h + 1)] for h in range(n)]


def _pre_math(x, g, m_scale, b_scale, m_shift, b_shift):
    return _rms(x, g) * (1.0 + (m_scale + b_scale)) + (m_shift + b_shift)


def _post_math(x, y, g, m_gate, b_gate):
    return x + (m_gate + b_gate) * _rms(y, g)


def _ev_mid_fwd(z, cos, sin, qg, kvg, w_uq, w_ukv):
    qscale = ATT_SCALE * LOG2E

    def fn(rows, p, mats):
        z_, l0, l1, l2, c_, s_ = rows
        qg_, kvg_ = p
        cos_, sin_ = c_[...], s_[...]
        lat = jnp.concatenate([l0[...], l1[...], l2[...]], axis=1)
        rq = jnp.concatenate([_rope(v[...], cos_, sin_) for v in _groups(z_, Z_RQ, RET_HEADS)], axis=1)
        rk = jnp.concatenate([_rope(v[...], cos_, sin_) for v in _groups(z_, Z_RK, RET_HEADS)], axis=1)
        rk = rk * (RET_QK_DIM ** -0.5)
        rv = z_[:, Z_RV:Z_RG]
        qn = _rms(lat[:, :Z_CKV - Z_CQ], qg_).astype(BF16)
        kvn = _rms(lat[:, Z_CKV - Z_CQ:Z_KR - Z_CQ], kvg_).astype(BF16)
        kr = _rope(lat[:, Z_KR - Z_CQ:], cos_, sin_)
        qf = jnp.dot(qn, mats[0][...], preferred_element_type=F32)
        kv = jnp.dot(kvn, mats[1][...], preferred_element_type=F32)
        qs, ks, vs = [], [], []
        for h in range(MLA_HEADS):
            b = 2 * HP * h
            qs += [qf[:, b:b + HP] * qscale, _rope(qf[:, b + HP:b + 2 * HP], cos_, sin_) * qscale]
            ks += [kv[:, b:b + HP], kr]
            vs += [kv[:, b + HP:b + 2 * HP]]
        return [rq, rk, rv, qn, kvn, jnp.concatenate(qs, axis=1), jnp.concatenate(ks, axis=1), jnp.concatenate(vs, axis=1)], []

    return _rowcall("ev_mid_fwd", fn, [_z_ret(z), *_z_latents(z), cos, sin], [qg, kvg],
                    [(512, BF16), (512, BF16), (512, BF16), (MLA_Q_RANK, BF16), (MLA_KV_RANK, BF16),
                     (1024, BF16), (1024, BF16), (512, BF16)], mats=[w_uq, w_ukv])


def _pre_bwd_math(dh, x, res, g, mod, bias):
    D = D_MODEL
    _, vjp = jax.vjp(_pre_math, x, g, mod[:, D:2 * D], bias[:, D:2 * D], mod[:, :D], bias[:, :D])
    dx, dg, dsc, _, dsh, _ = vjp(dh)
    return dx + res, dg, dsc, dsh


def _ev_in_bwd(z, cos, sin, qg, kvg, drq, drk, drv, drg, dqn, dkvn, dkr, dmg, x, dx_res, w_in, g, mod, bias):
    D = D_MODEL

    def fn(rows, p, mats):
        l0, l1, l2, c_, s_, drq_, drk_, drv_, drg_, dqn_, dkvn_, dkr_, dmg_, x_, res_ = rows
        qg_, kvg_, g_, mod_, b_ = p
        cos_, sin_ = c_[...], s_[...]
        lat = jnp.concatenate([l0[...], l1[...], l2[...]], axis=1)
        parts = []
        for h in range(RET_HEADS):
            t = drq_[:, HP * h:HP * (h + 1)].astype(F32) + drq_[:, 512 + HP * h:512 + HP * (h + 1)].astype(F32)
            parts.append(_rope_t(t, cos_, sin_))
        for h in range(RET_HEADS):
            t = drk_[:, HP * h:HP * (h + 1)].astype(F32) + drk_[:, 512 + HP * h:512 + HP * (h + 1)].astype(F32)
            parts.append(_rope_t(t, cos_, sin_) * (RET_QK_DIM ** -0.5))
        parts.append(drv_[:, :512].astype(F32) + drv_[:, 512:].astype(F32))
        parts += [drg_[...], dmg_[...]]
        _, vq = jax.vjp(_rms, lat[:, :Z_CKV - Z_CQ], qg_)
        dcq, dqg = vq(dqn_[...])
        _, vkv = jax.vjp(_rms, lat[:, Z_CKV - Z_CQ:Z_KR - Z_CQ], kvg_)
        dckv, dkvg = vkv(dkvn_[...])
        parts += [dcq, dckv, _rope_t(dkr_[...], cos_, sin_)]
        dz = jnp.concatenate([v.astype(BF16) for v in parts], axis=1)
        dh = lax.dot_general(dz, mats[0][...], NT_DIMS, preferred_element_type=F32)
        dx, dg, dsc, dsh = _pre_bwd_math(dh, x_[...], res_[...], g_, mod_, b_)
        return [dz, dx], [dqg, dkvg, dg, dsc, dsh]

    rows = [*_z_latents(z), cos, sin, drq, drk, drv, drg, dqn, dkvn, dkr, dmg, x, dx_res]
    return _rowcall("ev_in_bwd", fn, rows, [qg, kvg, g, mod, bias], [(Z_END, BF16), (D, F32)],
                    [(1, MLA_Q_RANK), (1, MLA_KV_RANK)] + [(1, D)] * 3, ts=256, mats=[w_in])


ATT_SCALE = (MLA_NOPE_DIM + MLA_ROPE_DIM) ** -0.5
LOG2E = 1.4426950408889634
LN2 = 0.6931471805599453


def _ev_mid_bwd(dQ, dK, dV, cos, sin, w_uq, w_ukv):
    def fn(rows, p, mats):
        dq_, dk_, dv_, c_, s_ = rows
        cos_, sin_ = c_[...], s_[...]
        dqs, dkvs = [], []
        dkr = None
        for h in range(MLA_HEADS):
            b = 2 * HP * h
            dqs += [dq_[:, b:b + HP] * ATT_SCALE, _rope_t(dq_[:, b + HP:b + 2 * HP], cos_, sin_) * ATT_SCALE]
            dkvs += [dk_[:, b:b + HP] * LN2, dv_[:, HP * h:HP * (h + 1)]]
            t = dk_[:, b + HP:b + 2 * HP]
            dkr = t if dkr is None else dkr + t
        dqf = jnp.concatenate(dqs, axis=1).astype(BF16)
        dkv = jnp.concatenate(dkvs, axis=1).astype(BF16)
        dqn = lax.dot_general(dqf, mats[0][...], NT_DIMS, preferred_element_type=F32)
        dkvn = lax.dot_general(dkv, mats[1][...], NT_DIMS, preferred_element_type=F32)
        return [dqf, dkv, dqn, dkvn, dkr * LN2], []

    return _rowcall("ev_mid_bwd", fn, [dQ, dK, dV, cos, sin], [],
                    [(1024, BF16), (1024, BF16), (MLA_Q_RANK, F32), (MLA_KV_RANK, F32), (HP, F32)], mats=[w_uq, w_ukv])


def _mix_math(o, rg, a, mg):
    outs = []
    for h in range(RET_HEADS):
        oh = o[:, HP * h:HP * (h + 1)]
        mu = jnp.mean(oh, axis=-1, keepdims=True)
        var = jnp.mean(jnp.square(oh - mu), axis=-1, keepdims=True)
        outs.append((oh - mu) * lax.rsqrt(var + EPS))
    ret = jnp.concatenate(outs, axis=1) * _silu(rg)
    return jnp.concatenate([ret, a * _silu(mg)], axis=1)


def _ev_out_fwd(o2, z, a, x, w_out, g, mod, bias):
    D = D_MODEL

    def fn(rows, p, mats):
        o_, rg_, mg_, a_, x_ = rows
        g_, mod_, b_ = p
        mix = _mix_math(o_[:, :512] + o_[:, 512:], rg_[...], a_[...], mg_[...]).astype(BF16)
        y = jnp.dot(mix, mats[0][...], preferred_element_type=F32)
        return [mix, y, _post_math(x_[...], y, g_, mod_[:, 2 * D:], b_[:, 2 * D:])], []

    return _rowcall("ev_out", fn, [o2, *_z_gates(z), a, x], [g, mod, bias], [(1024, BF16), (D, F32), (D, F32)], mats=[w_out])


def _post_bwd_math(y, dxo, g, mod, bias):
    D = D_MODEL
    b_gate = bias[:, 2 * D:]
    _, vjp = jax.vjp(lambda y_, g2, mg: _post_math(0.0, y_, g2, mg, b_gate), y, g, mod[:, 2 * D:])
    return vjp(dxo)


def _ev_out_bwd(y, dxo, o2, z, a, w_out, g, mod, bias):
    D = D_MODEL

    def fn(rows, p, mats):
        y_, dxo_, o_, rg_, mg_, a_ = rows
        g_, mod_, b_ = p
        dy, dg, dgate = _post_bwd_math(y_[...], dxo_[...], g_, mod_, b_)
        dyb = dy.astype(BF16)
        dmix = lax.dot_general(dyb, mats[0][...], NT_DIMS, preferred_element_type=F32)
        _, vjp = jax.vjp(_mix_math, o_[:, :512] + o_[:, 512:], rg_[...], a_[...], mg_[...])
        return [dyb, *vjp(dmix)], [dg, dgate]

    return _rowcall("ev_out_bwd", fn, [y, dxo, o2, *_z_gates(z), a], [g, mod, bias],
                    [(D, BF16), (512, BF16), (512, F32), (512, BF16), (512, F32)], [(1, D)] * 2, mats=[w_out])


def _glu_math(a, b, ba, bb):
    return (a + ba) * jax.nn.sigmoid(b + bb)


def _od_in_fwd(x, g, mod, bias, w_in, bin_):
    D = D_MODEL

    def fn(rows, p, mats):
        g_, mod_, b_, bi_ = p
        h = _pre_math(rows[0][...], g_, mod_[:, D:2 * D], b_[:, D:2 * D], mod_[:, :D], b_[:, :D]).astype(BF16)
        z = jnp.dot(h, mats[0][...], preferred_element_type=F32)
        return [h, z, _glu_math(z[:, :D], z[:, D:2 * D], bi_[:, :D], bi_[:, D:2 * D])], []

    return _rowcall("od_in", fn, [x], [g, mod, bias, bin_], [(D, BF16), (3 * D, F32), (D, F32)], mats=[w_in])


def _odmix_math(uc, g, bg, ln_g, ln_b):
    mu = jnp.mean(uc, axis=-1, keepdims=True)
    var = jnp.mean(jnp.square(uc - mu), axis=-1, keepdims=True)
    y = (uc - mu) * lax.rsqrt(var + EPS) * ln_g + ln_b
    return _silu(y) * _silu(g + bg)


def _od_out_fwd(uc, z, x, w_out, bin_, ln_g, ln_b, g, mod, bias, tgt=None):
    D = D_MODEL

    def fn(rows, p, mats):
        uc_, zg_, x_ = rows[:3]
        bi_, lg_, lb_, g_, mod_, b_ = p
        vv = _odmix_math(uc_[...], zg_[...], bi_[:, 2 * D:], lg_, lb_).astype(BF16)
        y = jnp.dot(vv, mats[0][...], preferred_element_type=F32)
        xn = _post_math(x_[...], y, g_, mod_[:, 2 * D:], b_[:, 2 * D:])
        if tgt is None:
            return [vv, y, xn], []
        err = xn - rows[3][...]
        part = 0.5 * jnp.sum(jnp.mean(err * err, axis=-1, keepdims=True), axis=0, keepdims=True)
        return [vv, y, err * (1.0 / D)], [jnp.broadcast_to(part, (1, LANES))]

    rows = [uc, (z, D, 2), x] + ([] if tgt is None else [tgt])
    return _rowcall("od_out" if tgt is None else "od_out_loss", fn, rows, [bin_, ln_g, ln_b, g, mod, bias],
                    [(D, BF16), (D, F32), (D, F32)], [] if tgt is None else [(1, LANES)], mats=[w_out])


def _od_out_bwd(y, dxo, uc, z, w_out, bin_, ln_g, ln_b, g, mod, bias):
    D = D_MODEL

    def fn(rows, p, mats):
        y_, dxo_, uc_, zg_ = rows
        bi_, lg_, lb_, g_, mod_, b_ = p
        dy, dpg, dgate = _post_bwd_math(y_[...], dxo_[...], g_, mod_, b_)
        dyb = dy.astype(BF16)
        dvv = lax.dot_general(dyb, mats[0][...], NT_DIMS, preferred_element_type=F32)
        _, vjp = jax.vjp(_odmix_math, uc_[...], zg_[...], bi_[:, 2 * D:], lg_, lb_)
        duc, dg, dbg, dlg, dlb = vjp(dvv)
        return [dyb, duc, dg], [dpg, dgate, dbg, dlg, dlb]

    return _rowcall("od_out_bwd", fn, [y, dxo, uc, (z, D, 2)], [bin_, ln_g, ln_b, g, mod, bias],
                    [(D, BF16), (D, F32), (D, BF16)], [(1, D)] * 5, mats=[w_out])


def _od_in_bwd(z, bin_, du, dg, x, dx_res, w_in, g, mod, bias):
    D = D_MODEL

    def fn(rows, p, mats):
        za_, zb_, du_, dg_, x_, res_ = rows
        bi_, g_, mod_, b_ = p
        _, vjp = jax.vjp(_glu_math, za_[...], zb_[...], bi_[:, :D], bi_[:, D:2 * D])
        da, db, dba, dbb = vjp(du_[...])
        dz = jnp.concatenate([da.astype(BF16), db.astype(BF16), dg_[...]], axis=1)
        dh = lax.dot_general(dz, mats[0][...], NT_DIMS, preferred_element_type=F32)
        dx, dpg, dsc, dsh = _pre_bwd_math(dh, x_[...], res_[...], g_, mod_, b_)
        return [dz, dx], [dba, dbb, dpg, dsc, dsh]

    return _rowcall("od_in_bwd", fn, [(z, D, 0), (z, D, 1), du, dg, x, dx_res], [bin_, g, mod, bias],
                    [(3 * D, BF16), (D, F32)], [(1, D)] * 5, ts=256, mats=[w_in])


def _tile(n, cap):
    if n <= cap:
        return n
    best = None
    for t in range(LANES, cap + 1, LANES):
        if n % t == 0:
            best = t
    assert best is not None, (n, cap)
    return best


MM_TN_CAP = 1792
NT_DIMS = (((1,), (1,)), ((), ()))
TN_DIMS = (((0,), (0,)), ((), ()))


def _mm_nn(name, a, b, out_dtype=F32, tm=512, tn_cap=MM_TN_CAP):
    M, K = a.shape
    N = b.shape[1]
    tm = min(tm, M)
    tn = _tile(N, tn_cap)

    def body(a_ref, b_ref, o_ref):
        o_ref[...] = jnp.dot(a_ref[...], b_ref[...], preferred_element_type=F32).astype(o_ref.dtype)

    nb = _nbytes((tm, K), a.dtype) + _nbytes((K, tn), b.dtype) + _nbytes((tm, tn), out_dtype) + _nbytes((tm, tn), F32)
    return pl.pallas_call(
        body, name=name, grid=(N // tn, M // tm),
        in_specs=[pl.BlockSpec((tm, K), lambda j, i: (i, 0)), pl.BlockSpec((K, tn), lambda j, i: (0, j))],
        out_specs=pl.BlockSpec((tm, tn), lambda j, i: (i, j)),
        out_shape=jax.ShapeDtypeStruct((M, N), out_dtype),
        compiler_params=_params(("parallel", "parallel"), nb),
    )(_hbm(a), _hbm(b))


def _pre_mm(name, x, g, mod, bias, w, tm=512):
    S, D = x.shape
    N = w.shape[1]
    tm = min(tm, S)

    def body(x_ref, g_ref, mod_ref, bias_ref, w_ref, h_ref, z_ref):
        mod_, b_ = mod_ref[...], bias_ref[...]
        h = _pre_math(x_ref[...], g_ref[...], mod_[:, D:2 * D], b_[:, D:2 * D], mod_[:, :D], b_[:, :D]).astype(BF16)
        h_ref[...] = h
        z_ref[...] = jnp.dot(h, w_ref[...], preferred_element_type=F32)

    nb = 2 * _nbytes((tm, D), F32) + _nbytes((D, N), w.dtype) + 2 * _nbytes((tm, N), F32)
    row = lambda i: (i, 0)
    whole = lambda i: (0, 0)
    return pl.pallas_call(
        body, name=name, grid=(S // tm,),
        in_specs=[pl.BlockSpec((tm, D), row), pl.BlockSpec(g.shape, whole), pl.BlockSpec(mod.shape, whole),
                  pl.BlockSpec(bias.shape, whole), pl.BlockSpec((D, N), whole)],
        out_specs=[pl.BlockSpec((tm, D), row), pl.BlockSpec((tm, N), row)],
        out_shape=[jax.ShapeDtypeStruct((S, D), BF16), jax.ShapeDtypeStruct((S, N), F32)],
        compiler_params=_params(("parallel",), nb),
    )(_hbm(x), g, mod, bias, _hbm(w))


def _mm_tn(name, a, b, out_dtype=F32, ts=1024, tm_cap=1024, tn_cap=MM_TN_CAP):
    S, M = a.shape
    N = b.shape[1]
    ts = min(ts, S)
    tm = _tile(M, tm_cap)
    tn = _tile(N, tn_cap)
    ns = S // ts

    def body(a_ref, b_ref, o_ref, acc):
        s = pl.program_id(2)

        @pl.when(s == 0)
        def _():
            acc[...] = jnp.zeros(acc.shape, F32)

        acc[...] += lax.dot_general(a_ref[...], b_ref[...], TN_DIMS, preferred_element_type=F32)

        @pl.when(s == ns - 1)
        def _():
            o_ref[...] = acc[...].astype(o_ref.dtype)

    nb = _nbytes((ts, tm), a.dtype) + _nbytes((ts, tn), b.dtype) + 3 * _nbytes((tm, tn), F32)
    return pl.pallas_call(
        body, name=name, grid=(M // tm, N // tn, ns),
        in_specs=[pl.BlockSpec((ts, tm), lambda i, j, s: (s, i)), pl.BlockSpec((ts, tn), lambda i, j, s: (s, j))],
        out_specs=pl.BlockSpec((tm, tn), lambda i, j, s: (i, j)),
        out_shape=jax.ShapeDtypeStruct((M, N), out_dtype),
        scratch_shapes=[pltpu.VMEM((tm, tn), F32)],
        compiler_params=_params(("parallel", "parallel", "arbitrary"), nb),
    )(_hbm(a), _hbm(b))


ATT_TQ = 1024
ATT_TK = 1024
ATT_TK_FWD = 1024
ATT_TQ_FWD = 2048
QW = 2 * HP


class _Side:
    def __init__(self, side):
        self.on = side is not None
        if self.on:
            self.p, self.mode = side
            shp = self.p.shape
            self.h, self.w = (shp[0] // 2, shp[1]) if self.mode == "half" else (shp[-2], shp[-1])
        self.in_specs = [_HBM] if self.on else []
        self.out_specs = [_HBM] if self.on else []
        self.out_shape = [jax.ShapeDtypeStruct((4, self.h, self.w), self.p.dtype)] if self.on else []
        self.scratch = list(_CHIP_SEMS) if self.on else []
        self.operands = [_hbm(self.p)] if self.on else []

    def copies(self, p_ref, x_ref, sems):
        return _chip_copies(p_ref, x_ref, *sems, self.mode, self.h)

    def start(self, p_ref, x_ref, sems, first):
        @pl.when(first)
        def _():
            for cp in self.copies(p_ref, x_ref, sems):
                cp.start()

    def finish(self, p_ref, x_ref, sems, last):
        @pl.when(last)
        def _():
            for cp in self.copies(p_ref, x_ref, sems):
                cp.wait()


def _flash_fwd(Q, K, V, side=None):
    S = Q.shape[0]
    H = MLA_HEADS
    tq, tk = min(ATT_TQ_FWD, S), min(ATT_TK_FWD, S)
    nk = S // tk
    sd = _Side(side)

    def body(*refs):
        if sd.on:
            q_ref, k_ref, vt_ref, p_ref, o_ref, lse_ref, x_ref, m_s, l_s, acc_s, s_a, s_b, *sems = refs
            step_id = pl.program_id(0) * (S // tq) + pl.program_id(1)
            sd.start(p_ref, x_ref, sems, step_id == 0)
        else:
            q_ref, k_ref, vt_ref, o_ref, lse_ref, m_s, l_s, acc_s, s_a, s_b = refs
        m_s[...] = jnp.full(m_s.shape, -jnp.inf, F32)
        l_s[...] = jnp.zeros(l_s.shape, F32)
        acc_s[...] = jnp.zeros(acc_s.shape, F32)

        def scores(j):
            k = k_ref[pl.ds(pl.multiple_of(j * tk, tk), tk), :]
            return lax.dot_general(k, q_ref[...], NT_DIMS, preferred_element_type=F32)

        def consume(st, j):
            m_prev = m_s[...]
            m_new = jnp.maximum(m_prev, jnp.max(st, axis=0, keepdims=True))
            alpha = jnp.exp2(m_prev - m_new)
            pt = jnp.exp2(st - m_new)
            l_s[...] = alpha * l_s[...] + jnp.sum(pt, axis=0, keepdims=True)
            v = vt_ref[pl.ds(pl.multiple_of(j * tk, tk), tk), :]
            acc_s[...] = alpha * acc_s[...] + lax.dot_general(v, pt.astype(BF16), TN_DIMS, preferred_element_type=F32)
            m_s[...] = m_new

        if nk % 2:
            def step(j, carry):
                consume(scores(j), j)
                return carry

            lax.fori_loop(0, nk, step, 0)
        else:
            s_a[...] = scores(0)

            def pair(jj, carry):
                j0 = 2 * jj
                s_b[...] = scores(j0 + 1)
                consume(s_a[...], j0)
                s_a[...] = scores(jnp.minimum(j0 + 2, nk - 1))
                consume(s_b[...], j0 + 1)
                return carry

            lax.fori_loop(0, nk // 2, pair, 0)
        l = l_s[...]
        o_ref[...] = (acc_s[...] * (1.0 / l)).T
        lse_ref[0] = m_s[...] + jnp.log2(l)
        if sd.on:
            sd.finish(p_ref, x_ref, sems, step_id == H * (S // tq) - 1)

    nb = (_nbytes((tq, QW), BF16) + _nbytes((S, QW), BF16) + _nbytes((S, HP), BF16) + 3 * _nbytes((tq, HP), F32)
          + 4 * _nbytes((tq, tk), F32))
    return pl.pallas_call(
        body, name="flash_fwd_side" if sd.on else "flash_fwd", grid=(H, S // tq),
        in_specs=[pl.BlockSpec((tq, QW), lambda h, i: (i, h)), pl.BlockSpec((S, QW), lambda h, i: (0, h)),
                  pl.BlockSpec((S, HP), lambda h, i: (0, h))] + sd.in_specs,
        out_specs=[pl.BlockSpec((tq, HP), lambda h, i: (i, h)), pl.BlockSpec((1, 1, tq), lambda h, i: (h, 0, i))] + sd.out_specs,
        out_shape=[jax.ShapeDtypeStruct((S, H * HP), F32), jax.ShapeDtypeStruct((H, 1, S), F32)] + sd.out_shape,
        scratch_shapes=[pltpu.VMEM((1, tq), F32), pltpu.VMEM((1, tq), F32), pltpu.VMEM((HP, tq), F32),
                        pltpu.VMEM((tk, tq), F32), pltpu.VMEM((tk, tq), F32)] + sd.scratch,
        compiler_params=_params(("arbitrary", "arbitrary") if sd.on else ("parallel", "parallel"), nb),
    )(_hbm(Q), _hbm(K), _hbm(V), *sd.operands)


def _flash_bwd(Q, K, V, dO, O, lse, side=None):
    S = Q.shape[0]
    H = MLA_HEADS
    tq, tk = min(ATT_TQ, S), min(ATT_TK, S)
    nk = S // tk
    sd = _Side(side)

    def body(*refs):
        if sd.on:
            q_ref, do_ref, o_ref, lse_ref, k_ref, v_ref, p_ref, dq_ref, dk_ref, dv_ref, x_ref, dq_s, *sems = refs
            step_id = pl.program_id(0) * (S // tq) + pl.program_id(1)
            sd.start(p_ref, x_ref, sems, step_id == 0)
        else:
            q_ref, do_ref, o_ref, lse_ref, k_ref, v_ref, dq_ref, dk_ref, dv_ref, dq_s = refs

        @pl.when(pl.program_id(1) == 0)
        def _():
            dk_ref[...] = jnp.zeros(dk_ref.shape, F32)
            dv_ref[...] = jnp.zeros(dv_ref.shape, F32)

        delta = jnp.sum((do_ref[...].astype(F32) * o_ref[...]).T, axis=0, keepdims=True)
        lse = lse_ref[0]
        dq_s[...] = jnp.zeros(dq_s.shape, F32)

        def rows(j):
            return pl.ds(pl.multiple_of(j * tk, tk), tk)

        def scores(j):
            st = lax.dot_general(k_ref[rows(j), :], q_ref[...], NT_DIMS, preferred_element_type=F32)
            dpt = lax.dot_general(v_ref[rows(j), :], do_ref[...], NT_DIMS, preferred_element_type=F32)
            return st, dpt

        def consume(st, dpt, j):
            pt = jnp.exp2(st - lse)
            dst = (pt * (dpt - delta)).astype(BF16)
            dv_ref[rows(j), :] += jnp.dot(pt.astype(BF16), do_ref[...], preferred_element_type=F32)
            dk_ref[rows(j), :] += jnp.dot(dst, q_ref[...], preferred_element_type=F32)
            dq_s[...] += lax.dot_general(dst, k_ref[rows(j), :], TN_DIMS, preferred_element_type=F32)

        def step(j, carry):
            consume(*scores(j), j)
            return carry

        lax.fori_loop(0, nk, step, 0)
        dq_ref[...] = dq_s[...]
        if sd.on:
            sd.finish(p_ref, x_ref, sems, step_id == H * (S // tq) - 1)

    nb = (_nbytes((tq, QW), BF16) + _nbytes((tq, HP), BF16) + _nbytes((tq, HP), F32) + _nbytes((S, QW), BF16)
          + _nbytes((S, HP), BF16) + 2 * _nbytes((tq, QW), F32) + _nbytes((S, QW), F32) + _nbytes((S, HP), F32))
    return pl.pallas_call(
        body, name="flash_bwd_side" if sd.on else "flash_bwd", grid=(H, S // tq),
        in_specs=[pl.BlockSpec((tq, QW), lambda h, i: (i, h)), pl.BlockSpec((tq, HP), lambda h, i: (i, h)),
                  pl.BlockSpec((tq, HP), lambda h, i: (i, h)), pl.BlockSpec((1, 1, tq), lambda h, i: (h, 0, i)),
                  pl.BlockSpec((S, QW), lambda h, i: (0, h)), pl.BlockSpec((S, HP), lambda h, i: (0, h))] + sd.in_specs,
        out_specs=[pl.BlockSpec((tq, QW), lambda h, i: (i, h)), pl.BlockSpec((S, QW), lambda h, i: (0, h)),
                   pl.BlockSpec((S, HP), lambda h, i: (0, h))] + sd.out_specs,
        out_shape=[jax.ShapeDtypeStruct((S, H * QW), F32), jax.ShapeDtypeStruct((S, H * QW), F32),
                   jax.ShapeDtypeStruct((S, H * HP), F32)] + sd.out_shape,
        scratch_shapes=[pltpu.VMEM((tq, QW), F32)] + sd.scratch,
        compiler_params=_params(("arbitrary", "arbitrary") if sd.on else ("parallel", "arbitrary"), nb),
    )(_hbm(Q), _hbm(dO), _hbm(O), lse, _hbm(K), _hbm(V), *sd.operands)


RET_CHUNK = 256


def _ret_tables(d, lg_ref, h, C):
    ii = lax.broadcasted_iota(jnp.int32, (C, C), 0).astype(F32)
    jj = lax.broadcasted_iota(jnp.int32, (C, C), 1).astype(F32)
    ci = lax.broadcasted_iota(jnp.int32, (C, 1), 0).astype(F32)
    fwd = d == 0
    dist = jnp.where(fwd, ii - jj, jj - ii)
    mask = dist >= jnp.where(fwd, 0.0, 1.0)
    dist = jnp.maximum(dist, 0.0)
    qpos = jnp.where(fwd, ci + 1.0, C - ci)
    kpos = jnp.where(fwd, C - 1.0 - ci, ci)
    lg = lg_ref[d, h]
    D = jnp.where(mask, jnp.exp(lg * dist), 0.0)
    return D, jnp.exp(lg * qpos), jnp.exp(lg * kpos), jnp.exp(lg * C), dist, qpos, kpos


def _ret_fwd(lg, q, k, v):
    S = q.shape[0]
    C = min(RET_CHUNK, S)
    N = S // C
    H = RET_HEADS

    def chunk(d, n):
        return jnp.where(d == 0, n, N - 1 - n)

    def body(lg_ref, q_ref, k_ref, v_ref, o_ref, st_ref, state, tab_d, tab_q, tab_k):
        d, n = pl.program_id(0), pl.program_id(1)

        @pl.when(n == 0)
        def _():
            state[...] = jnp.zeros_like(state)
            for h in range(H):
                D, qw, kw, _, _, _, _ = _ret_tables(d, lg_ref, h, C)
                tab_d[h] = D
                tab_q[h] = jnp.broadcast_to(qw, (C, HP))
                tab_k[h] = jnp.broadcast_to(kw, (C, HP))

        for h in range(H):
            sl = slice(HP * h, HP * (h + 1))
            D, qw, kw = tab_d[h], tab_q[h], tab_k[h]
            gc = jnp.exp(lg_ref[d, h] * C)
            qh, kh, vh = q_ref[:, sl], k_ref[:, sl], v_ref[:, sl]
            st = state[sl, :]
            sm = lax.dot_general(qh, kh, (((1,), (1,)), ((), ())), preferred_element_type=F32) * D
            inner = jnp.dot(sm.astype(BF16), vh, preferred_element_type=F32)
            cross = qw * jnp.dot(qh, st.astype(BF16), preferred_element_type=F32)
            o_ref[:, sl] = inner + cross
            st_ref[0, 0, sl, :] = st
            kvn = lax.dot_general((kh.astype(F32) * kw).astype(BF16), vh, (((0,), (0,)), ((), ())), preferred_element_type=F32)
            state[sl, :] = gc * st + kvn

    nb = 3 * _nbytes((C, 512), BF16) + _nbytes((C, 512), F32) + 2 * _nbytes((512, HP), F32) + 8 * _nbytes((C, C), F32)
    return pl.pallas_call(
        body, name="ret_fwd", grid=(2, N),
        in_specs=[pl.BlockSpec(memory_space=pltpu.SMEM)] + [pl.BlockSpec((C, 512), lambda d, n: (chunk(d, n), 0))] * 3,
        out_specs=[pl.BlockSpec((C, 512), lambda d, n: (chunk(d, n), d)),
                   pl.BlockSpec((1, 1, 512, HP), lambda d, n: (d, chunk(d, n), 0, 0))],
        out_shape=[jax.ShapeDtypeStruct((S, 1024), F32), jax.ShapeDtypeStruct((2, N, 512, HP), F32)],
        scratch_shapes=[pltpu.VMEM((512, HP), F32), pltpu.VMEM((H, C, C), F32), pltpu.VMEM((H, C, HP), F32),
                        pltpu.VMEM((H, C, HP), F32)],
        compiler_params=_params(("arbitrary", "arbitrary"), nb),
    )(lg, _hbm(q), _hbm(k), _hbm(v))


def _ret_bwd(lg, q, k, v, do, states):
    S = q.shape[0]
    C = min(RET_CHUNK, S)
    N = S // C
    H = RET_HEADS

    def chunk(d, n):
        return jnp.where(d == 0, N - 1 - n, n)

    def body(lg_ref, q_ref, k_ref, v_ref, do_ref, st_ref, dq_ref, dk_ref, dv_ref, dlg_ref, G, accA, accB, accC,
             tab_d, tab_q, tab_k, tab_qp, tab_kp):
        d, n = pl.program_id(0), pl.program_id(1)

        @pl.when(n == 0)
        def _():
            G[...] = jnp.zeros_like(G)
            accA[...] = jnp.zeros_like(accA)
            accB[...] = jnp.zeros_like(accB)
            accC[...] = jnp.zeros_like(accC)
            for h in range(H):
                D, qw, kw, _, _, qpos, kpos = _ret_tables(d, lg_ref, h, C)
                tab_d[h] = D
                tab_q[h] = jnp.broadcast_to(qw, (C, HP))
                tab_k[h] = jnp.broadcast_to(kw, (C, HP))
                tab_qp[h] = jnp.broadcast_to(qw * qpos, (C, HP))
                tab_kp[h] = jnp.broadcast_to(kw * kpos, (C, HP))

        nt = (((1,), (1,)), ((), ()))
        tn = (((0,), (0,)), ((), ()))
        for h in range(H):
            sl = slice(HP * h, HP * (h + 1))
            D, qw, kw = tab_d[h], tab_q[h], tab_k[h]
            gc = jnp.exp(lg_ref[d, h] * C)
            qh, kh, vh, doh = q_ref[:, sl], k_ref[:, sl], v_ref[:, sl], do_ref[:, sl]
            st = st_ref[0, 0, sl, :]
            g = G[sl, :]
            stb, gb = st.astype(BF16), g.astype(BF16)
            sraw = lax.dot_general(qh, kh, nt, preferred_element_type=F32)
            dS = lax.dot_general(doh, vh, nt, preferred_element_type=F32) * D
            dSb = dS.astype(BF16)
            smb = (sraw * D).astype(BF16)
            qs = jnp.dot(qh, stb, preferred_element_type=F32)
            kg = jnp.dot(kh, gb, preferred_element_type=F32)
            dqh = jnp.dot(dSb, kh, preferred_element_type=F32) + qw * lax.dot_general(doh, stb, nt, preferred_element_type=F32)
            dkh = lax.dot_general(dSb, qh, tn, preferred_element_type=F32) + kw * lax.dot_general(vh, gb, nt, preferred_element_type=F32)
            dvh = lax.dot_general(smb, doh, tn, preferred_element_type=F32) + kw * kg
            dq_ref[:, sl] = dqh.astype(dq_ref.dtype)
            dk_ref[:, sl] = dkh.astype(dk_ref.dtype)
            dv_ref[:, sl] = dvh.astype(dv_ref.dtype)
            dof, vf = doh.astype(F32), vh.astype(F32)
            accA[h] += sraw * dS
            accB[h] += tab_qp[h] * (qs * dof) + tab_kp[h] * (kg * vf)
            accC[h] += st * g
            G[sl, :] = gc * g + lax.dot_general((qh.astype(F32) * qw).astype(BF16), doh, tn, preferred_element_type=F32)

        @pl.when(n == N - 1)
        def _():
            rows = []
            for h in range(H):
                _, _, _, gc, dist, _, _ = _ret_tables(d, lg_ref, h, C)
                tot = jnp.sum(accA[h] * dist) + jnp.sum(accB[h]) + (C * gc) * jnp.sum(accC[h])
                rows.append(jnp.broadcast_to(tot, (1, HP)))
            dlg_ref[0] = jnp.concatenate(rows + [jnp.zeros((8 - H, HP), F32)], axis=0)

    nb = (4 * _nbytes((C, 512), BF16) + 3 * _nbytes((C, 512), F32) + 2 * _nbytes((512, HP), F32)
          + H * (_nbytes((C, C), F32) + _nbytes((C, HP), F32) + _nbytes((HP, HP), F32)) + 12 * _nbytes((C, C), F32))
    row = lambda d, n: (chunk(d, n), 0)
    out3 = lambda d, n: (chunk(d, n), d)
    return pl.pallas_call(
        body, name="ret_bwd", grid=(2, N),
        in_specs=[pl.BlockSpec(memory_space=pltpu.SMEM)] + [pl.BlockSpec((C, 512), row)] * 4
        + [pl.BlockSpec((1, 1, 512, HP), lambda d, n: (d, chunk(d, n), 0, 0))],
        out_specs=[pl.BlockSpec((C, 512), out3)] * 3 + [pl.BlockSpec((1, 8, HP), lambda d, n: (d, 0, 0))],
        out_shape=[jax.ShapeDtypeStruct((S, 1024), BF16)] * 3 + [jax.ShapeDtypeStruct((2, 8, HP), F32)],
        scratch_shapes=[pltpu.VMEM((512, HP), F32), pltpu.VMEM((H, C, C), F32), pltpu.VMEM((H, C, HP), F32),
                        pltpu.VMEM((H, HP, HP), F32), pltpu.VMEM((H, C, C), F32)] + [pltpu.VMEM((H, C, HP), F32)] * 4,
        compiler_params=_params(("arbitrary", "arbitrary"), nb),
    )(lg, _hbm(q), _hbm(k), _hbm(v), _hbm(do), _hbm(states))


CONV_PAD = 16
CONV_TR = 256
CONV_CB = LANES


def _fill_padded(pad, u_ref, S):
    pad[0:CONV_PAD, :] = jnp.zeros((CONV_PAD, CONV_CB), F32)
    pad[CONV_PAD + S:CONV_PAD + S + CONV_PAD, :] = jnp.zeros((CONV_PAD, CONV_CB), F32)
    pad[CONV_PAD:CONV_PAD + S, :] = u_ref[...]


def _conv_fwd(u, w32, b):
    S, D = u.shape
    tr = min(CONV_TR, S)

    def body(u_ref, w_ref, b_ref, o_ref, pad):
        _fill_padded(pad, u_ref, S)
        wv = w_ref[...]
        bv = b_ref[...]

        def step(t, carry):
            r0 = pl.multiple_of(t * tr, tr)
            acc = jnp.broadcast_to(bv, (tr, CONV_CB))
            for k in range(CONV_KERNEL):
                acc = acc + pad[pl.ds(r0 + (k + 1), tr), :] * wv[k:k + 1, :]
            o_ref[pl.ds(r0, tr), :] = acc
            return carry

        lax.fori_loop(0, S // tr, step, 0)

    nb = 2 * _nbytes((S, CONV_CB), F32) + _nbytes((S + 2 * CONV_PAD, CONV_CB), F32)
    return pl.pallas_call(
        body, name="conv_fwd", grid=(D // CONV_CB,),
        in_specs=[pl.BlockSpec((S, CONV_CB), lambda j: (0, j)), pl.BlockSpec((32, CONV_CB), lambda j: (0, j)),
                  pl.BlockSpec((1, CONV_CB), lambda j: (0, j))],
        out_specs=pl.BlockSpec((S, CONV_CB), lambda j: (0, j)),
        out_shape=jax.ShapeDtypeStruct((S, D), F32),
        scratch_shapes=[pltpu.VMEM((S + 2 * CONV_PAD, CONV_CB), F32)],
        compiler_params=_params(("parallel",), nb),
    )(_hbm(u), w32, b)


def _conv_bwd_w(u, dout):
    S, D = u.shape
    tr = min(CONV_TR, S)

    def body(u_ref, d_ref, o_ref, pad, acc):
        _fill_padded(pad, u_ref, S)
        acc[...] = jnp.zeros_like(acc)

        def fold(a):
            return jnp.sum(a.reshape(tr // 8, 8, CONV_CB), axis=0)

        def step(t, carry):
            r0 = pl.multiple_of(t * tr, tr)
            dv = d_ref[pl.ds(r0, tr), :]
            for k in range(CONV_KERNEL):
                acc[8 * k:8 * k + 8, :] += fold(pad[pl.ds(r0 + (k + 1), tr), :] * dv)
            acc[8 * CONV_KERNEL:8 * CONV_KERNEL + 8, :] += fold(dv)
            return carry

        lax.fori_loop(0, S // tr, step, 0)
        o_ref[...] = jnp.sum(acc[...].reshape(32, 8, CONV_CB), axis=1)

    nb = 2 * _nbytes((S, CONV_CB), F32) + _nbytes((S + 2 * CONV_PAD, CONV_CB), F32)
    return pl.pallas_call(
        body, name="conv_bwd_w", grid=(D // CONV_CB,),
        in_specs=[pl.BlockSpec((S, CONV_CB), lambda j: (0, j)), pl.BlockSpec((S, CONV_CB), lambda j: (0, j))],
        out_specs=pl.BlockSpec((32, CONV_CB), lambda j: (0, j)),
        out_shape=jax.ShapeDtypeStruct((32, D), F32),
        scratch_shapes=[pltpu.VMEM((S + 2 * CONV_PAD, CONV_CB), F32), pltpu.VMEM((256, CONV_CB), F32)],
        compiler_params=_params(("parallel",), nb),
    )(_hbm(u), _hbm(dout))


def _adamw(w, g, m, v):
    shape = w.shape
    w2, g2, m2, v2 = [a.reshape(-1, shape[-1]) for a in (w, g, m, v)]
    W = shape[-1]

    def fn(rows, p):
        w_, g_, m_, v_ = [r[...] for r in rows]
        mn = ADAM_B1 * m_ + (1.0 - ADAM_B1) * g_
        vn = ADAM_B2 * v_ + (1.0 - ADAM_B2) * jnp.square(g_)
        m_hat = mn / (1.0 - ADAM_B1 ** ADAM_STEP)
        v_hat = vn / (1.0 - ADAM_B2 ** ADAM_STEP)
        delta = -ADAM_LR * (m_hat / (jnp.sqrt(v_hat) + ADAM_EPS) + ADAM_WD * w_)
        return [delta, mn, vn], []

    R2 = w2.shape[0]
    ts = ROW_TILE if (R2 > ROW_TILE and R2 % ROW_TILE == 0) else R2
    d, mn, vn = _rowcall("adamw", fn, [w2, g2, m2, v2], [], [(W, F32)] * 3, ts=ts)
    return d.reshape(shape), mn.reshape(shape), vn.reshape(shape)


_HBM = pl.BlockSpec(memory_space=pltpu.HBM)


def _place():
    return lax.axis_index("x"), lax.axis_index("y"), lax.axis_index("c")


def _half_rows(ref, half, h):
    idx = (slice(None),) * (len(ref.shape) - 2) + (pl.ds(half * h, h), slice(None))
    return ref.at[idx]


def _chip_exchange(name, p, mode):
    assert mode in ("half", "whole", "scatter")
    if mode == "half":
        h, W = p.shape[0] // 2, p.shape[1]
    elif mode == "whole":
        h, W = p.shape
    else:
        h, W = p.shape[1], p.shape[2]

    def body(p_ref, o_ref, send_sems, recv_sems, local_sem):
        copies = _chip_copies(p_ref, o_ref, send_sems, recv_sems, local_sem, mode, h)
        for cp in copies:
            cp.start()
        for cp in copies:
            cp.wait()

    return pl.pallas_call(
        body, name=name, in_specs=[_HBM], out_specs=_HBM,
        out_shape=jax.ShapeDtypeStruct((4, h, W), p.dtype), scratch_shapes=_CHIP_SEMS,
    )(p)


_CHIP_SEMS = [pltpu.SemaphoreType.DMA((3,)), pltpu.SemaphoreType.DMA((3,)), pltpu.SemaphoreType.DMA]


def _chip_copies(p_ref, o_ref, send_sems, recv_sems, local_sem, mode, h):
    x, y, c = _place()
    k_me = 2 * x + y

    def src(k):
        if mode == "half":
            return _half_rows(p_ref, c, h)
        return p_ref if mode == "whole" else p_ref.at[k]

    copies = [pltpu.make_async_copy(src(k_me), o_ref.at[k_me], local_sem)]
    for j, (cx, cy) in enumerate([(1 - x, y), (x, 1 - y), (1 - x, 1 - y)]):
        copies.append(pltpu.make_async_remote_copy(
            src_ref=src(2 * cx + cy), dst_ref=o_ref.at[k_me], send_sem=send_sems.at[j], recv_sem=recv_sems.at[j],
            device_id=(cx, cy, c), device_id_type=MESH))
    return copies


def _sibling_send_half(name, g):
    n, R, W = g.shape
    h = R // 2

    def body(g_ref, o_ref, send_sem, recv_sem):
        x, y, c = _place()
        cp = pltpu.make_async_remote_copy(
            src_ref=_half_rows(g_ref, 1 - c, h), dst_ref=o_ref, send_sem=send_sem, recv_sem=recv_sem,
            device_id=(x, y, 1 - c), device_id_type=MESH)
        cp.start()
        cp.wait()

    return pl.pallas_call(
        body, name=name, in_specs=[_HBM], out_specs=_HBM,
        out_shape=jax.ShapeDtypeStruct((n, h, W), g.dtype),
        scratch_shapes=[pltpu.SemaphoreType.DMA, pltpu.SemaphoreType.DMA],
    )(g)


def _halves_merge(name, mine):
    def body(m_ref, o_ref, send_sem, recv_sem):
        x, y, c = _place()
        cp = pltpu.make_async_remote_copy(
            src_ref=m_ref, dst_ref=o_ref, send_sem=send_sem, recv_sem=recv_sem,
            device_id=(x, y, 1 - c), device_id_type=MESH)
        cp.start()
        cp.wait()

    theirs = pl.pallas_call(
        body, name=name, in_specs=[_HBM], out_specs=_HBM,
        out_shape=jax.ShapeDtypeStruct(mine.shape, mine.dtype),
        scratch_shapes=[pltpu.SemaphoreType.DMA, pltpu.SemaphoreType.DMA],
    )(mine)
    south = lax.axis_index("c") == 0
    axis = mine.ndim - 2
    return jnp.concatenate([jnp.where(south, mine, theirs), jnp.where(south, theirs, mine)], axis=axis)


def _add2(name, a, b, out_dtype):
    def fn(rows, p):
        return [rows[0][...].astype(F32) + rows[1][...].astype(F32)], []

    return _rowcall(name, fn, [a, b], [], [(a.shape[1], out_dtype)], ts=4 * PACK_TILE)[0]


def _sum4(name, b):
    _, h, W = b.shape
    ts = 2 * PACK_TILE if h % (2 * PACK_TILE) == 0 else (PACK_TILE if h % PACK_TILE == 0 else h)

    def body(b_ref, o_ref):
        o_ref[...] = ((b_ref[0].astype(F32) + b_ref[1].astype(F32)) + b_ref[2].astype(F32)) + b_ref[3].astype(F32)

    nb = _nbytes((4, ts, W), b.dtype) + _nbytes((ts, W), F32)
    return pl.pallas_call(
        body, name=name, grid=(h // ts,),
        in_specs=[pl.BlockSpec((4, ts, W), lambda i: (0, i, 0))], out_specs=pl.BlockSpec((ts, W), lambda i: (i, 0)),
        out_shape=jax.ShapeDtypeStruct((h, W), F32), compiler_params=_params(("parallel",), nb),
    )(b)


PACK_W = 1024
PACK_ROWS = 16
PACK_TILE = 128
EV_SHARDED = (("ev_w_in", "col", False), ("ev_w_uq", "col", False), ("ev_w_ukv", "col", False), ("ev_w_out", "row", False))
OD_SHARDED = (("od_w_in", "col", False), ("od_b_in", "col", True), ("od_dw_w", "col", True), ("od_dw_b", "col", True),
              ("od_ln_g", "col", True), ("od_ln_b", "col", True), ("od_w_out", "row", False))
REPLICATED =("ada_b", "pre_g", "post_g", "ev_dec_f", "ev_dec_b", "ev_q_norm_g", "ev_kv_norm_g")
WEIGHTS = ("ada_w", "ada_b", "pre_g", "post_g", "ev_w_in", "ev_dec_f", "ev_dec_b", "ev_q_norm_g", "ev_w_uq", "ev_kv_norm_g",
           "ev_w_ukv", "ev_w_out", "od_w_in", "od_b_in", "od_dw_w", "od_dw_b", "od_ln_g", "od_ln_b", "od_w_out")


def _layer_entries(layers):
    out = []
    for l in layers:
        out += [(name, how, exact, l // 2) for name, how, exact in (EV_SHARDED if l % 2 == 0 else OD_SHARDED)]
    return out


def _rows_of(n):
    unit = PACK_W * PACK_ROWS
    return (n + unit - 1) // unit * PACK_ROWS


def _to_rows(flat, lead):
    n = flat.shape[-1]
    r = _rows_of(n)
    flat = jnp.pad(flat, [(0, 0)] * len(lead) + [(0, r * PACK_W - n)])
    return flat.reshape(*lead, r, PACK_W)


def _pad_total(buf, axis):
    r = buf.shape[axis]
    r2 = (r + 2 * PACK_TILE - 1) // (2 * PACK_TILE) * (2 * PACK_TILE)
    pads = [(0, 0)] * buf.ndim
    pads[axis] = (0, r2 - r)
    return jnp.pad(buf, pads)


def _split_chips(full, how):
    if how == "col":
        n = full.shape[-1] // 4
        return jnp.moveaxis(full.reshape(*full.shape[:-1], 4, n), -2, 0)
    n = full.shape[-2] // 4
    return jnp.moveaxis(full.reshape(*full.shape[:-2], 4, n, full.shape[-1]), -3, 0)


def _join_chips(blocks, how):
    if how == "col":
        t = jnp.moveaxis(blocks, 0, -2)
        return t.reshape(*t.shape[:-2], t.shape[-2] * t.shape[-1])
    t = jnp.moveaxis(blocks, 0, -3)
    return t.reshape(*t.shape[:-3], t.shape[-3] * t.shape[-2], t.shape[-1])


def _pack_weights(w, entries):
    parts = []
    for name, _, exact, idx in entries:
        a = w[name][idx].reshape(-1)
        a = lax.bitcast_convert_type(a, BF16).reshape(-1) if exact else a.astype(BF16)
        parts.append(_to_rows(a, ()))
    return _pad_total(jnp.concatenate(parts, axis=0), 0)


def _unpack_weights(gathered, w, entries):
    out, r0 = {}, 0
    for name, how, exact, idx in entries:
        shp = w[name].shape[1:]
        n = (2 if exact else 1) * w[name][idx].size
        r = _rows_of(n)
        a = gathered[:, r0:r0 + r].reshape(4, -1)[:, :n]
        if exact:
            a = lax.bitcast_convert_type(a.reshape(4, -1, 2), F32)
        out[(name, idx)] = _join_chips(a.reshape(4, *shp), how)
        r0 += r
    return out


def _pack_grads(grads, entries, replicated):
    parts = []
    for name, how, _, idx in entries:
        b = _split_chips(grads[(name, idx)], how)
        parts.append(_to_rows(b.reshape(4, -1).astype(BF16), (4,)))
    for name in replicated:
        a = _to_rows(grads[name].reshape(-1).astype(BF16), ())
        parts.append(jnp.broadcast_to(a[None], (4, *a.shape)))
    return _pad_total(jnp.concatenate(parts, axis=1), 1)


def _reduce_begin(tag, g):
    h = g.shape[1] // 2
    c = lax.axis_index("c")
    theirs = _sibling_send_half("reduce_cores_" + tag, g)
    mine = lax.dynamic_slice_in_dim(g, c * h, h, axis=1)
    return _add2("reduce_add_" + tag, mine.reshape(4 * h, PACK_W), theirs.reshape(4 * h, PACK_W), BF16).reshape(4, h, PACK_W)


def _reduce_end(tag, exchanged):
    return _halves_merge("reduce_merge_" + tag, _sum4("reduce_sum_" + tag, exchanged))


def _unpack_grads(total, w, entries, replicated):
    out, r0 = {}, 0
    for name, _, _, idx in entries:
        n = w[name][idx].size
        r = _rows_of(n)
        out[(name, idx)] = total[r0:r0 + r].reshape(-1)[:n].reshape(w[name].shape[1:])
        r0 += r
    for name in replicated:
        n = w[name].size
        r = _rows_of(n)
        out[name] = total[r0:r0 + r].reshape(-1)[:n].reshape(w[name].shape)
        r0 += r
    return out


def _pad_heads(a, n, w, to):
    lead = a.shape[:-1]
    return jnp.pad(a.reshape(*lead, n, w), [(0, 0)] * (len(lead) + 1) + [(0, to - w)]).reshape(*lead, n * to)


def _unpad_heads(a, n, w, to):
    lead = a.shape[:-1]
    return a.reshape(*lead, n, to)[..., :w].reshape(*lead, n * w)


def _w_in_pad(w):
    return jnp.concatenate([_pad_heads(w[:, 0:256], 4, 64, HP), _pad_heads(w[:, 256:512], 4, 64, HP), w[:, 512:1536],
                            w[:, 2240:2752], w[:, 1536:2176], _pad_heads(w[:, 2176:2240], 1, 64, HP)], axis=1)


def _w_in_unpad(g):
    return jnp.concatenate([_unpad_heads(g[:, Z_RQ:Z_RK], 4, 64, HP), _unpad_heads(g[:, Z_RK:Z_RV], 4, 64, HP),
                            g[:, Z_RV:Z_MG], g[:, Z_CQ:Z_KR], g[:, Z_KR:Z_KR + 64], g[:, Z_MG:Z_CQ]], axis=1)


def _layer_weights(l, full, w):
    f32 = lambda a: a.astype(F32)
    i = l // 2
    common = dict(bias=f32(w["ada_b"][l])[None], pre_g=f32(w["pre_g"][l])[None], post_g=f32(w["post_g"][l])[None])
    if l % 2 == 0:
        w_uq_p = _pad_heads(full[("ev_w_uq", i)], MLA_HEADS, MLA_NOPE_DIM + MLA_ROPE_DIM, QW)
        lg = jnp.stack([jax.nn.log_sigmoid(f32(w["ev_dec_f"][i])), jax.nn.log_sigmoid(f32(w["ev_dec_b"][i]))])
        return dict(common, w_in=_w_in_pad(full[("ev_w_in", i)]), w_uq=w_uq_p, w_ukv=full[("ev_w_ukv", i)],
                    w_out=full[("ev_w_out", i)], qg=f32(w["ev_q_norm_g"][i])[None], kvg=f32(w["ev_kv_norm_g"][i])[None], lg=lg)
    dw = f32(full[("od_dw_w", i)])
    return dict(common, w_in=full[("od_w_in", i)], w_out=full[("od_w_out", i)], b_in=f32(full[("od_b_in", i)])[None],
                dw=jnp.pad(dw, ((0, 1), (0, 0))), dw_flip=jnp.pad(dw[::-1], ((0, 1), (0, 0))),
                dw_b=f32(full[("od_dw_b", i)])[None], ln_g=f32(full[("od_ln_g", i)])[None], ln_b=f32(full[("od_ln_b", i)])[None])


def _rope_tables(positions):
    inv_freq = ROPE_BASE ** (-jnp.arange(0, ROPE_DIM, 2, dtype=F32) / ROPE_DIM)
    ang = positions.astype(F32)[:, None] * inv_freq
    z = jnp.zeros((ang.shape[0], HP - ROPE_DIM), F32)
    return jnp.concatenate([jnp.cos(ang), jnp.cos(ang), z], axis=1), jnp.concatenate([jnp.sin(ang), jnp.sin(ang), z], axis=1)


def _even_fwd(x, mod, cos, sin, w, side=None):
    bias, pre_g, post_g = w["bias"], w["pre_g"], w["post_g"]
    h, z = _pre_mm("ev_in", x, pre_g, mod, bias, w["w_in"])
    rq, rk, rv, qn, kvn, Q, K, V = _ev_mid_fwd(z, cos, sin, w["qg"], w["kvg"], w["w_uq"], w["w_ukv"])
    a, lse, *got = _flash_fwd(Q, K, V, side)
    o2, states = _ret_fwd(w["lg"], rq, rk, rv)
    mix, y, x_new = _ev_out_fwd(o2, z, a, x, w["w_out"], post_g, mod, bias)
    saved = dict(x=x, h=h, z=z, rq=rq, rk=rk, rv=rv, qn=qn, kvn=kvn, Q=Q, K=K, V=V, a=a, lse=lse, o2=o2,
                 states=states, mix=mix, y=y)
    return x_new, saved, (got[0] if got else None)


def _even_bwd(dx, s, mod, cos, sin, w, side=None):
    bias, pre_g, post_g = w["bias"], w["pre_g"], w["post_g"]
    dy, do, drg, da, dmg, dpost_g, dgate = _ev_out_bwd(s["y"], dx, s["o2"], s["z"], s["a"], w["w_out"], post_g, mod, bias)
    dw_out = _mm_tn("ev_out_dw", s["mix"], dy, BF16)
    drq, drk, drv, dlg = _ret_bwd(w["lg"], s["rq"], s["rk"], s["rv"], do, s["states"])
    dQ, dK, dV, *got = _flash_bwd(s["Q"], s["K"], s["V"], da, s["a"], s["lse"], side)
    dqf, dkv, dqn, dkvn, dkr = _ev_mid_bwd(dQ, dK, dV, cos, sin, w["w_uq"], w["w_ukv"])
    dw_uq = _mm_tn("ev_uq_dw", s["qn"], dqf, BF16)
    dw_ukv = _mm_tn("ev_ukv_dw", s["kvn"], dkv, BF16)
    dz, dx_new, dqg, dkvg, dpre_g, dscale, dshift = _ev_in_bwd(
        s["z"], cos, sin, w["qg"], w["kvg"], drq, drk, drv, drg, dqn, dkvn, dkr, dmg, s["x"], dx, w["w_in"], pre_g, mod, bias)
    dw_in = _mm_tn("ev_in_dw", s["h"], dz, BF16)
    g = dict(ev_w_in=_w_in_unpad(dw_in), ev_w_uq=_unpad_heads(dw_uq, MLA_HEADS, MLA_NOPE_DIM + MLA_ROPE_DIM, QW),
             ev_w_ukv=dw_ukv, ev_w_out=dw_out, ev_q_norm_g=dqg[0], ev_kv_norm_g=dkvg[0],
             dlg_f=dlg[0, :RET_HEADS, 0], dlg_b=dlg[1, :RET_HEADS, 0])
    return dx_new, g, dpre_g[0], dpost_g[0], jnp.concatenate([dshift, dscale, dgate], axis=1), (got[0] if got else None)


def _odd_fwd(x, mod, w, tgt=None):
    bias, pre_g, post_g = w["bias"], w["pre_g"], w["post_g"]
    h, z, u = _od_in_fwd(x, pre_g, mod, bias, w["w_in"], w["b_in"])
    uc = _conv_fwd(u, w["dw"], w["dw_b"])
    vv, y, *rest = _od_out_fwd(uc, z, x, w["w_out"], w["b_in"], w["ln_g"], w["ln_b"], post_g, mod, bias, tgt)
    return (rest[0] if tgt is None else tuple(rest)), dict(x=x, h=h, z=z, u=u, uc=uc, vv=vv, y=y)


def _odd_bwd(dx, s, mod, w):
    bias, pre_g, post_g = w["bias"], w["pre_g"], w["post_g"]
    dy, duc, dg, dpost_g, dgate, dbg, dln_g, dln_b = _od_out_bwd(
        s["y"], dx, s["uc"], s["z"], w["w_out"], w["b_in"], w["ln_g"], w["ln_b"], post_g, mod, bias)
    dw_out = _mm_tn("od_out_dw", s["vv"], dy, BF16)
    du = _conv_fwd(duc, w["dw_flip"], jnp.zeros_like(w["dw_b"]))
    dwb = _conv_bwd_w(s["u"], duc)
    dz, dx_new, dba, dbb, dpre_g, dscale, dshift = _od_in_bwd(s["z"], w["b_in"], du, dg, s["x"], dx, w["w_in"], pre_g, mod, bias)
    dw_in = _mm_tn("od_in_dw", s["h"], dz, BF16)
    g = dict(od_w_in=dw_in, od_b_in=jnp.concatenate([dba, dbb, dbg], axis=1)[0], od_dw_w=dwb[:CONV_KERNEL], od_dw_b=dwb[CONV_KERNEL],
             od_ln_g=dln_g[0], od_ln_b=dln_b[0], od_w_out=dw_out)
    return dx_new, g, dpre_g[0], dpost_g[0], jnp.concatenate([dshift, dscale, dgate], axis=1)


LATE_LAYERS = tuple(range(1, DEPTH))


def _step(x, mod, positions, loss_target, w):
    cos, sin = _rope_tables(positions)
    f32 = lambda a: a.astype(F32)
    mods = [mod[l:l + 1] for l in range(DEPTH)]
    first, late = _layer_entries([0]), _layer_entries(LATE_LAYERS)

    gathered = _halves_merge("gather_merge_first", _chip_exchange("gather_chips_first", _pack_weights(w, first), "half"))
    full = _unpack_weights(gathered, w, first)
    lw = {0: _layer_weights(0, full, w)}
    x, s0, got = _even_fwd(x, mods[0], cos, sin, lw[0], side=(_pack_weights(w, late), "half"))
    full.update(_unpack_weights(_halves_merge("gather_merge_late", got), w, late))
    saved = [s0]
    assert DEPTH % 2 == 0
    for l in LATE_LAYERS:
        lw[l] = _layer_weights(l, full, w)
        if l % 2 == 0:
            x, s, _ = _even_fwd(x, mods[l], cos, sin, lw[l])
        else:
            x, s = _odd_fwd(x, mods[l], lw[l], loss_target if l == DEPTH - 1 else None)
        saved.append(s)
    dx, loss = x

    grads = {}
    dpre, dpost, dmod = [None] * DEPTH, [None] * DEPTH, [None] * DEPTH
    for l in reversed(LATE_LAYERS):
        if l % 2 == 0:
            dx, g, dpre[l], dpost[l], dmod[l], _ = _even_bwd(dx, saved[l], mods[l], cos, sin, lw[l])
        else:
            dx, g, dpre[l], dpost[l], dmod[l] = _odd_bwd(dx, saved[l], mods[l], lw[l])
        grads.update({(name, l // 2): val for name, val in g.items()})
    chip_sum = _reduce_begin("late", _pack_grads(grads, late, ()))
    dx, g, dpre[0], dpost[0], dmod[0], got = _even_bwd(dx, saved[0], mods[0], cos, sin, lw[0], side=(chip_sum, "scatter"))
    grads.update({(name, 0): val for name, val in g.items()})
    red = _unpack_grads(_reduce_end("late", got), w, late, ())

    n_ev = (DEPTH + 1) // 2
    grads.update(pre_g=jnp.stack(dpre), post_g=jnp.stack(dpost), ada_b=jnp.concatenate(dmod, axis=0))
    for name in ("ev_q_norm_g", "ev_kv_norm_g"):
        grads[name] = jnp.stack([grads[(name, i)] for i in range(n_ev)])
    grads["ev_dec_f"] = jnp.stack([grads[("dlg_f", i)] for i in range(n_ev)]) * jax.nn.sigmoid(-f32(w["ev_dec_f"]))
    grads["ev_dec_b"] = jnp.stack([grads[("dlg_b", i)] for i in range(n_ev)]) * jax.nn.sigmoid(-f32(w["ev_dec_b"]))
    chip_sum = _reduce_begin("first", _pack_grads(grads, first, REPLICATED))
    total = _reduce_end("first", _chip_exchange("reduce_chips_first", chip_sum, "scatter"))
    red.update(_unpack_grads(total, w, first, REPLICATED))

    out = {name: red[name] for name in REPLICATED}
    for name, _, _ in EV_SHARDED:
        out[name] = jnp.stack([red[(name, i)] for i in range(n_ev)])
    for name, _, _ in OD_SHARDED:
        out[name] = jnp.stack([red[(name, i)] for i in range(DEPTH // 2)])
    return loss[0, 0], dx, out, grads["ada_b"]


ADA_ROWS = 16
ADA_COLS = 3 * D_MODEL // 4
ADA_PACK = (8, 1024)


def _ada_pack(a):
    n = ADA_PACK[0] * ADA_PACK[1]
    return jnp.pad(a, ((0, 0), (0, n - a.shape[1]))).reshape(4, *ADA_PACK)


def _ada_unpack(a):
    return a.reshape(4, -1)[:, :DEPTH * ADA_COLS]


def _ada_forward(c, ada_w):
    D = D_MODEL
    c16 = jnp.zeros((ADA_ROWS, D), F32).at[0].set(c[0])
    c_act = _rowcall("silu_c", lambda rows, p: ([_silu(rows[0][...])], []), [c16], [], [(D, BF16)])[0]
    got = _chip_exchange("ada_c", c_act, "whole")
    c4 = jnp.pad(got[:, 0, :], ((0, ADA_ROWS - 4), (0, 0)))
    per_layer = [_mm_nn("ada", c4, ada_w[l].astype(BF16))[:4] for l in range(DEPTH)]
    p = jnp.stack(per_layer, axis=1).reshape(4, DEPTH * ADA_COLS)
    got = _chip_exchange("ada_mod", _ada_pack(p), "scatter")
    mod = _ada_unpack(got).reshape(4, DEPTH, ADA_COLS).transpose(1, 0, 2).reshape(DEPTH, 3 * D)
    return mod, c4


def _ada_backward(dmod, c4):
    p = dmod.reshape(DEPTH, 4, ADA_COLS).transpose(1, 0, 2).reshape(4, DEPTH * ADA_COLS)
    got = _chip_exchange("ada_dmod", _ada_pack(p), "scatter")
    dm4 = jnp.pad(_ada_unpack(got), ((0, ADA_ROWS - 4), (0, 0)))
    both = _halves_merge("ada_merge", jnp.concatenate([dm4, c4.astype(F32)], axis=1))
    dm32, c32 = both[:, :DEPTH * ADA_COLS].astype(BF16), both[:, DEPTH * ADA_COLS:].astype(BF16)
    return jnp.stack([_mm_tn("ada_dw", c32, dm32[:, l * ADA_COLS:(l + 1) * ADA_COLS]) for l in range(DEPTH)])


def kernel(x, c, positions, ada_w, ada_b, pre_g, post_g, ev_w_in, ev_dec_f, ev_dec_b, ev_q_norm_g, ev_w_uq, ev_kv_norm_g, ev_w_ukv, ev_w_out, od_w_in, od_b_in, od_dw_w, od_dw_b, od_ln_g, od_ln_b, od_w_out, loss_target, m_ada_w, m_ada_b, m_pre_g, m_post_g, m_ev_w_in, m_ev_dec_f, m_ev_dec_b, m_ev_q_norm_g, m_ev_w_uq, m_ev_kv_norm_g, m_ev_w_ukv, m_ev_w_out, m_od_w_in, m_od_b_in, m_od_dw_w, m_od_dw_b, m_od_ln_g, m_od_ln_b, m_od_w_out, v_ada_w, v_ada_b, v_pre_g, v_post_g, v_ev_w_in, v_ev_dec_f, v_ev_dec_b, v_ev_q_norm_g, v_ev_w_uq, v_ev_kv_norm_g, v_ev_w_ukv, v_ev_w_out, v_od_w_in, v_od_b_in, v_od_dw_w, v_od_dw_b, v_od_ln_g, v_od_ln_b, v_od_w_out):
    w = dict(ada_w=ada_w, ada_b=ada_b, pre_g=pre_g, post_g=post_g, ev_w_in=ev_w_in, ev_dec_f=ev_dec_f, ev_dec_b=ev_dec_b,
             ev_q_norm_g=ev_q_norm_g, ev_w_uq=ev_w_uq, ev_kv_norm_g=ev_kv_norm_g, ev_w_ukv=ev_w_ukv, ev_w_out=ev_w_out,
             od_w_in=od_w_in, od_b_in=od_b_in, od_dw_w=od_dw_w, od_dw_b=od_dw_b, od_ln_g=od_ln_g, od_ln_b=od_ln_b, od_w_out=od_w_out)
    m = dict(ada_w=m_ada_w, ada_b=m_ada_b, pre_g=m_pre_g, post_g=m_post_g, ev_w_in=m_ev_w_in, ev_dec_f=m_ev_dec_f, ev_dec_b=m_ev_dec_b,
             ev_q_norm_g=m_ev_q_norm_g, ev_w_uq=m_ev_w_uq, ev_kv_norm_g=m_ev_kv_norm_g, ev_w_ukv=m_ev_w_ukv, ev_w_out=m_ev_w_out,
             od_w_in=m_od_w_in, od_b_in=m_od_b_in, od_dw_w=m_od_dw_w, od_dw_b=m_od_dw_b, od_ln_g=m_od_ln_g, od_ln_b=m_od_ln_b, od_w_out=m_od_w_out)
    v = dict(ada_w=v_ada_w, ada_b=v_ada_b, pre_g=v_pre_g, post_g=v_post_g, ev_w_in=v_ev_w_in, ev_dec_f=v_ev_dec_f, ev_dec_b=v_ev_dec_b,
             ev_q_norm_g=v_ev_q_norm_g, ev_w_uq=v_ev_w_uq, ev_kv_norm_g=v_ev_kv_norm_g, ev_w_ukv=v_ev_w_ukv, ev_w_out=v_ev_w_out,
             od_w_in=v_od_w_in, od_b_in=v_od_b_in, od_dw_w=v_od_dw_w, od_dw_b=v_od_dw_b, od_ln_g=v_od_ln_g, od_ln_b=v_od_ln_b, od_w_out=v_od_w_out)

    mod, c4 = _ada_forward(c, ada_w)
    loss_local, grad_x, g, dmod = _step(x[0], mod, positions[0], loss_target[0], w)
    loss = lax.psum(loss_local, ("x", "y", "c"))
    g["ada_w"] = _ada_backward(dmod, c4)
    delta, new_m, new_v = {}, {}, {}
    for name in WEIGHTS:
        delta[name], new_m[name], new_v[name] = _adamw(w[name], g[name], m[name], v[name])
    return (loss, grad_x[None], *[g[n] for n in WEIGHTS], *[delta[n] for n in WEIGHTS],
            *[new_m[n] for n in WEIGHTS], *[new_v[n] for n in WEIGHTS])
```

```python
import functools

import jax
import jax.numpy as jnp
from jax import lax
from jax.experimental import pallas as pl
from jax.experimental.pallas import tpu as pltpu

F32 = jnp.float32
BF16 = jnp.bfloat16

D_MODEL = 1024
DEPTH = 4
RET_HEADS = 4
RET_QK_DIM = 64
RET_V_DIM = 128
MLA_HEADS = 4
MLA_Q_RANK = 384
MLA_KV_RANK = 256
MLA_NOPE_DIM = 128
MLA_ROPE_DIM = 64
MLA_V_DIM = 128
ROPE_DIM = 64
ROPE_BASE = 10000.0
CONV_KERNEL = 31
EPS = 1e-6
ADAM_LR, ADAM_B1, ADAM_B2, ADAM_EPS, ADAM_WD, ADAM_STEP = 0.001, 0.9, 0.999, 1e-08, 0.01, 10

LANES = 128
HP = 128
Z_RQ, Z_RK, Z_RV, Z_RG, Z_MG, Z_CQ, Z_CKV, Z_KR, Z_END = 0, 512, 1024, 1536, 2048, 2560, 2944, 3200, 3328
Z_LAT = Z_END - Z_CQ


def _z_ret(z):
    return (z, Z_RG, 0)


def _z_gates(z):
    return [(z, 512, Z_RG // 512), (z, 512, Z_MG // 512)]


def _z_latents(z):
    return [(z, 256, Z_CQ // 256 + i) for i in range(Z_LAT // 256)]
V7X_VMEM_BYTES = 64 * 1024 * 1024
VMEM_CAP = V7X_VMEM_BYTES - 8 * 1024 * 1024

MESH = pl.DeviceIdType.MESH


def _nbytes(shape, dtype):
    n = 1
    for s in shape:
        n *= s
    return n * jnp.dtype(dtype).itemsize


def _hbm(x):
    return pltpu.with_memory_space_constraint(x, pltpu.HBM)


def _params(sem, block_bytes):
    limit = min(VMEM_CAP, max(32 * 1024 * 1024, 2 * block_bytes + 16 * 1024 * 1024))
    return pltpu.CompilerParams(dimension_semantics=sem, vmem_limit_bytes=limit)


ROW_TILE = 512


def _rowcall(name, fn, rows, params, outs, accs=(), ts=ROW_TILE, mats=()):
    rows = [r if isinstance(r, tuple) else (r, r.shape[1], 0) for r in rows]
    S = rows[0][0].shape[0]
    ts = min(ts, S)
    assert S % ts == 0, (name, S, ts)
    nr, npar, nm, no, na = len(rows), len(params), len(mats), len(outs), len(accs)

    def body(*refs):
        row_refs = refs[:nr]
        pvals = [r[...] for r in refs[nr:nr + npar]]
        mat_refs = refs[nr + npar:nr + npar + nm]
        out_refs = refs[nr + npar + nm:nr + npar + nm + no]
        acc_refs = refs[nr + npar + nm + no:]
        ovals, avals = fn(row_refs, pvals, mat_refs) if nm else fn(row_refs, pvals)
        for r, v in zip(out_refs, ovals, strict=True):
            r[...] = v.astype(r.dtype)
        if na:
            @pl.when(pl.program_id(0) == 0)
            def _():
                for r in acc_refs:
                    r[...] = jnp.zeros_like(r)
            for r, v in zip(acc_refs, avals, strict=True):
                r[...] += v

    in_specs = [pl.BlockSpec((ts, w), functools.partial(lambda i, b: (i, b), b=blk)) for _, w, blk in rows]
    in_specs += [pl.BlockSpec(p.shape, lambda i: (0, 0)) for p in [*params, *mats]]
    out_specs = [pl.BlockSpec((ts, w), lambda i: (i, 0)) for w, _ in outs]
    out_specs += [pl.BlockSpec(s, lambda i: (0, 0)) for s in accs]
    out_shape = [jax.ShapeDtypeStruct((S, w), dt) for w, dt in outs]
    out_shape += [jax.ShapeDtypeStruct(s, F32) for s in accs]
    nb = sum(_nbytes((ts, w), a.dtype) for a, w, _ in rows) + sum(_nbytes((ts, w), dt) for w, dt in outs)
    nb += sum(_nbytes(p.shape, p.dtype) for p in params) + sum(_nbytes(s, F32) for s in accs)
    res = pl.pallas_call(
        body, name=name, grid=(S // ts,), in_specs=in_specs, out_specs=out_specs, out_shape=out_shape,
        compiler_params=_params(("arbitrary",) if na else ("parallel",), 3 * nb + sum(_nbytes(m.shape, m.dtype) for m in mats)),
    )(*[_hbm(a) for a, _, _ in rows], *params, *[_hbm(m) for m in mats])
    return res


def _silu(x):
    return x * jax.nn.sigmoid(x)


def _rms(x, g):
    return x * lax.rsqrt(jnp.mean(x * x, axis=-1, keepdims=True) + EPS) * g


def _rot(x):
    lane = lax.broadcasted_iota(jnp.int32, x.shape, 1)
    return jnp.where(lane < ROPE_DIM // 2, -pltpu.roll(x, LANES - ROPE_DIM // 2, 1), pltpu.roll(x, ROPE_DIM // 2, 1))


def _rope(x, cos, sin):
    return x * cos + _rot(x) * sin


def _rope_t(dy, cos, sin):
    return dy * cos - _rot(dy) * sin


def _groups(ref, start, n):
    return [ref[:, start + HP * h:start + HP * (h + 1)] for h in range(n)]


def _pre_math(x, g, m_scale, b_scale, m_shift, b_shift):
    return _rms(x, g) * (1.0 + (m_scale + b_scale)) + (m_shift + b_shift)


def _post_math(x, y, g, m_gate, b_gate):
    return x + (m_gate + b_gate) * _rms(y, g)


def _ev_mid_fwd(z, cos, sin, qg, kvg, w_uq, w_ukv):
    qscale = ATT_SCALE * LOG2E

    def fn(rows, p, mats):
        z_, l0, l1, l2, c_, s_ = rows
        qg_, kvg_ = p
        cos_, sin_ = c_[...], s_[...]
        lat = jnp.concatenate([l0[...], l1[...], l2[...]], axis=1)
        rq = jnp.concatenate([_rope(v[...], cos_, sin_) for v in _groups(z_, Z_RQ, RET_HEADS)], axis=1)
        rk = jnp.concatenate([_rope(v[...], cos_, sin_) for v in _groups(z_, Z_RK, RET_HEADS)], axis=1)
        rk = rk * (RET_QK_DIM ** -0.5)
        rv = z_[:, Z_RV:Z_RG]
        qn = _rms(lat[:, :Z_CKV - Z_CQ], qg_).astype(BF16)
        kvn = _rms(lat[:, Z_CKV - Z_CQ:Z_KR - Z_CQ], kvg_).astype(BF16)
        kr = _rope(lat[:, Z_KR - Z_CQ:], cos_, sin_)
        qf = jnp.dot(qn, mats[0][...], preferred_element_type=F32)
        kv = jnp.dot(kvn, mats[1][...], preferred_element_type=F32)
        qs, ks, vs = [], [], []
        for h in range(MLA_HEADS):
            b = 2 * HP * h
            qs += [qf[:, b:b + HP] * qscale, _rope(qf[:, b + HP:b + 2 * HP], cos_, sin_) * qscale]
            ks += [kv[:, b:b + HP], kr]
            vs += [kv[:, b + HP:b + 2 * HP]]
        return [rq, rk, rv, qn, kvn, jnp.concatenate(qs, axis=1), jnp.concatenate(ks, axis=1), jnp.concatenate(vs, axis=1)], []

    return _rowcall("ev_mid_fwd", fn, [_z_ret(z), *_z_latents(z), cos, sin], [qg, kvg],
                    [(512, BF16), (512, BF16), (512, BF16), (MLA_Q_RANK, BF16), (MLA_KV_RANK, BF16),
                     (1024, BF16), (1024, BF16), (512, BF16)], mats=[w_uq, w_ukv])


def _pre_bwd_math(dh, x, res, g, mod, bias):
    D = D_MODEL
    _, vjp = jax.vjp(_pre_math, x, g, mod[:, D:2 * D], bias[:, D:2 * D], mod[:, :D], bias[:, :D])
    dx, dg, dsc, _, dsh, _ = vjp(dh)
    return dx + res, dg, dsc, dsh


def _ev_in_bwd(z, cos, sin, qg, kvg, drq, drk, drv, drg, dqn, dkvn, dkr, dmg, x, dx_res, w_in, g, mod, bias):
    D = D_MODEL

    def fn(rows, p, mats):
        l0, l1, l2, c_, s_, drq_, drk_, drv_, drg_, dqn_, dkvn_, dkr_, dmg_, x_, res_ = rows
        qg_, kvg_, g_, mod_, b_ = p
        cos_, sin_ = c_[...], s_[...]
        lat = jnp.concatenate([l0[...], l1[...], l2[...]], axis=1)
        parts = []
        for h in range(RET_HEADS):
            t = drq_[:, HP * h:HP * (h + 1)].astype(F32) + drq_[:, 512 + HP * h:512 + HP * (h + 1)].astype(F32)
            parts.append(_rope_t(t, cos_, sin_))
        for h in range(RET_HEADS):
            t = drk_[:, HP * h:HP * (h + 1)].astype(F32) + drk_[:, 512 + HP * h:512 + HP * (h + 1)].astype(F32)
            parts.append(_rope_t(t, cos_, sin_) * (RET_QK_DIM ** -0.5))
        parts.append(drv_[:, :512].astype(F32) + drv_[:, 512:].astype(F32))
        parts += [drg_[...], dmg_[...]]
        _, vq = jax.vjp(_rms, lat[:, :Z_CKV - Z_CQ], qg_)
        dcq, dqg = vq(dqn_[...])
        _, vkv = jax.vjp(_rms, lat[:, Z_CKV - Z_CQ:Z_KR - Z_CQ], kvg_)
        dckv, dkvg = vkv(dkvn_[...])
        parts += [dcq, dckv, _rope_t(dkr_[...], cos_, sin_)]
        dz = jnp.concatenate([v.astype(BF16) for v in parts], axis=1)
        dh = lax.dot_general(dz, mats[0][...], NT_DIMS, preferred_element_type=F32)
        dx, dg, dsc, dsh = _pre_bwd_math(dh, x_[...], res_[...], g_, mod_, b_)
        return [dz, dx], [dqg, dkvg, dg, dsc, dsh]

    rows = [*_z_latents(z), cos, sin, drq, drk, drv, drg, dqn, dkvn, dkr, dmg, x, dx_res]
    return _rowcall("ev_in_bwd", fn, rows, [qg, kvg, g, mod, bias], [(Z_END, BF16), (D, F32)],
                    [(1, MLA_Q_RANK), (1, MLA_KV_RANK)] + [(1, D)] * 3, ts=256, mats=[w_in])


ATT_SCALE = (MLA_NOPE_DIM + MLA_ROPE_DIM) ** -0.5
LOG2E = 1.4426950408889634
LN2 = 0.6931471805599453


def _ev_mid_bwd(dQ, dK, dV, cos, sin, w_uq, w_ukv):
    def fn(rows, p, mats):
        dq_, dk_, dv_, c_, s_ = rows
        cos_, sin_ = c_[...], s_[...]
        dqs, dkvs = [], []
        dkr = None
        for h in range(MLA_HEADS):
            b = 2 * HP * h
            dqs += [dq_[:, b:b + HP] * ATT_SCALE, _rope_t(dq_[:, b + HP:b + 2 * HP], cos_, sin_) * ATT_SCALE]
            dkvs += [dk_[:, b:b + HP] * LN2, dv_[:, HP * h:HP * (h + 1)]]
            t = dk_[:, b + HP:b + 2 * HP]
            dkr = t if dkr is None else dkr + t
        dqf = jnp.concatenate(dqs, axis=1).astype(BF16)
        dkv = jnp.concatenate(dkvs, axis=1).astype(BF16)
        dqn = lax.dot_general(dqf, mats[0][...], NT_DIMS, preferred_element_type=F32)
        dkvn = lax.dot_general(dkv, mats[1][...], NT_DIMS, preferred_element_type=F32)
        return [dqf, dkv, dqn, dkvn, dkr * LN2], []

    return _rowcall("ev_mid_bwd", fn, [dQ, dK, dV, cos, sin], [],
                    [(1024, BF16), (1024, BF16), (MLA_Q_RANK, F32), (MLA_KV_RANK, F32), (HP, F32)], mats=[w_uq, w_ukv])


def _mix_math(o, rg, a, mg):
    outs = []
    for h in range(RET_HEADS):
        oh = o[:, HP * h:HP * (h + 1)]
        mu = jnp.mean(oh, axis=-1, keepdims=True)
        var = jnp.mean(jnp.square(oh - mu), axis=-1, keepdims=True)
        outs.append((oh - mu) * lax.rsqrt(var + EPS))
    ret = jnp.concatenate(outs, axis=1) * _silu(rg)
    return jnp.concatenate([ret, a * _silu(mg)], axis=1)


def _ev_out_fwd(o2, z, a, x, w_out, g, mod, bias):
    D = D_MODEL

    def fn(rows, p, mats):
        o_, rg_, mg_, a_, x_ = rows
        g_, mod_, b_ = p
        mix = _mix_math(o_[:, :512] + o_[:, 512:], rg_[...], a_[...], mg_[...]).astype(BF16)
        y = jnp.dot(mix, mats[0][...], preferred_element_type=F32)
        return [mix, y, _post_math(x_[...], y, g_, mod_[:, 2 * D:], b_[:, 2 * D:])], []

    return _rowcall("ev_out", fn, [o2, *_z_gates(z), a, x], [g, mod, bias], [(1024, BF16), (D, F32), (D, F32)], mats=[w_out])


def _post_bwd_math(y, dxo, g, mod, bias):
    D = D_MODEL
    b_gate = bias[:, 2 * D:]
    _, vjp = jax.vjp(lambda y_, g2, mg: _post_math(0.0, y_, g2, mg, b_gate), y, g, mod[:, 2 * D:])
    return vjp(dxo)


def _ev_out_bwd(y, dxo, o2, z, a, w_out, g, mod, bias):
    D = D_MODEL

    def fn(rows, p, mats):
        y_, dxo_, o_, rg_, mg_, a_ = rows
        g_, mod_, b_ = p
        dy, dg, dgate = _post_bwd_math(y_[...], dxo_[...], g_, mod_, b_)
        dyb = dy.astype(BF16)
        dmix = lax.dot_general(dyb, mats[0][...], NT_DIMS, preferred_element_type=F32)
        _, vjp = jax.vjp(_mix_math, o_[:, :512] + o_[:, 512:], rg_[...], a_[...], mg_[...])
        return [dyb, *vjp(dmix)], [dg, dgate]

    return _rowcall("ev_out_bwd", fn, [y, dxo, o2, *_z_gates(z), a], [g, mod, bias],
                    [(D, BF16), (512, BF16), (512, F32), (512, BF16), (512, F32)], [(1, D)] * 2, mats=[w_out])


def _glu_math(a, b, ba, bb):
    return (a + ba) * jax.nn.sigmoid(b + bb)


def _od_in_fwd(x, g, mod, bias, w_in, bin_):
    D = D_MODEL

    def fn(rows, p, mats):
        g_, mod_, b_, bi_ = p
        h = _pre_math(rows[0][...], g_, mod_[:, D:2 * D], b_[:, D:2 * D], mod_[:, :D], b_[:, :D]).astype(BF16)
        z = jnp.dot(h, mats[0][...], preferred_element_type=F32)
        return [h, z, _glu_math(z[:, :D], z[:, D:2 * D], bi_[:, :D], bi_[:, D:2 * D])], []

    return _rowcall("od_in", fn, [x], [g, mod, bias, bin_], [(D, BF16), (3 * D, F32), (D, F32)], mats=[w_in])


def _odmix_math(uc, g, bg, ln_g, ln_b):
    mu = jnp.mean(uc, axis=-1, keepdims=True)
    var = jnp.mean(jnp.square(uc - mu), axis=-1, keepdims=True)
    y = (uc - mu) * lax.rsqrt(var + EPS) * ln_g + ln_b
    return _silu(y) * _silu(g + bg)


def _od_out_fwd(uc, z, x, w_out, bin_, ln_g, ln_b, g, mod, bias, tgt=None):
    D = D_MODEL

    def fn(rows, p, mats):
        uc_, zg_, x_ = rows[:3]
        bi_, lg_, lb_, g_, mod_, b_ = p
        vv = _odmix_math(uc_[...], zg_[...], bi_[:, 2 * D:], lg_, lb_).astype(BF16)
        y = jnp.dot(vv, mats[0][...], preferred_element_type=F32)
        xn = _post_math(x_[...], y, g_, mod_[:, 2 * D:], b_[:, 2 * D:])
        if tgt is None:
            return [vv, y, xn], []
        err = xn - rows[3][...]
        part = 0.5 * jnp.sum(jnp.mean(err * err, axis=-1, keepdims=True), axis=0, keepdims=True)
        return [vv, y, err * (1.0 / D)], [jnp.broadcast_to(part, (1, LANES))]

    rows = [uc, (z, D, 2), x] + ([] if tgt is None else [tgt])
    return _rowcall("od_out" if tgt is None else "od_out_loss", fn, rows, [bin_, ln_g, ln_b, g, mod, bias],
                    [(D, BF16), (D, F32), (D, F32)], [] if tgt is None else [(1, LANES)], mats=[w_out])


def _od_out_bwd(y, dxo, uc, z, w_out, bin_, ln_g, ln_b, g, mod, bias):
    D = D_MODEL

    def fn(rows, p, mats):
        y_, dxo_, uc_, zg_ = rows
        bi_, lg_, lb_, g_, mod_, b_ = p
        dy, dpg, dgate = _post_bwd_math(y_[...], dxo_[...], g_, mod_, b_)
        dyb = dy.astype(BF16)
        dvv = lax.dot_general(dyb, mats[0][...], NT_DIMS, preferred_element_type=F32)
        _, vjp = jax.vjp(_odmix_math, uc_[...], zg_[...], bi_[:, 2 * D:], lg_, lb_)
        duc, dg, dbg, dlg, dlb = vjp(dvv)
        return [dyb, duc, dg], [dpg, dgate, dbg, dlg, dlb]

    return _rowcall("od_out_bwd", fn, [y, dxo, uc, (z, D, 2)], [bin_, ln_g, ln_b, g, mod, bias],
                    [(D, BF16), (D, F32), (D, BF16)], [(1, D)] * 5, mats=[w_out])


def _od_in_bwd(z, bin_, du, dg, x, dx_res, w_in, g, mod, bias):
    D = D_MODEL

    def fn(rows, p, mats):
        za_, zb_, du_, dg_, x_, res_ = rows
        bi_, g_, mod_, b_ = p
        _, vjp = jax.vjp(_glu_math, za_[...], zb_[...], bi_[:, :D], bi_[:, D:2 * D])
        da, db, dba, dbb = vjp(du_[...])
        dz = jnp.concatenate([da.astype(BF16), db.astype(BF16), dg_[...]], axis=1)
        dh = lax.dot_general(dz, mats[0][...], NT_DIMS, preferred_element_type=F32)
        dx, dpg, dsc, dsh = _pre_bwd_math(dh, x_[...], res_[...], g_, mod_, b_)
        return [dz, dx], [dba, dbb, dpg, dsc, dsh]

    return _rowcall("od_in_bwd", fn, [(z, D, 0), (z, D, 1), du, dg, x, dx_res], [bin_, g, mod, bias],
                    [(3 * D, BF16), (D, F32)], [(1, D)] * 5, ts=256, mats=[w_in])


def _tile(n, cap):
    if n <= cap:
        return n
    best = None
    for t in range(LANES, cap + 1, LANES):
        if n % t == 0:
            best = t
    assert best is not None, (n, cap)
    return best


MM_TN_CAP = 1792
NT_DIMS = (((1,), (1,)), ((), ()))
TN_DIMS = (((0,), (0,)), ((), ()))


def _mm_nn(name, a, b, out_dtype=F32, tm=512, tn_cap=MM_TN_CAP):
    M, K = a.shape
    N = b.shape[1]
    tm = min(tm, M)
    tn = _tile(N, tn_cap)

    def body(a_ref, b_ref, o_ref):
        o_ref[...] = jnp.dot(a_ref[...], b_ref[...], preferred_element_type=F32).astype(o_ref.dtype)

    nb = _nbytes((tm, K), a.dtype) + _nbytes((K, tn), b.dtype) + _nbytes((tm, tn), out_dtype) + _nbytes((tm, tn), F32)
    return pl.pallas_call(
        body, name=name, grid=(N // tn, M // tm),
        in_specs=[pl.BlockSpec((tm, K), lambda j, i: (i, 0)), pl.BlockSpec((K, tn), lambda j, i: (0, j))],
        out_specs=pl.BlockSpec((tm, tn), lambda j, i: (i, j)),
        out_shape=jax.ShapeDtypeStruct((M, N), out_dtype),
        compiler_params=_params(("parallel", "parallel"), nb),
    )(_hbm(a), _hbm(b))


def _pre_mm(name, x, g, mod, bias, w, tm=512):
    S, D = x.shape
    N = w.shape[1]
    tm = min(tm, S)

    def body(x_ref, g_ref, mod_ref, bias_ref, w_ref, h_ref, z_ref):
        mod_, b_ = mod_ref[...], bias_ref[...]
        h = _pre_math(x_ref[...], g_ref[...], mod_[:, D:2 * D], b_[:, D:2 * D], mod_[:, :D], b_[:, :D]).astype(BF16)
        h_ref[...] = h
        z_ref[...] = jnp.dot(h, w_ref[...], preferred_element_type=F32)

    nb = 2 * _nbytes((tm, D), F32) + _nbytes((D, N), w.dtype) + 2 * _nbytes((tm, N), F32)
    row = lambda i: (i, 0)
    whole = lambda i: (0, 0)
    return pl.pallas_call(
        body, name=name, grid=(S // tm,),
        in_specs=[pl.BlockSpec((tm, D), row), pl.BlockSpec(g.shape, whole), pl.BlockSpec(mod.shape, whole),
                  pl.BlockSpec(bias.shape, whole), pl.BlockSpec((D, N), whole)],
        out_specs=[pl.BlockSpec((tm, D), row), pl.BlockSpec((tm, N), row)],
        out_shape=[jax.ShapeDtypeStruct((S, D), BF16), jax.ShapeDtypeStruct((S, N), F32)],
        compiler_params=_params(("parallel",), nb),
    )(_hbm(x), g, mod, bias, _hbm(w))


def _mm_tn(name, a, b, out_dtype=F32, ts=2048, tm_cap=1024, tn_cap=MM_TN_CAP):
    S, M = a.shape
    N = b.shape[1]
    ts = min(ts, S)
    tm = _tile(M, tm_cap)
    tn = _tile(N, tn_cap)
    ns = S // ts

    def body(a_ref, b_ref, o_ref, acc):
        s = pl.program_id(2)

        @pl.when(s == 0)
        def _():
            acc[...] = jnp.zeros(acc.shape, F32)

        acc[...] += lax.dot_general(a_ref[...], b_ref[...], TN_DIMS, preferred_element_type=F32)

        @pl.when(s == ns - 1)
        def _():
            o_ref[...] = acc[...].astype(o_ref.dtype)

    nb = _nbytes((ts, tm), a.dtype) + _nbytes((ts, tn), b.dtype) + 3 * _nbytes((tm, tn), F32)
    return pl.pallas_call(
        body, name=name, grid=(M // tm, N // tn, ns),
        in_specs=[pl.BlockSpec((ts, tm), lambda i, j, s: (s, i)), pl.BlockSpec((ts, tn), lambda i, j, s: (s, j))],
        out_specs=pl.BlockSpec((tm, tn), lambda i, j, s: (i, j)),
        out_shape=jax.ShapeDtypeStruct((M, N), out_dtype),
        scratch_shapes=[pltpu.VMEM((tm, tn), F32)],
        compiler_params=_params(("parallel", "parallel", "arbitrary"), nb),
    )(_hbm(a), _hbm(b))


ATT_TQ = 1024
ATT_TK = 1024
ATT_TK_FWD = 1024
ATT_TQ_FWD = 2048
QW = 2 * HP


class _Side:
    def __init__(self, side):
        self.on = side is not None
        if self.on:
            self.p, self.mode = side
            shp = self.p.shape
            self.h, self.w = (shp[0] // 2, shp[1]) if self.mode == "half" else (shp[-2], shp[-1])
        self.in_specs = [_HBM] if self.on else []
        self.out_specs = [_HBM] if self.on else []
        self.out_shape = [jax.ShapeDtypeStruct((4, self.h, self.w), self.p.dtype)] if self.on else []
        self.scratch = list(_CHIP_SEMS) if self.on else []
        self.operands = [_hbm(self.p)] if self.on else []

    def copies(self, p_ref, x_ref, sems):
        return _chip_copies(p_ref, x_ref, *sems, self.mode, self.h)

    def start(self, p_ref, x_ref, sems, first):
        @pl.when(first)
        def _():
            for cp in self.copies(p_ref, x_ref, sems):
                cp.start()

    def finish(self, p_ref, x_ref, sems, last):
        @pl.when(last)
        def _():
            for cp in self.copies(p_ref, x_ref, sems):
                cp.wait()


def _flash_fwd(Q, K, V, side=None):
    S = Q.shape[0]
    H = MLA_HEADS
    tq, tk = min(ATT_TQ_FWD, S), min(ATT_TK_FWD, S)
    nk = S // tk
    sd = _Side(side)

    def body(*refs):
        if sd.on:
            q_ref, k_ref, vt_ref, p_ref, o_ref, lse_ref, x_ref, m_s, l_s, acc_s, s_a, s_b, *sems = refs
            step_id = pl.program_id(0) * (S // tq) + pl.program_id(1)
            sd.start(p_ref, x_ref, sems, step_id == 0)
        else:
            q_ref, k_ref, vt_ref, o_ref, lse_ref, m_s, l_s, acc_s, s_a, s_b = refs
        m_s[...] = jnp.full(m_s.shape, -jnp.inf, F32)
        l_s[...] = jnp.zeros(l_s.shape, F32)
        acc_s[...] = jnp.zeros(acc_s.shape, F32)

        def scores(j):
            k = k_ref[pl.ds(pl.multiple_of(j * tk, tk), tk), :]
            return lax.dot_general(k, q_ref[...], NT_DIMS, preferred_element_type=F32)

        def consume(st, j):
            m_prev = m_s[...]
            m_new = jnp.maximum(m_prev, jnp.max(st, axis=0, keepdims=True))
            alpha = jnp.exp2(m_prev - m_new)
            pt = jnp.exp2(st - m_new)
            l_s[...] = alpha * l_s[...] + jnp.sum(pt, axis=0, keepdims=True)
            v = vt_ref[pl.ds(pl.multiple_of(j * tk, tk), tk), :]
            acc_s[...] = alpha * acc_s[...] + lax.dot_general(v, pt.astype(BF16), TN_DIMS, preferred_element_type=F32)
            m_s[...] = m_new

        if nk % 2:
            def step(j, carry):
                consume(scores(j), j)
                return carry

            lax.fori_loop(0, nk, step, 0)
        else:
            s_a[...] = scores(0)

            def pair(jj, carry):
                j0 = 2 * jj
                s_b[...] = scores(j0 + 1)
                consume(s_a[...], j0)
                s_a[...] = scores(jnp.minimum(j0 + 2, nk - 1))
                consume(s_b[...], j0 + 1)
                return carry

            lax.fori_loop(0, nk // 2, pair, 0)
        l = l_s[...]
        o_ref[...] = (acc_s[...] * (1.0 / l)).T
        lse_ref[0] = m_s[...] + jnp.log2(l)
        if sd.on:
            sd.finish(p_ref, x_ref, sems, step_id == H * (S // tq) - 1)

    nb = (_nbytes((tq, QW), BF16) + _nbytes((S, QW), BF16) + _nbytes((S, HP), BF16) + 3 * _nbytes((tq, HP), F32)
          + 4 * _nbytes((tq, tk), F32))
    return pl.pallas_call(
        body, name="flash_fwd_side" if sd.on else "flash_fwd", grid=(H, S // tq),
        in_specs=[pl.BlockSpec((tq, QW), lambda h, i: (i, h)), pl.BlockSpec((S, QW), lambda h, i: (0, h)),
                  pl.BlockSpec((S, HP), lambda h, i: (0, h))] + sd.in_specs,
        out_specs=[pl.BlockSpec((tq, HP), lambda h, i: (i, h)), pl.BlockSpec((1, 1, tq), lambda h, i: (h, 0, i))] + sd.out_specs,
        out_shape=[jax.ShapeDtypeStruct((S, H * HP), F32), jax.ShapeDtypeStruct((H, 1, S), F32)] + sd.out_shape,
        scratch_shapes=[pltpu.VMEM((1, tq), F32), pltpu.VMEM((1, tq), F32), pltpu.VMEM((HP, tq), F32),
                        pltpu.VMEM((tk, tq), F32), pltpu.VMEM((tk, tq), F32)] + sd.scratch,
        compiler_params=_params(("arbitrary", "arbitrary") if sd.on else ("parallel", "parallel"), nb),
    )(_hbm(Q), _hbm(K), _hbm(V), *sd.operands)


def _flash_bwd(Q, K, V, dO, O, lse, side=None):
    S = Q.shape[0]
    H = MLA_HEADS
    tq, tk = min(ATT_TQ, S), min(ATT_TK, S)
    nk = S // tk
    sd = _Side(side)

    def body(*refs):
        if sd.on:
            q_ref, do_ref, o_ref, lse_ref, k_ref, v_ref, p_ref, dq_ref, dk_ref, dv_ref, x_ref, dq_s, *sems = refs
            step_id = pl.program_id(0) * (S // tq) + pl.program_id(1)
            sd.start(p_ref, x_ref, sems, step_id == 0)
        else:
            q_ref, do_ref, o_ref, lse_ref, k_ref, v_ref, dq_ref, dk_ref, dv_ref, dq_s = refs

        @pl.when(pl.program_id(1) == 0)
        def _():
            dk_ref[...] = jnp.zeros(dk_ref.shape, F32)
            dv_ref[...] = jnp.zeros(dv_ref.shape, F32)

        delta = jnp.sum((do_ref[...].astype(F32) * o_ref[...]).T, axis=0, keepdims=True)
        lse = lse_ref[0]
        dq_s[...] = jnp.zeros(dq_s.shape, F32)

        def rows(j):
            return pl.ds(pl.multiple_of(j * tk, tk), tk)

        def scores(j):
            st = lax.dot_general(k_ref[rows(j), :], q_ref[...], NT_DIMS, preferred_element_type=F32)
            dpt = lax.dot_general(v_ref[rows(j), :], do_ref[...], NT_DIMS, preferred_element_type=F32)
            return st, dpt

        def consume(st, dpt, j):
            pt = jnp.exp2(st - lse)
            dst = (pt * (dpt - delta)).astype(BF16)
            dv_ref[rows(j), :] += jnp.dot(pt.astype(BF16), do_ref[...], preferred_element_type=F32)
            dk_ref[rows(j), :] += jnp.dot(dst, q_ref[...], preferred_element_type=F32)
            dq_s[...] += lax.dot_general(dst, k_ref[rows(j), :], TN_DIMS, preferred_element_type=F32)

        def step(j, carry):
            consume(*scores(j), j)
            return carry

        lax.fori_loop(0, nk, step, 0)
        dq_ref[...] = dq_s[...]
        if sd.on:
            sd.finish(p_ref, x_ref, sems, step_id == H * (S // tq) - 1)

    nb = (_nbytes((tq, QW), BF16) + _nbytes((tq, HP), BF16) + _nbytes((tq, HP), F32) + _nbytes((S, QW), BF16)
          + _nbytes((S, HP), BF16) + 2 * _nbytes((tq, QW), F32) + _nbytes((S, QW), F32) + _nbytes((S, HP), F32))
    return pl.pallas_call(
        body, name="flash_bwd_side" if sd.on else "flash_bwd", grid=(H, S // tq),
        in_specs=[pl.BlockSpec((tq, QW), lambda h, i: (i, h)), pl.BlockSpec((tq, HP), lambda h, i: (i, h)),
                  pl.BlockSpec((tq, HP), lambda h, i: (i, h)), pl.BlockSpec((1, 1, tq), lambda h, i: (h, 0, i)),
                  pl.BlockSpec((S, QW), lambda h, i: (0, h)), pl.BlockSpec((S, HP), lambda h, i: (0, h))] + sd.in_specs,
        out_specs=[pl.BlockSpec((tq, QW), lambda h, i: (i, h)), pl.BlockSpec((S, QW), lambda h, i: (0, h)),
                   pl.BlockSpec((S, HP), lambda h, i: (0, h))] + sd.out_specs,
        out_shape=[jax.ShapeDtypeStruct((S, H * QW), F32), jax.ShapeDtypeStruct((S, H * QW), F32),
                   jax.ShapeDtypeStruct((S, H * HP), F32)] + sd.out_shape,
        scratch_shapes=[pltpu.VMEM((tq, QW), F32)] + sd.scratch,
        compiler_params=_params(("arbitrary", "arbitrary") if sd.on else ("parallel", "arbitrary"), nb),
    )(_hbm(Q), _hbm(dO), _hbm(O), lse, _hbm(K), _hbm(V), *sd.operands)


RET_CHUNK = 256


def _ret_tables(d, lg_ref, h, C):
    ii = lax.broadcasted_iota(jnp.int32, (C, C), 0).astype(F32)
    jj = lax.broadcasted_iota(jnp.int32, (C, C), 1).astype(F32)
    ci = lax.broadcasted_iota(jnp.int32, (C, 1), 0).astype(F32)
    fwd = d == 0
    dist = jnp.where(fwd, ii - jj, jj - ii)
    mask = dist >= jnp.where(fwd, 0.0, 1.0)
    dist = jnp.maximum(dist, 0.0)
    qpos = jnp.where(fwd, ci + 1.0, C - ci)
    kpos = jnp.where(fwd, C - 1.0 - ci, ci)
    lg = lg_ref[d, h]
    D = jnp.where(mask, jnp.exp(lg * dist), 0.0)
    return D, jnp.exp(lg * qpos), jnp.exp(lg * kpos), jnp.exp(lg * C), dist, qpos, kpos


def _ret_fwd(lg, q, k, v):
    S = q.shape[0]
    C = min(RET_CHUNK, S)
    N = S // C
    H = RET_HEADS

    def chunk(d, n):
        return jnp.where(d == 0, n, N - 1 - n)

    def body(lg_ref, q_ref, k_ref, v_ref, o_ref, st_ref, state, tab_d, tab_q, tab_k):
        d, n = pl.program_id(0), pl.program_id(1)

        @pl.when(n == 0)
        def _():
            state[...] = jnp.zeros_like(state)
            for h in range(H):
                D, qw, kw, _, _, _, _ = _ret_tables(d, lg_ref, h, C)
                tab_d[h] = D
                tab_q[h] = jnp.broadcast_to(qw, (C, HP))
                tab_k[h] = jnp.broadcast_to(kw, (C, HP))

        for h in range(H):
            sl = slice(HP * h, HP * (h + 1))
            D, qw, kw = tab_d[h], tab_q[h], tab_k[h]
            gc = jnp.exp(lg_ref[d, h] * C)
            qh, kh, vh = q_ref[:, sl], k_ref[:, sl], v_ref[:, sl]
            st = state[sl, :]
            sm = lax.dot_general(qh, kh, (((1,), (1,)), ((), ())), preferred_element_type=F32) * D
            inner = jnp.dot(sm.astype(BF16), vh, preferred_element_type=F32)
            cross = qw * jnp.dot(qh, st.astype(BF16), preferred_element_type=F32)
            o_ref[:, sl] = inner + cross
            st_ref[0, 0, sl, :] = st
            kvn = lax.dot_general((kh.astype(F32) * kw).astype(BF16), vh, (((0,), (0,)), ((), ())), preferred_element_type=F32)
            state[sl, :] = gc * st + kvn

    nb = 3 * _nbytes((C, 512), BF16) + _nbytes((C, 512), F32) + 2 * _nbytes((512, HP), F32) + 8 * _nbytes((C, C), F32)
    return pl.pallas_call(
        body, name="ret_fwd", grid=(2, N),
        in_specs=[pl.BlockSpec(memory_space=pltpu.SMEM)] + [pl.BlockSpec((C, 512), lambda d, n: (chunk(d, n), 0))] * 3,
        out_specs=[pl.BlockSpec((C, 512), lambda d, n: (chunk(d, n), d)),
                   pl.BlockSpec((1, 1, 512, HP), lambda d, n: (d, chunk(d, n), 0, 0))],
        out_shape=[jax.ShapeDtypeStruct((S, 1024), F32), jax.ShapeDtypeStruct((2, N, 512, HP), F32)],
        scratch_shapes=[pltpu.VMEM((512, HP), F32), pltpu.VMEM((H, C, C), F32), pltpu.VMEM((H, C, HP), F32),
                        pltpu.VMEM((H, C, HP), F32)],
        compiler_params=_params(("arbitrary", "arbitrary"), nb),
    )(lg, _hbm(q), _hbm(k), _hbm(v))


def _ret_bwd(lg, q, k, v, do, states):
    S = q.shape[0]
    C = min(RET_CHUNK, S)
    N = S // C
    H = RET_HEADS

    def chunk(d, n):
        return jnp.where(d == 0, N - 1 - n, n)

    def body(lg_ref, q_ref, k_ref, v_ref, do_ref, st_ref, dq_ref, dk_ref, dv_ref, dlg_ref, G, accA, accB, accC,
             tab_d, tab_q, tab_k, tab_qp, tab_kp):
        d, n = pl.program_id(0), pl.program_id(1)

        @pl.when(n == 0)
        def _():
            G[...] = jnp.zeros_like(G)
            accA[...] = jnp.zeros_like(accA)
            accB[...] = jnp.zeros_like(accB)
            accC[...] = jnp.zeros_like(accC)
            for h in range(H):
                D, qw, kw, _, _, qpos, kpos = _ret_tables(d, lg_ref, h, C)
                tab_d[h] = D
                tab_q[h] = jnp.broadcast_to(qw, (C, HP))
                tab_k[h] = jnp.broadcast_to(kw, (C, HP))
                tab_qp[h] = jnp.broadcast_to(qw * qpos, (C, HP))
                tab_kp[h] = jnp.broadcast_to(kw * kpos, (C, HP))

        nt = (((1,), (1,)), ((), ()))
        tn = (((0,), (0,)), ((), ()))
        for h in range(H):
            sl = slice(HP * h, HP * (h + 1))
            D, qw, kw = tab_d[h], tab_q[h], tab_k[h]
            gc = jnp.exp(lg_ref[d, h] * C)
            qh, kh, vh, doh = q_ref[:, sl], k_ref[:, sl], v_ref[:, sl], do_ref[:, sl]
            st = st_ref[0, 0, sl, :]
            g = G[sl, :]
            stb, gb = st.astype(BF16), g.astype(BF16)
            sraw = lax.dot_general(qh, kh, nt, preferred_element_type=F32)
            dS = lax.dot_general(doh, vh, nt, preferred_element_type=F32) * D
            dSb = dS.astype(BF16)
            smb = (sraw * D).astype(BF16)
            qs = jnp.dot(qh, stb, preferred_element_type=F32)
            kg = jnp.dot(kh, gb, preferred_element_type=F32)
            dqh = jnp.dot(dSb, kh, preferred_element_type=F32) + qw * lax.dot_general(doh, stb, nt, preferred_element_type=F32)
            dkh = lax.dot_general(dSb, qh, tn, preferred_element_type=F32) + kw * lax.dot_general(vh, gb, nt, preferred_element_type=F32)
            dvh = lax.dot_general(smb, doh, tn, preferred_element_type=F32) + kw * kg
            dq_ref[:, sl] = dqh.astype(dq_ref.dtype)
            dk_ref[:, sl] = dkh.astype(dk_ref.dtype)
            dv_ref[:, sl] = dvh.astype(dv_ref.dtype)
            dof, vf = doh.astype(F32), vh.astype(F32)
            accA[h] += sraw * dS
            accB[h] += tab_qp[h] * (qs * dof) + tab_kp[h] * (kg * vf)
            accC[h] += st * g
            G[sl, :] = gc * g + lax.dot_general((qh.astype(F32) * qw).astype(BF16), doh, tn, preferred_element_type=F32)

        @pl.when(n == N - 1)
        def _():
            rows = []
            for h in range(H):
                _, _, _, gc, dist, _, _ = _ret_tables(d, lg_ref, h, C)
                tot = jnp.sum(accA[h] * dist) + jnp.sum(accB[h]) + (C * gc) * jnp.sum(accC[h])
                rows.append(jnp.broadcast_to(tot, (1, HP)))
            dlg_ref[0] = jnp.concatenate(rows + [jnp.zeros((8 - H, HP), F32)], axis=0)

    nb = (4 * _nbytes((C, 512), BF16) + 3 * _nbytes((C, 512), F32) + 2 * _nbytes((512, HP), F32)
          + H * (_nbytes((C, C), F32) + _nbytes((C, HP), F32) + _nbytes((HP, HP), F32)) + 12 * _nbytes((C, C), F32))
    row = lambda d, n: (chunk(d, n), 0)
    out3 = lambda d, n: (chunk(d, n), d)
    return pl.pallas_call(
        body, name="ret_bwd", grid=(2, N),
        in_specs=[pl.BlockSpec(memory_space=pltpu.SMEM)] + [pl.BlockSpec((C, 512), row)] * 4
        + [pl.BlockSpec((1, 1, 512, HP), lambda d, n: (d, chunk(d, n), 0, 0))],
        out_specs=[pl.BlockSpec((C, 512), out3)] * 3 + [pl.BlockSpec((1, 8, HP), lambda d, n: (d, 0, 0))],
        out_shape=[jax.ShapeDtypeStruct((S, 1024), BF16)] * 3 + [jax.ShapeDtypeStruct((2, 8, HP), F32)],
        scratch_shapes=[pltpu.VMEM((512, HP), F32), pltpu.VMEM((H, C, C), F32), pltpu.VMEM((H, C, HP), F32),
                        pltpu.VMEM((H, HP, HP), F32), pltpu.VMEM((H, C, C), F32)] + [pltpu.VMEM((H, C, HP), F32)] * 4,
        compiler_params=_params(("arbitrary", "arbitrary"), nb),
    )(lg, _hbm(q), _hbm(k), _hbm(v), _hbm(do), _hbm(states))


CONV_PAD = 16
CONV_TR = 256
CONV_CB = LANES


def _fill_padded(pad, u_ref, S):
    pad[0:CONV_PAD, :] = jnp.zeros((CONV_PAD, CONV_CB), F32)
    pad[CONV_PAD + S:CONV_PAD + S + CONV_PAD, :] = jnp.zeros((CONV_PAD, CONV_CB), F32)
    pad[CONV_PAD:CONV_PAD + S, :] = u_ref[...]


def _conv_fwd(u, w32, b):
    S, D = u.shape
    tr = min(CONV_TR, S)

    def body(u_ref, w_ref, b_ref, o_ref, pad):
        _fill_padded(pad, u_ref, S)
        wv = w_ref[...]
        bv = b_ref[...]

        def step(t, carry):
            r0 = pl.multiple_of(t * tr, tr)
            acc = jnp.broadcast_to(bv, (tr, CONV_CB))
            for k in range(CONV_KERNEL):
                acc = acc + pad[pl.ds(r0 + (k + 1), tr), :] * wv[k:k + 1, :]
            o_ref[pl.ds(r0, tr), :] = acc
            return carry

        lax.fori_loop(0, S // tr, step, 0)

    nb = 2 * _nbytes((S, CONV_CB), F32) + _nbytes((S + 2 * CONV_PAD, CONV_CB), F32)
    return pl.pallas_call(
        body, name="conv_fwd", grid=(D // CONV_CB,),
        in_specs=[pl.BlockSpec((S, CONV_CB), lambda j: (0, j)), pl.BlockSpec((32, CONV_CB), lambda j: (0, j)),
                  pl.BlockSpec((1, CONV_CB), lambda j: (0, j))],
        out_specs=pl.BlockSpec((S, CONV_CB), lambda j: (0, j)),
        out_shape=jax.ShapeDtypeStruct((S, D), F32),
        scratch_shapes=[pltpu.VMEM((S + 2 * CONV_PAD, CONV_CB), F32)],
        compiler_params=_params(("parallel",), nb),
    )(_hbm(u), w32, b)


def _conv_bwd_w(u, dout):
    S, D = u.shape
    tr = min(CONV_TR, S)

    def body(u_ref, d_ref, o_ref, pad, acc):
        _fill_padded(pad, u_ref, S)
        acc[...] = jnp.zeros_like(acc)

        def fold(a):
            return jnp.sum(a.reshape(tr // 8, 8, CONV_CB), axis=0)

        def step(t, carry):
            r0 = pl.multiple_of(t * tr, tr)
            dv = d_ref[pl.ds(r0, tr), :]
            for k in range(CONV_KERNEL):
                acc[8 * k:8 * k + 8, :] += fold(pad[pl.ds(r0 + (k + 1), tr), :] * dv)
            acc[8 * CONV_KERNEL:8 * CONV_KERNEL + 8, :] += fold(dv)
            return carry

        lax.fori_loop(0, S // tr, step, 0)
        o_ref[...] = jnp.sum(acc[...].reshape(32, 8, CONV_CB), axis=1)

    nb = 2 * _nbytes((S, CONV_CB), F32) + _nbytes((S + 2 * CONV_PAD, CONV_CB), F32)
    return pl.pallas_call(
        body, name="conv_bwd_w", grid=(D // CONV_CB,),
        in_specs=[pl.BlockSpec((S, CONV_CB), lambda j: (0, j)), pl.BlockSpec((S, CONV_CB), lambda j: (0, j))],
        out_specs=pl.BlockSpec((32, CONV_CB), lambda j: (0, j)),
        out_shape=jax.ShapeDtypeStruct((32, D), F32),
        scratch_shapes=[pltpu.VMEM((S + 2 * CONV_PAD, CONV_CB), F32), pltpu.VMEM((256, CONV_CB), F32)],
        compiler_params=_params(("parallel",), nb),
    )(_hbm(u), _hbm(dout))


def _adamw(w, g, m, v):
    shape = w.shape
    w2, g2, m2, v2 = [a.reshape(-1, shape[-1]) for a in (w, g, m, v)]
    W = shape[-1]

    def fn(rows, p):
        w_, g_, m_, v_ = [r[...] for r in rows]
        mn = ADAM_B1 * m_ + (1.0 - ADAM_B1) * g_
        vn = ADAM_B2 * v_ + (1.0 - ADAM_B2) * jnp.square(g_)
        m_hat = mn / (1.0 - ADAM_B1 ** ADAM_STEP)
        v_hat = vn / (1.0 - ADAM_B2 ** ADAM_STEP)
        delta = -ADAM_LR * (m_hat / (jnp.sqrt(v_hat) + ADAM_EPS) + ADAM_WD * w_)
        return [delta, mn, vn], []

    R2 = w2.shape[0]
    ts = ROW_TILE if (R2 > ROW_TILE and R2 % ROW_TILE == 0) else R2
    d, mn, vn = _rowcall("adamw", fn, [w2, g2, m2, v2], [], [(W, F32)] * 3, ts=ts)
    return d.reshape(shape), mn.reshape(shape), vn.reshape(shape)


_HBM = pl.BlockSpec(memory_space=pltpu.HBM)


def _place():
    return lax.axis_index("x"), lax.axis_index("y"), lax.axis_index("c")


def _half_rows(ref, half, h):
    idx = (slice(None),) * (len(ref.shape) - 2) + (pl.ds(half * h, h), slice(None))
    return ref.at[idx]


def _chip_exchange(name, p, mode):
    assert mode in ("half", "whole", "scatter")
    if mode == "half":
        h, W = p.shape[0] // 2, p.shape[1]
    elif mode == "whole":
        h, W = p.shape
    else:
        h, W = p.shape[1], p.shape[2]

    def body(p_ref, o_ref, send_sems, recv_sems, local_sem):
        copies = _chip_copies(p_ref, o_ref, send_sems, recv_sems, local_sem, mode, h)
        for cp in copies:
            cp.start()
        for cp in copies:
            cp.wait()

    return pl.pallas_call(
        body, name=name, in_specs=[_HBM], out_specs=_HBM,
        out_shape=jax.ShapeDtypeStruct((4, h, W), p.dtype), scratch_shapes=_CHIP_SEMS,
    )(p)


_CHIP_SEMS = [pltpu.SemaphoreType.DMA((3,)), pltpu.SemaphoreType.DMA((3,)), pltpu.SemaphoreType.DMA]


def _chip_copies(p_ref, o_ref, send_sems, recv_sems, local_sem, mode, h):
    x, y, c = _place()
    k_me = 2 * x + y

    def src(k):
        if mode == "half":
            return _half_rows(p_ref, c, h)
        return p_ref if mode == "whole" else p_ref.at[k]

    copies = [pltpu.make_async_copy(src(k_me), o_ref.at[k_me], local_sem)]
    for j, (cx, cy) in enumerate([(1 - x, y), (x, 1 - y), (1 - x, 1 - y)]):
        copies.append(pltpu.make_async_remote_copy(
            src_ref=src(2 * cx + cy), dst_ref=o_ref.at[k_me], send_sem=send_sems.at[j], recv_sem=recv_sems.at[j],
            device_id=(cx, cy, c), device_id_type=MESH))
    return copies


def _sibling_send_half(name, g):
    n, R, W = g.shape
    h = R // 2

    def body(g_ref, o_ref, send_sem, recv_sem):
        x, y, c = _place()
        cp = pltpu.make_async_remote_copy(
            src_ref=_half_rows(g_ref, 1 - c, h), dst_ref=o_ref, send_sem=send_sem, recv_sem=recv_sem,
            device_id=(x, y, 1 - c), device_id_type=MESH)
        cp.start()
        cp.wait()

    return pl.pallas_call(
        body, name=name, in_specs=[_HBM], out_specs=_HBM,
        out_shape=jax.ShapeDtypeStruct((n, h, W), g.dtype),
        scratch_shapes=[pltpu.SemaphoreType.DMA, pltpu.SemaphoreType.DMA],
    )(g)


def _halves_merge(name, mine):
    def body(m_ref, o_ref, send_sem, recv_sem):
        x, y, c = _place()
        cp = pltpu.make_async_remote_copy(
            src_ref=m_ref, dst_ref=o_ref, send_sem=send_sem, recv_sem=recv_sem,
            device_id=(x, y, 1 - c), device_id_type=MESH)
        cp.start()
        cp.wait()

    theirs = pl.pallas_call(
        body, name=name, in_specs=[_HBM], out_specs=_HBM,
        out_shape=jax.ShapeDtypeStruct(mine.shape, mine.dtype),
        scratch_shapes=[pltpu.SemaphoreType.DMA, pltpu.SemaphoreType.DMA],
    )(mine)
    south = lax.axis_index("c") == 0
    axis = mine.ndim - 2
    return jnp.concatenate([jnp.where(south, mine, theirs), jnp.where(south, theirs, mine)], axis=axis)


def _add2(name, a, b, out_dtype):
    def fn(rows, p):
        return [rows[0][...].astype(F32) + rows[1][...].astype(F32)], []

    return _rowcall(name, fn, [a, b], [], [(a.shape[1], out_dtype)], ts=4 * PACK_TILE)[0]


def _sum4(name, b):
    _, h, W = b.shape
    ts = 2 * PACK_TILE if h % (2 * PACK_TILE) == 0 else (PACK_TILE if h % PACK_TILE == 0 else h)

    def body(b_ref, o_ref):
        o_ref[...] = ((b_ref[0].astype(F32) + b_ref[1].astype(F32)) + b_ref[2].astype(F32)) + b_ref[3].astype(F32)

    nb = _nbytes((4, ts, W), b.dtype) + _nbytes((ts, W), F32)
    return pl.pallas_call(
        body, name=name, grid=(h // ts,),
        in_specs=[pl.BlockSpec((4, ts, W), lambda i: (0, i, 0))], out_specs=pl.BlockSpec((ts, W), lambda i: (i, 0)),
        out_shape=jax.ShapeDtypeStruct((h, W), F32), compiler_params=_params(("parallel",), nb),
    )(b)


PACK_W = 1024
PACK_ROWS = 16
PACK_TILE = 128
EV_SHARDED = (("ev_w_in", "col", False), ("ev_w_uq", "col", False), ("ev_w_ukv", "col", False), ("ev_w_out", "row", False))
OD_SHARDED = (("od_w_in", "col", False), ("od_b_in", "col", True), ("od_dw_w", "col", True), ("od_dw_b", "col", True),
              ("od_ln_g", "col", True), ("od_ln_b", "col", True), ("od_w_out", "row", False))
REPLICATED =("ada_b", "pre_g", "post_g", "ev_dec_f", "ev_dec_b", "ev_q_norm_g", "ev_kv_norm_g")
WEIGHTS = ("ada_w", "ada_b", "pre_g", "post_g", "ev_w_in", "ev_dec_f", "ev_dec_b", "ev_q_norm_g", "ev_w_uq", "ev_kv_norm_g",
           "ev_w_ukv", "ev_w_out", "od_w_in", "od_b_in", "od_dw_w", "od_dw_b", "od_ln_g", "od_ln_b", "od_w_out")


def _layer_entries(layers):
    out = []
    for l in layers:
        out += [(name, how, exact, l // 2) for name, how, exact in (EV_SHARDED if l % 2 == 0 else OD_SHARDED)]
    return out


def _rows_of(n):
    unit = PACK_W * PACK_ROWS
    return (n + unit - 1) // unit * PACK_ROWS


def _to_rows(flat, lead):
    n = flat.shape[-1]
    r = _rows_of(n)
    flat = jnp.pad(flat, [(0, 0)] * len(lead) + [(0, r * PACK_W - n)])
    return flat.reshape(*lead, r, PACK_W)


def _pad_total(buf, axis):
    r = buf.shape[axis]
    r2 = (r + 2 * PACK_TILE - 1) // (2 * PACK_TILE) * (2 * PACK_TILE)
    pads = [(0, 0)] * buf.ndim
    pads[axis] = (0, r2 - r)
    return jnp.pad(buf, pads)


def _split_chips(full, how):
    if how == "col":
        n = full.shape[-1] // 4
        return jnp.moveaxis(full.reshape(*full.shape[:-1], 4, n), -2, 0)
    n = full.shape[-2] // 4
    return jnp.moveaxis(full.reshape(*full.shape[:-2], 4, n, full.shape[-1]), -3, 0)


def _join_chips(blocks, how):
    if how == "col":
        t = jnp.moveaxis(blocks, 0, -2)
        return t.reshape(*t.shape[:-2], t.shape[-2] * t.shape[-1])
    t = jnp.moveaxis(blocks, 0, -3)
    return t.reshape(*t.shape[:-3], t.shape[-3] * t.shape[-2], t.shape[-1])


def _pack_weights(w, entries):
    parts = []
    for name, _, exact, idx in entries:
        a = w[name][idx].reshape(-1)
        a = lax.bitcast_convert_type(a, BF16).reshape(-1) if exact else a.astype(BF16)
        parts.append(_to_rows(a, ()))
    return _pad_total(jnp.concatenate(parts, axis=0), 0)


def _unpack_weights(gathered, w, entries):
    out, r0 = {}, 0
    for name, how, exact, idx in entries:
        shp = w[name].shape[1:]
        n = (2 if exact else 1) * w[name][idx].size
        r = _rows_of(n)
        a = gathered[:, r0:r0 + r].reshape(4, -1)[:, :n]
        if exact:
            a = lax.bitcast_convert_type(a.reshape(4, -1, 2), F32)
        out[(name, idx)] = _join_chips(a.reshape(4, *shp), how)
        r0 += r
    return out


def _pack_grads(grads, entries, replicated):
    parts = []
    for name, how, _, idx in entries:
        b = _split_chips(grads[(name, idx)], how)
        parts.append(_to_rows(b.reshape(4, -1).astype(BF16), (4,)))
    for name in replicated:
        a = _to_rows(grads[name].reshape(-1).astype(BF16), ())
        parts.append(jnp.broadcast_to(a[None], (4, *a.shape)))
    return _pad_total(jnp.concatenate(parts, axis=1), 1)


def _reduce_begin(tag, g):
    h = g.shape[1] // 2
    c = lax.axis_index("c")
    theirs = _sibling_send_half("reduce_cores_" + tag, g)
    mine = lax.dynamic_slice_in_dim(g, c * h, h, axis=1)
    return _add2("reduce_add_" + tag, mine.reshape(4 * h, PACK_W), theirs.reshape(4 * h, PACK_W), BF16).reshape(4, h, PACK_W)


def _reduce_end(tag, exchanged):
    return _halves_merge("reduce_merge_" + tag, _sum4("reduce_sum_" + tag, exchanged))


def _unpack_grads(total, w, entries, replicated):
    out, r0 = {}, 0
    for name, _, _, idx in entries:
        n = w[name][idx].size
        r = _rows_of(n)
        out[(name, idx)] = total[r0:r0 + r].reshape(-1)[:n].reshape(w[name].shape[1:])
        r0 += r
    for name in replicated:
        n = w[name].size
        r = _rows_of(n)
        out[name] = total[r0:r0 + r].reshape(-1)[:n].reshape(w[name].shape)
        r0 += r
    return out


def _pad_heads(a, n, w, to):
    lead = a.shape[:-1]
    return jnp.pad(a.reshape(*lead, n, w), [(0, 0)] * (len(lead) + 1) + [(0, to - w)]).reshape(*lead, n * to)


def _unpad_heads(a, n, w, to):
    lead = a.shape[:-1]
    return a.reshape(*lead, n, to)[..., :w].reshape(*lead, n * w)


def _w_in_pad(w):
    return jnp.concatenate([_pad_heads(w[:, 0:256], 4, 64, HP), _pad_heads(w[:, 256:512], 4, 64, HP), w[:, 512:1536],
                            w[:, 2240:2752], w[:, 1536:2176], _pad_heads(w[:, 2176:2240], 1, 64, HP)], axis=1)


def _w_in_unpad(g):
    return jnp.concatenate([_unpad_heads(g[:, Z_RQ:Z_RK], 4, 64, HP), _unpad_heads(g[:, Z_RK:Z_RV], 4, 64, HP),
                            g[:, Z_RV:Z_MG], g[:, Z_CQ:Z_KR], g[:, Z_KR:Z_KR + 64], g[:, Z_MG:Z_CQ]], axis=1)


def _layer_weights(l, full, w):
    f32 = lambda a: a.astype(F32)
    i = l // 2
    common = dict(bias=f32(w["ada_b"][l])[None], pre_g=f32(w["pre_g"][l])[None], post_g=f32(w["post_g"][l])[None])
    if l % 2 == 0:
        w_uq_p = _pad_heads(full[("ev_w_uq", i)], MLA_HEADS, MLA_NOPE_DIM + MLA_ROPE_DIM, QW)
        lg = jnp.stack([jax.nn.log_sigmoid(f32(w["ev_dec_f"][i])), jax.nn.log_sigmoid(f32(w["ev_dec_b"][i]))])
        return dict(common, w_in=_w_in_pad(full[("ev_w_in", i)]), w_uq=w_uq_p, w_ukv=full[("ev_w_ukv", i)],
                    w_out=full[("ev_w_out", i)], qg=f32(w["ev_q_norm_g"][i])[None], kvg=f32(w["ev_kv_norm_g"][i])[None], lg=lg)
    dw = f32(full[("od_dw_w", i)])
    return dict(common, w_in=full[("od_w_in", i)], w_out=full[("od_w_out", i)], b_in=f32(full[("od_b_in", i)])[None],
                dw=jnp.pad(dw, ((0, 1), (0, 0))), dw_flip=jnp.pad(dw[::-1], ((0, 1), (0, 0))),
                dw_b=f32(full[("od_dw_b", i)])[None], ln_g=f32(full[("od_ln_g", i)])[None], ln_b=f32(full[("od_ln_b", i)])[None])


def _rope_tables(positions):
    inv_freq = ROPE_BASE ** (-jnp.arange(0, ROPE_DIM, 2, dtype=F32) / ROPE_DIM)
    ang = positions.astype(F32)[:, None] * inv_freq
    z = jnp.zeros((ang.shape[0], HP - ROPE_DIM), F32)
    return jnp.concatenate([jnp.cos(ang), jnp.cos(ang), z], axis=1), jnp.concatenate([jnp.sin(ang), jnp.sin(ang), z], axis=1)


def _even_fwd(x, mod, cos, sin, w, side=None):
    bias, pre_g, post_g = w["bias"], w["pre_g"], w["post_g"]
    h, z = _pre_mm("ev_in", x, pre_g, mod, bias, w["w_in"])
    rq, rk, rv, qn, kvn, Q, K, V = _ev_mid_fwd(z, cos, sin, w["qg"], w["kvg"], w["w_uq"], w["w_ukv"])
    a, lse, *got = _flash_fwd(Q, K, V, side)
    o2, states = _ret_fwd(w["lg"], rq, rk, rv)
    mix, y, x_new = _ev_out_fwd(o2, z, a, x, w["w_out"], post_g, mod, bias)
    saved = dict(x=x, h=h, z=z, rq=rq, rk=rk, rv=rv, qn=qn, kvn=kvn, Q=Q, K=K, V=V, a=a, lse=lse, o2=o2,
                 states=states, mix=mix, y=y)
    return x_new, saved, (got[0] if got else None)


def _even_bwd(dx, s, mod, cos, sin, w, side=None):
    bias, pre_g, post_g = w["bias"], w["pre_g"], w["post_g"]
    dy, do, drg, da, dmg, dpost_g, dgate = _ev_out_bwd(s["y"], dx, s["o2"], s["z"], s["a"], w["w_out"], post_g, mod, bias)
    dw_out = _mm_tn("ev_out_dw", s["mix"], dy, BF16)
    drq, drk, drv, dlg = _ret_bwd(w["lg"], s["rq"], s["rk"], s["rv"], do, s["states"])
    dQ, dK, dV, *got = _flash_bwd(s["Q"], s["K"], s["V"], da, s["a"], s["lse"], side)
    dqf, dkv, dqn, dkvn, dkr = _ev_mid_bwd(dQ, dK, dV, cos, sin, w["w_uq"], w["w_ukv"])
    dw_uq = _mm_tn("ev_uq_dw", s["qn"], dqf, BF16)
    dw_ukv = _mm_tn("ev_ukv_dw", s["kvn"], dkv, BF16)
    dz, dx_new, dqg, dkvg, dpre_g, dscale, dshift = _ev_in_bwd(
        s["z"], cos, sin, w["qg"], w["kvg"], drq, drk, drv, drg, dqn, dkvn, dkr, dmg, s["x"], dx, w["w_in"], pre_g, mod, bias)
    dw_in = _mm_tn("ev_in_dw", s["h"], dz, BF16)
    g = dict(ev_w_in=_w_in_unpad(dw_in), ev_w_uq=_unpad_heads(dw_uq, MLA_HEADS, MLA_NOPE_DIM + MLA_ROPE_DIM, QW),
             ev_w_ukv=dw_ukv, ev_w_out=dw_out, ev_q_norm_g=dqg[0], ev_kv_norm_g=dkvg[0],
             dlg_f=dlg[0, :RET_HEADS, 0], dlg_b=dlg[1, :RET_HEADS, 0])
    return dx_new, g, dpre_g[0], dpost_g[0], jnp.concatenate([dshift, dscale, dgate], axis=1), (got[0] if got else None)


def _odd_fwd(x, mod, w, tgt=None):
    bias, pre_g, post_g = w["bias"], w["pre_g"], w["post_g"]
    h, z, u = _od_in_fwd(x, pre_g, mod, bias, w["w_in"], w["b_in"])
    uc = _conv_fwd(u, w["dw"], w["dw_b"])
    vv, y, *rest = _od_out_fwd(uc, z, x, w["w_out"], w["b_in"], w["ln_g"], w["ln_b"], post_g, mod, bias, tgt)
    return (rest[0] if tgt is None else tuple(rest)), dict(x=x, h=h, z=z, u=u, uc=uc, vv=vv, y=y)


def _odd_bwd(dx, s, mod, w):
    bias, pre_g, post_g = w["bias"], w["pre_g"], w["post_g"]
    dy, duc, dg, dpost_g, dgate, dbg, dln_g, dln_b = _od_out_bwd(
        s["y"], dx, s["uc"], s["z"], w["w_out"], w["b_in"], w["ln_g"], w["ln_b"], post_g, mod, bias)
    dw_out = _mm_tn("od_out_dw", s["vv"], dy, BF16)
    du = _conv_fwd(duc, w["dw_flip"], jnp.zeros_like(w["dw_b"]))
    dwb = _conv_bwd_w(s["u"], duc)
    dz, dx_new, dba, dbb, dpre_g, dscale, dshift = _od_in_bwd(s["z"], w["b_in"], du, dg, s["x"], dx, w["w_in"], pre_g, mod, bias)
    dw_in = _mm_tn("od_in_dw", s["h"], dz, BF16)
    g = dict(od_w_in=dw_in, od_b_in=jnp.concatenate([dba, dbb, dbg], axis=1)[0], od_dw_w=dwb[:CONV_KERNEL], od_dw_b=dwb[CONV_KERNEL],
             od_ln_g=dln_g[0], od_ln_b=dln_b[0], od_w_out=dw_out)
    return dx_new, g, dpre_g[0], dpost_g[0], jnp.concatenate([dshift, dscale, dgate], axis=1)


LATE_LAYERS = tuple(range(1, DEPTH))


def _step(x, mod, positions, loss_target, w):
    cos, sin = _rope_tables(positions)
    f32 = lambda a: a.astype(F32)
    mods = [mod[l:l + 1] for l in range(DEPTH)]
    first, late = _layer_entries([0]), _layer_entries(LATE_LAYERS)

    gathered = _halves_merge("gather_merge_first", _chip_exchange("gather_chips_first", _pack_weights(w, first), "half"))
    full = _unpack_weights(gathered, w, first)
    lw = {0: _layer_weights(0, full, w)}
    x, s0, got = _even_fwd(x, mods[0], cos, sin, lw[0], side=(_pack_weights(w, late), "half"))
    full.update(_unpack_weights(_halves_merge("gather_merge_late", got), w, late))
    saved = [s0]
    assert DEPTH % 2 == 0
    for l in LATE_LAYERS:
        lw[l] = _layer_weights(l, full, w)
        if l % 2 == 0:
            x, s, _ = _even_fwd(x, mods[l], cos, sin, lw[l])
        else:
            x, s = _odd_fwd(x, mods[l], lw[l], loss_target if l == DEPTH - 1 else None)
        saved.append(s)
    dx, loss = x

    grads = {}
    dpre, dpost, dmod = [None] * DEPTH, [None] * DEPTH, [None] * DEPTH
    for l in reversed(LATE_LAYERS):
        if l % 2 == 0:
            dx, g, dpre[l], dpost[l], dmod[l], _ = _even_bwd(dx, saved[l], mods[l], cos, sin, lw[l])
        else:
            dx, g, dpre[l], dpost[l], dmod[l] = _odd_bwd(dx, saved[l], mods[l], lw[l])
        grads.update({(name, l // 2): val for name, val in g.items()})
    chip_sum = _reduce_begin("late", _pack_grads(grads, late, ()))
    dx, g, dpre[0], dpost[0], dmod[0], got = _even_bwd(dx, saved[0], mods[0], cos, sin, lw[0], side=(chip_sum, "scatter"))
    grads.update({(name, 0): val for name, val in g.items()})
    red = _unpack_grads(_reduce_end("late", got), w, late, ())

    n_ev = (DEPTH + 1) // 2
    grads.update(pre_g=jnp.stack(dpre), post_g=jnp.stack(dpost), ada_b=jnp.concatenate(dmod, axis=0))
    for name in ("ev_q_norm_g", "ev_kv_norm_g"):
        grads[name] = jnp.stack([grads[(name, i)] for i in range(n_ev)])
    grads["ev_dec_f"] = jnp.stack([grads[("dlg_f", i)] for i in range(n_ev)]) * jax.nn.sigmoid(-f32(w["ev_dec_f"]))
    grads["ev_dec_b"] = jnp.stack([grads[("dlg_b", i)] for i in range(n_ev)]) * jax.nn.sigmoid(-f32(w["ev_dec_b"]))
    chip_sum = _reduce_begin("first", _pack_grads(grads, first, REPLICATED))
    total = _reduce_end("first", _chip_exchange("reduce_chips_first", chip_sum, "scatter"))
    red.update(_unpack_grads(total, w, first, REPLICATED))

    out = {name: red[name] for name in REPLICATED}
    for name, _, _ in EV_SHARDED:
        out[name] = jnp.stack([red[(name, i)] for i in range(n_ev)])
    for name, _, _ in OD_SHARDED:
        out[name] = jnp.stack([red[(name, i)] for i in range(DEPTH // 2)])
    return loss[0, 0], dx, out, grads["ada_b"]


ADA_ROWS = 16
ADA_COLS = 3 * D_MODEL // 4
ADA_PACK = (8, 1024)


def _ada_pack(a):
    n = ADA_PACK[0] * ADA_PACK[1]
    return jnp.pad(a, ((0, 0), (0, n - a.shape[1]))).reshape(4, *ADA_PACK)


def _ada_unpack(a):
    return a.reshape(4, -1)[:, :DEPTH * ADA_COLS]


def _ada_forward(c, ada_w):
    D = D_MODEL
    c16 = jnp.zeros((ADA_ROWS, D), F32).at[0].set(c[0])
    c_act = _rowcall("silu_c", lambda rows, p: ([_silu(rows[0][...])], []), [c16], [], [(D, BF16)])[0]
    got = _chip_exchange("ada_c", c_act, "whole")
    c4 = jnp.pad(got[:, 0, :], ((0, ADA_ROWS - 4), (0, 0)))
    per_layer = [_mm_nn("ada", c4, ada_w[l].astype(BF16))[:4] for l in range(DEPTH)]
    p = jnp.stack(per_layer, axis=1).reshape(4, DEPTH * ADA_COLS)
    got = _chip_exchange("ada_mod", _ada_pack(p), "scatter")
    mod = _ada_unpack(got).reshape(4, DEPTH, ADA_COLS).transpose(1, 0, 2).reshape(DEPTH, 3 * D)
    return mod, c4


def _ada_backward(dmod, c4):
    p = dmod.reshape(DEPTH, 4, ADA_COLS).transpose(1, 0, 2).reshape(4, DEPTH * ADA_COLS)
    got = _chip_exchange("ada_dmod", _ada_pack(p), "scatter")
    dm4 = jnp.pad(_ada_unpack(got), ((0, ADA_ROWS - 4), (0, 0)))
    both = _halves_merge("ada_merge", jnp.concatenate([dm4, c4.astype(F32)], axis=1))
    dm32, c32 = both[:, :DEPTH * ADA_COLS].astype(BF16), both[:, DEPTH * ADA_COLS:].astype(BF16)
    return jnp.stack([_mm_tn("ada_dw", c32, dm32[:, l * ADA_COLS:(l + 1) * ADA_COLS]) for l in range(DEPTH)])


def kernel(x, c, positions, ada_w, ada_b, pre_g, post_g, ev_w_in, ev_dec_f, ev_dec_b, ev_q_norm_g, ev_w_uq, ev_kv_norm_g, ev_w_ukv, ev_w_out, od_w_in, od_b_in, od_dw_w, od_dw_b, od_ln_g, od_ln_b, od_w_out, loss_target, m_ada_w, m_ada_b, m_pre_g, m_post_g, m_ev_w_in, m_ev_dec_f, m_ev_dec_b, m_ev_q_norm_g, m_ev_w_uq, m_ev_kv_norm_g, m_ev_w_ukv, m_ev_w_out, m_od_w_in, m_od_b_in, m_od_dw_w, m_od_dw_b, m_od_ln_g, m_od_ln_b, m_od_w_out, v_ada_w, v_ada_b, v_pre_g, v_post_g, v_ev_w_in, v_ev_dec_f, v_ev_dec_b, v_ev_q_norm_g, v_ev_w_uq, v_ev_kv_norm_g, v_ev_w_ukv, v_ev_w_out, v_od_w_in, v_od_b_in, v_od_dw_w, v_od_dw_b, v_od_ln_g, v_od_ln_b, v_od_w_out):
    w = dict(ada_w=ada_w, ada_b=ada_b, pre_g=pre_g, post_g=post_g, ev_w_in=ev_w_in, ev_dec_f=ev_dec_f, ev_dec_b=ev_dec_b,
             ev_q_norm_g=ev_q_norm_g, ev_w_uq=ev_w_uq, ev_kv_norm_g=ev_kv_norm_g, ev_w_ukv=ev_w_ukv, ev_w_out=ev_w_out,
             od_w_in=od_w_in, od_b_in=od_b_in, od_dw_w=od_dw_w, od_dw_b=od_dw_b, od_ln_g=od_ln_g, od_ln_b=od_ln_b, od_w_out=od_w_out)
    m = dict(ada_w=m_ada_w, ada_b=m_ada_b, pre_g=m_pre_g, post_g=m_post_g, ev_w_in=m_ev_w_in, ev_dec_f=m_ev_dec_f, ev_dec_b=m_ev_dec_b,
             ev_q_norm_g=m_ev_q_norm_g, ev_w_uq=m_ev_w_uq, ev_kv_norm_g=m_ev_kv_norm_g, ev_w_ukv=m_ev_w_ukv, ev_w_out=m_ev_w_out,
             od_w_in=m_od_w_in, od_b_in=m_od_b_in, od_dw_w=m_od_dw_w, od_dw_b=m_od_dw_b, od_ln_g=m_od_ln_g, od_ln_b=m_od_ln_b, od_w_out=m_od_w_out)
    v = dict(ada_w=v_ada_w, ada_b=v_ada_b, pre_g=v_pre_g, post_g=v_post_g, ev_w_in=v_ev_w_in, ev_dec_f=v_ev_dec_f, ev_dec_b=v_ev_dec_b,
             ev_q_norm_g=v_ev_q_norm_g, ev_w_uq=v_ev_w_uq, ev_kv_norm_g=v_ev_kv_norm_g, ev_w_ukv=v_ev_w_ukv, ev_w_out=v_ev_w_out,
             od_w_in=v_od_w_in, od_b_in=v_od_b_in, od_dw_w=v_od_dw_w, od_dw_b=v_od_dw_b, od_ln_g=v_od_ln_g, od_ln_b=v_od_ln_b, od_w_out=v_od_w_out)

    mod, c4 = _ada_forward(c, ada_w)
    loss_local, grad_x, g, dmod = _step(x[0], mod, positions[0], loss_target[0], w)
    loss = lax.psum(loss_local, ("x", "y", "c"))
    g["ada_w"] = _ada_backward(dmod, c4)
    delta, new_m, new_v = {}, {}, {}
    for name in WEIGHTS:
        delta[name], new_m[name], new_v[name] = _adamw(w[name], g[name], m[name], v[name])
    return (loss, grad_x[None], *[g[n] for n in WEIGHTS], *[delta[n] for n in WEIGHTS],
            *[new_m[n] for n in WEIGHTS], *[new_v[n] for n in WEIGHTS])
```

```python
import functools

import jax
import jax.numpy as jnp
from jax import lax
from jax.experimental import pallas as pl
from jax.experimental.pallas import tpu as pltpu

F32 = jnp.float32
BF16 = jnp.bfloat16

D_MODEL = 1024
DEPTH = 4
RET_HEADS = 4
RET_QK_DIM = 64
RET_V_DIM = 128
MLA_HEADS = 4
MLA_Q_RANK = 384
MLA_KV_RANK = 256
MLA_NOPE_DIM = 128
MLA_ROPE_DIM = 64
MLA_V_DIM = 128
ROPE_DIM = 64
ROPE_BASE = 10000.0
CONV_KERNEL = 31
EPS = 1e-6
ADAM_LR, ADAM_B1, ADAM_B2, ADAM_EPS, ADAM_WD, ADAM_STEP = 0.001, 0.9, 0.999, 1e-08, 0.01, 10

LANES = 128
HP = 128
Z_RQ, Z_RK, Z_RV, Z_RG, Z_MG, Z_CQ, Z_CKV, Z_KR, Z_END = 0, 512, 1024, 1536, 2048, 2560, 2944, 3200, 3328
Z_LAT = Z_END - Z_CQ


def _z_ret(z):
    return (z, Z_RG, 0)


def _z_gates(z):
    return [(z, 512, Z_RG // 512), (z, 512, Z_MG // 512)]


def _z_latents(z):
    return [(z, 256, Z_CQ // 256 + i) for i in range(Z_LAT // 256)]
V7X_VMEM_BYTES = 64 * 1024 * 1024
VMEM_CAP = V7X_VMEM_BYTES - 8 * 1024 * 1024

MESH = pl.DeviceIdType.MESH


def _nbytes(shape, dtype):
    n = 1
    for s in shape:
        n *= s
    return n * jnp.dtype(dtype).itemsize


def _hbm(x):
    return pltpu.with_memory_space_constraint(x, pltpu.HBM)


def _params(sem, block_bytes):
    limit = min(VMEM_CAP, max(32 * 1024 * 1024, 2 * block_bytes + 16 * 1024 * 1024))
    return pltpu.CompilerParams(dimension_semantics=sem, vmem_limit_bytes=limit)


ROW_TILE = 512


def _rowcall(name, fn, rows, params, outs, accs=(), ts=ROW_TILE, mats=()):
    rows = [r if isinstance(r, tuple) else (r, r.shape[1], 0) for r in rows]
    S = rows[0][0].shape[0]
    ts = min(ts, S)
    assert S % ts == 0, (name, S, ts)
    nr, npar, nm, no, na = len(rows), len(params), len(mats), len(outs), len(accs)

    def body(*refs):
        row_refs = refs[:nr]
        pvals = [r[...] for r in refs[nr:nr + npar]]
        mat_refs = refs[nr + npar:nr + npar + nm]
        out_refs = refs[nr + npar + nm:nr + npar + nm + no]
        acc_refs = refs[nr + npar + nm + no:]
        ovals, avals = fn(row_refs, pvals, mat_refs) if nm else fn(row_refs, pvals)
        for r, v in zip(out_refs, ovals, strict=True):
            r[...] = v.astype(r.dtype)
        if na:
            @pl.when(pl.program_id(0) == 0)
            def _():
                for r in acc_refs:
                    r[...] = jnp.zeros_like(r)
            for r, v in zip(acc_refs, avals, strict=True):
                r[...] += v

    in_specs = [pl.BlockSpec((ts, w), functools.partial(lambda i, b: (i, b), b=blk)) for _, w, blk in rows]
    in_specs += [pl.BlockSpec(p.shape, lambda i: (0, 0)) for p in [*params, *mats]]
    out_specs = [pl.BlockSpec((ts, w), lambda i: (i, 0)) for w, _ in outs]
    out_specs += [pl.BlockSpec(s, lambda i: (0, 0)) for s in accs]
    out_shape = [jax.ShapeDtypeStruct((S, w), dt) for w, dt in outs]
    out_shape += [jax.ShapeDtypeStruct(s, F32) for s in accs]
    nb = sum(_nbytes((ts, w), a.dtype) for a, w, _ in rows) + sum(_nbytes((ts, w), dt) for w, dt in outs)
    nb += sum(_nbytes(p.shape, p.dtype) for p in params) + sum(_nbytes(s, F32) for s in accs)
    res = pl.pallas_call(
        body, name=name, grid=(S // ts,), in_specs=in_specs, out_specs=out_specs, out_shape=out_shape,
        compiler_params=_params(("arbitrary",) if na else ("parallel",), 3 * nb + sum(_nbytes(m.shape, m.dtype) for m in mats)),
    )(*[_hbm(a) for a, _, _ in rows], *params, *[_hbm(m) for m in mats])
    return res


def _silu(x):
    return x * jax.nn.sigmoid(x)


def _rms(x, g):
    return x * lax.rsqrt(jnp.mean(x * x, axis=-1, keepdims=True) + EPS) * g


def _rot(x):
    lane = lax.broadcasted_iota(jnp.int32, x.shape, 1)
    return jnp.where(lane < ROPE_DIM // 2, -pltpu.roll(x, LANES - ROPE_DIM // 2, 1), pltpu.roll(x, ROPE_DIM // 2, 1))


def _rope(x, cos, sin):
    return x * cos + _rot(x) * sin


def _rope_t(dy, cos, sin):
    return dy * cos - _rot(dy) * sin


def _groups(ref, start, n):
    return [ref[:, start + HP * h:start + HP * (h + 1)] for h in range(n)]


def _pre_math(x, g, m_scale, b_scale, m_shift, b_shift):
    return _rms(x, g) * (1.0 + (m_scale + b_scale)) + (m_shift + b_shift)


def _post_math(x, y, g, m_gate, b_gate):
    return x + (m_gate + b_gate) * _rms(y, g)


def _ev_mid_fwd(z, cos, sin, qg, kvg, w_uq, w_ukv):
    qscale = ATT_SCALE * LOG2E

    def fn(rows, p, mats):
        z_, l0, l1, l2, c_, s_ = rows
        qg_, kvg_ = p
        cos_, sin_ = c_[...], s_[...]
        lat = jnp.concatenate([l0[...], l1[...], l2[...]], axis=1)
        rq = jnp.concatenate([_rope(v[...], cos_, sin_) for v in _groups(z_, Z_RQ, RET_HEADS)], axis=1)
        rk = jnp.concatenate([_rope(v[...], cos_, sin_) for v in _groups(z_, Z_RK, RET_HEADS)], axis=1)
        rk = rk * (RET_QK_DIM ** -0.5)
        rv = z_[:, Z_RV:Z_RG]
        qn = _rms(lat[:, :Z_CKV - Z_CQ], qg_).astype(BF16)
        kvn = _rms(lat[:, Z_CKV - Z_CQ:Z_KR - Z_CQ], kvg_).astype(BF16)
        kr = _rope(lat[:, Z_KR - Z_CQ:], cos_, sin_)
        qf = jnp.dot(qn, mats[0][...], preferred_element_type=F32)
        kv = jnp.dot(kvn, mats[1][...], preferred_element_type=F32)
        qs, ks, vs = [], [], []
        for h in range(MLA_HEADS):
            b = 2 * HP * h
            qs += [qf[:, b:b + HP] * qscale, _rope(qf[:, b + HP:b + 2 * HP], cos_, sin_) * qscale]
            ks += [kv[:, b:b + HP], kr]
            vs += [kv[:, b + HP:b + 2 * HP]]
        return [rq, rk, rv, qn, kvn, jnp.concatenate(qs, axis=1), jnp.concatenate(ks, axis=1), jnp.concatenate(vs, axis=1)], []

    return _rowcall("ev_mid_fwd", fn, [_z_ret(z), *_z_latents(z), cos, sin], [qg, kvg],
                    [(512, BF16), (512, BF16), (512, BF16), (MLA_Q_RANK, BF16), (MLA_KV_RANK, BF16),
                     (1024, BF16), (1024, BF16), (512, BF16)], mats=[w_uq, w_ukv])


def _pre_bwd_math(dh, x, res, g, mod, bias):
    D = D_MODEL
    _, vjp = jax.vjp(_pre_math, x, g, mod[:, D:2 * D], bias[:, D:2 * D], mod[:, :D], bias[:, :D])
    dx, dg, dsc, _, dsh, _ = vjp(dh)
    return dx + res, dg, dsc, dsh


def _ev_in_bwd(z, cos, sin, qg, kvg, drq, drk, drv, drg, dqn, dkvn, dkr, dmg, x, dx_res, w_in, g, mod, bias):
    D = D_MODEL

    def fn(rows, p, mats):
        l0, l1, l2, c_, s_, drq_, drk_, drv_, drg_, dqn_, dkvn_, dkr_, dmg_, x_, res_ = rows
        qg_, kvg_, g_, mod_, b_ = p
        cos_, sin_ = c_[...], s_[...]
        lat = jnp.concatenate([l0[...], l1[...], l2[...]], axis=1)
        parts = []
        for h in range(RET_HEADS):
            t = drq_[:, HP * h:HP * (h + 1)].astype(F32) + drq_[:, 512 + HP * h:512 + HP * (h + 1)].astype(F32)
            parts.append(_rope_t(t, cos_, sin_))
        for h in range(RET_HEADS):
            t = drk_[:, HP * h:HP * (h + 1)].astype(F32) + drk_[:, 512 + HP * h:512 + HP * (h + 1)].astype(F32)
            parts.append(_rope_t(t, cos_, sin_) * (RET_QK_DIM ** -0.5))
        parts.append(drv_[:, :512].astype(F32) + drv_[:, 512:].astype(F32))
        parts += [drg_[...], dmg_[...]]
        _, vq = jax.vjp(_rms, lat[:, :Z_CKV - Z_CQ], qg_)
        dcq, dqg = vq(dqn_[...])
        _, vkv = jax.vjp(_rms, lat[:, Z_CKV - Z_CQ:Z_KR - Z_CQ], kvg_)
        dckv, dkvg = vkv(dkvn_[...])
        parts += [dcq, dckv, _rope_t(dkr_[...], cos_, sin_)]
        dz = jnp.concatenate([v.astype(BF16) for v in parts], axis=1)
        dh = lax.dot_general(dz, mats[0][...], NT_DIMS, preferred_element_type=F32)
        dx, dg, dsc, dsh = _pre_bwd_math(dh, x_[...], res_[...], g_, mod_, b_)
        return [dz, dx], [dqg, dkvg, dg, dsc, dsh]

    rows = [*_z_latents(z), cos, sin, drq, drk, drv, drg, dqn, dkvn, dkr, dmg, x, dx_res]
    return _rowcall("ev_in_bwd", fn, rows, [qg, kvg, g, mod, bias], [(Z_END, BF16), (D, F32)],
                    [(1, MLA_Q_RANK), (1, MLA_KV_RANK)] + [(1, D)] * 3, ts=256, mats=[w_in])


ATT_SCALE = (MLA_NOPE_DIM + MLA_ROPE_DIM) ** -0.5
LOG2E = 1.4426950408889634
LN2 = 0.6931471805599453


def _ev_mid_bwd(dQ, dK, dV, cos, sin, w_uq, w_ukv):
    def fn(rows, p, mats):
        dq_, dk_, dv_, c_, s_ = rows
        cos_, sin_ = c_[...], s_[...]
        dqs, dkvs = [], []
        dkr = None
        for h in range(MLA_HEADS):
            b = 2 * HP * h
            dqs += [dq_[:, b:b + HP] * ATT_SCALE, _rope_t(dq_[:, b + HP:b + 2 * HP], cos_, sin_) * ATT_SCALE]
            dkvs += [dk_[:, b:b + HP] * LN2, dv_[:, HP * h:HP * (h + 1)]]
            t = dk_[:, b + HP:b + 2 * HP]
            dkr = t if dkr is None else dkr + t
        dqf = jnp.concatenate(dqs, axis=1).astype(BF16)
        dkv = jnp.concatenate(dkvs, axis=1).astype(BF16)
        dqn = lax.dot_general(dqf, mats[0][...], NT_DIMS, preferred_element_type=F32)
        dkvn = lax.dot_general(dkv, mats[1][...], NT_DIMS, preferred_element_type=F32)
        return [dqf, dkv, dqn, dkvn, dkr * LN2], []

    return _rowcall("ev_mid_bwd", fn, [dQ, dK, dV, cos, sin], [],
                    [(1024, BF16), (1024, BF16), (MLA_Q_RANK, F32), (MLA_KV_RANK, F32), (HP, F32)], mats=[w_uq, w_ukv])


def _mix_math(o, rg, a, mg):
    outs = []
    for h in range(RET_HEADS):
        oh = o[:, HP * h:HP * (h + 1)]
        mu = jnp.mean(oh, axis=-1, keepdims=True)
        var = jnp.mean(jnp.square(oh - mu), axis=-1, keepdims=True)
        outs.append((oh - mu) * lax.rsqrt(var + EPS))
    ret = jnp.concatenate(outs, axis=1) * _silu(rg)
    return jnp.concatenate([ret, a * _silu(mg)], axis=1)


def _ev_out_fwd(o2, z, a, x, w_out, g, mod, bias):
    D = D_MODEL

    def fn(rows, p, mats):
        o_, rg_, mg_, a_, x_ = rows
        g_, mod_, b_ = p
        mix = _mix_math(o_[:, :512] + o_[:, 512:], rg_[...], a_[...], mg_[...]).astype(BF16)
        y = jnp.dot(mix, mats[0][...], preferred_element_type=F32)
        return [mix, y, _post_math(x_[...], y, g_, mod_[:, 2 * D:], b_[:, 2 * D:])], []

    return _rowcall("ev_out", fn, [o2, *_z_gates(z), a, x], [g, mod, bias], [(1024, BF16), (D, F32), (D, F32)], mats=[w_out])


def _post_bwd_math(y, dxo, g, mod, bias):
    D = D_MODEL
    b_gate = bias[:, 2 * D:]
    _, vjp = jax.vjp(lambda y_, g2, mg: _post_math(0.0, y_, g2, mg, b_gate), y, g, mod[:, 2 * D:])
    return vjp(dxo)


def _ev_out_bwd(y, dxo, o2, z, a, w_out, g, mod, bias):
    D = D_MODEL

    def fn(rows, p, mats):
        y_, dxo_, o_, rg_, mg_, a_ = rows
        g_, mod_, b_ = p
        dy, dg, dgate = _post_bwd_math(y_[...], dxo_[...], g_, mod_, b_)
        dyb = dy.astype(BF16)
        dmix = lax.dot_general(dyb, mats[0][...], NT_DIMS, preferred_element_type=F32)
        _, vjp = jax.vjp(_mix_math, o_[:, :512] + o_[:, 512:], rg_[...], a_[...], mg_[...])
        return [dyb, *vjp(dmix)], [dg, dgate]

    return _rowcall("ev_out_bwd", fn, [y, dxo, o2, *_z_gates(z), a], [g, mod, bias],
                    [(D, BF16), (512, BF16), (512, F32), (512, BF16), (512, F32)], [(1, D)] * 2, mats=[w_out])


def _glu_math(a, b, ba, bb):
    return (a + ba) * jax.nn.sigmoid(b + bb)


def _od_in_fwd(x, g, mod, bias, w_in, bin_):
    D = D_MODEL

    def fn(rows, p, mats):
        g_, mod_, b_, bi_ = p
        h = _pre_math(rows[0][...], g_, mod_[:, D:2 * D], b_[:, D:2 * D], mod_[:, :D], b_[:, :D]).astype(BF16)
        z = jnp.dot(h, mats[0][...], preferred_element_type=F32)
        return [h, z, _glu_math(z[:, :D], z[:, D:2 * D], bi_[:, :D], bi_[:, D:2 * D])], []

    return _rowcall("od_in", fn, [x], [g, mod, bias, bin_], [(D, BF16), (3 * D, F32), (D, F32)], mats=[w_in])


def _odmix_math(uc, g, bg, ln_g, ln_b):
    mu = jnp.mean(uc, axis=-1, keepdims=True)
    var = jnp.mean(jnp.square(uc - mu), axis=-1, keepdims=True)
    y = (uc - mu) * lax.rsqrt(var + EPS) * ln_g + ln_b
    return _silu(y) * _silu(g + bg)


def _od_out_fwd(uc, z, x, w_out, bin_, ln_g, ln_b, g, mod, bias, tgt=None):
    D = D_MODEL

    def fn(rows, p, mats):
        uc_, zg_, x_ = rows[:3]
        bi_, lg_, lb_, g_, mod_, b_ = p
        vv = _odmix_math(uc_[...], zg_[...], bi_[:, 2 * D:], lg_, lb_).astype(BF16)
        y = jnp.dot(vv, mats[0][...], preferred_element_type=F32)
        xn = _post_math(x_[...], y, g_, mod_[:, 2 * D:], b_[:, 2 * D:])
        if tgt is None:
            return [vv, y, xn], []
        err = xn - rows[3][...]
        part = 0.5 * jnp.sum(jnp.mean(err * err, axis=-1, keepdims=True), axis=0, keepdims=True)
        return [vv, y, err * (1.0 / D)], [jnp.broadcast_to(part, (1, LANES))]

    rows = [uc, (z, D, 2), x] + ([] if tgt is None else [tgt])
    return _rowcall("od_out" if tgt is None else "od_out_loss", fn, rows, [bin_, ln_g, ln_b, g, mod, bias],
                    [(D, BF16), (D, F32), (D, F32)], [] if tgt is None else [(1, LANES)], mats=[w_out])


def _od_out_bwd(y, dxo, uc, z, w_out, bin_, ln_g, ln_b, g, mod, bias):
    D = D_MODEL

    def fn(rows, p, mats):
        y_, dxo_, uc_, zg_ = rows
        bi_, lg_, lb_, g_, mod_, b_ = p
        dy, dpg, dgate = _post_bwd_math(y_[...], dxo_[...], g_, mod_, b_)
        dyb = dy.astype(BF16)
        dvv = lax.dot_general(dyb, mats[0][...], NT_DIMS, preferred_element_type=F32)
        _, vjp = jax.vjp(_odmix_math, uc_[...], zg_[...], bi_[:, 2 * D:], lg_, lb_)
        duc, dg, dbg, dlg, dlb = vjp(dvv)
        return [dyb, duc, dg], [dpg, dgate, dbg, dlg, dlb]

    return _rowcall("od_out_bwd", fn, [y, dxo, uc, (z, D, 2)], [bin_, ln_g, ln_b, g, mod, bias],
                    [(D, BF16), (D, F32), (D, BF16)], [(1, D)] * 5, mats=[w_out])


def _od_in_bwd(z, bin_, du, dg, x, dx_res, w_in, g, mod, bias):
    D = D_MODEL

    def fn(rows, p, mats):
        za_, zb_, du_, dg_, x_, res_ = rows
        bi_, g_, mod_, b_ = p
        _, vjp = jax.vjp(_glu_math, za_[...], zb_[...], bi_[:, :D], bi_[:, D:2 * D])
        da, db, dba, dbb = vjp(du_[...])
        dz = jnp.concatenate([da.astype(BF16), db.astype(BF16), dg_[...]], axis=1)
        dh = lax.dot_general(dz, mats[0][...], NT_DIMS, preferred_element_type=F32)
        dx, dpg, dsc, dsh = _pre_bwd_math(dh, x_[...], res_[...], g_, mod_, b_)
        return [dz, dx], [dba, dbb, dpg, dsc, dsh]

    return _rowcall("od_in_bwd", fn, [(z, D, 0), (z, D, 1), du, dg, x, dx_res], [bin_, g, mod, bias],
                    [(3 * D, BF16), (D, F32)], [(1, D)] * 5, ts=256, mats=[w_in])


def _tile(n, cap):
    if n <= cap:
        return n
    best = None
    for t in range(LANES, cap + 1, LANES):
        if n % t == 0:
            best = t
    assert best is not None, (n, cap)
    return best


MM_TN_CAP = 1792
NT_DIMS = (((1,), (1,)), ((), ()))
TN_DIMS = (((0,), (0,)), ((), ()))


def _mm_nn(name, a, b, out_dtype=F32, tm=512, tn_cap=MM_TN_CAP):
    M, K = a.shape
    N = b.shape[1]
    tm = min(tm, M)
    tn = _tile(N, tn_cap)

    def body(a_ref, b_ref, o_ref):
        o_ref[...] = jnp.dot(a_ref[...], b_ref[...], preferred_element_type=F32).astype(o_ref.dtype)

    nb = _nbytes((tm, K), a.dtype) + _nbytes((K, tn), b.dtype) + _nbytes((tm, tn), out_dtype) + _nbytes((tm, tn), F32)
    return pl.pallas_call(
        body, name=name, grid=(N // tn, M // tm),
        in_specs=[pl.BlockSpec((tm, K), lambda j, i: (i, 0)), pl.BlockSpec((K, tn), lambda j, i: (0, j))],
        out_specs=pl.BlockSpec((tm, tn), lambda j, i: (i, j)),
        out_shape=jax.ShapeDtypeStruct((M, N), out_dtype),
        compiler_params=_params(("parallel", "parallel"), nb),
    )(_hbm(a), _hbm(b))


def _pre_mm(name, x, g, mod, bias, w, tm=512):
    S, D = x.shape
    N = w.shape[1]
    tm = min(tm, S)

    def body(x_ref, g_ref, mod_ref, bias_ref, w_ref, h_ref, z_ref):
        mod_, b_ = mod_ref[...], bias_ref[...]
        h = _pre_math(x_ref[...], g_ref[...], mod_[:, D:2 * D], b_[:, D:2 * D], mod_[:, :D], b_[:, :D]).astype(BF16)
        h_ref[...] = h
        z_ref[...] = jnp.dot(h, w_ref[...], preferred_element_type=F32)

    nb = 2 * _nbytes((tm, D), F32) + _nbytes((D, N), w.dtype) + 2 * _nbytes((tm, N), F32)
    row = lambda i: (i, 0)
    whole = lambda i: (0, 0)
    return pl.pallas_call(
        body, name=name, grid=(S // tm,),
        in_specs=[pl.BlockSpec((tm, D), row), pl.BlockSpec(g.shape, whole), pl.BlockSpec(mod.shape, whole),
                  pl.BlockSpec(bias.shape, whole), pl.BlockSpec((D, N), whole)],
        out_specs=[pl.BlockSpec((tm, D), row), pl.BlockSpec((tm, N), row)],
        out_shape=[jax.ShapeDtypeStruct((S, D), BF16), jax.ShapeDtypeStruct((S, N), F32)],
        compiler_params=_params(("parallel",), nb),
    )(_hbm(x), g, mod, bias, _hbm(w))


def _mm_tn(name, a, b, out_dtype=F32, ts=1024, tm_cap=1024, tn_cap=MM_TN_CAP):
    S, M = a.shape
    N = b.shape[1]
    ts = min(ts, S)
    tm = _tile(M, tm_cap)
    tn = _tile(N, tn_cap)
    ns = S // ts

    def body(a_ref, b_ref, o_ref, acc):
        s = pl.program_id(2)

        @pl.when(s == 0)
        def _():
            acc[...] = jnp.zeros(acc.shape, F32)

        acc[...] += lax.dot_general(a_ref[...], b_ref[...], TN_DIMS, preferred_element_type=F32)

        @pl.when(s == ns - 1)
        def _():
            o_ref[...] = acc[...].astype(o_ref.dtype)

    nb = _nbytes((ts, tm), a.dtype) + _nbytes((ts, tn), b.dtype) + 3 * _nbytes((tm, tn), F32)
    return pl.pallas_call(
        body, name=name, grid=(M // tm, N // tn, ns),
        in_specs=[pl.BlockSpec((ts, tm), lambda i, j, s: (s, i)), pl.BlockSpec((ts, tn), lambda i, j, s: (s, j))],
        out_specs=pl.BlockSpec((tm, tn), lambda i, j, s: (i, j)),
        out_shape=jax.ShapeDtypeStruct((M, N), out_dtype),
        scratch_shapes=[pltpu.VMEM((tm, tn), F32)],
        compiler_params=_params(("parallel", "parallel", "arbitrary"), nb),
    )(_hbm(a), _hbm(b))


ATT_TQ = 2048
ATT_TK = 1024
ATT_TK_FWD = 1024
ATT_TQ_FWD = 2048
QW = 2 * HP


class _Side:
    def __init__(self, side):
        self.on = side is not None
        if self.on:
            self.p, self.mode = side
            shp = self.p.shape
            self.h, self.w = (shp[0] // 2, shp[1]) if self.mode == "half" else (shp[-2], shp[-1])
        self.in_specs = [_HBM] if self.on else []
        self.out_specs = [_HBM] if self.on else []
        self.out_shape = [jax.ShapeDtypeStruct((4, self.h, self.w), self.p.dtype)] if self.on else []
        self.scratch = list(_CHIP_SEMS) if self.on else []
        self.operands = [_hbm(self.p)] if self.on else []

    def copies(self, p_ref, x_ref, sems):
        return _chip_copies(p_ref, x_ref, *sems, self.mode, self.h)

    def start(self, p_ref, x_ref, sems, first):
        @pl.when(first)
        def _():
            for cp in self.copies(p_ref, x_ref, sems):
                cp.start()

    def finish(self, p_ref, x_ref, sems, last):
        @pl.when(last)
        def _():
            for cp in self.copies(p_ref, x_ref, sems):
                cp.wait()


def _flash_fwd(Q, K, V, side=None):
    S = Q.shape[0]
    H = MLA_HEADS
    tq, tk = min(ATT_TQ_FWD, S), min(ATT_TK_FWD, S)
    nk = S // tk
    sd = _Side(side)

    def body(*refs):
        if sd.on:
            q_ref, k_ref, vt_ref, p_ref, o_ref, lse_ref, x_ref, m_s, l_s, acc_s, s_a, s_b, *sems = refs
            step_id = pl.program_id(0) * (S // tq) + pl.program_id(1)
            sd.start(p_ref, x_ref, sems, step_id == 0)
        else:
            q_ref, k_ref, vt_ref, o_ref, lse_ref, m_s, l_s, acc_s, s_a, s_b = refs
        m_s[...] = jnp.full(m_s.shape, -jnp.inf, F32)
        l_s[...] = jnp.zeros(l_s.shape, F32)
        acc_s[...] = jnp.zeros(acc_s.shape, F32)

        def scores(j):
            k = k_ref[pl.ds(pl.multiple_of(j * tk, tk), tk), :]
            return lax.dot_general(k, q_ref[...], NT_DIMS, preferred_element_type=F32)

        def consume(st, j):
            m_prev = m_s[...]
            m_new = jnp.maximum(m_prev, jnp.max(st, axis=0, keepdims=True))
            alpha = jnp.exp2(m_prev - m_new)
            pt = jnp.exp2(st - m_new)
            l_s[...] = alpha * l_s[...] + jnp.sum(pt, axis=0, keepdims=True)
            v = vt_ref[pl.ds(pl.multiple_of(j * tk, tk), tk), :]
            acc_s[...] = alpha * acc_s[...] + lax.dot_general(v, pt.astype(BF16), TN_DIMS, preferred_element_type=F32)
            m_s[...] = m_new

        if nk % 2:
            def step(j, carry):
                consume(scores(j), j)
                return carry

            lax.fori_loop(0, nk, step, 0)
        else:
            s_a[...] = scores(0)

            def pair(jj, carry):
                j0 = 2 * jj
                s_b[...] = scores(j0 + 1)
                consume(s_a[...], j0)
                s_a[...] = scores(jnp.minimum(j0 + 2, nk - 1))
                consume(s_b[...], j0 + 1)
                return carry

            lax.fori_loop(0, nk // 2, pair, 0)
        l = l_s[...]
        o_ref[...] = (acc_s[...] * (1.0 / l)).T
        lse_ref[0] = m_s[...] + jnp.log2(l)
        if sd.on:
            sd.finish(p_ref, x_ref, sems, step_id == H * (S // tq) - 1)

    nb = (_nbytes((tq, QW), BF16) + _nbytes((S, QW), BF16) + _nbytes((S, HP), BF16) + 3 * _nbytes((tq, HP), F32)
          + 4 * _nbytes((tq, tk), F32))
    return pl.pallas_call(
        body, name="flash_fwd_side" if sd.on else "flash_fwd", grid=(H, S // tq),
        in_specs=[pl.BlockSpec((tq, QW), lambda h, i: (i, h)), pl.BlockSpec((S, QW), lambda h, i: (0, h)),
                  pl.BlockSpec((S, HP), lambda h, i: (0, h))] + sd.in_specs,
        out_specs=[pl.BlockSpec((tq, HP), lambda h, i: (i, h)), pl.BlockSpec((1, 1, tq), lambda h, i: (h, 0, i))] + sd.out_specs,
        out_shape=[jax.ShapeDtypeStruct((S, H * HP), F32), jax.ShapeDtypeStruct((H, 1, S), F32)] + sd.out_shape,
        scratch_shapes=[pltpu.VMEM((1, tq), F32), pltpu.VMEM((1, tq), F32), pltpu.VMEM((HP, tq), F32),
                        pltpu.VMEM((tk, tq), F32), pltpu.VMEM((tk, tq), F32)] + sd.scratch,
        compiler_params=_params(("arbitrary", "arbitrary") if sd.on else ("parallel", "parallel"), nb),
    )(_hbm(Q), _hbm(K), _hbm(V), *sd.operands)


def _flash_bwd(Q, K, V, dO, O, lse, side=None):
    S = Q.shape[0]
    H = MLA_HEADS
    tq, tk = min(ATT_TQ, S), min(ATT_TK, S)
    nk = S // tk
    sd = _Side(side)

    def body(*refs):
        if sd.on:
            q_ref, do_ref, o_ref, lse_ref, k_ref, v_ref, p_ref, dq_ref, dk_ref, dv_ref, x_ref, dq_s, *sems = refs
            step_id = pl.program_id(0) * (S // tq) + pl.program_id(1)
            sd.start(p_ref, x_ref, sems, step_id == 0)
        else:
            q_ref, do_ref, o_ref, lse_ref, k_ref, v_ref, dq_ref, dk_ref, dv_ref, dq_s = refs

        @pl.when(pl.program_id(1) == 0)
        def _():
            dk_ref[...] = jnp.zeros(dk_ref.shape, F32)
            dv_ref[...] = jnp.zeros(dv_ref.shape, F32)

        delta = jnp.sum((do_ref[...].astype(F32) * o_ref[...]).T, axis=0, keepdims=True)
        lse = lse_ref[0]
        dq_s[...] = jnp.zeros(dq_s.shape, F32)

        def rows(j):
            return pl.ds(pl.multiple_of(j * tk, tk), tk)

        def scores(j):
            st = lax.dot_general(k_ref[rows(j), :], q_ref[...], NT_DIMS, preferred_element_type=F32)
            dpt = lax.dot_general(v_ref[rows(j), :], do_ref[...], NT_DIMS, preferred_element_type=F32)
            return st, dpt

        def consume(st, dpt, j):
            pt = jnp.exp2(st - lse)
            dst = (pt * (dpt - delta)).astype(BF16)
            dv_ref[rows(j), :] += jnp.dot(pt.astype(BF16), do_ref[...], preferred_element_type=F32)
            dk_ref[rows(j), :] += jnp.dot(dst, q_ref[...], preferred_element_type=F32)
            dq_s[...] += lax.dot_general(dst, k_ref[rows(j), :], TN_DIMS, preferred_element_type=F32)

        def step(j, carry):
            consume(*scores(j), j)
            return carry

        lax.fori_loop(0, nk, step, 0)
        dq_ref[...] = dq_s[...]
        if sd.on:
            sd.finish(p_ref, x_ref, sems, step_id == H * (S // tq) - 1)

    nb = (_nbytes((tq, QW), BF16) + _nbytes((tq, HP), BF16) + _nbytes((tq, HP), F32) + _nbytes((S, QW), BF16)
          + _nbytes((S, HP), BF16) + 2 * _nbytes((tq, QW), F32) + _nbytes((S, QW), F32) + _nbytes((S, HP), F32))
    return pl.pallas_call(
        body, name="flash_bwd_side" if sd.on else "flash_bwd", grid=(H, S // tq),
        in_specs=[pl.BlockSpec((tq, QW), lambda h, i: (i, h)), pl.BlockSpec((tq, HP), lambda h, i: (i, h)),
                  pl.BlockSpec((tq, HP), lambda h, i: (i, h)), pl.BlockSpec((1, 1, tq), lambda h, i: (h, 0, i)),
                  pl.BlockSpec((S, QW), lambda h, i: (0, h), pipeline_mode=pl.Buffered(1)),
                  pl.BlockSpec((S, HP), lambda h, i: (0, h), pipeline_mode=pl.Buffered(1))] + sd.in_specs,
        out_specs=[pl.BlockSpec((tq, QW), lambda h, i: (i, h)), pl.BlockSpec((S, QW), lambda h, i: (0, h)),
                   pl.BlockSpec((S, HP), lambda h, i: (0, h))] + sd.out_specs,
        out_shape=[jax.ShapeDtypeStruct((S, H * QW), F32), jax.ShapeDtypeStruct((S, H * QW), F32),
                   jax.ShapeDtypeStruct((S, H * HP), F32)] + sd.out_shape,
        scratch_shapes=[pltpu.VMEM((tq, QW), F32)] + sd.scratch,
        compiler_params=_params(("arbitrary", "arbitrary") if sd.on else ("parallel", "arbitrary"), nb),
    )(_hbm(Q), _hbm(dO), _hbm(O), lse, _hbm(K), _hbm(V), *sd.operands)


RET_CHUNK = 256


def _ret_tables(d, lg_ref, h, C):
    ii = lax.broadcasted_iota(jnp.int32, (C, C), 0).astype(F32)
    jj = lax.broadcasted_iota(jnp.int32, (C, C), 1).astype(F32)
    ci = lax.broadcasted_iota(jnp.int32, (C, 1), 0).astype(F32)
    fwd = d == 0
    dist = jnp.where(fwd, ii - jj, jj - ii)
    mask = dist >= jnp.where(fwd, 0.0, 1.0)
    dist = jnp.maximum(dist, 0.0)
    qpos = jnp.where(fwd, ci + 1.0, C - ci)
    kpos = jnp.where(fwd, C - 1.0 - ci, ci)
    lg = lg_ref[d, h]
    D = jnp.where(mask, jnp.exp(lg * dist), 0.0)
    return D, jnp.exp(lg * qpos), jnp.exp(lg * kpos), jnp.exp(lg * C), dist, qpos, kpos


def _ret_fwd(lg, q, k, v):
    S = q.shape[0]
    C = min(RET_CHUNK, S)
    N = S // C
    H = RET_HEADS

    def chunk(d, n):
        return jnp.where(d == 0, n, N - 1 - n)

    def body(lg_ref, q_ref, k_ref, v_ref, o_ref, st_ref, state, tab_d, tab_q, tab_k):
        d, n = pl.program_id(0), pl.program_id(1)

        @pl.when(n == 0)
        def _():
            state[...] = jnp.zeros_like(state)
            for h in range(H):
                D, qw, kw, _, _, _, _ = _ret_tables(d, lg_ref, h, C)
                tab_d[h] = D
                tab_q[h] = jnp.broadcast_to(qw, (C, HP))
                tab_k[h] = jnp.broadcast_to(kw, (C, HP))

        for h in range(H):
            sl = slice(HP * h, HP * (h + 1))
            D, qw, kw = tab_d[h], tab_q[h], tab_k[h]
            gc = jnp.exp(lg_ref[d, h] * C)
            qh, kh, vh = q_ref[:, sl], k_ref[:, sl], v_ref[:, sl]
            st = state[sl, :]
            sm = lax.dot_general(qh, kh, (((1,), (1,)), ((), ())), preferred_element_type=F32) * D
            inner = jnp.dot(sm.astype(BF16), vh, preferred_element_type=F32)
            cross = qw * jnp.dot(qh, st.astype(BF16), preferred_element_type=F32)
            o_ref[:, sl] = inner + cross
            st_ref[0, 0, sl, :] = st
            kvn = lax.dot_general((kh.astype(F32) * kw).astype(BF16), vh, (((0,), (0,)), ((), ())), preferred_element_type=F32)
            state[sl, :] = gc * st + kvn

    nb = 3 * _nbytes((C, 512), BF16) + _nbytes((C, 512), F32) + 2 * _nbytes((512, HP), F32) + 8 * _nbytes((C, C), F32)
    return pl.pallas_call(
        body, name="ret_fwd", grid=(2, N),
        in_specs=[pl.BlockSpec(memory_space=pltpu.SMEM)] + [pl.BlockSpec((C, 512), lambda d, n: (chunk(d, n), 0))] * 3,
        out_specs=[pl.BlockSpec((C, 512), lambda d, n: (chunk(d, n), d)),
                   pl.BlockSpec((1, 1, 512, HP), lambda d, n: (d, chunk(d, n), 0, 0))],
        out_shape=[jax.ShapeDtypeStruct((S, 1024), F32), jax.ShapeDtypeStruct((2, N, 512, HP), F32)],
        scratch_shapes=[pltpu.VMEM((512, HP), F32), pltpu.VMEM((H, C, C), F32), pltpu.VMEM((H, C, HP), F32),
                        pltpu.VMEM((H, C, HP), F32)],
        compiler_params=_params(("arbitrary", "arbitrary"), nb),
    )(lg, _hbm(q), _hbm(k), _hbm(v))


def _ret_bwd(lg, q, k, v, do, states):
    S = q.shape[0]
    C = min(RET_CHUNK, S)
    N = S // C
    H = RET_HEADS

    def chunk(d, n):
        return jnp.where(d == 0, N - 1 - n, n)

    def body(lg_ref, q_ref, k_ref, v_ref, do_ref, st_ref, dq_ref, dk_ref, dv_ref, dlg_ref, G, accA, accB, accC,
             tab_d, tab_q, tab_k, tab_qp, tab_kp):
        d, n = pl.program_id(0), pl.program_id(1)

        @pl.when(n == 0)
        def _():
            G[...] = jnp.zeros_like(G)
            accA[...] = jnp.zeros_like(accA)
            accB[...] = jnp.zeros_like(accB)
            accC[...] = jnp.zeros_like(accC)
            for h in range(H):
                D, qw, kw, _, _, qpos, kpos = _ret_tables(d, lg_ref, h, C)
                tab_d[h] = D
                tab_q[h] = jnp.broadcast_to(qw, (C, HP))
                tab_k[h] = jnp.broadcast_to(kw, (C, HP))
                tab_qp[h] = jnp.broadcast_to(qw * qpos, (C, HP))
                tab_kp[h] = jnp.broadcast_to(kw * kpos, (C, HP))

        nt = (((1,), (1,)), ((), ()))
        tn = (((0,), (0,)), ((), ()))
        for h in range(H):
            sl = slice(HP * h, HP * (h + 1))
            D, qw, kw = tab_d[h], tab_q[h], tab_k[h]
            gc = jnp.exp(lg_ref[d, h] * C)
            qh, kh, vh, doh = q_ref[:, sl], k_ref[:, sl], v_ref[:, sl], do_ref[:, sl]
            st = st_ref[0, 0, sl, :]
            g = G[sl, :]
            stb, gb = st.astype(BF16), g.astype(BF16)
            sraw = lax.dot_general(qh, kh, nt, preferred_element_type=F32)
            dS = lax.dot_general(doh, vh, nt, preferred_element_type=F32) * D
            dSb = dS.astype(BF16)
            smb = (sraw * D).astype(BF16)
            qs = jnp.dot(qh, stb, preferred_element_type=F32)
            kg = jnp.dot(kh, gb, preferred_element_type=F32)
            dqh = jnp.dot(dSb, kh, preferred_element_type=F32) + qw * lax.dot_general(doh, stb, nt, preferred_element_type=F32)
            dkh = lax.dot_general(dSb, qh, tn, preferred_element_type=F32) + kw * lax.dot_general(vh, gb, nt, preferred_element_type=F32)
            dvh = lax.dot_general(smb, doh, tn, preferred_element_type=F32) + kw * kg
            dq_ref[:, sl] = dqh.astype(dq_ref.dtype)
            dk_ref[:, sl] = dkh.astype(dk_ref.dtype)
            dv_ref[:, sl] = dvh.astype(dv_ref.dtype)
            dof, vf = doh.astype(F32), vh.astype(F32)
            accA[h] += sraw * dS
            accB[h] += tab_qp[h] * (qs * dof) + tab_kp[h] * (kg * vf)
            accC[h] += st * g
            G[sl, :] = gc * g + lax.dot_general((qh.astype(F32) * qw).astype(BF16), doh, tn, preferred_element_type=F32)

        @pl.when(n == N - 1)
        def _():
            rows = []
            for h in range(H):
                _, _, _, gc, dist, _, _ = _ret_tables(d, lg_ref, h, C)
                tot = jnp.sum(accA[h] * dist) + jnp.sum(accB[h]) + (C * gc) * jnp.sum(accC[h])
                rows.append(jnp.broadcast_to(tot, (1, HP)))
            dlg_ref[0] = jnp.concatenate(rows + [jnp.zeros((8 - H, HP), F32)], axis=0)

    nb = (4 * _nbytes((C, 512), BF16) + 3 * _nbytes((C, 512), F32) + 2 * _nbytes((512, HP), F32)
          + H * (_nbytes((C, C), F32) + _nbytes((C, HP), F32) + _nbytes((HP, HP), F32)) + 12 * _nbytes((C, C), F32))
    row = lambda d, n: (chunk(d, n), 0)
    out3 = lambda d, n: (chunk(d, n), d)
    return pl.pallas_call(
        body, name="ret_bwd", grid=(2, N),
        in_specs=[pl.BlockSpec(memory_space=pltpu.SMEM)] + [pl.BlockSpec((C, 512), row)] * 4
        + [pl.BlockSpec((1, 1, 512, HP), lambda d, n: (d, chunk(d, n), 0, 0))],
        out_specs=[pl.BlockSpec((C, 512), out3)] * 3 + [pl.BlockSpec((1, 8, HP), lambda d, n: (d, 0, 0))],
        out_shape=[jax.ShapeDtypeStruct((S, 1024), BF16)] * 3 + [jax.ShapeDtypeStruct((2, 8, HP), F32)],
        scratch_shapes=[pltpu.VMEM((512, HP), F32), pltpu.VMEM((H, C, C), F32), pltpu.VMEM((H, C, HP), F32),
                        pltpu.VMEM((H, HP, HP), F32), pltpu.VMEM((H, C, C), F32)] + [pltpu.VMEM((H, C, HP), F32)] * 4,
        compiler_params=_params(("arbitrary", "arbitrary"), nb),
    )(lg, _hbm(q), _hbm(k), _hbm(v), _hbm(do), _hbm(states))


CONV_PAD = 16
CONV_TR = 256
CONV_CB = LANES


def _fill_padded(pad, u_ref, S):
    pad[0:CONV_PAD, :] = jnp.zeros((CONV_PAD, CONV_CB), F32)
    pad[CONV_PAD + S:CONV_PAD + S + CONV_PAD, :] = jnp.zeros((CONV_PAD, CONV_CB), F32)
    pad[CONV_PAD:CONV_PAD + S, :] = u_ref[...]


def _conv_fwd(u, w32, b):
    S, D = u.shape
    tr = min(CONV_TR, S)

    def body(u_ref, w_ref, b_ref, o_ref, pad):
        _fill_padded(pad, u_ref, S)
        wv = w_ref[...]
        bv = b_ref[...]

        def step(t, carry):
            r0 = pl.multiple_of(t * tr, tr)
            acc = jnp.broadcast_to(bv, (tr, CONV_CB))
            for k in range(CONV_KERNEL):
                acc = acc + pad[pl.ds(r0 + (k + 1), tr), :] * wv[k:k + 1, :]
            o_ref[pl.ds(r0, tr), :] = acc
            return carry

        lax.fori_loop(0, S // tr, step, 0)

    nb = 2 * _nbytes((S, CONV_CB), F32) + _nbytes((S + 2 * CONV_PAD, CONV_CB), F32)
    return pl.pallas_call(
        body, name="conv_fwd", grid=(D // CONV_CB,),
        in_specs=[pl.BlockSpec((S, CONV_CB), lambda j: (0, j)), pl.BlockSpec((32, CONV_CB), lambda j: (0, j)),
                  pl.BlockSpec((1, CONV_CB), lambda j: (0, j))],
        out_specs=pl.BlockSpec((S, CONV_CB), lambda j: (0, j)),
        out_shape=jax.ShapeDtypeStruct((S, D), F32),
        scratch_shapes=[pltpu.VMEM((S + 2 * CONV_PAD, CONV_CB), F32)],
        compiler_params=_params(("parallel",), nb),
    )(_hbm(u), w32, b)


def _conv_bwd_w(u, dout):
    S, D = u.shape
    tr = min(CONV_TR, S)

    def body(u_ref, d_ref, o_ref, pad, acc):
        _fill_padded(pad, u_ref, S)
        acc[...] = jnp.zeros_like(acc)

        def fold(a):
            return jnp.sum(a.reshape(tr // 8, 8, CONV_CB), axis=0)

        def step(t, carry):
            r0 = pl.multiple_of(t * tr, tr)
            dv = d_ref[pl.ds(r0, tr), :]
            for k in range(CONV_KERNEL):
                acc[8 * k:8 * k + 8, :] += fold(pad[pl.ds(r0 + (k + 1), tr), :] * dv)
            acc[8 * CONV_KERNEL:8 * CONV_KERNEL + 8, :] += fold(dv)
            return carry

        lax.fori_loop(0, S // tr, step, 0)
        o_ref[...] = jnp.sum(acc[...].reshape(32, 8, CONV_CB), axis=1)

    nb = 2 * _nbytes((S, CONV_CB), F32) + _nbytes((S + 2 * CONV_PAD, CONV_CB), F32)
    return pl.pallas_call(
        body, name="conv_bwd_w", grid=(D // CONV_CB,),
        in_specs=[pl.BlockSpec((S, CONV_CB), lambda j: (0, j)), pl.BlockSpec((S, CONV_CB), lambda j: (0, j))],
        out_specs=pl.BlockSpec((32, CONV_CB), lambda j: (0, j)),
        out_shape=jax.ShapeDtypeStruct((32, D), F32),
        scratch_shapes=[pltpu.VMEM((S + 2 * CONV_PAD, CONV_CB), F32), pltpu.VMEM((256, CONV_CB), F32)],
        compiler_params=_params(("parallel",), nb),
    )(_hbm(u), _hbm(dout))


def _adamw(w, g, m, v):
    shape = w.shape
    w2, g2, m2, v2 = [a.reshape(-1, shape[-1]) for a in (w, g, m, v)]
    W = shape[-1]

    def fn(rows, p):
        w_, g_, m_, v_ = [r[...] for r in rows]
        mn = ADAM_B1 * m_ + (1.0 - ADAM_B1) * g_
        vn = ADAM_B2 * v_ + (1.0 - ADAM_B2) * jnp.square(g_)
        m_hat = mn / (1.0 - ADAM_B1 ** ADAM_STEP)
        v_hat = vn / (1.0 - ADAM_B2 ** ADAM_STEP)
        delta = -ADAM_LR * (m_hat / (jnp.sqrt(v_hat) + ADAM_EPS) + ADAM_WD * w_)
        return [delta, mn, vn], []

    R2 = w2.shape[0]
    ts = ROW_TILE if (R2 > ROW_TILE and R2 % ROW_TILE == 0) else R2
    d, mn, vn = _rowcall("adamw", fn, [w2, g2, m2, v2], [], [(W, F32)] * 3, ts=ts)
    return d.reshape(shape), mn.reshape(shape), vn.reshape(shape)


_HBM = pl.BlockSpec(memory_space=pltpu.HBM)


def _place():
    return lax.axis_index("x"), lax.axis_index("y"), lax.axis_index("c")


def _half_rows(ref, half, h):
    idx = (slice(None),) * (len(ref.shape) - 2) + (pl.ds(half * h, h), slice(None))
    return ref.at[idx]


def _chip_exchange(name, p, mode):
    assert mode in ("half", "whole", "scatter")
    if mode == "half":
        h, W = p.shape[0] // 2, p.shape[1]
    elif mode == "whole":
        h, W = p.shape
    else:
        h, W = p.shape[1], p.shape[2]

    def body(p_ref, o_ref, send_sems, recv_sems, local_sem):
        copies = _chip_copies(p_ref, o_ref, send_sems, recv_sems, local_sem, mode, h)
        for cp in copies:
            cp.start()
        for cp in copies:
            cp.wait()

    return pl.pallas_call(
        body, name=name, in_specs=[_HBM], out_specs=_HBM,
        out_shape=jax.ShapeDtypeStruct((4, h, W), p.dtype), scratch_shapes=_CHIP_SEMS,
    )(p)


_CHIP_SEMS = [pltpu.SemaphoreType.DMA((3,)), pltpu.SemaphoreType.DMA((3,)), pltpu.SemaphoreType.DMA]


def _chip_copies(p_ref, o_ref, send_sems, recv_sems, local_sem, mode, h):
    x, y, c = _place()
    k_me = 2 * x + y

    def src(k):
        if mode == "half":
            return _half_rows(p_ref, c, h)
        return p_ref if mode == "whole" else p_ref.at[k]

    copies = [pltpu.make_async_copy(src(k_me), o_ref.at[k_me], local_sem)]
    for j, (cx, cy) in enumerate([(1 - x, y), (x, 1 - y), (1 - x, 1 - y)]):
        copies.append(pltpu.make_async_remote_copy(
            src_ref=src(2 * cx + cy), dst_ref=o_ref.at[k_me], send_sem=send_sems.at[j], recv_sem=recv_sems.at[j],
            device_id=(cx, cy, c), device_id_type=MESH))
    return copies


def _sibling_send_half(name, g):
    n, R, W = g.shape
    h = R // 2

    def body(g_ref, o_ref, send_sem, recv_sem):
        x, y, c = _place()
        cp = pltpu.make_async_remote_copy(
            src_ref=_half_rows(g_ref, 1 - c, h), dst_ref=o_ref, send_sem=send_sem, recv_sem=recv_sem,
            device_id=(x, y, 1 - c), device_id_type=MESH)
        cp.start()
        cp.wait()

    return pl.pallas_call(
        body, name=name, in_specs=[_HBM], out_specs=_HBM,
        out_shape=jax.ShapeDtypeStruct((n, h, W), g.dtype),
        scratch_shapes=[pltpu.SemaphoreType.DMA, pltpu.SemaphoreType.DMA],
    )(g)


def _halves_merge(name, mine):
    def body(m_ref, o_ref, send_sem, recv_sem):
        x, y, c = _place()
        cp = pltpu.make_async_remote_copy(
            src_ref=m_ref, dst_ref=o_ref, send_sem=send_sem, recv_sem=recv_sem,
            device_id=(x, y, 1 - c), device_id_type=MESH)
        cp.start()
        cp.wait()

    theirs = pl.pallas_call(
        body, name=name, in_specs=[_HBM], out_specs=_HBM,
        out_shape=jax.ShapeDtypeStruct(mine.shape, mine.dtype),
        scratch_shapes=[pltpu.SemaphoreType.DMA, pltpu.SemaphoreType.DMA],
    )(mine)
    south = lax.axis_index("c") == 0
    axis = mine.ndim - 2
    return jnp.concatenate([jnp.where(south, mine, theirs), jnp.where(south, theirs, mine)], axis=axis)


def _add2(name, a, b, out_dtype):
    def fn(rows, p):
        return [rows[0][...].astype(F32) + rows[1][...].astype(F32)], []

    return _rowcall(name, fn, [a, b], [], [(a.shape[1], out_dtype)], ts=4 * PACK_TILE)[0]


def _sum4(name, b):
    _, h, W = b.shape
    ts = 2 * PACK_TILE if h % (2 * PACK_TILE) == 0 else (PACK_TILE if h % PACK_TILE == 0 else h)

    def body(b_ref, o_ref):
        o_ref[...] = ((b_ref[0].astype(F32) + b_ref[1].astype(F32)) + b_ref[2].astype(F32)) + b_ref[3].astype(F32)

    nb = _nbytes((4, ts, W), b.dtype) + _nbytes((ts, W), F32)
    return pl.pallas_call(
        body, name=name, grid=(h // ts,),
        in_specs=[pl.BlockSpec((4, ts, W), lambda i: (0, i, 0))], out_specs=pl.BlockSpec((ts, W), lambda i: (i, 0)),
        out_shape=jax.ShapeDtypeStruct((h, W), F32), compiler_params=_params(("parallel",), nb),
    )(b)


PACK_W = 1024
PACK_ROWS = 16
PACK_TILE = 128
EV_SHARDED = (("ev_w_in", "col", False), ("ev_w_uq", "col", False), ("ev_w_ukv", "col", False), ("ev_w_out", "row", False))
OD_SHARDED = (("od_w_in", "col", False), ("od_b_in", "col", True), ("od_dw_w", "col", True), ("od_dw_b", "col", True),
              ("od_ln_g", "col", True), ("od_ln_b", "col", True), ("od_w_out", "row", False))
REPLICATED =("ada_b", "pre_g", "post_g", "ev_dec_f", "ev_dec_b", "ev_q_norm_g", "ev_kv_norm_g")
WEIGHTS = ("ada_w", "ada_b", "pre_g", "post_g", "ev_w_in", "ev_dec_f", "ev_dec_b", "ev_q_norm_g", "ev_w_uq", "ev_kv_norm_g",
           "ev_w_ukv", "ev_w_out", "od_w_in", "od_b_in", "od_dw_w", "od_dw_b", "od_ln_g", "od_ln_b", "od_w_out")


def _layer_entries(layers):
    out = []
    for l in layers:
        out += [(name, how, exact, l // 2) for name, how, exact in (EV_SHARDED if l % 2 == 0 else OD_SHARDED)]
    return out


def _rows_of(n):
    unit = PACK_W * PACK_ROWS
    return (n + unit - 1) // unit * PACK_ROWS


def _to_rows(flat, lead):
    n = flat.shape[-1]
    r = _rows_of(n)
    flat = jnp.pad(flat, [(0, 0)] * len(lead) + [(0, r * PACK_W - n)])
    return flat.reshape(*lead, r, PACK_W)


def _pad_total(buf, axis):
    r = buf.shape[axis]
    r2 = (r + 2 * PACK_TILE - 1) // (2 * PACK_TILE) * (2 * PACK_TILE)
    pads = [(0, 0)] * buf.ndim
    pads[axis] = (0, r2 - r)
    return jnp.pad(buf, pads)


def _split_chips(full, how):
    if how == "col":
        n = full.shape[-1] // 4
        return jnp.moveaxis(full.reshape(*full.shape[:-1], 4, n), -2, 0)
    n = full.shape[-2] // 4
    return jnp.moveaxis(full.reshape(*full.shape[:-2], 4, n, full.shape[-1]), -3, 0)


def _join_chips(blocks, how):
    if how == "col":
        t = jnp.moveaxis(blocks, 0, -2)
        return t.reshape(*t.shape[:-2], t.shape[-2] * t.shape[-1])
    t = jnp.moveaxis(blocks, 0, -3)
    return t.reshape(*t.shape[:-3], t.shape[-3] * t.shape[-2], t.shape[-1])


def _pack_weights(w, entries):
    parts = []
    for name, _, exact, idx in entries:
        a = w[name][idx].reshape(-1)
        a = lax.bitcast_convert_type(a, BF16).reshape(-1) if exact else a.astype(BF16)
        parts.append(_to_rows(a, ()))
    return _pad_total(jnp.concatenate(parts, axis=0), 0)


def _unpack_weights(gathered, w, entries):
    out, r0 = {}, 0
    for name, how, exact, idx in entries:
        shp = w[name].shape[1:]
        n = (2 if exact else 1) * w[name][idx].size
        r = _rows_of(n)
        a = gathered[:, r0:r0 + r].reshape(4, -1)[:, :n]
        if exact:
            a = lax.bitcast_convert_type(a.reshape(4, -1, 2), F32)
        out[(name, idx)] = _join_chips(a.reshape(4, *shp), how)
        r0 += r
    return out


def _pack_grads(grads, entries, replicated):
    parts = []
    for name, how, _, idx in entries:
        b = _split_chips(grads[(name, idx)], how)
        parts.append(_to_rows(b.reshape(4, -1).astype(BF16), (4,)))
    for name in replicated:
        a = _to_rows(grads[name].reshape(-1).astype(BF16), ())
        parts.append(jnp.broadcast_to(a[None], (4, *a.shape)))
    return _pad_total(jnp.concatenate(parts, axis=1), 1)


def _reduce_begin(tag, g):
    h = g.shape[1] // 2
    c = lax.axis_index("c")
    theirs = _sibling_send_half("reduce_cores_" + tag, g)
    mine = lax.dynamic_slice_in_dim(g, c * h, h, axis=1)
    return _add2("reduce_add_" + tag, mine.reshape(4 * h, PACK_W), theirs.reshape(4 * h, PACK_W), BF16).reshape(4, h, PACK_W)


def _reduce_end(tag, exchanged):
    return _halves_merge("reduce_merge_" + tag, _sum4("reduce_sum_" + tag, exchanged))


def _unpack_grads(total, w, entries, replicated):
    out, r0 = {}, 0
    for name, _, _, idx in entries:
        n = w[name][idx].size
        r = _rows_of(n)
        out[(name, idx)] = total[r0:r0 + r].reshape(-1)[:n].reshape(w[name].shape[1:])
        r0 += r
    for name in replicated:
        n = w[name].size
        r = _rows_of(n)
        out[name] = total[r0:r0 + r].reshape(-1)[:n].reshape(w[name].shape)
        r0 += r
    return out


def _pad_heads(a, n, w, to):
    lead = a.shape[:-1]
    return jnp.pad(a.reshape(*lead, n, w), [(0, 0)] * (len(lead) + 1) + [(0, to - w)]).reshape(*lead, n * to)


def _unpad_heads(a, n, w, to):
    lead = a.shape[:-1]
    return a.reshape(*lead, n, to)[..., :w].reshape(*lead, n * w)


def _w_in_pad(w):
    return jnp.concatenate([_pad_heads(w[:, 0:256], 4, 64, HP), _pad_heads(w[:, 256:512], 4, 64, HP), w[:, 512:1536],
                            w[:, 2240:2752], w[:, 1536:2176], _pad_heads(w[:, 2176:2240], 1, 64, HP)], axis=1)


def _w_in_unpad(g):
    return jnp.concatenate([_unpad_heads(g[:, Z_RQ:Z_RK], 4, 64, HP), _unpad_heads(g[:, Z_RK:Z_RV], 4, 64, HP),
                            g[:, Z_RV:Z_MG], g[:, Z_CQ:Z_KR], g[:, Z_KR:Z_KR + 64], g[:, Z_MG:Z_CQ]], axis=1)


def _layer_weights(l, full, w):
    f32 = lambda a: a.astype(F32)
    i = l // 2
    common = dict(bias=f32(w["ada_b"][l])[None], pre_g=f32(w["pre_g"][l])[None], post_g=f32(w["post_g"][l])[None])
    if l % 2 == 0:
        w_uq_p = _pad_heads(full[("ev_w_uq", i)], MLA_HEADS, MLA_NOPE_DIM + MLA_ROPE_DIM, QW)
        lg = jnp.stack([jax.nn.log_sigmoid(f32(w["ev_dec_f"][i])), jax.nn.log_sigmoid(f32(w["ev_dec_b"][i]))])
        return dict(common, w_in=_w_in_pad(full[("ev_w_in", i)]), w_uq=w_uq_p, w_ukv=full[("ev_w_ukv", i)],
                    w_out=full[("ev_w_out", i)], qg=f32(w["ev_q_norm_g"][i])[None], kvg=f32(w["ev_kv_norm_g"][i])[None], lg=lg)
    dw = f32(full[("od_dw_w", i)])
    return dict(common, w_in=full[("od_w_in", i)], w_out=full[("od_w_out", i)], b_in=f32(full[("od_b_in", i)])[None],
                dw=jnp.pad(dw, ((0, 1), (0, 0))), dw_flip=jnp.pad(dw[::-1], ((0, 1), (0, 0))),
                dw_b=f32(full[("od_dw_b", i)])[None], ln_g=f32(full[("od_ln_g", i)])[None], ln_b=f32(full[("od_ln_b", i)])[None])


def _rope_tables(positions):
    inv_freq = ROPE_BASE ** (-jnp.arange(0, ROPE_DIM, 2, dtype=F32) / ROPE_DIM)
    ang = positions.astype(F32)[:, None] * inv_freq
    z = jnp.zeros((ang.shape[0], HP - ROPE_DIM), F32)
    return jnp.concatenate([jnp.cos(ang), jnp.cos(ang), z], axis=1), jnp.concatenate([jnp.sin(ang), jnp.sin(ang), z], axis=1)


def _even_fwd(x, mod, cos, sin, w, side=None):
    bias, pre_g, post_g = w["bias"], w["pre_g"], w["post_g"]
    h, z = _pre_mm("ev_in", x, pre_g, mod, bias, w["w_in"])
    rq, rk, rv, qn, kvn, Q, K, V = _ev_mid_fwd(z, cos, sin, w["qg"], w["kvg"], w["w_uq"], w["w_ukv"])
    a, lse, *got = _flash_fwd(Q, K, V, side)
    o2, states = _ret_fwd(w["lg"], rq, rk, rv)
    mix, y, x_new = _ev_out_fwd(o2, z, a, x, w["w_out"], post_g, mod, bias)
    saved = dict(x=x, h=h, z=z, rq=rq, rk=rk, rv=rv, qn=qn, kvn=kvn, Q=Q, K=K, V=V, a=a, lse=lse, o2=o2,
                 states=states, mix=mix, y=y)
    return x_new, saved, (got[0] if got else None)


def _even_bwd(dx, s, mod, cos, sin, w, side=None):
    bias, pre_g, post_g = w["bias"], w["pre_g"], w["post_g"]
    dy, do, drg, da, dmg, dpost_g, dgate = _ev_out_bwd(s["y"], dx, s["o2"], s["z"], s["a"], w["w_out"], post_g, mod, bias)
    dw_out = _mm_tn("ev_out_dw", s["mix"], dy, BF16)
    drq, drk, drv, dlg = _ret_bwd(w["lg"], s["rq"], s["rk"], s["rv"], do, s["states"])
    dQ, dK, dV, *got = _flash_bwd(s["Q"], s["K"], s["V"], da, s["a"], s["lse"], side)
    dqf, dkv, dqn, dkvn, dkr = _ev_mid_bwd(dQ, dK, dV, cos, sin, w["w_uq"], w["w_ukv"])
    dw_uq = _mm_tn("ev_uq_dw", s["qn"], dqf, BF16)
    dw_ukv = _mm_tn("ev_ukv_dw", s["kvn"], dkv, BF16)
    dz, dx_new, dqg, dkvg, dpre_g, dscale, dshift = _ev_in_bwd(
        s["z"], cos, sin, w["qg"], w["kvg"], drq, drk, drv, drg, dqn, dkvn, dkr, dmg, s["x"], dx, w["w_in"], pre_g, mod, bias)
    dw_in = _mm_tn("ev_in_dw", s["h"], dz, BF16)
    g = dict(ev_w_in=_w_in_unpad(dw_in), ev_w_uq=_unpad_heads(dw_uq, MLA_HEADS, MLA_NOPE_DIM + MLA_ROPE_DIM, QW),
             ev_w_ukv=dw_ukv, ev_w_out=dw_out, ev_q_norm_g=dqg[0], ev_kv_norm_g=dkvg[0],
             dlg_f=dlg[0, :RET_HEADS, 0], dlg_b=dlg[1, :RET_HEADS, 0])
    return dx_new, g, dpre_g[0], dpost_g[0], jnp.concatenate([dshift, dscale, dgate], axis=1), (got[0] if got else None)


def _odd_fwd(x, mod, w, tgt=None):
    bias, pre_g, post_g = w["bias"], w["pre_g"], w["post_g"]
    h, z, u = _od_in_fwd(x, pre_g, mod, bias, w["w_in"], w["b_in"])
    uc = _conv_fwd(u, w["dw"], w["dw_b"])
    vv, y, *rest = _od_out_fwd(uc, z, x, w["w_out"], w["b_in"], w["ln_g"], w["ln_b"], post_g, mod, bias, tgt)
    return (rest[0] if tgt is None else tuple(rest)), dict(x=x, h=h, z=z, u=u, uc=uc, vv=vv, y=y)


def _odd_bwd(dx, s, mod, w):
    bias, pre_g, post_g = w["bias"], w["pre_g"], w["post_g"]
    dy, duc, dg, dpost_g, dgate, dbg, dln_g, dln_b = _od_out_bwd(
        s["y"], dx, s["uc"], s["z"], w["w_out"], w["b_in"], w["ln_g"], w["ln_b"], post_g, mod, bias)
    dw_out = _mm_tn("od_out_dw", s["vv"], dy, BF16)
    du = _conv_fwd(duc, w["dw_flip"], jnp.zeros_like(w["dw_b"]))
    dwb = _conv_bwd_w(s["u"], duc)
    dz, dx_new, dba, dbb, dpre_g, dscale, dshift = _od_in_bwd(s["z"], w["b_in"], du, dg, s["x"], dx, w["w_in"], pre_g, mod, bias)
    dw_in = _mm_tn("od_in_dw", s["h"], dz, BF16)
    g = dict(od_w_in=dw_in, od_b_in=jnp.concatenate([dba, dbb, dbg], axis=1)[0], od_dw_w=dwb[:CONV_KERNEL], od_dw_b=dwb[CONV_KERNEL],
             od_ln_g=dln_g[0], od_ln_b=dln_b[0], od_w_out=dw_out)
    return dx_new, g, dpre_g[0], dpost_g[0], jnp.concatenate([dshift, dscale, dgate], axis=1)


LATE_LAYERS = tuple(range(1, DEPTH))


def _step(x, mod, positions, loss_target, w):
    cos, sin = _rope_tables(positions)
    f32 = lambda a: a.astype(F32)
    mods = [mod[l:l + 1] for l in range(DEPTH)]
    first, late = _layer_entries([0]), _layer_entries(LATE_LAYERS)

    gathered = _halves_merge("gather_merge_first", _chip_exchange("gather_chips_first", _pack_weights(w, first), "half"))
    full = _unpack_weights(gathered, w, first)
    lw = {0: _layer_weights(0, full, w)}
    x, s0, got = _even_fwd(x, mods[0], cos, sin, lw[0], side=(_pack_weights(w, late), "half"))
    full.update(_unpack_weights(_halves_merge("gather_merge_late", got), w, late))
    saved = [s0]
    assert DEPTH % 2 == 0
    for l in LATE_LAYERS:
        lw[l] = _layer_weights(l, full, w)
        if l % 2 == 0:
            x, s, _ = _even_fwd(x, mods[l], cos, sin, lw[l])
        else:
            x, s = _odd_fwd(x, mods[l], lw[l], loss_target if l == DEPTH - 1 else None)
        saved.append(s)
    dx, loss = x

    grads = {}
    dpre, dpost, dmod = [None] * DEPTH, [None] * DEPTH, [None] * DEPTH
    for l in reversed(LATE_LAYERS):
        if l % 2 == 0:
            dx, g, dpre[l], dpost[l], dmod[l], _ = _even_bwd(dx, saved[l], mods[l], cos, sin, lw[l])
        else:
            dx, g, dpre[l], dpost[l], dmod[l] = _odd_bwd(dx, saved[l], mods[l], lw[l])
        grads.update({(name, l // 2): val for name, val in g.items()})
    chip_sum = _reduce_begin("late", _pack_grads(grads, late, ()))
    dx, g, dpre[0], dpost[0], dmod[0], got = _even_bwd(dx, saved[0], mods[0], cos, sin, lw[0], side=(chip_sum, "scatter"))
    grads.update({(name, 0): val for name, val in g.items()})
    red = _unpack_grads(_reduce_end("late", got), w, late, ())

    n_ev = (DEPTH + 1) // 2
    grads.update(pre_g=jnp.stack(dpre), post_g=jnp.stack(dpost), ada_b=jnp.concatenate(dmod, axis=0))
    for name in ("ev_q_norm_g", "ev_kv_norm_g"):
        grads[name] = jnp.stack([grads[(name, i)] for i in range(n_ev)])
    grads["ev_dec_f"] = jnp.stack([grads[("dlg_f", i)] for i in range(n_ev)]) * jax.nn.sigmoid(-f32(w["ev_dec_f"]))
    grads["ev_dec_b"] = jnp.stack([grads[("dlg_b", i)] for i in range(n_ev)]) * jax.nn.sigmoid(-f32(w["ev_dec_b"]))
    chip_sum = _reduce_begin("first", _pack_grads(grads, first, REPLICATED))
    total = _reduce_end("first", _chip_exchange("reduce_chips_first", chip_sum, "scatter"))
    red.update(_unpack_grads(total, w, first, REPLICATED))

    out = {name: red[name] for name in REPLICATED}
    for name, _, _ in EV_SHARDED:
        out[name] = jnp.stack([red[(name, i)] for i in range(n_ev)])
    for name, _, _ in OD_SHARDED:
        out[name] = jnp.stack([red[(name, i)] for i in range(DEPTH // 2)])
    return loss[0, 0], dx, out, grads["ada_b"]


ADA_ROWS = 16
ADA_COLS = 3 * D_MODEL // 4
ADA_PACK = (8, 1024)


def _ada_pack(a):
    n = ADA_PACK[0] * ADA_PACK[1]
    return jnp.pad(a, ((0, 0), (0, n - a.shape[1]))).reshape(4, *ADA_PACK)


def _ada_unpack(a):
    return a.reshape(4, -1)[:, :DEPTH * ADA_COLS]


def _ada_forward(c, ada_w):
    D = D_MODEL
    c16 = jnp.zeros((ADA_ROWS, D), F32).at[0].set(c[0])
    c_act = _rowcall("silu_c", lambda rows, p: ([_silu(rows[0][...])], []), [c16], [], [(D, BF16)])[0]
    got = _chip_exchange("ada_c", c_act, "whole")
    c4 = jnp.pad(got[:, 0, :], ((0, ADA_ROWS - 4), (0, 0)))
    per_layer = [_mm_nn("ada", c4, ada_w[l].astype(BF16))[:4] for l in range(DEPTH)]
    p = jnp.stack(per_layer, axis=1).reshape(4, DEPTH * ADA_COLS)
    got = _chip_exchange("ada_mod", _ada_pack(p), "scatter")
    mod = _ada_unpack(got).reshape(4, DEPTH, ADA_COLS).transpose(1, 0, 2).reshape(DEPTH, 3 * D)
    return mod, c4


def _ada_backward(dmod, c4):
    p = dmod.reshape(DEPTH, 4, ADA_COLS).transpose(1, 0, 2).reshape(4, DEPTH * ADA_COLS)
    got = _chip_exchange("ada_dmod", _ada_pack(p), "scatter")
    dm4 = jnp.pad(_ada_unpack(got), ((0, ADA_ROWS - 4), (0, 0)))
    both = _halves_merge("ada_merge", jnp.concatenate([dm4, c4.astype(F32)], axis=1))
    dm32, c32 = both[:, :DEPTH * ADA_COLS].astype(BF16), both[:, DEPTH * ADA_COLS:].astype(BF16)
    return jnp.stack([_mm_tn("ada_dw", c32, dm32[:, l * ADA_COLS:(l + 1) * ADA_COLS]) for l in range(DEPTH)])


def kernel(x, c, positions, ada_w, ada_b, pre_g, post_g, ev_w_in, ev_dec_f, ev_dec_b, ev_q_norm_g, ev_w_uq, ev_kv_norm_g, ev_w_ukv, ev_w_out, od_w_in, od_b_in, od_dw_w, od_dw_b, od_ln_g, od_ln_b, od_w_out, loss_target, m_ada_w, m_ada_b, m_pre_g, m_post_g, m_ev_w_in, m_ev_dec_f, m_ev_dec_b, m_ev_q_norm_g, m_ev_w_uq, m_ev_kv_norm_g, m_ev_w_ukv, m_ev_w_out, m_od_w_in, m_od_b_in, m_od_dw_w, m_od_dw_b, m_od_ln_g, m_od_ln_b, m_od_w_out, v_ada_w, v_ada_b, v_pre_g, v_post_g, v_ev_w_in, v_ev_dec_f, v_ev_dec_b, v_ev_q_norm_g, v_ev_w_uq, v_ev_kv_norm_g, v_ev_w_ukv, v_ev_w_out, v_od_w_in, v_od_b_in, v_od_dw_w, v_od_dw_b, v_od_ln_g, v_od_ln_b, v_od_w_out):
    w = dict(ada_w=ada_w, ada_b=ada_b, pre_g=pre_g, post_g=post_g, ev_w_in=ev_w_in, ev_dec_f=ev_dec_f, ev_dec_b=ev_dec_b,
             ev_q_norm_g=ev_q_norm_g, ev_w_uq=ev_w_uq, ev_kv_norm_g=ev_kv_norm_g, ev_w_ukv=ev_w_ukv, ev_w_out=ev_w_out,
             od_w_in=od_w_in, od_b_in=od_b_in, od_dw_w=od_dw_w, od_dw_b=od_dw_b, od_ln_g=od_ln_g, od_ln_b=od_ln_b, od_w_out=od_w_out)
    m = dict(ada_w=m_ada_w, ada_b=m_ada_b, pre_g=m_pre_g, post_g=m_post_g, ev_w_in=m_ev_w_in, ev_dec_f=m_ev_dec_f, ev_dec_b=m_ev_dec_b,
             ev_q_norm_g=m_ev_q_norm_g, ev_w_uq=m_ev_w_uq, ev_kv_norm_g=m_ev_kv_norm_g, ev_w_ukv=m_ev_w_ukv, ev_w_out=m_ev_w_out,
             od_w_in=m_od_w_in, od_b_in=m_od_b_in, od_dw_w=m_od_dw_w, od_dw_b=m_od_dw_b, od_ln_g=m_od_ln_g, od_ln_b=m_od_ln_b, od_w_out=m_od_w_out)
    v = dict(ada_w=v_ada_w, ada_b=v_ada_b, pre_g=v_pre_g, post_g=v_post_g, ev_w_in=v_ev_w_in, ev_dec_f=v_ev_dec_f, ev_dec_b=v_ev_dec_b,
             ev_q_norm_g=v_ev_q_norm_g, ev_w_uq=v_ev_w_uq, ev_kv_norm_g=v_ev_kv_norm_g, ev_w_ukv=v_ev_w_ukv, ev_w_out=v_ev_w_out,
             od_w_in=v_od_w_in, od_b_in=v_od_b_in, od_dw_w=v_od_dw_w, od_dw_b=v_od_dw_b, od_ln_g=v_od_ln_g, od_ln_b=v_od_ln_b, od_w_out=v_od_w_out)

    mod, c4 = _ada_forward(c, ada_w)
    loss_local, grad_x, g, dmod = _step(x[0], mod, positions[0], loss_target[0], w)
    loss = lax.psum(loss_local, ("x", "y", "c"))
    g["ada_w"] = _ada_backward(dmod, c4)
    delta, new_m, new_v = {}, {}, {}
    for name in WEIGHTS:
        delta[name], new_m[name], new_v[name] = _adamw(w[name], g[name], m[name], v[name])
    return (loss, grad_x[None], *[g[n] for n in WEIGHTS], *[delta[n] for n in WEIGHTS],
            *[new_m[n] for n in WEIGHTS], *[new_v[n] for n in WEIGHTS])
```
